```python
import math
import jax, jax.numpy as jnp
from jax import lax
import numpy as np

D_MODEL = 1024
BATCH = 8
SEQ = 16384
DEPTH = 4

CHUNK = 64
Q_BLOCK = 128
N_MIXERS = 3
N_A = (DEPTH + 2) // 3
N_B = (DEPTH + 1) // 3
N_C = DEPTH // 3
DEEPNORM_ALPHA = (2.0 * DEPTH) ** 0.25
DEEPNORM_BETA = (8.0 * DEPTH) ** -0.25
LN_EPS = 1e-5
RMS_EPS = 1e-6
ROPE_BASE = 10000.0

LRU_WIDTH = D_MODEL
LRU_BLOCKS = 4
LRU_BLOCK_W = LRU_WIDTH // LRU_BLOCKS
LRU_CONV_W = 4
LRU_C = 8.0

MLA_HEADS = 16
QK_NOPE = 64
QK_ROPE = 32
V_HEAD = 64
KV_LORA = 256
Q_LORA = 768

RET_HEADS = 4
RET_QK = D_MODEL // RET_HEADS
RET_V = 2 * D_MODEL // RET_HEADS

D_FF = 2816
FFN_CONV_W = 3

kernel_name = "hybrid_rglru_mla_retention_deepnorm"


def layer_norm(x, g, b):
    xf = x.astype(jnp.float32)
    mu = jnp.mean(xf, axis=-1, keepdims=True)
    var = jnp.mean(jnp.square(xf - mu), axis=-1, keepdims=True)
    return (xf - mu) * lax.rsqrt(var + LN_EPS) * g + b


def rms_norm(x, g):
    xf = x.astype(jnp.float32)
    return (xf * lax.rsqrt(jnp.mean(xf * xf, axis=-1, keepdims=True) + RMS_EPS) * g).astype(x.dtype)


def rope(x, positions):
    d = x.shape[-1]
    inv_freq = ROPE_BASE ** (-jnp.arange(0, d, 2, dtype=jnp.float32) / d)
    ang = positions.astype(jnp.float32)[:, :, None, None] * inv_freq
    cos, sin = jnp.cos(ang), jnp.sin(ang)
    xf = x.astype(jnp.float32)
    x1, x2 = xf[..., : d // 2], xf[..., d // 2:]
    return jnp.concatenate([x1 * cos - x2 * sin, x1 * sin + x2 * cos], axis=-1)


def causal_dwconv(x, w, b):
    k_w, c = w.shape
    y = lax.conv_general_dilated(
        x, w.astype(x.dtype)[:, None, :], window_strides=(1,), padding=[(k_w - 1, 0)],
        dimension_numbers=("NWC", "WIO", "NWC"), feature_group_count=c)
    return y + b.astype(x.dtype)


def rglru_mixer(x, w_in, conv_w, conv_b, w_a, b_a, w_x, b_x, lam, w_out):
    bsz, s, _ = x.shape
    proj = x @ w_in
    gate_branch, rnn_in = jnp.split(proj, 2, axis=-1)
    u = causal_dwconv(rnn_in, conv_w, conv_b)
    ub = u.reshape(bsz, s, LRU_BLOCKS, LRU_BLOCK_W)
    r = jax.nn.sigmoid(jnp.einsum("bsgi,gij->bsgj", ub, w_a) + b_a).reshape(bsz, s, LRU_WIDTH)
    i = jax.nn.sigmoid(jnp.einsum("bsgi,gij->bsgj", ub, w_x) + b_x).reshape(bsz, s, LRU_WIDTH)
    log_a = -LRU_C * r.astype(jnp.float32) * jax.nn.softplus(-lam.astype(jnp.float32))
    a = jnp.exp(log_a)
    mult = jnp.sqrt(-jnp.expm1(2.0 * log_a))
    bvals = mult * (i * u).astype(jnp.float32)

    def combine(c1, c2):
        a1, b1 = c1
        a2, b2 = c2
        return a1 * a2, a2 * b1 + b2

    _, h = lax.associative_scan(combine, (a, bvals), axis=1)
    y = jax.nn.gelu(gate_branch) * h.astype(x.dtype)
    return y @ w_out


def chunk_causal_attention(q_nope, q_pe, k_nope, k_pe, v):
    s_len = q_nope.shape[1]
    scale = (QK_NOPE + QK_ROPE) ** -0.5
    outs = []
    for j in range(s_len // Q_BLOCK):
        q0 = j * Q_BLOCK
        kv_len = q0 + Q_BLOCK
        sc = (jnp.einsum("bqhd,bkhd->bhqk", q_nope[:, q0:kv_len], k_nope[:, :kv_len])
              + jnp.einsum("bqhd,bkd->bhqk", q_pe[:, q0:kv_len], k_pe[:, :kv_len]))
        sc = sc.astype(jnp.float32) * scale
        q_chunk = (q0 + jnp.arange(Q_BLOCK)) // CHUNK
        k_chunk = jnp.arange(kv_len) // CHUNK
        mask = k_chunk[None, :] <= q_chunk[:, None]
        sc = jnp.where(mask, sc, -jnp.inf)
        p = jax.nn.softmax(sc, axis=-1).astype(v.dtype)
        outs.append(jnp.einsum("bhqk,bkhd->bqhd", p, v[:, :kv_len]))
    return jnp.concatenate(outs, axis=1)


def mla_mixer(x, positions, w_in, q_norm_g, kv_norm_g, w_uq, w_ukv, w_out):
    bsz, s, _ = x.shape
    proj = x @ w_in
    c_q = proj[..., :Q_LORA]
    c_kv = proj[..., Q_LORA:Q_LORA + KV_LORA]
    k_pe = proj[..., Q_LORA + KV_LORA:]
    q = (rms_norm(c_q, q_norm_g) @ w_uq).reshape(bsz, s, MLA_HEADS, QK_NOPE + QK_ROPE)
    q_nope, q_pe = q[..., :QK_NOPE], q[..., QK_NOPE:]
    q_pe = rope(q_pe, positions).astype(x.dtype)
    k_pe = rope(k_pe[:, :, None, :], positions)[:, :, 0, :].astype(x.dtype)
    kv = (rms_norm(c_kv, kv_norm_g) @ w_ukv).reshape(bsz, s, MLA_HEADS, QK_NOPE + V_HEAD)
    k_nope, v = kv[..., :QK_NOPE], kv[..., QK_NOPE:]
    o = chunk_causal_attention(q_nope, q_pe, k_nope, k_pe, v)
    return o.reshape(bsz, s, MLA_HEADS * V_HEAD) @ w_out


def retention_mixer(x, positions, w_in, gn_g, w_out):
    bsz, s, _ = x.shape
    n_c = s // CHUNK
    hq, hv = RET_HEADS * RET_QK, RET_HEADS * RET_V
    proj = x @ w_in
    q = rope(proj[..., :hq].reshape(bsz, s, RET_HEADS, RET_QK), positions)
    k = rope(proj[..., hq:2 * hq].reshape(bsz, s, RET_HEADS, RET_QK), positions) * (RET_QK ** -0.5)
    v = proj[..., 2 * hq:2 * hq + hv].reshape(bsz, s, RET_HEADS, RET_V).astype(jnp.float32)
    g = proj[..., 2 * hq + hv:]
    qc = q.reshape(bsz, n_c, CHUNK, RET_HEADS, RET_QK)
    kc = k.reshape(bsz, n_c, CHUNK, RET_HEADS, RET_QK)
    vc = v.reshape(bsz, n_c, CHUNK, RET_HEADS, RET_V)
    log_gamma = jnp.log1p(-jnp.exp2(-5.0 - jnp.arange(RET_HEADS, dtype=jnp.float32)))
    idx = jnp.arange(CHUNK, dtype=jnp.float32)
    rel = jnp.abs(idx[:, None] - idx[None, :])
    d_intra = jnp.exp(log_gamma[:, None, None] * rel)
    sc = jnp.einsum("bnqhd,bnkhd->bnhqk", qc, kc) * d_intra
    y_intra = jnp.einsum("bnhqk,bnkhe->bnqhe", sc, vc)
    xi = jnp.exp(log_gamma[None, :] * (idx + 1.0)[:, None])
    zeta = jnp.exp(log_gamma[None, :] * (CHUNK - 1.0 - idx)[:, None])
    chunk_decay = jnp.exp(log_gamma * CHUNK)

    def step(state, inp):
        qn, kn, vn = inp
        y = jnp.einsum("bqhd,bhde->bqhe", qn, state) * xi[None, :, :, None]
        state = state * chunk_decay[None, :, None, None] + jnp.einsum(
            "bkhd,bkhe->bhde", kn * zeta[None, :, :, None], vn)
        return state, y

    state0 = jnp.zeros((bsz, RET_HEADS, RET_QK, RET_V), jnp.float32)
    _, y_inter = lax.scan(step, state0, (jnp.swapaxes(qc, 0, 1), jnp.swapaxes(kc, 0, 1),
                                         jnp.swapaxes(vc, 0, 1)))
    y = (y_intra + jnp.swapaxes(y_inter, 0, 1)).reshape(bsz, s, RET_HEADS, RET_V)
    mu = jnp.mean(y, axis=-1, keepdims=True)
    var = jnp.mean(jnp.square(y - mu), axis=-1, keepdims=True)
    y = ((y - mu) * lax.rsqrt(var + LN_EPS)).reshape(bsz, s, hv) * gn_g
    return (jax.nn.silu(g.astype(jnp.float32)) * y).astype(x.dtype) @ w_out


def conv_gated_ffn(x, w_up, conv_w, conv_b, w_down):
    h = causal_dwconv(x @ w_up, conv_w, conv_b)
    gate, up = jnp.split(h, 2, axis=-1)
    return (jax.nn.gelu(gate) * up) @ w_down


def _fwd_setup_inputs(seed: int = 0) -> dict:
    key = jax.random.key(seed)
    ks = iter(jax.random.split(key, 64))
    f32 = jnp.float32

    def nrm(shape, scale):
        return scale * jax.random.normal(next(ks), shape, f32)

    def gain(shape):
        return 1.0 + nrm(shape, 0.02)

    W, F, D = LRU_WIDTH, D_FF, D_MODEL
    x = jax.random.normal(next(ks), (BATCH, SEQ, D), f32)
    positions = jnp.broadcast_to(jnp.arange(SEQ, dtype=jnp.int32), (BATCH, SEQ))
    a0 = jax.random.uniform(next(ks), (N_A, W), f32, 0.9, 0.999)
    s0 = a0 ** (1.0 / LRU_C)
    return {
        "x": x,
        "positions": positions,
        "ln1_g": gain((DEPTH, D)), "ln1_b": nrm((DEPTH, D), 0.01),
        "ln2_g": gain((DEPTH, D)), "ln2_b": nrm((DEPTH, D), 0.01),
        "ffn_w_up": nrm((DEPTH, D, 2 * F), D ** -0.5),
        "ffn_conv_w": nrm((DEPTH, FFN_CONV_W, 2 * F), FFN_CONV_W ** -0.5),
        "ffn_conv_b": nrm((DEPTH, 2 * F), 0.01),
        "ffn_w_down": nrm((DEPTH, F, D), DEEPNORM_BETA * F ** -0.5),
        "lru_w_in": nrm((N_A, D, 2 * W), D ** -0.5),
        "lru_conv_w": nrm((N_A, LRU_CONV_W, W), LRU_CONV_W ** -0.5),
        "lru_conv_b": nrm((N_A, W), 0.01),
        "lru_w_a": nrm((N_A, LRU_BLOCKS, LRU_BLOCK_W, LRU_BLOCK_W), LRU_BLOCK_W ** -0.5),
        "lru_b_a": nrm((N_A, LRU_BLOCKS, LRU_BLOCK_W), 0.01),
        "lru_w_x": nrm((N_A, LRU_BLOCKS, LRU_BLOCK_W, LRU_BLOCK_W), LRU_BLOCK_W ** -0.5),
        "lru_b_x": nrm((N_A, LRU_BLOCKS, LRU_BLOCK_W), 0.01),
        "lru_lambda": jnp.log(s0) - jnp.log1p(-s0),
        "lru_w_out": nrm((N_A, W, D), DEEPNORM_BETA * W ** -0.5),
        "mla_w_in": nrm((N_B, D, Q_LORA + KV_LORA + QK_ROPE), D ** -0.5),
        "mla_q_norm": gain((N_B, Q_LORA)),
        "mla_kv_norm": gain((N_B, KV_LORA)),
        "mla_w_uq": nrm((N_B, Q_LORA, MLA_HEADS * (QK_NOPE + QK_ROPE)), Q_LORA ** -0.5),
        "mla_w_ukv": nrm((N_B, KV_LORA, MLA_HEADS * (QK_NOPE + V_HEAD)), KV_LORA ** -0.5),
        "mla_w_out": nrm((N_B, MLA_HEADS * V_HEAD, D), DEEPNORM_BETA * (MLA_HEADS * V_HEAD) ** -0.5),
        "ret_w_in": nrm((N_C, D, 2 * RET_HEADS * RET_QK + 2 * RET_HEADS * RET_V), D ** -0.5),
        "ret_gn_g": gain((N_C, RET_HEADS * RET_V)),
        "ret_w_out": nrm((N_C, RET_HEADS * RET_V, D), DEEPNORM_BETA * (RET_HEADS * RET_V) ** -0.5),
    }


def _fwd_reference(x, positions, ln1_g, ln1_b, ln2_g, ln2_b, ffn_w_up, ffn_conv_w, ffn_conv_b, ffn_w_down,
              lru_w_in, lru_conv_w, lru_conv_b, lru_w_a, lru_b_a, lru_w_x, lru_b_x, lru_lambda, lru_w_out,
              mla_w_in, mla_q_norm, mla_kv_norm, mla_w_uq, mla_w_ukv, mla_w_out,
              ret_w_in, ret_gn_g, ret_w_out):
    for i in range(DEPTH):
        kind, j = i % N_MIXERS, i // N_MIXERS
        if kind == 0:
            mix = rglru_mixer(x, lru_w_in[j], lru_conv_w[j], lru_conv_b[j], lru_w_a[j], lru_b_a[j],
                              lru_w_x[j], lru_b_x[j], lru_lambda[j], lru_w_out[j])
        elif kind == 1:
            mix = mla_mixer(x, positions, mla_w_in[j], mla_q_norm[j], mla_kv_norm[j],
                            mla_w_uq[j], mla_w_ukv[j], mla_w_out[j])
        else:
            mix = retention_mixer(x, positions, ret_w_in[j], ret_gn_g[j], ret_w_out[j])
        x = layer_norm(DEEPNORM_ALPHA * x + mix.astype(x.dtype), ln1_g[i], ln1_b[i]).astype(x.dtype)
        f = conv_gated_ffn(x, ffn_w_up[i], ffn_conv_w[i], ffn_conv_b[i], ffn_w_down[i])
        x = layer_norm(DEEPNORM_ALPHA * x + f.astype(x.dtype), ln2_g[i], ln2_b[i]).astype(x.dtype)
    return x


import jax as _jax
import jax.numpy as _jnp

TWIN_FORMAT = 'train_step'
FWD_PARAMS = ['x', 'positions', 'ln1_g', 'ln1_b', 'ln2_g', 'ln2_b', 'ffn_w_up', 'ffn_conv_w', 'ffn_conv_b', 'ffn_w_down', 'lru_w_in', 'lru_conv_w', 'lru_conv_b', 'lru_w_a', 'lru_b_a', 'lru_w_x', 'lru_b_x', 'lru_lambda', 'lru_w_out', 'mla_w_in', 'mla_q_norm', 'mla_kv_norm', 'mla_w_uq', 'mla_w_ukv', 'mla_w_out', 'ret_w_in', 'ret_gn_g', 'ret_w_out']
TWIN_WEIGHTS = ['ln1_g', 'ln1_b', 'ln2_g', 'ln2_b', 'ffn_w_up', 'ffn_conv_w', 'ffn_conv_b', 'ffn_w_down', 'lru_w_in', 'lru_conv_w', 'lru_conv_b', 'lru_w_a', 'lru_b_a', 'lru_w_x', 'lru_b_x', 'lru_lambda', 'lru_w_out', 'mla_w_in', 'mla_q_norm', 'mla_kv_norm', 'mla_w_uq', 'mla_w_ukv', 'mla_w_out', 'ret_w_in', 'ret_gn_g', 'ret_w_out']
TWIN_DIFF_INPUT = 'x'
TWIN_INPUTS = ['x', 'positions', 'ln1_g', 'ln1_b', 'ln2_g', 'ln2_b', 'ffn_w_up', 'ffn_conv_w', 'ffn_conv_b', 'ffn_w_down', 'lru_w_in', 'lru_conv_w', 'lru_conv_b', 'lru_w_a', 'lru_b_a', 'lru_w_x', 'lru_b_x', 'lru_lambda', 'lru_w_out', 'mla_w_in', 'mla_q_norm', 'mla_kv_norm', 'mla_w_uq', 'mla_w_ukv', 'mla_w_out', 'ret_w_in', 'ret_gn_g', 'ret_w_out', 'loss_target', 'm_ln1_g', 'm_ln1_b', 'm_ln2_g', 'm_ln2_b', 'm_ffn_w_up', 'm_ffn_conv_w', 'm_ffn_conv_b', 'm_ffn_w_down', 'm_lru_w_in', 'm_lru_conv_w', 'm_lru_conv_b', 'm_lru_w_a', 'm_lru_b_a', 'm_lru_w_x', 'm_lru_b_x', 'm_lru_lambda', 'm_lru_w_out', 'm_mla_w_in', 'm_mla_q_norm', 'm_mla_kv_norm', 'm_mla_w_uq', 'm_mla_w_ukv', 'm_mla_w_out', 'm_ret_w_in', 'm_ret_gn_g', 'm_ret_w_out', 'v_ln1_g', 'v_ln1_b', 'v_ln2_g', 'v_ln2_b', 'v_ffn_w_up', 'v_ffn_conv_w', 'v_ffn_conv_b', 'v_ffn_w_down', 'v_lru_w_in', 'v_lru_conv_w', 'v_lru_conv_b', 'v_lru_w_a', 'v_lru_b_a', 'v_lru_w_x', 'v_lru_b_x', 'v_lru_lambda', 'v_lru_w_out', 'v_mla_w_in', 'v_mla_q_norm', 'v_mla_kv_norm', 'v_mla_w_uq', 'v_mla_w_ukv', 'v_mla_w_out', 'v_ret_w_in', 'v_ret_gn_g', 'v_ret_w_out']
TWIN_OUTPUTS = ['loss', 'grad_x', 'grad_ln1_g', 'grad_ln1_b', 'grad_ln2_g', 'grad_ln2_b', 'grad_ffn_w_up', 'grad_ffn_conv_w', 'grad_ffn_conv_b', 'grad_ffn_w_down', 'grad_lru_w_in', 'grad_lru_conv_w', 'grad_lru_conv_b', 'grad_lru_w_a', 'grad_lru_b_a', 'grad_lru_w_x', 'grad_lru_b_x', 'grad_lru_lambda', 'grad_lru_w_out', 'grad_mla_w_in', 'grad_mla_q_norm', 'grad_mla_kv_norm', 'grad_mla_w_uq', 'grad_mla_w_ukv', 'grad_mla_w_out', 'grad_ret_w_in', 'grad_ret_gn_g', 'grad_ret_w_out', 'delta_ln1_g', 'delta_ln1_b', 'delta_ln2_g', 'delta_ln2_b', 'delta_ffn_w_up', 'delta_ffn_conv_w', 'delta_ffn_conv_b', 'delta_ffn_w_down', 'delta_lru_w_in', 'delta_lru_conv_w', 'delta_lru_conv_b', 'delta_lru_w_a', 'delta_lru_b_a', 'delta_lru_w_x', 'delta_lru_b_x', 'delta_lru_lambda', 'delta_lru_w_out', 'delta_mla_w_in', 'delta_mla_q_norm', 'delta_mla_kv_norm', 'delta_mla_w_uq', 'delta_mla_w_ukv', 'delta_mla_w_out', 'delta_ret_w_in', 'delta_ret_gn_g', 'delta_ret_w_out', 'new_m_ln1_g', 'new_m_ln1_b', 'new_m_ln2_g', 'new_m_ln2_b', 'new_m_ffn_w_up', 'new_m_ffn_conv_w', 'new_m_ffn_conv_b', 'new_m_ffn_w_down', 'new_m_lru_w_in', 'new_m_lru_conv_w', 'new_m_lru_conv_b', 'new_m_lru_w_a', 'new_m_lru_b_a', 'new_m_lru_w_x', 'new_m_lru_b_x', 'new_m_lru_lambda', 'new_m_lru_w_out', 'new_m_mla_w_in', 'new_m_mla_q_norm', 'new_m_mla_kv_norm', 'new_m_mla_w_uq', 'new_m_mla_w_ukv', 'new_m_mla_w_out', 'new_m_ret_w_in', 'new_m_ret_gn_g', 'new_m_ret_w_out', 'new_v_ln1_g', 'new_v_ln1_b', 'new_v_ln2_g', 'new_v_ln2_b', 'new_v_ffn_w_up', 'new_v_ffn_conv_w', 'new_v_ffn_conv_b', 'new_v_ffn_w_down', 'new_v_lru_w_in', 'new_v_lru_conv_w', 'new_v_lru_conv_b', 'new_v_lru_w_a', 'new_v_lru_b_a', 'new_v_lru_w_x', 'new_v_lru_b_x', 'new_v_lru_lambda', 'new_v_lru_w_out', 'new_v_mla_w_in', 'new_v_mla_q_norm', 'new_v_mla_kv_norm', 'new_v_mla_w_uq', 'new_v_mla_w_ukv', 'new_v_mla_w_out', 'new_v_ret_w_in', 'new_v_ret_gn_g', 'new_v_ret_w_out']
TWIN_LEAF_KINDS = {'loss': 'loss', 'grad_x': 'grad_x', 'grad_ln1_g': 'grad_w', 'grad_ln1_b': 'grad_w', 'grad_ln2_g': 'grad_w', 'grad_ln2_b': 'grad_w', 'grad_ffn_w_up': 'grad_w', 'grad_ffn_conv_w': 'grad_w', 'grad_ffn_conv_b': 'grad_w', 'grad_ffn_w_down': 'grad_w', 'grad_lru_w_in': 'grad_w', 'grad_lru_conv_w': 'grad_w', 'grad_lru_conv_b': 'grad_w', 'grad_lru_w_a': 'grad_w', 'grad_lru_b_a': 'grad_w', 'grad_lru_w_x': 'grad_w', 'grad_lru_b_x': 'grad_w', 'grad_lru_lambda': 'grad_w', 'grad_lru_w_out': 'grad_w', 'grad_mla_w_in': 'grad_w', 'grad_mla_q_norm': 'grad_w', 'grad_mla_kv_norm': 'grad_w', 'grad_mla_w_uq': 'grad_w', 'grad_mla_w_ukv': 'grad_w', 'grad_mla_w_out': 'grad_w', 'grad_ret_w_in': 'grad_w', 'grad_ret_gn_g': 'grad_w', 'grad_ret_w_out': 'grad_w', 'delta_ln1_g': 'delta_w', 'delta_ln1_b': 'delta_w', 'delta_ln2_g': 'delta_w', 'delta_ln2_b': 'delta_w', 'delta_ffn_w_up': 'delta_w', 'delta_ffn_conv_w': 'delta_w', 'delta_ffn_conv_b': 'delta_w', 'delta_ffn_w_down': 'delta_w', 'delta_lru_w_in': 'delta_w', 'delta_lru_conv_w': 'delta_w', 'delta_lru_conv_b': 'delta_w', 'delta_lru_w_a': 'delta_w', 'delta_lru_b_a': 'delta_w', 'delta_lru_w_x': 'delta_w', 'delta_lru_b_x': 'delta_w', 'delta_lru_lambda': 'delta_w', 'delta_lru_w_out': 'delta_w', 'delta_mla_w_in': 'delta_w', 'delta_mla_q_norm': 'delta_w', 'delta_mla_kv_norm': 'delta_w', 'delta_mla_w_uq': 'delta_w', 'delta_mla_w_ukv': 'delta_w', 'delta_mla_w_out': 'delta_w', 'delta_ret_w_in': 'delta_w', 'delta_ret_gn_g': 'delta_w', 'delta_ret_w_out': 'delta_w', 'new_m_ln1_g': 'new_m', 'new_m_ln1_b': 'new_m', 'new_m_ln2_g': 'new_m', 'new_m_ln2_b': 'new_m', 'new_m_ffn_w_up': 'new_m', 'new_m_ffn_conv_w': 'new_m', 'new_m_ffn_conv_b': 'new_m', 'new_m_ffn_w_down': 'new_m', 'new_m_lru_w_in': 'new_m', 'new_m_lru_conv_w': 'new_m', 'new_m_lru_conv_b': 'new_m', 'new_m_lru_w_a': 'new_m', 'new_m_lru_b_a': 'new_m', 'new_m_lru_w_x': 'new_m', 'new_m_lru_b_x': 'new_m', 'new_m_lru_lambda': 'new_m', 'new_m_lru_w_out': 'new_m', 'new_m_mla_w_in': 'new_m', 'new_m_mla_q_norm': 'new_m', 'new_m_mla_kv_norm': 'new_m', 'new_m_mla_w_uq': 'new_m', 'new_m_mla_w_ukv': 'new_m', 'new_m_mla_w_out': 'new_m', 'new_m_ret_w_in': 'new_m', 'new_m_ret_gn_g': 'new_m', 'new_m_ret_w_out': 'new_m', 'new_v_ln1_g': 'new_v', 'new_v_ln1_b': 'new_v', 'new_v_ln2_g': 'new_v', 'new_v_ln2_b': 'new_v', 'new_v_ffn_w_up': 'new_v', 'new_v_ffn_conv_w': 'new_v', 'new_v_ffn_conv_b': 'new_v', 'new_v_ffn_w_down': 'new_v', 'new_v_lru_w_in': 'new_v', 'new_v_lru_conv_w': 'new_v', 'new_v_lru_conv_b': 'new_v', 'new_v_lru_w_a': 'new_v', 'new_v_lru_b_a': 'new_v', 'new_v_lru_w_x': 'new_v', 'new_v_lru_b_x': 'new_v', 'new_v_lru_lambda': 'new_v', 'new_v_lru_w_out': 'new_v', 'new_v_mla_w_in': 'new_v', 'new_v_mla_q_norm': 'new_v', 'new_v_mla_kv_norm': 'new_v', 'new_v_mla_w_uq': 'new_v', 'new_v_mla_w_ukv': 'new_v', 'new_v_mla_w_out': 'new_v', 'new_v_ret_w_in': 'new_v', 'new_v_ret_gn_g': 'new_v', 'new_v_ret_w_out': 'new_v'}


def _forward(args):
    return _fwd_reference(*[args[k] for k in FWD_PARAMS])


def _output_shape():
    def fwd():
        inp = _fwd_setup_inputs(0)
        return _fwd_reference(*[inp[k] for k in FWD_PARAMS])
    out = _jax.eval_shape(fwd)
    return out.shape, out.dtype

N_MICROBATCH = 1
ADAM_LR = 0.001
ADAM_B1 = 0.9
ADAM_B2 = 0.999
ADAM_EPS = 1e-08
ADAM_WD = 0.01
ADAM_STEP = 10
PER_EXAMPLE_BATCH_AXIS = {'x': 0, 'positions': 0, 'loss_target': 0}
SHARED_INPUTS = []
_WEIGHT_DTYPES = {'ln1_g': _jnp.float32, 'ln1_b': _jnp.float32, 'ln2_g': _jnp.float32, 'ln2_b': _jnp.float32, 'ffn_w_up': _jnp.float32, 'ffn_conv_w': _jnp.float32, 'ffn_conv_b': _jnp.float32, 'ffn_w_down': _jnp.float32, 'lru_w_in': _jnp.float32, 'lru_conv_w': _jnp.float32, 'lru_conv_b': _jnp.float32, 'lru_w_a': _jnp.float32, 'lru_b_a': _jnp.float32, 'lru_w_x': _jnp.float32, 'lru_b_x': _jnp.float32, 'lru_lambda': _jnp.float32, 'lru_w_out': _jnp.float32, 'mla_w_in': _jnp.float32, 'mla_q_norm': _jnp.float32, 'mla_kv_norm': _jnp.float32, 'mla_w_uq': _jnp.float32, 'mla_w_ukv': _jnp.float32, 'mla_w_out': _jnp.float32, 'ret_w_in': _jnp.float32, 'ret_gn_g': _jnp.float32, 'ret_w_out': _jnp.float32}
MOMENT_SCALE = {'ln1_g': 4.035974e+00, 'ln1_b': 1.050039e+00, 'ln2_g': 6.442984e+01, 'ln2_b': 1.918662e+00, 'ffn_w_up': 3.678434e-02, 'ffn_conv_w': 3.677238e-02, 'ffn_conv_b': 4.326330e-02, 'ffn_w_down': 1.428990e-01, 'lru_w_in': 4.278242e-02, 'lru_conv_w': 4.362514e-02, 'lru_conv_b': 8.654075e-01, 'lru_w_a': 1.354987e-02, 'lru_b_a': 1.033663e-02, 'lru_w_x': 2.433100e-02, 'lru_b_x': 1.516182e-02, 'lru_lambda': 2.202435e-02, 'lru_w_out': 1.063462e-01, 'mla_w_in': 2.420742e-02, 'mla_q_norm': 1.658635e-02, 'mla_kv_norm': 4.262127e-02, 'mla_w_uq': 1.153711e-02, 'mla_w_ukv': 1.404139e-02, 'mla_w_out': 3.803524e-02, 'ret_w_in': 4.465820e-02, 'ret_gn_g': 4.119479e-02, 'ret_w_out': 1.281268e-01}


def _to_microbatches(a, axis):
    t = _jnp.moveaxis(a, axis, 0)
    t = t.reshape((N_MICROBATCH, t.shape[0] // N_MICROBATCH) + t.shape[1:])
    return _jnp.moveaxis(t, 1, axis + 1)


def setup_inputs(seed: int = 0) -> dict:
    inp = _fwd_setup_inputs(seed)
    key = _jax.random.fold_in(_jax.random.key(seed), 7919)
    shape, _ = _output_shape()
    out = dict(inp)
    out["loss_target"] = _jax.random.normal(_jax.random.fold_in(key, 0), shape, _jnp.float32)
    for i, name in enumerate(TWIN_WEIGHTS):
        w = inp[name].astype(_jnp.float32)
        if MOMENT_SCALE is None:
            s = _jnp.sqrt(_jnp.mean(_jnp.square(w)) + 1e-30)
        else:
            s = MOMENT_SCALE[name]
        km, kv = _jax.random.split(_jax.random.fold_in(key, i + 1))
        out[name] = w
        out["m_" + name] = s * _jax.random.normal(km, w.shape, _jnp.float32)
        out["v_" + name] = (s * s) * _jax.random.uniform(kv, w.shape, _jnp.float32, 0.5, 1.5)
    if N_MICROBATCH > 1:
        for name, axis in PER_EXAMPLE_BATCH_AXIS.items():
            out[name] = _to_microbatches(out[name], axis)
    return {'x': out['x'], 'positions': out['positions'], 'ln1_g': out['ln1_g'], 'ln1_b': out['ln1_b'], 'ln2_g': out['ln2_g'], 'ln2_b': out['ln2_b'], 'ffn_w_up': out['ffn_w_up'], 'ffn_conv_w': out['ffn_conv_w'], 'ffn_conv_b': out['ffn_conv_b'], 'ffn_w_down': out['ffn_w_down'], 'lru_w_in': out['lru_w_in'], 'lru_conv_w': out['lru_conv_w'], 'lru_conv_b': out['lru_conv_b'], 'lru_w_a': out['lru_w_a'], 'lru_b_a': out['lru_b_a'], 'lru_w_x': out['lru_w_x'], 'lru_b_x': out['lru_b_x'], 'lru_lambda': out['lru_lambda'], 'lru_w_out': out['lru_w_out'], 'mla_w_in': out['mla_w_in'], 'mla_q_norm': out['mla_q_norm'], 'mla_kv_norm': out['mla_kv_norm'], 'mla_w_uq': out['mla_w_uq'], 'mla_w_ukv': out['mla_w_ukv'], 'mla_w_out': out['mla_w_out'], 'ret_w_in': out['ret_w_in'], 'ret_gn_g': out['ret_gn_g'], 'ret_w_out': out['ret_w_out'], 'loss_target': out['loss_target'], 'm_ln1_g': out['m_ln1_g'], 'm_ln1_b': out['m_ln1_b'], 'm_ln2_g': out['m_ln2_g'], 'm_ln2_b': out['m_ln2_b'], 'm_ffn_w_up': out['m_ffn_w_up'], 'm_ffn_conv_w': out['m_ffn_conv_w'], 'm_ffn_conv_b': out['m_ffn_conv_b'], 'm_ffn_w_down': out['m_ffn_w_down'], 'm_lru_w_in': out['m_lru_w_in'], 'm_lru_conv_w': out['m_lru_conv_w'], 'm_lru_conv_b': out['m_lru_conv_b'], 'm_lru_w_a': out['m_lru_w_a'], 'm_lru_b_a': out['m_lru_b_a'], 'm_lru_w_x': out['m_lru_w_x'], 'm_lru_b_x': out['m_lru_b_x'], 'm_lru_lambda': out['m_lru_lambda'], 'm_lru_w_out': out['m_lru_w_out'], 'm_mla_w_in': out['m_mla_w_in'], 'm_mla_q_norm': out['m_mla_q_norm'], 'm_mla_kv_norm': out['m_mla_kv_norm'], 'm_mla_w_uq': out['m_mla_w_uq'], 'm_mla_w_ukv': out['m_mla_w_ukv'], 'm_mla_w_out': out['m_mla_w_out'], 'm_ret_w_in': out['m_ret_w_in'], 'm_ret_gn_g': out['m_ret_gn_g'], 'm_ret_w_out': out['m_ret_w_out'], 'v_ln1_g': out['v_ln1_g'], 'v_ln1_b': out['v_ln1_b'], 'v_ln2_g': out['v_ln2_g'], 'v_ln2_b': out['v_ln2_b'], 'v_ffn_w_up': out['v_ffn_w_up'], 'v_ffn_conv_w': out['v_ffn_conv_w'], 'v_ffn_conv_b': out['v_ffn_conv_b'], 'v_ffn_w_down': out['v_ffn_w_down'], 'v_lru_w_in': out['v_lru_w_in'], 'v_lru_conv_w': out['v_lru_conv_w'], 'v_lru_conv_b': out['v_lru_conv_b'], 'v_lru_w_a': out['v_lru_w_a'], 'v_lru_b_a': out['v_lru_b_a'], 'v_lru_w_x': out['v_lru_w_x'], 'v_lru_b_x': out['v_lru_b_x'], 'v_lru_lambda': out['v_lru_lambda'], 'v_lru_w_out': out['v_lru_w_out'], 'v_mla_w_in': out['v_mla_w_in'], 'v_mla_q_norm': out['v_mla_q_norm'], 'v_mla_kv_norm': out['v_mla_kv_norm'], 'v_mla_w_uq': out['v_mla_w_uq'], 'v_mla_w_ukv': out['v_mla_w_ukv'], 'v_mla_w_out': out['v_mla_w_out'], 'v_ret_w_in': out['v_ret_w_in'], 'v_ret_gn_g': out['v_ret_gn_g'], 'v_ret_w_out': out['v_ret_w_out']}


def _loss(weights, diff, rest, loss_target):
    with _jax.named_scope("forward"):
        args = {**rest, TWIN_DIFF_INPUT: diff, **{k: w.astype(_WEIGHT_DTYPES[k]) for k, w in weights.items()}}
        y = _forward(args)
    with _jax.named_scope("loss_head"):
        err = _jnp.square(y.astype(_jnp.float32) - loss_target)
        return 0.5 * _jnp.sum(_jnp.mean(err, axis=-1)) if err.ndim else 0.5 * err


def _adamw(w, g, m, v):
    m = ADAM_B1 * m + (1.0 - ADAM_B1) * g
    v = ADAM_B2 * v + (1.0 - ADAM_B2) * _jnp.square(g)
    m_hat = m / (1.0 - ADAM_B1 ** ADAM_STEP)
    v_hat = v / (1.0 - ADAM_B2 ** ADAM_STEP)
    delta = -ADAM_LR * (m_hat / (_jnp.sqrt(v_hat) + ADAM_EPS) + ADAM_WD * w)
    return delta, m, v


def reference(x, positions, ln1_g, ln1_b, ln2_g, ln2_b, ffn_w_up, ffn_conv_w, ffn_conv_b, ffn_w_down, lru_w_in, lru_conv_w, lru_conv_b, lru_w_a, lru_b_a, lru_w_x, lru_b_x, lru_lambda, lru_w_out, mla_w_in, mla_q_norm, mla_kv_norm, mla_w_uq, mla_w_ukv, mla_w_out, ret_w_in, ret_gn_g, ret_w_out, loss_target, m_ln1_g, m_ln1_b, m_ln2_g, m_ln2_b, m_ffn_w_up, m_ffn_conv_w, m_ffn_conv_b, m_ffn_w_down, m_lru_w_in, m_lru_conv_w, m_lru_conv_b, m_lru_w_a, m_lru_b_a, m_lru_w_x, m_lru_b_x, m_lru_lambda, m_lru_w_out, m_mla_w_in, m_mla_q_norm, m_mla_kv_norm, m_mla_w_uq, m_mla_w_ukv, m_mla_w_out, m_ret_w_in, m_ret_gn_g, m_ret_w_out, v_ln1_g, v_ln1_b, v_ln2_g, v_ln2_b, v_ffn_w_up, v_ffn_conv_w, v_ffn_conv_b, v_ffn_w_down, v_lru_w_in, v_lru_conv_w, v_lru_conv_b, v_lru_w_a, v_lru_b_a, v_lru_w_x, v_lru_b_x, v_lru_lambda, v_lru_w_out, v_mla_w_in, v_mla_q_norm, v_mla_kv_norm, v_mla_w_uq, v_mla_w_ukv, v_mla_w_out, v_ret_w_in, v_ret_gn_g, v_ret_w_out):
    given = dict(x=x, positions=positions, ln1_g=ln1_g, ln1_b=ln1_b, ln2_g=ln2_g, ln2_b=ln2_b, ffn_w_up=ffn_w_up, ffn_conv_w=ffn_conv_w, ffn_conv_b=ffn_conv_b, ffn_w_down=ffn_w_down, lru_w_in=lru_w_in, lru_conv_w=lru_conv_w, lru_conv_b=lru_conv_b, lru_w_a=lru_w_a, lru_b_a=lru_b_a, lru_w_x=lru_w_x, lru_b_x=lru_b_x, lru_lambda=lru_lambda, lru_w_out=lru_w_out, mla_w_in=mla_w_in, mla_q_norm=mla_q_norm, mla_kv_norm=mla_kv_norm, mla_w_uq=mla_w_uq, mla_w_ukv=mla_w_ukv, mla_w_out=mla_w_out, ret_w_in=ret_w_in, ret_gn_g=ret_gn_g, ret_w_out=ret_w_out, loss_target=loss_target, m_ln1_g=m_ln1_g, m_ln1_b=m_ln1_b, m_ln2_g=m_ln2_g, m_ln2_b=m_ln2_b, m_ffn_w_up=m_ffn_w_up, m_ffn_conv_w=m_ffn_conv_w, m_ffn_conv_b=m_ffn_conv_b, m_ffn_w_down=m_ffn_w_down, m_lru_w_in=m_lru_w_in, m_lru_conv_w=m_lru_conv_w, m_lru_conv_b=m_lru_conv_b, m_lru_w_a=m_lru_w_a, m_lru_b_a=m_lru_b_a, m_lru_w_x=m_lru_w_x, m_lru_b_x=m_lru_b_x, m_lru_lambda=m_lru_lambda, m_lru_w_out=m_lru_w_out, m_mla_w_in=m_mla_w_in, m_mla_q_norm=m_mla_q_norm, m_mla_kv_norm=m_mla_kv_norm, m_mla_w_uq=m_mla_w_uq, m_mla_w_ukv=m_mla_w_ukv, m_mla_w_out=m_mla_w_out, m_ret_w_in=m_ret_w_in, m_ret_gn_g=m_ret_gn_g, m_ret_w_out=m_ret_w_out, v_ln1_g=v_ln1_g, v_ln1_b=v_ln1_b, v_ln2_g=v_ln2_g, v_ln2_b=v_ln2_b, v_ffn_w_up=v_ffn_w_up, v_ffn_conv_w=v_ffn_conv_w, v_ffn_conv_b=v_ffn_conv_b, v_ffn_w_down=v_ffn_w_down, v_lru_w_in=v_lru_w_in, v_lru_conv_w=v_lru_conv_w, v_lru_conv_b=v_lru_conv_b, v_lru_w_a=v_lru_w_a, v_lru_b_a=v_lru_b_a, v_lru_w_x=v_lru_w_x, v_lru_b_x=v_lru_b_x, v_lru_lambda=v_lru_lambda, v_lru_w_out=v_lru_w_out, v_mla_w_in=v_mla_w_in, v_mla_q_norm=v_mla_q_norm, v_mla_kv_norm=v_mla_kv_norm, v_mla_w_uq=v_mla_w_uq, v_mla_w_ukv=v_mla_w_ukv, v_mla_w_out=v_mla_w_out, v_ret_w_in=v_ret_w_in, v_ret_gn_g=v_ret_gn_g, v_ret_w_out=v_ret_w_out)
    weights = {n: given[n] for n in TWIN_WEIGHTS}
    shared = {n: given[n] for n in SHARED_INPUTS}
    per_example = {n: given[n] for n in ['x', 'positions']}
    grad_fn = _jax.value_and_grad(_loss, argnums=(0, 1))

    def one_microbatch(ex, loss_target):
        ex = dict(ex)
        diff = ex.pop(TWIN_DIFF_INPUT)
        return grad_fn(weights, diff, {**shared, **ex}, loss_target)

    if N_MICROBATCH == 1:
        loss, (grad_w, grad_x) = one_microbatch(per_example, given["loss_target"])
    else:
        def body(carry, xs):
            loss_sum, grad_sum = carry
            l_k, (gw_k, gx_k) = one_microbatch(xs[0], xs[1])
            with _jax.named_scope("update"):
                return (loss_sum + l_k, _jax.tree.map(_jnp.add, grad_sum, gw_k)), gx_k

        init = (_jnp.zeros((), _jnp.float32), _jax.tree.map(_jnp.zeros_like, weights))
        (loss, grad_w), grad_x = _jax.lax.scan(body, init, (per_example, given["loss_target"]))
    with _jax.named_scope("update"):
        delta_w, new_m, new_v = {}, {}, {}
        for n in TWIN_WEIGHTS:
            delta_w[n], new_m[n], new_v[n] = _adamw(weights[n], grad_w[n], given["m_" + n], given["v_" + n])
    return (loss, grad_x, *[grad_w[n] for n in TWIN_WEIGHTS], *[delta_w[n] for n in TWIN_WEIGHTS],
            *[new_m[n] for n in TWIN_WEIGHTS], *[new_v[n] for n in TWIN_WEIGHTS])
```

```python
import functools
import math

import numpy as np
import jax
import jax.numpy as jnp
from jax import lax
from jax.experimental import pallas as pl
from jax.experimental.pallas import tpu as pltpu

F32 = jnp.float32
BF16 = jnp.bfloat16

DEPTH = 4
ALPHA = (2.0 * DEPTH) ** 0.25
LN_EPS = 1e-5
RMS_EPS = 1e-6
ROPE_BASE = 10000.0
CHUNK = 64
LRU_C = 8.0
LRU_GROUPS = 4
MLA_HEADS = 16
QK_NOPE, QK_ROPE, V_HEAD = 64, 32, 64
Q_LORA, KV_LORA = 768, 256
RET_HEADS = 4
ADAM_LR, ADAM_B1, ADAM_B2, ADAM_EPS, ADAM_WD, ADAM_STEP = 0.001, 0.9, 0.999, 1e-08, 0.01, 10

LANES = 128
SUBLANES = 8
VMEM_LIMIT = 56 * 1024 * 1024

MESH_AXES = ("x", "y", "c")
N_CHIPS = 4


def _cp(sem):
    return pltpu.CompilerParams(dimension_semantics=sem, vmem_limit_bytes=VMEM_LIMIT)


def _sigmoid(x):
    return 1.0 / (1.0 + jnp.exp(-x))


_GELU_C = math.sqrt(2.0 / math.pi)


def _gelu_parts(x):
    x2 = x * x
    u = _GELU_C * (x + 0.044715 * x * x2)
    t = jnp.tanh(u)
    g = 0.5 * x * (1.0 + t)
    dg = 0.5 * (1.0 + t) + 0.5 * x * (1.0 - t * t) * _GELU_C * (1.0 + 3.0 * 0.044715 * x2)
    return g, dg


def _fold8(v):
    n = v.shape[0] // SUBLANES
    return v.reshape(n, SUBLANES, v.shape[1]).sum(axis=0)


def _dot(a, b):
    return jnp.dot(a, b, preferred_element_type=F32)


def _dot_tb(a, b):
    return lax.dot_general(a, b, (((1,), (1,)), ((), ())), preferred_element_type=F32)


def _dot_ta(a, b):
    return lax.dot_general(a, b, (((0,), (0,)), ((), ())), preferred_element_type=F32)


def _mm(a, b, *, out_dtype, tm, tn, name, a_koff=0):
    M = a.shape[0]
    K, N = b.shape

    def body(a_ref, b_ref, o_ref):
        o_ref[...] = _dot(a_ref[...].astype(BF16), b_ref[...].astype(BF16)).astype(out_dtype)

    return pl.pallas_call(
        body, name=name, grid=(M // tm, N // tn),
        in_specs=[pl.BlockSpec((tm, K), lambda i, j: (i, a_koff)),
                  pl.BlockSpec((K, tn), lambda i, j: (0, j))],
        out_specs=pl.BlockSpec((tm, tn), lambda i, j: (i, j)),
        out_shape=jax.ShapeDtypeStruct((M, N), out_dtype),
        compiler_params=_cp(("parallel", "parallel")),
    )(a, b)


def _mm_tb(pairs, b, *, out_dtype, tm, tk, name):
    M = pairs[0][0].shape[0]
    Kout = b.shape[0]
    n = len(pairs)

    def body(*refs):
        a_refs, b_refs, o_ref = refs[:n], refs[n:2 * n], refs[2 * n]
        acc = None
        for a_ref, b_ref in zip(a_refs, b_refs):
            t = _dot_tb(a_ref[...].astype(BF16), b_ref[...].astype(BF16))
            acc = t if acc is None else acc + t
        o_ref[...] = acc.astype(out_dtype)

    in_specs = [pl.BlockSpec((tm, a.shape[1]), lambda i, j: (i, 0)) for a, _ in pairs]
    for a, c0 in pairs:
        w = a.shape[1]
        assert c0 % w == 0
        in_specs.append(pl.BlockSpec((tk, w), functools.partial(lambda i, j, cb: (j, cb), cb=c0 // w)))
    return pl.pallas_call(
        body, name=name, grid=(M // tm, Kout // tk),
        in_specs=in_specs,
        out_specs=pl.BlockSpec((tm, tk), lambda i, j: (i, j)),
        out_shape=jax.ShapeDtypeStruct((M, Kout), out_dtype),
        compiler_params=_cp(("parallel", "parallel")),
    )(*[a for a, _ in pairs], *[b for _ in pairs])


def _mm_ta(a, b, *, tk, tn, tm, name, a_c0=0, a_w=None, b_c0=0, b_w=None):
    M = a.shape[0]
    a_w = a.shape[1] if a_w is None else a_w
    b_w = b.shape[1] if b_w is None else b_w
    assert a_c0 % tk == 0 and b_c0 % tn == 0 and a_w % tk == 0 and b_w % tn == 0

    def body(a_ref, b_ref, o_ref):
        @pl.when(pl.program_id(2) == 0)
        def _():
            o_ref[...] = jnp.zeros_like(o_ref)
        o_ref[...] += _dot_ta(a_ref[...].astype(BF16), b_ref[...].astype(BF16))

    return pl.pallas_call(
        body, name=name, grid=(a_w // tk, b_w // tn, M // tm),
        in_specs=[pl.BlockSpec((tm, tk), lambda i, j, m: (m, i + a_c0 // tk)),
                  pl.BlockSpec((tm, tn), lambda i, j, m: (m, j + b_c0 // tn))],
        out_specs=pl.BlockSpec((tk, tn), lambda i, j, m: (i, j)),
        out_shape=jax.ShapeDtypeStruct((a_w, b_w), F32),
        compiler_params=_cp(("parallel", "parallel", "arbitrary")),
    )(a, b)


def _ln_fwd(x, mix, g, b, *, ts, name):
    S, D = x.shape

    def body(x_ref, m_ref, g_ref, b_ref, o_ref, z_ref):
        z = ALPHA * x_ref[...] + m_ref[...]
        mu = jnp.mean(z, axis=-1, keepdims=True)
        zc = z - mu
        var = jnp.mean(zc * zc, axis=-1, keepdims=True)
        o_ref[...] = zc * lax.rsqrt(var + LN_EPS) * g_ref[...] + b_ref[...]
        z_ref[...] = z

    row = pl.BlockSpec((ts, D), lambda i: (i, 0))
    vec = pl.BlockSpec((1, D), lambda i: (0, 0))
    return pl.pallas_call(
        body, name=name, grid=(S // ts,),
        in_specs=[row, row, vec, vec], out_specs=[row, row],
        out_shape=[jax.ShapeDtypeStruct((S, D), F32)] * 2,
        compiler_params=_cp(("parallel",)),
    )(x, mix, g.reshape(1, D), b.reshape(1, D))


def _ln_bwd(da, db, z, g, *, ts, name):
    S, D = z.shape
    two = db is not None

    def body(*refs):
        if two:
            da_ref, db_ref, z_ref, g_ref, dz_ref, dg_ref, dbias_ref = refs
            dout = ALPHA * da_ref[...] + db_ref[...]
        else:
            da_ref, z_ref, g_ref, dz_ref, dg_ref, dbias_ref = refs
            dout = da_ref[...]

        @pl.when(pl.program_id(0) == 0)
        def _():
            dg_ref[...] = jnp.zeros_like(dg_ref)
            dbias_ref[...] = jnp.zeros_like(dbias_ref)

        z = z_ref[...]
        mu = jnp.mean(z, axis=-1, keepdims=True)
        zc = z - mu
        var = jnp.mean(zc * zc, axis=-1, keepdims=True)
        rstd = lax.rsqrt(var + LN_EPS)
        xhat = zc * rstd
        dxh = dout * g_ref[...]
        m1 = jnp.mean(dxh, axis=-1, keepdims=True)
        m2 = jnp.mean(dxh * xhat, axis=-1, keepdims=True)
        dz_ref[...] = rstd * (dxh - m1 - xhat * m2)
        dg_ref[...] += _fold8(dout * xhat)
        dbias_ref[...] += _fold8(dout)

    row = pl.BlockSpec((ts, D), lambda i: (i, 0))
    vec = pl.BlockSpec((1, D), lambda i: (0, 0))
    acc = pl.BlockSpec((SUBLANES, D), lambda i: (0, 0))
    args = [da, db, z, g.reshape(1, D)] if two else [da, z, g.reshape(1, D)]
    return pl.pallas_call(
        body, name=name, grid=(S // ts,),
        in_specs=[row] * (3 if two else 2) + [vec],
        out_specs=[row, acc, acc],
        out_shape=[jax.ShapeDtypeStruct((S, D), F32), jax.ShapeDtypeStruct((SUBLANES, D), F32),
                   jax.ShapeDtypeStruct((SUBLANES, D), F32)],
        compiler_params=_cp(("arbitrary",)),
    )(*args)


def _loss_head(y, t, *, ts, name):
    S, D = y.shape

    def body(y_ref, t_ref, dy_ref, p_ref):
        @pl.when(pl.program_id(0) == 0)
        def _():
            p_ref[...] = jnp.zeros_like(p_ref)
        d = y_ref[...] - t_ref[...]
        dy_ref[...] = d * (1.0 / D)
        p_ref[...] += _fold8(d * d)

    row = pl.BlockSpec((ts, D), lambda i: (i, 0))
    acc = pl.BlockSpec((SUBLANES, D), lambda i: (0, 0))
    return pl.pallas_call(
        body, name=name, grid=(S // ts,),
        in_specs=[row, row], out_specs=[row, acc],
        out_shape=[jax.ShapeDtypeStruct((S, D), F32), jax.ShapeDtypeStruct((SUBLANES, D), F32)],
        compiler_params=_cp(("arbitrary",)),
    )(y, t)


def _axpy(a, b, *, ts, name):
    S, D = a.shape

    def body(a_ref, b_ref, o_ref):
        o_ref[...] = ALPHA * a_ref[...] + b_ref[...]

    row = pl.BlockSpec((ts, D), lambda i: (i, 0))
    return pl.pallas_call(
        body, name=name, grid=(S // ts,), in_specs=[row, row], out_specs=row,
        out_shape=jax.ShapeDtypeStruct((S, D), F32), compiler_params=_cp(("parallel",)),
    )(a, b)


def _prev_halo_spec(ts, tc, coff):
    r = ts // SUBLANES
    return pl.BlockSpec((SUBLANES, tc), lambda i, j: (jnp.maximum(i * r - 1, 0), j + coff))


def _fill_prev(buf, halo_ref, cur, i):
    buf[0:SUBLANES, :] = jnp.where(i > 0, halo_ref[...], 0.0)
    buf[SUBLANES:, :] = cur


def _conv_fwd(x, w, b, *, K, ts, tc, x_c0, name):
    S = x.shape[0]
    C = w.shape[1]
    coff = x_c0 // tc

    def body(x_ref, halo_ref, w_ref, b_ref, o_ref, buf):
        _fill_prev(buf, halo_ref, x_ref[...], pl.program_id(0))
        acc = b_ref[...] + w_ref[K - 1:K, :] * x_ref[...]
        for k in range(K - 1):
            acc = acc + w_ref[k:k + 1, :] * buf[pl.ds(SUBLANES - (K - 1) + k, ts), :]
        o_ref[...] = acc

    return pl.pallas_call(
        body, name=name, grid=(S // ts, C // tc),
        in_specs=[pl.BlockSpec((ts, tc), lambda i, j: (i, j + coff)), _prev_halo_spec(ts, tc, coff),
                  pl.BlockSpec((K, tc), lambda i, j: (0, j)), pl.BlockSpec((1, tc), lambda i, j: (0, j))],
        out_specs=pl.BlockSpec((ts, tc), lambda i, j: (i, j)),
        out_shape=jax.ShapeDtypeStruct((S, C), F32),
        scratch_shapes=[pltpu.VMEM((ts + SUBLANES, tc), F32)],
        compiler_params=_cp(("parallel", "parallel")),
    )(x, x, w, b.reshape(1, C))


def _conv_wgrad(dy, x, *, K, ts, tc, x_c0, name):
    S, C = dy.shape
    coff = x_c0 // tc

    def body(dy_ref, x_ref, halo_ref, dw_ref, db_ref, buf):
        i = pl.program_id(1)

        @pl.when(i == 0)
        def _():
            dw_ref[...] = jnp.zeros_like(dw_ref)
            db_ref[...] = jnp.zeros_like(db_ref)

        _fill_prev(buf, halo_ref, x_ref[...], i)
        dy_v = dy_ref[...]
        db_ref[...] += _fold8(dy_v)
        for k in range(K):
            xs = buf[pl.ds(SUBLANES - (K - 1) + k, ts), :]
            dw_ref[k] += _fold8(dy_v * xs)

    r = ts // SUBLANES
    return pl.pallas_call(
        body, name=name, grid=(C // tc, S // ts),
        in_specs=[pl.BlockSpec((ts, tc), lambda j, i: (i, j)),
                  pl.BlockSpec((ts, tc), lambda j, i: (i, j + coff)),
                  pl.BlockSpec((SUBLANES, tc), lambda j, i: (jnp.maximum(i * r - 1, 0), j + coff))],
        out_specs=[pl.BlockSpec((K, SUBLANES, tc), lambda j, i: (0, 0, j)),
                   pl.BlockSpec((SUBLANES, tc), lambda j, i: (0, j))],
        out_shape=[jax.ShapeDtypeStruct((K, SUBLANES, C), F32), jax.ShapeDtypeStruct((SUBLANES, C), F32)],
        scratch_shapes=[pltpu.VMEM((ts + SUBLANES, tc), F32)],
        compiler_params=_cp(("parallel", "arbitrary")),
    )(dy, x, x)


def _conv_bwd(dy, w, *, K, ts, tc, w_c0, out_dtype, name):
    S, C = dy.shape
    nb = S // ts
    r = ts // SUBLANES
    woff = w_c0 // tc

    def body(dy_ref, halo_ref, w_ref, o_ref, buf):
        i = pl.program_id(0)
        buf[0:ts, :] = dy_ref[...]
        buf[ts:, :] = jnp.where(i < nb - 1, halo_ref[...], 0.0)
        acc = w_ref[K - 1:K, :] * dy_ref[...]
        for k in range(K - 1):
            acc = acc + w_ref[k:k + 1, :] * buf[pl.ds(K - 1 - k, ts), :]
        o_ref[...] = acc.astype(out_dtype)

    return pl.pallas_call(
        body, name=name, grid=(nb, C // tc),
        in_specs=[pl.BlockSpec((ts, tc), lambda i, j: (i, j)),
                  pl.BlockSpec((SUBLANES, tc), lambda i, j: (jnp.minimum((i + 1) * r, nb * r - 1), j)),
                  pl.BlockSpec((K, tc), lambda i, j: (0, j + woff))],
        out_specs=pl.BlockSpec((ts, tc), lambda i, j: (i, j)),
        out_shape=jax.ShapeDtypeStruct((S, C), out_dtype),
        scratch_shapes=[pltpu.VMEM((ts + SUBLANES, tc), F32)],
        compiler_params=_cp(("parallel", "parallel")),
    )(dy, dy, w)


def _ffn_gate_up(g_ref, gh_ref, u_ref, uh_ref, wg_ref, wu_ref, bg_ref, bu_ref, gbuf, ubuf, i, ts):
    _fill_prev(gbuf, gh_ref, g_ref[...], i)
    _fill_prev(ubuf, uh_ref, u_ref[...], i)
    gate = bg_ref[...] + wg_ref[2:3, :] * g_ref[...]
    up = bu_ref[...] + wu_ref[2:3, :] * u_ref[...]
    for k in range(2):
        gate = gate + wg_ref[k:k + 1, :] * gbuf[pl.ds(SUBLANES - 2 + k, ts), :]
        up = up + wu_ref[k:k + 1, :] * ubuf[pl.ds(SUBLANES - 2 + k, ts), :]
    return gate, up


def _ffn_mid_specs(ts, tc, nf):
    cur_g = pl.BlockSpec((ts, tc), lambda i, j: (i, j))
    cur_u = pl.BlockSpec((ts, tc), lambda i, j: (i, j + nf))
    return [cur_g, _prev_halo_spec(ts, tc, 0), cur_u, _prev_halo_spec(ts, tc, nf),
            pl.BlockSpec((3, tc), lambda i, j: (0, j)), pl.BlockSpec((3, tc), lambda i, j: (0, j + nf)),
            pl.BlockSpec((1, tc), lambda i, j: (0, j)), pl.BlockSpec((1, tc), lambda i, j: (0, j + nf))]


def _ffn_mid_fwd(hpre, w, b, *, ts, tc, name):
    S, F2 = hpre.shape
    F = F2 // 2
    nf = F // tc

    def body(g_ref, gh_ref, u_ref, uh_ref, wg_ref, wu_ref, bg_ref, bu_ref, o_ref, gbuf, ubuf):
        gate, up = _ffn_gate_up(g_ref, gh_ref, u_ref, uh_ref, wg_ref, wu_ref, bg_ref, bu_ref, gbuf, ubuf,
                                pl.program_id(0), ts)
        gel, _ = _gelu_parts(gate)
        o_ref[...] = (gel * up).astype(BF16)

    b2 = b.reshape(1, F2)
    return pl.pallas_call(
        body, name=name, grid=(S // ts, nf),
        in_specs=_ffn_mid_specs(ts, tc, nf),
        out_specs=pl.BlockSpec((ts, tc), lambda i, j: (i, j)),
        out_shape=jax.ShapeDtypeStruct((S, F), BF16),
        scratch_shapes=[pltpu.VMEM((ts + SUBLANES, tc), F32)] * 2,
        compiler_params=_cp(("parallel", "parallel")),
    )(hpre, hpre, hpre, hpre, w, w, b2, b2)


def _ffn_mid_bwd(hpre, da, w, b, *, ts, tc, name):
    S, F2 = hpre.shape
    F = F2 // 2
    nf = F // tc

    def body(g_ref, gh_ref, u_ref, uh_ref, wg_ref, wu_ref, bg_ref, bu_ref, da_ref, dg_ref, du_ref, gbuf, ubuf):
        gate, up = _ffn_gate_up(g_ref, gh_ref, u_ref, uh_ref, wg_ref, wu_ref, bg_ref, bu_ref, gbuf, ubuf,
                                pl.program_id(0), ts)
        gel, dgel = _gelu_parts(gate)
        da_v = da_ref[...]
        dg_ref[...] = da_v * up * dgel
        du_ref[...] = da_v * gel

    b2 = b.reshape(1, F2)
    blk = pl.BlockSpec((ts, tc), lambda i, j: (i, j))
    return pl.pallas_call(
        body, name=name, grid=(S // ts, nf),
        in_specs=_ffn_mid_specs(ts, tc, nf) + [blk],
        out_specs=[blk, blk],
        out_shape=[jax.ShapeDtypeStruct((S, F), F32)] * 2,
        scratch_shapes=[pltpu.VMEM((ts + SUBLANES, tc), F32)] * 2,
        compiler_params=_cp(("parallel", "parallel")),
    )(hpre, hpre, hpre, hpre, w, w, b2, b2, da)


def _expm1(x):
    u = jnp.exp(x)
    um1 = u - 1.0
    safe = jnp.where(um1 == 0.0, 1.0, jnp.log(u))
    r = jnp.where(um1 == 0.0, x, um1 * x / safe)
    return jnp.where(x < -30.0, -1.0, r)


def _softplus(z):
    return jnp.maximum(z, 0.0) + jnp.log1p(jnp.exp(-jnp.abs(z)))


def _lru_gates(u, wa_ref, ba_ref, wx_ref, bx_ref, lam_ref):
    ub = u.astype(BF16)
    r = _sigmoid(_dot(ub, wa_ref[...]) + ba_ref[...])
    ig = _sigmoid(_dot(ub, wx_ref[...]) + bx_ref[...])
    sp = _softplus(-lam_ref[...])
    la = -LRU_C * r * sp
    a = jnp.exp(la)
    mult = jnp.sqrt(-_expm1(2.0 * la))
    return ub, r, ig, sp, a, mult


def _lru_specs(ts, gw):
    blk = pl.BlockSpec((ts, gw), lambda g, i: (i, g))
    wsp = pl.BlockSpec((None, gw, gw), lambda g, i: (g, 0, 0))
    vsp = pl.BlockSpec((None, 1, gw), lambda g, i: (g, 0, 0))
    return blk, wsp, vsp


def _lru_fwd(u, proj, w_a, b_a, w_x, b_x, lam, *, ts, name):
    S, W = u.shape
    G = LRU_GROUPS
    gw = W // G
    nt = ts // SUBLANES

    def body(u_ref, gb_ref, wa_ref, ba_ref, wx_ref, bx_ref, lam_ref, y_ref, h_ref, a_buf, b_buf, carry):
        @pl.when(pl.program_id(1) == 0)
        def _():
            carry[...] = jnp.zeros_like(carry)

        u_v = u_ref[...]
        _, _, ig, _, a, mult = _lru_gates(u_v, wa_ref, ba_ref, wx_ref, bx_ref, lam_ref)
        a_buf[...] = a
        b_buf[...] = mult * ig * u_v
        row = lax.broadcasted_iota(jnp.int32, (SUBLANES, gw), 0)

        def tile(k, c):
            r0 = pl.multiple_of(k * SUBLANES, SUBLANES)
            A = a_buf[pl.ds(r0, SUBLANES), :]
            B = b_buf[pl.ds(r0, SUBLANES), :]
            for d in (1, 2, 4):
                m = row >= d
                B = jnp.where(m, A * pltpu.roll(B, d, 0) + B, B)
                A = jnp.where(m, A * pltpu.roll(A, d, 0), A)
            h = A * c + B
            h_ref[pl.ds(r0, SUBLANES), :] = h
            return h[SUBLANES - 1:SUBLANES, :]

        carry[...] = lax.fori_loop(0, nt, tile, carry[...])
        gel, _ = _gelu_parts(gb_ref[...])
        y_ref[...] = (gel * h_ref[...]).astype(BF16)

    blk, wsp, vsp = _lru_specs(ts, gw)
    return pl.pallas_call(
        body, name=name, grid=(G, S // ts),
        in_specs=[blk, blk, wsp, vsp, wsp, vsp, vsp],
        out_specs=[blk, blk],
        out_shape=[jax.ShapeDtypeStruct((S, W), BF16), jax.ShapeDtypeStruct((S, W), F32)],
        scratch_shapes=[pltpu.VMEM((ts, gw), F32), pltpu.VMEM((ts, gw), F32), pltpu.VMEM((1, gw), F32)],
        compiler_params=_cp(("parallel", "arbitrary")),
    )(u, proj, w_a, b_a.reshape(G, 1, gw), w_x, b_x.reshape(G, 1, gw), lam.reshape(G, 1, gw))


def _lru_bwd(dy, u, proj, h, w_a, b_a, w_x, b_x, lam, *, ts, name):
    S, W = u.shape
    G = LRU_GROUPS
    gw = W // G
    nt = ts // SUBLANES
    nb = S // ts
    r8 = ts // SUBLANES

    def body(dy_ref, u_ref, gb_ref, h_ref, hh_ref, wa_ref, ba_ref, wx_ref, bx_ref, lam_ref,
             dgb_ref, du_ref, dwa_ref, dwx_ref, dba_ref, dbx_ref, dlam_ref,
             a_buf, q_buf, p_buf, hbuf, carry):
        i = pl.program_id(1)
        ib = nb - 1 - i

        @pl.when(i == 0)
        def _():
            carry[...] = jnp.zeros_like(carry)
            dwa_ref[...] = jnp.zeros_like(dwa_ref)
            dwx_ref[...] = jnp.zeros_like(dwx_ref)
            dba_ref[...] = jnp.zeros_like(dba_ref)
            dbx_ref[...] = jnp.zeros_like(dbx_ref)
            dlam_ref[...] = jnp.zeros_like(dlam_ref)

        u_v = u_ref[...]
        ub, r, ig, sp, a, mult = _lru_gates(u_v, wa_ref, ba_ref, wx_ref, bx_ref, lam_ref)
        gel, dgel = _gelu_parts(gb_ref[...])
        dy_v = dy_ref[...]
        h_v = h_ref[...]
        dh = dy_v * gel
        dgb_ref[...] = (dy_v * h_v * dgel).astype(BF16)
        a_buf[...] = a
        q_buf[...] = a * dh
        row = lax.broadcasted_iota(jnp.int32, (SUBLANES, gw), 0)

        def tile(kk, c):
            r0 = pl.multiple_of((nt - 1 - kk) * SUBLANES, SUBLANES)
            A = a_buf[pl.ds(r0, SUBLANES), :]
            B = q_buf[pl.ds(r0, SUBLANES), :]
            for d in (1, 2, 4):
                m = row < SUBLANES - d
                B = jnp.where(m, A * pltpu.roll(B, SUBLANES - d, 0) + B, B)
                A = jnp.where(m, A * pltpu.roll(A, SUBLANES - d, 0), A)
            P = A * c + B
            p_buf[pl.ds(r0, SUBLANES), :] = jnp.where(row == SUBLANES - 1, c, pltpu.roll(P, SUBLANES - 1, 0))
            return P[0:1, :]

        carry[...] = lax.fori_loop(0, nt, tile, carry[...])
        Gt = dh + p_buf[...]
        hbuf[0:SUBLANES, :] = jnp.where(ib > 0, hh_ref[...], 0.0)
        hbuf[SUBLANES:, :] = h_v
        hprev = hbuf[pl.ds(SUBLANES - 1, ts), :]
        da = Gt * hprev
        dmult = Gt * (ig * u_v)
        dla = da * a - dmult * (a * a) / mult
        dr = dla * (-LRU_C * sp)
        dlam_ref[...] += _fold8(dla * (LRU_C * r)) * _sigmoid(-lam_ref[...])
        dig = Gt * mult * u_v
        dzr = dr * r * (1.0 - r)
        dzi = dig * ig * (1.0 - ig)
        dzr_b = dzr.astype(BF16)
        dzi_b = dzi.astype(BF16)
        du_ref[...] = Gt * mult * ig + _dot_tb(dzr_b, wa_ref[...]) + _dot_tb(dzi_b, wx_ref[...])
        dwa_ref[...] += _dot_ta(ub, dzr_b)
        dwx_ref[...] += _dot_ta(ub, dzi_b)
        dba_ref[...] += _fold8(dzr)
        dbx_ref[...] += _fold8(dzi)

    rblk = pl.BlockSpec((ts, gw), lambda g, i: (nb - 1 - i, g))
    halo = pl.BlockSpec((SUBLANES, gw), lambda g, i: (jnp.maximum((nb - 1 - i) * r8 - 1, 0), g))
    wsp = pl.BlockSpec((None, gw, gw), lambda g, i: (g, 0, 0))
    vsp = pl.BlockSpec((None, 1, gw), lambda g, i: (g, 0, 0))
    acc8 = pl.BlockSpec((None, SUBLANES, gw), lambda g, i: (g, 0, 0))
    return pl.pallas_call(
        body, name=name, grid=(G, nb),
        in_specs=[rblk, rblk, rblk, rblk, halo, wsp, vsp, wsp, vsp, vsp],
        out_specs=[rblk, rblk, wsp, wsp, acc8, acc8, acc8],
        out_shape=[jax.ShapeDtypeStruct((S, W), BF16), jax.ShapeDtypeStruct((S, W), F32),
                   jax.ShapeDtypeStruct((G, gw, gw), F32), jax.ShapeDtypeStruct((G, gw, gw), F32),
                   jax.ShapeDtypeStruct((G, SUBLANES, gw), F32), jax.ShapeDtypeStruct((G, SUBLANES, gw), F32),
                   jax.ShapeDtypeStruct((G, SUBLANES, gw), F32)],
        scratch_shapes=[pltpu.VMEM((ts, gw), F32)] * 3 + [pltpu.VMEM((ts + SUBLANES, gw), F32),
                                                          pltpu.VMEM((1, gw), F32)],
        compiler_params=_cp(("parallel", "arbitrary")),
    )(dy, u, proj, h, h, w_a, b_a.reshape(G, 1, gw), w_x, b_x.reshape(G, 1, gw), lam.reshape(G, 1, gw))


def _rt(S, pref):
    return min(S, pref)


def _lru_mixer_fwd(x, p, tag):
    S, D = x.shape
    W = p["w_out"].shape[0]
    ts = _rt(S, 512)
    proj = _mm(x, p["w_in"], out_dtype=F32, tm=ts, tn=W, name=tag + "_proj")
    u = _conv_fwd(proj, p["conv_w"], p["conv_b"], K=4, ts=ts, tc=512, x_c0=W, name=tag + "_conv")
    y, h = _lru_fwd(u, proj, p["w_a"], p["b_a"], p["w_x"], p["b_x"], p["lam"], ts=ts, name=tag + "_scan")
    mix = _mm(y, p["w_out"], out_dtype=F32, tm=ts, tn=D, name=tag + "_out")
    return mix, (proj, u, h, y)


def _lru_mixer_bwd(dmix, x, p, saved, tag):
    proj, u, h, y = saved
    S, D = x.shape
    W = p["w_out"].shape[0]
    ts = _rt(S, 512)
    g = {}
    dy = _mm_tb([(dmix, 0)], p["w_out"], out_dtype=F32, tm=ts, tk=W, name=tag + "_dy")
    g["w_out"] = _mm_ta(y, dmix, tk=W, tn=D, tm=ts, name=tag + "_dwout")
    dgb, du, g["w_a"], g["w_x"], dba8, dbx8, dlam8 = _lru_bwd(
        dy, u, proj, h, p["w_a"], p["b_a"], p["w_x"], p["b_x"], p["lam"], ts=ts, name=tag + "_scanb")
    g["b_a"] = dba8.sum(axis=1)
    g["b_x"] = dbx8.sum(axis=1)
    g["lam"] = dlam8.sum(axis=1).reshape(-1)
    dcw8, dcb8 = _conv_wgrad(du, proj, K=4, ts=ts, tc=512, x_c0=W, name=tag + "_convw")
    g["conv_w"] = dcw8.sum(axis=1)
    g["conv_b"] = dcb8.sum(axis=0)
    drnn = _conv_bwd(du, p["conv_w"], K=4, ts=ts, tc=512, w_c0=0, out_dtype=BF16, name=tag + "_convb")
    dx = _mm_tb([(dgb, 0), (drnn, W)], p["w_in"], out_dtype=F32, tm=ts, tk=D, name=tag + "_dx")
    g["w_in"] = jnp.concatenate(
        [_mm_ta(x, dgb, tk=D, tn=W, tm=ts, name=tag + "_dwin_g"),
         _mm_ta(x, drnn, tk=D, tn=W, tm=ts, name=tag + "_dwin_r")], axis=1)
    return dx, g


def _ffn_fwd(x, p, tag):
    S, D = x.shape
    F = p["w_down"].shape[0]
    ts = _rt(S, 512)
    tc = F // 2
    hpre = _mm(x, p["w_up"], out_dtype=F32, tm=ts, tn=tc, name=tag + "_up")
    a = _ffn_mid_fwd(hpre, p["conv_w"], p["conv_b"], ts=_rt(S, 256), tc=tc, name=tag + "_mid")
    f = _mm(a, p["w_down"], out_dtype=F32, tm=ts, tn=D, name=tag + "_down")
    return f, (hpre, a)


def _ffn_bwd(df, x, p, saved, tag):
    hpre, a = saved
    S, D = x.shape
    F = p["w_down"].shape[0]
    ts = _rt(S, 512)
    tsm = _rt(S, 256)
    tc = F // 2
    g = {}
    da = _mm_tb([(df, 0)], p["w_down"], out_dtype=F32, tm=ts, tk=tc, name=tag + "_da")
    g["w_down"] = _mm_ta(a, df, tk=tc, tn=D, tm=ts, name=tag + "_dwdown")
    dgate, dup = _ffn_mid_bwd(hpre, da, p["conv_w"], p["conv_b"], ts=tsm, tc=tc, name=tag + "_midb")
    dwg8, dbg8 = _conv_wgrad(dgate, hpre, K=3, ts=tsm, tc=tc, x_c0=0, name=tag + "_convw_g")
    dwu8, dbu8 = _conv_wgrad(dup, hpre, K=3, ts=tsm, tc=tc, x_c0=F, name=tag + "_convw_u")
    g["conv_w"] = jnp.concatenate([dwg8.sum(axis=1), dwu8.sum(axis=1)], axis=1)
    g["conv_b"] = jnp.concatenate([dbg8.sum(axis=0), dbu8.sum(axis=0)], axis=0)
    dpg = _conv_bwd(dgate, p["conv_w"], K=3, ts=tsm, tc=tc, w_c0=0, out_dtype=BF16, name=tag + "_convb_g")
    dpu = _conv_bwd(dup, p["conv_w"], K=3, ts=tsm, tc=tc, w_c0=F, out_dtype=BF16, name=tag + "_convb_u")
    dx = _mm_tb([(dpg, 0), (dpu, F)], p["w_up"], out_dtype=F32, tm=ts, tk=D, name=tag + "_dx")
    g["w_up"] = jnp.concatenate(
        [_mm_ta(x, dpg, tk=D, tn=tc, tm=ts, name=tag + "_dwup_g"),
         _mm_ta(x, dpu, tk=D, tn=tc, tm=ts, name=tag + "_dwup_u")], axis=1)
    return dx, g


HEAD_SLOT = LANES
MLA_SCALE = (QK_NOPE + QK_ROPE) ** -0.5
NEG_BIG = -1e30


def _rms_fwd(x, g, *, c0, ts, name):
    S = x.shape[0]
    w = g.shape[0]

    def body(x_ref, g_ref, o_ref):
        xv = x_ref[...]
        rstd = lax.rsqrt(jnp.mean(xv * xv, axis=-1, keepdims=True) + RMS_EPS)
        o_ref[...] = (xv * rstd * g_ref[...]).astype(BF16)

    return pl.pallas_call(
        body, name=name, grid=(S // ts,),
        in_specs=[pl.BlockSpec((ts, w), lambda i: (i, c0 // w)), pl.BlockSpec((1, w), lambda i: (0, 0))],
        out_specs=pl.BlockSpec((ts, w), lambda i: (i, 0)),
        out_shape=jax.ShapeDtypeStruct((S, w), BF16), compiler_params=_cp(("parallel",)),
    )(x, g.reshape(1, w))


def _rms_bwd(dy, x, g, *, c0, ts, name):
    S = x.shape[0]
    w = g.shape[0]

    def body(dy_ref, x_ref, g_ref, dx_ref, dg_ref):
        @pl.when(pl.program_id(0) == 0)
        def _():
            dg_ref[...] = jnp.zeros_like(dg_ref)
        xv = x_ref[...]
        dyv = dy_ref[...]
        rstd = lax.rsqrt(jnp.mean(xv * xv, axis=-1, keepdims=True) + RMS_EPS)
        dyg = dyv * g_ref[...]
        m = jnp.mean(dyg * xv, axis=-1, keepdims=True)
        dx_ref[...] = (rstd * (dyg - xv * (rstd * rstd) * m)).astype(BF16)
        dg_ref[...] += _fold8(dyv * xv * rstd)

    return pl.pallas_call(
        body, name=name, grid=(S // ts,),
        in_specs=[pl.BlockSpec((ts, w), lambda i: (i, 0)), pl.BlockSpec((ts, w), lambda i: (i, c0 // w)),
                  pl.BlockSpec((1, w), lambda i: (0, 0))],
        out_specs=[pl.BlockSpec((ts, w), lambda i: (i, 0)), pl.BlockSpec((SUBLANES, w), lambda i: (0, 0))],
        out_shape=[jax.ShapeDtypeStruct((S, w), BF16), jax.ShapeDtypeStruct((SUBLANES, w), F32)],
        compiler_params=_cp(("arbitrary",)),
    )(dy, x, g.reshape(1, w))


def _mla_tables(pos):
    S = pos.shape[0]
    half = QK_ROPE // 2
    inv_freq = ROPE_BASE ** (-jnp.arange(0, QK_ROPE, 2, dtype=F32) / QK_ROPE)
    ang = pos.astype(F32)[:, None] * inv_freq
    cos, sin = jnp.cos(ang), jnp.sin(ang)
    z = lambda n: jnp.zeros((S, n), F32)
    pad = HEAD_SLOT - QK_NOPE - QK_ROPE
    c = jnp.concatenate([jnp.ones((S, QK_NOPE), F32), cos, cos, z(pad)], axis=1)
    s1 = jnp.concatenate([z(QK_NOPE), -sin, z(half), z(pad)], axis=1)
    s2 = jnp.concatenate([z(QK_NOPE), z(half), sin, z(pad)], axis=1)
    return c, s1, s2


def _mla_prep_fwd(qraw, kvraw, proj, tabs, *, kpe_c0, ts, name):
    S = qraw.shape[0]
    H = MLA_HEADS
    half = QK_ROPE // 2

    def body(q_ref, kn_ref, kpe_ref, c_ref, s1_ref, s2_ref, qo_ref, ko_ref):
        c, s1, s2 = c_ref[...], s1_ref[...], s2_ref[...]

        def rope(v):
            return v * c + pltpu.roll(v, HEAD_SLOT - half, 1) * s1 + pltpu.roll(v, half, 1) * s2

        qo_ref[...] = (rope(q_ref[...]) * MLA_SCALE).astype(BF16)
        ko_ref[...] = (kn_ref[...] + rope(kpe_ref[...])).astype(BF16)

    slot = pl.BlockSpec((ts, HEAD_SLOT), lambda i, h: (i, h))
    tab = pl.BlockSpec((ts, HEAD_SLOT), lambda i, h: (i, 0))
    return pl.pallas_call(
        body, name=name, grid=(S // ts, H),
        in_specs=[slot, slot, pl.BlockSpec((ts, HEAD_SLOT), lambda i, h: (i, kpe_c0 // HEAD_SLOT)), tab, tab, tab],
        out_specs=[slot, slot],
        out_shape=[jax.ShapeDtypeStruct((S, H * HEAD_SLOT), BF16)] * 2,
        compiler_params=_cp(("parallel", "parallel")),
    )(qraw, kvraw, proj, *tabs)


def _mla_prep_bwd(dq, dk, dv, tabs, *, ts, name):
    S = dq.shape[0]
    H = MLA_HEADS
    half = QK_ROPE // 2
    kw = H * HEAD_SLOT
    vw = H * V_HEAD

    def body(dq_ref, dk_ref, dv_ref, c_ref, s1_ref, s2_ref, dqr_ref, dkv_ref, dkpe_ref):
        c, s1, s2 = c_ref[...], s1_ref[...], s2_ref[...]

        def rope_t(g):
            return g * c + pltpu.roll(g * s1, half, 1) + pltpu.roll(g * s2, HEAD_SLOT - half, 1)

        gsum = jnp.zeros((ts, HEAD_SLOT), F32)
        for h in range(H):
            sl = slice(h * HEAD_SLOT, (h + 1) * HEAD_SLOT)
            dqr_ref[:, sl] = (rope_t(dq_ref[:, sl]) * MLA_SCALE).astype(BF16)
            dkh = dk_ref[:, sl]
            dkv_ref[:, sl] = dkh.astype(BF16)
            gsum = gsum + dkh
        dkv_ref[:, kw:] = dv_ref[...].astype(BF16)
        lane = lax.broadcasted_iota(jnp.int32, (ts, HEAD_SLOT), 1)
        pe = jnp.logical_and(lane >= QK_NOPE, lane < QK_NOPE + QK_ROPE)
        dkpe_ref[...] = rope_t(jnp.where(pe, gsum, 0.0)).astype(BF16)

    tab = pl.BlockSpec((ts, HEAD_SLOT), lambda i: (i, 0))
    return pl.pallas_call(
        body, name=name, grid=(S // ts,),
        in_specs=[pl.BlockSpec((ts, kw), lambda i: (i, 0)), pl.BlockSpec((ts, kw), lambda i: (i, 0)),
                  pl.BlockSpec((ts, vw), lambda i: (i, 0)), tab, tab, tab],
        out_specs=[pl.BlockSpec((ts, kw), lambda i: (i, 0)), pl.BlockSpec((ts, kw + vw), lambda i: (i, 0)), tab],
        out_shape=[jax.ShapeDtypeStruct((S, kw), BF16), jax.ShapeDtypeStruct((S, kw + vw), BF16),
                   jax.ShapeDtypeStruct((S, HEAD_SLOT), BF16)],
        compiler_params=_cp(("parallel",)),
    )(dq, dk, dv, *tabs)


def _attn_pairs(nb, kv_outer):
    if kv_outer:
        pr = [(i, j) for j in range(nb) for i in range(j, nb)]
    else:
        pr = [(i, j) for i in range(nb) for j in range(i + 1)]
    return (jnp.asarray(np.array([p[0] for p in pr], np.int32)), jnp.asarray(np.array([p[1] for p in pr], np.int32)))


def _attn_scores(q_ref, k_ref, hh, i, j, T):
    sl = slice(hh * HEAD_SLOT, (hh + 1) * HEAD_SLOT)
    s = _dot_tb(q_ref[:, sl], k_ref[:, sl])
    row = lax.broadcasted_iota(jnp.int32, (T, T), 0) // CHUNK
    col = lax.broadcasted_iota(jnp.int32, (T, T), 1) // CHUNK
    return jnp.where(col <= row + (i - j) * (T // CHUNK), s, NEG_BIG)


def _attn_fwd(q, k, kvraw, *, T, name):
    S = q.shape[0]
    NP = MLA_HEADS // 2
    nb = S // T
    ii, jj = _attn_pairs(nb, kv_outer=False)
    v_c0 = MLA_HEADS * HEAD_SLOT // LANES

    def body(ii_ref, jj_ref, q_ref, k_ref, v_ref, o_ref, lse_ref, m_sc, l_sc, acc_sc):
        t = pl.program_id(1)
        i, j = ii_ref[t], jj_ref[t]

        @pl.when(j == 0)
        def _():
            m_sc[...] = jnp.full_like(m_sc, NEG_BIG)
            l_sc[...] = jnp.zeros_like(l_sc)
            acc_sc[...] = jnp.zeros_like(acc_sc)

        lo = lax.broadcasted_iota(jnp.int32, (T, LANES), 1) < V_HEAD
        v = v_ref[...].astype(BF16)
        vh = (jnp.where(lo, v, jnp.zeros_like(v)), jnp.where(lo, jnp.zeros_like(v), v))
        alphas, pv = [], None
        for hh in range(2):
            s = _attn_scores(q_ref, k_ref, hh, i, j, T)
            m_prev = m_sc[hh]
            m_new = jnp.maximum(m_prev, jnp.max(s, axis=1, keepdims=True))
            p = jnp.exp(s - m_new)
            alpha = jnp.exp(m_prev - m_new)
            l_sc[hh] = alpha * l_sc[hh] + jnp.sum(p, axis=1, keepdims=True)
            m_sc[hh] = m_new
            alphas.append(alpha)
            t_pv = _dot(p.astype(BF16), vh[hh])
            pv = t_pv if pv is None else pv + t_pv
        acc_sc[...] = acc_sc[...] * jnp.where(lo, alphas[0], alphas[1]) + pv

        @pl.when(j == i)
        def _():
            l0, l1 = l_sc[0], l_sc[1]
            o_ref[...] = acc_sc[...] * jnp.where(lo, 1.0 / l0, 1.0 / l1)
            lse_ref[...] = jnp.where(lo, m_sc[0] + jnp.log(l0), m_sc[1] + jnp.log(l1))

    grid_spec = pltpu.PrefetchScalarGridSpec(
        num_scalar_prefetch=2, grid=(NP, int(ii.shape[0])),
        in_specs=[pl.BlockSpec((T, 2 * HEAD_SLOT), lambda p, t, ii, jj: (ii[t], p)),
                  pl.BlockSpec((T, 2 * HEAD_SLOT), lambda p, t, ii, jj: (jj[t], p)),
                  pl.BlockSpec((T, LANES), lambda p, t, ii, jj: (jj[t], v_c0 + p))],
        out_specs=[pl.BlockSpec((T, LANES), lambda p, t, ii, jj: (ii[t], p)),
                   pl.BlockSpec((None, T, LANES), lambda p, t, ii, jj: (p, ii[t], 0))],
        scratch_shapes=[pltpu.VMEM((2, T, 1), F32), pltpu.VMEM((2, T, 1), F32), pltpu.VMEM((T, LANES), F32)])
    return pl.pallas_call(
        body, name=name, grid_spec=grid_spec,
        out_shape=[jax.ShapeDtypeStruct((S, MLA_HEADS * V_HEAD), F32), jax.ShapeDtypeStruct((NP, S, LANES), F32)],
        compiler_params=_cp(("parallel", "arbitrary")),
    )(ii, jj, q, k, kvraw)


def _attn_bwd_common(q_ref, k_ref, v, do, o, lse, hh, i, j, T, lo):
    sel = lo if hh == 0 else jnp.logical_not(lo)
    s = _attn_scores(q_ref, k_ref, hh, i, j, T)
    p = jnp.exp(s - lse[:, hh * V_HEAD:hh * V_HEAD + 1])
    do_h = jnp.where(sel, do, 0.0)
    dsum = jnp.sum(do_h * o, axis=1, keepdims=True)
    do_hb = do_h.astype(BF16)
    dp = _dot_tb(do_hb, v)
    return p, p * (dp - dsum), do_hb


def _attn_bwd_dq(q, k, kvraw, do, o, lse, *, T, name):
    S = q.shape[0]
    NP = MLA_HEADS // 2
    nb = S // T
    ii, jj = _attn_pairs(nb, kv_outer=False)
    v_c0 = MLA_HEADS * HEAD_SLOT // LANES

    def body(ii_ref, jj_ref, q_ref, k_ref, v_ref, do_ref, o_ref, lse_ref, dq_ref, dq_sc):
        t = pl.program_id(1)
        i, j = ii_ref[t], jj_ref[t]

        @pl.when(j == 0)
        def _():
            dq_sc[...] = jnp.zeros_like(dq_sc)

        lo = lax.broadcasted_iota(jnp.int32, (T, LANES), 1) < V_HEAD
        v = v_ref[...].astype(BF16)
        do, o_v, lse_v = do_ref[...], o_ref[...], lse_ref[...]
        for hh in range(2):
            sl = slice(hh * HEAD_SLOT, (hh + 1) * HEAD_SLOT)
            _, ds, _ = _attn_bwd_common(q_ref, k_ref, v, do, o_v, lse_v, hh, i, j, T, lo)
            dq_sc[:, sl] += _dot(ds.astype(BF16), k_ref[:, sl])

        @pl.when(j == i)
        def _():
            dq_ref[...] = dq_sc[...]

    qi = lambda p, t, ii, jj: (ii[t], p)
    kj = lambda p, t, ii, jj: (jj[t], p)
    grid_spec = pltpu.PrefetchScalarGridSpec(
        num_scalar_prefetch=2, grid=(NP, int(ii.shape[0])),
        in_specs=[pl.BlockSpec((T, 2 * HEAD_SLOT), qi), pl.BlockSpec((T, 2 * HEAD_SLOT), kj),
                  pl.BlockSpec((T, LANES), lambda p, t, ii, jj: (jj[t], v_c0 + p)),
                  pl.BlockSpec((T, LANES), qi), pl.BlockSpec((T, LANES), qi),
                  pl.BlockSpec((None, T, LANES), lambda p, t, ii, jj: (p, ii[t], 0))],
        out_specs=pl.BlockSpec((T, 2 * HEAD_SLOT), qi),
        scratch_shapes=[pltpu.VMEM((T, 2 * HEAD_SLOT), F32)])
    return pl.pallas_call(
        body, name=name, grid_spec=grid_spec,
        out_shape=jax.ShapeDtypeStruct((S, MLA_HEADS * HEAD_SLOT), F32),
        compiler_params=_cp(("parallel", "arbitrary")),
    )(ii, jj, q, k, kvraw, do, o, lse)


def _attn_bwd_dkv(q, k, kvraw, do, o, lse, *, T, name):
    S = q.shape[0]
    NP = MLA_HEADS // 2
    nb = S // T
    ii, jj = _attn_pairs(nb, kv_outer=True)
    v_c0 = MLA_HEADS * HEAD_SLOT // LANES

    def body(ii_ref, jj_ref, q_ref, k_ref, v_ref, do_ref, o_ref, lse_ref, dk_ref, dv_ref, dk_sc, dv_sc):
        t = pl.program_id(1)
        i, j = ii_ref[t], jj_ref[t]

        @pl.when(i == j)
        def _():
            dk_sc[...] = jnp.zeros_like(dk_sc)
            dv_sc[...] = jnp.zeros_like(dv_sc)

        lo = lax.broadcasted_iota(jnp.int32, (T, LANES), 1) < V_HEAD
        v = v_ref[...].astype(BF16)
        do, o_v, lse_v = do_ref[...], o_ref[...], lse_ref[...]
        for hh in range(2):
            sl = slice(hh * HEAD_SLOT, (hh + 1) * HEAD_SLOT)
            p, ds, do_hb = _attn_bwd_common(q_ref, k_ref, v, do, o_v, lse_v, hh, i, j, T, lo)
            dv_sc[...] += _dot_ta(p.astype(BF16), do_hb)
            dk_sc[:, sl] += _dot_ta(ds.astype(BF16), q_ref[:, sl])

        @pl.when(i == nb - 1)
        def _():
            dk_ref[...] = dk_sc[...]
            dv_ref[...] = dv_sc[...]

    qi = lambda p, t, ii, jj: (ii[t], p)
    kj = lambda p, t, ii, jj: (jj[t], p)
    grid_spec = pltpu.PrefetchScalarGridSpec(
        num_scalar_prefetch=2, grid=(NP, int(ii.shape[0])),
        in_specs=[pl.BlockSpec((T, 2 * HEAD_SLOT), qi), pl.BlockSpec((T, 2 * HEAD_SLOT), kj),
                  pl.BlockSpec((T, LANES), lambda p, t, ii, jj: (jj[t], v_c0 + p)),
                  pl.BlockSpec((T, LANES), qi), pl.BlockSpec((T, LANES), qi),
                  pl.BlockSpec((None, T, LANES), lambda p, t, ii, jj: (p, ii[t], 0))],
        out_specs=[pl.BlockSpec((T, 2 * HEAD_SLOT), kj), pl.BlockSpec((T, LANES), kj)],
        scratch_shapes=[pltpu.VMEM((T, 2 * HEAD_SLOT), F32), pltpu.VMEM((T, LANES), F32)])
    return pl.pallas_call(
        body, name=name, grid_spec=grid_spec,
        out_shape=[jax.ShapeDtypeStruct((S, MLA_HEADS * HEAD_SLOT), F32),
                   jax.ShapeDtypeStruct((S, MLA_HEADS * V_HEAD), F32)],
        compiler_params=_cp(("parallel", "arbitrary")),
    )(ii, jj, q, k, kvraw, do, o, lse)


def _mla_permute_weights(w_in, w_uq, w_ukv):
    D = w_in.shape[0]
    H = MLA_HEADS
    qk = QK_NOPE + QK_ROPE
    lat = Q_LORA + KV_LORA
    kpe = jnp.zeros((D, HEAD_SLOT), w_in.dtype).at[:, QK_NOPE:qk].set(w_in[:, lat:])
    w_in_p = jnp.concatenate([w_in[:, :lat], kpe], axis=1)
    w_uq_p = jnp.pad(w_uq.reshape(Q_LORA, H, qk), ((0, 0), (0, 0), (0, HEAD_SLOT - qk))).reshape(Q_LORA, H * HEAD_SLOT)
    kv = w_ukv.reshape(KV_LORA, H, QK_NOPE + V_HEAD)
    wk = jnp.pad(kv[:, :, :QK_NOPE], ((0, 0), (0, 0), (0, HEAD_SLOT - QK_NOPE))).reshape(KV_LORA, H * HEAD_SLOT)
    wv = kv[:, :, QK_NOPE:].reshape(KV_LORA, H * V_HEAD)
    return w_in_p, w_uq_p, jnp.concatenate([wk, wv], axis=1)


def _mla_unpermute_grads(g_in_p, g_uq_p, g_ukv_p):
    H = MLA_HEADS
    qk = QK_NOPE + QK_ROPE
    lat = Q_LORA + KV_LORA
    g_in = jnp.concatenate([g_in_p[:, :lat], g_in_p[:, lat + QK_NOPE:lat + qk]], axis=1)
    g_uq = g_uq_p.reshape(Q_LORA, H, HEAD_SLOT)[:, :, :qk].reshape(Q_LORA, H * qk)
    gk = g_ukv_p[:, :H * HEAD_SLOT].reshape(KV_LORA, H, HEAD_SLOT)[:, :, :QK_NOPE]
    gv = g_ukv_p[:, H * HEAD_SLOT:].reshape(KV_LORA, H, V_HEAD)
    g_ukv = jnp.concatenate([gk, gv], axis=2).reshape(KV_LORA, H * (QK_NOPE + V_HEAD))
    return g_in, g_uq, g_ukv


def _mla_mixer_fwd(x, pos, p, tag):
    S, D = x.shape
    ts = _rt(S, 512)
    T = _rt(S, 512)
    lat = Q_LORA + KV_LORA
    tabs = _mla_tables(pos)
    proj = _mm(x, p["w_in_p"], out_dtype=F32, tm=ts, tn=p["w_in_p"].shape[1], name=tag + "_proj")
    qn = _rms_fwd(proj, p["q_norm"], c0=0, ts=ts, name=tag + "_qn")
    kvn = _rms_fwd(proj, p["kv_norm"], c0=Q_LORA, ts=ts, name=tag + "_kvn")
    qraw = _mm(qn, p["w_uq_p"], out_dtype=F32, tm=ts, tn=1024, name=tag + "_uq")
    kvraw = _mm(kvn, p["w_ukv_p"], out_dtype=F32, tm=ts, tn=1024, name=tag + "_ukv")
    q, k = _mla_prep_fwd(qraw, kvraw, proj, tabs, kpe_c0=lat, ts=ts, name=tag + "_prep")
    o, lse = _attn_fwd(q, k, kvraw, T=T, name=tag + "_attn")
    mix = _mm(o, p["w_out"], out_dtype=F32, tm=ts, tn=D, name=tag + "_out")
    return mix, (proj, qn, kvn, kvraw, q, k, o, lse, tabs)


def _mla_mixer_bwd(dmix, x, p, saved, tag):
    proj, qn, kvn, kvraw, q, k, o, lse, tabs = saved
    S, D = x.shape
    ts = _rt(S, 512)
    T = _rt(S, 512)
    lat = Q_LORA + KV_LORA
    g = {}
    do = _mm_tb([(dmix, 0)], p["w_out"], out_dtype=F32, tm=ts, tk=p["w_out"].shape[0], name=tag + "_do")
    g["w_out"] = _mm_ta(o, dmix, tk=p["w_out"].shape[0], tn=D, tm=ts, name=tag + "_dwout")
    dq = _attn_bwd_dq(q, k, kvraw, do, o, lse, T=T, name=tag + "_attn_dq")
    dk, dv = _attn_bwd_dkv(q, k, kvraw, do, o, lse, T=T, name=tag + "_attn_dkv")
    dqraw, dkvraw, dkpe = _mla_prep_bwd(dq, dk, dv, tabs, ts=_rt(S, 256), name=tag + "_prepb")
    dqn = _mm_tb([(dqraw, 0)], p["w_uq_p"], out_dtype=F32, tm=ts, tk=Q_LORA, name=tag + "_dqn")
    g_uq_p = _mm_ta(qn, dqraw, tk=Q_LORA, tn=1024, tm=ts, name=tag + "_dwuq")
    dkvn = _mm_tb([(dkvraw, 0)], p["w_ukv_p"], out_dtype=F32, tm=ts, tk=KV_LORA, name=tag + "_dkvn")
    g_ukv_p = _mm_ta(kvn, dkvraw, tk=KV_LORA, tn=1024, tm=ts, name=tag + "_dwukv")
    dcq, dqg8 = _rms_bwd(dqn, proj, p["q_norm"], c0=0, ts=ts, name=tag + "_qnb")
    dckv, dkvg8 = _rms_bwd(dkvn, proj, p["kv_norm"], c0=Q_LORA, ts=ts, name=tag + "_kvnb")
    g["q_norm"] = dqg8.sum(axis=0)
    g["kv_norm"] = dkvg8.sum(axis=0)
    dx = _mm_tb([(dcq, 0), (dckv, Q_LORA), (dkpe, lat)], p["w_in_p"], out_dtype=F32, tm=ts, tk=D, name=tag + "_dx")
    g_in_p = jnp.concatenate(
        [_mm_ta(x, dcq, tk=D, tn=Q_LORA, tm=ts, name=tag + "_dwin_q"),
         _mm_ta(x, dckv, tk=D, tn=KV_LORA, tm=ts, name=tag + "_dwin_kv"),
         _mm_ta(x, dkpe, tk=D, tn=HEAD_SLOT, tm=ts, name=tag + "_dwin_pe")], axis=1)
    g["w_in"], g["w_uq"], g["w_ukv"] = _mla_unpermute_grads(g_in_p, g_uq_p, g_ukv_p)
    return dx, g


RET_QK = 256
RET_V = 512


def _ret_tables(pos, T):
    half = RET_QK // 2
    inv_freq = ROPE_BASE ** (-jnp.arange(0, RET_QK, 2, dtype=F32) / RET_QK)
    ang = pos.astype(F32)[:, None] * inv_freq
    lg = jnp.log1p(-jnp.exp2(-5.0 - jnp.arange(RET_HEADS, dtype=F32)))
    idx = jnp.arange(T, dtype=F32)
    ch = jnp.arange(T) // CHUNK
    dm = jnp.where(ch[None, :] <= ch[:, None], jnp.exp(lg[:, None, None] * jnp.abs(idx[:, None] - idx[None, :])), 0.0)
    xi = jnp.broadcast_to(jnp.exp(lg[:, None] * (idx + 1.0))[:, :, None], (RET_HEADS, T, RET_QK))
    zeta = jnp.broadcast_to(jnp.exp(lg[:, None] * (T - 1.0 - idx))[:, :, None], (RET_HEADS, T, RET_QK))
    g_t = jnp.broadcast_to(jnp.exp(lg * T)[:, None, None], (RET_HEADS, 1, RET_V))
    assert half == LANES
    return jnp.cos(ang), jnp.sin(ang), dm.astype(F32), xi.astype(F32), zeta.astype(F32), g_t.astype(F32)


def _rope_half(x, c, s):
    x1, x2 = x[:, :LANES], x[:, LANES:]
    return jnp.concatenate([x1 * c - x2 * s, x1 * s + x2 * c], axis=1)


def _rope_half_t(g, c, s):
    g1, g2 = g[:, :LANES], g[:, LANES:]
    return jnp.concatenate([g1 * c + g2 * s, g2 * c - g1 * s], axis=1)


def _ret_qkv(q_ref, k_ref, v_ref, c_ref, s_ref):
    c, s = c_ref[...], s_ref[...]
    q = _rope_half(q_ref[...], c, s)
    k = _rope_half(k_ref[...], c, s) * (RET_QK ** -0.5)
    return q, k, v_ref[...].astype(BF16)


def _ret_in_specs(T, H, rev_nb=None):
    rb = (lambda n: n) if rev_nb is None else (lambda n: rev_nb - 1 - n)
    nq = H * RET_QK // RET_QK
    nv = 2 * H * RET_QK // RET_V
    return dict(
        q=pl.BlockSpec((T, RET_QK), lambda h, n: (rb(n), h)),
        k=pl.BlockSpec((T, RET_QK), lambda h, n: (rb(n), nq + h)),
        v=pl.BlockSpec((T, RET_V), lambda h, n: (rb(n), nv + h)),
        g=pl.BlockSpec((T, RET_V), lambda h, n: (rb(n), nv + H + h)),
        yv=pl.BlockSpec((T, RET_V), lambda h, n: (rb(n), h)),
        cs=pl.BlockSpec((T, LANES), lambda h, n: (rb(n), 0)),
        dm=pl.BlockSpec((None, T, T), lambda h, n: (h, 0, 0)),
        xz=pl.BlockSpec((None, T, RET_QK), lambda h, n: (h, 0, 0)),
        gt=pl.BlockSpec((None, 1, RET_V), lambda h, n: (h, 0, 0)),
        gn=pl.BlockSpec((1, RET_V), lambda h, n: (0, h)),
        st=pl.BlockSpec((None, None, RET_QK, RET_V), lambda h, n: (h, rb(n), 0, 0)),
    )


def _ret_fwd(proj, gn_g, tabs, *, T, name):
    S = proj.shape[0]
    H = RET_HEADS
    nb = S // T
    cos, sin, dm, xi, zeta, g_t = tabs
    sp = _ret_in_specs(T, H)

    def body(q_ref, k_ref, v_ref, g_ref, gn_ref, c_ref, s_ref, dm_ref, xi_ref, zeta_ref, gt_ref,
             o_ref, y_ref, st_ref, st):
        @pl.when(pl.program_id(1) == 0)
        def _():
            st[...] = jnp.zeros_like(st)

        q, k, vb = _ret_qkv(q_ref, k_ref, v_ref, c_ref, s_ref)
        qb, kb = q.astype(BF16), k.astype(BF16)
        s0 = st[...]
        s0b = s0.astype(BF16)
        st_ref[...] = s0b
        a = _dot_tb(qb, kb) * dm_ref[...]
        y = _dot(a.astype(BF16), vb) + _dot((q * xi_ref[...]).astype(BF16), s0b)
        st[...] = s0 * gt_ref[...] + _dot_ta((k * zeta_ref[...]).astype(BF16), vb)
        y_ref[...] = y
        mu = jnp.mean(y, axis=-1, keepdims=True)
        yc = y - mu
        var = jnp.mean(yc * yc, axis=-1, keepdims=True)
        gv = g_ref[...]
        o_ref[...] = (gv * _sigmoid(gv) * (yc * lax.rsqrt(var + LN_EPS) * gn_ref[...])).astype(BF16)

    return pl.pallas_call(
        body, name=name, grid=(H, nb),
        in_specs=[sp["q"], sp["k"], sp["v"], sp["g"], sp["gn"], sp["cs"], sp["cs"], sp["dm"], sp["xz"], sp["xz"], sp["gt"]],
        out_specs=[sp["yv"], sp["yv"], sp["st"]],
        out_shape=[jax.ShapeDtypeStruct((S, H * RET_V), BF16), jax.ShapeDtypeStruct((S, H * RET_V), F32),
                   jax.ShapeDtypeStruct((H, nb, RET_QK, RET_V), BF16)],
        scratch_shapes=[pltpu.VMEM((RET_QK, RET_V), F32)],
        compiler_params=_cp(("parallel", "arbitrary")),
    )(proj, proj, proj, proj, gn_g.reshape(1, H * RET_V), cos, sin, dm, xi, zeta, g_t)


def _ret_gn_bwd(dout, proj, y, gn_g, *, ts, name):
    S = proj.shape[0]
    H = RET_HEADS
    goff = 2 * H * RET_QK // RET_V + H

    def body(do_ref, g_ref, y_ref, gn_ref, dy_ref, dg_ref, dgn_ref):
        @pl.when(pl.program_id(1) == 0)
        def _():
            dgn_ref[...] = jnp.zeros_like(dgn_ref)
        y_v = y_ref[...]
        mu = jnp.mean(y_v, axis=-1, keepdims=True)
        yc = y_v - mu
        var = jnp.mean(yc * yc, axis=-1, keepdims=True)
        rstd = lax.rsqrt(var + LN_EPS)
        yhat = yc * rstd
        gv = g_ref[...]
        sg = _sigmoid(gv)
        dout = do_ref[...]
        gn = gn_ref[...]
        dg_ref[...] = (dout * (yhat * gn) * (sg * (1.0 + gv * (1.0 - sg)))).astype(BF16)
        dyn = dout * (gv * sg)
        dgn_ref[...] += _fold8(dyn * yhat)
        dyh = dyn * gn
        m1 = jnp.mean(dyh, axis=-1, keepdims=True)
        m2 = jnp.mean(dyh * yhat, axis=-1, keepdims=True)
        dy_ref[...] = (rstd * (dyh - m1 - yhat * m2)).astype(BF16)

    blk = pl.BlockSpec((ts, RET_V), lambda h, i: (i, h))
    return pl.pallas_call(
        body, name=name, grid=(H, S // ts),
        in_specs=[blk, pl.BlockSpec((ts, RET_V), lambda h, i: (i, goff + h)), blk,
                  pl.BlockSpec((1, RET_V), lambda h, i: (0, h))],
        out_specs=[blk, blk, pl.BlockSpec((SUBLANES, RET_V), lambda h, i: (0, h))],
        out_shape=[jax.ShapeDtypeStruct((S, H * RET_V), BF16), jax.ShapeDtypeStruct((S, H * RET_V), BF16),
                   jax.ShapeDtypeStruct((SUBLANES, H * RET_V), F32)],
        compiler_params=_cp(("parallel", "arbitrary")),
    )(dout, proj, y, gn_g.reshape(1, H * RET_V))


def _ret_bwd(proj, dy, states, tabs, *, T, name):
    S = proj.shape[0]
    H = RET_HEADS
    nb = S // T
    cos, sin, dm, xi, zeta, g_t = tabs
    sp = _ret_in_specs(T, H, rev_nb=nb)

    def body(q_ref, k_ref, v_ref, dy_ref, st_ref, c_ref, s_ref, dm_ref, xi_ref, zeta_ref, gt_ref,
             dq_ref, dk_ref, dv_ref, ds):
        @pl.when(pl.program_id(1) == 0)
        def _():
            ds[...] = jnp.zeros_like(ds)

        q, k, vb = _ret_qkv(q_ref, k_ref, v_ref, c_ref, s_ref)
        qb, kb = q.astype(BF16), k.astype(BF16)
        dyb = dy_ref[...]
        s0b = st_ref[...]
        dmv, xiv, zv = dm_ref[...], xi_ref[...], zeta_ref[...]
        ds_v = ds[...]
        dsb = ds_v.astype(BF16)
        gm = (_dot_tb(dyb, vb) * dmv).astype(BF16)
        ab = (_dot_tb(qb, kb) * dmv).astype(BF16)
        kz = (k * zv).astype(BF16)
        qx = (q * xiv).astype(BF16)
        dq = _dot(gm, kb) + xiv * _dot_tb(dyb, s0b)
        dk = _dot_ta(gm, qb) + zv * _dot_tb(vb, dsb)
        dv_ref[...] = (_dot_ta(ab, dyb) + _dot(kz, dsb)).astype(BF16)
        ds[...] = ds_v * gt_ref[...] + _dot_ta(qx, dyb)
        c, s = c_ref[...], s_ref[...]
        dq_ref[...] = _rope_half_t(dq, c, s).astype(BF16)
        dk_ref[...] = _rope_half_t(dk * (RET_QK ** -0.5), c, s).astype(BF16)

    qblk = pl.BlockSpec((T, RET_QK), lambda h, n: (nb - 1 - n, h))
    return pl.pallas_call(
        body, name=name, grid=(H, nb),
        in_specs=[sp["q"], sp["k"], sp["v"], sp["yv"], sp["st"], sp["cs"], sp["cs"], sp["dm"], sp["xz"], sp["xz"], sp["gt"]],
        out_specs=[qblk, qblk, sp["yv"]],
        out_shape=[jax.ShapeDtypeStruct((S, H * RET_QK), BF16), jax.ShapeDtypeStruct((S, H * RET_QK), BF16),
                   jax.ShapeDtypeStruct((S, H * RET_V), BF16)],
        scratch_shapes=[pltpu.VMEM((RET_QK, RET_V), F32)],
        compiler_params=_cp(("parallel", "arbitrary")),
    )(proj, proj, proj, dy, states, cos, sin, dm, xi, zeta, g_t)


def _ret_mixer_fwd(x, pos, p, tag):
    S, D = x.shape
    ts = _rt(S, 512)
    T = _rt(S, 256)
    tabs = _ret_tables(pos, T)
    proj = _mm(x, p["w_in"], out_dtype=F32, tm=ts, tn=1024, name=tag + "_proj")
    gated, y, states = _ret_fwd(proj, p["gn_g"], tabs, T=T, name=tag + "_ret")
    mix = _mm(gated, p["w_out"], out_dtype=F32, tm=ts, tn=D, name=tag + "_out")
    return mix, (proj, gated, y, states, tabs)


def _ret_mixer_bwd(dmix, x, p, saved, tag):
    proj, gated, y, states, tabs = saved
    S, D = x.shape
    ts = _rt(S, 512)
    T = _rt(S, 256)
    H = RET_HEADS
    hq, hv = H * RET_QK, H * RET_V
    g = {}
    dout = _mm_tb([(dmix, 0)], p["w_out"], out_dtype=F32, tm=ts, tk=1024, name=tag + "_dgated")
    g["w_out"] = _mm_ta(gated, dmix, tk=1024, tn=D, tm=ts, name=tag + "_dwout")
    dy, dgate, dgn8 = _ret_gn_bwd(dout, proj, y, p["gn_g"], ts=_rt(S, 256), name=tag + "_gnb")
    g["gn_g"] = dgn8.sum(axis=0)
    dq, dk, dv = _ret_bwd(proj, dy, states, tabs, T=T, name=tag + "_retb")
    dx = _mm_tb([(dq, 0), (dk, hq), (dv, 2 * hq), (dgate, 2 * hq + hv)], p["w_in"], out_dtype=F32,
                tm=ts, tk=512, name=tag + "_dx")
    g["w_in"] = jnp.concatenate(
        [_mm_ta(x, dq, tk=D, tn=1024, tm=ts, name=tag + "_dwin_q"),
         _mm_ta(x, dk, tk=D, tn=1024, tm=ts, name=tag + "_dwin_k"),
         _mm_ta(x, dv, tk=D, tn=1024, tm=ts, name=tag + "_dwin_v"),
         _mm_ta(x, dgate, tk=D, tn=1024, tm=ts, name=tag + "_dwin_g")], axis=1)
    return dx, g


PACK_W = 1024
ANY = pl.BlockSpec(memory_space=pl.ANY)
MESH = pl.DeviceIdType.MESH


def _coords():
    return lax.axis_index("x"), lax.axis_index("y"), lax.axis_index("c")


def _chip_peers(x, y):
    return [(1 - x, y), (x, 1 - y), (1 - x, 1 - y)]


def _gather_chips(packs, name):
    n = len(packs)

    def body(*refs):
        srcs, outs = refs[:n], refs[n:2 * n]
        send_sems, recv_sems, local_sems = refs[2 * n:]
        x, y, c = _coords()
        me = 2 * x + y
        local = [pltpu.make_async_copy(srcs[t], outs[t].at[me], local_sems.at[t]) for t in range(n)]
        for cp in local:
            cp.start()
        sends, recvs = [], []
        for k, (px, py) in enumerate(_chip_peers(x, y)):
            for t in range(n):
                sem = k * n + t
                sends.append(pltpu.make_async_remote_copy(
                    src_ref=srcs[t], dst_ref=outs[t].at[me], send_sem=send_sems.at[sem], recv_sem=recv_sems.at[sem],
                    device_id=(px, py, c), device_id_type=MESH))
                recvs.append(pltpu.make_async_remote_copy(
                    src_ref=srcs[t], dst_ref=outs[t].at[2 * px + py], send_sem=send_sems.at[sem],
                    recv_sem=recv_sems.at[sem], device_id=(px, py, c), device_id_type=MESH))
        for cp in sends:
            cp.start()
        for cp in recvs:
            cp.wait_recv()
        for cp in sends:
            cp.wait_send()
        for cp in local:
            cp.wait()

    return pl.pallas_call(
        body, name=name,
        in_specs=[ANY] * n, out_specs=[ANY] * n,
        out_shape=[jax.ShapeDtypeStruct((N_CHIPS,) + p.shape, p.dtype) for p in packs],
        scratch_shapes=[pltpu.SemaphoreType.DMA((3 * n,)), pltpu.SemaphoreType.DMA((3 * n,)),
                        pltpu.SemaphoreType.DMA((n,))],
    )(*packs)


def _scatter_chips(g, name):
    _, R, Wd = g.shape

    def body(g_ref, o_ref, send_sems, recv_sems):
        x, y, c = _coords()
        sends, recvs = [], []
        for k, (px, py) in enumerate(_chip_peers(x, y)):
            sends.append(pltpu.make_async_remote_copy(
                src_ref=g_ref.at[2 * px + py], dst_ref=o_ref.at[k], send_sem=send_sems.at[k],
                recv_sem=recv_sems.at[k], device_id=(px, py, c), device_id_type=MESH))
            recvs.append(sends[-1])
        for cp in sends:
            cp.start()
        for cp in recvs:
            cp.wait_recv()
        for cp in sends:
            cp.wait_send()

    return pl.pallas_call(
        body, name=name, in_specs=[ANY], out_specs=ANY,
        out_shape=jax.ShapeDtypeStruct((3, R, Wd), g.dtype),
        scratch_shapes=[pltpu.SemaphoreType.DMA((3,)), pltpu.SemaphoreType.DMA((3,))],
    )(g)


def _swap_sibling(p, name):
    def body(p_ref, o_ref, send_sem, recv_sem):
        x, y, c = _coords()
        cp = pltpu.make_async_remote_copy(src_ref=p_ref, dst_ref=o_ref, send_sem=send_sem, recv_sem=recv_sem,
                                          device_id=(x, y, 1 - c), device_id_type=MESH)
        cp.start()
        cp.wait_recv()
        cp.wait_send()

    return pl.pallas_call(
        body, name=name, in_specs=[ANY], out_specs=ANY,
        out_shape=jax.ShapeDtypeStruct(p.shape, p.dtype),
        scratch_shapes=[pltpu.SemaphoreType.DMA, pltpu.SemaphoreType.DMA],
    )(p)


def _allreduce_small(v, name):
    R, Wd = v.shape

    def body(v_ref, o_ref, buf, send_sems, recv_sems):
        x, y, c = _coords()
        o_ref[...] = v_ref[...]
        for st, peer in enumerate([(x, y, 1 - c), (x, 1 - y, c), (1 - x, y, c)]):
            cp = pltpu.make_async_remote_copy(src_ref=o_ref, dst_ref=buf.at[st], send_sem=send_sems.at[st],
                                              recv_sem=recv_sems.at[st], device_id=peer, device_id_type=MESH)
            cp.start()
            cp.wait_recv()
            cp.wait_send()
            o_ref[...] = o_ref[...] + buf[st]

    vm = pl.BlockSpec(memory_space=pltpu.VMEM)
    return pl.pallas_call(
        body, name=name, in_specs=[vm], out_specs=vm,
        out_shape=jax.ShapeDtypeStruct((R, Wd), F32),
        scratch_shapes=[pltpu.VMEM((3, R, Wd), F32), pltpu.SemaphoreType.DMA((3,)), pltpu.SemaphoreType.DMA((3,))],
    )(v)


def _sum_partials(g, recv, *, tr, name):
    _, R, Wd = g.shape
    me = (2 * lax.axis_index("x") + lax.axis_index("y")).astype(jnp.int32).reshape(1)

    def body(me_ref, g_ref, r_ref, o_ref):
        o_ref[...] = ((g_ref[...] + r_ref[0]) + r_ref[1]) + r_ref[2]

    grid_spec = pltpu.PrefetchScalarGridSpec(
        num_scalar_prefetch=1, grid=(R // tr,),
        in_specs=[pl.BlockSpec((None, tr, Wd), lambda i, me: (me[0], i, 0)),
                  pl.BlockSpec((3, tr, Wd), lambda i, me: (0, i, 0))],
        out_specs=pl.BlockSpec((tr, Wd), lambda i, me: (i, 0)))
    return pl.pallas_call(
        body, name=name, grid_spec=grid_spec, out_shape=jax.ShapeDtypeStruct((R, Wd), F32),
        compiler_params=_cp(("parallel",)),
    )(me, g, recv)


def _adamw(w, m, v, ga, gb, *, tr, name):
    R, Wd = w.shape
    two = gb is not None
    c1 = 1.0 / (1.0 - ADAM_B1 ** ADAM_STEP)
    c2 = 1.0 / (1.0 - ADAM_B2 ** ADAM_STEP)

    def body(*refs):
        if two:
            w_ref, m_ref, v_ref, ga_ref, gb_ref, g_ref, d_ref, mo_ref, vo_ref = refs
            g = ga_ref[...] + gb_ref[...]
        else:
            w_ref, m_ref, v_ref, ga_ref, g_ref, d_ref, mo_ref, vo_ref = refs
            g = ga_ref[...]
        m2 = ADAM_B1 * m_ref[...] + (1.0 - ADAM_B1) * g
        v2 = ADAM_B2 * v_ref[...] + (1.0 - ADAM_B2) * (g * g)
        g_ref[...] = g
        mo_ref[...] = m2
        vo_ref[...] = v2
        d_ref[...] = -ADAM_LR * ((m2 * c1) / (jnp.sqrt(v2 * c2) + ADAM_EPS) + ADAM_WD * w_ref[...])

    blk = pl.BlockSpec((tr, Wd), lambda i: (i, 0))
    args = [w, m, v, ga] + ([gb] if two else [])
    return pl.pallas_call(
        body, name=name, grid=(R // tr,), in_specs=[blk] * len(args), out_specs=[blk] * 4,
        out_shape=[jax.ShapeDtypeStruct((R, Wd), F32)] * 4, compiler_params=_cp(("parallel",)),
    )(*args)


SHARDED = [
    ("ffn_w_up", 2, True), ("ffn_conv_w", 2, False), ("ffn_w_down", 1, True),
    ("lru_w_in", 2, True), ("lru_conv_w", 2, False), ("lru_conv_b", 1, False),
    ("lru_w_a", 2, True), ("lru_b_a", 2, False), ("lru_w_x", 2, True), ("lru_b_x", 2, False),
    ("lru_lambda", 1, False), ("lru_w_out", 1, True),
    ("mla_w_in", 2, True), ("mla_w_uq", 2, True), ("mla_w_ukv", 2, True), ("mla_w_out", 1, True),
    ("ret_w_in", 2, True), ("ret_gn_g", 1, False), ("ret_w_out", 1, True),
]
REPLICATED = ["ln1_g", "ln1_b", "ln2_g", "ln2_b", "ffn_conv_b", "mla_q_norm", "mla_kv_norm"]
WEIGHTS = ["ln1_g", "ln1_b", "ln2_g", "ln2_b", "ffn_w_up", "ffn_conv_w", "ffn_conv_b", "ffn_w_down", "lru_w_in",
           "lru_conv_w", "lru_conv_b", "lru_w_a", "lru_b_a", "lru_w_x", "lru_b_x", "lru_lambda", "lru_w_out",
           "mla_w_in", "mla_q_norm", "mla_kv_norm", "mla_w_uq", "mla_w_ukv", "mla_w_out", "ret_w_in", "ret_gn_g",
           "ret_w_out"]
PACK_ROWS = 512


def _pack(arrs, dtype, lead=(), rows=PACK_ROWS):
    nl = len(lead)
    flat = jnp.concatenate([a.astype(dtype).reshape(lead + (-1,)) for a in arrs], axis=nl)
    n = flat.shape[nl]
    quantum = rows * PACK_W
    total = -(-n // quantum) * quantum
    flat = jnp.pad(flat, [(0, 0)] * nl + [(0, total - n)])
    return flat.reshape(lead + (total // PACK_W, PACK_W))


def _unpack(buf, shapes, lead=()):
    nl = len(lead)
    flat = buf.reshape(lead + (-1,))
    out, off = [], 0
    for shp in shapes:
        n = int(np.prod(shp))
        out.append(lax.slice_in_dim(flat, off, off + n, axis=nl).reshape(lead + tuple(shp)))
        off += n
    return out


def _layer_params(full, rep, i):
    kind, j = i % 3, i // 3
    ffn = dict(w_up=full["ffn_w_up"][i], conv_w=full["ffn_conv_w"][i], conv_b=rep["ffn_conv_b"][i],
               w_down=full["ffn_w_down"][i])
    if kind == 0:
        mix = dict(w_in=full["lru_w_in"][j], conv_w=full["lru_conv_w"][j], conv_b=full["lru_conv_b"][j],
                   w_a=full["lru_w_a"][j], b_a=full["lru_b_a"][j], w_x=full["lru_w_x"][j], b_x=full["lru_b_x"][j],
                   lam=full["lru_lambda"][j], w_out=full["lru_w_out"][j])
    elif kind == 1:
        w_in_p, w_uq_p, w_ukv_p = _mla_permute_weights(full["mla_w_in"][j], full["mla_w_uq"][j], full["mla_w_ukv"][j])
        mix = dict(w_in_p=w_in_p, w_uq_p=w_uq_p, w_ukv_p=w_ukv_p, q_norm=rep["mla_q_norm"][j],
                   kv_norm=rep["mla_kv_norm"][j], w_out=full["mla_w_out"][j])
    else:
        mix = dict(w_in=full["ret_w_in"][j], gn_g=full["ret_gn_g"][j], w_out=full["ret_w_out"][j])
    return kind, mix, ffn


_MIX_FWD = {0: lambda x, pos, p, tag: _lru_mixer_fwd(x, p, tag), 1: _mla_mixer_fwd, 2: _ret_mixer_fwd}
_MIX_BWD = {0: _lru_mixer_bwd, 1: _mla_mixer_bwd, 2: _ret_mixer_bwd}
_MIX_PREFIX = {0: "lru_", 1: "mla_", 2: "ret_"}
_MIX_KEYS = {0: {"w_in": "lru_w_in", "conv_w": "lru_conv_w", "conv_b": "lru_conv_b", "w_a": "lru_w_a", "b_a": "lru_b_a",
                 "w_x": "lru_w_x", "b_x": "lru_b_x", "lam": "lru_lambda", "w_out": "lru_w_out"},
             1: {"w_in": "mla_w_in", "q_norm": "mla_q_norm", "kv_norm": "mla_kv_norm", "w_uq": "mla_w_uq",
                 "w_ukv": "mla_w_ukv", "w_out": "mla_w_out"},
             2: {"w_in": "ret_w_in", "gn_g": "ret_gn_g", "w_out": "ret_w_out"}}
_FFN_KEYS = {"w_up": "ffn_w_up", "conv_w": "ffn_conv_w", "conv_b": "ffn_conv_b", "w_down": "ffn_w_down"}


def _local_step(x, pos, target, full, rep):
    S, D = x.shape
    ts = _rt(S, 256)
    acts = []
    h = x
    for i in range(DEPTH):
        kind, mp, fp = _layer_params(full, rep, i)
        tag = "l%d" % i
        mix, msaved = _MIX_FWD[kind](h, pos, mp, tag + "m")
        h1, z1 = _ln_fwd(h, mix, rep["ln1_g"][i], rep["ln1_b"][i], ts=ts, name=tag + "_ln1")
        f, fsaved = _ffn_fwd(h1, fp, tag + "f")
        h2, z2 = _ln_fwd(h1, f, rep["ln2_g"][i], rep["ln2_b"][i], ts=ts, name=tag + "_ln2")
        acts.append((kind, mp, fp, h, msaved, h1, z1, fsaved, z2))
        h = h2
    dy, part = _loss_head(h, target, ts=ts, name="loss_head")

    grads = {n: {} for n in WEIGHTS}
    d_a, d_b = dy, None
    for i in reversed(range(DEPTH)):
        kind, mp, fp, h_in, msaved, h1, z1, fsaved, z2 = acts[i]
        tag = "l%d" % i
        dz2, dg8, db8 = _ln_bwd(d_a, d_b, z2, rep["ln2_g"][i], ts=ts, name=tag + "_ln2b")
        grads["ln2_g"][i], grads["ln2_b"][i] = dg8.sum(axis=0), db8.sum(axis=0)
        dx_f, gf = _ffn_bwd(dz2, h1, fp, fsaved, tag + "f")
        for k, v in gf.items():
            grads[_FFN_KEYS[k]][i] = v
        dz1, dg8, db8 = _ln_bwd(dz2, dx_f, z1, rep["ln1_g"][i], ts=ts, name=tag + "_ln1b")
        grads["ln1_g"][i], grads["ln1_b"][i] = dg8.sum(axis=0), db8.sum(axis=0)
        dx_m, gm = _MIX_BWD[kind](dz1, h_in, mp, msaved, tag + "m")
        for k, v in gm.items():
            grads[_MIX_KEYS[kind][k]][i // 3] = v
        d_a, d_b = dz1, dx_m
    grad_x = _axpy(d_a, d_b, ts=ts, name="grad_x")
    stacked = {n: jnp.stack([grads[n][j] for j in sorted(grads[n])]) for n in WEIGHTS}
    return part, grad_x, stacked


def kernel(x, positions, ln1_g, ln1_b, ln2_g, ln2_b, ffn_w_up, ffn_conv_w, ffn_conv_b, ffn_w_down, lru_w_in, lru_conv_w, lru_conv_b, lru_w_a, lru_b_a, lru_w_x, lru_b_x, lru_lambda, lru_w_out, mla_w_in, mla_q_norm, mla_kv_norm, mla_w_uq, mla_w_ukv, mla_w_out, ret_w_in, ret_gn_g, ret_w_out, loss_target, m_ln1_g, m_ln1_b, m_ln2_g, m_ln2_b, m_ffn_w_up, m_ffn_conv_w, m_ffn_conv_b, m_ffn_w_down, m_lru_w_in, m_lru_conv_w, m_lru_conv_b, m_lru_w_a, m_lru_b_a, m_lru_w_x, m_lru_b_x, m_lru_lambda, m_lru_w_out, m_mla_w_in, m_mla_q_norm, m_mla_kv_norm, m_mla_w_uq, m_mla_w_ukv, m_mla_w_out, m_ret_w_in, m_ret_gn_g, m_ret_w_out, v_ln1_g, v_ln1_b, v_ln2_g, v_ln2_b, v_ffn_w_up, v_ffn_conv_w, v_ffn_conv_b, v_ffn_w_down, v_lru_w_in, v_lru_conv_w, v_lru_conv_b, v_lru_w_a, v_lru_b_a, v_lru_w_x, v_lru_b_x, v_lru_lambda, v_lru_w_out, v_mla_w_in, v_mla_q_norm, v_mla_kv_norm, v_mla_w_uq, v_mla_w_ukv, v_mla_w_out, v_ret_w_in, v_ret_gn_g, v_ret_w_out):
    w = dict(ln1_g=ln1_g, ln1_b=ln1_b, ln2_g=ln2_g, ln2_b=ln2_b, ffn_w_up=ffn_w_up, ffn_conv_w=ffn_conv_w, ffn_conv_b=ffn_conv_b, ffn_w_down=ffn_w_down, lru_w_in=lru_w_in, lru_conv_w=lru_conv_w, lru_conv_b=lru_conv_b, lru_w_a=lru_w_a, lru_b_a=lru_b_a, lru_w_x=lru_w_x, lru_b_x=lru_b_x, lru_lambda=lru_lambda, lru_w_out=lru_w_out, mla_w_in=mla_w_in, mla_q_norm=mla_q_norm, mla_kv_norm=mla_kv_norm, mla_w_uq=mla_w_uq, mla_w_ukv=mla_w_ukv, mla_w_out=mla_w_out, ret_w_in=ret_w_in, ret_gn_g=ret_gn_g, ret_w_out=ret_w_out)
    m = dict(ln1_g=m_ln1_g, ln1_b=m_ln1_b, ln2_g=m_ln2_g, ln2_b=m_ln2_b, ffn_w_up=m_ffn_w_up, ffn_conv_w=m_ffn_conv_w, ffn_conv_b=m_ffn_conv_b, ffn_w_down=m_ffn_w_down, lru_w_in=m_lru_w_in, lru_conv_w=m_lru_conv_w, lru_conv_b=m_lru_conv_b, lru_w_a=m_lru_w_a, lru_b_a=m_lru_b_a, lru_w_x=m_lru_w_x, lru_b_x=m_lru_b_x, lru_lambda=m_lru_lambda, lru_w_out=m_lru_w_out, mla_w_in=m_mla_w_in, mla_q_norm=m_mla_q_norm, mla_kv_norm=m_mla_kv_norm, mla_w_uq=m_mla_w_uq, mla_w_ukv=m_mla_w_ukv, mla_w_out=m_mla_w_out, ret_w_in=m_ret_w_in, ret_gn_g=m_ret_gn_g, ret_w_out=m_ret_w_out)
    v = dict(ln1_g=v_ln1_g, ln1_b=v_ln1_b, ln2_g=v_ln2_g, ln2_b=v_ln2_b, ffn_w_up=v_ffn_w_up, ffn_conv_w=v_ffn_conv_w, ffn_conv_b=v_ffn_conv_b, ffn_w_down=v_ffn_w_down, lru_w_in=v_lru_w_in, lru_conv_w=v_lru_conv_w, lru_conv_b=v_lru_conv_b, lru_w_a=v_lru_w_a, lru_b_a=v_lru_b_a, lru_w_x=v_lru_w_x, lru_b_x=v_lru_b_x, lru_lambda=v_lru_lambda, lru_w_out=v_lru_w_out, mla_w_in=v_mla_w_in, mla_q_norm=v_mla_q_norm, mla_kv_norm=v_mla_kv_norm, mla_w_uq=v_mla_w_uq, mla_w_ukv=v_mla_w_ukv, mla_w_out=v_mla_w_out, ret_w_in=v_ret_w_in, ret_gn_g=v_ret_gn_g, ret_w_out=v_ret_w_out)
    D = x.shape[-1]

    big = [n for n, _, mx in SHARDED if mx]
    small = [n for n, _, mx in SHARDED if not mx]
    g_big, g_small = _gather_chips([_pack([w[n] for n in big], BF16), _pack([w[n] for n in small], F32)], "gather_weights")
    axis_of = {n: ax for n, ax, _ in SHARDED}
    full = {}
    for names, buf in ((big, g_big), (small, g_small)):
        blocks = _unpack(buf, [w[n].shape for n in names], lead=(N_CHIPS,))
        for n, blk in zip(names, blocks):
            full[n] = jnp.concatenate([blk[s] for s in range(N_CHIPS)], axis=axis_of[n])
    rep = {n: w[n] for n in REPLICATED}

    part, grad_x, grads = _local_step(x[0], positions[0], loss_target[0], full, rep)
    loss = lax.psum((0.5 / D) * jnp.sum(part), MESH_AXES)

    names = [n for n, _, _ in SHARDED]
    g_pack = _pack([jnp.stack(jnp.split(grads[n], N_CHIPS, axis=axis_of[n])) for n in names], F32, lead=(N_CHIPS,))
    recv = _scatter_chips(g_pack, "scatter_grads")
    p_mine = _sum_partials(g_pack, recv, tr=PACK_ROWS, name="sum_chip_partials")
    p_sib = _swap_sibling(p_mine, "swap_core_partials")
    shapes = [w[n].shape for n in names]
    outs = _adamw(_pack([w[n] for n in names], F32), _pack([m[n] for n in names], F32),
                  _pack([v[n] for n in names], F32), p_mine, p_sib, tr=PACK_ROWS, name="adamw_sharded")
    res = {kind: dict(zip(names, _unpack(o, shapes))) for kind, o in zip(("g", "d", "m", "v"), outs)}

    r_shapes = [w[n].shape for n in REPLICATED]
    rpack = lambda d: _pack([d[n] for n in REPLICATED], F32, rows=SUBLANES)
    r_sum = _allreduce_small(rpack(grads), "allreduce_replicated")
    r_outs = _adamw(rpack(w), rpack(m), rpack(v), r_sum, None, tr=r_sum.shape[0], name="adamw_replicated")
    for kind, o in zip(("g", "d", "m", "v"), r_outs):
        res[kind].update(zip(REPLICATED, _unpack(o, r_shapes)))

    return (loss, grad_x[None], *[res["g"][n] for n in WEIGHTS], *[res["d"][n] for n in WEIGHTS],
            *[res["m"][n] for n in WEIGHTS], *[res["v"][n] for n in WEIGHTS])
```

```python
import functools
import math

import numpy as np
import jax
import jax.numpy as jnp
from jax import lax
from jax.experimental import pallas as pl
from jax.experimental.pallas import tpu as pltpu

F32 = jnp.float32
BF16 = jnp.bfloat16

DEPTH = 4
ALPHA = (2.0 * DEPTH) ** 0.25
LN_EPS = 1e-5
RMS_EPS = 1e-6
ROPE_BASE = 10000.0
CHUNK = 64
LRU_C = 8.0
LRU_GROUPS = 4
MLA_HEADS = 16
QK_NOPE, QK_ROPE, V_HEAD = 64, 32, 64
Q_LORA, KV_LORA = 768, 256
RET_HEADS = 4
ADAM_LR, ADAM_B1, ADAM_B2, ADAM_EPS, ADAM_WD, ADAM_STEP = 0.001, 0.9, 0.999, 1e-08, 0.01, 10

LANES = 128
SUBLANES = 8
VMEM_LIMIT = 56 * 1024 * 1024

MESH_AXES = ("x", "y", "c")
N_CHIPS = 4


def _cp(sem):
    return pltpu.CompilerParams(dimension_semantics=sem, vmem_limit_bytes=VMEM_LIMIT)


def _sigmoid(x):
    return 1.0 / (1.0 + jnp.exp(-x))


_GELU_C = math.sqrt(2.0 / math.pi)


def _gelu_parts(x):
    x2 = x * x
    u = _GELU_C * (x + 0.044715 * x * x2)
    t = jnp.tanh(u)
    g = 0.5 * x * (1.0 + t)
    dg = 0.5 * (1.0 + t) + 0.5 * x * (1.0 - t * t) * _GELU_C * (1.0 + 3.0 * 0.044715 * x2)
    return g, dg


def _fold8(v):
    n = v.shape[0] // SUBLANES
    return v.reshape(n, SUBLANES, v.shape[1]).sum(axis=0)


def _dot(a, b):
    return jnp.dot(a, b, preferred_element_type=F32)


def _dot_tb(a, b):
    return lax.dot_general(a, b, (((1,), (1,)), ((), ())), preferred_element_type=F32)


def _dot_ta(a, b):
    return lax.dot_general(a, b, (((0,), (0,)), ((), ())), preferred_element_type=F32)


def _mm(a, b, *, out_dtype, tm, tn, name, a_koff=0):
    M = a.shape[0]
    K, N = b.shape

    def body(a_ref, b_ref, o_ref):
        o_ref[...] = _dot(a_ref[...].astype(BF16), b_ref[...].astype(BF16)).astype(out_dtype)

    return pl.pallas_call(
        body, name=name, grid=(M // tm, N // tn),
        in_specs=[pl.BlockSpec((tm, K), lambda i, j: (i, a_koff)),
                  pl.BlockSpec((K, tn), lambda i, j: (0, j))],
        out_specs=pl.BlockSpec((tm, tn), lambda i, j: (i, j)),
        out_shape=jax.ShapeDtypeStruct((M, N), out_dtype),
        compiler_params=_cp(("parallel", "parallel")),
    )(a, b)


def _mm_tb(pairs, b, *, out_dtype, tm, tk, name):
    M = pairs[0][0].shape[0]
    Kout = b.shape[0]
    n = len(pairs)

    def body(*refs):
        a_refs, b_refs, o_ref = refs[:n], refs[n:2 * n], refs[2 * n]
        acc = None
        for a_ref, b_ref in zip(a_refs, b_refs):
            t = _dot_tb(a_ref[...].astype(BF16), b_ref[...].astype(BF16))
            acc = t if acc is None else acc + t
        o_ref[...] = acc.astype(out_dtype)

    in_specs = [pl.BlockSpec((tm, a.shape[1]), lambda i, j: (i, 0)) for a, _ in pairs]
    for a, c0 in pairs:
        w = a.shape[1]
        assert c0 % w == 0
        in_specs.append(pl.BlockSpec((tk, w), functools.partial(lambda i, j, cb: (j, cb), cb=c0 // w)))
    return pl.pallas_call(
        body, name=name, grid=(M // tm, Kout // tk),
        in_specs=in_specs,
        out_specs=pl.BlockSpec((tm, tk), lambda i, j: (i, j)),
        out_shape=jax.ShapeDtypeStruct((M, Kout), out_dtype),
        compiler_params=_cp(("parallel", "parallel")),
    )(*[a for a, _ in pairs], *[b for _ in pairs])


def _mm_ta(a, b, *, tk, tn, tm, name, a_c0=0, a_w=None, b_c0=0, b_w=None):
    M = a.shape[0]
    a_w = a.shape[1] if a_w is None else a_w
    b_w = b.shape[1] if b_w is None else b_w
    assert a_c0 % tk == 0 and b_c0 % tn == 0 and a_w % tk == 0 and b_w % tn == 0

    def body(a_ref, b_ref, o_ref):
        @pl.when(pl.program_id(2) == 0)
        def _():
            o_ref[...] = jnp.zeros_like(o_ref)
        o_ref[...] += _dot_ta(a_ref[...].astype(BF16), b_ref[...].astype(BF16))

    return pl.pallas_call(
        body, name=name, grid=(a_w // tk, b_w // tn, M // tm),
        in_specs=[pl.BlockSpec((tm, tk), lambda i, j, m: (m, i + a_c0 // tk)),
                  pl.BlockSpec((tm, tn), lambda i, j, m: (m, j + b_c0 // tn))],
        out_specs=pl.BlockSpec((tk, tn), lambda i, j, m: (i, j)),
        out_shape=jax.ShapeDtypeStruct((a_w, b_w), F32),
        compiler_params=_cp(("parallel", "parallel", "arbitrary")),
    )(a, b)


def _ln_fwd(x, mix, g, b, *, ts, name):
    S, D = x.shape

    def body(x_ref, m_ref, g_ref, b_ref, o_ref, z_ref):
        z = ALPHA * x_ref[...] + m_ref[...]
        mu = jnp.mean(z, axis=-1, keepdims=True)
        zc = z - mu
        var = jnp.mean(zc * zc, axis=-1, keepdims=True)
        o_ref[...] = zc * lax.rsqrt(var + LN_EPS) * g_ref[...] + b_ref[...]
        z_ref[...] = z

    row = pl.BlockSpec((ts, D), lambda i: (i, 0))
    vec = pl.BlockSpec((1, D), lambda i: (0, 0))
    return pl.pallas_call(
        body, name=name, grid=(S // ts,),
        in_specs=[row, row, vec, vec], out_specs=[row, row],
        out_shape=[jax.ShapeDtypeStruct((S, D), F32)] * 2,
        compiler_params=_cp(("parallel",)),
    )(x, mix, g.reshape(1, D), b.reshape(1, D))


def _ln_bwd(da, db, z, g, *, ts, name):
    S, D = z.shape
    two = db is not None

    def body(*refs):
        if two:
            da_ref, db_ref, z_ref, g_ref, dz_ref, dg_ref, dbias_ref = refs
            dout = ALPHA * da_ref[...] + db_ref[...]
        else:
            da_ref, z_ref, g_ref, dz_ref, dg_ref, dbias_ref = refs
            dout = da_ref[...]

        @pl.when(pl.program_id(0) == 0)
        def _():
            dg_ref[...] = jnp.zeros_like(dg_ref)
            dbias_ref[...] = jnp.zeros_like(dbias_ref)

        z = z_ref[...]
        mu = jnp.mean(z, axis=-1, keepdims=True)
        zc = z - mu
        var = jnp.mean(zc * zc, axis=-1, keepdims=True)
        rstd = lax.rsqrt(var + LN_EPS)
        xhat = zc * rstd
        dxh = dout * g_ref[...]
        m1 = jnp.mean(dxh, axis=-1, keepdims=True)
        m2 = jnp.mean(dxh * xhat, axis=-1, keepdims=True)
        dz_ref[...] = rstd * (dxh - m1 - xhat * m2)
        dg_ref[...] += _fold8(dout * xhat)
        dbias_ref[...] += _fold8(dout)

    row = pl.BlockSpec((ts, D), lambda i: (i, 0))
    vec = pl.BlockSpec((1, D), lambda i: (0, 0))
    acc = pl.BlockSpec((SUBLANES, D), lambda i: (0, 0))
    args = [da, db, z, g.reshape(1, D)] if two else [da, z, g.reshape(1, D)]
    return pl.pallas_call(
        body, name=name, grid=(S // ts,),
        in_specs=[row] * (3 if two else 2) + [vec],
        out_specs=[row, acc, acc],
        out_shape=[jax.ShapeDtypeStruct((S, D), F32), jax.ShapeDtypeStruct((SUBLANES, D), F32),
                   jax.ShapeDtypeStruct((SUBLANES, D), F32)],
        compiler_params=_cp(("arbitrary",)),
    )(*args)


def _loss_head(y, t, *, ts, name):
    S, D = y.shape

    def body(y_ref, t_ref, dy_ref, p_ref):
        @pl.when(pl.program_id(0) == 0)
        def _():
            p_ref[...] = jnp.zeros_like(p_ref)
        d = y_ref[...] - t_ref[...]
        dy_ref[...] = d * (1.0 / D)
        p_ref[...] += _fold8(d * d)

    row = pl.BlockSpec((ts, D), lambda i: (i, 0))
    acc = pl.BlockSpec((SUBLANES, D), lambda i: (0, 0))
    return pl.pallas_call(
        body, name=name, grid=(S // ts,),
        in_specs=[row, row], out_specs=[row, acc],
        out_shape=[jax.ShapeDtypeStruct((S, D), F32), jax.ShapeDtypeStruct((SUBLANES, D), F32)],
        compiler_params=_cp(("arbitrary",)),
    )(y, t)


def _axpy(a, b, *, ts, name):
    S, D = a.shape

    def body(a_ref, b_ref, o_ref):
        o_ref[...] = ALPHA * a_ref[...] + b_ref[...]

    row = pl.BlockSpec((ts, D), lambda i: (i, 0))
    return pl.pallas_call(
        body, name=name, grid=(S // ts,), in_specs=[row, row], out_specs=row,
        out_shape=jax.ShapeDtypeStruct((S, D), F32), compiler_params=_cp(("parallel",)),
    )(a, b)


def _prev_halo_spec(ts, tc, coff):
    r = ts // SUBLANES
    return pl.BlockSpec((SUBLANES, tc), lambda i, j: (jnp.maximum(i * r - 1, 0), j + coff))


def _fill_prev(buf, halo_ref, cur, i):
    buf[0:SUBLANES, :] = jnp.where(i > 0, halo_ref[...], 0.0)
    buf[SUBLANES:, :] = cur


def _conv_fwd(x, w, b, *, K, ts, tc, x_c0, name):
    S = x.shape[0]
    C = w.shape[1]
    coff = x_c0 // tc

    def body(x_ref, halo_ref, w_ref, b_ref, o_ref, buf):
        _fill_prev(buf, halo_ref, x_ref[...], pl.program_id(0))
        acc = b_ref[...] + w_ref[K - 1:K, :] * x_ref[...]
        for k in range(K - 1):
            acc = acc + w_ref[k:k + 1, :] * buf[pl.ds(SUBLANES - (K - 1) + k, ts), :]
        o_ref[...] = acc

    return pl.pallas_call(
        body, name=name, grid=(S // ts, C // tc),
        in_specs=[pl.BlockSpec((ts, tc), lambda i, j: (i, j + coff)), _prev_halo_spec(ts, tc, coff),
                  pl.BlockSpec((K, tc), lambda i, j: (0, j)), pl.BlockSpec((1, tc), lambda i, j: (0, j))],
        out_specs=pl.BlockSpec((ts, tc), lambda i, j: (i, j)),
        out_shape=jax.ShapeDtypeStruct((S, C), F32),
        scratch_shapes=[pltpu.VMEM((ts + SUBLANES, tc), F32)],
        compiler_params=_cp(("parallel", "parallel")),
    )(x, x, w, b.reshape(1, C))


def _conv_wgrad(dy, x, *, K, ts, tc, x_c0, name):
    S, C = dy.shape
    coff = x_c0 // tc

    def body(dy_ref, x_ref, halo_ref, dw_ref, db_ref, buf):
        i = pl.program_id(1)

        @pl.when(i == 0)
        def _():
            dw_ref[...] = jnp.zeros_like(dw_ref)
            db_ref[...] = jnp.zeros_like(db_ref)

        _fill_prev(buf, halo_ref, x_ref[...], i)
        dy_v = dy_ref[...]
        db_ref[...] += _fold8(dy_v)
        for k in range(K):
            xs = buf[pl.ds(SUBLANES - (K - 1) + k, ts), :]
            dw_ref[k] += _fold8(dy_v * xs)

    r = ts // SUBLANES
    return pl.pallas_call(
        body, name=name, grid=(C // tc, S // ts),
        in_specs=[pl.BlockSpec((ts, tc), lambda j, i: (i, j)),
                  pl.BlockSpec((ts, tc), lambda j, i: (i, j + coff)),
                  pl.BlockSpec((SUBLANES, tc), lambda j, i: (jnp.maximum(i * r - 1, 0), j + coff))],
        out_specs=[pl.BlockSpec((K, SUBLANES, tc), lambda j, i: (0, 0, j)),
                   pl.BlockSpec((SUBLANES, tc), lambda j, i: (0, j))],
        out_shape=[jax.ShapeDtypeStruct((K, SUBLANES, C), F32), jax.ShapeDtypeStruct((SUBLANES, C), F32)],
        scratch_shapes=[pltpu.VMEM((ts + SUBLANES, tc), F32)],
        compiler_params=_cp(("parallel", "arbitrary")),
    )(dy, x, x)


def _conv_bwd(dy, w, *, K, ts, tc, w_c0, out_dtype, name):
    S, C = dy.shape
    nb = S // ts
    r = ts // SUBLANES
    woff = w_c0 // tc

    def body(dy_ref, halo_ref, w_ref, o_ref, buf):
        i = pl.program_id(0)
        buf[0:ts, :] = dy_ref[...]
        buf[ts:, :] = jnp.where(i < nb - 1, halo_ref[...], 0.0)
        acc = w_ref[K - 1:K, :] * dy_ref[...]
        for k in range(K - 1):
            acc = acc + w_ref[k:k + 1, :] * buf[pl.ds(K - 1 - k, ts), :]
        o_ref[...] = acc.astype(out_dtype)

    return pl.pallas_call(
        body, name=name, grid=(nb, C // tc),
        in_specs=[pl.BlockSpec((ts, tc), lambda i, j: (i, j)),
                  pl.BlockSpec((SUBLANES, tc), lambda i, j: (jnp.minimum((i + 1) * r, nb * r - 1), j)),
                  pl.BlockSpec((K, tc), lambda i, j: (0, j + woff))],
        out_specs=pl.BlockSpec((ts, tc), lambda i, j: (i, j)),
        out_shape=jax.ShapeDtypeStruct((S, C), out_dtype),
        scratch_shapes=[pltpu.VMEM((ts + SUBLANES, tc), F32)],
        compiler_params=_cp(("parallel", "parallel")),
    )(dy, dy, w)


HALO16 = 16


def _ffn_conv(buf, w_ref, b_ref, n):
    taps = [buf[pl.ds(HALO16 - 2 + k, n), :] for k in range(3)]
    acc = b_ref[...] + w_ref[0:1, :] * taps[0]
    for k in (1, 2):
        acc = acc + w_ref[k:k + 1, :] * taps[k]
    return acc, taps


def _ffn_mid_specs(ts, tc, nf, nb, with_next):
    r = ts // HALO16
    specs = []
    for off in (0, nf):
        specs.append(pl.BlockSpec((ts, tc), functools.partial(lambda j, i, o: (i, j + o), o=off)))
        specs.append(pl.BlockSpec((HALO16, tc), functools.partial(lambda j, i, o: (jnp.maximum(i * r - 1, 0), j + o), o=off)))
        if with_next:
            specs.append(pl.BlockSpec(
                (HALO16, tc), functools.partial(lambda j, i, o: (jnp.minimum((i + 1) * r, nb * r - 1), j + o), o=off)))
    for rows in (3, 1):
        for off in (0, nf):
            specs.append(pl.BlockSpec((rows, tc), functools.partial(lambda j, i, o: (0, j + o), o=off)))
    return specs


def _ffn_mid_fwd(hpre, w, b, *, ts, tc, name):
    S, F2 = hpre.shape
    F = F2 // 2
    nf = F // tc

    def body(g_ref, gp_ref, u_ref, up_ref, wg_ref, wu_ref, bg_ref, bu_ref, o_ref, gbuf, ubuf):
        first = pl.program_id(1) == 0
        for buf, prev, cur in ((gbuf, gp_ref, g_ref), (ubuf, up_ref, u_ref)):
            buf[0:HALO16, :] = jnp.where(first, 0.0, prev[...].astype(F32))
            buf[HALO16:, :] = cur[...].astype(F32)
        gel, _ = _gelu_parts(_ffn_conv(gbuf, wg_ref, bg_ref, ts)[0])
        o_ref[...] = (gel * _ffn_conv(ubuf, wu_ref, bu_ref, ts)[0]).astype(BF16)

    b2 = b.reshape(1, F2)
    return pl.pallas_call(
        body, name=name, grid=(nf, S // ts),
        in_specs=_ffn_mid_specs(ts, tc, nf, S // ts, False),
        out_specs=pl.BlockSpec((ts, tc), lambda j, i: (i, j)),
        out_shape=jax.ShapeDtypeStruct((S, F), BF16),
        scratch_shapes=[pltpu.VMEM((ts + HALO16, tc), F32)] * 2,
        compiler_params=_cp(("parallel", "parallel")),
    )(hpre, hpre, hpre, hpre, w, w, b2, b2)


def _ffn_mid_bwd(hpre, da, w, b, *, ts, tc, name):
    S, F2 = hpre.shape
    F = F2 // 2
    nf = F // tc
    nb = S // ts
    r = ts // HALO16
    ne = ts + HALO16

    def body(g_ref, gp_ref, gn_ref, u_ref, up_ref, un_ref, wg_ref, wu_ref, bg_ref, bu_ref, da_ref, dan_ref,
             dpg_ref, dpu_ref, dwg_ref, dwu_ref, dbg_ref, dbu_ref, gbuf, ubuf, dabuf, dgbuf, dubuf):
        i = pl.program_id(1)

        @pl.when(i == 0)
        def _():
            for ref in (dwg_ref, dwu_ref, dbg_ref, dbu_ref):
                ref[...] = jnp.zeros_like(ref)

        for buf, prev, cur, nxt in ((gbuf, gp_ref, g_ref, gn_ref), (ubuf, up_ref, u_ref, un_ref)):
            buf[0:HALO16, :] = jnp.where(i == 0, 0.0, prev[...].astype(F32))
            buf[HALO16:HALO16 + ts, :] = cur[...].astype(F32)
            buf[HALO16 + ts:, :] = nxt[...].astype(F32)
        dabuf[0:ts, :] = da_ref[...].astype(F32)
        dabuf[ts:, :] = jnp.where(i == nb - 1, 0.0, dan_ref[...].astype(F32))

        gate, gtaps = _ffn_conv(gbuf, wg_ref, bg_ref, ne)
        up, utaps = _ffn_conv(ubuf, wu_ref, bu_ref, ne)
        gel, dgel = _gelu_parts(gate)
        da_v = dabuf[...]
        dgbuf[...] = da_v * up * dgel
        dubuf[...] = da_v * gel
        for dbuf, taps, w_ref, dp_ref, dw_ref, db_ref in ((dgbuf, gtaps, wg_ref, dpg_ref, dwg_ref, dbg_ref),
                                                          (dubuf, utaps, wu_ref, dpu_ref, dwu_ref, dbu_ref)):
            dh = dbuf[0:ts, :]
            db_ref[...] += _fold8(dh)
            acc = w_ref[2:3, :] * dh
            for k in range(3):
                dw_ref[k] += _fold8(dh * taps[k][0:ts, :])
                if k < 2:
                    acc = acc + w_ref[k:k + 1, :] * dbuf[pl.ds(2 - k, ts), :]
            dp_ref[...] = acc.astype(BF16)

    b2 = b.reshape(1, F2)
    blk = pl.BlockSpec((ts, tc), lambda j, i: (i, j))
    nxt = pl.BlockSpec((HALO16, tc), lambda j, i: (jnp.minimum((i + 1) * r, nb * r - 1), j))
    in_specs = _ffn_mid_specs(ts, tc, nf, nb, True) + [blk, nxt]
    out_specs = [blk, blk,
                 pl.BlockSpec((3, SUBLANES, tc), lambda j, i: (0, 0, j)), pl.BlockSpec((3, SUBLANES, tc), lambda j, i: (0, 0, j)),
                 pl.BlockSpec((SUBLANES, tc), lambda j, i: (0, j)), pl.BlockSpec((SUBLANES, tc), lambda j, i: (0, j))]
    return pl.pallas_call(
        body, name=name, grid=(nf, nb),
        in_specs=in_specs, out_specs=out_specs,
        out_shape=[jax.ShapeDtypeStruct((S, F), BF16)] * 2 + [jax.ShapeDtypeStruct((3, SUBLANES, F), F32)] * 2
                  + [jax.ShapeDtypeStruct((SUBLANES, F), F32)] * 2,
        scratch_shapes=[pltpu.VMEM((ts + 2 * HALO16, tc), F32)] * 2 + [pltpu.VMEM((ne, tc), F32)] * 3,
        compiler_params=_cp(("parallel", "arbitrary")),
    )(hpre, hpre, hpre, hpre, hpre, hpre, w, w, b2, b2, da, da)


def _expm1(x):
    u = jnp.exp(x)
    um1 = u - 1.0
    safe = jnp.where(um1 == 0.0, 1.0, jnp.log(u))
    r = jnp.where(um1 == 0.0, x, um1 * x / safe)
    return jnp.where(x < -30.0, -1.0, r)


def _softplus(z):
    return jnp.maximum(z, 0.0) + jnp.log1p(jnp.exp(-jnp.abs(z)))


def _lru_gates(u, wa_ref, ba_ref, wx_ref, bx_ref, lam_ref):
    ub = u.astype(BF16)
    r = _sigmoid(_dot(ub, wa_ref[...]) + ba_ref[...])
    ig = _sigmoid(_dot(ub, wx_ref[...]) + bx_ref[...])
    sp = _softplus(-lam_ref[...])
    la = -LRU_C * r * sp
    a = jnp.exp(la)
    mult = jnp.sqrt(-_expm1(2.0 * la))
    return ub, r, ig, sp, a, mult


def _lru_specs(ts, gw):
    blk = pl.BlockSpec((ts, gw), lambda g, i: (i, g))
    wsp = pl.BlockSpec((None, gw, gw), lambda g, i: (g, 0, 0))
    vsp = pl.BlockSpec((None, 1, gw), lambda g, i: (g, 0, 0))
    return blk, wsp, vsp


def _lru_fwd(u, proj, w_a, b_a, w_x, b_x, lam, *, ts, name):
    S, W = u.shape
    G = LRU_GROUPS
    gw = W // G
    nt = ts // SUBLANES

    def body(u_ref, gb_ref, wa_ref, ba_ref, wx_ref, bx_ref, lam_ref, y_ref, h_ref, a_buf, b_buf, carry):
        @pl.when(pl.program_id(1) == 0)
        def _():
            carry[...] = jnp.zeros_like(carry)

        u_v = u_ref[...]
        _, _, ig, _, a, mult = _lru_gates(u_v, wa_ref, ba_ref, wx_ref, bx_ref, lam_ref)
        a_buf[...] = a
        b_buf[...] = mult * ig * u_v
        row = lax.broadcasted_iota(jnp.int32, (SUBLANES, gw), 0)

        def tile(k, c):
            r0 = pl.multiple_of(k * SUBLANES, SUBLANES)
            A = a_buf[pl.ds(r0, SUBLANES), :]
            B = b_buf[pl.ds(r0, SUBLANES), :]
            for d in (1, 2, 4):
                m = row >= d
                B = jnp.where(m, A * pltpu.roll(B, d, 0) + B, B)
                A = jnp.where(m, A * pltpu.roll(A, d, 0), A)
            h = A * c + B
            h_ref[pl.ds(r0, SUBLANES), :] = h
            return h[SUBLANES - 1:SUBLANES, :]

        carry[...] = lax.fori_loop(0, nt, tile, carry[...])
        gel, _ = _gelu_parts(gb_ref[...])
        y_ref[...] = (gel * h_ref[...]).astype(BF16)

    blk, wsp, vsp = _lru_specs(ts, gw)
    return pl.pallas_call(
        body, name=name, grid=(G, S // ts),
        in_specs=[blk, blk, wsp, vsp, wsp, vsp, vsp],
        out_specs=[blk, blk],
        out_shape=[jax.ShapeDtypeStruct((S, W), BF16), jax.ShapeDtypeStruct((S, W), F32)],
        scratch_shapes=[pltpu.VMEM((ts, gw), F32), pltpu.VMEM((ts, gw), F32), pltpu.VMEM((1, gw), F32)],
        compiler_params=_cp(("parallel", "arbitrary")),
    )(u, proj, w_a, b_a.reshape(G, 1, gw), w_x, b_x.reshape(G, 1, gw), lam.reshape(G, 1, gw))


def _lru_bwd(dy, u, proj, h, w_a, b_a, w_x, b_x, lam, *, ts, name):
    S, W = u.shape
    G = LRU_GROUPS
    gw = W // G
    nt = ts // SUBLANES
    nb = S // ts
    r8 = ts // SUBLANES

    def body(dy_ref, u_ref, gb_ref, h_ref, hh_ref, wa_ref, ba_ref, wx_ref, bx_ref, lam_ref,
             dgb_ref, du_ref, dwa_ref, dwx_ref, dba_ref, dbx_ref, dlam_ref,
             a_buf, q_buf, p_buf, hbuf, carry):
        i = pl.program_id(1)
        ib = nb - 1 - i

        @pl.when(i == 0)
        def _():
            carry[...] = jnp.zeros_like(carry)
            dwa_ref[...] = jnp.zeros_like(dwa_ref)
            dwx_ref[...] = jnp.zeros_like(dwx_ref)
            dba_ref[...] = jnp.zeros_like(dba_ref)
            dbx_ref[...] = jnp.zeros_like(dbx_ref)
            dlam_ref[...] = jnp.zeros_like(dlam_ref)

        u_v = u_ref[...]
        ub, r, ig, sp, a, mult = _lru_gates(u_v, wa_ref, ba_ref, wx_ref, bx_ref, lam_ref)
        gel, dgel = _gelu_parts(gb_ref[...])
        dy_v = dy_ref[...]
        h_v = h_ref[...]
        dh = dy_v * gel
        dgb_ref[...] = (dy_v * h_v * dgel).astype(BF16)
        a_buf[...] = a
        q_buf[...] = a * dh
        row = lax.broadcasted_iota(jnp.int32, (SUBLANES, gw), 0)

        def tile(kk, c):
            r0 = pl.multiple_of((nt - 1 - kk) * SUBLANES, SUBLANES)
            A = a_buf[pl.ds(r0, SUBLANES), :]
            B = q_buf[pl.ds(r0, SUBLANES), :]
            for d in (1, 2, 4):
                m = row < SUBLANES - d
                B = jnp.where(m, A * pltpu.roll(B, SUBLANES - d, 0) + B, B)
                A = jnp.where(m, A * pltpu.roll(A, SUBLANES - d, 0), A)
            P = A * c + B
            p_buf[pl.ds(r0, SUBLANES), :] = jnp.where(row == SUBLANES - 1, c, pltpu.roll(P, SUBLANES - 1, 0))
            return P[0:1, :]

        carry[...] = lax.fori_loop(0, nt, tile, carry[...])
        Gt = dh + p_buf[...]
        hbuf[0:SUBLANES, :] = jnp.where(ib > 0, hh_ref[...], 0.0)
        hbuf[SUBLANES:, :] = h_v
        hprev = hbuf[pl.ds(SUBLANES - 1, ts), :]
        da = Gt * hprev
        dmult = Gt * (ig * u_v)
        dla = da * a - dmult * (a * a) / mult
        dr = dla * (-LRU_C * sp)
        dlam_ref[...] += _fold8(dla * (LRU_C * r)) * _sigmoid(-lam_ref[...])
        dig = Gt * mult * u_v
        dzr = dr * r * (1.0 - r)
        dzi = dig * ig * (1.0 - ig)
        dzr_b = dzr.astype(BF16)
        dzi_b = dzi.astype(BF16)
        du_ref[...] = Gt * mult * ig + _dot_tb(dzr_b, wa_ref[...]) + _dot_tb(dzi_b, wx_ref[...])
        dwa_ref[...] += _dot_ta(ub, dzr_b)
        dwx_ref[...] += _dot_ta(ub, dzi_b)
        dba_ref[...] += _fold8(dzr)
        dbx_ref[...] += _fold8(dzi)

    rblk = pl.BlockSpec((ts, gw), lambda g, i: (nb - 1 - i, g))
    halo = pl.BlockSpec((SUBLANES, gw), lambda g, i: (jnp.maximum((nb - 1 - i) * r8 - 1, 0), g))
    wsp = pl.BlockSpec((None, gw, gw), lambda g, i: (g, 0, 0))
    vsp = pl.BlockSpec((None, 1, gw), lambda g, i: (g, 0, 0))
    acc8 = pl.BlockSpec((None, SUBLANES, gw), lambda g, i: (g, 0, 0))
    return pl.pallas_call(
        body, name=name, grid=(G, nb),
        in_specs=[rblk, rblk, rblk, rblk, halo, wsp, vsp, wsp, vsp, vsp],
        out_specs=[rblk, rblk, wsp, wsp, acc8, acc8, acc8],
        out_shape=[jax.ShapeDtypeStruct((S, W), BF16), jax.ShapeDtypeStruct((S, W), F32),
                   jax.ShapeDtypeStruct((G, gw, gw), F32), jax.ShapeDtypeStruct((G, gw, gw), F32),
                   jax.ShapeDtypeStruct((G, SUBLANES, gw), F32), jax.ShapeDtypeStruct((G, SUBLANES, gw), F32),
                   jax.ShapeDtypeStruct((G, SUBLANES, gw), F32)],
        scratch_shapes=[pltpu.VMEM((ts, gw), F32)] * 3 + [pltpu.VMEM((ts + SUBLANES, gw), F32),
                                                          pltpu.VMEM((1, gw), F32)],
        compiler_params=_cp(("parallel", "arbitrary")),
    )(dy, u, proj, h, h, w_a, b_a.reshape(G, 1, gw), w_x, b_x.reshape(G, 1, gw), lam.reshape(G, 1, gw))


def _rt(S, pref):
    return min(S, pref)


def _lru_mixer_fwd(x, p, tag):
    S, D = x.shape
    W = p["w_out"].shape[0]
    ts = _rt(S, 512)
    proj = _mm(x, p["w_in"], out_dtype=F32, tm=ts, tn=W, name=tag + "_proj")
    u = _conv_fwd(proj, p["conv_w"], p["conv_b"], K=4, ts=ts, tc=512, x_c0=W, name=tag + "_conv")
    y, h = _lru_fwd(u, proj, p["w_a"], p["b_a"], p["w_x"], p["b_x"], p["lam"], ts=ts, name=tag + "_scan")
    mix = _mm(y, p["w_out"], out_dtype=F32, tm=ts, tn=D, name=tag + "_out")
    return mix, (proj, u, h, y)


def _lru_mixer_bwd(dmix, x, p, saved, tag):
    proj, u, h, y = saved
    S, D = x.shape
    W = p["w_out"].shape[0]
    ts = _rt(S, 512)
    g = {}
    dy = _mm_tb([(dmix, 0)], p["w_out"], out_dtype=F32, tm=ts, tk=W, name=tag + "_dy")
    g["w_out"] = _mm_ta(y, dmix, tk=W, tn=D, tm=ts, name=tag + "_dwout")
    dgb, du, g["w_a"], g["w_x"], dba8, dbx8, dlam8 = _lru_bwd(
        dy, u, proj, h, p["w_a"], p["b_a"], p["w_x"], p["b_x"], p["lam"], ts=ts, name=tag + "_scanb")
    g["b_a"] = dba8.sum(axis=1)
    g["b_x"] = dbx8.sum(axis=1)
    g["lam"] = dlam8.sum(axis=1).reshape(-1)
    dcw8, dcb8 = _conv_wgrad(du, proj, K=4, ts=ts, tc=512, x_c0=W, name=tag + "_convw")
    g["conv_w"] = dcw8.sum(axis=1)
    g["conv_b"] = dcb8.sum(axis=0)
    drnn = _conv_bwd(du, p["conv_w"], K=4, ts=ts, tc=512, w_c0=0, out_dtype=BF16, name=tag + "_convb")
    dx = _mm_tb([(dgb, 0), (drnn, W)], p["w_in"], out_dtype=F32, tm=ts, tk=D, name=tag + "_dx")
    g["w_in"] = jnp.concatenate(
        [_mm_ta(x, dgb, tk=D, tn=W, tm=ts, name=tag + "_dwin_g"),
         _mm_ta(x, drnn, tk=D, tn=W, tm=ts, name=tag + "_dwin_r")], axis=1)
    return dx, g


def _ffn_fwd(x, p, tag):
    S, D = x.shape
    F = p["w_down"].shape[0]
    ts = _rt(S, 512)
    tc = F // 2
    hpre = _mm(x, p["w_up"], out_dtype=BF16, tm=ts, tn=tc, name=tag + "_up")
    a = _ffn_mid_fwd(hpre, p["conv_w"], p["conv_b"], ts=_rt(S, 256), tc=tc, name=tag + "_mid")
    f = _mm(a, p["w_down"], out_dtype=F32, tm=ts, tn=D, name=tag + "_down")
    return f, (hpre, a)


def _ffn_bwd(df, x, p, saved, tag):
    hpre, a = saved
    S, D = x.shape
    F = p["w_down"].shape[0]
    ts = _rt(S, 512)
    tc = F // 2
    g = {}
    da = _mm_tb([(df, 0)], p["w_down"], out_dtype=BF16, tm=ts, tk=tc, name=tag + "_da")
    g["w_down"] = _mm_ta(a, df, tk=tc, tn=D, tm=ts, name=tag + "_dwdown")
    dpg, dpu, dwg8, dwu8, dbg8, dbu8 = _ffn_mid_bwd(hpre, da, p["conv_w"], p["conv_b"], ts=_rt(S, 256), tc=tc,
                                                    name=tag + "_midb")
    g["conv_w"] = jnp.concatenate([dwg8.sum(axis=1), dwu8.sum(axis=1)], axis=1)
    g["conv_b"] = jnp.concatenate([dbg8.sum(axis=0), dbu8.sum(axis=0)], axis=0)
    dx = _mm_tb([(dpg, 0), (dpu, F)], p["w_up"], out_dtype=F32, tm=ts, tk=D, name=tag + "_dx")
    g["w_up"] = jnp.concatenate(
        [_mm_ta(x, dpg, tk=D, tn=tc, tm=ts, name=tag + "_dwup_g"),
         _mm_ta(x, dpu, tk=D, tn=tc, tm=ts, name=tag + "_dwup_u")], axis=1)
    return dx, g


HEAD_SLOT = LANES
MLA_SCALE = (QK_NOPE + QK_ROPE) ** -0.5
NEG_BIG = -1e30
ATTN_BLOCK = 1024


def _rms_fwd(x, g, *, c0, ts, name):
    S = x.shape[0]
    w = g.shape[0]

    def body(x_ref, g_ref, o_ref):
        xv = x_ref[...]
        rstd = lax.rsqrt(jnp.mean(xv * xv, axis=-1, keepdims=True) + RMS_EPS)
        o_ref[...] = (xv * rstd * g_ref[...]).astype(BF16)

    return pl.pallas_call(
        body, name=name, grid=(S // ts,),
        in_specs=[pl.BlockSpec((ts, w), lambda i: (i, c0 // w)), pl.BlockSpec((1, w), lambda i: (0, 0))],
        out_specs=pl.BlockSpec((ts, w), lambda i: (i, 0)),
        out_shape=jax.ShapeDtypeStruct((S, w), BF16), compiler_params=_cp(("parallel",)),
    )(x, g.reshape(1, w))


def _rms_bwd(dy, x, g, *, c0, ts, name):
    S = x.shape[0]
    w = g.shape[0]

    def body(dy_ref, x_ref, g_ref, dx_ref, dg_ref):
        @pl.when(pl.program_id(0) == 0)
        def _():
            dg_ref[...] = jnp.zeros_like(dg_ref)
        xv = x_ref[...]
        dyv = dy_ref[...]
        rstd = lax.rsqrt(jnp.mean(xv * xv, axis=-1, keepdims=True) + RMS_EPS)
        dyg = dyv * g_ref[...]
        m = jnp.mean(dyg * xv, axis=-1, keepdims=True)
        dx_ref[...] = (rstd * (dyg - xv * (rstd * rstd) * m)).astype(BF16)
        dg_ref[...] += _fold8(dyv * xv * rstd)

    return pl.pallas_call(
        body, name=name, grid=(S // ts,),
        in_specs=[pl.BlockSpec((ts, w), lambda i: (i, 0)), pl.BlockSpec((ts, w), lambda i: (i, c0 // w)),
                  pl.BlockSpec((1, w), lambda i: (0, 0))],
        out_specs=[pl.BlockSpec((ts, w), lambda i: (i, 0)), pl.BlockSpec((SUBLANES, w), lambda i: (0, 0))],
        out_shape=[jax.ShapeDtypeStruct((S, w), BF16), jax.ShapeDtypeStruct((SUBLANES, w), F32)],
        compiler_params=_cp(("arbitrary",)),
    )(dy, x, g.reshape(1, w))


def _mla_tables(pos):
    S = pos.shape[0]
    half = QK_ROPE // 2
    inv_freq = ROPE_BASE ** (-jnp.arange(0, QK_ROPE, 2, dtype=F32) / QK_ROPE)
    ang = pos.astype(F32)[:, None] * inv_freq
    cos, sin = jnp.cos(ang), jnp.sin(ang)
    z = lambda n: jnp.zeros((S, n), F32)
    pad = HEAD_SLOT - QK_NOPE - QK_ROPE
    c = jnp.concatenate([jnp.ones((S, QK_NOPE), F32), cos, cos, z(pad)], axis=1)
    s1 = jnp.concatenate([z(QK_NOPE), -sin, z(half), z(pad)], axis=1)
    s2 = jnp.concatenate([z(QK_NOPE), z(half), sin, z(pad)], axis=1)
    return c, s1, s2


def _mla_prep_fwd(qraw, kvraw, proj, tabs, *, kpe_c0, ts, name):
    S = qraw.shape[0]
    H = MLA_HEADS
    half = QK_ROPE // 2

    def body(q_ref, kn_ref, kpe_ref, c_ref, s1_ref, s2_ref, qo_ref, ko_ref):
        c, s1, s2 = c_ref[...], s1_ref[...], s2_ref[...]

        def rope(v):
            return v * c + pltpu.roll(v, HEAD_SLOT - half, 1) * s1 + pltpu.roll(v, half, 1) * s2

        qo_ref[...] = (rope(q_ref[...]) * MLA_SCALE).astype(BF16)
        ko_ref[...] = (kn_ref[...] + rope(kpe_ref[...])).astype(BF16)

    slot = pl.BlockSpec((ts, HEAD_SLOT), lambda i, h: (i, h))
    tab = pl.BlockSpec((ts, HEAD_SLOT), lambda i, h: (i, 0))
    return pl.pallas_call(
        body, name=name, grid=(S // ts, H),
        in_specs=[slot, slot, pl.BlockSpec((ts, HEAD_SLOT), lambda i, h: (i, kpe_c0 // HEAD_SLOT)), tab, tab, tab],
        out_specs=[slot, slot],
        out_shape=[jax.ShapeDtypeStruct((S, H * HEAD_SLOT), BF16)] * 2,
        compiler_params=_cp(("parallel", "parallel")),
    )(qraw, kvraw, proj, *tabs)


def _mla_prep_bwd(dq, dk, dv, tabs, *, ts, name):
    S = dk.shape[0]
    H = MLA_HEADS
    half = QK_ROPE // 2
    kw = H * HEAD_SLOT
    vw = H * V_HEAD

    def body(dq_ref, dk_ref, dv_ref, c_ref, s1_ref, s2_ref, dqr_ref, dkv_ref, dkpe_ref):
        c, s1, s2 = c_ref[...], s1_ref[...], s2_ref[...]

        def rope_t(g):
            return g * c + pltpu.roll(g * s1, half, 1) + pltpu.roll(g * s2, HEAD_SLOT - half, 1)

        gsum = jnp.zeros((ts, HEAD_SLOT), F32)
        for h in range(H):
            sl = slice(h * HEAD_SLOT, (h + 1) * HEAD_SLOT)
            hs = slice((h % 2) * HEAD_SLOT, (h % 2 + 1) * HEAD_SLOT)
            dqr_ref[:, sl] = (rope_t(dq_ref[h // 2, :, hs]) * MLA_SCALE).astype(BF16)
            dkh = dk_ref[:, sl]
            dkv_ref[:, sl] = dkh.astype(BF16)
            gsum = gsum + dkh
        dkv_ref[:, kw:] = dv_ref[...].astype(BF16)
        lane = lax.broadcasted_iota(jnp.int32, (ts, HEAD_SLOT), 1)
        pe = jnp.logical_and(lane >= QK_NOPE, lane < QK_NOPE + QK_ROPE)
        dkpe_ref[...] = rope_t(jnp.where(pe, gsum, 0.0)).astype(BF16)

    tab = pl.BlockSpec((ts, HEAD_SLOT), lambda i: (i, 0))
    return pl.pallas_call(
        body, name=name, grid=(S // ts,),
        in_specs=[pl.BlockSpec((H // 2, ts, 2 * HEAD_SLOT), lambda i: (0, i, 0)), pl.BlockSpec((ts, kw), lambda i: (i, 0)),
                  pl.BlockSpec((ts, vw), lambda i: (i, 0)), tab, tab, tab],
        out_specs=[pl.BlockSpec((ts, kw), lambda i: (i, 0)), pl.BlockSpec((ts, kw + vw), lambda i: (i, 0)), tab],
        out_shape=[jax.ShapeDtypeStruct((S, kw), BF16), jax.ShapeDtypeStruct((S, kw + vw), BF16),
                   jax.ShapeDtypeStruct((S, HEAD_SLOT), BF16)],
        compiler_params=_cp(("parallel",)),
    )(dq, dk, dv, *tabs)


def _attn_pairs(nb, kv_outer):
    if kv_outer:
        pr = [(i, j) for j in range(nb) for i in range(j, nb)]
    else:
        pr = [(i, j) for i in range(nb) for j in range(i + 1)]
    return (jnp.asarray(np.array([p[0] for p in pr], np.int32)), jnp.asarray(np.array([p[1] for p in pr], np.int32)))


def _attn_scores(q_ref, k_ref, hh, diag, T):
    sl = slice(hh * HEAD_SLOT, (hh + 1) * HEAD_SLOT)
    s = _dot_tb(q_ref[:, sl], k_ref[:, sl])
    if not diag:
        return s
    row = lax.broadcasted_iota(jnp.int32, (T, T), 0) // CHUNK
    col = lax.broadcasted_iota(jnp.int32, (T, T), 1) // CHUNK
    return jnp.where(col <= row, s, NEG_BIG)


def _on_block_kind(i, j, step):
    @pl.when(i == j)
    def _():
        step(True)

    @pl.when(i != j)
    def _():
        step(False)


def _attn_fwd(q, k, kvraw, *, T, name):
    S = q.shape[0]
    NP = MLA_HEADS // 2
    nb = S // T
    ii, jj = _attn_pairs(nb, kv_outer=False)
    v_c0 = MLA_HEADS * HEAD_SLOT // LANES

    def body(ii_ref, jj_ref, q_ref, k_ref, v_ref, o_ref, lse_ref, m_sc, l_sc, acc_sc):
        t = pl.program_id(1)
        i, j = ii_ref[t], jj_ref[t]

        @pl.when(j == 0)
        def _():
            m_sc[...] = jnp.full_like(m_sc, NEG_BIG)
            l_sc[...] = jnp.zeros_like(l_sc)
            acc_sc[...] = jnp.zeros_like(acc_sc)

        lo = lax.broadcasted_iota(jnp.int32, (T, LANES), 1) < V_HEAD

        def step(diag):
            v = v_ref[...].astype(BF16)
            vh = (jnp.where(lo, v, jnp.zeros_like(v)), jnp.where(lo, jnp.zeros_like(v), v))
            alphas, pv = [], None
            for hh in range(2):
                s = _attn_scores(q_ref, k_ref, hh, diag, T)
                m_prev = m_sc[hh]
                m_new = jnp.maximum(m_prev, jnp.max(s, axis=1, keepdims=True))
                p = jnp.exp(s - jnp.tile(m_new, (1, T // LANES)))
                alpha = jnp.exp(m_prev - m_new)
                l_sc[hh] = alpha * l_sc[hh] + jnp.sum(p, axis=1, keepdims=True)
                m_sc[hh] = m_new
                alphas.append(alpha)
                t_pv = _dot(p.astype(BF16), vh[hh])
                pv = t_pv if pv is None else pv + t_pv
            acc_sc[...] = acc_sc[...] * jnp.where(lo, alphas[0], alphas[1]) + pv

        _on_block_kind(i, j, step)

        @pl.when(j == i)
        def _():
            l0, l1 = l_sc[0], l_sc[1]
            o_ref[...] = acc_sc[...] * jnp.where(lo, 1.0 / l0, 1.0 / l1)
            lse_ref[...] = jnp.where(lo, m_sc[0] + jnp.log(l0), m_sc[1] + jnp.log(l1))

    grid_spec = pltpu.PrefetchScalarGridSpec(
        num_scalar_prefetch=2, grid=(NP, int(ii.shape[0])),
        in_specs=[pl.BlockSpec((T, 2 * HEAD_SLOT), lambda p, t, ii, jj: (ii[t], p)),
                  pl.BlockSpec((T, 2 * HEAD_SLOT), lambda p, t, ii, jj: (jj[t], p)),
                  pl.BlockSpec((T, LANES), lambda p, t, ii, jj: (jj[t], v_c0 + p))],
        out_specs=[pl.BlockSpec((T, LANES), lambda p, t, ii, jj: (ii[t], p)),
                   pl.BlockSpec((None, T, LANES), lambda p, t, ii, jj: (p, ii[t], 0))],
        scratch_shapes=[pltpu.VMEM((2, T, LANES), F32), pltpu.VMEM((2, T, LANES), F32), pltpu.VMEM((T, LANES), F32)])
    return pl.pallas_call(
        body, name=name, grid_spec=grid_spec,
        out_shape=[jax.ShapeDtypeStruct((S, MLA_HEADS * V_HEAD), F32), jax.ShapeDtypeStruct((NP, S, LANES), F32)],
        compiler_params=_cp(("parallel", "arbitrary")),
    )(ii, jj, q, k, kvraw)


def _attn_bwd_common(q_ref, k_ref, v, do, o, lse, hh, diag, T, lo):
    sel = lo if hh == 0 else jnp.logical_not(lo)
    s = _attn_scores(q_ref, k_ref, hh, diag, T)
    p = jnp.exp(s - lse[:, hh * V_HEAD:hh * V_HEAD + 1])
    do_h = jnp.where(sel, do, 0.0)
    dsum = jnp.sum(do_h * o, axis=1, keepdims=True)
    do_hb = do_h.astype(BF16)
    dp = _dot_tb(do_hb, v)
    return p, p * (dp - dsum), do_hb


def _attn_bwd(q, k, kvraw, do, o, lse, *, T, name):
    S = q.shape[0]
    NP = MLA_HEADS // 2
    nb = S // T
    ii, jj = _attn_pairs(nb, kv_outer=True)
    n_steps = int(ii.shape[0])
    v_c0 = MLA_HEADS * HEAD_SLOT // LANES

    def body(ii_ref, jj_ref, q_ref, k_ref, v_ref, do_ref, o_ref, lse_ref, dq_hbm, dk_ref, dv_ref,
             dq_sc, dk_sc, dv_sc, sem):
        pair = pl.program_id(0)
        t = pl.program_id(1)
        i, j = ii_ref[t], jj_ref[t]

        @pl.when(t == 0)
        def _():
            dq_sc[...] = jnp.zeros_like(dq_sc)

        @pl.when(i == j)
        def _():
            dk_sc[...] = jnp.zeros_like(dk_sc)
            dv_sc[...] = jnp.zeros_like(dv_sc)

        lo = lax.broadcasted_iota(jnp.int32, (T, LANES), 1) < V_HEAD
        rows = pl.ds(pl.multiple_of(i * T, T), T)

        def step(diag):
            v = v_ref[...].astype(BF16)
            do, o_v, lse_v = do_ref[...], o_ref[...], lse_ref[...]
            for hh in range(2):
                sl = slice(hh * HEAD_SLOT, (hh + 1) * HEAD_SLOT)
                p, ds, do_hb = _attn_bwd_common(q_ref, k_ref, v, do, o_v, lse_v, hh, diag, T, lo)
                dsb = ds.astype(BF16)
                dv_sc[...] += _dot_ta(p.astype(BF16), do_hb)
                dk_sc[:, sl] += _dot_ta(dsb, q_ref[:, sl])
                dq_sc[rows, sl] += _dot(dsb, k_ref[:, sl])

        _on_block_kind(i, j, step)

        @pl.when(i == nb - 1)
        def _():
            dk_ref[...] = dk_sc[...]
            dv_ref[...] = dv_sc[...]

        @pl.when(t == n_steps - 1)
        def _():
            cp = pltpu.make_async_copy(dq_sc, dq_hbm.at[pair], sem)
            cp.start()
            cp.wait()

    qi = lambda p, t, ii, jj: (ii[t], p)
    kj = lambda p, t, ii, jj: (jj[t], p)
    grid_spec = pltpu.PrefetchScalarGridSpec(
        num_scalar_prefetch=2, grid=(NP, n_steps),
        in_specs=[pl.BlockSpec((T, 2 * HEAD_SLOT), qi), pl.BlockSpec((T, 2 * HEAD_SLOT), kj),
                  pl.BlockSpec((T, LANES), lambda p, t, ii, jj: (jj[t], v_c0 + p)),
                  pl.BlockSpec((T, LANES), qi), pl.BlockSpec((T, LANES), qi),
                  pl.BlockSpec((None, T, LANES), lambda p, t, ii, jj: (p, ii[t], 0))],
        out_specs=[pl.BlockSpec(memory_space=pl.ANY), pl.BlockSpec((T, 2 * HEAD_SLOT), kj),
                   pl.BlockSpec((T, LANES), kj)],
        scratch_shapes=[pltpu.VMEM((S, 2 * HEAD_SLOT), F32), pltpu.VMEM((T, 2 * HEAD_SLOT), F32),
                        pltpu.VMEM((T, LANES), F32), pltpu.SemaphoreType.DMA])
    return pl.pallas_call(
        body, name=name, grid_spec=grid_spec,
        out_shape=[jax.ShapeDtypeStruct((NP, S, 2 * HEAD_SLOT), F32),
                   jax.ShapeDtypeStruct((S, MLA_HEADS * HEAD_SLOT), F32),
                   jax.ShapeDtypeStruct((S, MLA_HEADS * V_HEAD), F32)],
        compiler_params=_cp(("arbitrary", "arbitrary")),
    )(ii, jj, q, k, kvraw, do, o, lse)


def _mla_permute_weights(w_in, w_uq, w_ukv):
    D = w_in.shape[0]
    H = MLA_HEADS
    qk = QK_NOPE + QK_ROPE
    lat = Q_LORA + KV_LORA
    kpe = jnp.zeros((D, HEAD_SLOT), w_in.dtype).at[:, QK_NOPE:qk].set(w_in[:, lat:])
    w_in_p = jnp.concatenate([w_in[:, :lat], kpe], axis=1)
    w_uq_p = jnp.pad(w_uq.reshape(Q_LORA, H, qk), ((0, 0), (0, 0), (0, HEAD_SLOT - qk))).reshape(Q_LORA, H * HEAD_SLOT)
    kv = w_ukv.reshape(KV_LORA, H, QK_NOPE + V_HEAD)
    wk = jnp.pad(kv[:, :, :QK_NOPE], ((0, 0), (0, 0), (0, HEAD_SLOT - QK_NOPE))).reshape(KV_LORA, H * HEAD_SLOT)
    wv = kv[:, :, QK_NOPE:].reshape(KV_LORA, H * V_HEAD)
    return w_in_p, w_uq_p, jnp.concatenate([wk, wv], axis=1)


def _mla_unpermute_grads(g_in_p, g_uq_p, g_ukv_p):
    H = MLA_HEADS
    qk = QK_NOPE + QK_ROPE
    lat = Q_LORA + KV_LORA
    g_in = jnp.concatenate([g_in_p[:, :lat], g_in_p[:, lat + QK_NOPE:lat + qk]], axis=1)
    g_uq = g_uq_p.reshape(Q_LORA, H, HEAD_SLOT)[:, :, :qk].reshape(Q_LORA, H * qk)
    gk = g_ukv_p[:, :H * HEAD_SLOT].reshape(KV_LORA, H, HEAD_SLOT)[:, :, :QK_NOPE]
    gv = g_ukv_p[:, H * HEAD_SLOT:].reshape(KV_LORA, H, V_HEAD)
    g_ukv = jnp.concatenate([gk, gv], axis=2).reshape(KV_LORA, H * (QK_NOPE + V_HEAD))
    return g_in, g_uq, g_ukv


def _mla_mixer_fwd(x, pos, p, tag):
    S, D = x.shape
    ts = _rt(S, 512)
    T = _rt(S, ATTN_BLOCK)
    lat = Q_LORA + KV_LORA
    tabs = _mla_tables(pos)
    proj = _mm(x, p["w_in_p"], out_dtype=F32, tm=ts, tn=p["w_in_p"].shape[1], name=tag + "_proj")
    qn = _rms_fwd(proj, p["q_norm"], c0=0, ts=ts, name=tag + "_qn")
    kvn = _rms_fwd(proj, p["kv_norm"], c0=Q_LORA, ts=ts, name=tag + "_kvn")
    qraw = _mm(qn, p["w_uq_p"], out_dtype=F32, tm=ts, tn=1024, name=tag + "_uq")
    kvraw = _mm(kvn, p["w_ukv_p"], out_dtype=F32, tm=ts, tn=1024, name=tag + "_ukv")
    q, k = _mla_prep_fwd(qraw, kvraw, proj, tabs, kpe_c0=lat, ts=ts, name=tag + "_prep")
    o, lse = _attn_fwd(q, k, kvraw, T=T, name=tag + "_attn")
    mix = _mm(o, p["w_out"], out_dtype=F32, tm=ts, tn=D, name=tag + "_out")
    return mix, (proj, qn, kvn, kvraw, q, k, o, lse, tabs)


def _mla_mixer_bwd(dmix, x, p, saved, tag):
    proj, qn, kvn, kvraw, q, k, o, lse, tabs = saved
    S, D = x.shape
    ts = _rt(S, 512)
    T = _rt(S, ATTN_BLOCK)
    lat = Q_LORA + KV_LORA
    g = {}
    do = _mm_tb([(dmix, 0)], p["w_out"], out_dtype=F32, tm=ts, tk=p["w_out"].shape[0], name=tag + "_do")
    g["w_out"] = _mm_ta(o, dmix, tk=p["w_out"].shape[0], tn=D, tm=ts, name=tag + "_dwout")
    dq, dk, dv = _attn_bwd(q, k, kvraw, do, o, lse, T=T, name=tag + "_attn_bwd")
    dqraw, dkvraw, dkpe = _mla_prep_bwd(dq, dk, dv, tabs, ts=_rt(S, 256), name=tag + "_prepb")
    dqn = _mm_tb([(dqraw, 0)], p["w_uq_p"], out_dtype=F32, tm=ts, tk=Q_LORA, name=tag + "_dqn")
    g_uq_p = _mm_ta(qn, dqraw, tk=Q_LORA, tn=1024, tm=ts, name=tag + "_dwuq")
    dkvn = _mm_tb([(dkvraw, 0)], p["w_ukv_p"], out_dtype=F32, tm=ts, tk=KV_LORA, name=tag + "_dkvn")
    g_ukv_p = _mm_ta(kvn, dkvraw, tk=KV_LORA, tn=1024, tm=ts, name=tag + "_dwukv")
    dcq, dqg8 = _rms_bwd(dqn, proj, p["q_norm"], c0=0, ts=ts, name=tag + "_qnb")
    dckv, dkvg8 = _rms_bwd(dkvn, proj, p["kv_norm"], c0=Q_LORA, ts=ts, name=tag + "_kvnb")
    g["q_norm"] = dqg8.sum(axis=0)
    g["kv_norm"] = dkvg8.sum(axis=0)
    dx = _mm_tb([(dcq, 0), (dckv, Q_LORA), (dkpe, lat)], p["w_in_p"], out_dtype=F32, tm=ts, tk=D, name=tag + "_dx")
    g_in_p = jnp.concatenate(
        [_mm_ta(x, dcq, tk=D, tn=Q_LORA, tm=ts, name=tag + "_dwin_q"),
         _mm_ta(x, dckv, tk=D, tn=KV_LORA, tm=ts, name=tag + "_dwin_kv"),
         _mm_ta(x, dkpe, tk=D, tn=HEAD_SLOT, tm=ts, name=tag + "_dwin_pe")], axis=1)
    g["w_in"], g["w_uq"], g["w_ukv"] = _mla_unpermute_grads(g_in_p, g_uq_p, g_ukv_p)
    return dx, g


RET_QK = 256
RET_V = 512


def _ret_tables(pos, T):
    half = RET_QK // 2
    inv_freq = ROPE_BASE ** (-jnp.arange(0, RET_QK, 2, dtype=F32) / RET_QK)
    ang = pos.astype(F32)[:, None] * inv_freq
    lg = jnp.log1p(-jnp.exp2(-5.0 - jnp.arange(RET_HEADS, dtype=F32)))
    idx = jnp.arange(T, dtype=F32)
    ch = jnp.arange(T) // CHUNK
    dm = jnp.where(ch[None, :] <= ch[:, None], jnp.exp(lg[:, None, None] * jnp.abs(idx[:, None] - idx[None, :])), 0.0)
    xi = jnp.broadcast_to(jnp.exp(lg[:, None] * (idx + 1.0))[:, :, None], (RET_HEADS, T, RET_QK))
    zeta = jnp.broadcast_to(jnp.exp(lg[:, None] * (T - 1.0 - idx))[:, :, None], (RET_HEADS, T, RET_QK))
    g_t = jnp.broadcast_to(jnp.exp(lg * T)[:, None, None], (RET_HEADS, 1, RET_V))
    assert half == LANES
    return jnp.cos(ang), jnp.sin(ang), dm.astype(F32), xi.astype(F32), zeta.astype(F32), g_t.astype(F32)


def _rope_half(x, c, s):
    x1, x2 = x[:, :LANES], x[:, LANES:]
    return jnp.concatenate([x1 * c - x2 * s, x1 * s + x2 * c], axis=1)


def _rope_half_t(g, c, s):
    g1, g2 = g[:, :LANES], g[:, LANES:]
    return jnp.concatenate([g1 * c + g2 * s, g2 * c - g1 * s], axis=1)


def _ret_qkv(q_ref, k_ref, v_ref, c_ref, s_ref):
    c, s = c_ref[...], s_ref[...]
    q = _rope_half(q_ref[...], c, s)
    k = _rope_half(k_ref[...], c, s) * (RET_QK ** -0.5)
    return q, k, v_ref[...].astype(BF16)


def _ret_in_specs(T, H, rev_nb=None):
    rb = (lambda n: n) if rev_nb is None else (lambda n: rev_nb - 1 - n)
    nq = H * RET_QK // RET_QK
    nv = 2 * H * RET_QK // RET_V
    return dict(
        q=pl.BlockSpec((T, RET_QK), lambda h, n: (rb(n), h)),
        k=pl.BlockSpec((T, RET_QK), lambda h, n: (rb(n), nq + h)),
        v=pl.BlockSpec((T, RET_V), lambda h, n: (rb(n), nv + h)),
        g=pl.BlockSpec((T, RET_V), lambda h, n: (rb(n), nv + H + h)),
        yv=pl.BlockSpec((T, RET_V), lambda h, n: (rb(n), h)),
        cs=pl.BlockSpec((T, LANES), lambda h, n: (rb(n), 0)),
        dm=pl.BlockSpec((None, T, T), lambda h, n: (h, 0, 0)),
        xz=pl.BlockSpec((None, T, RET_QK), lambda h, n: (h, 0, 0)),
        gt=pl.BlockSpec((None, 1, RET_V), lambda h, n: (h, 0, 0)),
        gn=pl.BlockSpec((1, RET_V), lambda h, n: (0, h)),
        st=pl.BlockSpec((None, None, RET_QK, RET_V), lambda h, n: (h, rb(n), 0, 0)),
    )


def _ret_fwd(proj, gn_g, tabs, *, T, name):
    S = proj.shape[0]
    H = RET_HEADS
    nb = S // T
    cos, sin, dm, xi, zeta, g_t = tabs
    sp = _ret_in_specs(T, H)

    def body(q_ref, k_ref, v_ref, g_ref, gn_ref, c_ref, s_ref, dm_ref, xi_ref, zeta_ref, gt_ref,
             o_ref, y_ref, st_ref, st):
        @pl.when(pl.program_id(1) == 0)
        def _():
            st[...] = jnp.zeros_like(st)

        q, k, vb = _ret_qkv(q_ref, k_ref, v_ref, c_ref, s_ref)
        qb, kb = q.astype(BF16), k.astype(BF16)
        s0 = st[...]
        s0b = s0.astype(BF16)
        st_ref[...] = s0b
        a = _dot_tb(qb, kb) * dm_ref[...]
        y = _dot(a.astype(BF16), vb) + _dot((q * xi_ref[...]).astype(BF16), s0b)
        st[...] = s0 * gt_ref[...] + _dot_ta((k * zeta_ref[...]).astype(BF16), vb)
        y_ref[...] = y
        mu = jnp.mean(y, axis=-1, keepdims=True)
        yc = y - mu
        var = jnp.mean(yc * yc, axis=-1, keepdims=True)
        gv = g_ref[...]
        o_ref[...] = (gv * _sigmoid(gv) * (yc * lax.rsqrt(var + LN_EPS) * gn_ref[...])).astype(BF16)

    return pl.pallas_call(
        body, name=name, grid=(H, nb),
        in_specs=[sp["q"], sp["k"], sp["v"], sp["g"], sp["gn"], sp["cs"], sp["cs"], sp["dm"], sp["xz"], sp["xz"], sp["gt"]],
        out_specs=[sp["yv"], sp["yv"], sp["st"]],
        out_shape=[jax.ShapeDtypeStruct((S, H * RET_V), BF16), jax.ShapeDtypeStruct((S, H * RET_V), F32),
                   jax.ShapeDtypeStruct((H, nb, RET_QK, RET_V), BF16)],
        scratch_shapes=[pltpu.VMEM((RET_QK, RET_V), F32)],
        compiler_params=_cp(("parallel", "arbitrary")),
    )(proj, proj, proj, proj, gn_g.reshape(1, H * RET_V), cos, sin, dm, xi, zeta, g_t)


def _ret_gn_bwd(dout, proj, y, gn_g, *, ts, name):
    S = proj.shape[0]
    H = RET_HEADS
    goff = 2 * H * RET_QK // RET_V + H

    def body(do_ref, g_ref, y_ref, gn_ref, dy_ref, dg_ref, dgn_ref):
        @pl.when(pl.program_id(1) == 0)
        def _():
            dgn_ref[...] = jnp.zeros_like(dgn_ref)
        y_v = y_ref[...]
        mu = jnp.mean(y_v, axis=-1, keepdims=True)
        yc = y_v - mu
        var = jnp.mean(yc * yc, axis=-1, keepdims=True)
        rstd = lax.rsqrt(var + LN_EPS)
        yhat = yc * rstd
        gv = g_ref[...]
        sg = _sigmoid(gv)
        dout = do_ref[...]
        gn = gn_ref[...]
        dg_ref[...] = (dout * (yhat * gn) * (sg * (1.0 + gv * (1.0 - sg)))).astype(BF16)
        dyn = dout * (gv * sg)
        dgn_ref[...] += _fold8(dyn * yhat)
        dyh = dyn * gn
        m1 = jnp.mean(dyh, axis=-1, keepdims=True)
        m2 = jnp.mean(dyh * yhat, axis=-1, keepdims=True)
        dy_ref[...] = (rstd * (dyh - m1 - yhat * m2)).astype(BF16)

    blk = pl.BlockSpec((ts, RET_V), lambda h, i: (i, h))
    return pl.pallas_call(
        body, name=name, grid=(H, S // ts),
        in_specs=[blk, pl.BlockSpec((ts, RET_V), lambda h, i: (i, goff + h)), blk,
                  pl.BlockSpec((1, RET_V), lambda h, i: (0, h))],
        out_specs=[blk, blk, pl.BlockSpec((SUBLANES, RET_V), lambda h, i: (0, h))],
        out_shape=[jax.ShapeDtypeStruct((S, H * RET_V), BF16), jax.ShapeDtypeStruct((S, H * RET_V), BF16),
                   jax.ShapeDtypeStruct((SUBLANES, H * RET_V), F32)],
        compiler_params=_cp(("parallel", "arbitrary")),
    )(dout, proj, y, gn_g.reshape(1, H * RET_V))


def _ret_bwd(proj, dy, states, tabs, *, T, name):
    S = proj.shape[0]
    H = RET_HEADS
    nb = S // T
    cos, sin, dm, xi, zeta, g_t = tabs
    sp = _ret_in_specs(T, H, rev_nb=nb)

    def body(q_ref, k_ref, v_ref, dy_ref, st_ref, c_ref, s_ref, dm_ref, xi_ref, zeta_ref, gt_ref,
             dq_ref, dk_ref, dv_ref, ds):
        @pl.when(pl.program_id(1) == 0)
        def _():
            ds[...] = jnp.zeros_like(ds)

        q, k, vb = _ret_qkv(q_ref, k_ref, v_ref, c_ref, s_ref)
        qb, kb = q.astype(BF16), k.astype(BF16)
        dyb = dy_ref[...]
        s0b = st_ref[...]
        dmv, xiv, zv = dm_ref[...], xi_ref[...], zeta_ref[...]
        ds_v = ds[...]
        dsb = ds_v.astype(BF16)
        gm = (_dot_tb(dyb, vb) * dmv).astype(BF16)
        ab = (_dot_tb(qb, kb) * dmv).astype(BF16)
        kz = (k * zv).astype(BF16)
        qx = (q * xiv).astype(BF16)
        dq = _dot(gm, kb) + xiv * _dot_tb(dyb, s0b)
        dk = _dot_ta(gm, qb) + zv * _dot_tb(vb, dsb)
        dv_ref[...] = (_dot_ta(ab, dyb) + _dot(kz, dsb)).astype(BF16)
        ds[...] = ds_v * gt_ref[...] + _dot_ta(qx, dyb)
        c, s = c_ref[...], s_ref[...]
        dq_ref[...] = _rope_half_t(dq, c, s).astype(BF16)
        dk_ref[...] = _rope_half_t(dk * (RET_QK ** -0.5), c, s).astype(BF16)

    qblk = pl.BlockSpec((T, RET_QK), lambda h, n: (nb - 1 - n, h))
    return pl.pallas_call(
        body, name=name, grid=(H, nb),
        in_specs=[sp["q"], sp["k"], sp["v"], sp["yv"], sp["st"], sp["cs"], sp["cs"], sp["dm"], sp["xz"], sp["xz"], sp["gt"]],
        out_specs=[qblk, qblk, sp["yv"]],
        out_shape=[jax.ShapeDtypeStruct((S, H * RET_QK), BF16), jax.ShapeDtypeStruct((S, H * RET_QK), BF16),
                   jax.ShapeDtypeStruct((S, H * RET_V), BF16)],
        scratch_shapes=[pltpu.VMEM((RET_QK, RET_V), F32)],
        compiler_params=_cp(("parallel", "arbitrary")),
    )(proj, proj, proj, dy, states, cos, sin, dm, xi, zeta, g_t)


def _ret_mixer_fwd(x, pos, p, tag):
    S, D = x.shape
    ts = _rt(S, 512)
    T = _rt(S, 256)
    tabs = _ret_tables(pos, T)
    proj = _mm(x, p["w_in"], out_dtype=F32, tm=ts, tn=1024, name=tag + "_proj")
    gated, y, states = _ret_fwd(proj, p["gn_g"], tabs, T=T, name=tag + "_ret")
    mix = _mm(gated, p["w_out"], out_dtype=F32, tm=ts, tn=D, name=tag + "_out")
    return mix, (proj, gated, y, states, tabs)


def _ret_mixer_bwd(dmix, x, p, saved, tag):
    proj, gated, y, states, tabs = saved
    S, D = x.shape
    ts = _rt(S, 512)
    T = _rt(S, 256)
    H = RET_HEADS
    hq, hv = H * RET_QK, H * RET_V
    g = {}
    dout = _mm_tb([(dmix, 0)], p["w_out"], out_dtype=F32, tm=ts, tk=1024, name=tag + "_dgated")
    g["w_out"] = _mm_ta(gated, dmix, tk=1024, tn=D, tm=ts, name=tag + "_dwout")
    dy, dgate, dgn8 = _ret_gn_bwd(dout, proj, y, p["gn_g"], ts=_rt(S, 256), name=tag + "_gnb")
    g["gn_g"] = dgn8.sum(axis=0)
    dq, dk, dv = _ret_bwd(proj, dy, states, tabs, T=T, name=tag + "_retb")
    dx = _mm_tb([(dq, 0), (dk, hq), (dv, 2 * hq), (dgate, 2 * hq + hv)], p["w_in"], out_dtype=F32,
                tm=ts, tk=512, name=tag + "_dx")
    g["w_in"] = jnp.concatenate(
        [_mm_ta(x, dq, tk=D, tn=1024, tm=ts, name=tag + "_dwin_q"),
         _mm_ta(x, dk, tk=D, tn=1024, tm=ts, name=tag + "_dwin_k"),
         _mm_ta(x, dv, tk=D, tn=1024, tm=ts, name=tag + "_dwin_v"),
         _mm_ta(x, dgate, tk=D, tn=1024, tm=ts, name=tag + "_dwin_g")], axis=1)
    return dx, g


PACK_W = 1024
ANY = pl.BlockSpec(memory_space=pl.ANY)
MESH = pl.DeviceIdType.MESH


def _coords():
    return lax.axis_index("x"), lax.axis_index("y"), lax.axis_index("c")


def _chip_peers(x, y):
    return [(1 - x, y), (x, 1 - y), (1 - x, 1 - y)]


def _gather_chips(packs, name):
    n = len(packs)

    def body(*refs):
        srcs, outs = refs[:n], refs[n:2 * n]
        send_sems, recv_sems, local_sems = refs[2 * n:]
        x, y, c = _coords()
        me = 2 * x + y
        local = [pltpu.make_async_copy(srcs[t], outs[t].at[me], local_sems.at[t]) for t in range(n)]
        for cp in local:
            cp.start()
        sends, recvs = [], []
        for k, (px, py) in enumerate(_chip_peers(x, y)):
            for t in range(n):
                sem = k * n + t
                sends.append(pltpu.make_async_remote_copy(
                    src_ref=srcs[t], dst_ref=outs[t].at[me], send_sem=send_sems.at[sem], recv_sem=recv_sems.at[sem],
                    device_id=(px, py, c), device_id_type=MESH))
                recvs.append(pltpu.make_async_remote_copy(
                    src_ref=srcs[t], dst_ref=outs[t].at[2 * px + py], send_sem=send_sems.at[sem],
                    recv_sem=recv_sems.at[sem], device_id=(px, py, c), device_id_type=MESH))
        for cp in sends:
            cp.start()
        for cp in recvs:
            cp.wait_recv()
        for cp in sends:
            cp.wait_send()
        for cp in local:
            cp.wait()

    return pl.pallas_call(
        body, name=name,
        in_specs=[ANY] * n, out_specs=[ANY] * n,
        out_shape=[jax.ShapeDtypeStruct((N_CHIPS,) + p.shape, p.dtype) for p in packs],
        scratch_shapes=[pltpu.SemaphoreType.DMA((3 * n,)), pltpu.SemaphoreType.DMA((3 * n,)),
                        pltpu.SemaphoreType.DMA((n,))],
    )(*packs)


def _scatter_chips(g, name):
    _, R, Wd = g.shape

    def body(g_ref, o_ref, send_sems, recv_sems):
        x, y, c = _coords()
        sends, recvs = [], []
        for k, (px, py) in enumerate(_chip_peers(x, y)):
            sends.append(pltpu.make_async_remote_copy(
                src_ref=g_ref.at[2 * px + py], dst_ref=o_ref.at[k], send_sem=send_sems.at[k],
                recv_sem=recv_sems.at[k], device_id=(px, py, c), device_id_type=MESH))
            recvs.append(sends[-1])
        for cp in sends:
            cp.start()
        for cp in recvs:
            cp.wait_recv()
        for cp in sends:
            cp.wait_send()

    return pl.pallas_call(
        body, name=name, in_specs=[ANY], out_specs=ANY,
        out_shape=jax.ShapeDtypeStruct((3, R, Wd), g.dtype),
        scratch_shapes=[pltpu.SemaphoreType.DMA((3,)), pltpu.SemaphoreType.DMA((3,))],
    )(g)


def _swap_sibling(p, name):
    def body(p_ref, o_ref, send_sem, recv_sem):
        x, y, c = _coords()
        cp = pltpu.make_async_remote_copy(src_ref=p_ref, dst_ref=o_ref, send_sem=send_sem, recv_sem=recv_sem,
                                          device_id=(x, y, 1 - c), device_id_type=MESH)
        cp.start()
        cp.wait_recv()
        cp.wait_send()

    return pl.pallas_call(
        body, name=name, in_specs=[ANY], out_specs=ANY,
        out_shape=jax.ShapeDtypeStruct(p.shape, p.dtype),
        scratch_shapes=[pltpu.SemaphoreType.DMA, pltpu.SemaphoreType.DMA],
    )(p)


def _allreduce_small(v, name):
    R, Wd = v.shape

    def body(v_ref, o_ref, buf, send_sems, recv_sems):
        x, y, c = _coords()
        o_ref[...] = v_ref[...]
        for st, peer in enumerate([(x, y, 1 - c), (x, 1 - y, c), (1 - x, y, c)]):
            cp = pltpu.make_async_remote_copy(src_ref=o_ref, dst_ref=buf.at[st], send_sem=send_sems.at[st],
                                              recv_sem=recv_sems.at[st], device_id=peer, device_id_type=MESH)
            cp.start()
            cp.wait_recv()
            cp.wait_send()
            o_ref[...] = o_ref[...] + buf[st]

    vm = pl.BlockSpec(memory_space=pltpu.VMEM)
    return pl.pallas_call(
        body, name=name, in_specs=[vm], out_specs=vm,
        out_shape=jax.ShapeDtypeStruct((R, Wd), F32),
        scratch_shapes=[pltpu.VMEM((3, R, Wd), F32), pltpu.SemaphoreType.DMA((3,)), pltpu.SemaphoreType.DMA((3,))],
    )(v)


def _sum_partials(g, recv, *, tr, name):
    _, R, Wd = g.shape
    me = (2 * lax.axis_index("x") + lax.axis_index("y")).astype(jnp.int32).reshape(1)

    def body(me_ref, g_ref, r_ref, o_ref):
        o_ref[...] = ((g_ref[...] + r_ref[0]) + r_ref[1]) + r_ref[2]

    grid_spec = pltpu.PrefetchScalarGridSpec(
        num_scalar_prefetch=1, grid=(R // tr,),
        in_specs=[pl.BlockSpec((None, tr, Wd), lambda i, me: (me[0], i, 0)),
                  pl.BlockSpec((3, tr, Wd), lambda i, me: (0, i, 0))],
        out_specs=pl.BlockSpec((tr, Wd), lambda i, me: (i, 0)))
    return pl.pallas_call(
        body, name=name, grid_spec=grid_spec, out_shape=jax.ShapeDtypeStruct((R, Wd), F32),
        compiler_params=_cp(("parallel",)),
    )(me, g, recv)


def _adamw(w, m, v, ga, gb, *, tr, name):
    R, Wd = w.shape
    two = gb is not None
    c1 = 1.0 / (1.0 - ADAM_B1 ** ADAM_STEP)
    c2 = 1.0 / (1.0 - ADAM_B2 ** ADAM_STEP)

    def body(*refs):
        if two:
            w_ref, m_ref, v_ref, ga_ref, gb_ref, g_ref, d_ref, mo_ref, vo_ref = refs
            g = ga_ref[...] + gb_ref[...]
        else:
            w_ref, m_ref, v_ref, ga_ref, g_ref, d_ref, mo_ref, vo_ref = refs
            g = ga_ref[...]
        m2 = ADAM_B1 * m_ref[...] + (1.0 - ADAM_B1) * g
        v2 = ADAM_B2 * v_ref[...] + (1.0 - ADAM_B2) * (g * g)
        g_ref[...] = g
        mo_ref[...] = m2
        vo_ref[...] = v2
        d_ref[...] = -ADAM_LR * ((m2 * c1) / (jnp.sqrt(v2 * c2) + ADAM_EPS) + ADAM_WD * w_ref[...])

    blk = pl.BlockSpec((tr, Wd), lambda i: (i, 0))
    args = [w, m, v, ga] + ([gb] if two else [])
    return pl.pallas_call(
        body, name=name, grid=(R // tr,), in_specs=[blk] * len(args), out_specs=[blk] * 4,
        out_shape=[jax.ShapeDtypeStruct((R, Wd), F32)] * 4, compiler_params=_cp(("parallel",)),
    )(*args)


SHARDED = [
    ("ffn_w_up", 2, True), ("ffn_conv_w", 2, False), ("ffn_w_down", 1, True),
    ("lru_w_in", 2, True), ("lru_conv_w", 2, False), ("lru_conv_b", 1, False),
    ("lru_w_a", 2, True), ("lru_b_a", 2, False), ("lru_w_x", 2, True), ("lru_b_x", 2, False),
    ("lru_lambda", 1, False), ("lru_w_out", 1, True),
    ("mla_w_in", 2, True), ("mla_w_uq", 2, True), ("mla_w_ukv", 2, True), ("mla_w_out", 1, True),
    ("ret_w_in", 2, True), ("ret_gn_g", 1, False), ("ret_w_out", 1, True),
]
REPLICATED = ["ln1_g", "ln1_b", "ln2_g", "ln2_b", "ffn_conv_b", "mla_q_norm", "mla_kv_norm"]
WEIGHTS = ["ln1_g", "ln1_b", "ln2_g", "ln2_b", "ffn_w_up", "ffn_conv_w", "ffn_conv_b", "ffn_w_down", "lru_w_in",
           "lru_conv_w", "lru_conv_b", "lru_w_a", "lru_b_a", "lru_w_x", "lru_b_x", "lru_lambda", "lru_w_out",
           "mla_w_in", "mla_q_norm", "mla_kv_norm", "mla_w_uq", "mla_w_ukv", "mla_w_out", "ret_w_in", "ret_gn_g",
           "ret_w_out"]
PACK_ROWS = 512


def _pack(arrs, dtype, lead=(), rows=PACK_ROWS):
    nl = len(lead)
    flat = jnp.concatenate([a.astype(dtype).reshape(lead + (-1,)) for a in arrs], axis=nl)
    n = flat.shape[nl]
    quantum = rows * PACK_W
    total = -(-n // quantum) * quantum
    flat = jnp.pad(flat, [(0, 0)] * nl + [(0, total - n)])
    return flat.reshape(lead + (total // PACK_W, PACK_W))


def _unpack(buf, shapes, lead=()):
    nl = len(lead)
    flat = buf.reshape(lead + (-1,))
    out, off = [], 0
    for shp in shapes:
        n = int(np.prod(shp))
        out.append(lax.slice_in_dim(flat, off, off + n, axis=nl).reshape(lead + tuple(shp)))
        off += n
    return out


def _layer_params(full, rep, i):
    kind, j = i % 3, i // 3
    ffn = dict(w_up=full["ffn_w_up"][i], conv_w=full["ffn_conv_w"][i], conv_b=rep["ffn_conv_b"][i],
               w_down=full["ffn_w_down"][i])
    if kind == 0:
        mix = dict(w_in=full["lru_w_in"][j], conv_w=full["lru_conv_w"][j], conv_b=full["lru_conv_b"][j],
                   w_a=full["lru_w_a"][j], b_a=full["lru_b_a"][j], w_x=full["lru_w_x"][j], b_x=full["lru_b_x"][j],
                   lam=full["lru_lambda"][j], w_out=full["lru_w_out"][j])
    elif kind == 1:
        w_in_p, w_uq_p, w_ukv_p = _mla_permute_weights(full["mla_w_in"][j], full["mla_w_uq"][j], full["mla_w_ukv"][j])
        mix = dict(w_in_p=w_in_p, w_uq_p=w_uq_p, w_ukv_p=w_ukv_p, q_norm=rep["mla_q_norm"][j],
                   kv_norm=rep["mla_kv_norm"][j], w_out=full["mla_w_out"][j])
    else:
        mix = dict(w_in=full["ret_w_in"][j], gn_g=full["ret_gn_g"][j], w_out=full["ret_w_out"][j])
    return kind, mix, ffn


_MIX_FWD = {0: lambda x, pos, p, tag: _lru_mixer_fwd(x, p, tag), 1: _mla_mixer_fwd, 2: _ret_mixer_fwd}
_MIX_BWD = {0: _lru_mixer_bwd, 1: _mla_mixer_bwd, 2: _ret_mixer_bwd}
_MIX_PREFIX = {0: "lru_", 1: "mla_", 2: "ret_"}
_MIX_KEYS = {0: {"w_in": "lru_w_in", "conv_w": "lru_conv_w", "conv_b": "lru_conv_b", "w_a": "lru_w_a", "b_a": "lru_b_a",
                 "w_x": "lru_w_x", "b_x": "lru_b_x", "lam": "lru_lambda", "w_out": "lru_w_out"},
             1: {"w_in": "mla_w_in", "q_norm": "mla_q_norm", "kv_norm": "mla_kv_norm", "w_uq": "mla_w_uq",
                 "w_ukv": "mla_w_ukv", "w_out": "mla_w_out"},
             2: {"w_in": "ret_w_in", "gn_g": "ret_gn_g", "w_out": "ret_w_out"}}
_FFN_KEYS = {"w_up": "ffn_w_up", "conv_w": "ffn_conv_w", "conv_b": "ffn_conv_b", "w_down": "ffn_w_down"}


def _local_step(x, pos, target, full, rep):
    S, D = x.shape
    ts = _rt(S, 256)
    acts = []
    h = x
    for i in range(DEPTH):
        kind, mp, fp = _layer_params(full, rep, i)
        tag = "l%d" % i
        mix, msaved = _MIX_FWD[kind](h, pos, mp, tag + "m")
        h1, z1 = _ln_fwd(h, mix, rep["ln1_g"][i], rep["ln1_b"][i], ts=ts, name=tag + "_ln1")
        f, fsaved = _ffn_fwd(h1, fp, tag + "f")
        h2, z2 = _ln_fwd(h1, f, rep["ln2_g"][i], rep["ln2_b"][i], ts=ts, name=tag + "_ln2")
        acts.append((kind, mp, fp, h, msaved, h1, z1, fsaved, z2))
        h = h2
    dy, part = _loss_head(h, target, ts=ts, name="loss_head")

    grads = {n: {} for n in WEIGHTS}
    d_a, d_b = dy, None
    for i in reversed(range(DEPTH)):
        kind, mp, fp, h_in, msaved, h1, z1, fsaved, z2 = acts[i]
        tag = "l%d" % i
        dz2, dg8, db8 = _ln_bwd(d_a, d_b, z2, rep["ln2_g"][i], ts=ts, name=tag + "_ln2b")
        grads["ln2_g"][i], grads["ln2_b"][i] = dg8.sum(axis=0), db8.sum(axis=0)
        dx_f, gf = _ffn_bwd(dz2, h1, fp, fsaved, tag + "f")
        for k, v in gf.items():
            grads[_FFN_KEYS[k]][i] = v
        dz1, dg8, db8 = _ln_bwd(dz2, dx_f, z1, rep["ln1_g"][i], ts=ts, name=tag + "_ln1b")
        grads["ln1_g"][i], grads["ln1_b"][i] = dg8.sum(axis=0), db8.sum(axis=0)
        dx_m, gm = _MIX_BWD[kind](dz1, h_in, mp, msaved, tag + "m")
        for k, v in gm.items():
            grads[_MIX_KEYS[kind][k]][i // 3] = v
        d_a, d_b = dz1, dx_m
    grad_x = _axpy(d_a, d_b, ts=ts, name="grad_x")
    stacked = {n: jnp.stack([grads[n][j] for j in sorted(grads[n])]) for n in WEIGHTS}
    return part, grad_x, stacked


def kernel(x, positions, ln1_g, ln1_b, ln2_g, ln2_b, ffn_w_up, ffn_conv_w, ffn_conv_b, ffn_w_down, lru_w_in, lru_conv_w, lru_conv_b, lru_w_a, lru_b_a, lru_w_x, lru_b_x, lru_lambda, lru_w_out, mla_w_in, mla_q_norm, mla_kv_norm, mla_w_uq, mla_w_ukv, mla_w_out, ret_w_in, ret_gn_g, ret_w_out, loss_target, m_ln1_g, m_ln1_b, m_ln2_g, m_ln2_b, m_ffn_w_up, m_ffn_conv_w, m_ffn_conv_b, m_ffn_w_down, m_lru_w_in, m_lru_conv_w, m_lru_conv_b, m_lru_w_a, m_lru_b_a, m_lru_w_x, m_lru_b_x, m_lru_lambda, m_lru_w_out, m_mla_w_in, m_mla_q_norm, m_mla_kv_norm, m_mla_w_uq, m_mla_w_ukv, m_mla_w_out, m_ret_w_in, m_ret_gn_g, m_ret_w_out, v_ln1_g, v_ln1_b, v_ln2_g, v_ln2_b, v_ffn_w_up, v_ffn_conv_w, v_ffn_conv_b, v_ffn_w_down, v_lru_w_in, v_lru_conv_w, v_lru_conv_b, v_lru_w_a, v_lru_b_a, v_lru_w_x, v_lru_b_x, v_lru_lambda, v_lru_w_out, v_mla_w_in, v_mla_q_norm, v_mla_kv_norm, v_mla_w_uq, v_mla_w_ukv, v_mla_w_out, v_ret_w_in, v_ret_gn_g, v_ret_w_out):
    w = dict(ln1_g=ln1_g, ln1_b=ln1_b, ln2_g=ln2_g, ln2_b=ln2_b, ffn_w_up=ffn_w_up, ffn_conv_w=ffn_conv_w, ffn_conv_b=ffn_conv_b, ffn_w_down=ffn_w_down, lru_w_in=lru_w_in, lru_conv_w=lru_conv_w, lru_conv_b=lru_conv_b, lru_w_a=lru_w_a, lru_b_a=lru_b_a, lru_w_x=lru_w_x, lru_b_x=lru_b_x, lru_lambda=lru_lambda, lru_w_out=lru_w_out, mla_w_in=mla_w_in, mla_q_norm=mla_q_norm, mla_kv_norm=mla_kv_norm, mla_w_uq=mla_w_uq, mla_w_ukv=mla_w_ukv, mla_w_out=mla_w_out, ret_w_in=ret_w_in, ret_gn_g=ret_gn_g, ret_w_out=ret_w_out)
    m = dict(ln1_g=m_ln1_g, ln1_b=m_ln1_b, ln2_g=m_ln2_g, ln2_b=m_ln2_b, ffn_w_up=m_ffn_w_up, ffn_conv_w=m_ffn_conv_w, ffn_conv_b=m_ffn_conv_b, ffn_w_down=m_ffn_w_down, lru_w_in=m_lru_w_in, lru_conv_w=m_lru_conv_w, lru_conv_b=m_lru_conv_b, lru_w_a=m_lru_w_a, lru_b_a=m_lru_b_a, lru_w_x=m_lru_w_x, lru_b_x=m_lru_b_x, lru_lambda=m_lru_lambda, lru_w_out=m_lru_w_out, mla_w_in=m_mla_w_in, mla_q_norm=m_mla_q_norm, mla_kv_norm=m_mla_kv_norm, mla_w_uq=m_mla_w_uq, mla_w_ukv=m_mla_w_ukv, mla_w_out=m_mla_w_out, ret_w_in=m_ret_w_in, ret_gn_g=m_ret_gn_g, ret_w_out=m_ret_w_out)
    v = dict(ln1_g=v_ln1_g, ln1_b=v_ln1_b, ln2_g=v_ln2_g, ln2_b=v_ln2_b, ffn_w_up=v_ffn_w_up, ffn_conv_w=v_ffn_conv_w, ffn_conv_b=v_ffn_conv_b, ffn_w_down=v_ffn_w_down, lru_w_in=v_lru_w_in, lru_conv_w=v_lru_conv_w, lru_conv_b=v_lru_conv_b, lru_w_a=v_lru_w_a, lru_b_a=v_lru_b_a, lru_w_x=v_lru_w_x, lru_b_x=v_lru_b_x, lru_lambda=v_lru_lambda, lru_w_out=v_lru_w_out, mla_w_in=v_mla_w_in, mla_q_norm=v_mla_q_norm, mla_kv_norm=v_mla_kv_norm, mla_w_uq=v_mla_w_uq, mla_w_ukv=v_mla_w_ukv, mla_w_out=v_mla_w_out, ret_w_in=v_ret_w_in, ret_gn_g=v_ret_gn_g, ret_w_out=v_ret_w_out)
    D = x.shape[-1]

    big = [n for n, _, mx in SHARDED if mx]
    small = [n for n, _, mx in SHARDED if not mx]
    g_big, g_small = _gather_chips([_pack([w[n] for n in big], BF16), _pack([w[n] for n in small], F32)], "gather_weights")
    axis_of = {n: ax for n, ax, _ in SHARDED}
    full = {}
    for names, buf in ((big, g_big), (small, g_small)):
        blocks = _unpack(buf, [w[n].shape for n in names], lead=(N_CHIPS,))
        for n, blk in zip(names, blocks):
            full[n] = jnp.concatenate([blk[s] for s in range(N_CHIPS)], axis=axis_of[n])
    rep = {n: w[n] for n in REPLICATED}

    part, grad_x, grads = _local_step(x[0], positions[0], loss_target[0], full, rep)
    loss = lax.psum((0.5 / D) * jnp.sum(part), MESH_AXES)

    names = [n for n, _, _ in SHARDED]
    g_pack = _pack([jnp.stack(jnp.split(grads[n], N_CHIPS, axis=axis_of[n])) for n in names], F32, lead=(N_CHIPS,))
    recv = _scatter_chips(g_pack, "scatter_grads")
    p_mine = _sum_partials(g_pack, recv, tr=PACK_ROWS, name="sum_chip_partials")
    p_sib = _swap_sibling(p_mine, "swap_core_partials")
    shapes = [w[n].shape for n in names]
    outs = _adamw(_pack([w[n] for n in names], F32), _pack([m[n] for n in names], F32),
                  _pack([v[n] for n in names], F32), p_mine, p_sib, tr=PACK_ROWS, name="adamw_sharded")
    res = {kind: dict(zip(names, _unpack(o, shapes))) for kind, o in zip(("g", "d", "m", "v"), outs)}

    r_shapes = [w[n].shape for n in REPLICATED]
    rpack = lambda d: _pack([d[n] for n in REPLICATED], F32, rows=SUBLANES)
    r_sum = _allreduce_small(rpack(grads), "allreduce_replicated")
    r_outs = _adamw(rpack(w), rpack(m), rpack(v), r_sum, None, tr=r_sum.shape[0], name="adamw_replicated")
    for kind, o in zip(("g", "d", "m", "v"), r_outs):
        res[kind].update(zip(REPLICATED, _unpack(o, r_shapes)))

    return (loss, grad_x[None], *[res["g"][n] for n in WEIGHTS], *[res["d"][n] for n in WEIGHTS],
            *[res["m"][n] for n in WEIGHTS], *[res["v"][n] for n in WEIGHTS])
```

```python
import functools
import math

import numpy as np
import jax
import jax.numpy as jnp
from jax import lax
from jax.experimental import pallas as pl
from jax.experimental.pallas import tpu as pltpu

F32 = jnp.float32
BF16 = jnp.bfloat16

DEPTH = 4
ALPHA = (2.0 * DEPTH) ** 0.25
LN_EPS = 1e-5
RMS_EPS = 1e-6
ROPE_BASE = 10000.0
CHUNK = 64
LRU_C = 8.0
LRU_GROUPS = 4
MLA_HEADS = 16
QK_NOPE, QK_ROPE, V_HEAD = 64, 32, 64
Q_LORA, KV_LORA = 768, 256
RET_HEADS = 4
ADAM_LR, ADAM_B1, ADAM_B2, ADAM_EPS, ADAM_WD, ADAM_STEP = 0.001, 0.9, 0.999, 1e-08, 0.01, 10

LANES = 128
SUBLANES = 8
VMEM_LIMIT = 56 * 1024 * 1024

MESH_AXES = ("x", "y", "c")
N_CHIPS = 4


def _cp(sem):
    return pltpu.CompilerParams(dimension_semantics=sem, vmem_limit_bytes=VMEM_LIMIT)


def _sigmoid(x):
    return 1.0 / (1.0 + jnp.exp(-x))


_GELU_C = math.sqrt(2.0 / math.pi)


def _gelu_parts(x):
    x2 = x * x
    u = _GELU_C * (x + 0.044715 * x * x2)
    t = jnp.tanh(u)
    g = 0.5 * x * (1.0 + t)
    dg = 0.5 * (1.0 + t) + 0.5 * x * (1.0 - t * t) * _GELU_C * (1.0 + 3.0 * 0.044715 * x2)
    return g, dg


def _fold8(v):
    n = v.shape[0] // SUBLANES
    return v.reshape(n, SUBLANES, v.shape[1]).sum(axis=0)


def _dot(a, b):
    return jnp.dot(a, b, preferred_element_type=F32)


def _dot_tb(a, b):
    return lax.dot_general(a, b, (((1,), (1,)), ((), ())), preferred_element_type=F32)


def _dot_ta(a, b):
    return lax.dot_general(a, b, (((0,), (0,)), ((), ())), preferred_element_type=F32)


def _mm(a, b, *, out_dtype, tm, tn, name, a_koff=0):
    M = a.shape[0]
    K, N = b.shape

    def body(a_ref, b_ref, o_ref):
        o_ref[...] = _dot(a_ref[...].astype(BF16), b_ref[...].astype(BF16)).astype(out_dtype)

    return pl.pallas_call(
        body, name=name, grid=(M // tm, N // tn),
        in_specs=[pl.BlockSpec((tm, K), lambda i, j: (i, a_koff)),
                  pl.BlockSpec((K, tn), lambda i, j: (0, j))],
        out_specs=pl.BlockSpec((tm, tn), lambda i, j: (i, j)),
        out_shape=jax.ShapeDtypeStruct((M, N), out_dtype),
        compiler_params=_cp(("parallel", "parallel")),
    )(a, b)


def _mm_tb(pairs, b, *, out_dtype, tm, tk, name):
    M = pairs[0][0].shape[0]
    Kout = b.shape[0]
    n = len(pairs)

    def body(*refs):
        a_refs, b_refs, o_ref = refs[:n], refs[n:2 * n], refs[2 * n]
        acc = None
        for a_ref, b_ref in zip(a_refs, b_refs):
            t = _dot_tb(a_ref[...].astype(BF16), b_ref[...].astype(BF16))
            acc = t if acc is None else acc + t
        o_ref[...] = acc.astype(out_dtype)

    in_specs = [pl.BlockSpec((tm, a.shape[1]), lambda i, j: (i, 0)) for a, _ in pairs]
    for a, c0 in pairs:
        w = a.shape[1]
        assert c0 % w == 0
        in_specs.append(pl.BlockSpec((tk, w), functools.partial(lambda i, j, cb: (j, cb), cb=c0 // w)))
    return pl.pallas_call(
        body, name=name, grid=(M // tm, Kout // tk),
        in_specs=in_specs,
        out_specs=pl.BlockSpec((tm, tk), lambda i, j: (i, j)),
        out_shape=jax.ShapeDtypeStruct((M, Kout), out_dtype),
        compiler_params=_cp(("parallel", "parallel")),
    )(*[a for a, _ in pairs], *[b for _ in pairs])


def _mm_ta(a, b, *, tk, tn, tm, name, a_c0=0, a_w=None, b_c0=0, b_w=None, dest=None):
    M = a.shape[0]
    a_w = a.shape[1] if a_w is None else a_w
    b_w = b.shape[1] if b_w is None else b_w
    assert a_c0 % tk == 0 and b_c0 % tn == 0 and a_w % tk == 0 and b_w % tn == 0

    def body(*refs):
        a_ref, b_ref, o_ref = refs[0], refs[1], refs[-1]

        @pl.when(pl.program_id(2) == 0)
        def _():
            o_ref[...] = jnp.zeros_like(o_ref)
        o_ref[...] += _dot_ta(a_ref[...].astype(BF16), b_ref[...].astype(BF16))

    in_specs = [pl.BlockSpec((tm, tk), lambda i, j, m: (m, i + a_c0 // tk)),
                pl.BlockSpec((tm, tn), lambda i, j, m: (m, j + b_c0 // tn))]
    args = [a, b]
    if dest is None:
        out_spec = pl.BlockSpec((tk, tn), lambda i, j, m: (i, j))
        out_shape = jax.ShapeDtypeStruct((a_w, b_w), F32)
        aliases = {}
    else:
        buf, full_shape, layer, r0, c0 = dest
        assert r0 % tk == 0 and c0 % tn == 0
        out_spec = pl.BlockSpec((None, tk, tn), lambda i, j, m: (layer, i + r0 // tk, j + c0 // tn))
        out_shape = jax.ShapeDtypeStruct(full_shape, F32)
        aliases = {}
        if buf is not None:
            in_specs.append(pl.BlockSpec(memory_space=pl.ANY))
            args.append(buf)
            aliases = {2: 0}
    return pl.pallas_call(
        body, name=name, grid=(a_w // tk, b_w // tn, M // tm),
        in_specs=in_specs, out_specs=out_spec, out_shape=out_shape, input_output_aliases=aliases,
        compiler_params=_cp(("parallel", "parallel", "arbitrary")),
    )(*args)


def _grad_into(gbuf, key, full_shape, layer, r0, c0, a, b, **kw):
    gbuf[key] = _mm_ta(a, b, dest=(gbuf.get(key), full_shape, layer, r0, c0), **kw)


def _ln_fwd(x, mix, g, b, *, ts, name):
    S, D = x.shape

    def body(x_ref, m_ref, g_ref, b_ref, o_ref, z_ref):
        z = ALPHA * x_ref[...] + m_ref[...]
        mu = jnp.mean(z, axis=-1, keepdims=True)
        zc = z - mu
        var = jnp.mean(zc * zc, axis=-1, keepdims=True)
        o_ref[...] = zc * lax.rsqrt(var + LN_EPS) * g_ref[...] + b_ref[...]
        z_ref[...] = z

    row = pl.BlockSpec((ts, D), lambda i: (i, 0))
    vec = pl.BlockSpec((1, D), lambda i: (0, 0))
    return pl.pallas_call(
        body, name=name, grid=(S // ts,),
        in_specs=[row, row, vec, vec], out_specs=[row, row],
        out_shape=[jax.ShapeDtypeStruct((S, D), F32)] * 2,
        compiler_params=_cp(("parallel",)),
    )(x, mix, g.reshape(1, D), b.reshape(1, D))


def _ln_bwd(da, db, z, g, *, ts, name):
    S, D = z.shape
    two = db is not None

    def body(*refs):
        if two:
            da_ref, db_ref, z_ref, g_ref, dz_ref, dg_ref, dbias_ref = refs
            dout = ALPHA * da_ref[...] + db_ref[...]
        else:
            da_ref, z_ref, g_ref, dz_ref, dg_ref, dbias_ref = refs
            dout = da_ref[...]

        @pl.when(pl.program_id(0) == 0)
        def _():
            dg_ref[...] = jnp.zeros_like(dg_ref)
            dbias_ref[...] = jnp.zeros_like(dbias_ref)

        z = z_ref[...]
        mu = jnp.mean(z, axis=-1, keepdims=True)
        zc = z - mu
        var = jnp.mean(zc * zc, axis=-1, keepdims=True)
        rstd = lax.rsqrt(var + LN_EPS)
        xhat = zc * rstd
        dxh = dout * g_ref[...]
        m1 = jnp.mean(dxh, axis=-1, keepdims=True)
        m2 = jnp.mean(dxh * xhat, axis=-1, keepdims=True)
        dz_ref[...] = rstd * (dxh - m1 - xhat * m2)
        dg_ref[...] += _fold8(dout * xhat)
        dbias_ref[...] += _fold8(dout)

    row = pl.BlockSpec((ts, D), lambda i: (i, 0))
    vec = pl.BlockSpec((1, D), lambda i: (0, 0))
    acc = pl.BlockSpec((SUBLANES, D), lambda i: (0, 0))
    args = [da, db, z, g.reshape(1, D)] if two else [da, z, g.reshape(1, D)]
    return pl.pallas_call(
        body, name=name, grid=(S // ts,),
        in_specs=[row] * (3 if two else 2) + [vec],
        out_specs=[row, acc, acc],
        out_shape=[jax.ShapeDtypeStruct((S, D), F32), jax.ShapeDtypeStruct((SUBLANES, D), F32),
                   jax.ShapeDtypeStruct((SUBLANES, D), F32)],
        compiler_params=_cp(("arbitrary",)),
    )(*args)


def _loss_head(y, t, *, ts, name):
    S, D = y.shape

    def body(y_ref, t_ref, dy_ref, p_ref):
        @pl.when(pl.program_id(0) == 0)
        def _():
            p_ref[...] = jnp.zeros_like(p_ref)
        d = y_ref[...] - t_ref[...]
        dy_ref[...] = d * (1.0 / D)
        p_ref[...] += _fold8(d * d)

    row = pl.BlockSpec((ts, D), lambda i: (i, 0))
    acc = pl.BlockSpec((SUBLANES, D), lambda i: (0, 0))
    return pl.pallas_call(
        body, name=name, grid=(S // ts,),
        in_specs=[row, row], out_specs=[row, acc],
        out_shape=[jax.ShapeDtypeStruct((S, D), F32), jax.ShapeDtypeStruct((SUBLANES, D), F32)],
        compiler_params=_cp(("arbitrary",)),
    )(y, t)


def _axpy(a, b, *, ts, name):
    S, D = a.shape

    def body(a_ref, b_ref, o_ref):
        o_ref[...] = ALPHA * a_ref[...] + b_ref[...]

    row = pl.BlockSpec((ts, D), lambda i: (i, 0))
    return pl.pallas_call(
        body, name=name, grid=(S // ts,), in_specs=[row, row], out_specs=row,
        out_shape=jax.ShapeDtypeStruct((S, D), F32), compiler_params=_cp(("parallel",)),
    )(a, b)


def _prev_halo_spec(ts, tc, coff):
    r = ts // SUBLANES
    return pl.BlockSpec((SUBLANES, tc), lambda i, j: (jnp.maximum(i * r - 1, 0), j + coff))


def _fill_prev(buf, halo_ref, cur, i):
    buf[0:SUBLANES, :] = jnp.where(i > 0, halo_ref[...], 0.0)
    buf[SUBLANES:, :] = cur


def _conv_fwd(x, w, b, *, K, ts, tc, x_c0, name):
    S = x.shape[0]
    C = w.shape[1]
    coff = x_c0 // tc

    def body(x_ref, halo_ref, w_ref, b_ref, o_ref, buf):
        _fill_prev(buf, halo_ref, x_ref[...], pl.program_id(0))
        acc = b_ref[...] + w_ref[K - 1:K, :] * x_ref[...]
        for k in range(K - 1):
            acc = acc + w_ref[k:k + 1, :] * buf[pl.ds(SUBLANES - (K - 1) + k, ts), :]
        o_ref[...] = acc

    return pl.pallas_call(
        body, name=name, grid=(S // ts, C // tc),
        in_specs=[pl.BlockSpec((ts, tc), lambda i, j: (i, j + coff)), _prev_halo_spec(ts, tc, coff),
                  pl.BlockSpec((K, tc), lambda i, j: (0, j)), pl.BlockSpec((1, tc), lambda i, j: (0, j))],
        out_specs=pl.BlockSpec((ts, tc), lambda i, j: (i, j)),
        out_shape=jax.ShapeDtypeStruct((S, C), F32),
        scratch_shapes=[pltpu.VMEM((ts + SUBLANES, tc), F32)],
        compiler_params=_cp(("parallel", "parallel")),
    )(x, x, w, b.reshape(1, C))


def _conv_wgrad(dy, x, *, K, ts, tc, x_c0, name):
    S, C = dy.shape
    coff = x_c0 // tc

    def body(dy_ref, x_ref, halo_ref, dw_ref, db_ref, buf):
        i = pl.program_id(1)

        @pl.when(i == 0)
        def _():
            dw_ref[...] = jnp.zeros_like(dw_ref)
            db_ref[...] = jnp.zeros_like(db_ref)

        _fill_prev(buf, halo_ref, x_ref[...], i)
        dy_v = dy_ref[...]
        db_ref[...] += _fold8(dy_v)
        for k in range(K):
            xs = buf[pl.ds(SUBLANES - (K - 1) + k, ts), :]
            dw_ref[k] += _fold8(dy_v * xs)

    r = ts // SUBLANES
    return pl.pallas_call(
        body, name=name, grid=(C // tc, S // ts),
        in_specs=[pl.BlockSpec((ts, tc), lambda j, i: (i, j)),
                  pl.BlockSpec((ts, tc), lambda j, i: (i, j + coff)),
                  pl.BlockSpec((SUBLANES, tc), lambda j, i: (jnp.maximum(i * r - 1, 0), j + coff))],
        out_specs=[pl.BlockSpec((K, SUBLANES, tc), lambda j, i: (0, 0, j)),
                   pl.BlockSpec((SUBLANES, tc), lambda j, i: (0, j))],
        out_shape=[jax.ShapeDtypeStruct((K, SUBLANES, C), F32), jax.ShapeDtypeStruct((SUBLANES, C), F32)],
        scratch_shapes=[pltpu.VMEM((ts + SUBLANES, tc), F32)],
        compiler_params=_cp(("parallel", "arbitrary")),
    )(dy, x, x)


def _conv_bwd(dy, w, *, K, ts, tc, w_c0, out_dtype, name):
    S, C = dy.shape
    nb = S // ts
    r = ts // SUBLANES
    woff = w_c0 // tc

    def body(dy_ref, halo_ref, w_ref, o_ref, buf):
        i = pl.program_id(0)
        buf[0:ts, :] = dy_ref[...]
        buf[ts:, :] = jnp.where(i < nb - 1, halo_ref[...], 0.0)
        acc = w_ref[K - 1:K, :] * dy_ref[...]
        for k in range(K - 1):
            acc = acc + w_ref[k:k + 1, :] * buf[pl.ds(K - 1 - k, ts), :]
        o_ref[...] = acc.astype(out_dtype)

    return pl.pallas_call(
        body, name=name, grid=(nb, C // tc),
        in_specs=[pl.BlockSpec((ts, tc), lambda i, j: (i, j)),
                  pl.BlockSpec((SUBLANES, tc), lambda i, j: (jnp.minimum((i + 1) * r, nb * r - 1), j)),
                  pl.BlockSpec((K, tc), lambda i, j: (0, j + woff))],
        out_specs=pl.BlockSpec((ts, tc), lambda i, j: (i, j)),
        out_shape=jax.ShapeDtypeStruct((S, C), out_dtype),
        scratch_shapes=[pltpu.VMEM((ts + SUBLANES, tc), F32)],
        compiler_params=_cp(("parallel", "parallel")),
    )(dy, dy, w)


HALO16 = 16


def _ffn_conv(buf, w_ref, b_ref, n):
    taps = [buf[pl.ds(HALO16 - 2 + k, n), :] for k in range(3)]
    acc = b_ref[...] + w_ref[0:1, :] * taps[0]
    for k in (1, 2):
        acc = acc + w_ref[k:k + 1, :] * taps[k]
    return acc, taps


def _ffn_mid_specs(ts, tc, nf, nb, with_next):
    r = ts // HALO16
    specs = []
    for off in (0, nf):
        specs.append(pl.BlockSpec((ts, tc), functools.partial(lambda j, i, o: (i, j + o), o=off)))
        specs.append(pl.BlockSpec((HALO16, tc), functools.partial(lambda j, i, o: (jnp.maximum(i * r - 1, 0), j + o), o=off)))
        if with_next:
            specs.append(pl.BlockSpec(
                (HALO16, tc), functools.partial(lambda j, i, o: (jnp.minimum((i + 1) * r, nb * r - 1), j + o), o=off)))
    for rows in (3, 1):
        for off in (0, nf):
            specs.append(pl.BlockSpec((rows, tc), functools.partial(lambda j, i, o: (0, j + o), o=off)))
    return specs


def _ffn_mid_fwd(hpre, w, b, *, ts, tc, name):
    S, F2 = hpre.shape
    F = F2 // 2
    nf = F // tc

    def body(g_ref, gp_ref, u_ref, up_ref, wg_ref, wu_ref, bg_ref, bu_ref, o_ref, gbuf, ubuf):
        first = pl.program_id(1) == 0
        for buf, prev, cur in ((gbuf, gp_ref, g_ref), (ubuf, up_ref, u_ref)):
            buf[0:HALO16, :] = jnp.where(first, 0.0, prev[...].astype(F32))
            buf[HALO16:, :] = cur[...].astype(F32)
        gel, _ = _gelu_parts(_ffn_conv(gbuf, wg_ref, bg_ref, ts)[0])
        o_ref[...] = (gel * _ffn_conv(ubuf, wu_ref, bu_ref, ts)[0]).astype(BF16)

    b2 = b.reshape(1, F2)
    return pl.pallas_call(
        body, name=name, grid=(nf, S // ts),
        in_specs=_ffn_mid_specs(ts, tc, nf, S // ts, False),
        out_specs=pl.BlockSpec((ts, tc), lambda j, i: (i, j)),
        out_shape=jax.ShapeDtypeStruct((S, F), BF16),
        scratch_shapes=[pltpu.VMEM((ts + HALO16, tc), F32)] * 2,
        compiler_params=_cp(("parallel", "parallel")),
    )(hpre, hpre, hpre, hpre, w, w, b2, b2)


def _ffn_mid_bwd(hpre, da, w, b, *, ts, tc, name):
    S, F2 = hpre.shape
    F = F2 // 2
    nf = F // tc
    nb = S // ts
    r = ts // HALO16
    ne = ts + HALO16

    def body(g_ref, gp_ref, gn_ref, u_ref, up_ref, un_ref, wg_ref, wu_ref, bg_ref, bu_ref, da_ref, dan_ref,
             dpg_ref, dpu_ref, dwg_ref, dwu_ref, dbg_ref, dbu_ref, gbuf, ubuf, dabuf, dgbuf, dubuf):
        i = pl.program_id(1)

        @pl.when(i == 0)
        def _():
            for ref in (dwg_ref, dwu_ref, dbg_ref, dbu_ref):
                ref[...] = jnp.zeros_like(ref)

        for buf, prev, cur, nxt in ((gbuf, gp_ref, g_ref, gn_ref), (ubuf, up_ref, u_ref, un_ref)):
            buf[0:HALO16, :] = jnp.where(i == 0, 0.0, prev[...].astype(F32))
            buf[HALO16:HALO16 + ts, :] = cur[...].astype(F32)
            buf[HALO16 + ts:, :] = nxt[...].astype(F32)
        dabuf[0:ts, :] = da_ref[...].astype(F32)
        dabuf[ts:, :] = jnp.where(i == nb - 1, 0.0, dan_ref[...].astype(F32))

        gate, gtaps = _ffn_conv(gbuf, wg_ref, bg_ref, ne)
        up, utaps = _ffn_conv(ubuf, wu_ref, bu_ref, ne)
        gel, dgel = _gelu_parts(gate)
        da_v = dabuf[...]
        dgbuf[...] = da_v * up * dgel
        dubuf[...] = da_v * gel
        for dbuf, taps, w_ref, dp_ref, dw_ref, db_ref in ((dgbuf, gtaps, wg_ref, dpg_ref, dwg_ref, dbg_ref),
                                                          (dubuf, utaps, wu_ref, dpu_ref, dwu_ref, dbu_ref)):
            dh = dbuf[0:ts, :]
            db_ref[...] += _fold8(dh)
            acc = w_ref[2:3, :] * dh
            for k in range(3):
                dw_ref[k] += _fold8(dh * taps[k][0:ts, :])
                if k < 2:
                    acc = acc + w_ref[k:k + 1, :] * dbuf[pl.ds(2 - k, ts), :]
            dp_ref[...] = acc.astype(BF16)

    b2 = b.reshape(1, F2)
    blk = pl.BlockSpec((ts, tc), lambda j, i: (i, j))
    nxt = pl.BlockSpec((HALO16, tc), lambda j, i: (jnp.minimum((i + 1) * r, nb * r - 1), j))
    in_specs = _ffn_mid_specs(ts, tc, nf, nb, True) + [blk, nxt]
    out_specs = [blk, blk,
                 pl.BlockSpec((3, SUBLANES, tc), lambda j, i: (0, 0, j)), pl.BlockSpec((3, SUBLANES, tc), lambda j, i: (0, 0, j)),
                 pl.BlockSpec((SUBLANES, tc), lambda j, i: (0, j)), pl.BlockSpec((SUBLANES, tc), lambda j, i: (0, j))]
    return pl.pallas_call(
        body, name=name, grid=(nf, nb),
        in_specs=in_specs, out_specs=out_specs,
        out_shape=[jax.ShapeDtypeStruct((S, F), BF16)] * 2 + [jax.ShapeDtypeStruct((3, SUBLANES, F), F32)] * 2
                  + [jax.ShapeDtypeStruct((SUBLANES, F), F32)] * 2,
        scratch_shapes=[pltpu.VMEM((ts + 2 * HALO16, tc), F32)] * 2 + [pltpu.VMEM((ne, tc), F32)] * 3,
        compiler_params=_cp(("parallel", "arbitrary")),
    )(hpre, hpre, hpre, hpre, hpre, hpre, w, w, b2, b2, da, da)


def _expm1(x):
    u = jnp.exp(x)
    um1 = u - 1.0
    safe = jnp.where(um1 == 0.0, 1.0, jnp.log(u))
    r = jnp.where(um1 == 0.0, x, um1 * x / safe)
    return jnp.where(x < -30.0, -1.0, r)


def _softplus(z):
    return jnp.maximum(z, 0.0) + jnp.log1p(jnp.exp(-jnp.abs(z)))


def _lru_gates(u, wa_ref, ba_ref, wx_ref, bx_ref, lam_ref):
    ub = u.astype(BF16)
    r = _sigmoid(_dot(ub, wa_ref[...]) + ba_ref[...])
    ig = _sigmoid(_dot(ub, wx_ref[...]) + bx_ref[...])
    sp = _softplus(-lam_ref[...])
    la = -LRU_C * r * sp
    a = jnp.exp(la)
    mult = jnp.sqrt(-_expm1(2.0 * la))
    return ub, r, ig, sp, a, mult


def _lru_specs(ts, gw):
    blk = pl.BlockSpec((ts, gw), lambda g, i: (i, g))
    wsp = pl.BlockSpec((None, gw, gw), lambda g, i: (g, 0, 0))
    vsp = pl.BlockSpec((None, 1, gw), lambda g, i: (g, 0, 0))
    return blk, wsp, vsp


def _lru_fwd(u, proj, w_a, b_a, w_x, b_x, lam, *, ts, name):
    S, W = u.shape
    G = LRU_GROUPS
    gw = W // G
    nt = ts // SUBLANES

    def body(u_ref, gb_ref, wa_ref, ba_ref, wx_ref, bx_ref, lam_ref, y_ref, h_ref, a_buf, b_buf, carry):
        @pl.when(pl.program_id(1) == 0)
        def _():
            carry[...] = jnp.zeros_like(carry)

        u_v = u_ref[...]
        _, _, ig, _, a, mult = _lru_gates(u_v, wa_ref, ba_ref, wx_ref, bx_ref, lam_ref)
        a_buf[...] = a
        b_buf[...] = mult * ig * u_v
        row = lax.broadcasted_iota(jnp.int32, (SUBLANES, gw), 0)

        def tile(k, c):
            r0 = pl.multiple_of(k * SUBLANES, SUBLANES)
            A = a_buf[pl.ds(r0, SUBLANES), :]
            B = b_buf[pl.ds(r0, SUBLANES), :]
            for d in (1, 2, 4):
                m = row >= d
                B = jnp.where(m, A * pltpu.roll(B, d, 0) + B, B)
                A = jnp.where(m, A * pltpu.roll(A, d, 0), A)
            h = A * c + B
            h_ref[pl.ds(r0, SUBLANES), :] = h
            return h[SUBLANES - 1:SUBLANES, :]

        carry[...] = lax.fori_loop(0, nt, tile, carry[...])
        gel, _ = _gelu_parts(gb_ref[...])
        y_ref[...] = (gel * h_ref[...]).astype(BF16)

    blk, wsp, vsp = _lru_specs(ts, gw)
    return pl.pallas_call(
        body, name=name, grid=(G, S // ts),
        in_specs=[blk, blk, wsp, vsp, wsp, vsp, vsp],
        out_specs=[blk, blk],
        out_shape=[jax.ShapeDtypeStruct((S, W), BF16), jax.ShapeDtypeStruct((S, W), F32)],
        scratch_shapes=[pltpu.VMEM((ts, gw), F32), pltpu.VMEM((ts, gw), F32), pltpu.VMEM((1, gw), F32)],
        compiler_params=_cp(("parallel", "arbitrary")),
    )(u, proj, w_a, b_a.reshape(G, 1, gw), w_x, b_x.reshape(G, 1, gw), lam.reshape(G, 1, gw))


def _lru_bwd(dy, u, proj, h, w_a, b_a, w_x, b_x, lam, *, ts, name):
    S, W = u.shape
    G = LRU_GROUPS
    gw = W // G
    nt = ts // SUBLANES
    nb = S // ts
    r8 = ts // SUBLANES

    def body(dy_ref, u_ref, gb_ref, h_ref, hh_ref, wa_ref, ba_ref, wx_ref, bx_ref, lam_ref,
             dgb_ref, du_ref, dwa_ref, dwx_ref, dba_ref, dbx_ref, dlam_ref,
             a_buf, q_buf, p_buf, hbuf, carry):
        i = pl.program_id(1)
        ib = nb - 1 - i

        @pl.when(i == 0)
        def _():
            carry[...] = jnp.zeros_like(carry)
            dwa_ref[...] = jnp.zeros_like(dwa_ref)
            dwx_ref[...] = jnp.zeros_like(dwx_ref)
            dba_ref[...] = jnp.zeros_like(dba_ref)
            dbx_ref[...] = jnp.zeros_like(dbx_ref)
            dlam_ref[...] = jnp.zeros_like(dlam_ref)

        u_v = u_ref[...]
        ub, r, ig, sp, a, mult = _lru_gates(u_v, wa_ref, ba_ref, wx_ref, bx_ref, lam_ref)
        gel, dgel = _gelu_parts(gb_ref[...])
        dy_v = dy_ref[...]
        h_v = h_ref[...]
        dh = dy_v * gel
        dgb_ref[...] = (dy_v * h_v * dgel).astype(BF16)
        a_buf[...] = a
        q_buf[...] = a * dh
        row = lax.broadcasted_iota(jnp.int32, (SUBLANES, gw), 0)

        def tile(kk, c):
            r0 = pl.multiple_of((nt - 1 - kk) * SUBLANES, SUBLANES)
            A = a_buf[pl.ds(r0, SUBLANES), :]
            B = q_buf[pl.ds(r0, SUBLANES), :]
            for d in (1, 2, 4):
                m = row < SUBLANES - d
                B = jnp.where(m, A * pltpu.roll(B, SUBLANES - d, 0) + B, B)
                A = jnp.where(m, A * pltpu.roll(A, SUBLANES - d, 0), A)
            P = A * c + B
            p_buf[pl.ds(r0, SUBLANES), :] = jnp.where(row == SUBLANES - 1, c, pltpu.roll(P, SUBLANES - 1, 0))
            return P[0:1, :]

        carry[...] = lax.fori_loop(0, nt, tile, carry[...])
        Gt = dh + p_buf[...]
        hbuf[0:SUBLANES, :] = jnp.where(ib > 0, hh_ref[...], 0.0)
        hbuf[SUBLANES:, :] = h_v
        hprev = hbuf[pl.ds(SUBLANES - 1, ts), :]
        da = Gt * hprev
        dmult = Gt * (ig * u_v)
        dla = da * a - dmult * (a * a) / mult
        dr = dla * (-LRU_C * sp)
        dlam_ref[...] += _fold8(dla * (LRU_C * r)) * _sigmoid(-lam_ref[...])
        dig = Gt * mult * u_v
        dzr = dr * r * (1.0 - r)
        dzi = dig * ig * (1.0 - ig)
        dzr_b = dzr.astype(BF16)
        dzi_b = dzi.astype(BF16)
        du_ref[...] = Gt * mult * ig + _dot_tb(dzr_b, wa_ref[...]) + _dot_tb(dzi_b, wx_ref[...])
        dwa_ref[...] += _dot_ta(ub, dzr_b)
        dwx_ref[...] += _dot_ta(ub, dzi_b)
        dba_ref[...] += _fold8(dzr)
        dbx_ref[...] += _fold8(dzi)

    rblk = pl.BlockSpec((ts, gw), lambda g, i: (nb - 1 - i, g))
    halo = pl.BlockSpec((SUBLANES, gw), lambda g, i: (jnp.maximum((nb - 1 - i) * r8 - 1, 0), g))
    wsp = pl.BlockSpec((None, gw, gw), lambda g, i: (g, 0, 0))
    vsp = pl.BlockSpec((None, 1, gw), lambda g, i: (g, 0, 0))
    acc8 = pl.BlockSpec((None, SUBLANES, gw), lambda g, i: (g, 0, 0))
    return pl.pallas_call(
        body, name=name, grid=(G, nb),
        in_specs=[rblk, rblk, rblk, rblk, halo, wsp, vsp, wsp, vsp, vsp],
        out_specs=[rblk, rblk, wsp, wsp, acc8, acc8, acc8],
        out_shape=[jax.ShapeDtypeStruct((S, W), BF16), jax.ShapeDtypeStruct((S, W), F32),
                   jax.ShapeDtypeStruct((G, gw, gw), F32), jax.ShapeDtypeStruct((G, gw, gw), F32),
                   jax.ShapeDtypeStruct((G, SUBLANES, gw), F32), jax.ShapeDtypeStruct((G, SUBLANES, gw), F32),
                   jax.ShapeDtypeStruct((G, SUBLANES, gw), F32)],
        scratch_shapes=[pltpu.VMEM((ts, gw), F32)] * 3 + [pltpu.VMEM((ts + SUBLANES, gw), F32),
                                                          pltpu.VMEM((1, gw), F32)],
        compiler_params=_cp(("parallel", "arbitrary")),
    )(dy, u, proj, h, h, w_a, b_a.reshape(G, 1, gw), w_x, b_x.reshape(G, 1, gw), lam.reshape(G, 1, gw))


def _rt(S, pref):
    return min(S, pref)


def _lru_mixer_fwd(x, p, tag):
    S, D = x.shape
    W = p["w_out"].shape[0]
    ts = _rt(S, 512)
    proj = _mm(x, p["w_in"], out_dtype=F32, tm=ts, tn=W, name=tag + "_proj")
    u = _conv_fwd(proj, p["conv_w"], p["conv_b"], K=4, ts=ts, tc=512, x_c0=W, name=tag + "_conv")
    y, h = _lru_fwd(u, proj, p["w_a"], p["b_a"], p["w_x"], p["b_x"], p["lam"], ts=ts, name=tag + "_scan")
    mix = _mm(y, p["w_out"], out_dtype=F32, tm=ts, tn=D, name=tag + "_out")
    return mix, (proj, u, h, y)


def _lru_mixer_bwd(dmix, x, p, saved, tag, gbuf, j):
    proj, u, h, y = saved
    S, D = x.shape
    W = p["w_out"].shape[0]
    ts = _rt(S, 512)
    g = {}
    dy = _mm_tb([(dmix, 0)], p["w_out"], out_dtype=F32, tm=ts, tk=W, name=tag + "_dy")
    n_lru = (DEPTH + 2) // 3
    _grad_into(gbuf, "lru_w_out", (n_lru, W, D), j, 0, 0, y, dmix, tk=W, tn=D, tm=ts, name=tag + "_dwout")
    dgb, du, g["w_a"], g["w_x"], dba8, dbx8, dlam8 = _lru_bwd(
        dy, u, proj, h, p["w_a"], p["b_a"], p["w_x"], p["b_x"], p["lam"], ts=ts, name=tag + "_scanb")
    g["b_a"] = dba8.sum(axis=1)
    g["b_x"] = dbx8.sum(axis=1)
    g["lam"] = dlam8.sum(axis=1).reshape(-1)
    dcw8, dcb8 = _conv_wgrad(du, proj, K=4, ts=ts, tc=512, x_c0=W, name=tag + "_convw")
    g["conv_w"] = dcw8.sum(axis=1)
    g["conv_b"] = dcb8.sum(axis=0)
    drnn = _conv_bwd(du, p["conv_w"], K=4, ts=ts, tc=512, w_c0=0, out_dtype=BF16, name=tag + "_convb")
    dx = _mm_tb([(dgb, 0), (drnn, W)], p["w_in"], out_dtype=F32, tm=ts, tk=D, name=tag + "_dx")
    _grad_into(gbuf, "lru_w_in", (n_lru, D, 2 * W), j, 0, 0, x, dgb, tk=D, tn=W, tm=ts, name=tag + "_dwin_g")
    _grad_into(gbuf, "lru_w_in", (n_lru, D, 2 * W), j, 0, W, x, drnn, tk=D, tn=W, tm=ts, name=tag + "_dwin_r")
    return dx, g


def _ffn_fwd(x, p, tag):
    S, D = x.shape
    F = p["w_down"].shape[0]
    ts = _rt(S, 512)
    tc = F // 2
    hpre = _mm(x, p["w_up"], out_dtype=BF16, tm=ts, tn=tc, name=tag + "_up")
    a = _ffn_mid_fwd(hpre, p["conv_w"], p["conv_b"], ts=_rt(S, 256), tc=tc, name=tag + "_mid")
    f = _mm(a, p["w_down"], out_dtype=F32, tm=ts, tn=D, name=tag + "_down")
    return f, (hpre, a)


def _ffn_bwd(df, x, p, saved, tag, gbuf, i):
    hpre, a = saved
    S, D = x.shape
    F = p["w_down"].shape[0]
    ts = _rt(S, 512)
    tc = F // 2
    g = {}
    da = _mm_tb([(df, 0)], p["w_down"], out_dtype=BF16, tm=ts, tk=tc, name=tag + "_da")
    _grad_into(gbuf, "ffn_w_down", (DEPTH, F, D), i, 0, 0, a, df, tk=tc, tn=D, tm=ts, name=tag + "_dwdown")
    dpg, dpu, dwg8, dwu8, dbg8, dbu8 = _ffn_mid_bwd(hpre, da, p["conv_w"], p["conv_b"], ts=_rt(S, 256), tc=tc,
                                                    name=tag + "_midb")
    g["conv_w"] = jnp.concatenate([dwg8.sum(axis=1), dwu8.sum(axis=1)], axis=1)
    g["conv_b"] = jnp.concatenate([dbg8.sum(axis=0), dbu8.sum(axis=0)], axis=0)
    dx = _mm_tb([(dpg, 0), (dpu, F)], p["w_up"], out_dtype=F32, tm=ts, tk=D, name=tag + "_dx")
    _grad_into(gbuf, "ffn_w_up", (DEPTH, D, 2 * F), i, 0, 0, x, dpg, tk=D, tn=tc, tm=ts, name=tag + "_dwup_g")
    _grad_into(gbuf, "ffn_w_up", (DEPTH, D, 2 * F), i, 0, F, x, dpu, tk=D, tn=tc, tm=ts, name=tag + "_dwup_u")
    return dx, g


HEAD_SLOT = LANES
MLA_SCALE = (QK_NOPE + QK_ROPE) ** -0.5
NEG_BIG = -1e30
ATTN_BLOCK = 1024


def _rms_fwd(x, g, *, c0, ts, name):
    S = x.shape[0]
    w = g.shape[0]

    def body(x_ref, g_ref, o_ref):
        xv = x_ref[...]
        rstd = lax.rsqrt(jnp.mean(xv * xv, axis=-1, keepdims=True) + RMS_EPS)
        o_ref[...] = (xv * rstd * g_ref[...]).astype(BF16)

    return pl.pallas_call(
        body, name=name, grid=(S // ts,),
        in_specs=[pl.BlockSpec((ts, w), lambda i: (i, c0 // w)), pl.BlockSpec((1, w), lambda i: (0, 0))],
        out_specs=pl.BlockSpec((ts, w), lambda i: (i, 0)),
        out_shape=jax.ShapeDtypeStruct((S, w), BF16), compiler_params=_cp(("parallel",)),
    )(x, g.reshape(1, w))


def _rms_bwd(dy, x, g, *, c0, ts, name):
    S = x.shape[0]
    w = g.shape[0]

    def body(dy_ref, x_ref, g_ref, dx_ref, dg_ref):
        @pl.when(pl.program_id(0) == 0)
        def _():
            dg_ref[...] = jnp.zeros_like(dg_ref)
        xv = x_ref[...]
        dyv = dy_ref[...]
        rstd = lax.rsqrt(jnp.mean(xv * xv, axis=-1, keepdims=True) + RMS_EPS)
        dyg = dyv * g_ref[...]
        m = jnp.mean(dyg * xv, axis=-1, keepdims=True)
        dx_ref[...] = (rstd * (dyg - xv * (rstd * rstd) * m)).astype(BF16)
        dg_ref[...] += _fold8(dyv * xv * rstd)

    return pl.pallas_call(
        body, name=name, grid=(S // ts,),
        in_specs=[pl.BlockSpec((ts, w), lambda i: (i, 0)), pl.BlockSpec((ts, w), lambda i: (i, c0 // w)),
                  pl.BlockSpec((1, w), lambda i: (0, 0))],
        out_specs=[pl.BlockSpec((ts, w), lambda i: (i, 0)), pl.BlockSpec((SUBLANES, w), lambda i: (0, 0))],
        out_shape=[jax.ShapeDtypeStruct((S, w), BF16), jax.ShapeDtypeStruct((SUBLANES, w), F32)],
        compiler_params=_cp(("arbitrary",)),
    )(dy, x, g.reshape(1, w))


def _mla_tables(pos):
    S = pos.shape[0]
    half = QK_ROPE // 2
    inv_freq = ROPE_BASE ** (-jnp.arange(0, QK_ROPE, 2, dtype=F32) / QK_ROPE)
    ang = pos.astype(F32)[:, None] * inv_freq
    cos, sin = jnp.cos(ang), jnp.sin(ang)
    z = lambda n: jnp.zeros((S, n), F32)
    pad = HEAD_SLOT - QK_NOPE - QK_ROPE
    c = jnp.concatenate([jnp.ones((S, QK_NOPE), F32), cos, cos, z(pad)], axis=1)
    s1 = jnp.concatenate([z(QK_NOPE), -sin, z(half), z(pad)], axis=1)
    s2 = jnp.concatenate([z(QK_NOPE), z(half), sin, z(pad)], axis=1)
    return c, s1, s2


def _mla_prep_fwd(qraw, kvraw, proj, tabs, *, kpe_c0, ts, name):
    S = qraw.shape[0]
    H = MLA_HEADS
    half = QK_ROPE // 2

    def body(q_ref, kn_ref, kpe_ref, c_ref, s1_ref, s2_ref, qo_ref, ko_ref):
        c, s1, s2 = c_ref[...], s1_ref[...], s2_ref[...]

        def rope(v):
            return v * c + pltpu.roll(v, HEAD_SLOT - half, 1) * s1 + pltpu.roll(v, half, 1) * s2

        qo_ref[...] = (rope(q_ref[...]) * MLA_SCALE).astype(BF16)
        ko_ref[...] = (kn_ref[...] + rope(kpe_ref[...])).astype(BF16)

    slot = pl.BlockSpec((ts, HEAD_SLOT), lambda i, h: (i, h))
    tab = pl.BlockSpec((ts, HEAD_SLOT), lambda i, h: (i, 0))
    return pl.pallas_call(
        body, name=name, grid=(S // ts, H),
        in_specs=[slot, slot, pl.BlockSpec((ts, HEAD_SLOT), lambda i, h: (i, kpe_c0 // HEAD_SLOT)), tab, tab, tab],
        out_specs=[slot, slot],
        out_shape=[jax.ShapeDtypeStruct((S, H * HEAD_SLOT), BF16)] * 2,
        compiler_params=_cp(("parallel", "parallel")),
    )(qraw, kvraw, proj, *tabs)


def _mla_prep_bwd(dq, dk, dv, tabs, *, ts, name):
    S = dk.shape[0]
    H = MLA_HEADS
    half = QK_ROPE // 2
    kw = H * HEAD_SLOT
    vw = H * V_HEAD

    def body(dq_ref, dk_ref, dv_ref, c_ref, s1_ref, s2_ref, dqr_ref, dkv_ref, dkpe_ref):
        c, s1, s2 = c_ref[...], s1_ref[...], s2_ref[...]

        def rope_t(g):
            return g * c + pltpu.roll(g * s1, half, 1) + pltpu.roll(g * s2, HEAD_SLOT - half, 1)

        gsum = jnp.zeros((ts, HEAD_SLOT), F32)
        for h in range(H):
            sl = slice(h * HEAD_SLOT, (h + 1) * HEAD_SLOT)
            hs = slice((h % 2) * HEAD_SLOT, (h % 2 + 1) * HEAD_SLOT)
            dqr_ref[:, sl] = (rope_t(dq_ref[h // 2, :, hs]) * MLA_SCALE).astype(BF16)
            dkh = dk_ref[:, sl]
            dkv_ref[:, sl] = dkh.astype(BF16)
            gsum = gsum + dkh
        dkv_ref[:, kw:] = dv_ref[...].astype(BF16)
        lane = lax.broadcasted_iota(jnp.int32, (ts, HEAD_SLOT), 1)
        pe = jnp.logical_and(lane >= QK_NOPE, lane < QK_NOPE + QK_ROPE)
        dkpe_ref[...] = rope_t(jnp.where(pe, gsum, 0.0)).astype(BF16)

    tab = pl.BlockSpec((ts, HEAD_SLOT), lambda i: (i, 0))
    return pl.pallas_call(
        body, name=name, grid=(S // ts,),
        in_specs=[pl.BlockSpec((H // 2, ts, 2 * HEAD_SLOT), lambda i: (0, i, 0)), pl.BlockSpec((ts, kw), lambda i: (i, 0)),
                  pl.BlockSpec((ts, vw), lambda i: (i, 0)), tab, tab, tab],
        out_specs=[pl.BlockSpec((ts, kw), lambda i: (i, 0)), pl.BlockSpec((ts, kw + vw), lambda i: (i, 0)), tab],
        out_shape=[jax.ShapeDtypeStruct((S, kw), BF16), jax.ShapeDtypeStruct((S, kw + vw), BF16),
                   jax.ShapeDtypeStruct((S, HEAD_SLOT), BF16)],
        compiler_params=_cp(("parallel",)),
    )(dq, dk, dv, *tabs)


def _attn_pairs(nb, kv_outer):
    if kv_outer:
        pr = [(i, j) for j in range(nb) for i in range(j, nb)]
    else:
        pr = [(i, j) for i in range(nb) for j in range(i + 1)]
    return (jnp.asarray(np.array([p[0] for p in pr], np.int32)), jnp.asarray(np.array([p[1] for p in pr], np.int32)))


def _attn_scores(q_ref, k_ref, hh, diag, T):
    sl = slice(hh * HEAD_SLOT, (hh + 1) * HEAD_SLOT)
    s = _dot_tb(q_ref[:, sl], k_ref[:, sl])
    if not diag:
        return s
    row = lax.broadcasted_iota(jnp.int32, (T, T), 0) // CHUNK
    col = lax.broadcasted_iota(jnp.int32, (T, T), 1) // CHUNK
    return jnp.where(col <= row, s, NEG_BIG)


def _on_block_kind(i, j, step):
    @pl.when(i == j)
    def _():
        step(True)

    @pl.when(i != j)
    def _():
        step(False)


def _attn_fwd(q, k, kvraw, *, T, name):
    S = q.shape[0]
    NP = MLA_HEADS // 2
    nb = S // T
    ii, jj = _attn_pairs(nb, kv_outer=False)
    v_c0 = MLA_HEADS * HEAD_SLOT // LANES

    def body(ii_ref, jj_ref, q_ref, k_ref, v_ref, o_ref, lse_ref, m_sc, l_sc, acc_sc):
        t = pl.program_id(1)
        i, j = ii_ref[t], jj_ref[t]

        @pl.when(j == 0)
        def _():
            m_sc[...] = jnp.full_like(m_sc, NEG_BIG)
            l_sc[...] = jnp.zeros_like(l_sc)
            acc_sc[...] = jnp.zeros_like(acc_sc)

        lo = lax.broadcasted_iota(jnp.int32, (T, LANES), 1) < V_HEAD

        def step(diag):
            v = v_ref[...].astype(BF16)
            vh = (jnp.where(lo, v, jnp.zeros_like(v)), jnp.where(lo, jnp.zeros_like(v), v))
            alphas, pv = [], None
            for hh in range(2):
                s = _attn_scores(q_ref, k_ref, hh, diag, T)
                m_prev = m_sc[hh]
                m_new = jnp.maximum(m_prev, jnp.max(s, axis=1, keepdims=True))
                p = jnp.exp(s - jnp.tile(m_new, (1, T // LANES)))
                alpha = jnp.exp(m_prev - m_new)
                l_sc[hh] = alpha * l_sc[hh] + jnp.sum(p, axis=1, keepdims=True)
                m_sc[hh] = m_new
                alphas.append(alpha)
                t_pv = _dot(p.astype(BF16), vh[hh])
                pv = t_pv if pv is None else pv + t_pv
            acc_sc[...] = acc_sc[...] * jnp.where(lo, alphas[0], alphas[1]) + pv

        _on_block_kind(i, j, step)

        @pl.when(j == i)
        def _():
            l0, l1 = l_sc[0], l_sc[1]
            o_ref[...] = acc_sc[...] * jnp.where(lo, 1.0 / l0, 1.0 / l1)
            lse_ref[...] = jnp.where(lo, m_sc[0] + jnp.log(l0), m_sc[1] + jnp.log(l1))

    grid_spec = pltpu.PrefetchScalarGridSpec(
        num_scalar_prefetch=2, grid=(NP, int(ii.shape[0])),
        in_specs=[pl.BlockSpec((T, 2 * HEAD_SLOT), lambda p, t, ii, jj: (ii[t], p)),
                  pl.BlockSpec((T, 2 * HEAD_SLOT), lambda p, t, ii, jj: (jj[t], p)),
                  pl.BlockSpec((T, LANES), lambda p, t, ii, jj: (jj[t], v_c0 + p))],
        out_specs=[pl.BlockSpec((T, LANES), lambda p, t, ii, jj: (ii[t], p)),
                   pl.BlockSpec((None, T, LANES), lambda p, t, ii, jj: (p, ii[t], 0))],
        scratch_shapes=[pltpu.VMEM((2, T, LANES), F32), pltpu.VMEM((2, T, LANES), F32), pltpu.VMEM((T, LANES), F32)])
    return pl.pallas_call(
        body, name=name, grid_spec=grid_spec,
        out_shape=[jax.ShapeDtypeStruct((S, MLA_HEADS * V_HEAD), F32), jax.ShapeDtypeStruct((NP, S, LANES), F32)],
        compiler_params=_cp(("parallel", "arbitrary")),
    )(ii, jj, q, k, kvraw)


def _attn_bwd_common(q_ref, k_ref, v, do, o, lse, hh, diag, T, lo):
    sel = lo if hh == 0 else jnp.logical_not(lo)
    s = _attn_scores(q_ref, k_ref, hh, diag, T)
    p = jnp.exp(s - lse[:, hh * V_HEAD:hh * V_HEAD + 1])
    do_h = jnp.where(sel, do, 0.0)
    dsum = jnp.sum(do_h * o, axis=1, keepdims=True)
    do_hb = do_h.astype(BF16)
    dp = _dot_tb(do_hb, v)
    return p, p * (dp - dsum), do_hb


def _attn_bwd(q, k, kvraw, do, o, lse, *, T, name):
    S = q.shape[0]
    NP = MLA_HEADS // 2
    nb = S // T
    ii, jj = _attn_pairs(nb, kv_outer=True)
    n_steps = int(ii.shape[0])
    v_c0 = MLA_HEADS * HEAD_SLOT // LANES

    def body(ii_ref, jj_ref, q_ref, k_ref, v_ref, do_ref, o_ref, lse_ref, dq_hbm, dk_ref, dv_ref,
             dq_sc, dk_sc, dv_sc, sem):
        pair = pl.program_id(0)
        t = pl.program_id(1)
        i, j = ii_ref[t], jj_ref[t]

        @pl.when(t == 0)
        def _():
            dq_sc[...] = jnp.zeros_like(dq_sc)

        @pl.when(i == j)
        def _():
            dk_sc[...] = jnp.zeros_like(dk_sc)
            dv_sc[...] = jnp.zeros_like(dv_sc)

        lo = lax.broadcasted_iota(jnp.int32, (T, LANES), 1) < V_HEAD
        rows = pl.ds(pl.multiple_of(i * T, T), T)

        def step(diag):
            v = v_ref[...].astype(BF16)
            do, o_v, lse_v = do_ref[...], o_ref[...], lse_ref[...]
            for hh in range(2):
                sl = slice(hh * HEAD_SLOT, (hh + 1) * HEAD_SLOT)
                p, ds, do_hb = _attn_bwd_common(q_ref, k_ref, v, do, o_v, lse_v, hh, diag, T, lo)
                dsb = ds.astype(BF16)
                dv_sc[...] += _dot_ta(p.astype(BF16), do_hb)
                dk_sc[:, sl] += _dot_ta(dsb, q_ref[:, sl])
                dq_sc[rows, sl] += _dot(dsb, k_ref[:, sl])

        _on_block_kind(i, j, step)

        @pl.when(i == nb - 1)
        def _():
            dk_ref[...] = dk_sc[...]
            dv_ref[...] = dv_sc[...]

        @pl.when(t == n_steps - 1)
        def _():
            cp = pltpu.make_async_copy(dq_sc, dq_hbm.at[pair], sem)
            cp.start()
            cp.wait()

    qi = lambda p, t, ii, jj: (ii[t], p)
    kj = lambda p, t, ii, jj: (jj[t], p)
    grid_spec = pltpu.PrefetchScalarGridSpec(
        num_scalar_prefetch=2, grid=(NP, n_steps),
        in_specs=[pl.BlockSpec((T, 2 * HEAD_SLOT), qi), pl.BlockSpec((T, 2 * HEAD_SLOT), kj),
                  pl.BlockSpec((T, LANES), lambda p, t, ii, jj: (jj[t], v_c0 + p)),
                  pl.BlockSpec((T, LANES), qi), pl.BlockSpec((T, LANES), qi),
                  pl.BlockSpec((None, T, LANES), lambda p, t, ii, jj: (p, ii[t], 0))],
        out_specs=[pl.BlockSpec(memory_space=pl.ANY), pl.BlockSpec((T, 2 * HEAD_SLOT), kj),
                   pl.BlockSpec((T, LANES), kj)],
        scratch_shapes=[pltpu.VMEM((S, 2 * HEAD_SLOT), F32), pltpu.VMEM((T, 2 * HEAD_SLOT), F32),
                        pltpu.VMEM((T, LANES), F32), pltpu.SemaphoreType.DMA])
    return pl.pallas_call(
        body, name=name, grid_spec=grid_spec,
        out_shape=[jax.ShapeDtypeStruct((NP, S, 2 * HEAD_SLOT), F32),
                   jax.ShapeDtypeStruct((S, MLA_HEADS * HEAD_SLOT), F32),
                   jax.ShapeDtypeStruct((S, MLA_HEADS * V_HEAD), F32)],
        compiler_params=_cp(("arbitrary", "arbitrary")),
    )(ii, jj, q, k, kvraw, do, o, lse)


def _mla_permute_weights(w_in, w_uq, w_ukv):
    D = w_in.shape[0]
    H = MLA_HEADS
    qk = QK_NOPE + QK_ROPE
    lat = Q_LORA + KV_LORA
    kpe = jnp.zeros((D, HEAD_SLOT), w_in.dtype).at[:, QK_NOPE:qk].set(w_in[:, lat:])
    w_in_p = jnp.concatenate([w_in[:, :lat], kpe], axis=1)
    w_uq_p = jnp.pad(w_uq.reshape(Q_LORA, H, qk), ((0, 0), (0, 0), (0, HEAD_SLOT - qk))).reshape(Q_LORA, H * HEAD_SLOT)
    kv = w_ukv.reshape(KV_LORA, H, QK_NOPE + V_HEAD)
    wk = jnp.pad(kv[:, :, :QK_NOPE], ((0, 0), (0, 0), (0, HEAD_SLOT - QK_NOPE))).reshape(KV_LORA, H * HEAD_SLOT)
    wv = kv[:, :, QK_NOPE:].reshape(KV_LORA, H * V_HEAD)
    return w_in_p, w_uq_p, jnp.concatenate([wk, wv], axis=1)


def _mla_unpermute_grads(g_in_p, g_uq_p, g_ukv_p):
    H = MLA_HEADS
    qk = QK_NOPE + QK_ROPE
    lat = Q_LORA + KV_LORA
    g_in = jnp.concatenate([g_in_p[:, :lat], g_in_p[:, lat + QK_NOPE:lat + qk]], axis=1)
    g_uq = g_uq_p.reshape(Q_LORA, H, HEAD_SLOT)[:, :, :qk].reshape(Q_LORA, H * qk)
    gk = g_ukv_p[:, :H * HEAD_SLOT].reshape(KV_LORA, H, HEAD_SLOT)[:, :, :QK_NOPE]
    gv = g_ukv_p[:, H * HEAD_SLOT:].reshape(KV_LORA, H, V_HEAD)
    g_ukv = jnp.concatenate([gk, gv], axis=2).reshape(KV_LORA, H * (QK_NOPE + V_HEAD))
    return g_in, g_uq, g_ukv


def _mla_mixer_fwd(x, pos, p, tag):
    S, D = x.shape
    ts = _rt(S, 512)
    T = _rt(S, ATTN_BLOCK)
    lat = Q_LORA + KV_LORA
    tabs = _mla_tables(pos)
    proj = _mm(x, p["w_in_p"], out_dtype=F32, tm=ts, tn=p["w_in_p"].shape[1], name=tag + "_proj")
    qn = _rms_fwd(proj, p["q_norm"], c0=0, ts=ts, name=tag + "_qn")
    kvn = _rms_fwd(proj, p["kv_norm"], c0=Q_LORA, ts=ts, name=tag + "_kvn")
    qraw = _mm(qn, p["w_uq_p"], out_dtype=F32, tm=ts, tn=1024, name=tag + "_uq")
    kvraw = _mm(kvn, p["w_ukv_p"], out_dtype=F32, tm=ts, tn=1024, name=tag + "_ukv")
    q, k = _mla_prep_fwd(qraw, kvraw, proj, tabs, kpe_c0=lat, ts=ts, name=tag + "_prep")
    o, lse = _attn_fwd(q, k, kvraw, T=T, name=tag + "_attn")
    mix = _mm(o, p["w_out"], out_dtype=F32, tm=ts, tn=D, name=tag + "_out")
    return mix, (proj, qn, kvn, kvraw, q, k, o, lse, tabs)


def _mla_mixer_bwd(dmix, x, p, saved, tag, gbuf, j):
    proj, qn, kvn, kvraw, q, k, o, lse, tabs = saved
    S, D = x.shape
    ts = _rt(S, 512)
    T = _rt(S, ATTN_BLOCK)
    lat = Q_LORA + KV_LORA
    g = {}
    do = _mm_tb([(dmix, 0)], p["w_out"], out_dtype=F32, tm=ts, tk=p["w_out"].shape[0], name=tag + "_do")
    g["w_out"] = _mm_ta(o, dmix, tk=p["w_out"].shape[0], tn=D, tm=ts, name=tag + "_dwout")
    dq, dk, dv = _attn_bwd(q, k, kvraw, do, o, lse, T=T, name=tag + "_attn_bwd")
    dqraw, dkvraw, dkpe = _mla_prep_bwd(dq, dk, dv, tabs, ts=_rt(S, 256), name=tag + "_prepb")
    dqn = _mm_tb([(dqraw, 0)], p["w_uq_p"], out_dtype=F32, tm=ts, tk=Q_LORA, name=tag + "_dqn")
    g_uq_p = _mm_ta(qn, dqraw, tk=Q_LORA, tn=1024, tm=ts, name=tag + "_dwuq")
    dkvn = _mm_tb([(dkvraw, 0)], p["w_ukv_p"], out_dtype=F32, tm=ts, tk=KV_LORA, name=tag + "_dkvn")
    g_ukv_p = _mm_ta(kvn, dkvraw, tk=KV_LORA, tn=1024, tm=ts, name=tag + "_dwukv")
    dcq, dqg8 = _rms_bwd(dqn, proj, p["q_norm"], c0=0, ts=ts, name=tag + "_qnb")
    dckv, dkvg8 = _rms_bwd(dkvn, proj, p["kv_norm"], c0=Q_LORA, ts=ts, name=tag + "_kvnb")
    g["q_norm"] = dqg8.sum(axis=0)
    g["kv_norm"] = dkvg8.sum(axis=0)
    dx = _mm_tb([(dcq, 0), (dckv, Q_LORA), (dkpe, lat)], p["w_in_p"], out_dtype=F32, tm=ts, tk=D, name=tag + "_dx")
    g_in_p = jnp.concatenate(
        [_mm_ta(x, dcq, tk=D, tn=Q_LORA, tm=ts, name=tag + "_dwin_q"),
         _mm_ta(x, dckv, tk=D, tn=KV_LORA, tm=ts, name=tag + "_dwin_kv"),
         _mm_ta(x, dkpe, tk=D, tn=HEAD_SLOT, tm=ts, name=tag + "_dwin_pe")], axis=1)
    g["w_in"], g["w_uq"], g["w_ukv"] = _mla_unpermute_grads(g_in_p, g_uq_p, g_ukv_p)
    return dx, g


RET_QK = 256
RET_V = 512


def _ret_tables(pos, T):
    half = RET_QK // 2
    inv_freq = ROPE_BASE ** (-jnp.arange(0, RET_QK, 2, dtype=F32) / RET_QK)
    ang = pos.astype(F32)[:, None] * inv_freq
    lg = jnp.log1p(-jnp.exp2(-5.0 - jnp.arange(RET_HEADS, dtype=F32)))
    idx = jnp.arange(T, dtype=F32)
    ch = jnp.arange(T) // CHUNK
    dm = jnp.where(ch[None, :] <= ch[:, None], jnp.exp(lg[:, None, None] * jnp.abs(idx[:, None] - idx[None, :])), 0.0)
    xi = jnp.broadcast_to(jnp.exp(lg[:, None] * (idx + 1.0))[:, :, None], (RET_HEADS, T, RET_QK))
    zeta = jnp.broadcast_to(jnp.exp(lg[:, None] * (T - 1.0 - idx))[:, :, None], (RET_HEADS, T, RET_QK))
    g_t = jnp.broadcast_to(jnp.exp(lg * T)[:, None, None], (RET_HEADS, 1, RET_V))
    assert half == LANES
    return jnp.cos(ang), jnp.sin(ang), dm.astype(F32), xi.astype(F32), zeta.astype(F32), g_t.astype(F32)


def _rope_half(x, c, s):
    x1, x2 = x[:, :LANES], x[:, LANES:]
    return jnp.concatenate([x1 * c - x2 * s, x1 * s + x2 * c], axis=1)


def _rope_half_t(g, c, s):
    g1, g2 = g[:, :LANES], g[:, LANES:]
    return jnp.concatenate([g1 * c + g2 * s, g2 * c - g1 * s], axis=1)


def _ret_qkv(q_ref, k_ref, v_ref, c_ref, s_ref):
    c, s = c_ref[...], s_ref[...]
    q = _rope_half(q_ref[...], c, s)
    k = _rope_half(k_ref[...], c, s) * (RET_QK ** -0.5)
    return q, k, v_ref[...].astype(BF16)


def _ret_in_specs(T, H, rev_nb=None):
    rb = (lambda n: n) if rev_nb is None else (lambda n: rev_nb - 1 - n)
    nq = H * RET_QK // RET_QK
    nv = 2 * H * RET_QK // RET_V
    return dict(
        q=pl.BlockSpec((T, RET_QK), lambda h, n: (rb(n), h)),
        k=pl.BlockSpec((T, RET_QK), lambda h, n: (rb(n), nq + h)),
        v=pl.BlockSpec((T, RET_V), lambda h, n: (rb(n), nv + h)),
        g=pl.BlockSpec((T, RET_V), lambda h, n: (rb(n), nv + H + h)),
        yv=pl.BlockSpec((T, RET_V), lambda h, n: (rb(n), h)),
        cs=pl.BlockSpec((T, LANES), lambda h, n: (rb(n), 0)),
        dm=pl.BlockSpec((None, T, T), lambda h, n: (h, 0, 0)),
        xz=pl.BlockSpec((None, T, RET_QK), lambda h, n: (h, 0, 0)),
        gt=pl.BlockSpec((None, 1, RET_V), lambda h, n: (h, 0, 0)),
        gn=pl.BlockSpec((1, RET_V), lambda h, n: (0, h)),
        st=pl.BlockSpec((None, None, RET_QK, RET_V), lambda h, n: (h, rb(n), 0, 0)),
    )


def _ret_fwd(proj, gn_g, tabs, *, T, name):
    S = proj.shape[0]
    H = RET_HEADS
    nb = S // T
    cos, sin, dm, xi, zeta, g_t = tabs
    sp = _ret_in_specs(T, H)

    def body(q_ref, k_ref, v_ref, g_ref, gn_ref, c_ref, s_ref, dm_ref, xi_ref, zeta_ref, gt_ref,
             o_ref, y_ref, st_ref, st):
        @pl.when(pl.program_id(1) == 0)
        def _():
            st[...] = jnp.zeros_like(st)

        q, k, vb = _ret_qkv(q_ref, k_ref, v_ref, c_ref, s_ref)
        qb, kb = q.astype(BF16), k.astype(BF16)
        s0 = st[...]
        s0b = s0.astype(BF16)
        st_ref[...] = s0b
        a = _dot_tb(qb, kb) * dm_ref[...]
        y = _dot(a.astype(BF16), vb) + _dot((q * xi_ref[...]).astype(BF16), s0b)
        st[...] = s0 * gt_ref[...] + _dot_ta((k * zeta_ref[...]).astype(BF16), vb)
        y_ref[...] = y
        mu = jnp.mean(y, axis=-1, keepdims=True)
        yc = y - mu
        var = jnp.mean(yc * yc, axis=-1, keepdims=True)
        gv = g_ref[...]
        o_ref[...] = (gv * _sigmoid(gv) * (yc * lax.rsqrt(var + LN_EPS) * gn_ref[...])).astype(BF16)

    return pl.pallas_call(
        body, name=name, grid=(H, nb),
        in_specs=[sp["q"], sp["k"], sp["v"], sp["g"], sp["gn"], sp["cs"], sp["cs"], sp["dm"], sp["xz"], sp["xz"], sp["gt"]],
        out_specs=[sp["yv"], sp["yv"], sp["st"]],
        out_shape=[jax.ShapeDtypeStruct((S, H * RET_V), BF16), jax.ShapeDtypeStruct((S, H * RET_V), F32),
                   jax.ShapeDtypeStruct((H, nb, RET_QK, RET_V), BF16)],
        scratch_shapes=[pltpu.VMEM((RET_QK, RET_V), F32)],
        compiler_params=_cp(("parallel", "arbitrary")),
    )(proj, proj, proj, proj, gn_g.reshape(1, H * RET_V), cos, sin, dm, xi, zeta, g_t)


def _ret_gn_bwd(dout, proj, y, gn_g, *, ts, name):
    S = proj.shape[0]
    H = RET_HEADS
    goff = 2 * H * RET_QK // RET_V + H

    def body(do_ref, g_ref, y_ref, gn_ref, dy_ref, dg_ref, dgn_ref):
        @pl.when(pl.program_id(1) == 0)
        def _():
            dgn_ref[...] = jnp.zeros_like(dgn_ref)
        y_v = y_ref[...]
        mu = jnp.mean(y_v, axis=-1, keepdims=True)
        yc = y_v - mu
        var = jnp.mean(yc * yc, axis=-1, keepdims=True)
        rstd = lax.rsqrt(var + LN_EPS)
        yhat = yc * rstd
        gv = g_ref[...]
        sg = _sigmoid(gv)
        dout = do_ref[...]
        gn = gn_ref[...]
        dg_ref[...] = (dout * (yhat * gn) * (sg * (1.0 + gv * (1.0 - sg)))).astype(BF16)
        dyn = dout * (gv * sg)
        dgn_ref[...] += _fold8(dyn * yhat)
        dyh = dyn * gn
        m1 = jnp.mean(dyh, axis=-1, keepdims=True)
        m2 = jnp.mean(dyh * yhat, axis=-1, keepdims=True)
        dy_ref[...] = (rstd * (dyh - m1 - yhat * m2)).astype(BF16)

    blk = pl.BlockSpec((ts, RET_V), lambda h, i: (i, h))
    return pl.pallas_call(
        body, name=name, grid=(H, S // ts),
        in_specs=[blk, pl.BlockSpec((ts, RET_V), lambda h, i: (i, goff + h)), blk,
                  pl.BlockSpec((1, RET_V), lambda h, i: (0, h))],
        out_specs=[blk, blk, pl.BlockSpec((SUBLANES, RET_V), lambda h, i: (0, h))],
        out_shape=[jax.ShapeDtypeStruct((S, H * RET_V), BF16), jax.ShapeDtypeStruct((S, H * RET_V), BF16),
                   jax.ShapeDtypeStruct((SUBLANES, H * RET_V), F32)],
        compiler_params=_cp(("parallel", "arbitrary")),
    )(dout, proj, y, gn_g.reshape(1, H * RET_V))


def _ret_bwd(proj, dy, states, tabs, *, T, name):
    S = proj.shape[0]
    H = RET_HEADS
    nb = S // T
    cos, sin, dm, xi, zeta, g_t = tabs
    sp = _ret_in_specs(T, H, rev_nb=nb)

    def body(q_ref, k_ref, v_ref, dy_ref, st_ref, c_ref, s_ref, dm_ref, xi_ref, zeta_ref, gt_ref,
             dq_ref, dk_ref, dv_ref, ds):
        @pl.when(pl.program_id(1) == 0)
        def _():
            ds[...] = jnp.zeros_like(ds)

        q, k, vb = _ret_qkv(q_ref, k_ref, v_ref, c_ref, s_ref)
        qb, kb = q.astype(BF16), k.astype(BF16)
        dyb = dy_ref[...]
        s0b = st_ref[...]
        dmv, xiv, zv = dm_ref[...], xi_ref[...], zeta_ref[...]
        ds_v = ds[...]
        dsb = ds_v.astype(BF16)
        gm = (_dot_tb(dyb, vb) * dmv).astype(BF16)
        ab = (_dot_tb(qb, kb) * dmv).astype(BF16)
        kz = (k * zv).astype(BF16)
        qx = (q * xiv).astype(BF16)
        dq = _dot(gm, kb) + xiv * _dot_tb(dyb, s0b)
        dk = _dot_ta(gm, qb) + zv * _dot_tb(vb, dsb)
        dv_ref[...] = (_dot_ta(ab, dyb) + _dot(kz, dsb)).astype(BF16)
        ds[...] = ds_v * gt_ref[...] + _dot_ta(qx, dyb)
        c, s = c_ref[...], s_ref[...]
        dq_ref[...] = _rope_half_t(dq, c, s).astype(BF16)
        dk_ref[...] = _rope_half_t(dk * (RET_QK ** -0.5), c, s).astype(BF16)

    qblk = pl.BlockSpec((T, RET_QK), lambda h, n: (nb - 1 - n, h))
    return pl.pallas_call(
        body, name=name, grid=(H, nb),
        in_specs=[sp["q"], sp["k"], sp["v"], sp["yv"], sp["st"], sp["cs"], sp["cs"], sp["dm"], sp["xz"], sp["xz"], sp["gt"]],
        out_specs=[qblk, qblk, sp["yv"]],
        out_shape=[jax.ShapeDtypeStruct((S, H * RET_QK), BF16), jax.ShapeDtypeStruct((S, H * RET_QK), BF16),
                   jax.ShapeDtypeStruct((S, H * RET_V), BF16)],
        scratch_shapes=[pltpu.VMEM((RET_QK, RET_V), F32)],
        compiler_params=_cp(("parallel", "arbitrary")),
    )(proj, proj, proj, dy, states, cos, sin, dm, xi, zeta, g_t)


def _ret_mixer_fwd(x, pos, p, tag):
    S, D = x.shape
    ts = _rt(S, 512)
    T = _rt(S, 256)
    tabs = _ret_tables(pos, T)
    proj = _mm(x, p["w_in"], out_dtype=F32, tm=ts, tn=1024, name=tag + "_proj")
    gated, y, states = _ret_fwd(proj, p["gn_g"], tabs, T=T, name=tag + "_ret")
    mix = _mm(gated, p["w_out"], out_dtype=F32, tm=ts, tn=D, name=tag + "_out")
    return mix, (proj, gated, y, states, tabs)


def _ret_mixer_bwd(dmix, x, p, saved, tag, gbuf, j):
    proj, gated, y, states, tabs = saved
    S, D = x.shape
    ts = _rt(S, 512)
    T = _rt(S, 256)
    H = RET_HEADS
    hq, hv = H * RET_QK, H * RET_V
    g = {}
    dout = _mm_tb([(dmix, 0)], p["w_out"], out_dtype=F32, tm=ts, tk=1024, name=tag + "_dgated")
    n_ret = DEPTH // 3
    _grad_into(gbuf, "ret_w_out", (n_ret, hv, D), j, 0, 0, gated, dmix, tk=1024, tn=D, tm=ts, name=tag + "_dwout")
    dy, dgate, dgn8 = _ret_gn_bwd(dout, proj, y, p["gn_g"], ts=_rt(S, 256), name=tag + "_gnb")
    g["gn_g"] = dgn8.sum(axis=0)
    dq, dk, dv = _ret_bwd(proj, dy, states, tabs, T=T, name=tag + "_retb")
    dx = _mm_tb([(dq, 0), (dk, hq), (dv, 2 * hq), (dgate, 2 * hq + hv)], p["w_in"], out_dtype=F32,
                tm=ts, tk=512, name=tag + "_dx")
    w_in_shape = (n_ret, D, 2 * hq + 2 * hv)
    for part, c0, nm in ((dq, 0, "q"), (dk, hq, "k"), (dv, 2 * hq, "v"), (dgate, 2 * hq + hv, "g")):
        _grad_into(gbuf, "ret_w_in", w_in_shape, j, 0, c0, x, part, tk=D, tn=1024, tm=ts, name=tag + "_dwin_" + nm)
    return dx, g


PACK_W = 1024
ANY = pl.BlockSpec(memory_space=pl.ANY)
MESH = pl.DeviceIdType.MESH


def _coords():
    return lax.axis_index("x"), lax.axis_index("y"), lax.axis_index("c")


def _chip_peers(x, y):
    return [(1 - x, y), (x, 1 - y), (1 - x, 1 - y)]


def _slot(ref, axis, s, n):
    if axis is None:
        return ref.at[s]
    size = n // N_CHIPS
    sl = pl.ds(pl.multiple_of(s * size, LANES if axis == 2 else 2 * SUBLANES), size)
    return ref.at[:, sl, :] if axis == 1 else ref.at[:, :, sl]


def _gather_chips(items, name):
    n = len(items)
    axes = [ax for _, ax in items]
    out_shapes = []
    for arr, ax in items:
        shp = (N_CHIPS,) + arr.shape if ax is None else tuple(d * (N_CHIPS if i == ax else 1) for i, d in enumerate(arr.shape))
        out_shapes.append(jax.ShapeDtypeStruct(shp, arr.dtype))

    def body(*refs):
        srcs, outs = refs[:n], refs[n:2 * n]
        send_sems, recv_sems, local_sems = refs[2 * n:]
        x, y, c = _coords()
        me = 2 * x + y
        dst = lambda t, s: _slot(outs[t], axes[t], s, out_shapes[t].shape[axes[t]] if axes[t] is not None else 0)
        local = [pltpu.make_async_copy(srcs[t], dst(t, me), local_sems.at[t]) for t in range(n)]
        for cp in local:
            cp.start()
        sends, recvs = [], []
        for k, (px, py) in enumerate(_chip_peers(x, y)):
            for t in range(n):
                sem = k * n + t
                sends.append(pltpu.make_async_remote_copy(
                    src_ref=srcs[t], dst_ref=dst(t, me), send_sem=send_sems.at[sem], recv_sem=recv_sems.at[sem],
                    device_id=(px, py, c), device_id_type=MESH))
                recvs.append(pltpu.make_async_remote_copy(
                    src_ref=srcs[t], dst_ref=dst(t, 2 * px + py), send_sem=send_sems.at[sem],
                    recv_sem=recv_sems.at[sem], device_id=(px, py, c), device_id_type=MESH))
        for cp in sends:
            cp.start()
        for cp in recvs:
            cp.wait_recv()
        for cp in sends:
            cp.wait_send()
        for cp in local:
            cp.wait()

    return pl.pallas_call(
        body, name=name, in_specs=[ANY] * n, out_specs=[ANY] * n, out_shape=out_shapes,
        scratch_shapes=[pltpu.SemaphoreType.DMA((3 * n,)), pltpu.SemaphoreType.DMA((3 * n,)),
                        pltpu.SemaphoreType.DMA((n,))],
    )(*[arr for arr, _ in items])


def _scatter_chips(items, name):
    n = len(items)
    axes = [ax for _, ax in items]
    out_shapes = []
    for arr, ax in items:
        part = arr.shape[1:] if ax is None else tuple(d // (N_CHIPS if i == ax else 1) for i, d in enumerate(arr.shape))
        out_shapes.append(jax.ShapeDtypeStruct((3,) + part, arr.dtype))

    def body(*refs):
        srcs, outs = refs[:n], refs[n:2 * n]
        send_sems, recv_sems = refs[2 * n:]
        x, y, c = _coords()
        copies = []
        for k, (px, py) in enumerate(_chip_peers(x, y)):
            for t in range(n):
                src = _slot(srcs[t], axes[t], 2 * px + py, srcs[t].shape[axes[t]] if axes[t] is not None else 0)
                copies.append(pltpu.make_async_remote_copy(
                    src_ref=src, dst_ref=outs[t].at[k], send_sem=send_sems.at[k * n + t],
                    recv_sem=recv_sems.at[k * n + t], device_id=(px, py, c), device_id_type=MESH))
        for cp in copies:
            cp.start()
        for cp in copies:
            cp.wait_recv()
        for cp in copies:
            cp.wait_send()

    return pl.pallas_call(
        body, name=name, in_specs=[ANY] * n, out_specs=[ANY] * n, out_shape=out_shapes,
        scratch_shapes=[pltpu.SemaphoreType.DMA((3 * n,)), pltpu.SemaphoreType.DMA((3 * n,))],
    )(*[arr for arr, _ in items])


def _swap_sibling(arrs, name):
    n = len(arrs)

    def body(*refs):
        srcs, outs = refs[:n], refs[n:2 * n]
        send_sems, recv_sems = refs[2 * n:]
        x, y, c = _coords()
        copies = [pltpu.make_async_remote_copy(src_ref=srcs[t], dst_ref=outs[t], send_sem=send_sems.at[t],
                                               recv_sem=recv_sems.at[t], device_id=(x, y, 1 - c), device_id_type=MESH)
                  for t in range(n)]
        for cp in copies:
            cp.start()
        for cp in copies:
            cp.wait_recv()
        for cp in copies:
            cp.wait_send()

    return pl.pallas_call(
        body, name=name, in_specs=[ANY] * n, out_specs=[ANY] * n,
        out_shape=[jax.ShapeDtypeStruct(a_.shape, a_.dtype) for a_ in arrs],
        scratch_shapes=[pltpu.SemaphoreType.DMA((n,)), pltpu.SemaphoreType.DMA((n,))],
    )(*arrs)


def _allreduce_small(v, name):
    R, Wd = v.shape

    def body(v_ref, o_ref, buf, send_sems, recv_sems):
        x, y, c = _coords()
        o_ref[...] = v_ref[...]
        for st, peer in enumerate([(x, y, 1 - c), (x, 1 - y, c), (1 - x, y, c)]):
            cp = pltpu.make_async_remote_copy(src_ref=o_ref, dst_ref=buf.at[st], send_sem=send_sems.at[st],
                                              recv_sem=recv_sems.at[st], device_id=peer, device_id_type=MESH)
            cp.start()
            cp.wait_recv()
            cp.wait_send()
            o_ref[...] = o_ref[...] + buf[st]

    vm = pl.BlockSpec(memory_space=pltpu.VMEM)
    return pl.pallas_call(
        body, name=name, in_specs=[vm], out_specs=vm,
        out_shape=jax.ShapeDtypeStruct((R, Wd), F32),
        scratch_shapes=[pltpu.VMEM((3, R, Wd), F32), pltpu.SemaphoreType.DMA((3,)), pltpu.SemaphoreType.DMA((3,))],
    )(v)


def _row_tile(rows):
    t = rows
    while t > 256:
        assert t % 2 == 0
        t //= 2
    assert t % SUBLANES == 0
    return t


def _sum_partials(g, recv, axis, *, name):
    _, L, R, C = recv.shape
    tr = _row_tile(R)
    me = (2 * lax.axis_index("x") + lax.axis_index("y")).astype(jnp.int32).reshape(1)

    def body(me_ref, g_ref, r_ref, o_ref):
        o_ref[...] = ((g_ref[...] + r_ref[0]) + r_ref[1]) + r_ref[2]

    if axis is None:
        g_spec = pl.BlockSpec((None, None, tr, C), lambda l, i, me: (me[0], l, i, 0))
    elif axis == 1:
        g_spec = pl.BlockSpec((None, tr, C), lambda l, i, me: (l, me[0] * (R // tr) + i, 0))
    else:
        g_spec = pl.BlockSpec((None, tr, C), lambda l, i, me: (l, i, me[0]))
    grid_spec = pltpu.PrefetchScalarGridSpec(
        num_scalar_prefetch=1, grid=(L, R // tr),
        in_specs=[g_spec, pl.BlockSpec((3, None, tr, C), lambda l, i, me: (0, l, i, 0))],
        out_specs=pl.BlockSpec((None, tr, C), lambda l, i, me: (l, i, 0)))
    return pl.pallas_call(
        body, name=name, grid_spec=grid_spec, out_shape=jax.ShapeDtypeStruct((L, R, C), F32),
        compiler_params=_cp(("parallel", "parallel")),
    )(me, g, recv)


def _adamw(w, m, v, ga, gb, *, name):
    L, R, C = w.shape
    tr = _row_tile(R)
    two = gb is not None
    c1 = 1.0 / (1.0 - ADAM_B1 ** ADAM_STEP)
    c2 = 1.0 / (1.0 - ADAM_B2 ** ADAM_STEP)

    def body(*refs):
        if two:
            w_ref, m_ref, v_ref, ga_ref, gb_ref, g_ref, d_ref, mo_ref, vo_ref = refs
            g = ga_ref[...] + gb_ref[...]
        else:
            w_ref, m_ref, v_ref, ga_ref, g_ref, d_ref, mo_ref, vo_ref = refs
            g = ga_ref[...]
        m2 = ADAM_B1 * m_ref[...] + (1.0 - ADAM_B1) * g
        v2 = ADAM_B2 * v_ref[...] + (1.0 - ADAM_B2) * (g * g)
        g_ref[...] = g
        mo_ref[...] = m2
        vo_ref[...] = v2
        d_ref[...] = -ADAM_LR * ((m2 * c1) / (jnp.sqrt(v2 * c2) + ADAM_EPS) + ADAM_WD * w_ref[...])

    blk = pl.BlockSpec((None, tr, C), lambda l, i: (l, i, 0))
    args = [w, m, v, ga] + ([gb] if two else [])
    return pl.pallas_call(
        body, name=name, grid=(L, R // tr), in_specs=[blk] * len(args), out_specs=[blk] * 4,
        out_shape=[jax.ShapeDtypeStruct((L, R, C), F32)] * 4, compiler_params=_cp(("parallel", "parallel")),
    )(*args)


SHARDED = [
    ("ffn_w_up", 2, True), ("ffn_conv_w", 2, False), ("ffn_w_down", 1, True),
    ("lru_w_in", 2, True), ("lru_conv_w", 2, False), ("lru_conv_b", 1, False),
    ("lru_w_a", 2, True), ("lru_b_a", 2, False), ("lru_w_x", 2, True), ("lru_b_x", 2, False),
    ("lru_lambda", 1, False), ("lru_w_out", 1, True),
    ("mla_w_in", 2, True), ("mla_w_uq", 2, True), ("mla_w_ukv", 2, True), ("mla_w_out", 1, True),
    ("ret_w_in", 2, True), ("ret_gn_g", 1, False), ("ret_w_out", 1, True),
]
BIG_AXIS = {"ffn_w_up": 2, "ffn_w_down": 1, "lru_w_in": 2, "lru_w_out": 1, "ret_w_in": 2, "ret_w_out": 1}
REPLICATED = ["ln1_g", "ln1_b", "ln2_g", "ln2_b", "ffn_conv_b", "mla_q_norm", "mla_kv_norm"]
WEIGHTS = ["ln1_g", "ln1_b", "ln2_g", "ln2_b", "ffn_w_up", "ffn_conv_w", "ffn_conv_b", "ffn_w_down", "lru_w_in",
           "lru_conv_w", "lru_conv_b", "lru_w_a", "lru_b_a", "lru_w_x", "lru_b_x", "lru_lambda", "lru_w_out",
           "mla_w_in", "mla_q_norm", "mla_kv_norm", "mla_w_uq", "mla_w_ukv", "mla_w_out", "ret_w_in", "ret_gn_g",
           "ret_w_out"]
PACK_ROWS = 512


def _pack(arrs, dtype, lead=(), rows=PACK_ROWS):
    nl = len(lead)
    flat = jnp.concatenate([a.astype(dtype).reshape(lead + (-1,)) for a in arrs], axis=nl)
    n = flat.shape[nl]
    quantum = rows * PACK_W
    total = -(-n // quantum) * quantum
    flat = jnp.pad(flat, [(0, 0)] * nl + [(0, total - n)])
    return flat.reshape(lead + (total // PACK_W, PACK_W))


def _unpack(buf, shapes, lead=()):
    nl = len(lead)
    flat = buf.reshape(lead + (-1,))
    out, off = [], 0
    for shp in shapes:
        n = int(np.prod(shp))
        out.append(lax.slice_in_dim(flat, off, off + n, axis=nl).reshape(lead + tuple(shp)))
        off += n
    return out


def _layer_params(full, rep, i):
    kind, j = i % 3, i // 3
    ffn = dict(w_up=full["ffn_w_up"][i], conv_w=full["ffn_conv_w"][i], conv_b=rep["ffn_conv_b"][i],
               w_down=full["ffn_w_down"][i])
    if kind == 0:
        mix = dict(w_in=full["lru_w_in"][j], conv_w=full["lru_conv_w"][j], conv_b=full["lru_conv_b"][j],
                   w_a=full["lru_w_a"][j], b_a=full["lru_b_a"][j], w_x=full["lru_w_x"][j], b_x=full["lru_b_x"][j],
                   lam=full["lru_lambda"][j], w_out=full["lru_w_out"][j])
    elif kind == 1:
        w_in_p, w_uq_p, w_ukv_p = _mla_permute_weights(full["mla_w_in"][j], full["mla_w_uq"][j], full["mla_w_ukv"][j])
        mix = dict(w_in_p=w_in_p, w_uq_p=w_uq_p, w_ukv_p=w_ukv_p, q_norm=rep["mla_q_norm"][j],
                   kv_norm=rep["mla_kv_norm"][j], w_out=full["mla_w_out"][j])
    else:
        mix = dict(w_in=full["ret_w_in"][j], gn_g=full["ret_gn_g"][j], w_out=full["ret_w_out"][j])
    return kind, mix, ffn


_MIX_FWD = {0: lambda x, pos, p, tag: _lru_mixer_fwd(x, p, tag), 1: _mla_mixer_fwd, 2: _ret_mixer_fwd}
_MIX_BWD = {0: _lru_mixer_bwd, 1: _mla_mixer_bwd, 2: _ret_mixer_bwd}
_MIX_PREFIX = {0: "lru_", 1: "mla_", 2: "ret_"}
_MIX_KEYS = {0: {"w_in": "lru_w_in", "conv_w": "lru_conv_w", "conv_b": "lru_conv_b", "w_a": "lru_w_a", "b_a": "lru_b_a",
                 "w_x": "lru_w_x", "b_x": "lru_b_x", "lam": "lru_lambda", "w_out": "lru_w_out"},
             1: {"w_in": "mla_w_in", "q_norm": "mla_q_norm", "kv_norm": "mla_kv_norm", "w_uq": "mla_w_uq",
                 "w_ukv": "mla_w_ukv", "w_out": "mla_w_out"},
             2: {"w_in": "ret_w_in", "gn_g": "ret_gn_g", "w_out": "ret_w_out"}}
_FFN_KEYS = {"w_up": "ffn_w_up", "conv_w": "ffn_conv_w", "conv_b": "ffn_conv_b", "w_down": "ffn_w_down"}


def _local_step(x, pos, target, full, rep):
    S, D = x.shape
    ts = _rt(S, 256)
    acts = []
    h = x
    for i in range(DEPTH):
        kind, mp, fp = _layer_params(full, rep, i)
        tag = "l%d" % i
        mix, msaved = _MIX_FWD[kind](h, pos, mp, tag + "m")
        h1, z1 = _ln_fwd(h, mix, rep["ln1_g"][i], rep["ln1_b"][i], ts=ts, name=tag + "_ln1")
        f, fsaved = _ffn_fwd(h1, fp, tag + "f")
        h2, z2 = _ln_fwd(h1, f, rep["ln2_g"][i], rep["ln2_b"][i], ts=ts, name=tag + "_ln2")
        acts.append((kind, mp, fp, h, msaved, h1, z1, fsaved, z2))
        h = h2
    dy, part = _loss_head(h, target, ts=ts, name="loss_head")

    grads = {n: {} for n in WEIGHTS if n not in BIG_AXIS}
    gbuf = {}
    d_a, d_b = dy, None
    for i in reversed(range(DEPTH)):
        kind, mp, fp, h_in, msaved, h1, z1, fsaved, z2 = acts[i]
        tag = "l%d" % i
        dz2, dg8, db8 = _ln_bwd(d_a, d_b, z2, rep["ln2_g"][i], ts=ts, name=tag + "_ln2b")
        grads["ln2_g"][i], grads["ln2_b"][i] = dg8.sum(axis=0), db8.sum(axis=0)
        dx_f, gf = _ffn_bwd(dz2, h1, fp, fsaved, tag + "f", gbuf, i)
        for k, v in gf.items():
            grads[_FFN_KEYS[k]][i] = v
        dz1, dg8, db8 = _ln_bwd(dz2, dx_f, z1, rep["ln1_g"][i], ts=ts, name=tag + "_ln1b")
        grads["ln1_g"][i], grads["ln1_b"][i] = dg8.sum(axis=0), db8.sum(axis=0)
        dx_m, gm = _MIX_BWD[kind](dz1, h_in, mp, msaved, tag + "m", gbuf, i // 3)
        for k, v in gm.items():
            grads[_MIX_KEYS[kind][k]][i // 3] = v
        d_a, d_b = dz1, dx_m
    grad_x = _axpy(d_a, d_b, ts=ts, name="grad_x")
    stacked = {n: jnp.stack([grads[n][j] for j in sorted(grads[n])]) for n in grads}
    return part, grad_x, stacked, gbuf


def kernel(x, positions, ln1_g, ln1_b, ln2_g, ln2_b, ffn_w_up, ffn_conv_w, ffn_conv_b, ffn_w_down, lru_w_in, lru_conv_w, lru_conv_b, lru_w_a, lru_b_a, lru_w_x, lru_b_x, lru_lambda, lru_w_out, mla_w_in, mla_q_norm, mla_kv_norm, mla_w_uq, mla_w_ukv, mla_w_out, ret_w_in, ret_gn_g, ret_w_out, loss_target, m_ln1_g, m_ln1_b, m_ln2_g, m_ln2_b, m_ffn_w_up, m_ffn_conv_w, m_ffn_conv_b, m_ffn_w_down, m_lru_w_in, m_lru_conv_w, m_lru_conv_b, m_lru_w_a, m_lru_b_a, m_lru_w_x, m_lru_b_x, m_lru_lambda, m_lru_w_out, m_mla_w_in, m_mla_q_norm, m_mla_kv_norm, m_mla_w_uq, m_mla_w_ukv, m_mla_w_out, m_ret_w_in, m_ret_gn_g, m_ret_w_out, v_ln1_g, v_ln1_b, v_ln2_g, v_ln2_b, v_ffn_w_up, v_ffn_conv_w, v_ffn_conv_b, v_ffn_w_down, v_lru_w_in, v_lru_conv_w, v_lru_conv_b, v_lru_w_a, v_lru_b_a, v_lru_w_x, v_lru_b_x, v_lru_lambda, v_lru_w_out, v_mla_w_in, v_mla_q_norm, v_mla_kv_norm, v_mla_w_uq, v_mla_w_ukv, v_mla_w_out, v_ret_w_in, v_ret_gn_g, v_ret_w_out):
    w = dict(ln1_g=ln1_g, ln1_b=ln1_b, ln2_g=ln2_g, ln2_b=ln2_b, ffn_w_up=ffn_w_up, ffn_conv_w=ffn_conv_w, ffn_conv_b=ffn_conv_b, ffn_w_down=ffn_w_down, lru_w_in=lru_w_in, lru_conv_w=lru_conv_w, lru_conv_b=lru_conv_b, lru_w_a=lru_w_a, lru_b_a=lru_b_a, lru_w_x=lru_w_x, lru_b_x=lru_b_x, lru_lambda=lru_lambda, lru_w_out=lru_w_out, mla_w_in=mla_w_in, mla_q_norm=mla_q_norm, mla_kv_norm=mla_kv_norm, mla_w_uq=mla_w_uq, mla_w_ukv=mla_w_ukv, mla_w_out=mla_w_out, ret_w_in=ret_w_in, ret_gn_g=ret_gn_g, ret_w_out=ret_w_out)
    m = dict(ln1_g=m_ln1_g, ln1_b=m_ln1_b, ln2_g=m_ln2_g, ln2_b=m_ln2_b, ffn_w_up=m_ffn_w_up, ffn_conv_w=m_ffn_conv_w, ffn_conv_b=m_ffn_conv_b, ffn_w_down=m_ffn_w_down, lru_w_in=m_lru_w_in, lru_conv_w=m_lru_conv_w, lru_conv_b=m_lru_conv_b, lru_w_a=m_lru_w_a, lru_b_a=m_lru_b_a, lru_w_x=m_lru_w_x, lru_b_x=m_lru_b_x, lru_lambda=m_lru_lambda, lru_w_out=m_lru_w_out, mla_w_in=m_mla_w_in, mla_q_norm=m_mla_q_norm, mla_kv_norm=m_mla_kv_norm, mla_w_uq=m_mla_w_uq, mla_w_ukv=m_mla_w_ukv, mla_w_out=m_mla_w_out, ret_w_in=m_ret_w_in, ret_gn_g=m_ret_gn_g, ret_w_out=m_ret_w_out)
    v = dict(ln1_g=v_ln1_g, ln1_b=v_ln1_b, ln2_g=v_ln2_g, ln2_b=v_ln2_b, ffn_w_up=v_ffn_w_up, ffn_conv_w=v_ffn_conv_w, ffn_conv_b=v_ffn_conv_b, ffn_w_down=v_ffn_w_down, lru_w_in=v_lru_w_in, lru_conv_w=v_lru_conv_w, lru_conv_b=v_lru_conv_b, lru_w_a=v_lru_w_a, lru_b_a=v_lru_b_a, lru_w_x=v_lru_w_x, lru_b_x=v_lru_b_x, lru_lambda=v_lru_lambda, lru_w_out=v_lru_w_out, mla_w_in=v_mla_w_in, mla_q_norm=v_mla_q_norm, mla_kv_norm=v_mla_kv_norm, mla_w_uq=v_mla_w_uq, mla_w_ukv=v_mla_w_ukv, mla_w_out=v_mla_w_out, ret_w_in=v_ret_w_in, ret_gn_g=v_ret_gn_g, ret_w_out=v_ret_w_out)
    D = x.shape[-1]
    axis_of = {n: ax for n, ax, _ in SHARDED}
    big = list(BIG_AXIS)
    small_mx = [n for n, _, mx in SHARDED if mx and n not in BIG_AXIS]
    small_vec = [n for n, _, mx in SHARDED if not mx]
    small = small_mx + small_vec

    gathered = _gather_chips([(w[n].astype(BF16), BIG_AXIS[n]) for n in big]
                             + [(_pack([w[n] for n in small_mx], BF16), None), (_pack([w[n] for n in small_vec], F32), None)],
                             "gather_weights")
    full = dict(zip(big, gathered))
    for names, buf in ((small_mx, gathered[-2]), (small_vec, gathered[-1])):
        blocks = _unpack(buf, [w[n].shape for n in names], lead=(N_CHIPS,))
        for n, blk in zip(names, blocks):
            full[n] = jnp.concatenate([blk[s] for s in range(N_CHIPS)], axis=axis_of[n])
    rep = {n: w[n] for n in REPLICATED}

    part, grad_x, grads, gbuf = _local_step(x[0], positions[0], loss_target[0], full, rep)
    loss = lax.psum((0.5 / D) * jnp.sum(part), MESH_AXES)

    g_pack = _pack([jnp.stack(jnp.split(grads[n], N_CHIPS, axis=axis_of[n])) for n in small], F32, lead=(N_CHIPS,))
    recv = _scatter_chips([(gbuf[n], BIG_AXIS[n]) for n in big] + [(g_pack, None)], "scatter_grads")
    sums = [_sum_partials(gbuf[n], r, BIG_AXIS[n], name="sum_" + n) for n, r in zip(big, recv)]
    sums.append(_sum_partials(g_pack[:, None], recv[-1][:, None], None, name="sum_small"))
    sibs = _swap_sibling(sums, "swap_core_partials")
    res = {kind: {} for kind in "gdmv"}
    for n, p_mine, p_sib in zip(big, sums, sibs):
        for kind, o in zip("gdmv", _adamw(w[n], m[n], v[n], p_mine, p_sib, name="adamw_" + n)):
            res[kind][n] = o
    spack = lambda d: _pack([d[n] for n in small], F32)[None]
    shapes = [w[n].shape for n in small]
    for kind, o in zip("gdmv", _adamw(spack(w), spack(m), spack(v), sums[-1], sibs[-1], name="adamw_small")):
        res[kind].update(zip(small, _unpack(o[0], shapes)))

    r_shapes = [w[n].shape for n in REPLICATED]
    rpack = lambda d: _pack([d[n] for n in REPLICATED], F32, rows=SUBLANES)
    r_sum = _allreduce_small(rpack(grads), "allreduce_replicated")
    r_outs = _adamw(rpack(w)[None], rpack(m)[None], rpack(v)[None], r_sum[None], None, name="adamw_replicated")
    for kind, o in zip("gdmv", r_outs):
        res[kind].update(zip(REPLICATED, _unpack(o[0], r_shapes)))

    return (loss, grad_x[None], *[res["g"][n] for n in WEIGHTS], *[res["d"][n] for n in WEIGHTS],
            *[res["m"][n] for n in WEIGHTS], *[res["v"][n] for n in WEIGHTS])
```

```python
import functools
import math

import numpy as np
import jax
import jax.numpy as jnp
from jax import lax
from jax.experimental import pallas as pl
from jax.experimental.pallas import tpu as pltpu

F32 = jnp.float32
BF16 = jnp.bfloat16

DEPTH = 4
ALPHA = (2.0 * DEPTH) ** 0.25
LN_EPS = 1e-5
RMS_EPS = 1e-6
ROPE_BASE = 10000.0
CHUNK = 64
LRU_C = 8.0
LRU_GROUPS = 4
MLA_HEADS = 16
QK_NOPE, QK_ROPE, V_HEAD = 64, 32, 64
Q_LORA, KV_LORA = 768, 256
RET_HEADS = 4
ADAM_LR, ADAM_B1, ADAM_B2, ADAM_EPS, ADAM_WD, ADAM_STEP = 0.001, 0.9, 0.999, 1e-08, 0.01, 10

LANES = 128
SUBLANES = 8
VMEM_LIMIT = 56 * 1024 * 1024

MESH_AXES = ("x", "y", "c")
N_CHIPS = 4


def _cp(sem):
    return pltpu.CompilerParams(dimension_semantics=sem, vmem_limit_bytes=VMEM_LIMIT)


def _sigmoid(x):
    return 1.0 / (1.0 + jnp.exp(-x))


_GELU_C = math.sqrt(2.0 / math.pi)


def _gelu_parts(x):
    x2 = x * x
    u = _GELU_C * (x + 0.044715 * x * x2)
    t = jnp.tanh(u)
    g = 0.5 * x * (1.0 + t)
    dg = 0.5 * (1.0 + t) + 0.5 * x * (1.0 - t * t) * _GELU_C * (1.0 + 3.0 * 0.044715 * x2)
    return g, dg


def _fold8(v):
    n = v.shape[0] // SUBLANES
    return v.reshape(n, SUBLANES, v.shape[1]).sum(axis=0)


def _dot(a, b):
    return jnp.dot(a, b, preferred_element_type=F32)


def _dot_tb(a, b):
    return lax.dot_general(a, b, (((1,), (1,)), ((), ())), preferred_element_type=F32)


def _dot_ta(a, b):
    return lax.dot_general(a, b, (((0,), (0,)), ((), ())), preferred_element_type=F32)


def _mm(a, b, *, out_dtype, tm, tn, name, a_koff=0):
    M = a.shape[0]
    K, N = b.shape

    def body(a_ref, b_ref, o_ref):
        o_ref[...] = _dot(a_ref[...].astype(BF16), b_ref[...].astype(BF16)).astype(out_dtype)

    return pl.pallas_call(
        body, name=name, grid=(M // tm, N // tn),
        in_specs=[pl.BlockSpec((tm, K), lambda i, j: (i, a_koff)),
                  pl.BlockSpec((K, tn), lambda i, j: (0, j))],
        out_specs=pl.BlockSpec((tm, tn), lambda i, j: (i, j)),
        out_shape=jax.ShapeDtypeStruct((M, N), out_dtype),
        compiler_params=_cp(("parallel", "parallel")),
    )(a, b)


def _mm_tb(pairs, b, *, out_dtype, tm, tk, name):
    M = pairs[0][0].shape[0]
    Kout = b.shape[0]
    n = len(pairs)

    def body(*refs):
        a_refs, b_refs, o_ref = refs[:n], refs[n:2 * n], refs[2 * n]
        acc = None
        for a_ref, b_ref in zip(a_refs, b_refs):
            t = _dot_tb(a_ref[...].astype(BF16), b_ref[...].astype(BF16))
            acc = t if acc is None else acc + t
        o_ref[...] = acc.astype(out_dtype)

    in_specs = [pl.BlockSpec((tm, a.shape[1]), lambda i, j: (i, 0)) for a, _ in pairs]
    for a, c0 in pairs:
        w = a.shape[1]
        assert c0 % w == 0
        in_specs.append(pl.BlockSpec((tk, w), functools.partial(lambda i, j, cb: (j, cb), cb=c0 // w)))
    return pl.pallas_call(
        body, name=name, grid=(M // tm, Kout // tk),
        in_specs=in_specs,
        out_specs=pl.BlockSpec((tm, tk), lambda i, j: (i, j)),
        out_shape=jax.ShapeDtypeStruct((M, Kout), out_dtype),
        compiler_params=_cp(("parallel", "parallel")),
    )(*[a for a, _ in pairs], *[b for _ in pairs])


def _mm_ta(a, b, *, tk, tn, tm, name, a_c0=0, a_w=None, b_c0=0, b_w=None, dest=None):
    M = a.shape[0]
    nm = M // tm
    a_w = a.shape[1] if a_w is None else a_w
    b_w = b.shape[1] if b_w is None else b_w
    assert a_c0 % tk == 0 and b_c0 % tn == 0 and a_w % tk == 0 and b_w % tn == 0

    def body(*refs):
        a_ref, b_ref = refs[0], refs[1]
        o_ref = refs[-1] if dest is None else refs[-2]

        @pl.when(pl.program_id(2) == 0)
        def _():
            o_ref[...] = jnp.zeros_like(o_ref)
        o_ref[...] += _dot_ta(a_ref[...].astype(BF16), b_ref[...].astype(BF16))

        if dest is not None:
            @pl.when(pl.program_id(2) == nm - 1)
            def _():
                refs[-1][...] = o_ref[...].astype(BF16)

    in_specs = [pl.BlockSpec((tm, tk), lambda i, j, m: (m, i + a_c0 // tk)),
                pl.BlockSpec((tm, tn), lambda i, j, m: (m, j + b_c0 // tn))]
    args = [a, b]
    if dest is None:
        out_spec = pl.BlockSpec((tk, tn), lambda i, j, m: (i, j))
        out_shape = jax.ShapeDtypeStruct((a_w, b_w), F32)
        aliases = {}
    else:
        bufs, full_shape, layer, r0, c0 = dest
        assert r0 % tk == 0 and c0 % tn == 0
        spec = pl.BlockSpec((None, tk, tn), lambda i, j, m: (layer, i + r0 // tk, j + c0 // tn))
        out_spec = [spec, spec]
        out_shape = [jax.ShapeDtypeStruct(full_shape, F32), jax.ShapeDtypeStruct(full_shape, BF16)]
        aliases = {}
        if bufs is not None:
            in_specs += [pl.BlockSpec(memory_space=pl.ANY)] * 2
            args += list(bufs)
            aliases = {2: 0, 3: 1}
    return pl.pallas_call(
        body, name=name, grid=(a_w // tk, b_w // tn, nm),
        in_specs=in_specs, out_specs=out_spec, out_shape=out_shape, input_output_aliases=aliases,
        compiler_params=_cp(("parallel", "parallel", "arbitrary")),
    )(*args)


def _grad_into(gbuf, key, full_shape, layer, r0, c0, a, b, **kw):
    gbuf[key] = tuple(_mm_ta(a, b, dest=(gbuf.get(key), full_shape, layer, r0, c0), **kw))


def _ln_fwd(x, mix, g, b, *, ts, name):
    S, D = x.shape

    def body(x_ref, m_ref, g_ref, b_ref, o_ref, z_ref):
        z = ALPHA * x_ref[...] + m_ref[...]
        mu = jnp.mean(z, axis=-1, keepdims=True)
        zc = z - mu
        var = jnp.mean(zc * zc, axis=-1, keepdims=True)
        o_ref[...] = zc * lax.rsqrt(var + LN_EPS) * g_ref[...] + b_ref[...]
        z_ref[...] = z

    row = pl.BlockSpec((ts, D), lambda i: (i, 0))
    vec = pl.BlockSpec((1, D), lambda i: (0, 0))
    return pl.pallas_call(
        body, name=name, grid=(S // ts,),
        in_specs=[row, row, vec, vec], out_specs=[row, row],
        out_shape=[jax.ShapeDtypeStruct((S, D), F32)] * 2,
        compiler_params=_cp(("parallel",)),
    )(x, mix, g.reshape(1, D), b.reshape(1, D))


def _ln_bwd(da, db, z, g, *, ts, name):
    S, D = z.shape
    two = db is not None

    def body(*refs):
        if two:
            da_ref, db_ref, z_ref, g_ref, dz_ref, dg_ref, dbias_ref = refs
            dout = ALPHA * da_ref[...] + db_ref[...]
        else:
            da_ref, z_ref, g_ref, dz_ref, dg_ref, dbias_ref = refs
            dout = da_ref[...]

        @pl.when(pl.program_id(0) == 0)
        def _():
            dg_ref[...] = jnp.zeros_like(dg_ref)
            dbias_ref[...] = jnp.zeros_like(dbias_ref)

        z = z_ref[...]
        mu = jnp.mean(z, axis=-1, keepdims=True)
        zc = z - mu
        var = jnp.mean(zc * zc, axis=-1, keepdims=True)
        rstd = lax.rsqrt(var + LN_EPS)
        xhat = zc * rstd
        dxh = dout * g_ref[...]
        m1 = jnp.mean(dxh, axis=-1, keepdims=True)
        m2 = jnp.mean(dxh * xhat, axis=-1, keepdims=True)
        dz_ref[...] = rstd * (dxh - m1 - xhat * m2)
        dg_ref[...] += _fold8(dout * xhat)
        dbias_ref[...] += _fold8(dout)

    row = pl.BlockSpec((ts, D), lambda i: (i, 0))
    vec = pl.BlockSpec((1, D), lambda i: (0, 0))
    acc = pl.BlockSpec((SUBLANES, D), lambda i: (0, 0))
    args = [da, db, z, g.reshape(1, D)] if two else [da, z, g.reshape(1, D)]
    return pl.pallas_call(
        body, name=name, grid=(S // ts,),
        in_specs=[row] * (3 if two else 2) + [vec],
        out_specs=[row, acc, acc],
        out_shape=[jax.ShapeDtypeStruct((S, D), F32), jax.ShapeDtypeStruct((SUBLANES, D), F32),
                   jax.ShapeDtypeStruct((SUBLANES, D), F32)],
        compiler_params=_cp(("arbitrary",)),
    )(*args)


def _loss_head(y, t, *, ts, name):
    S, D = y.shape

    def body(y_ref, t_ref, dy_ref, p_ref):
        @pl.when(pl.program_id(0) == 0)
        def _():
            p_ref[...] = jnp.zeros_like(p_ref)
        d = y_ref[...] - t_ref[...]
        dy_ref[...] = d * (1.0 / D)
        p_ref[...] += _fold8(d * d)

    row = pl.BlockSpec((ts, D), lambda i: (i, 0))
    acc = pl.BlockSpec((SUBLANES, D), lambda i: (0, 0))
    return pl.pallas_call(
        body, name=name, grid=(S // ts,),
        in_specs=[row, row], out_specs=[row, acc],
        out_shape=[jax.ShapeDtypeStruct((S, D), F32), jax.ShapeDtypeStruct((SUBLANES, D), F32)],
        compiler_params=_cp(("arbitrary",)),
    )(y, t)


def _axpy(a, b, *, ts, name):
    S, D = a.shape

    def body(a_ref, b_ref, o_ref):
        o_ref[...] = ALPHA * a_ref[...] + b_ref[...]

    row = pl.BlockSpec((ts, D), lambda i: (i, 0))
    return pl.pallas_call(
        body, name=name, grid=(S // ts,), in_specs=[row, row], out_specs=row,
        out_shape=jax.ShapeDtypeStruct((S, D), F32), compiler_params=_cp(("parallel",)),
    )(a, b)


def _prev_halo_spec(ts, tc, coff):
    r = ts // SUBLANES
    return pl.BlockSpec((SUBLANES, tc), lambda i, j: (jnp.maximum(i * r - 1, 0), j + coff))


def _fill_prev(buf, halo_ref, cur, i):
    buf[0:SUBLANES, :] = jnp.where(i > 0, halo_ref[...], 0.0)
    buf[SUBLANES:, :] = cur


def _conv_fwd(x, w, b, *, K, ts, tc, x_c0, name):
    S = x.shape[0]
    C = w.shape[1]
    coff = x_c0 // tc

    def body(x_ref, halo_ref, w_ref, b_ref, o_ref, buf):
        _fill_prev(buf, halo_ref, x_ref[...], pl.program_id(0))
        acc = b_ref[...] + w_ref[K - 1:K, :] * x_ref[...]
        for k in range(K - 1):
            acc = acc + w_ref[k:k + 1, :] * buf[pl.ds(SUBLANES - (K - 1) + k, ts), :]
        o_ref[...] = acc

    return pl.pallas_call(
        body, name=name, grid=(S // ts, C // tc),
        in_specs=[pl.BlockSpec((ts, tc), lambda i, j: (i, j + coff)), _prev_halo_spec(ts, tc, coff),
                  pl.BlockSpec((K, tc), lambda i, j: (0, j)), pl.BlockSpec((1, tc), lambda i, j: (0, j))],
        out_specs=pl.BlockSpec((ts, tc), lambda i, j: (i, j)),
        out_shape=jax.ShapeDtypeStruct((S, C), F32),
        scratch_shapes=[pltpu.VMEM((ts + SUBLANES, tc), F32)],
        compiler_params=_cp(("parallel", "parallel")),
    )(x, x, w, b.reshape(1, C))


def _conv_wgrad(dy, x, *, K, ts, tc, x_c0, name):
    S, C = dy.shape
    coff = x_c0 // tc

    def body(dy_ref, x_ref, halo_ref, dw_ref, db_ref, buf):
        i = pl.program_id(1)

        @pl.when(i == 0)
        def _():
            dw_ref[...] = jnp.zeros_like(dw_ref)
            db_ref[...] = jnp.zeros_like(db_ref)

        _fill_prev(buf, halo_ref, x_ref[...], i)
        dy_v = dy_ref[...]
        db_ref[...] += _fold8(dy_v)
        for k in range(K):
            xs = buf[pl.ds(SUBLANES - (K - 1) + k, ts), :]
            dw_ref[k] += _fold8(dy_v * xs)

    r = ts // SUBLANES
    return pl.pallas_call(
        body, name=name, grid=(C // tc, S // ts),
        in_specs=[pl.BlockSpec((ts, tc), lambda j, i: (i, j)),
                  pl.BlockSpec((ts, tc), lambda j, i: (i, j + coff)),
                  pl.BlockSpec((SUBLANES, tc), lambda j, i: (jnp.maximum(i * r - 1, 0), j + coff))],
        out_specs=[pl.BlockSpec((K, SUBLANES, tc), lambda j, i: (0, 0, j)),
                   pl.BlockSpec((SUBLANES, tc), lambda j, i: (0, j))],
        out_shape=[jax.ShapeDtypeStruct((K, SUBLANES, C), F32), jax.ShapeDtypeStruct((SUBLANES, C), F32)],
        scratch_shapes=[pltpu.VMEM((ts + SUBLANES, tc), F32)],
        compiler_params=_cp(("parallel", "arbitrary")),
    )(dy, x, x)


def _conv_bwd(dy, w, *, K, ts, tc, w_c0, out_dtype, name):
    S, C = dy.shape
    nb = S // ts
    r = ts // SUBLANES
    woff = w_c0 // tc

    def body(dy_ref, halo_ref, w_ref, o_ref, buf):
        i = pl.program_id(0)
        buf[0:ts, :] = dy_ref[...]
        buf[ts:, :] = jnp.where(i < nb - 1, halo_ref[...], 0.0)
        acc = w_ref[K - 1:K, :] * dy_ref[...]
        for k in range(K - 1):
            acc = acc + w_ref[k:k + 1, :] * buf[pl.ds(K - 1 - k, ts), :]
        o_ref[...] = acc.astype(out_dtype)

    return pl.pallas_call(
        body, name=name, grid=(nb, C // tc),
        in_specs=[pl.BlockSpec((ts, tc), lambda i, j: (i, j)),
                  pl.BlockSpec((SUBLANES, tc), lambda i, j: (jnp.minimum((i + 1) * r, nb * r - 1), j)),
                  pl.BlockSpec((K, tc), lambda i, j: (0, j + woff))],
        out_specs=pl.BlockSpec((ts, tc), lambda i, j: (i, j)),
        out_shape=jax.ShapeDtypeStruct((S, C), out_dtype),
        scratch_shapes=[pltpu.VMEM((ts + SUBLANES, tc), F32)],
        compiler_params=_cp(("parallel", "parallel")),
    )(dy, dy, w)


HALO16 = 16
FFN_UNROLL = 4


def _ffn_taps_w(w_ref, cs):
    return [w_ref[k:k + 1, cs] for k in range(3)]


def _ffn_taps8(prev, cur):
    row = lax.broadcasted_iota(jnp.int32, cur.shape, 0)
    return (jnp.where(row < 2, pltpu.roll(prev, 2, 0), pltpu.roll(cur, 2, 0)),
            jnp.where(row < 1, pltpu.roll(prev, 1, 0), pltpu.roll(cur, 1, 0)), cur)


def _ffn_conv8(taps, w, b):
    return b + w[0] * taps[0] + w[1] * taps[1] + w[2] * taps[2]


def _ffn_conv8_t(dh, dh_next, w):
    row = lax.broadcasted_iota(jnp.int32, dh.shape, 0)
    s1 = jnp.where(row < SUBLANES - 1, pltpu.roll(dh, SUBLANES - 1, 0), pltpu.roll(dh_next, SUBLANES - 1, 0))
    s2 = jnp.where(row < SUBLANES - 2, pltpu.roll(dh, SUBLANES - 2, 0), pltpu.roll(dh_next, SUBLANES - 2, 0))
    return w[2] * dh + w[1] * s1 + w[0] * s2


def _ffn_mid_specs(ts, tc, nf, nb, with_next):
    r = ts // HALO16
    specs = []
    for off in (0, nf):
        specs.append(pl.BlockSpec((ts, tc), functools.partial(lambda j, i, o: (i, j + o), o=off)))
        specs.append(pl.BlockSpec((HALO16, tc), functools.partial(lambda j, i, o: (jnp.maximum(i * r - 1, 0), j + o), o=off)))
        if with_next:
            specs.append(pl.BlockSpec(
                (HALO16, tc), functools.partial(lambda j, i, o: (jnp.minimum((i + 1) * r, nb * r - 1), j + o), o=off)))
    for rows in (3, 1):
        for off in (0, nf):
            specs.append(pl.BlockSpec((rows, tc), functools.partial(lambda j, i, o: (0, j + o), o=off)))
    return specs


def _ffn_mid_fwd(hpre, w, b, *, ts, tc, name):
    S, F2 = hpre.shape
    F = F2 // 2
    nf = F // tc

    def body(g_ref, gp_ref, u_ref, up_ref, wg_ref, wu_ref, bg_ref, bu_ref, o_ref, gbuf, ubuf, obuf):
        first = pl.program_id(1) == 0
        for buf, prev, cur in ((gbuf, gp_ref, g_ref), (ubuf, up_ref, u_ref)):
            buf[0:HALO16, :] = jnp.where(first, 0.0, prev[...].astype(F32))
            buf[HALO16:, :] = cur[...].astype(F32)
        for lt in range(tc // LANES):
            cs = slice(lt * LANES, (lt + 1) * LANES)
            wg, wu = _ffn_taps_w(wg_ref, cs), _ffn_taps_w(wu_ref, cs)
            bg, bu = bg_ref[:, cs], bu_ref[:, cs]

            def step(c, carry):
                a_g, a_u = carry
                for un in range(FFN_UNROLL):
                    r0 = pl.multiple_of(c * (FFN_UNROLL * SUBLANES), SUBLANES) + un * SUBLANES
                    b_g = gbuf[pl.ds(HALO16 + r0, SUBLANES), cs]
                    b_u = ubuf[pl.ds(HALO16 + r0, SUBLANES), cs]
                    gel, _ = _gelu_parts(_ffn_conv8(_ffn_taps8(a_g, b_g), wg, bg))
                    obuf[pl.ds(r0, SUBLANES), cs] = gel * _ffn_conv8(_ffn_taps8(a_u, b_u), wu, bu)
                    a_g, a_u = b_g, b_u
                return a_g, a_u

            lax.fori_loop(0, ts // (FFN_UNROLL * SUBLANES), step,
                          (gbuf[HALO16 - SUBLANES:HALO16, cs], ubuf[HALO16 - SUBLANES:HALO16, cs]))
        o_ref[...] = obuf[...].astype(BF16)

    b2 = b.reshape(1, F2)
    return pl.pallas_call(
        body, name=name, grid=(nf, S // ts),
        in_specs=_ffn_mid_specs(ts, tc, nf, S // ts, False),
        out_specs=pl.BlockSpec((ts, tc), lambda j, i: (i, j)),
        out_shape=jax.ShapeDtypeStruct((S, F), BF16),
        scratch_shapes=[pltpu.VMEM((ts + HALO16, tc), F32)] * 2 + [pltpu.VMEM((ts, tc), F32)],
        compiler_params=_cp(("parallel", "parallel")),
    )(hpre, hpre, hpre, hpre, w, w, b2, b2)


def _ffn_mid_bwd(hpre, da, w, b, *, ts, tc, name):
    S, F2 = hpre.shape
    F = F2 // 2
    nf = F // tc
    nb = S // ts
    r = ts // HALO16
    nch = ts // SUBLANES

    def body(g_ref, gp_ref, gn_ref, u_ref, up_ref, un_ref, wg_ref, wu_ref, bg_ref, bu_ref, da_ref, dan_ref,
             dpg_ref, dpu_ref, dwg_ref, dwu_ref, dbg_ref, dbu_ref, gbuf, ubuf, dabuf, pgbuf, pubuf):
        i = pl.program_id(1)

        @pl.when(i == 0)
        def _():
            for ref in (dwg_ref, dwu_ref, dbg_ref, dbu_ref):
                ref[...] = jnp.zeros_like(ref)

        for buf, prev, cur, nxt in ((gbuf, gp_ref, g_ref, gn_ref), (ubuf, up_ref, u_ref, un_ref)):
            buf[0:HALO16, :] = jnp.where(i == 0, 0.0, prev[...].astype(F32))
            buf[HALO16:HALO16 + ts, :] = cur[...].astype(F32)
            buf[HALO16 + ts:, :] = nxt[...].astype(F32)
        dabuf[0:ts, :] = da_ref[...].astype(F32)
        dabuf[ts:, :] = jnp.where(i == nb - 1, 0.0, dan_ref[...].astype(F32))

        for lt in range(tc // LANES):
            cs = slice(lt * LANES, (lt + 1) * LANES)
            wg, wu = _ffn_taps_w(wg_ref, cs), _ffn_taps_w(wu_ref, cs)
            bg, bu = bg_ref[:, cs], bu_ref[:, cs]

            def piece(r0, a_g, a_u):
                b_g = gbuf[pl.ds(HALO16 + r0, SUBLANES), cs]
                b_u = ubuf[pl.ds(HALO16 + r0, SUBLANES), cs]
                tg, tu = _ffn_taps8(a_g, b_g), _ffn_taps8(a_u, b_u)
                gel, dgel = _gelu_parts(_ffn_conv8(tg, wg, bg))
                da_v = dabuf[pl.ds(r0, SUBLANES), cs]
                return da_v * _ffn_conv8(tu, wu, bu) * dgel, da_v * gel, tg, tu, b_g, b_u

            def step(c, carry):
                a_g, a_u, pdg, pdu, acc = carry
                for un in range(FFN_UNROLL):
                    r0 = pl.multiple_of(c * (FFN_UNROLL * SUBLANES), SUBLANES) + un * SUBLANES
                    dg, du, tg, tu, a_g, a_u = piece(r0, a_g, a_u)
                    pgbuf[pl.ds(r0, SUBLANES), cs] = _ffn_conv8_t(pdg, dg, wg)
                    pubuf[pl.ds(r0, SUBLANES), cs] = _ffn_conv8_t(pdu, du, wu)
                    acc = (tuple(a + dg * t for a, t in zip(acc[0], tg)), tuple(a + du * t for a, t in zip(acc[1], tu)),
                           acc[2] + dg, acc[3] + du)
                    pdg, pdu = dg, du
                return a_g, a_u, pdg, pdu, acc

            zero = jnp.zeros((SUBLANES, LANES), F32)
            a_g, a_u, pdg, pdu, acc = lax.fori_loop(
                0, nch // FFN_UNROLL, step,
                (gbuf[HALO16 - SUBLANES:HALO16, cs], ubuf[HALO16 - SUBLANES:HALO16, cs], zero, zero,
                 ((zero,) * 3, (zero,) * 3, zero, zero)))
            dg, du, _, _, _, _ = piece(ts, a_g, a_u)
            pgbuf[ts:ts + SUBLANES, cs] = _ffn_conv8_t(pdg, dg, wg)
            pubuf[ts:ts + SUBLANES, cs] = _ffn_conv8_t(pdu, du, wu)
            for k in range(3):
                dwg_ref[k, :, cs] += acc[0][k]
                dwu_ref[k, :, cs] += acc[1][k]
            dbg_ref[:, cs] += acc[2]
            dbu_ref[:, cs] += acc[3]
        dpg_ref[...] = pgbuf[SUBLANES:, :].astype(BF16)
        dpu_ref[...] = pubuf[SUBLANES:, :].astype(BF16)

    b2 = b.reshape(1, F2)
    blk = pl.BlockSpec((ts, tc), lambda j, i: (i, j))
    nxt = pl.BlockSpec((HALO16, tc), lambda j, i: (jnp.minimum((i + 1) * r, nb * r - 1), j))
    in_specs = _ffn_mid_specs(ts, tc, nf, nb, True) + [blk, nxt]
    out_specs = [blk, blk,
                 pl.BlockSpec((3, SUBLANES, tc), lambda j, i: (0, 0, j)), pl.BlockSpec((3, SUBLANES, tc), lambda j, i: (0, 0, j)),
                 pl.BlockSpec((SUBLANES, tc), lambda j, i: (0, j)), pl.BlockSpec((SUBLANES, tc), lambda j, i: (0, j))]
    return pl.pallas_call(
        body, name=name, grid=(nf, nb),
        in_specs=in_specs, out_specs=out_specs,
        out_shape=[jax.ShapeDtypeStruct((S, F), BF16)] * 2 + [jax.ShapeDtypeStruct((3, SUBLANES, F), F32)] * 2
                  + [jax.ShapeDtypeStruct((SUBLANES, F), F32)] * 2,
        scratch_shapes=[pltpu.VMEM((ts + 2 * HALO16, tc), F32)] * 2 + [pltpu.VMEM((ts + HALO16, tc), F32)]
                       + [pltpu.VMEM((ts + SUBLANES, tc), F32)] * 2,
        compiler_params=_cp(("parallel", "arbitrary")),
    )(hpre, hpre, hpre, hpre, hpre, hpre, w, w, b2, b2, da, da)


def _expm1(x):
    u = jnp.exp(x)
    um1 = u - 1.0
    safe = jnp.where(um1 == 0.0, 1.0, jnp.log(u))
    r = jnp.where(um1 == 0.0, x, um1 * x / safe)
    return jnp.where(x < -30.0, -1.0, r)


def _softplus(z):
    return jnp.maximum(z, 0.0) + jnp.log1p(jnp.exp(-jnp.abs(z)))


def _lru_gates(u, wa_ref, ba_ref, wx_ref, bx_ref, lam_ref):
    ub = u.astype(BF16)
    r = _sigmoid(_dot(ub, wa_ref[...]) + ba_ref[...])
    ig = _sigmoid(_dot(ub, wx_ref[...]) + bx_ref[...])
    sp = _softplus(-lam_ref[...])
    la = -LRU_C * r * sp
    a = jnp.exp(la)
    mult = jnp.sqrt(-_expm1(2.0 * la))
    return ub, r, ig, sp, a, mult


def _lru_specs(ts, gw):
    blk = pl.BlockSpec((ts, gw), lambda g, i: (i, g))
    wsp = pl.BlockSpec((None, gw, gw), lambda g, i: (g, 0, 0))
    vsp = pl.BlockSpec((None, 1, gw), lambda g, i: (g, 0, 0))
    return blk, wsp, vsp


def _lru_fwd(u, proj, w_a, b_a, w_x, b_x, lam, *, ts, name):
    S, W = u.shape
    G = LRU_GROUPS
    gw = W // G
    nt = ts // SUBLANES

    def body(u_ref, gb_ref, wa_ref, ba_ref, wx_ref, bx_ref, lam_ref, y_ref, h_ref, a_buf, b_buf, carry):
        @pl.when(pl.program_id(1) == 0)
        def _():
            carry[...] = jnp.zeros_like(carry)

        u_v = u_ref[...]
        _, _, ig, _, a, mult = _lru_gates(u_v, wa_ref, ba_ref, wx_ref, bx_ref, lam_ref)
        a_buf[...] = a
        b_buf[...] = mult * ig * u_v
        row = lax.broadcasted_iota(jnp.int32, (SUBLANES, gw), 0)

        def tile(k, c):
            r0 = pl.multiple_of(k * SUBLANES, SUBLANES)
            A = a_buf[pl.ds(r0, SUBLANES), :]
            B = b_buf[pl.ds(r0, SUBLANES), :]
            for d in (1, 2, 4):
                m = row >= d
                B = jnp.where(m, A * pltpu.roll(B, d, 0) + B, B)
                A = jnp.where(m, A * pltpu.roll(A, d, 0), A)
            h = A * c + B
            h_ref[pl.ds(r0, SUBLANES), :] = h
            return h[SUBLANES - 1:SUBLANES, :]

        carry[...] = lax.fori_loop(0, nt, tile, carry[...])
        gel, _ = _gelu_parts(gb_ref[...])
        y_ref[...] = (gel * h_ref[...]).astype(BF16)

    blk, wsp, vsp = _lru_specs(ts, gw)
    return pl.pallas_call(
        body, name=name, grid=(G, S // ts),
        in_specs=[blk, blk, wsp, vsp, wsp, vsp, vsp],
        out_specs=[blk, blk],
        out_shape=[jax.ShapeDtypeStruct((S, W), BF16), jax.ShapeDtypeStruct((S, W), F32)],
        scratch_shapes=[pltpu.VMEM((ts, gw), F32), pltpu.VMEM((ts, gw), F32), pltpu.VMEM((1, gw), F32)],
        compiler_params=_cp(("parallel", "arbitrary")),
    )(u, proj, w_a, b_a.reshape(G, 1, gw), w_x, b_x.reshape(G, 1, gw), lam.reshape(G, 1, gw))


def _lru_bwd(dy, u, proj, h, w_a, b_a, w_x, b_x, lam, *, ts, name):
    S, W = u.shape
    G = LRU_GROUPS
    gw = W // G
    nt = ts // SUBLANES
    nb = S // ts
    r8 = ts // SUBLANES

    def body(dy_ref, u_ref, gb_ref, h_ref, hh_ref, wa_ref, ba_ref, wx_ref, bx_ref, lam_ref,
             dgb_ref, du_ref, dwa_ref, dwx_ref, dba_ref, dbx_ref, dlam_ref,
             a_buf, q_buf, p_buf, hbuf, carry):
        i = pl.program_id(1)
        ib = nb - 1 - i

        @pl.when(i == 0)
        def _():
            carry[...] = jnp.zeros_like(carry)
            dwa_ref[...] = jnp.zeros_like(dwa_ref)
            dwx_ref[...] = jnp.zeros_like(dwx_ref)
            dba_ref[...] = jnp.zeros_like(dba_ref)
            dbx_ref[...] = jnp.zeros_like(dbx_ref)
            dlam_ref[...] = jnp.zeros_like(dlam_ref)

        u_v = u_ref[...]
        ub, r, ig, sp, a, mult = _lru_gates(u_v, wa_ref, ba_ref, wx_ref, bx_ref, lam_ref)
        gel, dgel = _gelu_parts(gb_ref[...])
        dy_v = dy_ref[...]
        h_v = h_ref[...]
        dh = dy_v * gel
        dgb_ref[...] = (dy_v * h_v * dgel).astype(BF16)
        a_buf[...] = a
        q_buf[...] = a * dh
        row = lax.broadcasted_iota(jnp.int32, (SUBLANES, gw), 0)

        def tile(kk, c):
            r0 = pl.multiple_of((nt - 1 - kk) * SUBLANES, SUBLANES)
            A = a_buf[pl.ds(r0, SUBLANES), :]
            B = q_buf[pl.ds(r0, SUBLANES), :]
            for d in (1, 2, 4):
                m = row < SUBLANES - d
                B = jnp.where(m, A * pltpu.roll(B, SUBLANES - d, 0) + B, B)
                A = jnp.where(m, A * pltpu.roll(A, SUBLANES - d, 0), A)
            P = A * c + B
            p_buf[pl.ds(r0, SUBLANES), :] = jnp.where(row == SUBLANES - 1, c, pltpu.roll(P, SUBLANES - 1, 0))
            return P[0:1, :]

        carry[...] = lax.fori_loop(0, nt, tile, carry[...])
        Gt = dh + p_buf[...]
        hbuf[0:SUBLANES, :] = jnp.where(ib > 0, hh_ref[...], 0.0)
        hbuf[SUBLANES:, :] = h_v
        hprev = hbuf[pl.ds(SUBLANES - 1, ts), :]
        da = Gt * hprev
        dmult = Gt * (ig * u_v)
        dla = da * a - dmult * (a * a) / mult
        dr = dla * (-LRU_C * sp)
        dlam_ref[...] += _fold8(dla * (LRU_C * r)) * _sigmoid(-lam_ref[...])
        dig = Gt * mult * u_v
        dzr = dr * r * (1.0 - r)
        dzi = dig * ig * (1.0 - ig)
        dzr_b = dzr.astype(BF16)
        dzi_b = dzi.astype(BF16)
        du_ref[...] = Gt * mult * ig + _dot_tb(dzr_b, wa_ref[...]) + _dot_tb(dzi_b, wx_ref[...])
        dwa_ref[...] += _dot_ta(ub, dzr_b)
        dwx_ref[...] += _dot_ta(ub, dzi_b)
        dba_ref[...] += _fold8(dzr)
        dbx_ref[...] += _fold8(dzi)

    rblk = pl.BlockSpec((ts, gw), lambda g, i: (nb - 1 - i, g))
    halo = pl.BlockSpec((SUBLANES, gw), lambda g, i: (jnp.maximum((nb - 1 - i) * r8 - 1, 0), g))
    wsp = pl.BlockSpec((None, gw, gw), lambda g, i: (g, 0, 0))
    vsp = pl.BlockSpec((None, 1, gw), lambda g, i: (g, 0, 0))
    acc8 = pl.BlockSpec((None, SUBLANES, gw), lambda g, i: (g, 0, 0))
    return pl.pallas_call(
        body, name=name, grid=(G, nb),
        in_specs=[rblk, rblk, rblk, rblk, halo, wsp, vsp, wsp, vsp, vsp],
        out_specs=[rblk, rblk, wsp, wsp, acc8, acc8, acc8],
        out_shape=[jax.ShapeDtypeStruct((S, W), BF16), jax.ShapeDtypeStruct((S, W), F32),
                   jax.ShapeDtypeStruct((G, gw, gw), F32), jax.ShapeDtypeStruct((G, gw, gw), F32),
                   jax.ShapeDtypeStruct((G, SUBLANES, gw), F32), jax.ShapeDtypeStruct((G, SUBLANES, gw), F32),
                   jax.ShapeDtypeStruct((G, SUBLANES, gw), F32)],
        scratch_shapes=[pltpu.VMEM((ts, gw), F32)] * 3 + [pltpu.VMEM((ts + SUBLANES, gw), F32),
                                                          pltpu.VMEM((1, gw), F32)],
        compiler_params=_cp(("parallel", "arbitrary")),
    )(dy, u, proj, h, h, w_a, b_a.reshape(G, 1, gw), w_x, b_x.reshape(G, 1, gw), lam.reshape(G, 1, gw))


def _rt(S, pref):
    return min(S, pref)


def _lru_mixer_fwd(x, p, tag):
    S, D = x.shape
    W = p["w_out"].shape[0]
    ts = _rt(S, 512)
    proj = _mm(x, p["w_in"], out_dtype=F32, tm=ts, tn=W, name=tag + "_proj")
    u = _conv_fwd(proj, p["conv_w"], p["conv_b"], K=4, ts=ts, tc=512, x_c0=W, name=tag + "_conv")
    y, h = _lru_fwd(u, proj, p["w_a"], p["b_a"], p["w_x"], p["b_x"], p["lam"], ts=ts, name=tag + "_scan")
    mix = _mm(y, p["w_out"], out_dtype=F32, tm=ts, tn=D, name=tag + "_out")
    return mix, (proj, u, h, y)


def _lru_mixer_bwd(dmix, x, p, saved, tag, gbuf, j):
    proj, u, h, y = saved
    S, D = x.shape
    W = p["w_out"].shape[0]
    ts = _rt(S, 512)
    g = {}
    dy = _mm_tb([(dmix, 0)], p["w_out"], out_dtype=F32, tm=ts, tk=W, name=tag + "_dy")
    n_lru = (DEPTH + 2) // 3
    _grad_into(gbuf, "lru_w_out", (n_lru, W, D), j, 0, 0, y, dmix, tk=W, tn=D, tm=ts, name=tag + "_dwout")
    dgb, du, g["w_a"], g["w_x"], dba8, dbx8, dlam8 = _lru_bwd(
        dy, u, proj, h, p["w_a"], p["b_a"], p["w_x"], p["b_x"], p["lam"], ts=ts, name=tag + "_scanb")
    g["b_a"] = dba8.sum(axis=1)
    g["b_x"] = dbx8.sum(axis=1)
    g["lam"] = dlam8.sum(axis=1).reshape(-1)
    dcw8, dcb8 = _conv_wgrad(du, proj, K=4, ts=ts, tc=512, x_c0=W, name=tag + "_convw")
    g["conv_w"] = dcw8.sum(axis=1)
    g["conv_b"] = dcb8.sum(axis=0)
    drnn = _conv_bwd(du, p["conv_w"], K=4, ts=ts, tc=512, w_c0=0, out_dtype=BF16, name=tag + "_convb")
    dx = _mm_tb([(dgb, 0), (drnn, W)], p["w_in"], out_dtype=F32, tm=ts, tk=D, name=tag + "_dx")
    _grad_into(gbuf, "lru_w_in", (n_lru, D, 2 * W), j, 0, 0, x, dgb, tk=D, tn=W, tm=ts, name=tag + "_dwin_g")
    _grad_into(gbuf, "lru_w_in", (n_lru, D, 2 * W), j, 0, W, x, drnn, tk=D, tn=W, tm=ts, name=tag + "_dwin_r")
    return dx, g


def _ffn_fwd(x, p, tag):
    S, D = x.shape
    F = p["w_down"].shape[0]
    ts = _rt(S, 512)
    tc = F // 2
    hpre = _mm(x, p["w_up"], out_dtype=BF16, tm=ts, tn=tc, name=tag + "_up")
    a = _ffn_mid_fwd(hpre, p["conv_w"], p["conv_b"], ts=_rt(S, 256), tc=tc, name=tag + "_mid")
    f = _mm(a, p["w_down"], out_dtype=F32, tm=ts, tn=D, name=tag + "_down")
    return f, (hpre, a)


def _ffn_bwd(df, x, p, saved, tag, gbuf, i):
    hpre, a = saved
    S, D = x.shape
    F = p["w_down"].shape[0]
    ts = _rt(S, 512)
    tc = F // 2
    g = {}
    da = _mm_tb([(df, 0)], p["w_down"], out_dtype=BF16, tm=ts, tk=tc, name=tag + "_da")
    _grad_into(gbuf, "ffn_w_down", (DEPTH, F, D), i, 0, 0, a, df, tk=tc, tn=D, tm=ts, name=tag + "_dwdown")
    dpg, dpu, dwg8, dwu8, dbg8, dbu8 = _ffn_mid_bwd(hpre, da, p["conv_w"], p["conv_b"], ts=_rt(S, 256), tc=tc,
                                                    name=tag + "_midb")
    g["conv_w"] = jnp.concatenate([dwg8.sum(axis=1), dwu8.sum(axis=1)], axis=1)
    g["conv_b"] = jnp.concatenate([dbg8.sum(axis=0), dbu8.sum(axis=0)], axis=0)
    dx = _mm_tb([(dpg, 0), (dpu, F)], p["w_up"], out_dtype=F32, tm=ts, tk=D, name=tag + "_dx")
    _grad_into(gbuf, "ffn_w_up", (DEPTH, D, 2 * F), i, 0, 0, x, dpg, tk=D, tn=tc, tm=ts, name=tag + "_dwup_g")
    _grad_into(gbuf, "ffn_w_up", (DEPTH, D, 2 * F), i, 0, F, x, dpu, tk=D, tn=tc, tm=ts, name=tag + "_dwup_u")
    return dx, g


HEAD_SLOT = LANES
MLA_SCALE = (QK_NOPE + QK_ROPE) ** -0.5
NEG_BIG = -1e30
ATTN_BLOCK = 1024


def _rms_fwd(x, g, *, c0, ts, name):
    S = x.shape[0]
    w = g.shape[0]

    def body(x_ref, g_ref, o_ref):
        xv = x_ref[...]
        rstd = lax.rsqrt(jnp.mean(xv * xv, axis=-1, keepdims=True) + RMS_EPS)
        o_ref[...] = (xv * rstd * g_ref[...]).astype(BF16)

    return pl.pallas_call(
        body, name=name, grid=(S // ts,),
        in_specs=[pl.BlockSpec((ts, w), lambda i: (i, c0 // w)), pl.BlockSpec((1, w), lambda i: (0, 0))],
        out_specs=pl.BlockSpec((ts, w), lambda i: (i, 0)),
        out_shape=jax.ShapeDtypeStruct((S, w), BF16), compiler_params=_cp(("parallel",)),
    )(x, g.reshape(1, w))


def _rms_bwd(dy, x, g, *, c0, ts, name):
    S = x.shape[0]
    w = g.shape[0]

    def body(dy_ref, x_ref, g_ref, dx_ref, dg_ref):
        @pl.when(pl.program_id(0) == 0)
        def _():
            dg_ref[...] = jnp.zeros_like(dg_ref)
        xv = x_ref[...]
        dyv = dy_ref[...]
        rstd = lax.rsqrt(jnp.mean(xv * xv, axis=-1, keepdims=True) + RMS_EPS)
        dyg = dyv * g_ref[...]
        m = jnp.mean(dyg * xv, axis=-1, keepdims=True)
        dx_ref[...] = (rstd * (dyg - xv * (rstd * rstd) * m)).astype(BF16)
        dg_ref[...] += _fold8(dyv * xv * rstd)

    return pl.pallas_call(
        body, name=name, grid=(S // ts,),
        in_specs=[pl.BlockSpec((ts, w), lambda i: (i, 0)), pl.BlockSpec((ts, w), lambda i: (i, c0 // w)),
                  pl.BlockSpec((1, w), lambda i: (0, 0))],
        out_specs=[pl.BlockSpec((ts, w), lambda i: (i, 0)), pl.BlockSpec((SUBLANES, w), lambda i: (0, 0))],
        out_shape=[jax.ShapeDtypeStruct((S, w), BF16), jax.ShapeDtypeStruct((SUBLANES, w), F32)],
        compiler_params=_cp(("arbitrary",)),
    )(dy, x, g.reshape(1, w))


def _mla_tables(pos):
    S = pos.shape[0]
    half = QK_ROPE // 2
    inv_freq = ROPE_BASE ** (-jnp.arange(0, QK_ROPE, 2, dtype=F32) / QK_ROPE)
    ang = pos.astype(F32)[:, None] * inv_freq
    cos, sin = jnp.cos(ang), jnp.sin(ang)
    z = lambda n: jnp.zeros((S, n), F32)
    pad = HEAD_SLOT - QK_NOPE - QK_ROPE
    c = jnp.concatenate([jnp.ones((S, QK_NOPE), F32), cos, cos, z(pad)], axis=1)
    s1 = jnp.concatenate([z(QK_NOPE), -sin, z(half), z(pad)], axis=1)
    s2 = jnp.concatenate([z(QK_NOPE), z(half), sin, z(pad)], axis=1)
    return c, s1, s2


def _mla_prep_fwd(qraw, kvraw, proj, tabs, *, kpe_c0, ts, name):
    S = qraw.shape[0]
    H = MLA_HEADS
    half = QK_ROPE // 2

    def body(q_ref, kn_ref, kpe_ref, c_ref, s1_ref, s2_ref, qo_ref, ko_ref):
        c, s1, s2 = c_ref[...], s1_ref[...], s2_ref[...]

        def rope(v):
            return v * c + pltpu.roll(v, HEAD_SLOT - half, 1) * s1 + pltpu.roll(v, half, 1) * s2

        qo_ref[...] = (rope(q_ref[...]) * MLA_SCALE).astype(BF16)
        ko_ref[...] = (kn_ref[...] + rope(kpe_ref[...])).astype(BF16)

    slot = pl.BlockSpec((ts, HEAD_SLOT), lambda i, h: (i, h))
    tab = pl.BlockSpec((ts, HEAD_SLOT), lambda i, h: (i, 0))
    return pl.pallas_call(
        body, name=name, grid=(S // ts, H),
        in_specs=[slot, slot, pl.BlockSpec((ts, HEAD_SLOT), lambda i, h: (i, kpe_c0 // HEAD_SLOT)), tab, tab, tab],
        out_specs=[slot, slot],
        out_shape=[jax.ShapeDtypeStruct((S, H * HEAD_SLOT), BF16)] * 2,
        compiler_params=_cp(("parallel", "parallel")),
    )(qraw, kvraw, proj, *tabs)


def _mla_prep_bwd(dq, dk, dv, tabs, *, ts, name):
    S = dk.shape[0]
    H = MLA_HEADS
    half = QK_ROPE // 2
    kw = H * HEAD_SLOT
    vw = H * V_HEAD

    def body(dq_ref, dk_ref, dv_ref, c_ref, s1_ref, s2_ref, dqr_ref, dkv_ref, dkpe_ref):
        c, s1, s2 = c_ref[...], s1_ref[...], s2_ref[...]

        def rope_t(g):
            return g * c + pltpu.roll(g * s1, half, 1) + pltpu.roll(g * s2, HEAD_SLOT - half, 1)

        gsum = jnp.zeros((ts, HEAD_SLOT), F32)
        for h in range(H):
            sl = slice(h * HEAD_SLOT, (h + 1) * HEAD_SLOT)
            hs = slice((h % 2) * HEAD_SLOT, (h % 2 + 1) * HEAD_SLOT)
            dqr_ref[:, sl] = (rope_t(dq_ref[h // 2, :, hs]) * MLA_SCALE).astype(BF16)
            dkh = dk_ref[:, sl]
            dkv_ref[:, sl] = dkh.astype(BF16)
            gsum = gsum + dkh
        dkv_ref[:, kw:] = dv_ref[...].astype(BF16)
        lane = lax.broadcasted_iota(jnp.int32, (ts, HEAD_SLOT), 1)
        pe = jnp.logical_and(lane >= QK_NOPE, lane < QK_NOPE + QK_ROPE)
        dkpe_ref[...] = rope_t(jnp.where(pe, gsum, 0.0)).astype(BF16)

    tab = pl.BlockSpec((ts, HEAD_SLOT), lambda i: (i, 0))
    return pl.pallas_call(
        body, name=name, grid=(S // ts,),
        in_specs=[pl.BlockSpec((H // 2, ts, 2 * HEAD_SLOT), lambda i: (0, i, 0)), pl.BlockSpec((ts, kw), lambda i: (i, 0)),
                  pl.BlockSpec((ts, vw), lambda i: (i, 0)), tab, tab, tab],
        out_specs=[pl.BlockSpec((ts, kw), lambda i: (i, 0)), pl.BlockSpec((ts, kw + vw), lambda i: (i, 0)), tab],
        out_shape=[jax.ShapeDtypeStruct((S, kw), BF16), jax.ShapeDtypeStruct((S, kw + vw), BF16),
                   jax.ShapeDtypeStruct((S, HEAD_SLOT), BF16)],
        compiler_params=_cp(("parallel",)),
    )(dq, dk, dv, *tabs)


def _attn_pairs(nb, kv_outer):
    if kv_outer:
        pr = [(i, j) for j in range(nb) for i in range(j, nb)]
    else:
        pr = [(i, j) for i in range(nb) for j in range(i + 1)]
    return (jnp.asarray(np.array([p[0] for p in pr], np.int32)), jnp.asarray(np.array([p[1] for p in pr], np.int32)))


def _attn_scores(q_ref, k_ref, hh, diag, T):
    sl = slice(hh * HEAD_SLOT, (hh + 1) * HEAD_SLOT)
    s = _dot_tb(q_ref[:, sl], k_ref[:, sl])
    if not diag:
        return s
    row = lax.broadcasted_iota(jnp.int32, (T, T), 0) // CHUNK
    col = lax.broadcasted_iota(jnp.int32, (T, T), 1) // CHUNK
    return jnp.where(col <= row, s, NEG_BIG)


def _on_block_kind(i, j, step):
    @pl.when(i == j)
    def _():
        step(True)

    @pl.when(i != j)
    def _():
        step(False)


def _attn_fwd(q, k, kvraw, *, T, name):
    S = q.shape[0]
    NP = MLA_HEADS // 2
    nb = S // T
    ii, jj = _attn_pairs(nb, kv_outer=False)
    v_c0 = MLA_HEADS * HEAD_SLOT // LANES

    def body(ii_ref, jj_ref, q_ref, k_ref, v_ref, o_ref, lse_ref, m_sc, l_sc, acc_sc):
        t = pl.program_id(1)
        i, j = ii_ref[t], jj_ref[t]

        @pl.when(j == 0)
        def _():
            m_sc[...] = jnp.full_like(m_sc, NEG_BIG)
            l_sc[...] = jnp.zeros_like(l_sc)
            acc_sc[...] = jnp.zeros_like(acc_sc)

        lo = lax.broadcasted_iota(jnp.int32, (T, LANES), 1) < V_HEAD

        def step(diag):
            v = v_ref[...].astype(BF16)
            vh = (jnp.where(lo, v, jnp.zeros_like(v)), jnp.where(lo, jnp.zeros_like(v), v))
            alphas, pv = [], None
            for hh in range(2):
                s = _attn_scores(q_ref, k_ref, hh, diag, T)
                m_prev = m_sc[hh]
                m_new = jnp.maximum(m_prev, jnp.max(s, axis=1, keepdims=True))
                p = jnp.exp(s - jnp.tile(m_new, (1, T // LANES)))
                alpha = jnp.exp(m_prev - m_new)
                l_sc[hh] = alpha * l_sc[hh] + jnp.sum(p, axis=1, keepdims=True)
                m_sc[hh] = m_new
                alphas.append(alpha)
                t_pv = _dot(p.astype(BF16), vh[hh])
                pv = t_pv if pv is None else pv + t_pv
            acc_sc[...] = acc_sc[...] * jnp.where(lo, alphas[0], alphas[1]) + pv

        _on_block_kind(i, j, step)

        @pl.when(j == i)
        def _():
            l0, l1 = l_sc[0], l_sc[1]
            o_ref[...] = acc_sc[...] * jnp.where(lo, 1.0 / l0, 1.0 / l1)
            lse_ref[...] = jnp.where(lo, m_sc[0] + jnp.log(l0), m_sc[1] + jnp.log(l1))

    grid_spec = pltpu.PrefetchScalarGridSpec(
        num_scalar_prefetch=2, grid=(NP, int(ii.shape[0])),
        in_specs=[pl.BlockSpec((T, 2 * HEAD_SLOT), lambda p, t, ii, jj: (ii[t], p)),
                  pl.BlockSpec((T, 2 * HEAD_SLOT), lambda p, t, ii, jj: (jj[t], p)),
                  pl.BlockSpec((T, LANES), lambda p, t, ii, jj: (jj[t], v_c0 + p))],
        out_specs=[pl.BlockSpec((T, LANES), lambda p, t, ii, jj: (ii[t], p)),
                   pl.BlockSpec((None, T, LANES), lambda p, t, ii, jj: (p, ii[t], 0))],
        scratch_shapes=[pltpu.VMEM((2, T, LANES), F32), pltpu.VMEM((2, T, LANES), F32), pltpu.VMEM((T, LANES), F32)])
    return pl.pallas_call(
        body, name=name, grid_spec=grid_spec,
        out_shape=[jax.ShapeDtypeStruct((S, MLA_HEADS * V_HEAD), F32), jax.ShapeDtypeStruct((NP, S, LANES), F32)],
        compiler_params=_cp(("parallel", "arbitrary")),
    )(ii, jj, q, k, kvraw)


def _attn_bwd_common(q_ref, k_ref, v, do, o, lse, hh, diag, T, lo):
    sel = lo if hh == 0 else jnp.logical_not(lo)
    s = _attn_scores(q_ref, k_ref, hh, diag, T)
    p = jnp.exp(s - lse[:, hh * V_HEAD:hh * V_HEAD + 1])
    do_h = jnp.where(sel, do, 0.0)
    dsum = jnp.sum(do_h * o, axis=1, keepdims=True)
    do_hb = do_h.astype(BF16)
    dp = _dot_tb(do_hb, v)
    return p, p * (dp - dsum), do_hb


def _attn_bwd(q, k, kvraw, do, o, lse, *, T, name):
    S = q.shape[0]
    NP = MLA_HEADS // 2
    nb = S // T
    ii, jj = _attn_pairs(nb, kv_outer=True)
    n_steps = int(ii.shape[0])
    v_c0 = MLA_HEADS * HEAD_SLOT // LANES

    def body(ii_ref, jj_ref, q_ref, k_ref, v_ref, do_ref, o_ref, lse_ref, dq_hbm, dk_ref, dv_ref,
             dq_sc, dk_sc, dv_sc, sem):
        pair = pl.program_id(0)
        t = pl.program_id(1)
        i, j = ii_ref[t], jj_ref[t]

        @pl.when(t == 0)
        def _():
            dq_sc[...] = jnp.zeros_like(dq_sc)

        @pl.when(i == j)
        def _():
            dk_sc[...] = jnp.zeros_like(dk_sc)
            dv_sc[...] = jnp.zeros_like(dv_sc)

        lo = lax.broadcasted_iota(jnp.int32, (T, LANES), 1) < V_HEAD
        rows = pl.ds(pl.multiple_of(i * T, T), T)

        def step(diag):
            v = v_ref[...].astype(BF16)
            do, o_v, lse_v = do_ref[...], o_ref[...], lse_ref[...]
            for hh in range(2):
                sl = slice(hh * HEAD_SLOT, (hh + 1) * HEAD_SLOT)
                p, ds, do_hb = _attn_bwd_common(q_ref, k_ref, v, do, o_v, lse_v, hh, diag, T, lo)
                dsb = ds.astype(BF16)
                dv_sc[...] += _dot_ta(p.astype(BF16), do_hb)
                dk_sc[:, sl] += _dot_ta(dsb, q_ref[:, sl])
                dq_sc[rows, sl] += _dot(dsb, k_ref[:, sl])

        _on_block_kind(i, j, step)

        @pl.when(i == nb - 1)
        def _():
            dk_ref[...] = dk_sc[...]
            dv_ref[...] = dv_sc[...]

        @pl.when(t == n_steps - 1)
        def _():
            cp = pltpu.make_async_copy(dq_sc, dq_hbm.at[pair], sem)
            cp.start()
            cp.wait()

    qi = lambda p, t, ii, jj: (ii[t], p)
    kj = lambda p, t, ii, jj: (jj[t], p)
    grid_spec = pltpu.PrefetchScalarGridSpec(
        num_scalar_prefetch=2, grid=(NP, n_steps),
        in_specs=[pl.BlockSpec((T, 2 * HEAD_SLOT), qi), pl.BlockSpec((T, 2 * HEAD_SLOT), kj),
                  pl.BlockSpec((T, LANES), lambda p, t, ii, jj: (jj[t], v_c0 + p)),
                  pl.BlockSpec((T, LANES), qi), pl.BlockSpec((T, LANES), qi),
                  pl.BlockSpec((None, T, LANES), lambda p, t, ii, jj: (p, ii[t], 0))],
        out_specs=[pl.BlockSpec(memory_space=pl.ANY), pl.BlockSpec((T, 2 * HEAD_SLOT), kj),
                   pl.BlockSpec((T, LANES), kj)],
        scratch_shapes=[pltpu.VMEM((S, 2 * HEAD_SLOT), F32), pltpu.VMEM((T, 2 * HEAD_SLOT), F32),
                        pltpu.VMEM((T, LANES), F32), pltpu.SemaphoreType.DMA])
    return pl.pallas_call(
        body, name=name, grid_spec=grid_spec,
        out_shape=[jax.ShapeDtypeStruct((NP, S, 2 * HEAD_SLOT), F32),
                   jax.ShapeDtypeStruct((S, MLA_HEADS * HEAD_SLOT), F32),
                   jax.ShapeDtypeStruct((S, MLA_HEADS * V_HEAD), F32)],
        compiler_params=_cp(("arbitrary", "arbitrary")),
    )(ii, jj, q, k, kvraw, do, o, lse)


def _mla_permute_weights(w_in, w_uq, w_ukv):
    D = w_in.shape[0]
    H = MLA_HEADS
    qk = QK_NOPE + QK_ROPE
    lat = Q_LORA + KV_LORA
    kpe = jnp.zeros((D, HEAD_SLOT), w_in.dtype).at[:, QK_NOPE:qk].set(w_in[:, lat:])
    w_in_p = jnp.concatenate([w_in[:, :lat], kpe], axis=1)
    w_uq_p = jnp.pad(w_uq.reshape(Q_LORA, H, qk), ((0, 0), (0, 0), (0, HEAD_SLOT - qk))).reshape(Q_LORA, H * HEAD_SLOT)
    kv = w_ukv.reshape(KV_LORA, H, QK_NOPE + V_HEAD)
    wk = jnp.pad(kv[:, :, :QK_NOPE], ((0, 0), (0, 0), (0, HEAD_SLOT - QK_NOPE))).reshape(KV_LORA, H * HEAD_SLOT)
    wv = kv[:, :, QK_NOPE:].reshape(KV_LORA, H * V_HEAD)
    return w_in_p, w_uq_p, jnp.concatenate([wk, wv], axis=1)


def _mla_unpermute_grads(g_in_p, g_uq_p, g_ukv_p):
    H = MLA_HEADS
    qk = QK_NOPE + QK_ROPE
    lat = Q_LORA + KV_LORA
    g_in = jnp.concatenate([g_in_p[:, :lat], g_in_p[:, lat + QK_NOPE:lat + qk]], axis=1)
    g_uq = g_uq_p.reshape(Q_LORA, H, HEAD_SLOT)[:, :, :qk].reshape(Q_LORA, H * qk)
    gk = g_ukv_p[:, :H * HEAD_SLOT].reshape(KV_LORA, H, HEAD_SLOT)[:, :, :QK_NOPE]
    gv = g_ukv_p[:, H * HEAD_SLOT:].reshape(KV_LORA, H, V_HEAD)
    g_ukv = jnp.concatenate([gk, gv], axis=2).reshape(KV_LORA, H * (QK_NOPE + V_HEAD))
    return g_in, g_uq, g_ukv


def _mla_mixer_fwd(x, pos, p, tag):
    S, D = x.shape
    ts = _rt(S, 512)
    T = _rt(S, ATTN_BLOCK)
    lat = Q_LORA + KV_LORA
    tabs = _mla_tables(pos)
    proj = _mm(x, p["w_in_p"], out_dtype=F32, tm=ts, tn=p["w_in_p"].shape[1], name=tag + "_proj")
    qn = _rms_fwd(proj, p["q_norm"], c0=0, ts=ts, name=tag + "_qn")
    kvn = _rms_fwd(proj, p["kv_norm"], c0=Q_LORA, ts=ts, name=tag + "_kvn")
    qraw = _mm(qn, p["w_uq_p"], out_dtype=F32, tm=ts, tn=1024, name=tag + "_uq")
    kvraw = _mm(kvn, p["w_ukv_p"], out_dtype=F32, tm=ts, tn=1024, name=tag + "_ukv")
    q, k = _mla_prep_fwd(qraw, kvraw, proj, tabs, kpe_c0=lat, ts=ts, name=tag + "_prep")
    o, lse = _attn_fwd(q, k, kvraw, T=T, name=tag + "_attn")
    mix = _mm(o, p["w_out"], out_dtype=F32, tm=ts, tn=D, name=tag + "_out")
    return mix, (proj, qn, kvn, kvraw, q, k, o, lse, tabs)


def _mla_mixer_bwd(dmix, x, p, saved, tag, gbuf, j):
    proj, qn, kvn, kvraw, q, k, o, lse, tabs = saved
    S, D = x.shape
    ts = _rt(S, 512)
    T = _rt(S, ATTN_BLOCK)
    lat = Q_LORA + KV_LORA
    g = {}
    do = _mm_tb([(dmix, 0)], p["w_out"], out_dtype=F32, tm=ts, tk=p["w_out"].shape[0], name=tag + "_do")
    g["w_out"] = _mm_ta(o, dmix, tk=p["w_out"].shape[0], tn=D, tm=ts, name=tag + "_dwout")
    dq, dk, dv = _attn_bwd(q, k, kvraw, do, o, lse, T=T, name=tag + "_attn_bwd")
    dqraw, dkvraw, dkpe = _mla_prep_bwd(dq, dk, dv, tabs, ts=_rt(S, 256), name=tag + "_prepb")
    dqn = _mm_tb([(dqraw, 0)], p["w_uq_p"], out_dtype=F32, tm=ts, tk=Q_LORA, name=tag + "_dqn")
    g_uq_p = _mm_ta(qn, dqraw, tk=Q_LORA, tn=1024, tm=ts, name=tag + "_dwuq")
    dkvn = _mm_tb([(dkvraw, 0)], p["w_ukv_p"], out_dtype=F32, tm=ts, tk=KV_LORA, name=tag + "_dkvn")
    g_ukv_p = _mm_ta(kvn, dkvraw, tk=KV_LORA, tn=1024, tm=ts, name=tag + "_dwukv")
    dcq, dqg8 = _rms_bwd(dqn, proj, p["q_norm"], c0=0, ts=ts, name=tag + "_qnb")
    dckv, dkvg8 = _rms_bwd(dkvn, proj, p["kv_norm"], c0=Q_LORA, ts=ts, name=tag + "_kvnb")
    g["q_norm"] = dqg8.sum(axis=0)
    g["kv_norm"] = dkvg8.sum(axis=0)
    dx = _mm_tb([(dcq, 0), (dckv, Q_LORA), (dkpe, lat)], p["w_in_p"], out_dtype=F32, tm=ts, tk=D, name=tag + "_dx")
    g_in_p = jnp.concatenate(
        [_mm_ta(x, dcq, tk=D, tn=Q_LORA, tm=ts, name=tag + "_dwin_q"),
         _mm_ta(x, dckv, tk=D, tn=KV_LORA, tm=ts, name=tag + "_dwin_kv"),
         _mm_ta(x, dkpe, tk=D, tn=HEAD_SLOT, tm=ts, name=tag + "_dwin_pe")], axis=1)
    g["w_in"], g["w_uq"], g["w_ukv"] = _mla_unpermute_grads(g_in_p, g_uq_p, g_ukv_p)
    return dx, g


RET_QK = 256
RET_V = 512


def _ret_tables(pos, T):
    half = RET_QK // 2
    inv_freq = ROPE_BASE ** (-jnp.arange(0, RET_QK, 2, dtype=F32) / RET_QK)
    ang = pos.astype(F32)[:, None] * inv_freq
    lg = jnp.log1p(-jnp.exp2(-5.0 - jnp.arange(RET_HEADS, dtype=F32)))
    idx = jnp.arange(T, dtype=F32)
    ch = jnp.arange(T) // CHUNK
    dm = jnp.where(ch[None, :] <= ch[:, None], jnp.exp(lg[:, None, None] * jnp.abs(idx[:, None] - idx[None, :])), 0.0)
    xi = jnp.broadcast_to(jnp.exp(lg[:, None] * (idx + 1.0))[:, :, None], (RET_HEADS, T, RET_QK))
    zeta = jnp.broadcast_to(jnp.exp(lg[:, None] * (T - 1.0 - idx))[:, :, None], (RET_HEADS, T, RET_QK))
    g_t = jnp.broadcast_to(jnp.exp(lg * T)[:, None, None], (RET_HEADS, 1, RET_V))
    assert half == LANES
    return jnp.cos(ang), jnp.sin(ang), dm.astype(F32), xi.astype(F32), zeta.astype(F32), g_t.astype(F32)


def _rope_half(x, c, s):
    x1, x2 = x[:, :LANES], x[:, LANES:]
    return jnp.concatenate([x1 * c - x2 * s, x1 * s + x2 * c], axis=1)


def _rope_half_t(g, c, s):
    g1, g2 = g[:, :LANES], g[:, LANES:]
    return jnp.concatenate([g1 * c + g2 * s, g2 * c - g1 * s], axis=1)


def _ret_qkv(q_ref, k_ref, v_ref, c_ref, s_ref):
    c, s = c_ref[...], s_ref[...]
    q = _rope_half(q_ref[...], c, s)
    k = _rope_half(k_ref[...], c, s) * (RET_QK ** -0.5)
    return q, k, v_ref[...].astype(BF16)


def _ret_in_specs(T, H, rev_nb=None):
    rb = (lambda n: n) if rev_nb is None else (lambda n: rev_nb - 1 - n)
    nq = H * RET_QK // RET_QK
    nv = 2 * H * RET_QK // RET_V
    return dict(
        q=pl.BlockSpec((T, RET_QK), lambda h, n: (rb(n), h)),
        k=pl.BlockSpec((T, RET_QK), lambda h, n: (rb(n), nq + h)),
        v=pl.BlockSpec((T, RET_V), lambda h, n: (rb(n), nv + h)),
        g=pl.BlockSpec((T, RET_V), lambda h, n: (rb(n), nv + H + h)),
        yv=pl.BlockSpec((T, RET_V), lambda h, n: (rb(n), h)),
        cs=pl.BlockSpec((T, LANES), lambda h, n: (rb(n), 0)),
        dm=pl.BlockSpec((None, T, T), lambda h, n: (h, 0, 0)),
        xz=pl.BlockSpec((None, T, RET_QK), lambda h, n: (h, 0, 0)),
        gt=pl.BlockSpec((None, 1, RET_V), lambda h, n: (h, 0, 0)),
        gn=pl.BlockSpec((1, RET_V), lambda h, n: (0, h)),
        st=pl.BlockSpec((None, None, RET_QK, RET_V), lambda h, n: (h, rb(n), 0, 0)),
    )


def _ret_fwd(proj, gn_g, tabs, *, T, name):
    S = proj.shape[0]
    H = RET_HEADS
    nb = S // T
    cos, sin, dm, xi, zeta, g_t = tabs
    sp = _ret_in_specs(T, H)

    def body(q_ref, k_ref, v_ref, g_ref, gn_ref, c_ref, s_ref, dm_ref, xi_ref, zeta_ref, gt_ref,
             o_ref, y_ref, st_ref, st):
        @pl.when(pl.program_id(1) == 0)
        def _():
            st[...] = jnp.zeros_like(st)

        q, k, vb = _ret_qkv(q_ref, k_ref, v_ref, c_ref, s_ref)
        qb, kb = q.astype(BF16), k.astype(BF16)
        s0 = st[...]
        s0b = s0.astype(BF16)
        st_ref[...] = s0b
        a = _dot_tb(qb, kb) * dm_ref[...]
        y = _dot(a.astype(BF16), vb) + _dot((q * xi_ref[...]).astype(BF16), s0b)
        st[...] = s0 * gt_ref[...] + _dot_ta((k * zeta_ref[...]).astype(BF16), vb)
        y_ref[...] = y
        mu = jnp.mean(y, axis=-1, keepdims=True)
        yc = y - mu
        var = jnp.mean(yc * yc, axis=-1, keepdims=True)
        gv = g_ref[...]
        o_ref[...] = (gv * _sigmoid(gv) * (yc * lax.rsqrt(var + LN_EPS) * gn_ref[...])).astype(BF16)

    return pl.pallas_call(
        body, name=name, grid=(H, nb),
        in_specs=[sp["q"], sp["k"], sp["v"], sp["g"], sp["gn"], sp["cs"], sp["cs"], sp["dm"], sp["xz"], sp["xz"], sp["gt"]],
        out_specs=[sp["yv"], sp["yv"], sp["st"]],
        out_shape=[jax.ShapeDtypeStruct((S, H * RET_V), BF16), jax.ShapeDtypeStruct((S, H * RET_V), F32),
                   jax.ShapeDtypeStruct((H, nb, RET_QK, RET_V), BF16)],
        scratch_shapes=[pltpu.VMEM((RET_QK, RET_V), F32)],
        compiler_params=_cp(("parallel", "arbitrary")),
    )(proj, proj, proj, proj, gn_g.reshape(1, H * RET_V), cos, sin, dm, xi, zeta, g_t)


def _ret_gn_bwd(dout, proj, y, gn_g, *, ts, name):
    S = proj.shape[0]
    H = RET_HEADS
    goff = 2 * H * RET_QK // RET_V + H

    def body(do_ref, g_ref, y_ref, gn_ref, dy_ref, dg_ref, dgn_ref):
        @pl.when(pl.program_id(1) == 0)
        def _():
            dgn_ref[...] = jnp.zeros_like(dgn_ref)
        y_v = y_ref[...]
        mu = jnp.mean(y_v, axis=-1, keepdims=True)
        yc = y_v - mu
        var = jnp.mean(yc * yc, axis=-1, keepdims=True)
        rstd = lax.rsqrt(var + LN_EPS)
        yhat = yc * rstd
        gv = g_ref[...]
        sg = _sigmoid(gv)
        dout = do_ref[...]
        gn = gn_ref[...]
        dg_ref[...] = (dout * (yhat * gn) * (sg * (1.0 + gv * (1.0 - sg)))).astype(BF16)
        dyn = dout * (gv * sg)
        dgn_ref[...] += _fold8(dyn * yhat)
        dyh = dyn * gn
        m1 = jnp.mean(dyh, axis=-1, keepdims=True)
        m2 = jnp.mean(dyh * yhat, axis=-1, keepdims=True)
        dy_ref[...] = (rstd * (dyh - m1 - yhat * m2)).astype(BF16)

    blk = pl.BlockSpec((ts, RET_V), lambda h, i: (i, h))
    return pl.pallas_call(
        body, name=name, grid=(H, S // ts),
        in_specs=[blk, pl.BlockSpec((ts, RET_V), lambda h, i: (i, goff + h)), blk,
                  pl.BlockSpec((1, RET_V), lambda h, i: (0, h))],
        out_specs=[blk, blk, pl.BlockSpec((SUBLANES, RET_V), lambda h, i: (0, h))],
        out_shape=[jax.ShapeDtypeStruct((S, H * RET_V), BF16), jax.ShapeDtypeStruct((S, H * RET_V), BF16),
                   jax.ShapeDtypeStruct((SUBLANES, H * RET_V), F32)],
        compiler_params=_cp(("parallel", "arbitrary")),
    )(dout, proj, y, gn_g.reshape(1, H * RET_V))


def _ret_bwd(proj, dy, states, tabs, *, T, name):
    S = proj.shape[0]
    H = RET_HEADS
    nb = S // T
    cos, sin, dm, xi, zeta, g_t = tabs
    sp = _ret_in_specs(T, H, rev_nb=nb)

    def body(q_ref, k_ref, v_ref, dy_ref, st_ref, c_ref, s_ref, dm_ref, xi_ref, zeta_ref, gt_ref,
             dq_ref, dk_ref, dv_ref, ds):
        @pl.when(pl.program_id(1) == 0)
        def _():
            ds[...] = jnp.zeros_like(ds)

        q, k, vb = _ret_qkv(q_ref, k_ref, v_ref, c_ref, s_ref)
        qb, kb = q.astype(BF16), k.astype(BF16)
        dyb = dy_ref[...]
        s0b = st_ref[...]
        dmv, xiv, zv = dm_ref[...], xi_ref[...], zeta_ref[...]
        ds_v = ds[...]
        dsb = ds_v.astype(BF16)
        gm = (_dot_tb(dyb, vb) * dmv).astype(BF16)
        ab = (_dot_tb(qb, kb) * dmv).astype(BF16)
        kz = (k * zv).astype(BF16)
        qx = (q * xiv).astype(BF16)
        dq = _dot(gm, kb) + xiv * _dot_tb(dyb, s0b)
        dk = _dot_ta(gm, qb) + zv * _dot_tb(vb, dsb)
        dv_ref[...] = (_dot_ta(ab, dyb) + _dot(kz, dsb)).astype(BF16)
        ds[...] = ds_v * gt_ref[...] + _dot_ta(qx, dyb)
        c, s = c_ref[...], s_ref[...]
        dq_ref[...] = _rope_half_t(dq, c, s).astype(BF16)
        dk_ref[...] = _rope_half_t(dk * (RET_QK ** -0.5), c, s).astype(BF16)

    qblk = pl.BlockSpec((T, RET_QK), lambda h, n: (nb - 1 - n, h))
    return pl.pallas_call(
        body, name=name, grid=(H, nb),
        in_specs=[sp["q"], sp["k"], sp["v"], sp["yv"], sp["st"], sp["cs"], sp["cs"], sp["dm"], sp["xz"], sp["xz"], sp["gt"]],
        out_specs=[qblk, qblk, sp["yv"]],
        out_shape=[jax.ShapeDtypeStruct((S, H * RET_QK), BF16), jax.ShapeDtypeStruct((S, H * RET_QK), BF16),
                   jax.ShapeDtypeStruct((S, H * RET_V), BF16)],
        scratch_shapes=[pltpu.VMEM((RET_QK, RET_V), F32)],
        compiler_params=_cp(("parallel", "arbitrary")),
    )(proj, proj, proj, dy, states, cos, sin, dm, xi, zeta, g_t)


def _ret_mixer_fwd(x, pos, p, tag):
    S, D = x.shape
    ts = _rt(S, 512)
    T = _rt(S, 256)
    tabs = _ret_tables(pos, T)
    proj = _mm(x, p["w_in"], out_dtype=F32, tm=ts, tn=1024, name=tag + "_proj")
    gated, y, states = _ret_fwd(proj, p["gn_g"], tabs, T=T, name=tag + "_ret")
    mix = _mm(gated, p["w_out"], out_dtype=F32, tm=ts, tn=D, name=tag + "_out")
    return mix, (proj, gated, y, states, tabs)


def _ret_mixer_bwd(dmix, x, p, saved, tag, gbuf, j):
    proj, gated, y, states, tabs = saved
    S, D = x.shape
    ts = _rt(S, 512)
    T = _rt(S, 256)
    H = RET_HEADS
    hq, hv = H * RET_QK, H * RET_V
    g = {}
    dout = _mm_tb([(dmix, 0)], p["w_out"], out_dtype=F32, tm=ts, tk=1024, name=tag + "_dgated")
    n_ret = DEPTH // 3
    _grad_into(gbuf, "ret_w_out", (n_ret, hv, D), j, 0, 0, gated, dmix, tk=1024, tn=D, tm=ts, name=tag + "_dwout")
    dy, dgate, dgn8 = _ret_gn_bwd(dout, proj, y, p["gn_g"], ts=_rt(S, 256), name=tag + "_gnb")
    g["gn_g"] = dgn8.sum(axis=0)
    dq, dk, dv = _ret_bwd(proj, dy, states, tabs, T=T, name=tag + "_retb")
    dx = _mm_tb([(dq, 0), (dk, hq), (dv, 2 * hq), (dgate, 2 * hq + hv)], p["w_in"], out_dtype=F32,
                tm=ts, tk=512, name=tag + "_dx")
    w_in_shape = (n_ret, D, 2 * hq + 2 * hv)
    for part, c0, nm in ((dq, 0, "q"), (dk, hq, "k"), (dv, 2 * hq, "v"), (dgate, 2 * hq + hv, "g")):
        _grad_into(gbuf, "ret_w_in", w_in_shape, j, 0, c0, x, part, tk=D, tn=1024, tm=ts, name=tag + "_dwin_" + nm)
    return dx, g


PACK_W = 1024
ANY = pl.BlockSpec(memory_space=pl.ANY)
MESH = pl.DeviceIdType.MESH


def _coords():
    return lax.axis_index("x"), lax.axis_index("y"), lax.axis_index("c")


def _chip_peers(x, y):
    return [(1 - x, y), (x, 1 - y), (1 - x, 1 - y)]


def _slot(ref, axis, s, n):
    if axis is None:
        return ref.at[s]
    size = n // N_CHIPS
    sl = pl.ds(pl.multiple_of(s * size, LANES if axis == 2 else 2 * SUBLANES), size)
    return ref.at[:, sl, :] if axis == 1 else ref.at[:, :, sl]


def _gather_chips(items, name):
    n = len(items)
    axes = [ax for _, ax in items]
    out_shapes = []
    for arr, ax in items:
        shp = (N_CHIPS,) + arr.shape if ax is None else tuple(d * (N_CHIPS if i == ax else 1) for i, d in enumerate(arr.shape))
        out_shapes.append(jax.ShapeDtypeStruct(shp, arr.dtype))

    def body(*refs):
        srcs, outs = refs[:n], refs[n:2 * n]
        send_sems, recv_sems, local_sems = refs[2 * n:]
        x, y, c = _coords()
        me = 2 * x + y
        dst = lambda t, s: _slot(outs[t], axes[t], s, out_shapes[t].shape[axes[t]] if axes[t] is not None else 0)
        local = [pltpu.make_async_copy(srcs[t], dst(t, me), local_sems.at[t]) for t in range(n)]
        for cp in local:
            cp.start()
        sends, recvs = [], []
        for k, (px, py) in enumerate(_chip_peers(x, y)):
            for t in range(n):
                sem = k * n + t
                sends.append(pltpu.make_async_remote_copy(
                    src_ref=srcs[t], dst_ref=dst(t, me), send_sem=send_sems.at[sem], recv_sem=recv_sems.at[sem],
                    device_id=(px, py, c), device_id_type=MESH))
                recvs.append(pltpu.make_async_remote_copy(
                    src_ref=srcs[t], dst_ref=dst(t, 2 * px + py), send_sem=send_sems.at[sem],
                    recv_sem=recv_sems.at[sem], device_id=(px, py, c), device_id_type=MESH))
        for cp in sends:
            cp.start()
        for cp in recvs:
            cp.wait_recv()
        for cp in sends:
            cp.wait_send()
        for cp in local:
            cp.wait()

    return pl.pallas_call(
        body, name=name, in_specs=[ANY] * n, out_specs=[ANY] * n, out_shape=out_shapes,
        scratch_shapes=[pltpu.SemaphoreType.DMA((3 * n,)), pltpu.SemaphoreType.DMA((3 * n,)),
                        pltpu.SemaphoreType.DMA((n,))],
    )(*[arr for arr, _ in items])


def _scatter_chips(items, name):
    n = len(items)
    axes = [ax for _, ax in items]
    out_shapes = []
    for arr, ax in items:
        part = arr.shape[1:] if ax is None else tuple(d // (N_CHIPS if i == ax else 1) for i, d in enumerate(arr.shape))
        out_shapes.append(jax.ShapeDtypeStruct((3,) + part, arr.dtype))

    def body(*refs):
        srcs, outs = refs[:n], refs[n:2 * n]
        send_sems, recv_sems = refs[2 * n:]
        x, y, c = _coords()
        copies = []
        for k, (px, py) in enumerate(_chip_peers(x, y)):
            for t in range(n):
                src = _slot(srcs[t], axes[t], 2 * px + py, srcs[t].shape[axes[t]] if axes[t] is not None else 0)
                copies.append(pltpu.make_async_remote_copy(
                    src_ref=src, dst_ref=outs[t].at[k], send_sem=send_sems.at[k * n + t],
                    recv_sem=recv_sems.at[k * n + t], device_id=(px, py, c), device_id_type=MESH))
        for cp in copies:
            cp.start()
        for cp in copies:
            cp.wait_recv()
        for cp in copies:
            cp.wait_send()

    return pl.pallas_call(
        body, name=name, in_specs=[ANY] * n, out_specs=[ANY] * n, out_shape=out_shapes,
        scratch_shapes=[pltpu.SemaphoreType.DMA((3 * n,)), pltpu.SemaphoreType.DMA((3 * n,))],
    )(*[arr for arr, _ in items])


def _swap_sibling(arrs, name):
    n = len(arrs)

    def body(*refs):
        srcs, outs = refs[:n], refs[n:2 * n]
        send_sems, recv_sems = refs[2 * n:]
        x, y, c = _coords()
        copies = [pltpu.make_async_remote_copy(src_ref=srcs[t], dst_ref=outs[t], send_sem=send_sems.at[t],
                                               recv_sem=recv_sems.at[t], device_id=(x, y, 1 - c), device_id_type=MESH)
                  for t in range(n)]
        for cp in copies:
            cp.start()
        for cp in copies:
            cp.wait_recv()
        for cp in copies:
            cp.wait_send()

    return pl.pallas_call(
        body, name=name, in_specs=[ANY] * n, out_specs=[ANY] * n,
        out_shape=[jax.ShapeDtypeStruct(a_.shape, a_.dtype) for a_ in arrs],
        scratch_shapes=[pltpu.SemaphoreType.DMA((n,)), pltpu.SemaphoreType.DMA((n,))],
    )(*arrs)


def _allreduce_small(v, name):
    R, Wd = v.shape

    def body(v_ref, o_ref, buf, send_sems, recv_sems):
        x, y, c = _coords()
        o_ref[...] = v_ref[...]
        for st, peer in enumerate([(x, y, 1 - c), (x, 1 - y, c), (1 - x, y, c)]):
            cp = pltpu.make_async_remote_copy(src_ref=o_ref, dst_ref=buf.at[st], send_sem=send_sems.at[st],
                                              recv_sem=recv_sems.at[st], device_id=peer, device_id_type=MESH)
            cp.start()
            cp.wait_recv()
            cp.wait_send()
            o_ref[...] = o_ref[...] + buf[st]

    vm = pl.BlockSpec(memory_space=pltpu.VMEM)
    return pl.pallas_call(
        body, name=name, in_specs=[vm], out_specs=vm,
        out_shape=jax.ShapeDtypeStruct((R, Wd), F32),
        scratch_shapes=[pltpu.VMEM((3, R, Wd), F32), pltpu.SemaphoreType.DMA((3,)), pltpu.SemaphoreType.DMA((3,))],
    )(v)


def _row_tile(rows):
    t = rows
    while t > 256:
        assert t % 2 == 0
        t //= 2
    assert t % SUBLANES == 0
    return t


def _sum_partials(g, recv, axis, *, name):
    _, L, R, C = recv.shape
    tr = _row_tile(R)
    me = (2 * lax.axis_index("x") + lax.axis_index("y")).astype(jnp.int32).reshape(1)

    def body(me_ref, g_ref, r_ref, o_ref):
        o_ref[...] = ((g_ref[...] + r_ref[0].astype(F32)) + r_ref[1].astype(F32)) + r_ref[2].astype(F32)

    if axis is None:
        g_spec = pl.BlockSpec((None, None, tr, C), lambda l, i, me: (me[0], l, i, 0))
    elif axis == 1:
        g_spec = pl.BlockSpec((None, tr, C), lambda l, i, me: (l, me[0] * (R // tr) + i, 0))
    else:
        g_spec = pl.BlockSpec((None, tr, C), lambda l, i, me: (l, i, me[0]))
    grid_spec = pltpu.PrefetchScalarGridSpec(
        num_scalar_prefetch=1, grid=(L, R // tr),
        in_specs=[g_spec, pl.BlockSpec((3, None, tr, C), lambda l, i, me: (0, l, i, 0))],
        out_specs=pl.BlockSpec((None, tr, C), lambda l, i, me: (l, i, 0)))
    return pl.pallas_call(
        body, name=name, grid_spec=grid_spec, out_shape=jax.ShapeDtypeStruct((L, R, C), F32),
        compiler_params=_cp(("parallel", "parallel")),
    )(me, g, recv)


def _adamw(w, m, v, ga, gb, *, name):
    L, R, C = w.shape
    tr = _row_tile(R)
    two = gb is not None
    c1 = 1.0 / (1.0 - ADAM_B1 ** ADAM_STEP)
    c2 = 1.0 / (1.0 - ADAM_B2 ** ADAM_STEP)

    def body(*refs):
        if two:
            w_ref, m_ref, v_ref, ga_ref, gb_ref, g_ref, d_ref, mo_ref, vo_ref = refs
            g = ga_ref[...] + gb_ref[...]
        else:
            w_ref, m_ref, v_ref, ga_ref, g_ref, d_ref, mo_ref, vo_ref = refs
            g = ga_ref[...]
        m2 = ADAM_B1 * m_ref[...] + (1.0 - ADAM_B1) * g
        v2 = ADAM_B2 * v_ref[...] + (1.0 - ADAM_B2) * (g * g)
        g_ref[...] = g
        mo_ref[...] = m2
        vo_ref[...] = v2
        d_ref[...] = -ADAM_LR * ((m2 * c1) / (jnp.sqrt(v2 * c2) + ADAM_EPS) + ADAM_WD * w_ref[...])

    blk = pl.BlockSpec((None, tr, C), lambda l, i: (l, i, 0))
    args = [w, m, v, ga] + ([gb] if two else [])
    return pl.pallas_call(
        body, name=name, grid=(L, R // tr), in_specs=[blk] * len(args), out_specs=[blk] * 4,
        out_shape=[jax.ShapeDtypeStruct((L, R, C), F32)] * 4, compiler_params=_cp(("parallel", "parallel")),
    )(*args)


SHARDED = [
    ("ffn_w_up", 2, True), ("ffn_conv_w", 2, False), ("ffn_w_down", 1, True),
    ("lru_w_in", 2, True), ("lru_conv_w", 2, False), ("lru_conv_b", 1, False),
    ("lru_w_a", 2, True), ("lru_b_a", 2, False), ("lru_w_x", 2, True), ("lru_b_x", 2, False),
    ("lru_lambda", 1, False), ("lru_w_out", 1, True),
    ("mla_w_in", 2, True), ("mla_w_uq", 2, True), ("mla_w_ukv", 2, True), ("mla_w_out", 1, True),
    ("ret_w_in", 2, True), ("ret_gn_g", 1, False), ("ret_w_out", 1, True),
]
BIG_AXIS = {"ffn_w_up": 2, "ffn_w_down": 1, "lru_w_in": 2, "lru_w_out": 1, "ret_w_in": 2, "ret_w_out": 1}
REPLICATED = ["ln1_g", "ln1_b", "ln2_g", "ln2_b", "ffn_conv_b", "mla_q_norm", "mla_kv_norm"]
WEIGHTS = ["ln1_g", "ln1_b", "ln2_g", "ln2_b", "ffn_w_up", "ffn_conv_w", "ffn_conv_b", "ffn_w_down", "lru_w_in",
           "lru_conv_w", "lru_conv_b", "lru_w_a", "lru_b_a", "lru_w_x", "lru_b_x", "lru_lambda", "lru_w_out",
           "mla_w_in", "mla_q_norm", "mla_kv_norm", "mla_w_uq", "mla_w_ukv", "mla_w_out", "ret_w_in", "ret_gn_g",
           "ret_w_out"]
PACK_ROWS = 512


def _pack(arrs, dtype, lead=(), rows=PACK_ROWS):
    nl = len(lead)
    flat = jnp.concatenate([a.astype(dtype).reshape(lead + (-1,)) for a in arrs], axis=nl)
    n = flat.shape[nl]
    quantum = rows * PACK_W
    total = -(-n // quantum) * quantum
    flat = jnp.pad(flat, [(0, 0)] * nl + [(0, total - n)])
    return flat.reshape(lead + (total // PACK_W, PACK_W))


def _unpack(buf, shapes, lead=()):
    nl = len(lead)
    flat = buf.reshape(lead + (-1,))
    out, off = [], 0
    for shp in shapes:
        n = int(np.prod(shp))
        out.append(lax.slice_in_dim(flat, off, off + n, axis=nl).reshape(lead + tuple(shp)))
        off += n
    return out


def _layer_params(full, rep, i):
    kind, j = i % 3, i // 3
    ffn = dict(w_up=full["ffn_w_up"][i], conv_w=full["ffn_conv_w"][i], conv_b=rep["ffn_conv_b"][i],
               w_down=full["ffn_w_down"][i])
    if kind == 0:
        mix = dict(w_in=full["lru_w_in"][j], conv_w=full["lru_conv_w"][j], conv_b=full["lru_conv_b"][j],
                   w_a=full["lru_w_a"][j], b_a=full["lru_b_a"][j], w_x=full["lru_w_x"][j], b_x=full["lru_b_x"][j],
                   lam=full["lru_lambda"][j], w_out=full["lru_w_out"][j])
    elif kind == 1:
        w_in_p, w_uq_p, w_ukv_p = _mla_permute_weights(full["mla_w_in"][j], full["mla_w_uq"][j], full["mla_w_ukv"][j])
        mix = dict(w_in_p=w_in_p, w_uq_p=w_uq_p, w_ukv_p=w_ukv_p, q_norm=rep["mla_q_norm"][j],
                   kv_norm=rep["mla_kv_norm"][j], w_out=full["mla_w_out"][j])
    else:
        mix = dict(w_in=full["ret_w_in"][j], gn_g=full["ret_gn_g"][j], w_out=full["ret_w_out"][j])
    return kind, mix, ffn


_MIX_FWD = {0: lambda x, pos, p, tag: _lru_mixer_fwd(x, p, tag), 1: _mla_mixer_fwd, 2: _ret_mixer_fwd}
_MIX_BWD = {0: _lru_mixer_bwd, 1: _mla_mixer_bwd, 2: _ret_mixer_bwd}
_MIX_PREFIX = {0: "lru_", 1: "mla_", 2: "ret_"}
_MIX_KEYS = {0: {"w_in": "lru_w_in", "conv_w": "lru_conv_w", "conv_b": "lru_conv_b", "w_a": "lru_w_a", "b_a": "lru_b_a",
                 "w_x": "lru_w_x", "b_x": "lru_b_x", "lam": "lru_lambda", "w_out": "lru_w_out"},
             1: {"w_in": "mla_w_in", "q_norm": "mla_q_norm", "kv_norm": "mla_kv_norm", "w_uq": "mla_w_uq",
                 "w_ukv": "mla_w_ukv", "w_out": "mla_w_out"},
             2: {"w_in": "ret_w_in", "gn_g": "ret_gn_g", "w_out": "ret_w_out"}}
_FFN_KEYS = {"w_up": "ffn_w_up", "conv_w": "ffn_conv_w", "conv_b": "ffn_conv_b", "w_down": "ffn_w_down"}


def _local_step(x, pos, target, full, rep):
    S, D = x.shape
    ts = _rt(S, 256)
    acts = []
    h = x
    for i in range(DEPTH):
        kind, mp, fp = _layer_params(full, rep, i)
        tag = "l%d" % i
        mix, msaved = _MIX_FWD[kind](h, pos, mp, tag + "m")
        h1, z1 = _ln_fwd(h, mix, rep["ln1_g"][i], rep["ln1_b"][i], ts=ts, name=tag + "_ln1")
        f, fsaved = _ffn_fwd(h1, fp, tag + "f")
        h2, z2 = _ln_fwd(h1, f, rep["ln2_g"][i], rep["ln2_b"][i], ts=ts, name=tag + "_ln2")
        acts.append((kind, mp, fp, h, msaved, h1, z1, fsaved, z2))
        h = h2
    dy, part = _loss_head(h, target, ts=ts, name="loss_head")

    grads = {n: {} for n in WEIGHTS if n not in BIG_AXIS}
    gbuf = {}
    d_a, d_b = dy, None
    for i in reversed(range(DEPTH)):
        kind, mp, fp, h_in, msaved, h1, z1, fsaved, z2 = acts[i]
        tag = "l%d" % i
        dz2, dg8, db8 = _ln_bwd(d_a, d_b, z2, rep["ln2_g"][i], ts=ts, name=tag + "_ln2b")
        grads["ln2_g"][i], grads["ln2_b"][i] = dg8.sum(axis=0), db8.sum(axis=0)
        dx_f, gf = _ffn_bwd(dz2, h1, fp, fsaved, tag + "f", gbuf, i)
        for k, v in gf.items():
            grads[_FFN_KEYS[k]][i] = v
        dz1, dg8, db8 = _ln_bwd(dz2, dx_f, z1, rep["ln1_g"][i], ts=ts, name=tag + "_ln1b")
        grads["ln1_g"][i], grads["ln1_b"][i] = dg8.sum(axis=0), db8.sum(axis=0)
        dx_m, gm = _MIX_BWD[kind](dz1, h_in, mp, msaved, tag + "m", gbuf, i // 3)
        for k, v in gm.items():
            grads[_MIX_KEYS[kind][k]][i // 3] = v
        d_a, d_b = dz1, dx_m
    grad_x = _axpy(d_a, d_b, ts=ts, name="grad_x")
    stacked = {n: jnp.stack([grads[n][j] for j in sorted(grads[n])]) for n in grads}
    return part, grad_x, stacked, gbuf


def kernel(x, positions, ln1_g, ln1_b, ln2_g, ln2_b, ffn_w_up, ffn_conv_w, ffn_conv_b, ffn_w_down, lru_w_in, lru_conv_w, lru_conv_b, lru_w_a, lru_b_a, lru_w_x, lru_b_x, lru_lambda, lru_w_out, mla_w_in, mla_q_norm, mla_kv_norm, mla_w_uq, mla_w_ukv, mla_w_out, ret_w_in, ret_gn_g, ret_w_out, loss_target, m_ln1_g, m_ln1_b, m_ln2_g, m_ln2_b, m_ffn_w_up, m_ffn_conv_w, m_ffn_conv_b, m_ffn_w_down, m_lru_w_in, m_lru_conv_w, m_lru_conv_b, m_lru_w_a, m_lru_b_a, m_lru_w_x, m_lru_b_x, m_lru_lambda, m_lru_w_out, m_mla_w_in, m_mla_q_norm, m_mla_kv_norm, m_mla_w_uq, m_mla_w_ukv, m_mla_w_out, m_ret_w_in, m_ret_gn_g, m_ret_w_out, v_ln1_g, v_ln1_b, v_ln2_g, v_ln2_b, v_ffn_w_up, v_ffn_conv_w, v_ffn_conv_b, v_ffn_w_down, v_lru_w_in, v_lru_conv_w, v_lru_conv_b, v_lru_w_a, v_lru_b_a, v_lru_w_x, v_lru_b_x, v_lru_lambda, v_lru_w_out, v_mla_w_in, v_mla_q_norm, v_mla_kv_norm, v_mla_w_uq, v_mla_w_ukv, v_mla_w_out, v_ret_w_in, v_ret_gn_g, v_ret_w_out):
    w = dict(ln1_g=ln1_g, ln1_b=ln1_b, ln2_g=ln2_g, ln2_b=ln2_b, ffn_w_up=ffn_w_up, ffn_conv_w=ffn_conv_w, ffn_conv_b=ffn_conv_b, ffn_w_down=ffn_w_down, lru_w_in=lru_w_in, lru_conv_w=lru_conv_w, lru_conv_b=lru_conv_b, lru_w_a=lru_w_a, lru_b_a=lru_b_a, lru_w_x=lru_w_x, lru_b_x=lru_b_x, lru_lambda=lru_lambda, lru_w_out=lru_w_out, mla_w_in=mla_w_in, mla_q_norm=mla_q_norm, mla_kv_norm=mla_kv_norm, mla_w_uq=mla_w_uq, mla_w_ukv=mla_w_ukv, mla_w_out=mla_w_out, ret_w_in=ret_w_in, ret_gn_g=ret_gn_g, ret_w_out=ret_w_out)
    m = dict(ln1_g=m_ln1_g, ln1_b=m_ln1_b, ln2_g=m_ln2_g, ln2_b=m_ln2_b, ffn_w_up=m_ffn_w_up, ffn_conv_w=m_ffn_conv_w, ffn_conv_b=m_ffn_conv_b, ffn_w_down=m_ffn_w_down, lru_w_in=m_lru_w_in, lru_conv_w=m_lru_conv_w, lru_conv_b=m_lru_conv_b, lru_w_a=m_lru_w_a, lru_b_a=m_lru_b_a, lru_w_x=m_lru_w_x, lru_b_x=m_lru_b_x, lru_lambda=m_lru_lambda, lru_w_out=m_lru_w_out, mla_w_in=m_mla_w_in, mla_q_norm=m_mla_q_norm, mla_kv_norm=m_mla_kv_norm, mla_w_uq=m_mla_w_uq, mla_w_ukv=m_mla_w_ukv, mla_w_out=m_mla_w_out, ret_w_in=m_ret_w_in, ret_gn_g=m_ret_gn_g, ret_w_out=m_ret_w_out)
    v = dict(ln1_g=v_ln1_g, ln1_b=v_ln1_b, ln2_g=v_ln2_g, ln2_b=v_ln2_b, ffn_w_up=v_ffn_w_up, ffn_conv_w=v_ffn_conv_w, ffn_conv_b=v_ffn_conv_b, ffn_w_down=v_ffn_w_down, lru_w_in=v_lru_w_in, lru_conv_w=v_lru_conv_w, lru_conv_b=v_lru_conv_b, lru_w_a=v_lru_w_a, lru_b_a=v_lru_b_a, lru_w_x=v_lru_w_x, lru_b_x=v_lru_b_x, lru_lambda=v_lru_lambda, lru_w_out=v_lru_w_out, mla_w_in=v_mla_w_in, mla_q_norm=v_mla_q_norm, mla_kv_norm=v_mla_kv_norm, mla_w_uq=v_mla_w_uq, mla_w_ukv=v_mla_w_ukv, mla_w_out=v_mla_w_out, ret_w_in=v_ret_w_in, ret_gn_g=v_ret_gn_g, ret_w_out=v_ret_w_out)
    D = x.shape[-1]
    axis_of = {n: ax for n, ax, _ in SHARDED}
    big = list(BIG_AXIS)
    small_mx = [n for n, _, mx in SHARDED if mx and n not in BIG_AXIS]
    small_vec = [n for n, _, mx in SHARDED if not mx]
    small = small_mx + small_vec

    gathered = _gather_chips([(w[n].astype(BF16), BIG_AXIS[n]) for n in big]
                             + [(_pack([w[n] for n in small_mx], BF16), None), (_pack([w[n] for n in small_vec], F32), None)],
                             "gather_weights")
    full = dict(zip(big, gathered))
    for names, buf in ((small_mx, gathered[-2]), (small_vec, gathered[-1])):
        blocks = _unpack(buf, [w[n].shape for n in names], lead=(N_CHIPS,))
        for n, blk in zip(names, blocks):
            full[n] = jnp.concatenate([blk[s] for s in range(N_CHIPS)], axis=axis_of[n])
    rep = {n: w[n] for n in REPLICATED}

    part, grad_x, grads, gbuf = _local_step(x[0], positions[0], loss_target[0], full, rep)
    loss = lax.psum((0.5 / D) * jnp.sum(part), MESH_AXES)

    g_pack = _pack([jnp.stack(jnp.split(grads[n], N_CHIPS, axis=axis_of[n])) for n in small], F32, lead=(N_CHIPS,))
    recv = _scatter_chips([(gbuf[n][1], BIG_AXIS[n]) for n in big] + [(g_pack, None)], "scatter_grads")
    sums = [_sum_partials(gbuf[n][0], r, BIG_AXIS[n], name="sum_" + n) for n, r in zip(big, recv)]
    sums.append(_sum_partials(g_pack[:, None], recv[-1][:, None], None, name="sum_small"))
    sibs = _swap_sibling(sums, "swap_core_partials")
    res = {kind: {} for kind in "gdmv"}
    for n, p_mine, p_sib in zip(big, sums, sibs):
        for kind, o in zip("gdmv", _adamw(w[n], m[n], v[n], p_mine, p_sib, name="adamw_" + n)):
            res[kind][n] = o
    spack = lambda d: _pack([d[n] for n in small], F32)[None]
    shapes = [w[n].shape for n in small]
    for kind, o in zip("gdmv", _adamw(spack(w), spack(m), spack(v), sums[-1], sibs[-1], name="adamw_small")):
        res[kind].update(zip(small, _unpack(o[0], shapes)))

    r_shapes = [w[n].shape for n in REPLICATED]
    rpack = lambda d: _pack([d[n] for n in REPLICATED], F32, rows=SUBLANES)
    r_sum = _allreduce_small(rpack(grads), "allreduce_replicated")
    r_outs = _adamw(rpack(w)[None], rpack(m)[None], rpack(v)[None], r_sum[None], None, name="adamw_replicated")
    for kind, o in zip("gdmv", r_outs):
        res[kind].update(zip(REPLICATED, _unpack(o[0], r_shapes)))

    return (loss, grad_x[None], *[res["g"][n] for n in WEIGHTS], *[res["d"][n] for n in WEIGHTS],
            *[res["m"][n] for n in WEIGHTS], *[res["v"][n] for n in WEIGHTS])
```

```python
import functools
import math

import numpy as np
import jax
import jax.numpy as jnp
from jax import lax
from jax.experimental import pallas as pl
from jax.experimental.pallas import tpu as pltpu

F32 = jnp.float32
BF16 = jnp.bfloat16

DEPTH = 4
ALPHA = (2.0 * DEPTH) ** 0.25
LN_EPS = 1e-5
RMS_EPS = 1e-6
ROPE_BASE = 10000.0
CHUNK = 64
LRU_C = 8.0
LRU_GROUPS = 4
MLA_HEADS = 16
QK_NOPE, QK_ROPE, V_HEAD = 64, 32, 64
Q_LORA, KV_LORA = 768, 256
RET_HEADS = 4
ADAM_LR, ADAM_B1, ADAM_B2, ADAM_EPS, ADAM_WD, ADAM_STEP = 0.001, 0.9, 0.999, 1e-08, 0.01, 10

LANES = 128
SUBLANES = 8
VMEM_LIMIT = 56 * 1024 * 1024

MESH_AXES = ("x", "y", "c")
N_CHIPS = 4


def _cp(sem):
    return pltpu.CompilerParams(dimension_semantics=sem, vmem_limit_bytes=VMEM_LIMIT)


def _sigmoid(x):
    return 1.0 / (1.0 + jnp.exp(-x))


_GELU_C = math.sqrt(2.0 / math.pi)


def _gelu_parts(x):
    x2 = x * x
    u = _GELU_C * (x + 0.044715 * x * x2)
    t = jnp.tanh(u)
    g = 0.5 * x * (1.0 + t)
    dg = 0.5 * (1.0 + t) + 0.5 * x * (1.0 - t * t) * _GELU_C * (1.0 + 3.0 * 0.044715 * x2)
    return g, dg


def _fold8(v):
    n = v.shape[0] // SUBLANES
    return v.reshape(n, SUBLANES, v.shape[1]).sum(axis=0)


def _dot(a, b):
    return jnp.dot(a, b, preferred_element_type=F32)


def _dot_tb(a, b):
    return lax.dot_general(a, b, (((1,), (1,)), ((), ())), preferred_element_type=F32)


def _dot_ta(a, b):
    return lax.dot_general(a, b, (((0,), (0,)), ((), ())), preferred_element_type=F32)


def _mm(a, b, *, out_dtype, tm, tn, name, a_koff=0):
    M = a.shape[0]
    K, N = b.shape

    def body(a_ref, b_ref, o_ref):
        o_ref[...] = _dot(a_ref[...].astype(BF16), b_ref[...].astype(BF16)).astype(out_dtype)

    return pl.pallas_call(
        body, name=name, grid=(M // tm, N // tn),
        in_specs=[pl.BlockSpec((tm, K), lambda i, j: (i, a_koff)),
                  pl.BlockSpec((K, tn), lambda i, j: (0, j))],
        out_specs=pl.BlockSpec((tm, tn), lambda i, j: (i, j)),
        out_shape=jax.ShapeDtypeStruct((M, N), out_dtype),
        compiler_params=_cp(("parallel", "parallel")),
    )(a, b)


def _mm_tb(pairs, b, *, out_dtype, tm, tk, name):
    M = pairs[0][0].shape[0]
    Kout = b.shape[0]
    n = len(pairs)

    def body(*refs):
        a_refs, b_refs, o_ref = refs[:n], refs[n:2 * n], refs[2 * n]
        acc = None
        for a_ref, b_ref in zip(a_refs, b_refs):
            t = _dot_tb(a_ref[...].astype(BF16), b_ref[...].astype(BF16))
            acc = t if acc is None else acc + t
        o_ref[...] = acc.astype(out_dtype)

    in_specs = [pl.BlockSpec((tm, a.shape[1]), lambda i, j: (i, 0)) for a, _ in pairs]
    for a, c0 in pairs:
        w = a.shape[1]
        assert c0 % w == 0
        in_specs.append(pl.BlockSpec((tk, w), functools.partial(lambda i, j, cb: (j, cb), cb=c0 // w)))
    return pl.pallas_call(
        body, name=name, grid=(M // tm, Kout // tk),
        in_specs=in_specs,
        out_specs=pl.BlockSpec((tm, tk), lambda i, j: (i, j)),
        out_shape=jax.ShapeDtypeStruct((M, Kout), out_dtype),
        compiler_params=_cp(("parallel", "parallel")),
    )(*[a for a, _ in pairs], *[b for _ in pairs])


def _mm_ta(a, b, *, tk, tn, tm, name, a_c0=0, a_w=None, b_c0=0, b_w=None, dest=None):
    M = a.shape[0]
    nm = M // tm
    a_w = a.shape[1] if a_w is None else a_w
    b_w = b.shape[1] if b_w is None else b_w
    assert a_c0 % tk == 0 and b_c0 % tn == 0 and a_w % tk == 0 and b_w % tn == 0

    def body(*refs):
        a_ref, b_ref = refs[0], refs[1]
        o_ref = refs[-1] if dest is None else refs[-2]

        @pl.when(pl.program_id(2) == 0)
        def _():
            o_ref[...] = jnp.zeros_like(o_ref)
        o_ref[...] += _dot_ta(a_ref[...].astype(BF16), b_ref[...].astype(BF16))

        if dest is not None:
            @pl.when(pl.program_id(2) == nm - 1)
            def _():
                refs[-1][...] = o_ref[...].astype(BF16)

    in_specs = [pl.BlockSpec((tm, tk), lambda i, j, m: (m, i + a_c0 // tk)),
                pl.BlockSpec((tm, tn), lambda i, j, m: (m, j + b_c0 // tn))]
    args = [a, b]
    if dest is None:
        out_spec = pl.BlockSpec((tk, tn), lambda i, j, m: (i, j))
        out_shape = jax.ShapeDtypeStruct((a_w, b_w), F32)
        aliases = {}
    else:
        bufs, full_shape, layer, r0, c0 = dest
        assert r0 % tk == 0 and c0 % tn == 0
        spec = pl.BlockSpec((None, tk, tn), lambda i, j, m: (layer, i + r0 // tk, j + c0 // tn))
        out_spec = [spec, spec]
        out_shape = [jax.ShapeDtypeStruct(full_shape, F32), jax.ShapeDtypeStruct(full_shape, BF16)]
        aliases = {}
        if bufs is not None:
            in_specs += [pl.BlockSpec(memory_space=pl.ANY)] * 2
            args += list(bufs)
            aliases = {2: 0, 3: 1}
    return pl.pallas_call(
        body, name=name, grid=(a_w // tk, b_w // tn, nm),
        in_specs=in_specs, out_specs=out_spec, out_shape=out_shape, input_output_aliases=aliases,
        compiler_params=_cp(("parallel", "parallel", "arbitrary")),
    )(*args)


def _grad_into(gbuf, key, full_shape, layer, r0, c0, a, b, **kw):
    gbuf[key] = tuple(_mm_ta(a, b, dest=(gbuf.get(key), full_shape, layer, r0, c0), **kw))


def _ln_fwd(x, mix, g, b, *, ts, name):
    S, D = x.shape

    def body(x_ref, m_ref, g_ref, b_ref, o_ref, z_ref):
        z = ALPHA * x_ref[...] + m_ref[...]
        mu = jnp.mean(z, axis=-1, keepdims=True)
        zc = z - mu
        var = jnp.mean(zc * zc, axis=-1, keepdims=True)
        o_ref[...] = zc * lax.rsqrt(var + LN_EPS) * g_ref[...] + b_ref[...]
        z_ref[...] = z

    row = pl.BlockSpec((ts, D), lambda i: (i, 0))
    vec = pl.BlockSpec((1, D), lambda i: (0, 0))
    return pl.pallas_call(
        body, name=name, grid=(S // ts,),
        in_specs=[row, row, vec, vec], out_specs=[row, row],
        out_shape=[jax.ShapeDtypeStruct((S, D), F32)] * 2,
        compiler_params=_cp(("parallel",)),
    )(x, mix, g.reshape(1, D), b.reshape(1, D))


def _ln_bwd(da, db, z, g, *, ts, name):
    S, D = z.shape
    two = db is not None

    def body(*refs):
        if two:
            da_ref, db_ref, z_ref, g_ref, dz_ref, dg_ref, dbias_ref = refs
            dout = ALPHA * da_ref[...] + db_ref[...]
        else:
            da_ref, z_ref, g_ref, dz_ref, dg_ref, dbias_ref = refs
            dout = da_ref[...]

        @pl.when(pl.program_id(0) == 0)
        def _():
            dg_ref[...] = jnp.zeros_like(dg_ref)
            dbias_ref[...] = jnp.zeros_like(dbias_ref)

        z = z_ref[...]
        mu = jnp.mean(z, axis=-1, keepdims=True)
        zc = z - mu
        var = jnp.mean(zc * zc, axis=-1, keepdims=True)
        rstd = lax.rsqrt(var + LN_EPS)
        xhat = zc * rstd
        dxh = dout * g_ref[...]
        m1 = jnp.mean(dxh, axis=-1, keepdims=True)
        m2 = jnp.mean(dxh * xhat, axis=-1, keepdims=True)
        dz_ref[...] = rstd * (dxh - m1 - xhat * m2)
        dg_ref[...] += _fold8(dout * xhat)
        dbias_ref[...] += _fold8(dout)

    row = pl.BlockSpec((ts, D), lambda i: (i, 0))
    vec = pl.BlockSpec((1, D), lambda i: (0, 0))
    acc = pl.BlockSpec((SUBLANES, D), lambda i: (0, 0))
    args = [da, db, z, g.reshape(1, D)] if two else [da, z, g.reshape(1, D)]
    return pl.pallas_call(
        body, name=name, grid=(S // ts,),
        in_specs=[row] * (3 if two else 2) + [vec],
        out_specs=[row, acc, acc],
        out_shape=[jax.ShapeDtypeStruct((S, D), F32), jax.ShapeDtypeStruct((SUBLANES, D), F32),
                   jax.ShapeDtypeStruct((SUBLANES, D), F32)],
        compiler_params=_cp(("arbitrary",)),
    )(*args)


def _loss_head(y, t, *, ts, name):
    S, D = y.shape

    def body(y_ref, t_ref, dy_ref, p_ref):
        @pl.when(pl.program_id(0) == 0)
        def _():
            p_ref[...] = jnp.zeros_like(p_ref)
        d = y_ref[...] - t_ref[...]
        dy_ref[...] = d * (1.0 / D)
        p_ref[...] += _fold8(d * d)

    row = pl.BlockSpec((ts, D), lambda i: (i, 0))
    acc = pl.BlockSpec((SUBLANES, D), lambda i: (0, 0))
    return pl.pallas_call(
        body, name=name, grid=(S // ts,),
        in_specs=[row, row], out_specs=[row, acc],
        out_shape=[jax.ShapeDtypeStruct((S, D), F32), jax.ShapeDtypeStruct((SUBLANES, D), F32)],
        compiler_params=_cp(("arbitrary",)),
    )(y, t)


def _axpy(a, b, *, ts, name):
    S, D = a.shape

    def body(a_ref, b_ref, o_ref):
        o_ref[...] = ALPHA * a_ref[...] + b_ref[...]

    row = pl.BlockSpec((ts, D), lambda i: (i, 0))
    return pl.pallas_call(
        body, name=name, grid=(S // ts,), in_specs=[row, row], out_specs=row,
        out_shape=jax.ShapeDtypeStruct((S, D), F32), compiler_params=_cp(("parallel",)),
    )(a, b)


def _prev_halo_spec(ts, tc, coff):
    r = ts // SUBLANES
    return pl.BlockSpec((SUBLANES, tc), lambda i, j: (jnp.maximum(i * r - 1, 0), j + coff))


def _fill_prev(buf, halo_ref, cur, i):
    buf[0:SUBLANES, :] = jnp.where(i > 0, halo_ref[...], 0.0)
    buf[SUBLANES:, :] = cur


def _conv_fwd(x, w, b, *, K, ts, tc, x_c0, name):
    S = x.shape[0]
    C = w.shape[1]
    coff = x_c0 // tc

    def body(x_ref, halo_ref, w_ref, b_ref, o_ref, buf):
        _fill_prev(buf, halo_ref, x_ref[...], pl.program_id(0))
        acc = b_ref[...] + w_ref[K - 1:K, :] * x_ref[...]
        for k in range(K - 1):
            acc = acc + w_ref[k:k + 1, :] * buf[pl.ds(SUBLANES - (K - 1) + k, ts), :]
        o_ref[...] = acc

    return pl.pallas_call(
        body, name=name, grid=(S // ts, C // tc),
        in_specs=[pl.BlockSpec((ts, tc), lambda i, j: (i, j + coff)), _prev_halo_spec(ts, tc, coff),
                  pl.BlockSpec((K, tc), lambda i, j: (0, j)), pl.BlockSpec((1, tc), lambda i, j: (0, j))],
        out_specs=pl.BlockSpec((ts, tc), lambda i, j: (i, j)),
        out_shape=jax.ShapeDtypeStruct((S, C), F32),
        scratch_shapes=[pltpu.VMEM((ts + SUBLANES, tc), F32)],
        compiler_params=_cp(("parallel", "parallel")),
    )(x, x, w, b.reshape(1, C))


def _conv_wgrad(dy, x, *, K, ts, tc, x_c0, name):
    S, C = dy.shape
    coff = x_c0 // tc

    def body(dy_ref, x_ref, halo_ref, dw_ref, db_ref, buf):
        i = pl.program_id(1)

        @pl.when(i == 0)
        def _():
            dw_ref[...] = jnp.zeros_like(dw_ref)
            db_ref[...] = jnp.zeros_like(db_ref)

        _fill_prev(buf, halo_ref, x_ref[...], i)
        dy_v = dy_ref[...]
        db_ref[...] += _fold8(dy_v)
        for k in range(K):
            xs = buf[pl.ds(SUBLANES - (K - 1) + k, ts), :]
            dw_ref[k] += _fold8(dy_v * xs)

    r = ts // SUBLANES
    return pl.pallas_call(
        body, name=name, grid=(C // tc, S // ts),
        in_specs=[pl.BlockSpec((ts, tc), lambda j, i: (i, j)),
                  pl.BlockSpec((ts, tc), lambda j, i: (i, j + coff)),
                  pl.BlockSpec((SUBLANES, tc), lambda j, i: (jnp.maximum(i * r - 1, 0), j + coff))],
        out_specs=[pl.BlockSpec((K, SUBLANES, tc), lambda j, i: (0, 0, j)),
                   pl.BlockSpec((SUBLANES, tc), lambda j, i: (0, j))],
        out_shape=[jax.ShapeDtypeStruct((K, SUBLANES, C), F32), jax.ShapeDtypeStruct((SUBLANES, C), F32)],
        scratch_shapes=[pltpu.VMEM((ts + SUBLANES, tc), F32)],
        compiler_params=_cp(("parallel", "arbitrary")),
    )(dy, x, x)


def _conv_bwd(dy, w, *, K, ts, tc, w_c0, out_dtype, name):
    S, C = dy.shape
    nb = S // ts
    r = ts // SUBLANES
    woff = w_c0 // tc

    def body(dy_ref, halo_ref, w_ref, o_ref, buf):
        i = pl.program_id(0)
        buf[0:ts, :] = dy_ref[...]
        buf[ts:, :] = jnp.where(i < nb - 1, halo_ref[...], 0.0)
        acc = w_ref[K - 1:K, :] * dy_ref[...]
        for k in range(K - 1):
            acc = acc + w_ref[k:k + 1, :] * buf[pl.ds(K - 1 - k, ts), :]
        o_ref[...] = acc.astype(out_dtype)

    return pl.pallas_call(
        body, name=name, grid=(nb, C // tc),
        in_specs=[pl.BlockSpec((ts, tc), lambda i, j: (i, j)),
                  pl.BlockSpec((SUBLANES, tc), lambda i, j: (jnp.minimum((i + 1) * r, nb * r - 1), j)),
                  pl.BlockSpec((K, tc), lambda i, j: (0, j + woff))],
        out_specs=pl.BlockSpec((ts, tc), lambda i, j: (i, j)),
        out_shape=jax.ShapeDtypeStruct((S, C), out_dtype),
        scratch_shapes=[pltpu.VMEM((ts + SUBLANES, tc), F32)],
        compiler_params=_cp(("parallel", "parallel")),
    )(dy, dy, w)


HALO16 = 16
FFN_UNROLL = 4


def _ffn_taps_w(w_ref, cs):
    return [w_ref[k:k + 1, cs] for k in range(3)]


def _ffn_taps8(prev, cur):
    row = lax.broadcasted_iota(jnp.int32, cur.shape, 0)
    return (jnp.where(row < 2, pltpu.roll(prev, 2, 0), pltpu.roll(cur, 2, 0)),
            jnp.where(row < 1, pltpu.roll(prev, 1, 0), pltpu.roll(cur, 1, 0)), cur)


def _ffn_conv8(taps, w, b):
    return b + w[0] * taps[0] + w[1] * taps[1] + w[2] * taps[2]


def _ffn_conv8_t(dh, dh_next, w):
    row = lax.broadcasted_iota(jnp.int32, dh.shape, 0)
    s1 = jnp.where(row < SUBLANES - 1, pltpu.roll(dh, SUBLANES - 1, 0), pltpu.roll(dh_next, SUBLANES - 1, 0))
    s2 = jnp.where(row < SUBLANES - 2, pltpu.roll(dh, SUBLANES - 2, 0), pltpu.roll(dh_next, SUBLANES - 2, 0))
    return w[2] * dh + w[1] * s1 + w[0] * s2


def _ffn_mid_specs(ts, tc, nf, nb, with_next):
    r = ts // HALO16
    specs = []
    for off in (0, nf):
        specs.append(pl.BlockSpec((ts, tc), functools.partial(lambda j, i, o: (i, j + o), o=off)))
        specs.append(pl.BlockSpec((HALO16, tc), functools.partial(lambda j, i, o: (jnp.maximum(i * r - 1, 0), j + o), o=off)))
        if with_next:
            specs.append(pl.BlockSpec(
                (HALO16, tc), functools.partial(lambda j, i, o: (jnp.minimum((i + 1) * r, nb * r - 1), j + o), o=off)))
    for rows in (3, 1):
        for off in (0, nf):
            specs.append(pl.BlockSpec((rows, tc), functools.partial(lambda j, i, o: (0, j + o), o=off)))
    return specs


def _ffn_mid_fwd(hpre, w, b, *, ts, tc, name):
    S, F2 = hpre.shape
    F = F2 // 2
    nf = F // tc

    def body(g_ref, gp_ref, u_ref, up_ref, wg_ref, wu_ref, bg_ref, bu_ref, o_ref, gbuf, ubuf, obuf):
        first = pl.program_id(1) == 0
        for buf, prev, cur in ((gbuf, gp_ref, g_ref), (ubuf, up_ref, u_ref)):
            buf[0:HALO16, :] = jnp.where(first, 0.0, prev[...].astype(F32))
            buf[HALO16:, :] = cur[...].astype(F32)
        for lt in range(tc // LANES):
            cs = slice(lt * LANES, (lt + 1) * LANES)
            wg, wu = _ffn_taps_w(wg_ref, cs), _ffn_taps_w(wu_ref, cs)
            bg, bu = bg_ref[:, cs], bu_ref[:, cs]

            def step(c, carry):
                a_g, a_u = carry
                for un in range(FFN_UNROLL):
                    r0 = pl.multiple_of(c * (FFN_UNROLL * SUBLANES), SUBLANES) + un * SUBLANES
                    b_g = gbuf[pl.ds(HALO16 + r0, SUBLANES), cs]
                    b_u = ubuf[pl.ds(HALO16 + r0, SUBLANES), cs]
                    gel, _ = _gelu_parts(_ffn_conv8(_ffn_taps8(a_g, b_g), wg, bg))
                    obuf[pl.ds(r0, SUBLANES), cs] = gel * _ffn_conv8(_ffn_taps8(a_u, b_u), wu, bu)
                    a_g, a_u = b_g, b_u
                return a_g, a_u

            lax.fori_loop(0, ts // (FFN_UNROLL * SUBLANES), step,
                          (gbuf[HALO16 - SUBLANES:HALO16, cs], ubuf[HALO16 - SUBLANES:HALO16, cs]))
        o_ref[...] = obuf[...].astype(BF16)

    b2 = b.reshape(1, F2)
    return pl.pallas_call(
        body, name=name, grid=(nf, S // ts),
        in_specs=_ffn_mid_specs(ts, tc, nf, S // ts, False),
        out_specs=pl.BlockSpec((ts, tc), lambda j, i: (i, j)),
        out_shape=jax.ShapeDtypeStruct((S, F), BF16),
        scratch_shapes=[pltpu.VMEM((ts + HALO16, tc), F32)] * 2 + [pltpu.VMEM((ts, tc), F32)],
        compiler_params=_cp(("parallel", "parallel")),
    )(hpre, hpre, hpre, hpre, w, w, b2, b2)


def _ffn_mid_bwd(hpre, da, w, b, *, ts, tc, name):
    S, F2 = hpre.shape
    F = F2 // 2
    nf = F // tc
    nb = S // ts
    r = ts // HALO16
    nch = ts // SUBLANES

    def body(g_ref, gp_ref, gn_ref, u_ref, up_ref, un_ref, wg_ref, wu_ref, bg_ref, bu_ref, da_ref, dan_ref,
             dpg_ref, dpu_ref, dwg_ref, dwu_ref, dbg_ref, dbu_ref, gbuf, ubuf, dabuf, pgbuf, pubuf):
        i = pl.program_id(1)

        @pl.when(i == 0)
        def _():
            for ref in (dwg_ref, dwu_ref, dbg_ref, dbu_ref):
                ref[...] = jnp.zeros_like(ref)

        for buf, prev, cur, nxt in ((gbuf, gp_ref, g_ref, gn_ref), (ubuf, up_ref, u_ref, un_ref)):
            buf[0:HALO16, :] = jnp.where(i == 0, 0.0, prev[...].astype(F32))
            buf[HALO16:HALO16 + ts, :] = cur[...].astype(F32)
            buf[HALO16 + ts:, :] = nxt[...].astype(F32)
        dabuf[0:ts, :] = da_ref[...].astype(F32)
        dabuf[ts:, :] = jnp.where(i == nb - 1, 0.0, dan_ref[...].astype(F32))

        for lt in range(tc // LANES):
            cs = slice(lt * LANES, (lt + 1) * LANES)
            wg, wu = _ffn_taps_w(wg_ref, cs), _ffn_taps_w(wu_ref, cs)
            bg, bu = bg_ref[:, cs], bu_ref[:, cs]

            def piece(r0, a_g, a_u):
                b_g = gbuf[pl.ds(HALO16 + r0, SUBLANES), cs]
                b_u = ubuf[pl.ds(HALO16 + r0, SUBLANES), cs]
                tg, tu = _ffn_taps8(a_g, b_g), _ffn_taps8(a_u, b_u)
                gel, dgel = _gelu_parts(_ffn_conv8(tg, wg, bg))
                da_v = dabuf[pl.ds(r0, SUBLANES), cs]
                return da_v * _ffn_conv8(tu, wu, bu) * dgel, da_v * gel, tg, tu, b_g, b_u

            def step(c, carry):
                a_g, a_u, pdg, pdu, acc = carry
                for un in range(FFN_UNROLL):
                    r0 = pl.multiple_of(c * (FFN_UNROLL * SUBLANES), SUBLANES) + un * SUBLANES
                    dg, du, tg, tu, a_g, a_u = piece(r0, a_g, a_u)
                    pgbuf[pl.ds(r0, SUBLANES), cs] = _ffn_conv8_t(pdg, dg, wg)
                    pubuf[pl.ds(r0, SUBLANES), cs] = _ffn_conv8_t(pdu, du, wu)
                    acc = (tuple(a + dg * t for a, t in zip(acc[0], tg)), tuple(a + du * t for a, t in zip(acc[1], tu)),
                           acc[2] + dg, acc[3] + du)
                    pdg, pdu = dg, du
                return a_g, a_u, pdg, pdu, acc

            zero = jnp.zeros((SUBLANES, LANES), F32)
            a_g, a_u, pdg, pdu, acc = lax.fori_loop(
                0, nch // FFN_UNROLL, step,
                (gbuf[HALO16 - SUBLANES:HALO16, cs], ubuf[HALO16 - SUBLANES:HALO16, cs], zero, zero,
                 ((zero,) * 3, (zero,) * 3, zero, zero)))
            dg, du, _, _, _, _ = piece(ts, a_g, a_u)
            pgbuf[ts:ts + SUBLANES, cs] = _ffn_conv8_t(pdg, dg, wg)
            pubuf[ts:ts + SUBLANES, cs] = _ffn_conv8_t(pdu, du, wu)
            for k in range(3):
                dwg_ref[k, :, cs] += acc[0][k]
                dwu_ref[k, :, cs] += acc[1][k]
            dbg_ref[:, cs] += acc[2]
            dbu_ref[:, cs] += acc[3]
        dpg_ref[...] = pgbuf[SUBLANES:, :].astype(BF16)
        dpu_ref[...] = pubuf[SUBLANES:, :].astype(BF16)

    b2 = b.reshape(1, F2)
    blk = pl.BlockSpec((ts, tc), lambda j, i: (i, j))
    nxt = pl.BlockSpec((HALO16, tc), lambda j, i: (jnp.minimum((i + 1) * r, nb * r - 1), j))
    in_specs = _ffn_mid_specs(ts, tc, nf, nb, True) + [blk, nxt]
    out_specs = [blk, blk,
                 pl.BlockSpec((3, SUBLANES, tc), lambda j, i: (0, 0, j)), pl.BlockSpec((3, SUBLANES, tc), lambda j, i: (0, 0, j)),
                 pl.BlockSpec((SUBLANES, tc), lambda j, i: (0, j)), pl.BlockSpec((SUBLANES, tc), lambda j, i: (0, j))]
    return pl.pallas_call(
        body, name=name, grid=(nf, nb),
        in_specs=in_specs, out_specs=out_specs,
        out_shape=[jax.ShapeDtypeStruct((S, F), BF16)] * 2 + [jax.ShapeDtypeStruct((3, SUBLANES, F), F32)] * 2
                  + [jax.ShapeDtypeStruct((SUBLANES, F), F32)] * 2,
        scratch_shapes=[pltpu.VMEM((ts + 2 * HALO16, tc), F32)] * 2 + [pltpu.VMEM((ts + HALO16, tc), F32)]
                       + [pltpu.VMEM((ts + SUBLANES, tc), F32)] * 2,
        compiler_params=_cp(("parallel", "arbitrary")),
    )(hpre, hpre, hpre, hpre, hpre, hpre, w, w, b2, b2, da, da)


def _expm1(x):
    u = jnp.exp(x)
    um1 = u - 1.0
    safe = jnp.where(um1 == 0.0, 1.0, jnp.log(u))
    r = jnp.where(um1 == 0.0, x, um1 * x / safe)
    return jnp.where(x < -30.0, -1.0, r)


def _softplus(z):
    return jnp.maximum(z, 0.0) + jnp.log1p(jnp.exp(-jnp.abs(z)))


def _lru_gates(u, wa_ref, ba_ref, wx_ref, bx_ref, lam_ref):
    ub = u.astype(BF16)
    r = _sigmoid(_dot(ub, wa_ref[...]) + ba_ref[...])
    ig = _sigmoid(_dot(ub, wx_ref[...]) + bx_ref[...])
    sp = _softplus(-lam_ref[...])
    la = -LRU_C * r * sp
    a = jnp.exp(la)
    mult = jnp.sqrt(-_expm1(2.0 * la))
    return ub, r, ig, sp, a, mult


def _lru_fwd(u, proj, w_a, b_a, w_x, b_x, lam, *, ts, name):
    S, W = u.shape
    G = LRU_GROUPS
    gw = W // G
    nt = ts // SUBLANES

    def body(u_ref, gb_ref, wa_ref, ba_ref, wx_ref, bx_ref, lam_ref, y_ref, h_ref, a_buf, b_buf, carry):
        @pl.when(pl.program_id(0) == 0)
        def _():
            carry[...] = jnp.zeros_like(carry)

        for g in range(G):
            gs = slice(g * gw, (g + 1) * gw)
            u_v = u_ref[:, gs]
            _, _, ig, _, a, mult = _lru_gates(u_v, wa_ref.at[g], ba_ref.at[g], wx_ref.at[g], bx_ref.at[g], lam_ref.at[g])
            a_buf[:, gs] = a
            b_buf[:, gs] = mult * ig * u_v
        row = lax.broadcasted_iota(jnp.int32, (SUBLANES, W), 0)

        def tile(k, c):
            r0 = pl.multiple_of(k * SUBLANES, SUBLANES)
            A = a_buf[pl.ds(r0, SUBLANES), :]
            B = b_buf[pl.ds(r0, SUBLANES), :]
            for d in (1, 2, 4):
                m = row >= d
                B = jnp.where(m, A * pltpu.roll(B, d, 0) + B, B)
                A = jnp.where(m, A * pltpu.roll(A, d, 0), A)
            h = A * c + B
            h_ref[pl.ds(r0, SUBLANES), :] = h
            return h[SUBLANES - 1:SUBLANES, :]

        carry[...] = lax.fori_loop(0, nt, tile, carry[...])
        gel, _ = _gelu_parts(gb_ref[...])
        y_ref[...] = (gel * h_ref[...]).astype(BF16)

    blk = pl.BlockSpec((ts, W), lambda i: (i, 0))
    wsp = pl.BlockSpec((G, gw, gw), lambda i: (0, 0, 0))
    vsp = pl.BlockSpec((G, 1, gw), lambda i: (0, 0, 0))
    return pl.pallas_call(
        body, name=name, grid=(S // ts,),
        in_specs=[blk, blk, wsp, vsp, wsp, vsp, vsp],
        out_specs=[blk, blk],
        out_shape=[jax.ShapeDtypeStruct((S, W), BF16), jax.ShapeDtypeStruct((S, W), F32)],
        scratch_shapes=[pltpu.VMEM((ts, W), F32), pltpu.VMEM((ts, W), F32), pltpu.VMEM((1, W), F32)],
        compiler_params=_cp(("arbitrary",)),
    )(u, proj, w_a, b_a.reshape(G, 1, gw), w_x, b_x.reshape(G, 1, gw), lam.reshape(G, 1, gw))


def _lru_bwd(dy, u, proj, h, w_a, b_a, w_x, b_x, lam, *, ts, name):
    S, W = u.shape
    G = LRU_GROUPS
    gw = W // G
    nt = ts // SUBLANES
    nb = S // ts
    r8 = ts // SUBLANES

    def body(dy_ref, u_ref, gb_ref, h_ref, hh_ref, wa_ref, ba_ref, wx_ref, bx_ref, lam_ref,
             dgb_ref, du_ref, dwa_ref, dwx_ref, dba_ref, dbx_ref, dlam_ref,
             a_buf, q_buf, p_buf, hbuf, carry):
        i = pl.program_id(0)
        ib = nb - 1 - i

        @pl.when(i == 0)
        def _():
            carry[...] = jnp.zeros_like(carry)
            dwa_ref[...] = jnp.zeros_like(dwa_ref)
            dwx_ref[...] = jnp.zeros_like(dwx_ref)
            dba_ref[...] = jnp.zeros_like(dba_ref)
            dbx_ref[...] = jnp.zeros_like(dbx_ref)
            dlam_ref[...] = jnp.zeros_like(dlam_ref)

        def gates(g):
            gs = slice(g * gw, (g + 1) * gw)
            return gs, _lru_gates(u_ref[:, gs], wa_ref.at[g], ba_ref.at[g], wx_ref.at[g], bx_ref.at[g], lam_ref.at[g])

        for g in range(G):
            gs, (_, _, _, _, a, _) = gates(g)
            gel, dgel = _gelu_parts(gb_ref[:, gs])
            dy_v = dy_ref[:, gs]
            dgb_ref[:, gs] = (dy_v * h_ref[:, gs] * dgel).astype(BF16)
            a_buf[:, gs] = a
            q_buf[:, gs] = a * (dy_v * gel)
        row = lax.broadcasted_iota(jnp.int32, (SUBLANES, W), 0)

        def tile(kk, c):
            r0 = pl.multiple_of((nt - 1 - kk) * SUBLANES, SUBLANES)
            A = a_buf[pl.ds(r0, SUBLANES), :]
            B = q_buf[pl.ds(r0, SUBLANES), :]
            for d in (1, 2, 4):
                m = row < SUBLANES - d
                B = jnp.where(m, A * pltpu.roll(B, SUBLANES - d, 0) + B, B)
                A = jnp.where(m, A * pltpu.roll(A, SUBLANES - d, 0), A)
            P = A * c + B
            p_buf[pl.ds(r0, SUBLANES), :] = jnp.where(row == SUBLANES - 1, c, pltpu.roll(P, SUBLANES - 1, 0))
            return P[0:1, :]

        carry[...] = lax.fori_loop(0, nt, tile, carry[...])
        hbuf[0:SUBLANES, :] = jnp.where(ib > 0, hh_ref[...], 0.0)
        hbuf[SUBLANES:, :] = h_ref[...]
        for g in range(G):
            gs, (ub, r, ig, sp, a, mult) = gates(g)
            u_v = u_ref[:, gs]
            gel, _ = _gelu_parts(gb_ref[:, gs])
            Gt = dy_ref[:, gs] * gel + p_buf[:, gs]
            hprev = hbuf[pl.ds(SUBLANES - 1, ts), gs]
            da = Gt * hprev
            dmult = Gt * (ig * u_v)
            dla = da * a - dmult * (a * a) / mult
            dr = dla * (-LRU_C * sp)
            dlam_ref[g] += _fold8(dla * (LRU_C * r)) * _sigmoid(-lam_ref[g])
            dig = Gt * mult * u_v
            dzr = dr * r * (1.0 - r)
            dzi = dig * ig * (1.0 - ig)
            dzr_b = dzr.astype(BF16)
            dzi_b = dzi.astype(BF16)
            du_ref[:, gs] = Gt * mult * ig + _dot_tb(dzr_b, wa_ref[g]) + _dot_tb(dzi_b, wx_ref[g])
            dwa_ref[g] += _dot_ta(ub, dzr_b)
            dwx_ref[g] += _dot_ta(ub, dzi_b)
            dba_ref[g] += _fold8(dzr)
            dbx_ref[g] += _fold8(dzi)

    rblk = pl.BlockSpec((ts, W), lambda i: (nb - 1 - i, 0))
    halo = pl.BlockSpec((SUBLANES, W), lambda i: (jnp.maximum((nb - 1 - i) * r8 - 1, 0), 0))
    wsp = pl.BlockSpec((G, gw, gw), lambda i: (0, 0, 0))
    vsp = pl.BlockSpec((G, 1, gw), lambda i: (0, 0, 0))
    acc8 = pl.BlockSpec((G, SUBLANES, gw), lambda i: (0, 0, 0))
    return pl.pallas_call(
        body, name=name, grid=(nb,),
        in_specs=[rblk, rblk, rblk, rblk, halo, wsp, vsp, wsp, vsp, vsp],
        out_specs=[rblk, rblk, wsp, wsp, acc8, acc8, acc8],
        out_shape=[jax.ShapeDtypeStruct((S, W), BF16), jax.ShapeDtypeStruct((S, W), F32),
                   jax.ShapeDtypeStruct((G, gw, gw), F32), jax.ShapeDtypeStruct((G, gw, gw), F32),
                   jax.ShapeDtypeStruct((G, SUBLANES, gw), F32), jax.ShapeDtypeStruct((G, SUBLANES, gw), F32),
                   jax.ShapeDtypeStruct((G, SUBLANES, gw), F32)],
        scratch_shapes=[pltpu.VMEM((ts, W), F32)] * 3 + [pltpu.VMEM((ts + SUBLANES, W), F32),
                                                         pltpu.VMEM((1, W), F32)],
        compiler_params=_cp(("arbitrary",)),
    )(dy, u, proj, h, h, w_a, b_a.reshape(G, 1, gw), w_x, b_x.reshape(G, 1, gw), lam.reshape(G, 1, gw))


def _rt(S, pref):
    return min(S, pref)


def _lru_mixer_fwd(x, p, tag):
    S, D = x.shape
    W = p["w_out"].shape[0]
    ts = _rt(S, 512)
    proj = _mm(x, p["w_in"], out_dtype=F32, tm=ts, tn=2 * W, name=tag + "_proj")
    u = _conv_fwd(proj, p["conv_w"], p["conv_b"], K=4, ts=ts, tc=512, x_c0=W, name=tag + "_conv")
    y, h = _lru_fwd(u, proj, p["w_a"], p["b_a"], p["w_x"], p["b_x"], p["lam"], ts=ts, name=tag + "_scan")
    mix = _mm(y, p["w_out"], out_dtype=F32, tm=ts, tn=D, name=tag + "_out")
    return mix, (proj, u, h, y)


def _lru_mixer_bwd(dmix, x, p, saved, tag, gbuf, j):
    proj, u, h, y = saved
    S, D = x.shape
    W = p["w_out"].shape[0]
    ts = _rt(S, 512)
    g = {}
    dy = _mm_tb([(dmix, 0)], p["w_out"], out_dtype=F32, tm=ts, tk=W, name=tag + "_dy")
    n_lru = (DEPTH + 2) // 3
    _grad_into(gbuf, "lru_w_out", (n_lru, W, D), j, 0, 0, y, dmix, tk=W, tn=D, tm=ts, name=tag + "_dwout")
    dgb, du, g["w_a"], g["w_x"], dba8, dbx8, dlam8 = _lru_bwd(
        dy, u, proj, h, p["w_a"], p["b_a"], p["w_x"], p["b_x"], p["lam"], ts=ts, name=tag + "_scanb")
    g["b_a"] = dba8.sum(axis=1)
    g["b_x"] = dbx8.sum(axis=1)
    g["lam"] = dlam8.sum(axis=1).reshape(-1)
    dcw8, dcb8 = _conv_wgrad(du, proj, K=4, ts=ts, tc=512, x_c0=W, name=tag + "_convw")
    g["conv_w"] = dcw8.sum(axis=1)
    g["conv_b"] = dcb8.sum(axis=0)
    drnn = _conv_bwd(du, p["conv_w"], K=4, ts=ts, tc=512, w_c0=0, out_dtype=BF16, name=tag + "_convb")
    dx = _mm_tb([(dgb, 0), (drnn, W)], p["w_in"], out_dtype=F32, tm=ts, tk=D, name=tag + "_dx")
    _grad_into(gbuf, "lru_w_in", (n_lru, D, 2 * W), j, 0, 0, x, dgb, tk=D, tn=W, tm=ts, name=tag + "_dwin_g")
    _grad_into(gbuf, "lru_w_in", (n_lru, D, 2 * W), j, 0, W, x, drnn, tk=D, tn=W, tm=ts, name=tag + "_dwin_r")
    return dx, g


def _ffn_fwd(x, p, tag):
    S, D = x.shape
    F = p["w_down"].shape[0]
    ts = _rt(S, 512)
    tc = F // 2
    hpre = _mm(x, p["w_up"], out_dtype=BF16, tm=ts, tn=2 * F, name=tag + "_up")
    a = _ffn_mid_fwd(hpre, p["conv_w"], p["conv_b"], ts=_rt(S, 256), tc=tc, name=tag + "_mid")
    f = _mm(a, p["w_down"], out_dtype=F32, tm=ts, tn=D, name=tag + "_down")
    return f, (hpre, a)


def _ffn_bwd(df, x, p, saved, tag, gbuf, i):
    hpre, a = saved
    S, D = x.shape
    F = p["w_down"].shape[0]
    ts = _rt(S, 512)
    tw = _rt(S, 1024)
    tc = F // 2
    g = {}
    da = _mm_tb([(df, 0)], p["w_down"], out_dtype=BF16, tm=ts, tk=F, name=tag + "_da")
    _grad_into(gbuf, "ffn_w_down", (DEPTH, F, D), i, 0, 0, a, df, tk=tc, tn=D, tm=tw, name=tag + "_dwdown")
    dpg, dpu, dwg8, dwu8, dbg8, dbu8 = _ffn_mid_bwd(hpre, da, p["conv_w"], p["conv_b"], ts=_rt(S, 256), tc=tc,
                                                    name=tag + "_midb")
    g["conv_w"] = jnp.concatenate([dwg8.sum(axis=1), dwu8.sum(axis=1)], axis=1)
    g["conv_b"] = jnp.concatenate([dbg8.sum(axis=0), dbu8.sum(axis=0)], axis=0)
    dx = _mm_tb([(dpg, 0), (dpu, F)], p["w_up"], out_dtype=F32, tm=ts, tk=D, name=tag + "_dx")
    _grad_into(gbuf, "ffn_w_up", (DEPTH, D, 2 * F), i, 0, 0, x, dpg, tk=D, tn=tc, tm=tw, name=tag + "_dwup_g")
    _grad_into(gbuf, "ffn_w_up", (DEPTH, D, 2 * F), i, 0, F, x, dpu, tk=D, tn=tc, tm=tw, name=tag + "_dwup_u")
    return dx, g


HEAD_SLOT = LANES
MLA_SCALE = (QK_NOPE + QK_ROPE) ** -0.5
NEG_BIG = -1e30
ATTN_BLOCK = 1024


def _rms_fwd(x, g, *, c0, ts, name):
    S = x.shape[0]
    w = g.shape[0]

    def body(x_ref, g_ref, o_ref):
        xv = x_ref[...]
        rstd = lax.rsqrt(jnp.mean(xv * xv, axis=-1, keepdims=True) + RMS_EPS)
        o_ref[...] = (xv * rstd * g_ref[...]).astype(BF16)

    return pl.pallas_call(
        body, name=name, grid=(S // ts,),
        in_specs=[pl.BlockSpec((ts, w), lambda i: (i, c0 // w)), pl.BlockSpec((1, w), lambda i: (0, 0))],
        out_specs=pl.BlockSpec((ts, w), lambda i: (i, 0)),
        out_shape=jax.ShapeDtypeStruct((S, w), BF16), compiler_params=_cp(("parallel",)),
    )(x, g.reshape(1, w))


def _rms_bwd(dy, x, g, *, c0, ts, name):
    S = x.shape[0]
    w = g.shape[0]

    def body(dy_ref, x_ref, g_ref, dx_ref, dg_ref):
        @pl.when(pl.program_id(0) == 0)
        def _():
            dg_ref[...] = jnp.zeros_like(dg_ref)
        xv = x_ref[...]
        dyv = dy_ref[...]
        rstd = lax.rsqrt(jnp.mean(xv * xv, axis=-1, keepdims=True) + RMS_EPS)
        dyg = dyv * g_ref[...]
        m = jnp.mean(dyg * xv, axis=-1, keepdims=True)
        dx_ref[...] = (rstd * (dyg - xv * (rstd * rstd) * m)).astype(BF16)
        dg_ref[...] += _fold8(dyv * xv * rstd)

    return pl.pallas_call(
        body, name=name, grid=(S // ts,),
        in_specs=[pl.BlockSpec((ts, w), lambda i: (i, 0)), pl.BlockSpec((ts, w), lambda i: (i, c0 // w)),
                  pl.BlockSpec((1, w), lambda i: (0, 0))],
        out_specs=[pl.BlockSpec((ts, w), lambda i: (i, 0)), pl.BlockSpec((SUBLANES, w), lambda i: (0, 0))],
        out_shape=[jax.ShapeDtypeStruct((S, w), BF16), jax.ShapeDtypeStruct((SUBLANES, w), F32)],
        compiler_params=_cp(("arbitrary",)),
    )(dy, x, g.reshape(1, w))


def _mla_tables(pos):
    S = pos.shape[0]
    half = QK_ROPE // 2
    inv_freq = ROPE_BASE ** (-jnp.arange(0, QK_ROPE, 2, dtype=F32) / QK_ROPE)
    ang = pos.astype(F32)[:, None] * inv_freq
    cos, sin = jnp.cos(ang), jnp.sin(ang)
    z = lambda n: jnp.zeros((S, n), F32)
    pad = HEAD_SLOT - QK_NOPE - QK_ROPE
    c = jnp.concatenate([jnp.ones((S, QK_NOPE), F32), cos, cos, z(pad)], axis=1)
    s1 = jnp.concatenate([z(QK_NOPE), -sin, z(half), z(pad)], axis=1)
    s2 = jnp.concatenate([z(QK_NOPE), z(half), sin, z(pad)], axis=1)
    return c, s1, s2


def _mla_prep_fwd(qraw, kvraw, proj, tabs, *, kpe_c0, ts, name):
    S = qraw.shape[0]
    H = MLA_HEADS
    half = QK_ROPE // 2

    def body(q_ref, kn_ref, kpe_ref, c_ref, s1_ref, s2_ref, qo_ref, ko_ref):
        c, s1, s2 = c_ref[...], s1_ref[...], s2_ref[...]

        def rope(v):
            return v * c + pltpu.roll(v, HEAD_SLOT - half, 1) * s1 + pltpu.roll(v, half, 1) * s2

        qo_ref[...] = (rope(q_ref[...]) * MLA_SCALE).astype(BF16)
        ko_ref[...] = (kn_ref[...] + rope(kpe_ref[...])).astype(BF16)

    slot = pl.BlockSpec((ts, HEAD_SLOT), lambda i, h: (i, h))
    tab = pl.BlockSpec((ts, HEAD_SLOT), lambda i, h: (i, 0))
    return pl.pallas_call(
        body, name=name, grid=(S // ts, H),
        in_specs=[slot, slot, pl.BlockSpec((ts, HEAD_SLOT), lambda i, h: (i, kpe_c0 // HEAD_SLOT)), tab, tab, tab],
        out_specs=[slot, slot],
        out_shape=[jax.ShapeDtypeStruct((S, H * HEAD_SLOT), BF16)] * 2,
        compiler_params=_cp(("parallel", "parallel")),
    )(qraw, kvraw, proj, *tabs)


def _mla_prep_bwd(dq, dk, dv, tabs, *, ts, name):
    S = dk.shape[0]
    H = MLA_HEADS
    half = QK_ROPE // 2
    kw = H * HEAD_SLOT
    vw = H * V_HEAD

    def body(dq_ref, dk_ref, dv_ref, c_ref, s1_ref, s2_ref, dqr_ref, dkv_ref, dkpe_ref):
        c, s1, s2 = c_ref[...], s1_ref[...], s2_ref[...]

        def rope_t(g):
            return g * c + pltpu.roll(g * s1, half, 1) + pltpu.roll(g * s2, HEAD_SLOT - half, 1)

        gsum = jnp.zeros((ts, HEAD_SLOT), F32)
        for h in range(H):
            sl = slice(h * HEAD_SLOT, (h + 1) * HEAD_SLOT)
            hs = slice((h % 2) * HEAD_SLOT, (h % 2 + 1) * HEAD_SLOT)
            dqr_ref[:, sl] = (rope_t(dq_ref[h // 2, :, hs]) * MLA_SCALE).astype(BF16)
            dkh = dk_ref[:, sl]
            dkv_ref[:, sl] = dkh.astype(BF16)
            gsum = gsum + dkh
        dkv_ref[:, kw:] = dv_ref[...].astype(BF16)
        lane = lax.broadcasted_iota(jnp.int32, (ts, HEAD_SLOT), 1)
        pe = jnp.logical_and(lane >= QK_NOPE, lane < QK_NOPE + QK_ROPE)
        dkpe_ref[...] = rope_t(jnp.where(pe, gsum, 0.0)).astype(BF16)

    tab = pl.BlockSpec((ts, HEAD_SLOT), lambda i: (i, 0))
    return pl.pallas_call(
        body, name=name, grid=(S // ts,),
        in_specs=[pl.BlockSpec((H // 2, ts, 2 * HEAD_SLOT), lambda i: (0, i, 0)), pl.BlockSpec((ts, kw), lambda i: (i, 0)),
                  pl.BlockSpec((ts, vw), lambda i: (i, 0)), tab, tab, tab],
        out_specs=[pl.BlockSpec((ts, kw), lambda i: (i, 0)), pl.BlockSpec((ts, kw + vw), lambda i: (i, 0)), tab],
        out_shape=[jax.ShapeDtypeStruct((S, kw), BF16), jax.ShapeDtypeStruct((S, kw + vw), BF16),
                   jax.ShapeDtypeStruct((S, HEAD_SLOT), BF16)],
        compiler_params=_cp(("parallel",)),
    )(dq, dk, dv, *tabs)


def _attn_pairs(nb, kv_outer):
    if kv_outer:
        pr = [(i, j) for j in range(nb) for i in range(j, nb)]
    else:
        pr = [(i, j) for i in range(nb) for j in range(i + 1)]
    return (jnp.asarray(np.array([p[0] for p in pr], np.int32)), jnp.asarray(np.array([p[1] for p in pr], np.int32)))


def _attn_scores(q_ref, k_ref, hh, diag, T):
    sl = slice(hh * HEAD_SLOT, (hh + 1) * HEAD_SLOT)
    s = _dot_tb(q_ref[:, sl], k_ref[:, sl])
    if not diag:
        return s
    row = lax.broadcasted_iota(jnp.int32, (T, T), 0) // CHUNK
    col = lax.broadcasted_iota(jnp.int32, (T, T), 1) // CHUNK
    return jnp.where(col <= row, s, NEG_BIG)


def _on_block_kind(i, j, step):
    @pl.when(i == j)
    def _():
        step(True)

    @pl.when(i != j)
    def _():
        step(False)


def _attn_fwd(q, k, kvraw, *, T, name):
    S = q.shape[0]
    NP = MLA_HEADS // 2
    nb = S // T
    ii, jj = _attn_pairs(nb, kv_outer=False)
    v_c0 = MLA_HEADS * HEAD_SLOT // LANES

    def body(ii_ref, jj_ref, q_ref, k_ref, v_ref, o_ref, lse_ref, m_sc, l_sc, acc_sc):
        t = pl.program_id(1)
        i, j = ii_ref[t], jj_ref[t]

        @pl.when(j == 0)
        def _():
            m_sc[...] = jnp.full_like(m_sc, NEG_BIG)
            l_sc[...] = jnp.zeros_like(l_sc)
            acc_sc[...] = jnp.zeros_like(acc_sc)

        lo = lax.broadcasted_iota(jnp.int32, (T, LANES), 1) < V_HEAD

        def step(diag):
            v = v_ref[...].astype(BF16)
            vh = (jnp.where(lo, v, jnp.zeros_like(v)), jnp.where(lo, jnp.zeros_like(v), v))
            alphas, pv = [], None
            for hh in range(2):
                s = _attn_scores(q_ref, k_ref, hh, diag, T)
                m_prev = m_sc[hh]
                m_new = jnp.maximum(m_prev, jnp.max(s, axis=1, keepdims=True))
                p = jnp.exp(s - jnp.tile(m_new, (1, T // LANES)))
                alpha = jnp.exp(m_prev - m_new)
                l_sc[hh] = alpha * l_sc[hh] + jnp.sum(p, axis=1, keepdims=True)
                m_sc[hh] = m_new
                alphas.append(alpha)
                t_pv = _dot(p.astype(BF16), vh[hh])
                pv = t_pv if pv is None else pv + t_pv
            acc_sc[...] = acc_sc[...] * jnp.where(lo, alphas[0], alphas[1]) + pv

        _on_block_kind(i, j, step)

        @pl.when(j == i)
        def _():
            l0, l1 = l_sc[0], l_sc[1]
            o_ref[...] = acc_sc[...] * jnp.where(lo, 1.0 / l0, 1.0 / l1)
            lse_ref[...] = jnp.where(lo, m_sc[0] + jnp.log(l0), m_sc[1] + jnp.log(l1))

    grid_spec = pltpu.PrefetchScalarGridSpec(
        num_scalar_prefetch=2, grid=(NP, int(ii.shape[0])),
        in_specs=[pl.BlockSpec((T, 2 * HEAD_SLOT), lambda p, t, ii, jj: (ii[t], p)),
                  pl.BlockSpec((T, 2 * HEAD_SLOT), lambda p, t, ii, jj: (jj[t], p)),
                  pl.BlockSpec((T, LANES), lambda p, t, ii, jj: (jj[t], v_c0 + p))],
        out_specs=[pl.BlockSpec((T, LANES), lambda p, t, ii, jj: (ii[t], p)),
                   pl.BlockSpec((None, T, LANES), lambda p, t, ii, jj: (p, ii[t], 0))],
        scratch_shapes=[pltpu.VMEM((2, T, LANES), F32), pltpu.VMEM((2, T, LANES), F32), pltpu.VMEM((T, LANES), F32)])
    return pl.pallas_call(
        body, name=name, grid_spec=grid_spec,
        out_shape=[jax.ShapeDtypeStruct((S, MLA_HEADS * V_HEAD), F32), jax.ShapeDtypeStruct((NP, S, LANES), F32)],
        compiler_params=_cp(("parallel", "arbitrary")),
    )(ii, jj, q, k, kvraw)


def _attn_bwd_common(q_ref, k_ref, v, do, o, lse, hh, diag, T, lo):
    sel = lo if hh == 0 else jnp.logical_not(lo)
    s = _attn_scores(q_ref, k_ref, hh, diag, T)
    p = jnp.exp(s - lse[:, hh * V_HEAD:hh * V_HEAD + 1])
    do_h = jnp.where(sel, do, 0.0)
    dsum = jnp.sum(do_h * o, axis=1, keepdims=True)
    do_hb = do_h.astype(BF16)
    dp = _dot_tb(do_hb, v)
    return p, p * (dp - dsum), do_hb


def _attn_bwd(q, k, kvraw, do, o, lse, *, T, name):
    S = q.shape[0]
    NP = MLA_HEADS // 2
    nb = S // T
    ii, jj = _attn_pairs(nb, kv_outer=True)
    n_steps = int(ii.shape[0])
    v_c0 = MLA_HEADS * HEAD_SLOT // LANES

    def body(ii_ref, jj_ref, q_ref, k_ref, v_ref, do_ref, o_ref, lse_ref, dq_hbm, dk_ref, dv_ref,
             dq_sc, dk_sc, dv_sc, sem):
        pair = pl.program_id(0)
        t = pl.program_id(1)
        i, j = ii_ref[t], jj_ref[t]

        @pl.when(t == 0)
        def _():
            dq_sc[...] = jnp.zeros_like(dq_sc)

        @pl.when(i == j)
        def _():
            dk_sc[...] = jnp.zeros_like(dk_sc)
            dv_sc[...] = jnp.zeros_like(dv_sc)

        lo = lax.broadcasted_iota(jnp.int32, (T, LANES), 1) < V_HEAD
        rows = pl.ds(pl.multiple_of(i * T, T), T)

        def step(diag):
            v = v_ref[...].astype(BF16)
            do, o_v, lse_v = do_ref[...], o_ref[...], lse_ref[...]
            for hh in range(2):
                sl = slice(hh * HEAD_SLOT, (hh + 1) * HEAD_SLOT)
                p, ds, do_hb = _attn_bwd_common(q_ref, k_ref, v, do, o_v, lse_v, hh, diag, T, lo)
                dsb = ds.astype(BF16)
                dv_sc[...] += _dot_ta(p.astype(BF16), do_hb)
                dk_sc[:, sl] += _dot_ta(dsb, q_ref[:, sl])
                dq_sc[rows, sl] += _dot(dsb, k_ref[:, sl])

        _on_block_kind(i, j, step)

        @pl.when(i == nb - 1)
        def _():
            dk_ref[...] = dk_sc[...]
            dv_ref[...] = dv_sc[...]

        @pl.when(t == n_steps - 1)
        def _():
            cp = pltpu.make_async_copy(dq_sc, dq_hbm.at[pair], sem)
            cp.start()
            cp.wait()

    qi = lambda p, t, ii, jj: (ii[t], p)
    kj = lambda p, t, ii, jj: (jj[t], p)
    grid_spec = pltpu.PrefetchScalarGridSpec(
        num_scalar_prefetch=2, grid=(NP, n_steps),
        in_specs=[pl.BlockSpec((T, 2 * HEAD_SLOT), qi), pl.BlockSpec((T, 2 * HEAD_SLOT), kj),
                  pl.BlockSpec((T, LANES), lambda p, t, ii, jj: (jj[t], v_c0 + p)),
                  pl.BlockSpec((T, LANES), qi), pl.BlockSpec((T, LANES), qi),
                  pl.BlockSpec((None, T, LANES), lambda p, t, ii, jj: (p, ii[t], 0))],
        out_specs=[pl.BlockSpec(memory_space=pl.ANY), pl.BlockSpec((T, 2 * HEAD_SLOT), kj),
                   pl.BlockSpec((T, LANES), kj)],
        scratch_shapes=[pltpu.VMEM((S, 2 * HEAD_SLOT), F32), pltpu.VMEM((T, 2 * HEAD_SLOT), F32),
                        pltpu.VMEM((T, LANES), F32), pltpu.SemaphoreType.DMA])
    return pl.pallas_call(
        body, name=name, grid_spec=grid_spec,
        out_shape=[jax.ShapeDtypeStruct((NP, S, 2 * HEAD_SLOT), F32),
                   jax.ShapeDtypeStruct((S, MLA_HEADS * HEAD_SLOT), F32),
                   jax.ShapeDtypeStruct((S, MLA_HEADS * V_HEAD), F32)],
        compiler_params=_cp(("arbitrary", "arbitrary")),
    )(ii, jj, q, k, kvraw, do, o, lse)


def _mla_permute_weights(w_in, w_uq, w_ukv):
    D = w_in.shape[0]
    H = MLA_HEADS
    qk = QK_NOPE + QK_ROPE
    lat = Q_LORA + KV_LORA
    kpe = jnp.zeros((D, HEAD_SLOT), w_in.dtype).at[:, QK_NOPE:qk].set(w_in[:, lat:])
    w_in_p = jnp.concatenate([w_in[:, :lat], kpe], axis=1)
    w_uq_p = jnp.pad(w_uq.reshape(Q_LORA, H, qk), ((0, 0), (0, 0), (0, HEAD_SLOT - qk))).reshape(Q_LORA, H * HEAD_SLOT)
    kv = w_ukv.reshape(KV_LORA, H, QK_NOPE + V_HEAD)
    wk = jnp.pad(kv[:, :, :QK_NOPE], ((0, 0), (0, 0), (0, HEAD_SLOT - QK_NOPE))).reshape(KV_LORA, H * HEAD_SLOT)
    wv = kv[:, :, QK_NOPE:].reshape(KV_LORA, H * V_HEAD)
    return w_in_p, w_uq_p, jnp.concatenate([wk, wv], axis=1)


def _mla_unpermute_grads(g_in_p, g_uq_p, g_ukv_p):
    H = MLA_HEADS
    qk = QK_NOPE + QK_ROPE
    lat = Q_LORA + KV_LORA
    g_in = jnp.concatenate([g_in_p[:, :lat], g_in_p[:, lat + QK_NOPE:lat + qk]], axis=1)
    g_uq = g_uq_p.reshape(Q_LORA, H, HEAD_SLOT)[:, :, :qk].reshape(Q_LORA, H * qk)
    gk = g_ukv_p[:, :H * HEAD_SLOT].reshape(KV_LORA, H, HEAD_SLOT)[:, :, :QK_NOPE]
    gv = g_ukv_p[:, H * HEAD_SLOT:].reshape(KV_LORA, H, V_HEAD)
    g_ukv = jnp.concatenate([gk, gv], axis=2).reshape(KV_LORA, H * (QK_NOPE + V_HEAD))
    return g_in, g_uq, g_ukv


def _mla_mixer_fwd(x, pos, p, tag):
    S, D = x.shape
    ts = _rt(S, 512)
    T = _rt(S, ATTN_BLOCK)
    lat = Q_LORA + KV_LORA
    tabs = _mla_tables(pos)
    proj = _mm(x, p["w_in_p"], out_dtype=F32, tm=ts, tn=p["w_in_p"].shape[1], name=tag + "_proj")
    qn = _rms_fwd(proj, p["q_norm"], c0=0, ts=ts, name=tag + "_qn")
    kvn = _rms_fwd(proj, p["kv_norm"], c0=Q_LORA, ts=ts, name=tag + "_kvn")
    qraw = _mm(qn, p["w_uq_p"], out_dtype=F32, tm=ts, tn=1024, name=tag + "_uq")
    kvraw = _mm(kvn, p["w_ukv_p"], out_dtype=F32, tm=ts, tn=1024, name=tag + "_ukv")
    q, k = _mla_prep_fwd(qraw, kvraw, proj, tabs, kpe_c0=lat, ts=ts, name=tag + "_prep")
    o, lse = _attn_fwd(q, k, kvraw, T=T, name=tag + "_attn")
    mix = _mm(o, p["w_out"], out_dtype=F32, tm=ts, tn=D, name=tag + "_out")
    return mix, (proj, qn, kvn, kvraw, q, k, o, lse, tabs)


def _mla_mixer_bwd(dmix, x, p, saved, tag, gbuf, j):
    proj, qn, kvn, kvraw, q, k, o, lse, tabs = saved
    S, D = x.shape
    ts = _rt(S, 512)
    T = _rt(S, ATTN_BLOCK)
    lat = Q_LORA + KV_LORA
    g = {}
    do = _mm_tb([(dmix, 0)], p["w_out"], out_dtype=F32, tm=ts, tk=p["w_out"].shape[0], name=tag + "_do")
    g["w_out"] = _mm_ta(o, dmix, tk=p["w_out"].shape[0], tn=D, tm=ts, name=tag + "_dwout")
    dq, dk, dv = _attn_bwd(q, k, kvraw, do, o, lse, T=T, name=tag + "_attn_bwd")
    dqraw, dkvraw, dkpe = _mla_prep_bwd(dq, dk, dv, tabs, ts=_rt(S, 256), name=tag + "_prepb")
    dqn = _mm_tb([(dqraw, 0)], p["w_uq_p"], out_dtype=F32, tm=ts, tk=Q_LORA, name=tag + "_dqn")
    g_uq_p = _mm_ta(qn, dqraw, tk=Q_LORA, tn=1024, tm=ts, name=tag + "_dwuq")
    dkvn = _mm_tb([(dkvraw, 0)], p["w_ukv_p"], out_dtype=F32, tm=ts, tk=KV_LORA, name=tag + "_dkvn")
    g_ukv_p = _mm_ta(kvn, dkvraw, tk=KV_LORA, tn=1024, tm=ts, name=tag + "_dwukv")
    dcq, dqg8 = _rms_bwd(dqn, proj, p["q_norm"], c0=0, ts=ts, name=tag + "_qnb")
    dckv, dkvg8 = _rms_bwd(dkvn, proj, p["kv_norm"], c0=Q_LORA, ts=ts, name=tag + "_kvnb")
    g["q_norm"] = dqg8.sum(axis=0)
    g["kv_norm"] = dkvg8.sum(axis=0)
    dx = _mm_tb([(dcq, 0), (dckv, Q_LORA), (dkpe, lat)], p["w_in_p"], out_dtype=F32, tm=ts, tk=D, name=tag + "_dx")
    g_in_p = jnp.concatenate(
        [_mm_ta(x, dcq, tk=D, tn=Q_LORA, tm=ts, name=tag + "_dwin_q"),
         _mm_ta(x, dckv, tk=D, tn=KV_LORA, tm=ts, name=tag + "_dwin_kv"),
         _mm_ta(x, dkpe, tk=D, tn=HEAD_SLOT, tm=ts, name=tag + "_dwin_pe")], axis=1)
    g["w_in"], g["w_uq"], g["w_ukv"] = _mla_unpermute_grads(g_in_p, g_uq_p, g_ukv_p)
    return dx, g


RET_QK = 256
RET_V = 512


def _ret_tables(pos, T):
    half = RET_QK // 2
    inv_freq = ROPE_BASE ** (-jnp.arange(0, RET_QK, 2, dtype=F32) / RET_QK)
    ang = pos.astype(F32)[:, None] * inv_freq
    lg = jnp.log1p(-jnp.exp2(-5.0 - jnp.arange(RET_HEADS, dtype=F32)))
    idx = jnp.arange(T, dtype=F32)
    ch = jnp.arange(T) // CHUNK
    dm = jnp.where(ch[None, :] <= ch[:, None], jnp.exp(lg[:, None, None] * jnp.abs(idx[:, None] - idx[None, :])), 0.0)
    xi = jnp.broadcast_to(jnp.exp(lg[:, None] * (idx + 1.0))[:, :, None], (RET_HEADS, T, RET_QK))
    zeta = jnp.broadcast_to(jnp.exp(lg[:, None] * (T - 1.0 - idx))[:, :, None], (RET_HEADS, T, RET_QK))
    g_t = jnp.broadcast_to(jnp.exp(lg * T)[:, None, None], (RET_HEADS, 1, RET_V))
    assert half == LANES
    return jnp.cos(ang), jnp.sin(ang), dm.astype(F32), xi.astype(F32), zeta.astype(F32), g_t.astype(F32)


def _rope_half(x, c, s):
    x1, x2 = x[:, :LANES], x[:, LANES:]
    return jnp.concatenate([x1 * c - x2 * s, x1 * s + x2 * c], axis=1)


def _rope_half_t(g, c, s):
    g1, g2 = g[:, :LANES], g[:, LANES:]
    return jnp.concatenate([g1 * c + g2 * s, g2 * c - g1 * s], axis=1)


def _ret_qkv(q_ref, k_ref, v_ref, c_ref, s_ref):
    c, s = c_ref[...], s_ref[...]
    q = _rope_half(q_ref[...], c, s)
    k = _rope_half(k_ref[...], c, s) * (RET_QK ** -0.5)
    return q, k, v_ref[...].astype(BF16)


def _ret_in_specs(T, H, rev_nb=None):
    rb = (lambda n: n) if rev_nb is None else (lambda n: rev_nb - 1 - n)
    nq = H * RET_QK // RET_QK
    nv = 2 * H * RET_QK // RET_V
    return dict(
        q=pl.BlockSpec((T, RET_QK), lambda h, n: (rb(n), h)),
        k=pl.BlockSpec((T, RET_QK), lambda h, n: (rb(n), nq + h)),
        v=pl.BlockSpec((T, RET_V), lambda h, n: (rb(n), nv + h)),
        g=pl.BlockSpec((T, RET_V), lambda h, n: (rb(n), nv + H + h)),
        yv=pl.BlockSpec((T, RET_V), lambda h, n: (rb(n), h)),
        cs=pl.BlockSpec((T, LANES), lambda h, n: (rb(n), 0)),
        dm=pl.BlockSpec((None, T, T), lambda h, n: (h, 0, 0)),
        xz=pl.BlockSpec((None, T, RET_QK), lambda h, n: (h, 0, 0)),
        gt=pl.BlockSpec((None, 1, RET_V), lambda h, n: (h, 0, 0)),
        gn=pl.BlockSpec((1, RET_V), lambda h, n: (0, h)),
        st=pl.BlockSpec((None, None, RET_QK, RET_V), lambda h, n: (h, rb(n), 0, 0)),
    )


def _ret_fwd(proj, gn_g, tabs, *, T, name):
    S = proj.shape[0]
    H = RET_HEADS
    nb = S // T
    cos, sin, dm, xi, zeta, g_t = tabs
    sp = _ret_in_specs(T, H)

    def body(q_ref, k_ref, v_ref, g_ref, gn_ref, c_ref, s_ref, dm_ref, xi_ref, zeta_ref, gt_ref,
             o_ref, y_ref, st_ref, st):
        @pl.when(pl.program_id(1) == 0)
        def _():
            st[...] = jnp.zeros_like(st)

        q, k, vb = _ret_qkv(q_ref, k_ref, v_ref, c_ref, s_ref)
        qb, kb = q.astype(BF16), k.astype(BF16)
        s0 = st[...]
        s0b = s0.astype(BF16)
        st_ref[...] = s0b
        a = _dot_tb(qb, kb) * dm_ref[...]
        y = _dot(a.astype(BF16), vb) + _dot((q * xi_ref[...]).astype(BF16), s0b)
        st[...] = s0 * gt_ref[...] + _dot_ta((k * zeta_ref[...]).astype(BF16), vb)
        y_ref[...] = y
        mu = jnp.mean(y, axis=-1, keepdims=True)
        yc = y - mu
        var = jnp.mean(yc * yc, axis=-1, keepdims=True)
        gv = g_ref[...]
        o_ref[...] = (gv * _sigmoid(gv) * (yc * lax.rsqrt(var + LN_EPS) * gn_ref[...])).astype(BF16)

    return pl.pallas_call(
        body, name=name, grid=(H, nb),
        in_specs=[sp["q"], sp["k"], sp["v"], sp["g"], sp["gn"], sp["cs"], sp["cs"], sp["dm"], sp["xz"], sp["xz"], sp["gt"]],
        out_specs=[sp["yv"], sp["yv"], sp["st"]],
        out_shape=[jax.ShapeDtypeStruct((S, H * RET_V), BF16), jax.ShapeDtypeStruct((S, H * RET_V), F32),
                   jax.ShapeDtypeStruct((H, nb, RET_QK, RET_V), BF16)],
        scratch_shapes=[pltpu.VMEM((RET_QK, RET_V), F32)],
        compiler_params=_cp(("parallel", "arbitrary")),
    )(proj, proj, proj, proj, gn_g.reshape(1, H * RET_V), cos, sin, dm, xi, zeta, g_t)


def _ret_gn_bwd(dout, proj, y, gn_g, *, ts, name):
    S = proj.shape[0]
    H = RET_HEADS
    goff = 2 * H * RET_QK // RET_V + H

    def body(do_ref, g_ref, y_ref, gn_ref, dy_ref, dg_ref, dgn_ref):
        @pl.when(pl.program_id(1) == 0)
        def _():
            dgn_ref[...] = jnp.zeros_like(dgn_ref)
        y_v = y_ref[...]
        mu = jnp.mean(y_v, axis=-1, keepdims=True)
        yc = y_v - mu
        var = jnp.mean(yc * yc, axis=-1, keepdims=True)
        rstd = lax.rsqrt(var + LN_EPS)
        yhat = yc * rstd
        gv = g_ref[...]
        sg = _sigmoid(gv)
        dout = do_ref[...]
        gn = gn_ref[...]
        dg_ref[...] = (dout * (yhat * gn) * (sg * (1.0 + gv * (1.0 - sg)))).astype(BF16)
        dyn = dout * (gv * sg)
        dgn_ref[...] += _fold8(dyn * yhat)
        dyh = dyn * gn
        m1 = jnp.mean(dyh, axis=-1, keepdims=True)
        m2 = jnp.mean(dyh * yhat, axis=-1, keepdims=True)
        dy_ref[...] = (rstd * (dyh - m1 - yhat * m2)).astype(BF16)

    blk = pl.BlockSpec((ts, RET_V), lambda h, i: (i, h))
    return pl.pallas_call(
        body, name=name, grid=(H, S // ts),
        in_specs=[blk, pl.BlockSpec((ts, RET_V), lambda h, i: (i, goff + h)), blk,
                  pl.BlockSpec((1, RET_V), lambda h, i: (0, h))],
        out_specs=[blk, blk, pl.BlockSpec((SUBLANES, RET_V), lambda h, i: (0, h))],
        out_shape=[jax.ShapeDtypeStruct((S, H * RET_V), BF16), jax.ShapeDtypeStruct((S, H * RET_V), BF16),
                   jax.ShapeDtypeStruct((SUBLANES, H * RET_V), F32)],
        compiler_params=_cp(("parallel", "arbitrary")),
    )(dout, proj, y, gn_g.reshape(1, H * RET_V))


def _ret_bwd(proj, dy, states, tabs, *, T, name):
    S = proj.shape[0]
    H = RET_HEADS
    nb = S // T
    cos, sin, dm, xi, zeta, g_t = tabs
    sp = _ret_in_specs(T, H, rev_nb=nb)

    def body(q_ref, k_ref, v_ref, dy_ref, st_ref, c_ref, s_ref, dm_ref, xi_ref, zeta_ref, gt_ref,
             dq_ref, dk_ref, dv_ref, ds):
        @pl.when(pl.program_id(1) == 0)
        def _():
            ds[...] = jnp.zeros_like(ds)

        q, k, vb = _ret_qkv(q_ref, k_ref, v_ref, c_ref, s_ref)
        qb, kb = q.astype(BF16), k.astype(BF16)
        dyb = dy_ref[...]
        s0b = st_ref[...]
        dmv, xiv, zv = dm_ref[...], xi_ref[...], zeta_ref[...]
        ds_v = ds[...]
        dsb = ds_v.astype(BF16)
        gm = (_dot_tb(dyb, vb) * dmv).astype(BF16)
        ab = (_dot_tb(qb, kb) * dmv).astype(BF16)
        kz = (k * zv).astype(BF16)
        qx = (q * xiv).astype(BF16)
        dq = _dot(gm, kb) + xiv * _dot_tb(dyb, s0b)
        dk = _dot_ta(gm, qb) + zv * _dot_tb(vb, dsb)
        dv_ref[...] = (_dot_ta(ab, dyb) + _dot(kz, dsb)).astype(BF16)
        ds[...] = ds_v * gt_ref[...] + _dot_ta(qx, dyb)
        c, s = c_ref[...], s_ref[...]
        dq_ref[...] = _rope_half_t(dq, c, s).astype(BF16)
        dk_ref[...] = _rope_half_t(dk * (RET_QK ** -0.5), c, s).astype(BF16)

    qblk = pl.BlockSpec((T, RET_QK), lambda h, n: (nb - 1 - n, h))
    return pl.pallas_call(
        body, name=name, grid=(H, nb),
        in_specs=[sp["q"], sp["k"], sp["v"], sp["yv"], sp["st"], sp["cs"], sp["cs"], sp["dm"], sp["xz"], sp["xz"], sp["gt"]],
        out_specs=[qblk, qblk, sp["yv"]],
        out_shape=[jax.ShapeDtypeStruct((S, H * RET_QK), BF16), jax.ShapeDtypeStruct((S, H * RET_QK), BF16),
                   jax.ShapeDtypeStruct((S, H * RET_V), BF16)],
        scratch_shapes=[pltpu.VMEM((RET_QK, RET_V), F32)],
        compiler_params=_cp(("parallel", "arbitrary")),
    )(proj, proj, proj, dy, states, cos, sin, dm, xi, zeta, g_t)


def _ret_mixer_fwd(x, pos, p, tag):
    S, D = x.shape
    ts = _rt(S, 512)
    T = _rt(S, 256)
    tabs = _ret_tables(pos, T)
    proj = _mm(x, p["w_in"], out_dtype=F32, tm=_rt(S, 256), tn=p["w_in"].shape[1], name=tag + "_proj")
    gated, y, states = _ret_fwd(proj, p["gn_g"], tabs, T=T, name=tag + "_ret")
    mix = _mm(gated, p["w_out"], out_dtype=F32, tm=ts, tn=D, name=tag + "_out")
    return mix, (proj, gated, y, states, tabs)


def _ret_mixer_bwd(dmix, x, p, saved, tag, gbuf, j):
    proj, gated, y, states, tabs = saved
    S, D = x.shape
    ts = _rt(S, 512)
    T = _rt(S, 256)
    H = RET_HEADS
    hq, hv = H * RET_QK, H * RET_V
    g = {}
    dout = _mm_tb([(dmix, 0)], p["w_out"], out_dtype=F32, tm=ts, tk=1024, name=tag + "_dgated")
    n_ret = DEPTH // 3
    _grad_into(gbuf, "ret_w_out", (n_ret, hv, D), j, 0, 0, gated, dmix, tk=1024, tn=D, tm=ts, name=tag + "_dwout")
    dy, dgate, dgn8 = _ret_gn_bwd(dout, proj, y, p["gn_g"], ts=_rt(S, 256), name=tag + "_gnb")
    g["gn_g"] = dgn8.sum(axis=0)
    dq, dk, dv = _ret_bwd(proj, dy, states, tabs, T=T, name=tag + "_retb")
    dx = _mm_tb([(dq, 0), (dk, hq), (dv, 2 * hq), (dgate, 2 * hq + hv)], p["w_in"], out_dtype=F32,
                tm=ts, tk=512, name=tag + "_dx")
    w_in_shape = (n_ret, D, 2 * hq + 2 * hv)
    for part, c0, nm in ((dq, 0, "q"), (dk, hq, "k"), (dv, 2 * hq, "v"), (dgate, 2 * hq + hv, "g")):
        _grad_into(gbuf, "ret_w_in", w_in_shape, j, 0, c0, x, part, tk=D, tn=1024, tm=ts, name=tag + "_dwin_" + nm)
    return dx, g


PACK_W = 1024
ANY = pl.BlockSpec(memory_space=pl.ANY)
MESH = pl.DeviceIdType.MESH


def _coords():
    return lax.axis_index("x"), lax.axis_index("y"), lax.axis_index("c")


def _chip_peers(x, y):
    return [(1 - x, y), (x, 1 - y), (1 - x, 1 - y)]


def _slot(ref, axis, s, n):
    if axis is None:
        return ref.at[s]
    size = n // N_CHIPS
    sl = pl.ds(pl.multiple_of(s * size, LANES if axis == 2 else 2 * SUBLANES), size)
    return ref.at[:, sl, :] if axis == 1 else ref.at[:, :, sl]


def _gather_chips(items, name):
    n = len(items)
    axes = [ax for _, ax in items]
    out_shapes = []
    for arr, ax in items:
        shp = (N_CHIPS,) + arr.shape if ax is None else tuple(d * (N_CHIPS if i == ax else 1) for i, d in enumerate(arr.shape))
        out_shapes.append(jax.ShapeDtypeStruct(shp, arr.dtype))

    def body(*refs):
        srcs, outs = refs[:n], refs[n:2 * n]
        send_sems, recv_sems, local_sems = refs[2 * n:]
        x, y, c = _coords()
        me = 2 * x + y
        dst = lambda t, s: _slot(outs[t], axes[t], s, out_shapes[t].shape[axes[t]] if axes[t] is not None else 0)
        local = [pltpu.make_async_copy(srcs[t], dst(t, me), local_sems.at[t]) for t in range(n)]
        for cp in local:
            cp.start()
        sends, recvs = [], []
        for k, (px, py) in enumerate(_chip_peers(x, y)):
            for t in range(n):
                sem = k * n + t
                sends.append(pltpu.make_async_remote_copy(
                    src_ref=srcs[t], dst_ref=dst(t, me), send_sem=send_sems.at[sem], recv_sem=recv_sems.at[sem],
                    device_id=(px, py, c), device_id_type=MESH))
                recvs.append(pltpu.make_async_remote_copy(
                    src_ref=srcs[t], dst_ref=dst(t, 2 * px + py), send_sem=send_sems.at[sem],
                    recv_sem=recv_sems.at[sem], device_id=(px, py, c), device_id_type=MESH))
        for cp in sends:
            cp.start()
        for cp in recvs:
            cp.wait_recv()
        for cp in sends:
            cp.wait_send()
        for cp in local:
            cp.wait()

    return pl.pallas_call(
        body, name=name, in_specs=[ANY] * n, out_specs=[ANY] * n, out_shape=out_shapes,
        scratch_shapes=[pltpu.SemaphoreType.DMA((3 * n,)), pltpu.SemaphoreType.DMA((3 * n,)),
                        pltpu.SemaphoreType.DMA((n,))],
    )(*[arr for arr, _ in items])


def _scatter_chips(items, name):
    n = len(items)
    axes = [ax for _, ax in items]
    out_shapes = []
    for arr, ax in items:
        part = arr.shape[1:] if ax is None else tuple(d // (N_CHIPS if i == ax else 1) for i, d in enumerate(arr.shape))
        out_shapes.append(jax.ShapeDtypeStruct((3,) + part, arr.dtype))

    def body(*refs):
        srcs, outs = refs[:n], refs[n:2 * n]
        send_sems, recv_sems = refs[2 * n:]
        x, y, c = _coords()
        copies = []
        for k, (px, py) in enumerate(_chip_peers(x, y)):
            for t in range(n):
                src = _slot(srcs[t], axes[t], 2 * px + py, srcs[t].shape[axes[t]] if axes[t] is not None else 0)
                copies.append(pltpu.make_async_remote_copy(
                    src_ref=src, dst_ref=outs[t].at[k], send_sem=send_sems.at[k * n + t],
                    recv_sem=recv_sems.at[k * n + t], device_id=(px, py, c), device_id_type=MESH))
        for cp in copies:
            cp.start()
        for cp in copies:
            cp.wait_recv()
        for cp in copies:
            cp.wait_send()

    return pl.pallas_call(
        body, name=name, in_specs=[ANY] * n, out_specs=[ANY] * n, out_shape=out_shapes,
        scratch_shapes=[pltpu.SemaphoreType.DMA((3 * n,)), pltpu.SemaphoreType.DMA((3 * n,))],
    )(*[arr for arr, _ in items])


def _swap_sibling(arrs, name):
    n = len(arrs)

    def body(*refs):
        srcs, outs = refs[:n], refs[n:2 * n]
        send_sems, recv_sems = refs[2 * n:]
        x, y, c = _coords()
        copies = [pltpu.make_async_remote_copy(src_ref=srcs[t], dst_ref=outs[t], send_sem=send_sems.at[t],
                                               recv_sem=recv_sems.at[t], device_id=(x, y, 1 - c), device_id_type=MESH)
                  for t in range(n)]
        for cp in copies:
            cp.start()
        for cp in copies:
            cp.wait_recv()
        for cp in copies:
            cp.wait_send()

    return pl.pallas_call(
        body, name=name, in_specs=[ANY] * n, out_specs=[ANY] * n,
        out_shape=[jax.ShapeDtypeStruct(a_.shape, a_.dtype) for a_ in arrs],
        scratch_shapes=[pltpu.SemaphoreType.DMA((n,)), pltpu.SemaphoreType.DMA((n,))],
    )(*arrs)


def _allreduce_small(v, name):
    R, Wd = v.shape

    def body(v_ref, o_ref, buf, send_sems, recv_sems):
        x, y, c = _coords()
        o_ref[...] = v_ref[...]
        for st, peer in enumerate([(x, y, 1 - c), (x, 1 - y, c), (1 - x, y, c)]):
            cp = pltpu.make_async_remote_copy(src_ref=o_ref, dst_ref=buf.at[st], send_sem=send_sems.at[st],
                                              recv_sem=recv_sems.at[st], device_id=peer, device_id_type=MESH)
            cp.start()
            cp.wait_recv()
            cp.wait_send()
            o_ref[...] = o_ref[...] + buf[st]

    vm = pl.BlockSpec(memory_space=pltpu.VMEM)
    return pl.pallas_call(
        body, name=name, in_specs=[vm], out_specs=vm,
        out_shape=jax.ShapeDtypeStruct((R, Wd), F32),
        scratch_shapes=[pltpu.VMEM((3, R, Wd), F32), pltpu.SemaphoreType.DMA((3,)), pltpu.SemaphoreType.DMA((3,))],
    )(v)


def _row_tile(rows):
    t = rows
    while t > 256:
        assert t % 2 == 0
        t //= 2
    assert t % SUBLANES == 0
    return t


def _sum_partials(g, recv, axis, *, name):
    _, L, R, C = recv.shape
    tr = _row_tile(R)
    me = (2 * lax.axis_index("x") + lax.axis_index("y")).astype(jnp.int32).reshape(1)

    def body(me_ref, g_ref, r_ref, o_ref):
        o_ref[...] = ((g_ref[...] + r_ref[0].astype(F32)) + r_ref[1].astype(F32)) + r_ref[2].astype(F32)

    if axis is None:
        g_spec = pl.BlockSpec((None, None, tr, C), lambda l, i, me: (me[0], l, i, 0))
    elif axis == 1:
        g_spec = pl.BlockSpec((None, tr, C), lambda l, i, me: (l, me[0] * (R // tr) + i, 0))
    else:
        g_spec = pl.BlockSpec((None, tr, C), lambda l, i, me: (l, i, me[0]))
    grid_spec = pltpu.PrefetchScalarGridSpec(
        num_scalar_prefetch=1, grid=(L, R // tr),
        in_specs=[g_spec, pl.BlockSpec((3, None, tr, C), lambda l, i, me: (0, l, i, 0))],
        out_specs=pl.BlockSpec((None, tr, C), lambda l, i, me: (l, i, 0)))
    return pl.pallas_call(
        body, name=name, grid_spec=grid_spec, out_shape=jax.ShapeDtypeStruct((L, R, C), F32),
        compiler_params=_cp(("parallel", "parallel")),
    )(me, g, recv)


def _adamw(w, m, v, ga, gb, *, name):
    L, R, C = w.shape
    tr = _row_tile(R)
    two = gb is not None
    c1 = 1.0 / (1.0 - ADAM_B1 ** ADAM_STEP)
    c2 = 1.0 / (1.0 - ADAM_B2 ** ADAM_STEP)

    def body(*refs):
        if two:
            w_ref, m_ref, v_ref, ga_ref, gb_ref, g_ref, d_ref, mo_ref, vo_ref = refs
            g = ga_ref[...] + gb_ref[...]
        else:
            w_ref, m_ref, v_ref, ga_ref, g_ref, d_ref, mo_ref, vo_ref = refs
            g = ga_ref[...]
        m2 = ADAM_B1 * m_ref[...] + (1.0 - ADAM_B1) * g
        v2 = ADAM_B2 * v_ref[...] + (1.0 - ADAM_B2) * (g * g)
        g_ref[...] = g
        mo_ref[...] = m2
        vo_ref[...] = v2
        d_ref[...] = -ADAM_LR * ((m2 * c1) / (jnp.sqrt(v2 * c2) + ADAM_EPS) + ADAM_WD * w_ref[...])

    blk = pl.BlockSpec((None, tr, C), lambda l, i: (l, i, 0))
    args = [w, m, v, ga] + ([gb] if two else [])
    return pl.pallas_call(
        body, name=name, grid=(L, R // tr), in_specs=[blk] * len(args), out_specs=[blk] * 4,
        out_shape=[jax.ShapeDtypeStruct((L, R, C), F32)] * 4, compiler_params=_cp(("parallel", "parallel")),
    )(*args)


SHARDED = [
    ("ffn_w_up", 2, True), ("ffn_conv_w", 2, False), ("ffn_w_down", 1, True),
    ("lru_w_in", 2, True), ("lru_conv_w", 2, False), ("lru_conv_b", 1, False),
    ("lru_w_a", 2, True), ("lru_b_a", 2, False), ("lru_w_x", 2, True), ("lru_b_x", 2, False),
    ("lru_lambda", 1, False), ("lru_w_out", 1, True),
    ("mla_w_in", 2, True), ("mla_w_uq", 2, True), ("mla_w_ukv", 2, True), ("mla_w_out", 1, True),
    ("ret_w_in", 2, True), ("ret_gn_g", 1, False), ("ret_w_out", 1, True),
]
BIG_AXIS = {"ffn_w_up": 2, "ffn_w_down": 1, "lru_w_in": 2, "lru_w_out": 1, "ret_w_in": 2, "ret_w_out": 1}
REPLICATED = ["ln1_g", "ln1_b", "ln2_g", "ln2_b", "ffn_conv_b", "mla_q_norm", "mla_kv_norm"]
WEIGHTS = ["ln1_g", "ln1_b", "ln2_g", "ln2_b", "ffn_w_up", "ffn_conv_w", "ffn_conv_b", "ffn_w_down", "lru_w_in",
           "lru_conv_w", "lru_conv_b", "lru_w_a", "lru_b_a", "lru_w_x", "lru_b_x", "lru_lambda", "lru_w_out",
           "mla_w_in", "mla_q_norm", "mla_kv_norm", "mla_w_uq", "mla_w_ukv", "mla_w_out", "ret_w_in", "ret_gn_g",
           "ret_w_out"]
PACK_ROWS = 512


def _pack(arrs, dtype, lead=(), rows=PACK_ROWS):
    nl = len(lead)
    flat = jnp.concatenate([a.astype(dtype).reshape(lead + (-1,)) for a in arrs], axis=nl)
    n = flat.shape[nl]
    quantum = rows * PACK_W
    total = -(-n // quantum) * quantum
    flat = jnp.pad(flat, [(0, 0)] * nl + [(0, total - n)])
    return flat.reshape(lead + (total // PACK_W, PACK_W))


def _unpack(buf, shapes, lead=()):
    nl = len(lead)
    flat = buf.reshape(lead + (-1,))
    out, off = [], 0
    for shp in shapes:
        n = int(np.prod(shp))
        out.append(lax.slice_in_dim(flat, off, off + n, axis=nl).reshape(lead + tuple(shp)))
        off += n
    return out


def _layer_params(full, rep, i):
    kind, j = i % 3, i // 3
    ffn = dict(w_up=full["ffn_w_up"][i], conv_w=full["ffn_conv_w"][i], conv_b=rep["ffn_conv_b"][i],
               w_down=full["ffn_w_down"][i])
    if kind == 0:
        mix = dict(w_in=full["lru_w_in"][j], conv_w=full["lru_conv_w"][j], conv_b=full["lru_conv_b"][j],
                   w_a=full["lru_w_a"][j], b_a=full["lru_b_a"][j], w_x=full["lru_w_x"][j], b_x=full["lru_b_x"][j],
                   lam=full["lru_lambda"][j], w_out=full["lru_w_out"][j])
    elif kind == 1:
        w_in_p, w_uq_p, w_ukv_p = _mla_permute_weights(full["mla_w_in"][j], full["mla_w_uq"][j], full["mla_w_ukv"][j])
        mix = dict(w_in_p=w_in_p, w_uq_p=w_uq_p, w_ukv_p=w_ukv_p, q_norm=rep["mla_q_norm"][j],
                   kv_norm=rep["mla_kv_norm"][j], w_out=full["mla_w_out"][j])
    else:
        mix = dict(w_in=full["ret_w_in"][j], gn_g=full["ret_gn_g"][j], w_out=full["ret_w_out"][j])
    return kind, mix, ffn


_MIX_FWD = {0: lambda x, pos, p, tag: _lru_mixer_fwd(x, p, tag), 1: _mla_mixer_fwd, 2: _ret_mixer_fwd}
_MIX_BWD = {0: _lru_mixer_bwd, 1: _mla_mixer_bwd, 2: _ret_mixer_bwd}
_MIX_PREFIX = {0: "lru_", 1: "mla_", 2: "ret_"}
_MIX_KEYS = {0: {"w_in": "lru_w_in", "conv_w": "lru_conv_w", "conv_b": "lru_conv_b", "w_a": "lru_w_a", "b_a": "lru_b_a",
                 "w_x": "lru_w_x", "b_x": "lru_b_x", "lam": "lru_lambda", "w_out": "lru_w_out"},
             1: {"w_in": "mla_w_in", "q_norm": "mla_q_norm", "kv_norm": "mla_kv_norm", "w_uq": "mla_w_uq",
                 "w_ukv": "mla_w_ukv", "w_out": "mla_w_out"},
             2: {"w_in": "ret_w_in", "gn_g": "ret_gn_g", "w_out": "ret_w_out"}}
_FFN_KEYS = {"w_up": "ffn_w_up", "conv_w": "ffn_conv_w", "conv_b": "ffn_conv_b", "w_down": "ffn_w_down"}


def _local_step(x, pos, target, full, rep):
    S, D = x.shape
    ts = _rt(S, 256)
    acts = []
    h = x
    for i in range(DEPTH):
        kind, mp, fp = _layer_params(full, rep, i)
        tag = "l%d" % i
        mix, msaved = _MIX_FWD[kind](h, pos, mp, tag + "m")
        h1, z1 = _ln_fwd(h, mix, rep["ln1_g"][i], rep["ln1_b"][i], ts=ts, name=tag + "_ln1")
        f, fsaved = _ffn_fwd(h1, fp, tag + "f")
        h2, z2 = _ln_fwd(h1, f, rep["ln2_g"][i], rep["ln2_b"][i], ts=ts, name=tag + "_ln2")
        acts.append((kind, mp, fp, h, msaved, h1, z1, fsaved, z2))
        h = h2
    dy, part = _loss_head(h, target, ts=ts, name="loss_head")

    grads = {n: {} for n in WEIGHTS if n not in BIG_AXIS}
    gbuf = {}
    d_a, d_b = dy, None
    for i in reversed(range(DEPTH)):
        kind, mp, fp, h_in, msaved, h1, z1, fsaved, z2 = acts[i]
        tag = "l%d" % i
        dz2, dg8, db8 = _ln_bwd(d_a, d_b, z2, rep["ln2_g"][i], ts=ts, name=tag + "_ln2b")
        grads["ln2_g"][i], grads["ln2_b"][i] = dg8.sum(axis=0), db8.sum(axis=0)
        dx_f, gf = _ffn_bwd(dz2, h1, fp, fsaved, tag + "f", gbuf, i)
        for k, v in gf.items():
            grads[_FFN_KEYS[k]][i] = v
        dz1, dg8, db8 = _ln_bwd(dz2, dx_f, z1, rep["ln1_g"][i], ts=ts, name=tag + "_ln1b")
        grads["ln1_g"][i], grads["ln1_b"][i] = dg8.sum(axis=0), db8.sum(axis=0)
        dx_m, gm = _MIX_BWD[kind](dz1, h_in, mp, msaved, tag + "m", gbuf, i // 3)
        for k, v in gm.items():
            grads[_MIX_KEYS[kind][k]][i // 3] = v
        d_a, d_b = dz1, dx_m
    grad_x = _axpy(d_a, d_b, ts=ts, name="grad_x")
    stacked = {n: jnp.stack([grads[n][j] for j in sorted(grads[n])]) for n in grads}
    return part, grad_x, stacked, gbuf


def kernel(x, positions, ln1_g, ln1_b, ln2_g, ln2_b, ffn_w_up, ffn_conv_w, ffn_conv_b, ffn_w_down, lru_w_in, lru_conv_w, lru_conv_b, lru_w_a, lru_b_a, lru_w_x, lru_b_x, lru_lambda, lru_w_out, mla_w_in, mla_q_norm, mla_kv_norm, mla_w_uq, mla_w_ukv, mla_w_out, ret_w_in, ret_gn_g, ret_w_out, loss_target, m_ln1_g, m_ln1_b, m_ln2_g, m_ln2_b, m_ffn_w_up, m_ffn_conv_w, m_ffn_conv_b, m_ffn_w_down, m_lru_w_in, m_lru_conv_w, m_lru_conv_b, m_lru_w_a, m_lru_b_a, m_lru_w_x, m_lru_b_x, m_lru_lambda, m_lru_w_out, m_mla_w_in, m_mla_q_norm, m_mla_kv_norm, m_mla_w_uq, m_mla_w_ukv, m_mla_w_out, m_ret_w_in, m_ret_gn_g, m_ret_w_out, v_ln1_g, v_ln1_b, v_ln2_g, v_ln2_b, v_ffn_w_up, v_ffn_conv_w, v_ffn_conv_b, v_ffn_w_down, v_lru_w_in, v_lru_conv_w, v_lru_conv_b, v_lru_w_a, v_lru_b_a, v_lru_w_x, v_lru_b_x, v_lru_lambda, v_lru_w_out, v_mla_w_in, v_mla_q_norm, v_mla_kv_norm, v_mla_w_uq, v_mla_w_ukv, v_mla_w_out, v_ret_w_in, v_ret_gn_g, v_ret_w_out):
    w = dict(ln1_g=ln1_g, ln1_b=ln1_b, ln2_g=ln2_g, ln2_b=ln2_b, ffn_w_up=ffn_w_up, ffn_conv_w=ffn_conv_w, ffn_conv_b=ffn_conv_b, ffn_w_down=ffn_w_down, lru_w_in=lru_w_in, lru_conv_w=lru_conv_w, lru_conv_b=lru_conv_b, lru_w_a=lru_w_a, lru_b_a=lru_b_a, lru_w_x=lru_w_x, lru_b_x=lru_b_x, lru_lambda=lru_lambda, lru_w_out=lru_w_out, mla_w_in=mla_w_in, mla_q_norm=mla_q_norm, mla_kv_norm=mla_kv_norm, mla_w_uq=mla_w_uq, mla_w_ukv=mla_w_ukv, mla_w_out=mla_w_out, ret_w_in=ret_w_in, ret_gn_g=ret_gn_g, ret_w_out=ret_w_out)
    m = dict(ln1_g=m_ln1_g, ln1_b=m_ln1_b, ln2_g=m_ln2_g, ln2_b=m_ln2_b, ffn_w_up=m_ffn_w_up, ffn_conv_w=m_ffn_conv_w, ffn_conv_b=m_ffn_conv_b, ffn_w_down=m_ffn_w_down, lru_w_in=m_lru_w_in, lru_conv_w=m_lru_conv_w, lru_conv_b=m_lru_conv_b, lru_w_a=m_lru_w_a, lru_b_a=m_lru_b_a, lru_w_x=m_lru_w_x, lru_b_x=m_lru_b_x, lru_lambda=m_lru_lambda, lru_w_out=m_lru_w_out, mla_w_in=m_mla_w_in, mla_q_norm=m_mla_q_norm, mla_kv_norm=m_mla_kv_norm, mla_w_uq=m_mla_w_uq, mla_w_ukv=m_mla_w_ukv, mla_w_out=m_mla_w_out, ret_w_in=m_ret_w_in, ret_gn_g=m_ret_gn_g, ret_w_out=m_ret_w_out)
    v = dict(ln1_g=v_ln1_g, ln1_b=v_ln1_b, ln2_g=v_ln2_g, ln2_b=v_ln2_b, ffn_w_up=v_ffn_w_up, ffn_conv_w=v_ffn_conv_w, ffn_conv_b=v_ffn_conv_b, ffn_w_down=v_ffn_w_down, lru_w_in=v_lru_w_in, lru_conv_w=v_lru_conv_w, lru_conv_b=v_lru_conv_b, lru_w_a=v_lru_w_a, lru_b_a=v_lru_b_a, lru_w_x=v_lru_w_x, lru_b_x=v_lru_b_x, lru_lambda=v_lru_lambda, lru_w_out=v_lru_w_out, mla_w_in=v_mla_w_in, mla_q_norm=v_mla_q_norm, mla_kv_norm=v_mla_kv_norm, mla_w_uq=v_mla_w_uq, mla_w_ukv=v_mla_w_ukv, mla_w_out=v_mla_w_out, ret_w_in=v_ret_w_in, ret_gn_g=v_ret_gn_g, ret_w_out=v_ret_w_out)
    D = x.shape[-1]
    axis_of = {n: ax for n, ax, _ in SHARDED}
    big = list(BIG_AXIS)
    small_mx = [n for n, _, mx in SHARDED if mx and n not in BIG_AXIS]
    small_vec = [n for n, _, mx in SHARDED if not mx]
    small = small_mx + small_vec

    gathered = _gather_chips([(w[n].astype(BF16), BIG_AXIS[n]) for n in big]
                             + [(_pack([w[n] for n in small_mx], BF16), None), (_pack([w[n] for n in small_vec], F32), None)],
                             "gather_weights")
    full = dict(zip(big, gathered))
    for names, buf in ((small_mx, gathered[-2]), (small_vec, gathered[-1])):
        blocks = _unpack(buf, [w[n].shape for n in names], lead=(N_CHIPS,))
        for n, blk in zip(names, blocks):
            full[n] = jnp.concatenate([blk[s] for s in range(N_CHIPS)], axis=axis_of[n])
    rep = {n: w[n] for n in REPLICATED}

    part, grad_x, grads, gbuf = _local_step(x[0], positions[0], loss_target[0], full, rep)
    loss = lax.psum((0.5 / D) * jnp.sum(part), MESH_AXES)

    g_pack = _pack([jnp.stack(jnp.split(grads[n], N_CHIPS, axis=axis_of[n])) for n in small], F32, lead=(N_CHIPS,))
    recv = _scatter_chips([(gbuf[n][1], BIG_AXIS[n]) for n in big] + [(g_pack, None)], "scatter_grads")
    sums = [_sum_partials(gbuf[n][0], r, BIG_AXIS[n], name="sum_" + n) for n, r in zip(big, recv)]
    sums.append(_sum_partials(g_pack[:, None], recv[-1][:, None], None, name="sum_small"))
    sibs = _swap_sibling(sums, "swap_core_partials")
    res = {kind: {} for kind in "gdmv"}
    for n, p_mine, p_sib in zip(big, sums, sibs):
        for kind, o in zip("gdmv", _adamw(w[n], m[n], v[n], p_mine, p_sib, name="adamw_" + n)):
            res[kind][n] = o
    spack = lambda d: _pack([d[n] for n in small], F32)[None]
    shapes = [w[n].shape for n in small]
    for kind, o in zip("gdmv", _adamw(spack(w), spack(m), spack(v), sums[-1], sibs[-1], name="adamw_small")):
        res[kind].update(zip(small, _unpack(o[0], shapes)))

    r_shapes = [w[n].shape for n in REPLICATED]
    rpack = lambda d: _pack([d[n] for n in REPLICATED], F32, rows=SUBLANES)
    r_sum = _allreduce_small(rpack(grads), "allreduce_replicated")
    r_outs = _adamw(rpack(w)[None], rpack(m)[None], rpack(v)[None], r_sum[None], None, name="adamw_replicated")
    for kind, o in zip("gdmv", r_outs):
        res[kind].update(zip(REPLICATED, _unpack(o[0], r_shapes)))

    return (loss, grad_x[None], *[res["g"][n] for n in WEIGHTS], *[res["d"][n] for n in WEIGHTS],
            *[res["m"][n] for n in WEIGHTS], *[res["v"][n] for n in WEIGHTS])
```

```python
import functools
import math

import numpy as np
import jax
import jax.numpy as jnp
from jax import lax
from jax.experimental import pallas as pl
from jax.experimental.pallas import tpu as pltpu

F32 = jnp.float32
BF16 = jnp.bfloat16

DEPTH = 4
ALPHA = (2.0 * DEPTH) ** 0.25
LN_EPS = 1e-5
RMS_EPS = 1e-6
ROPE_BASE = 10000.0
CHUNK = 64
LRU_C = 8.0
LRU_GROUPS = 4
MLA_HEADS = 16
QK_NOPE, QK_ROPE, V_HEAD = 64, 32, 64
Q_LORA, KV_LORA = 768, 256
RET_HEADS = 4
ADAM_LR, ADAM_B1, ADAM_B2, ADAM_EPS, ADAM_WD, ADAM_STEP = 0.001, 0.9, 0.999, 1e-08, 0.01, 10

LANES = 128
SUBLANES = 8
VMEM_LIMIT = 56 * 1024 * 1024

MESH_AXES = ("x", "y", "c")
N_CHIPS = 4


def _cp(sem):
    return pltpu.CompilerParams(dimension_semantics=sem, vmem_limit_bytes=VMEM_LIMIT)


def _sigmoid(x):
    return 1.0 / (1.0 + jnp.exp(-x))


_GELU_C = math.sqrt(2.0 / math.pi)


def _gelu_parts(x):
    x2 = x * x
    u = _GELU_C * (x + 0.044715 * x * x2)
    t = jnp.tanh(u)
    g = 0.5 * x * (1.0 + t)
    dg = 0.5 * (1.0 + t) + 0.5 * x * (1.0 - t * t) * _GELU_C * (1.0 + 3.0 * 0.044715 * x2)
    return g, dg


def _fold8(v):
    n = v.shape[0] // SUBLANES
    return v.reshape(n, SUBLANES, v.shape[1]).sum(axis=0)


def _dot(a, b):
    return jnp.dot(a, b, preferred_element_type=F32)


def _dot_tb(a, b):
    return lax.dot_general(a, b, (((1,), (1,)), ((), ())), preferred_element_type=F32)


def _dot_ta(a, b):
    return lax.dot_general(a, b, (((0,), (0,)), ((), ())), preferred_element_type=F32)


def _mm(a, b, *, out_dtype, tm, tn, name, a_koff=0):
    M = a.shape[0]
    K, N = b.shape

    def body(a_ref, b_ref, o_ref):
        o_ref[...] = _dot(a_ref[...].astype(BF16), b_ref[...].astype(BF16)).astype(out_dtype)

    return pl.pallas_call(
        body, name=name, grid=(M // tm, N // tn),
        in_specs=[pl.BlockSpec((tm, K), lambda i, j: (i, a_koff)),
                  pl.BlockSpec((K, tn), lambda i, j: (0, j))],
        out_specs=pl.BlockSpec((tm, tn), lambda i, j: (i, j)),
        out_shape=jax.ShapeDtypeStruct((M, N), out_dtype),
        compiler_params=_cp(("parallel", "parallel")),
    )(a, b)


def _mm_tb(pairs, b, *, out_dtype, tm, tk, name):
    M = pairs[0][0].shape[0]
    Kout = b.shape[0]
    n = len(pairs)

    def body(*refs):
        a_refs, b_refs, o_ref = refs[:n], refs[n:2 * n], refs[2 * n]
        acc = None
        for a_ref, b_ref in zip(a_refs, b_refs):
            t = _dot_tb(a_ref[...].astype(BF16), b_ref[...].astype(BF16))
            acc = t if acc is None else acc + t
        o_ref[...] = acc.astype(out_dtype)

    in_specs = [pl.BlockSpec((tm, a.shape[1]), lambda i, j: (i, 0)) for a, _ in pairs]
    for a, c0 in pairs:
        w = a.shape[1]
        assert c0 % w == 0
        in_specs.append(pl.BlockSpec((tk, w), functools.partial(lambda i, j, cb: (j, cb), cb=c0 // w)))
    return pl.pallas_call(
        body, name=name, grid=(M // tm, Kout // tk),
        in_specs=in_specs,
        out_specs=pl.BlockSpec((tm, tk), lambda i, j: (i, j)),
        out_shape=jax.ShapeDtypeStruct((M, Kout), out_dtype),
        compiler_params=_cp(("parallel", "parallel")),
    )(*[a for a, _ in pairs], *[b for _ in pairs])


def _mm_ta(a, b, *, tk, tn, tm, name, a_c0=0, a_w=None, b_c0=0, b_w=None, dest=None):
    M = a.shape[0]
    nm = M // tm
    a_w = a.shape[1] if a_w is None else a_w
    b_w = b.shape[1] if b_w is None else b_w
    assert a_c0 % tk == 0 and b_c0 % tn == 0 and a_w % tk == 0 and b_w % tn == 0

    def body(*refs):
        a_ref, b_ref = refs[0], refs[1]
        o_ref = refs[-1] if dest is None else refs[-2]

        @pl.when(pl.program_id(2) == 0)
        def _():
            o_ref[...] = jnp.zeros_like(o_ref)
        o_ref[...] += _dot_ta(a_ref[...].astype(BF16), b_ref[...].astype(BF16))

        if dest is not None:
            @pl.when(pl.program_id(2) == nm - 1)
            def _():
                refs[-1][...] = o_ref[...].astype(BF16)

    in_specs = [pl.BlockSpec((tm, tk), lambda i, j, m: (m, i + a_c0 // tk)),
                pl.BlockSpec((tm, tn), lambda i, j, m: (m, j + b_c0 // tn))]
    args = [a, b]
    if dest is None:
        out_spec = pl.BlockSpec((tk, tn), lambda i, j, m: (i, j))
        out_shape = jax.ShapeDtypeStruct((a_w, b_w), F32)
        aliases = {}
    else:
        bufs, full_shape, layer, r0, c0 = dest
        assert r0 % tk == 0 and c0 % tn == 0
        spec = pl.BlockSpec((None, tk, tn), lambda i, j, m: (layer, i + r0 // tk, j + c0 // tn))
        out_spec = [spec, spec]
        out_shape = [jax.ShapeDtypeStruct(full_shape, F32), jax.ShapeDtypeStruct(full_shape, BF16)]
        aliases = {}
        if bufs is not None:
            in_specs += [pl.BlockSpec(memory_space=pl.ANY)] * 2
            args += list(bufs)
            aliases = {2: 0, 3: 1}
    return pl.pallas_call(
        body, name=name, grid=(a_w // tk, b_w // tn, nm),
        in_specs=in_specs, out_specs=out_spec, out_shape=out_shape, input_output_aliases=aliases,
        compiler_params=_cp(("parallel", "parallel", "arbitrary")),
    )(*args)


def _grad_into(gbuf, key, full_shape, layer, r0, c0, a, b, **kw):
    gbuf[key] = tuple(_mm_ta(a, b, dest=(gbuf.get(key), full_shape, layer, r0, c0), **kw))


def _ln_fwd(x, mix, g, b, *, ts, name):
    S, D = x.shape

    def body(x_ref, m_ref, g_ref, b_ref, o_ref, z_ref):
        z = ALPHA * x_ref[...] + m_ref[...]
        mu = jnp.mean(z, axis=-1, keepdims=True)
        zc = z - mu
        var = jnp.mean(zc * zc, axis=-1, keepdims=True)
        o_ref[...] = zc * lax.rsqrt(var + LN_EPS) * g_ref[...] + b_ref[...]
        z_ref[...] = z

    row = pl.BlockSpec((ts, D), lambda i: (i, 0))
    vec = pl.BlockSpec((1, D), lambda i: (0, 0))
    return pl.pallas_call(
        body, name=name, grid=(S // ts,),
        in_specs=[row, row, vec, vec], out_specs=[row, row],
        out_shape=[jax.ShapeDtypeStruct((S, D), F32)] * 2,
        compiler_params=_cp(("parallel",)),
    )(x, mix, g.reshape(1, D), b.reshape(1, D))


def _ln_bwd(da, db, z, g, *, ts, name):
    S, D = z.shape
    two = db is not None

    def body(*refs):
        if two:
            da_ref, db_ref, z_ref, g_ref, dz_ref, dg_ref, dbias_ref = refs
            dout = ALPHA * da_ref[...] + db_ref[...]
        else:
            da_ref, z_ref, g_ref, dz_ref, dg_ref, dbias_ref = refs
            dout = da_ref[...]

        @pl.when(pl.program_id(0) == 0)
        def _():
            dg_ref[...] = jnp.zeros_like(dg_ref)
            dbias_ref[...] = jnp.zeros_like(dbias_ref)

        z = z_ref[...]
        mu = jnp.mean(z, axis=-1, keepdims=True)
        zc = z - mu
        var = jnp.mean(zc * zc, axis=-1, keepdims=True)
        rstd = lax.rsqrt(var + LN_EPS)
        xhat = zc * rstd
        dxh = dout * g_ref[...]
        m1 = jnp.mean(dxh, axis=-1, keepdims=True)
        m2 = jnp.mean(dxh * xhat, axis=-1, keepdims=True)
        dz_ref[...] = rstd * (dxh - m1 - xhat * m2)
        dg_ref[...] += _fold8(dout * xhat)
        dbias_ref[...] += _fold8(dout)

    row = pl.BlockSpec((ts, D), lambda i: (i, 0))
    vec = pl.BlockSpec((1, D), lambda i: (0, 0))
    acc = pl.BlockSpec((SUBLANES, D), lambda i: (0, 0))
    args = [da, db, z, g.reshape(1, D)] if two else [da, z, g.reshape(1, D)]
    return pl.pallas_call(
        body, name=name, grid=(S // ts,),
        in_specs=[row] * (3 if two else 2) + [vec],
        out_specs=[row, acc, acc],
        out_shape=[jax.ShapeDtypeStruct((S, D), F32), jax.ShapeDtypeStruct((SUBLANES, D), F32),
                   jax.ShapeDtypeStruct((SUBLANES, D), F32)],
        compiler_params=_cp(("arbitrary",)),
    )(*args)


def _loss_head(y, t, *, ts, name):
    S, D = y.shape

    def body(y_ref, t_ref, dy_ref, p_ref):
        @pl.when(pl.program_id(0) == 0)
        def _():
            p_ref[...] = jnp.zeros_like(p_ref)
        d = y_ref[...] - t_ref[...]
        dy_ref[...] = d * (1.0 / D)
        p_ref[...] += _fold8(d * d)

    row = pl.BlockSpec((ts, D), lambda i: (i, 0))
    acc = pl.BlockSpec((SUBLANES, D), lambda i: (0, 0))
    return pl.pallas_call(
        body, name=name, grid=(S // ts,),
        in_specs=[row, row], out_specs=[row, acc],
        out_shape=[jax.ShapeDtypeStruct((S, D), F32), jax.ShapeDtypeStruct((SUBLANES, D), F32)],
        compiler_params=_cp(("arbitrary",)),
    )(y, t)


def _axpy(a, b, *, ts, name):
    S, D = a.shape

    def body(a_ref, b_ref, o_ref):
        o_ref[...] = ALPHA * a_ref[...] + b_ref[...]

    row = pl.BlockSpec((ts, D), lambda i: (i, 0))
    return pl.pallas_call(
        body, name=name, grid=(S // ts,), in_specs=[row, row], out_specs=row,
        out_shape=jax.ShapeDtypeStruct((S, D), F32), compiler_params=_cp(("parallel",)),
    )(a, b)


def _prev_halo_spec(ts, tc, coff):
    r = ts // SUBLANES
    return pl.BlockSpec((SUBLANES, tc), lambda i, j: (jnp.maximum(i * r - 1, 0), j + coff))


def _fill_prev(buf, halo_ref, cur, i):
    buf[0:SUBLANES, :] = jnp.where(i > 0, halo_ref[...], 0.0)
    buf[SUBLANES:, :] = cur


def _conv_fwd(x, w, b, *, K, ts, tc, x_c0, name):
    S = x.shape[0]
    C = w.shape[1]
    coff = x_c0 // tc

    def body(x_ref, halo_ref, w_ref, b_ref, o_ref, buf):
        _fill_prev(buf, halo_ref, x_ref[...], pl.program_id(0))
        acc = b_ref[...] + w_ref[K - 1:K, :] * x_ref[...]
        for k in range(K - 1):
            acc = acc + w_ref[k:k + 1, :] * buf[pl.ds(SUBLANES - (K - 1) + k, ts), :]
        o_ref[...] = acc

    return pl.pallas_call(
        body, name=name, grid=(S // ts, C // tc),
        in_specs=[pl.BlockSpec((ts, tc), lambda i, j: (i, j + coff)), _prev_halo_spec(ts, tc, coff),
                  pl.BlockSpec((K, tc), lambda i, j: (0, j)), pl.BlockSpec((1, tc), lambda i, j: (0, j))],
        out_specs=pl.BlockSpec((ts, tc), lambda i, j: (i, j)),
        out_shape=jax.ShapeDtypeStruct((S, C), F32),
        scratch_shapes=[pltpu.VMEM((ts + SUBLANES, tc), F32)],
        compiler_params=_cp(("parallel", "parallel")),
    )(x, x, w, b.reshape(1, C))


def _conv_wgrad(dy, x, *, K, ts, tc, x_c0, name):
    S, C = dy.shape
    coff = x_c0 // tc

    def body(dy_ref, x_ref, halo_ref, dw_ref, db_ref, buf):
        i = pl.program_id(1)

        @pl.when(i == 0)
        def _():
            dw_ref[...] = jnp.zeros_like(dw_ref)
            db_ref[...] = jnp.zeros_like(db_ref)

        _fill_prev(buf, halo_ref, x_ref[...], i)
        dy_v = dy_ref[...]
        db_ref[...] += _fold8(dy_v)
        for k in range(K):
            xs = buf[pl.ds(SUBLANES - (K - 1) + k, ts), :]
            dw_ref[k] += _fold8(dy_v * xs)

    r = ts // SUBLANES
    return pl.pallas_call(
        body, name=name, grid=(C // tc, S // ts),
        in_specs=[pl.BlockSpec((ts, tc), lambda j, i: (i, j)),
                  pl.BlockSpec((ts, tc), lambda j, i: (i, j + coff)),
                  pl.BlockSpec((SUBLANES, tc), lambda j, i: (jnp.maximum(i * r - 1, 0), j + coff))],
        out_specs=[pl.BlockSpec((K, SUBLANES, tc), lambda j, i: (0, 0, j)),
                   pl.BlockSpec((SUBLANES, tc), lambda j, i: (0, j))],
        out_shape=[jax.ShapeDtypeStruct((K, SUBLANES, C), F32), jax.ShapeDtypeStruct((SUBLANES, C), F32)],
        scratch_shapes=[pltpu.VMEM((ts + SUBLANES, tc), F32)],
        compiler_params=_cp(("parallel", "arbitrary")),
    )(dy, x, x)


def _conv_bwd(dy, w, *, K, ts, tc, w_c0, out_dtype, name):
    S, C = dy.shape
    nb = S // ts
    r = ts // SUBLANES
    woff = w_c0 // tc

    def body(dy_ref, halo_ref, w_ref, o_ref, buf):
        i = pl.program_id(0)
        buf[0:ts, :] = dy_ref[...]
        buf[ts:, :] = jnp.where(i < nb - 1, halo_ref[...], 0.0)
        acc = w_ref[K - 1:K, :] * dy_ref[...]
        for k in range(K - 1):
            acc = acc + w_ref[k:k + 1, :] * buf[pl.ds(K - 1 - k, ts), :]
        o_ref[...] = acc.astype(out_dtype)

    return pl.pallas_call(
        body, name=name, grid=(nb, C // tc),
        in_specs=[pl.BlockSpec((ts, tc), lambda i, j: (i, j)),
                  pl.BlockSpec((SUBLANES, tc), lambda i, j: (jnp.minimum((i + 1) * r, nb * r - 1), j)),
                  pl.BlockSpec((K, tc), lambda i, j: (0, j + woff))],
        out_specs=pl.BlockSpec((ts, tc), lambda i, j: (i, j)),
        out_shape=jax.ShapeDtypeStruct((S, C), out_dtype),
        scratch_shapes=[pltpu.VMEM((ts + SUBLANES, tc), F32)],
        compiler_params=_cp(("parallel", "parallel")),
    )(dy, dy, w)


HALO16 = 16
FFN_UNROLL = 4


def _ffn_taps_w(w_ref, cs):
    return [w_ref[k:k + 1, cs] for k in range(3)]


def _ffn_taps8(prev, cur):
    row = lax.broadcasted_iota(jnp.int32, cur.shape, 0)
    return (jnp.where(row < 2, pltpu.roll(prev, 2, 0), pltpu.roll(cur, 2, 0)),
            jnp.where(row < 1, pltpu.roll(prev, 1, 0), pltpu.roll(cur, 1, 0)), cur)


def _ffn_conv8(taps, w, b):
    return b + w[0] * taps[0] + w[1] * taps[1] + w[2] * taps[2]


def _ffn_conv8_t(dh, dh_next, w):
    row = lax.broadcasted_iota(jnp.int32, dh.shape, 0)
    s1 = jnp.where(row < SUBLANES - 1, pltpu.roll(dh, SUBLANES - 1, 0), pltpu.roll(dh_next, SUBLANES - 1, 0))
    s2 = jnp.where(row < SUBLANES - 2, pltpu.roll(dh, SUBLANES - 2, 0), pltpu.roll(dh_next, SUBLANES - 2, 0))
    return w[2] * dh + w[1] * s1 + w[0] * s2


def _ffn_mid_specs(ts, tc, nf, nb, with_next):
    r = ts // HALO16
    specs = []
    for off in (0, nf):
        specs.append(pl.BlockSpec((ts, tc), functools.partial(lambda j, i, o: (i, j + o), o=off)))
        specs.append(pl.BlockSpec((HALO16, tc), functools.partial(lambda j, i, o: (jnp.maximum(i * r - 1, 0), j + o), o=off)))
        if with_next:
            specs.append(pl.BlockSpec(
                (HALO16, tc), functools.partial(lambda j, i, o: (jnp.minimum((i + 1) * r, nb * r - 1), j + o), o=off)))
    for rows in (3, 1):
        for off in (0, nf):
            specs.append(pl.BlockSpec((rows, tc), functools.partial(lambda j, i, o: (0, j + o), o=off)))
    return specs


def _ffn_mid_fwd(hpre, w, b, *, ts, tc, name):
    S, F2 = hpre.shape
    F = F2 // 2
    nf = F // tc

    def body(g_ref, gp_ref, u_ref, up_ref, wg_ref, wu_ref, bg_ref, bu_ref, o_ref, gbuf, ubuf, obuf):
        first = pl.program_id(1) == 0
        for buf, prev, cur in ((gbuf, gp_ref, g_ref), (ubuf, up_ref, u_ref)):
            buf[0:HALO16, :] = jnp.where(first, 0.0, prev[...].astype(F32))
            buf[HALO16:, :] = cur[...].astype(F32)
        for lt in range(tc // LANES):
            cs = slice(lt * LANES, (lt + 1) * LANES)
            wg, wu = _ffn_taps_w(wg_ref, cs), _ffn_taps_w(wu_ref, cs)
            bg, bu = bg_ref[:, cs], bu_ref[:, cs]

            def step(c, carry):
                a_g, a_u = carry
                for un in range(FFN_UNROLL):
                    r0 = pl.multiple_of(c * (FFN_UNROLL * SUBLANES), SUBLANES) + un * SUBLANES
                    b_g = gbuf[pl.ds(HALO16 + r0, SUBLANES), cs]
                    b_u = ubuf[pl.ds(HALO16 + r0, SUBLANES), cs]
                    gel, _ = _gelu_parts(_ffn_conv8(_ffn_taps8(a_g, b_g), wg, bg))
                    obuf[pl.ds(r0, SUBLANES), cs] = gel * _ffn_conv8(_ffn_taps8(a_u, b_u), wu, bu)
                    a_g, a_u = b_g, b_u
                return a_g, a_u

            lax.fori_loop(0, ts // (FFN_UNROLL * SUBLANES), step,
                          (gbuf[HALO16 - SUBLANES:HALO16, cs], ubuf[HALO16 - SUBLANES:HALO16, cs]))
        o_ref[...] = obuf[...].astype(BF16)

    b2 = b.reshape(1, F2)
    return pl.pallas_call(
        body, name=name, grid=(nf, S // ts),
        in_specs=_ffn_mid_specs(ts, tc, nf, S // ts, False),
        out_specs=pl.BlockSpec((ts, tc), lambda j, i: (i, j)),
        out_shape=jax.ShapeDtypeStruct((S, F), BF16),
        scratch_shapes=[pltpu.VMEM((ts + HALO16, tc), F32)] * 2 + [pltpu.VMEM((ts, tc), F32)],
        compiler_params=_cp(("parallel", "parallel")),
    )(hpre, hpre, hpre, hpre, w, w, b2, b2)


def _ffn_mid_bwd(hpre, da, w, b, *, ts, tc, name):
    S, F2 = hpre.shape
    F = F2 // 2
    nf = F // tc
    nb = S // ts
    r = ts // HALO16
    nch = ts // SUBLANES

    def body(g_ref, gp_ref, gn_ref, u_ref, up_ref, un_ref, wg_ref, wu_ref, bg_ref, bu_ref, da_ref, dan_ref,
             dpg_ref, dpu_ref, dwg_ref, dwu_ref, dbg_ref, dbu_ref, gbuf, ubuf, dabuf, pgbuf, pubuf):
        i = pl.program_id(1)

        @pl.when(i == 0)
        def _():
            for ref in (dwg_ref, dwu_ref, dbg_ref, dbu_ref):
                ref[...] = jnp.zeros_like(ref)

        for buf, prev, cur, nxt in ((gbuf, gp_ref, g_ref, gn_ref), (ubuf, up_ref, u_ref, un_ref)):
            buf[0:HALO16, :] = jnp.where(i == 0, 0.0, prev[...].astype(F32))
            buf[HALO16:HALO16 + ts, :] = cur[...].astype(F32)
            buf[HALO16 + ts:, :] = nxt[...].astype(F32)
        dabuf[0:ts, :] = da_ref[...].astype(F32)
        dabuf[ts:, :] = jnp.where(i == nb - 1, 0.0, dan_ref[...].astype(F32))

        for lt in range(tc // LANES):
            cs = slice(lt * LANES, (lt + 1) * LANES)
            wg, wu = _ffn_taps_w(wg_ref, cs), _ffn_taps_w(wu_ref, cs)
            bg, bu = bg_ref[:, cs], bu_ref[:, cs]

            def piece(r0, a_g, a_u):
                b_g = gbuf[pl.ds(HALO16 + r0, SUBLANES), cs]
                b_u = ubuf[pl.ds(HALO16 + r0, SUBLANES), cs]
                tg, tu = _ffn_taps8(a_g, b_g), _ffn_taps8(a_u, b_u)
                gel, dgel = _gelu_parts(_ffn_conv8(tg, wg, bg))
                da_v = dabuf[pl.ds(r0, SUBLANES), cs]
                return da_v * _ffn_conv8(tu, wu, bu) * dgel, da_v * gel, tg, tu, b_g, b_u

            def step(c, carry):
                a_g, a_u, pdg, pdu, acc = carry
                for un in range(FFN_UNROLL):
                    r0 = pl.multiple_of(c * (FFN_UNROLL * SUBLANES), SUBLANES) + un * SUBLANES
                    dg, du, tg, tu, a_g, a_u = piece(r0, a_g, a_u)
                    pgbuf[pl.ds(r0, SUBLANES), cs] = _ffn_conv8_t(pdg, dg, wg)
                    pubuf[pl.ds(r0, SUBLANES), cs] = _ffn_conv8_t(pdu, du, wu)
                    acc = (tuple(a + dg * t for a, t in zip(acc[0], tg)), tuple(a + du * t for a, t in zip(acc[1], tu)),
                           acc[2] + dg, acc[3] + du)
                    pdg, pdu = dg, du
                return a_g, a_u, pdg, pdu, acc

            zero = jnp.zeros((SUBLANES, LANES), F32)
            a_g, a_u, pdg, pdu, acc = lax.fori_loop(
                0, nch // FFN_UNROLL, step,
                (gbuf[HALO16 - SUBLANES:HALO16, cs], ubuf[HALO16 - SUBLANES:HALO16, cs], zero, zero,
                 ((zero,) * 3, (zero,) * 3, zero, zero)))
            dg, du, _, _, _, _ = piece(ts, a_g, a_u)
            pgbuf[ts:ts + SUBLANES, cs] = _ffn_conv8_t(pdg, dg, wg)
            pubuf[ts:ts + SUBLANES, cs] = _ffn_conv8_t(pdu, du, wu)
            for k in range(3):
                dwg_ref[k, :, cs] += acc[0][k]
                dwu_ref[k, :, cs] += acc[1][k]
            dbg_ref[:, cs] += acc[2]
            dbu_ref[:, cs] += acc[3]
        dpg_ref[...] = pgbuf[SUBLANES:, :].astype(BF16)
        dpu_ref[...] = pubuf[SUBLANES:, :].astype(BF16)

    b2 = b.reshape(1, F2)
    blk = pl.BlockSpec((ts, tc), lambda j, i: (i, j))
    nxt = pl.BlockSpec((HALO16, tc), lambda j, i: (jnp.minimum((i + 1) * r, nb * r - 1), j))
    in_specs = _ffn_mid_specs(ts, tc, nf, nb, True) + [blk, nxt]
    out_specs = [blk, blk,
                 pl.BlockSpec((3, SUBLANES, tc), lambda j, i: (0, 0, j)), pl.BlockSpec((3, SUBLANES, tc), lambda j, i: (0, 0, j)),
                 pl.BlockSpec((SUBLANES, tc), lambda j, i: (0, j)), pl.BlockSpec((SUBLANES, tc), lambda j, i: (0, j))]
    return pl.pallas_call(
        body, name=name, grid=(nf, nb),
        in_specs=in_specs, out_specs=out_specs,
        out_shape=[jax.ShapeDtypeStruct((S, F), BF16)] * 2 + [jax.ShapeDtypeStruct((3, SUBLANES, F), F32)] * 2
                  + [jax.ShapeDtypeStruct((SUBLANES, F), F32)] * 2,
        scratch_shapes=[pltpu.VMEM((ts + 2 * HALO16, tc), F32)] * 2 + [pltpu.VMEM((ts + HALO16, tc), F32)]
                       + [pltpu.VMEM((ts + SUBLANES, tc), F32)] * 2,
        compiler_params=_cp(("parallel", "arbitrary")),
    )(hpre, hpre, hpre, hpre, hpre, hpre, w, w, b2, b2, da, da)


def _expm1(x):
    u = jnp.exp(x)
    um1 = u - 1.0
    safe = jnp.where(um1 == 0.0, 1.0, jnp.log(u))
    r = jnp.where(um1 == 0.0, x, um1 * x / safe)
    return jnp.where(x < -30.0, -1.0, r)


def _softplus(z):
    return jnp.maximum(z, 0.0) + jnp.log1p(jnp.exp(-jnp.abs(z)))


def _lru_gates(u, wa_ref, ba_ref, wx_ref, bx_ref, lam_ref):
    ub = u.astype(BF16)
    r = _sigmoid(_dot(ub, wa_ref[...]) + ba_ref[...])
    ig = _sigmoid(_dot(ub, wx_ref[...]) + bx_ref[...])
    sp = _softplus(-lam_ref[...])
    la = -LRU_C * r * sp
    a = jnp.exp(la)
    mult = jnp.sqrt(-_expm1(2.0 * la))
    return ub, r, ig, sp, a, mult


def _lru_fwd(u, proj, w_a, b_a, w_x, b_x, lam, *, ts, name):
    S, W = u.shape
    G = LRU_GROUPS
    gw = W // G
    nt = ts // SUBLANES

    def body(u_ref, gb_ref, wa_ref, ba_ref, wx_ref, bx_ref, lam_ref, y_ref, h_ref, a_buf, b_buf, carry):
        @pl.when(pl.program_id(0) == 0)
        def _():
            carry[...] = jnp.zeros_like(carry)

        for g in range(G):
            gs = slice(g * gw, (g + 1) * gw)
            u_v = u_ref[:, gs]
            _, _, ig, _, a, mult = _lru_gates(u_v, wa_ref.at[g], ba_ref.at[g], wx_ref.at[g], bx_ref.at[g], lam_ref.at[g])
            a_buf[:, gs] = a
            b_buf[:, gs] = mult * ig * u_v
        row = lax.broadcasted_iota(jnp.int32, (SUBLANES, W), 0)

        def tile(k, c):
            r0 = pl.multiple_of(k * SUBLANES, SUBLANES)
            A = a_buf[pl.ds(r0, SUBLANES), :]
            B = b_buf[pl.ds(r0, SUBLANES), :]
            for d in (1, 2, 4):
                m = row >= d
                B = jnp.where(m, A * pltpu.roll(B, d, 0) + B, B)
                A = jnp.where(m, A * pltpu.roll(A, d, 0), A)
            h = A * c + B
            h_ref[pl.ds(r0, SUBLANES), :] = h
            return h[SUBLANES - 1:SUBLANES, :]

        carry[...] = lax.fori_loop(0, nt, tile, carry[...])
        gel, _ = _gelu_parts(gb_ref[...])
        y_ref[...] = (gel * h_ref[...]).astype(BF16)

    blk = pl.BlockSpec((ts, W), lambda i: (i, 0))
    wsp = pl.BlockSpec((G, gw, gw), lambda i: (0, 0, 0))
    vsp = pl.BlockSpec((G, 1, gw), lambda i: (0, 0, 0))
    return pl.pallas_call(
        body, name=name, grid=(S // ts,),
        in_specs=[blk, blk, wsp, vsp, wsp, vsp, vsp],
        out_specs=[blk, blk],
        out_shape=[jax.ShapeDtypeStruct((S, W), BF16), jax.ShapeDtypeStruct((S, W), F32)],
        scratch_shapes=[pltpu.VMEM((ts, W), F32), pltpu.VMEM((ts, W), F32), pltpu.VMEM((1, W), F32)],
        compiler_params=_cp(("arbitrary",)),
    )(u, proj, w_a, b_a.reshape(G, 1, gw), w_x, b_x.reshape(G, 1, gw), lam.reshape(G, 1, gw))


def _lru_bwd(dy, u, proj, h, w_a, b_a, w_x, b_x, lam, *, ts, name):
    S, W = u.shape
    G = LRU_GROUPS
    gw = W // G
    nt = ts // SUBLANES
    nb = S // ts
    r8 = ts // SUBLANES

    def body(dy_ref, u_ref, gb_ref, h_ref, hh_ref, wa_ref, ba_ref, wx_ref, bx_ref, lam_ref,
             dgb_ref, du_ref, dwa_ref, dwx_ref, dba_ref, dbx_ref, dlam_ref,
             a_buf, q_buf, p_buf, hbuf, carry):
        i = pl.program_id(0)
        ib = nb - 1 - i

        @pl.when(i == 0)
        def _():
            carry[...] = jnp.zeros_like(carry)
            dwa_ref[...] = jnp.zeros_like(dwa_ref)
            dwx_ref[...] = jnp.zeros_like(dwx_ref)
            dba_ref[...] = jnp.zeros_like(dba_ref)
            dbx_ref[...] = jnp.zeros_like(dbx_ref)
            dlam_ref[...] = jnp.zeros_like(dlam_ref)

        def gates(g):
            gs = slice(g * gw, (g + 1) * gw)
            return gs, _lru_gates(u_ref[:, gs], wa_ref.at[g], ba_ref.at[g], wx_ref.at[g], bx_ref.at[g], lam_ref.at[g])

        for g in range(G):
            gs, (_, _, _, _, a, _) = gates(g)
            gel, dgel = _gelu_parts(gb_ref[:, gs])
            dy_v = dy_ref[:, gs]
            dgb_ref[:, gs] = (dy_v * h_ref[:, gs] * dgel).astype(BF16)
            a_buf[:, gs] = a
            q_buf[:, gs] = a * (dy_v * gel)
        row = lax.broadcasted_iota(jnp.int32, (SUBLANES, W), 0)

        def tile(kk, c):
            r0 = pl.multiple_of((nt - 1 - kk) * SUBLANES, SUBLANES)
            A = a_buf[pl.ds(r0, SUBLANES), :]
            B = q_buf[pl.ds(r0, SUBLANES), :]
            for d in (1, 2, 4):
                m = row < SUBLANES - d
                B = jnp.where(m, A * pltpu.roll(B, SUBLANES - d, 0) + B, B)
                A = jnp.where(m, A * pltpu.roll(A, SUBLANES - d, 0), A)
            P = A * c + B
            p_buf[pl.ds(r0, SUBLANES), :] = jnp.where(row == SUBLANES - 1, c, pltpu.roll(P, SUBLANES - 1, 0))
            return P[0:1, :]

        carry[...] = lax.fori_loop(0, nt, tile, carry[...])
        hbuf[0:SUBLANES, :] = jnp.where(ib > 0, hh_ref[...], 0.0)
        hbuf[SUBLANES:, :] = h_ref[...]
        for g in range(G):
            gs, (ub, r, ig, sp, a, mult) = gates(g)
            u_v = u_ref[:, gs]
            gel, _ = _gelu_parts(gb_ref[:, gs])
            Gt = dy_ref[:, gs] * gel + p_buf[:, gs]
            hprev = hbuf[pl.ds(SUBLANES - 1, ts), gs]
            da = Gt * hprev
            dmult = Gt * (ig * u_v)
            dla = da * a - dmult * (a * a) / mult
            dr = dla * (-LRU_C * sp)
            dlam_ref[g] += _fold8(dla * (LRU_C * r)) * _sigmoid(-lam_ref[g])
            dig = Gt * mult * u_v
            dzr = dr * r * (1.0 - r)
            dzi = dig * ig * (1.0 - ig)
            dzr_b = dzr.astype(BF16)
            dzi_b = dzi.astype(BF16)
            du_ref[:, gs] = Gt * mult * ig + _dot_tb(dzr_b, wa_ref[g]) + _dot_tb(dzi_b, wx_ref[g])
            dwa_ref[g] += _dot_ta(ub, dzr_b)
            dwx_ref[g] += _dot_ta(ub, dzi_b)
            dba_ref[g] += _fold8(dzr)
            dbx_ref[g] += _fold8(dzi)

    rblk = pl.BlockSpec((ts, W), lambda i: (nb - 1 - i, 0))
    halo = pl.BlockSpec((SUBLANES, W), lambda i: (jnp.maximum((nb - 1 - i) * r8 - 1, 0), 0))
    wsp = pl.BlockSpec((G, gw, gw), lambda i: (0, 0, 0))
    vsp = pl.BlockSpec((G, 1, gw), lambda i: (0, 0, 0))
    acc8 = pl.BlockSpec((G, SUBLANES, gw), lambda i: (0, 0, 0))
    return pl.pallas_call(
        body, name=name, grid=(nb,),
        in_specs=[rblk, rblk, rblk, rblk, halo, wsp, vsp, wsp, vsp, vsp],
        out_specs=[rblk, rblk, wsp, wsp, acc8, acc8, acc8],
        out_shape=[jax.ShapeDtypeStruct((S, W), BF16), jax.ShapeDtypeStruct((S, W), F32),
                   jax.ShapeDtypeStruct((G, gw, gw), F32), jax.ShapeDtypeStruct((G, gw, gw), F32),
                   jax.ShapeDtypeStruct((G, SUBLANES, gw), F32), jax.ShapeDtypeStruct((G, SUBLANES, gw), F32),
                   jax.ShapeDtypeStruct((G, SUBLANES, gw), F32)],
        scratch_shapes=[pltpu.VMEM((ts, W), F32)] * 3 + [pltpu.VMEM((ts + SUBLANES, W), F32),
                                                         pltpu.VMEM((1, W), F32)],
        compiler_params=_cp(("arbitrary",)),
    )(dy, u, proj, h, h, w_a, b_a.reshape(G, 1, gw), w_x, b_x.reshape(G, 1, gw), lam.reshape(G, 1, gw))


def _rt(S, pref):
    return min(S, pref)


def _lru_mixer_fwd(x, p, tag):
    S, D = x.shape
    W = p["w_out"].shape[0]
    ts = _rt(S, 512)
    proj = _mm(x, p["w_in"], out_dtype=F32, tm=ts, tn=2 * W, name=tag + "_proj")
    u = _conv_fwd(proj, p["conv_w"], p["conv_b"], K=4, ts=ts, tc=512, x_c0=W, name=tag + "_conv")
    y, h = _lru_fwd(u, proj, p["w_a"], p["b_a"], p["w_x"], p["b_x"], p["lam"], ts=ts, name=tag + "_scan")
    mix = _mm(y, p["w_out"], out_dtype=F32, tm=ts, tn=D, name=tag + "_out")
    return mix, (proj, u, h, y)


def _lru_mixer_bwd(dmix, x, p, saved, tag, gbuf, j):
    proj, u, h, y = saved
    S, D = x.shape
    W = p["w_out"].shape[0]
    ts = _rt(S, 512)
    g = {}
    dy = _mm_tb([(dmix, 0)], p["w_out"], out_dtype=F32, tm=ts, tk=W, name=tag + "_dy")
    n_lru = (DEPTH + 2) // 3
    _grad_into(gbuf, "lru_w_out", (n_lru, W, D), j, 0, 0, y, dmix, tk=W, tn=D, tm=ts, name=tag + "_dwout")
    dgb, du, g["w_a"], g["w_x"], dba8, dbx8, dlam8 = _lru_bwd(
        dy, u, proj, h, p["w_a"], p["b_a"], p["w_x"], p["b_x"], p["lam"], ts=ts, name=tag + "_scanb")
    g["b_a"] = dba8.sum(axis=1)
    g["b_x"] = dbx8.sum(axis=1)
    g["lam"] = dlam8.sum(axis=1).reshape(-1)
    dcw8, dcb8 = _conv_wgrad(du, proj, K=4, ts=ts, tc=512, x_c0=W, name=tag + "_convw")
    g["conv_w"] = dcw8.sum(axis=1)
    g["conv_b"] = dcb8.sum(axis=0)
    drnn = _conv_bwd(du, p["conv_w"], K=4, ts=ts, tc=512, w_c0=0, out_dtype=BF16, name=tag + "_convb")
    dx = _mm_tb([(dgb, 0), (drnn, W)], p["w_in"], out_dtype=F32, tm=ts, tk=D, name=tag + "_dx")
    _grad_into(gbuf, "lru_w_in", (n_lru, D, 2 * W), j, 0, 0, x, dgb, tk=D, tn=W, tm=ts, name=tag + "_dwin_g")
    _grad_into(gbuf, "lru_w_in", (n_lru, D, 2 * W), j, 0, W, x, drnn, tk=D, tn=W, tm=ts, name=tag + "_dwin_r")
    return dx, g


def _ffn_fwd(x, p, tag):
    S, D = x.shape
    F = p["w_down"].shape[0]
    ts = _rt(S, 512)
    tc = F // 2
    hpre = _mm(x, p["w_up"], out_dtype=BF16, tm=ts, tn=2 * F, name=tag + "_up")
    a = _ffn_mid_fwd(hpre, p["conv_w"], p["conv_b"], ts=_rt(S, 256), tc=tc, name=tag + "_mid")
    f = _mm(a, p["w_down"], out_dtype=F32, tm=ts, tn=D, name=tag + "_down")
    return f, (hpre, a)


def _ffn_bwd(df, x, p, saved, tag, gbuf, i):
    hpre, a = saved
    S, D = x.shape
    F = p["w_down"].shape[0]
    ts = _rt(S, 512)
    tw = _rt(S, 1024)
    tc = F // 2
    g = {}
    da = _mm_tb([(df, 0)], p["w_down"], out_dtype=BF16, tm=ts, tk=F, name=tag + "_da")
    _grad_into(gbuf, "ffn_w_down", (DEPTH, F, D), i, 0, 0, a, df, tk=tc, tn=D, tm=tw, name=tag + "_dwdown")
    dpg, dpu, dwg8, dwu8, dbg8, dbu8 = _ffn_mid_bwd(hpre, da, p["conv_w"], p["conv_b"], ts=_rt(S, 256), tc=tc,
                                                    name=tag + "_midb")
    g["conv_w"] = jnp.concatenate([dwg8.sum(axis=1), dwu8.sum(axis=1)], axis=1)
    g["conv_b"] = jnp.concatenate([dbg8.sum(axis=0), dbu8.sum(axis=0)], axis=0)
    dx = _mm_tb([(dpg, 0), (dpu, F)], p["w_up"], out_dtype=F32, tm=ts, tk=D, name=tag + "_dx")
    _grad_into(gbuf, "ffn_w_up", (DEPTH, D, 2 * F), i, 0, 0, x, dpg, tk=D, tn=tc, tm=tw, name=tag + "_dwup_g")
    _grad_into(gbuf, "ffn_w_up", (DEPTH, D, 2 * F), i, 0, F, x, dpu, tk=D, tn=tc, tm=tw, name=tag + "_dwup_u")
    return dx, g


HEAD_SLOT = LANES
MLA_SCALE = (QK_NOPE + QK_ROPE) ** -0.5
NEG_BIG = -1e30
ATTN_BLOCK = 1024


def _rms_fwd(x, g, *, c0, ts, name):
    S = x.shape[0]
    w = g.shape[0]

    def body(x_ref, g_ref, o_ref):
        xv = x_ref[...]
        rstd = lax.rsqrt(jnp.mean(xv * xv, axis=-1, keepdims=True) + RMS_EPS)
        o_ref[...] = (xv * rstd * g_ref[...]).astype(BF16)

    return pl.pallas_call(
        body, name=name, grid=(S // ts,),
        in_specs=[pl.BlockSpec((ts, w), lambda i: (i, c0 // w)), pl.BlockSpec((1, w), lambda i: (0, 0))],
        out_specs=pl.BlockSpec((ts, w), lambda i: (i, 0)),
        out_shape=jax.ShapeDtypeStruct((S, w), BF16), compiler_params=_cp(("parallel",)),
    )(x, g.reshape(1, w))


def _rms_bwd(dy, x, g, *, c0, ts, name):
    S = x.shape[0]
    w = g.shape[0]

    def body(dy_ref, x_ref, g_ref, dx_ref, dg_ref):
        @pl.when(pl.program_id(0) == 0)
        def _():
            dg_ref[...] = jnp.zeros_like(dg_ref)
        xv = x_ref[...]
        dyv = dy_ref[...]
        rstd = lax.rsqrt(jnp.mean(xv * xv, axis=-1, keepdims=True) + RMS_EPS)
        dyg = dyv * g_ref[...]
        m = jnp.mean(dyg * xv, axis=-1, keepdims=True)
        dx_ref[...] = (rstd * (dyg - xv * (rstd * rstd) * m)).astype(BF16)
        dg_ref[...] += _fold8(dyv * xv * rstd)

    return pl.pallas_call(
        body, name=name, grid=(S // ts,),
        in_specs=[pl.BlockSpec((ts, w), lambda i: (i, 0)), pl.BlockSpec((ts, w), lambda i: (i, c0 // w)),
                  pl.BlockSpec((1, w), lambda i: (0, 0))],
        out_specs=[pl.BlockSpec((ts, w), lambda i: (i, 0)), pl.BlockSpec((SUBLANES, w), lambda i: (0, 0))],
        out_shape=[jax.ShapeDtypeStruct((S, w), BF16), jax.ShapeDtypeStruct((SUBLANES, w), F32)],
        compiler_params=_cp(("arbitrary",)),
    )(dy, x, g.reshape(1, w))


def _mla_tables(pos):
    S = pos.shape[0]
    half = QK_ROPE // 2
    inv_freq = ROPE_BASE ** (-jnp.arange(0, QK_ROPE, 2, dtype=F32) / QK_ROPE)
    ang = pos.astype(F32)[:, None] * inv_freq
    cos, sin = jnp.cos(ang), jnp.sin(ang)
    z = lambda n: jnp.zeros((S, n), F32)
    pad = HEAD_SLOT - QK_NOPE - QK_ROPE
    c = jnp.concatenate([jnp.ones((S, QK_NOPE), F32), cos, cos, z(pad)], axis=1)
    s1 = jnp.concatenate([z(QK_NOPE), -sin, z(half), z(pad)], axis=1)
    s2 = jnp.concatenate([z(QK_NOPE), z(half), sin, z(pad)], axis=1)
    return c, s1, s2


def _mla_prep_fwd(qraw, kvraw, proj, tabs, *, kpe_c0, ts, name):
    S = qraw.shape[0]
    H = MLA_HEADS
    half = QK_ROPE // 2

    def body(q_ref, kn_ref, kpe_ref, c_ref, s1_ref, s2_ref, qo_ref, ko_ref):
        c, s1, s2 = c_ref[...], s1_ref[...], s2_ref[...]

        def rope(v):
            return v * c + pltpu.roll(v, HEAD_SLOT - half, 1) * s1 + pltpu.roll(v, half, 1) * s2

        kpe_r = rope(kpe_ref[...])
        for h in range(H):
            sl = slice(h * HEAD_SLOT, (h + 1) * HEAD_SLOT)
            qo_ref[:, sl] = (rope(q_ref[:, sl]) * MLA_SCALE).astype(BF16)
            ko_ref[:, sl] = (kn_ref[:, sl] + kpe_r).astype(BF16)

    wide = pl.BlockSpec((ts, H * HEAD_SLOT), lambda i: (i, 0))
    tab = pl.BlockSpec((ts, HEAD_SLOT), lambda i: (i, 0))
    return pl.pallas_call(
        body, name=name, grid=(S // ts,),
        in_specs=[wide, wide, pl.BlockSpec((ts, HEAD_SLOT), lambda i: (i, kpe_c0 // HEAD_SLOT)), tab, tab, tab],
        out_specs=[wide, wide],
        out_shape=[jax.ShapeDtypeStruct((S, H * HEAD_SLOT), BF16)] * 2,
        compiler_params=_cp(("parallel",)),
    )(qraw, kvraw, proj, *tabs)


def _mla_prep_bwd(dq, dk, dv, tabs, *, ts, name):
    S = dk.shape[0]
    H = MLA_HEADS
    half = QK_ROPE // 2
    kw = H * HEAD_SLOT
    vw = H * V_HEAD

    def body(dq_ref, dk_ref, dv_ref, c_ref, s1_ref, s2_ref, dqr_ref, dkv_ref, dkpe_ref):
        c, s1, s2 = c_ref[...], s1_ref[...], s2_ref[...]

        def rope_t(g):
            return g * c + pltpu.roll(g * s1, half, 1) + pltpu.roll(g * s2, HEAD_SLOT - half, 1)

        gsum = jnp.zeros((ts, HEAD_SLOT), F32)
        for h in range(H):
            sl = slice(h * HEAD_SLOT, (h + 1) * HEAD_SLOT)
            hs = slice((h % 2) * HEAD_SLOT, (h % 2 + 1) * HEAD_SLOT)
            dqr_ref[:, sl] = (rope_t(dq_ref[h // 2, :, hs]) * MLA_SCALE).astype(BF16)
            dkh = dk_ref[:, sl]
            dkv_ref[:, sl] = dkh.astype(BF16)
            gsum = gsum + dkh
        dkv_ref[:, kw:] = dv_ref[...].astype(BF16)
        lane = lax.broadcasted_iota(jnp.int32, (ts, HEAD_SLOT), 1)
        pe = jnp.logical_and(lane >= QK_NOPE, lane < QK_NOPE + QK_ROPE)
        dkpe_ref[...] = rope_t(jnp.where(pe, gsum, 0.0)).astype(BF16)

    tab = pl.BlockSpec((ts, HEAD_SLOT), lambda i: (i, 0))
    return pl.pallas_call(
        body, name=name, grid=(S // ts,),
        in_specs=[pl.BlockSpec((H // 2, ts, 2 * HEAD_SLOT), lambda i: (0, i, 0)), pl.BlockSpec((ts, kw), lambda i: (i, 0)),
                  pl.BlockSpec((ts, vw), lambda i: (i, 0)), tab, tab, tab],
        out_specs=[pl.BlockSpec((ts, kw), lambda i: (i, 0)), pl.BlockSpec((ts, kw + vw), lambda i: (i, 0)), tab],
        out_shape=[jax.ShapeDtypeStruct((S, kw), BF16), jax.ShapeDtypeStruct((S, kw + vw), BF16),
                   jax.ShapeDtypeStruct((S, HEAD_SLOT), BF16)],
        compiler_params=_cp(("parallel",)),
    )(dq, dk, dv, *tabs)


def _attn_pairs(nb, kv_outer):
    if kv_outer:
        pr = [(i, j) for j in range(nb) for i in range(j, nb)]
    else:
        pr = [(i, j) for i in range(nb) for j in range(i + 1)]
    return (jnp.asarray(np.array([p[0] for p in pr], np.int32)), jnp.asarray(np.array([p[1] for p in pr], np.int32)))


def _attn_scores(q_ref, k_ref, hh, diag, T):
    sl = slice(hh * HEAD_SLOT, (hh + 1) * HEAD_SLOT)
    s = _dot_tb(q_ref[:, sl], k_ref[:, sl])
    if not diag:
        return s
    row = lax.broadcasted_iota(jnp.int32, (T, T), 0) // CHUNK
    col = lax.broadcasted_iota(jnp.int32, (T, T), 1) // CHUNK
    return jnp.where(col <= row, s, NEG_BIG)


def _on_block_kind(i, j, step):
    @pl.when(i == j)
    def _():
        step(True)

    @pl.when(i != j)
    def _():
        step(False)


def _attn_fwd(q, k, kvraw, *, T, name):
    S = q.shape[0]
    NP = MLA_HEADS // 2
    nb = S // T
    ii, jj = _attn_pairs(nb, kv_outer=False)
    v_c0 = MLA_HEADS * HEAD_SLOT // LANES

    def body(ii_ref, jj_ref, q_ref, k_ref, v_ref, o_ref, lse_ref, m_sc, l_sc, acc_sc):
        t = pl.program_id(1)
        i, j = ii_ref[t], jj_ref[t]

        @pl.when(j == 0)
        def _():
            m_sc[...] = jnp.full_like(m_sc, NEG_BIG)
            l_sc[...] = jnp.zeros_like(l_sc)
            acc_sc[...] = jnp.zeros_like(acc_sc)

        lo = lax.broadcasted_iota(jnp.int32, (T, LANES), 1) < V_HEAD

        def step(diag):
            v = v_ref[...].astype(BF16)
            vh = (jnp.where(lo, v, jnp.zeros_like(v)), jnp.where(lo, jnp.zeros_like(v), v))
            alphas, pv = [], None
            for hh in range(2):
                s = _attn_scores(q_ref, k_ref, hh, diag, T)
                m_prev = m_sc[hh]
                m_new = jnp.maximum(m_prev, jnp.max(s, axis=1, keepdims=True))
                p = jnp.exp(s - jnp.tile(m_new, (1, T // LANES)))
                alpha = jnp.exp(m_prev - m_new)
                l_sc[hh] = alpha * l_sc[hh] + jnp.sum(p, axis=1, keepdims=True)
                m_sc[hh] = m_new
                alphas.append(alpha)
                t_pv = _dot(p.astype(BF16), vh[hh])
                pv = t_pv if pv is None else pv + t_pv
            acc_sc[...] = acc_sc[...] * jnp.where(lo, alphas[0], alphas[1]) + pv

        _on_block_kind(i, j, step)

        @pl.when(j == i)
        def _():
            l0, l1 = l_sc[0], l_sc[1]
            o_ref[...] = acc_sc[...] * jnp.where(lo, 1.0 / l0, 1.0 / l1)
            lse_ref[...] = jnp.where(lo, m_sc[0] + jnp.log(l0), m_sc[1] + jnp.log(l1))

    grid_spec = pltpu.PrefetchScalarGridSpec(
        num_scalar_prefetch=2, grid=(NP, int(ii.shape[0])),
        in_specs=[pl.BlockSpec((T, 2 * HEAD_SLOT), lambda p, t, ii, jj: (ii[t], p)),
                  pl.BlockSpec((T, 2 * HEAD_SLOT), lambda p, t, ii, jj: (jj[t], p)),
                  pl.BlockSpec((T, LANES), lambda p, t, ii, jj: (jj[t], v_c0 + p))],
        out_specs=[pl.BlockSpec((T, LANES), lambda p, t, ii, jj: (ii[t], p)),
                   pl.BlockSpec((None, T, LANES), lambda p, t, ii, jj: (p, ii[t], 0))],
        scratch_shapes=[pltpu.VMEM((2, T, LANES), F32), pltpu.VMEM((2, T, LANES), F32), pltpu.VMEM((T, LANES), F32)])
    return pl.pallas_call(
        body, name=name, grid_spec=grid_spec,
        out_shape=[jax.ShapeDtypeStruct((S, MLA_HEADS * V_HEAD), F32), jax.ShapeDtypeStruct((NP, S, LANES), F32)],
        compiler_params=_cp(("parallel", "arbitrary")),
    )(ii, jj, q, k, kvraw)


def _attn_bwd_common(q_ref, k_ref, v, do, o, lse, hh, diag, T, lo):
    sel = lo if hh == 0 else jnp.logical_not(lo)
    s = _attn_scores(q_ref, k_ref, hh, diag, T)
    p = jnp.exp(s - lse[:, hh * V_HEAD:hh * V_HEAD + 1])
    do_h = jnp.where(sel, do, 0.0)
    dsum = jnp.sum(do_h * o, axis=1, keepdims=True)
    do_hb = do_h.astype(BF16)
    dp = _dot_tb(do_hb, v)
    return p, p * (dp - dsum), do_hb


def _attn_bwd(q, k, kvraw, do, o, lse, *, T, name):
    S = q.shape[0]
    NP = MLA_HEADS // 2
    nb = S // T
    ii, jj = _attn_pairs(nb, kv_outer=True)
    n_steps = int(ii.shape[0])
    v_c0 = MLA_HEADS * HEAD_SLOT // LANES

    def body(ii_ref, jj_ref, q_ref, k_ref, v_ref, do_ref, o_ref, lse_ref, dq_hbm, dk_ref, dv_ref,
             dq_sc, dk_sc, dv_sc, sem):
        pair = pl.program_id(0)
        t = pl.program_id(1)
        i, j = ii_ref[t], jj_ref[t]

        @pl.when(t == 0)
        def _():
            dq_sc[...] = jnp.zeros_like(dq_sc)

        @pl.when(i == j)
        def _():
            dk_sc[...] = jnp.zeros_like(dk_sc)
            dv_sc[...] = jnp.zeros_like(dv_sc)

        lo = lax.broadcasted_iota(jnp.int32, (T, LANES), 1) < V_HEAD
        rows = pl.ds(pl.multiple_of(i * T, T), T)

        def step(diag):
            v = v_ref[...].astype(BF16)
            do, o_v, lse_v = do_ref[...], o_ref[...], lse_ref[...]
            for hh in range(2):
                sl = slice(hh * HEAD_SLOT, (hh + 1) * HEAD_SLOT)
                p, ds, do_hb = _attn_bwd_common(q_ref, k_ref, v, do, o_v, lse_v, hh, diag, T, lo)
                dsb = ds.astype(BF16)
                dv_sc[...] += _dot_ta(p.astype(BF16), do_hb)
                dk_sc[:, sl] += _dot_ta(dsb, q_ref[:, sl])
                dq_sc[rows, sl] += _dot(dsb, k_ref[:, sl])

        _on_block_kind(i, j, step)

        @pl.when(i == nb - 1)
        def _():
            dk_ref[...] = dk_sc[...]
            dv_ref[...] = dv_sc[...]

        @pl.when(t == n_steps - 1)
        def _():
            cp = pltpu.make_async_copy(dq_sc, dq_hbm.at[pair], sem)
            cp.start()
            cp.wait()

    qi = lambda p, t, ii, jj: (ii[t], p)
    kj = lambda p, t, ii, jj: (jj[t], p)
    grid_spec = pltpu.PrefetchScalarGridSpec(
        num_scalar_prefetch=2, grid=(NP, n_steps),
        in_specs=[pl.BlockSpec((T, 2 * HEAD_SLOT), qi), pl.BlockSpec((T, 2 * HEAD_SLOT), kj),
                  pl.BlockSpec((T, LANES), lambda p, t, ii, jj: (jj[t], v_c0 + p)),
                  pl.BlockSpec((T, LANES), qi), pl.BlockSpec((T, LANES), qi),
                  pl.BlockSpec((None, T, LANES), lambda p, t, ii, jj: (p, ii[t], 0))],
        out_specs=[pl.BlockSpec(memory_space=pl.ANY), pl.BlockSpec((T, 2 * HEAD_SLOT), kj),
                   pl.BlockSpec((T, LANES), kj)],
        scratch_shapes=[pltpu.VMEM((S, 2 * HEAD_SLOT), F32), pltpu.VMEM((T, 2 * HEAD_SLOT), F32),
                        pltpu.VMEM((T, LANES), F32), pltpu.SemaphoreType.DMA])
    return pl.pallas_call(
        body, name=name, grid_spec=grid_spec,
        out_shape=[jax.ShapeDtypeStruct((NP, S, 2 * HEAD_SLOT), F32),
                   jax.ShapeDtypeStruct((S, MLA_HEADS * HEAD_SLOT), F32),
                   jax.ShapeDtypeStruct((S, MLA_HEADS * V_HEAD), F32)],
        compiler_params=_cp(("arbitrary", "arbitrary")),
    )(ii, jj, q, k, kvraw, do, o, lse)


def _mla_permute_weights(w_in, w_uq, w_ukv):
    D = w_in.shape[0]
    H = MLA_HEADS
    qk = QK_NOPE + QK_ROPE
    lat = Q_LORA + KV_LORA
    kpe = jnp.zeros((D, HEAD_SLOT), w_in.dtype).at[:, QK_NOPE:qk].set(w_in[:, lat:])
    w_in_p = jnp.concatenate([w_in[:, :lat], kpe], axis=1)
    w_uq_p = jnp.pad(w_uq.reshape(Q_LORA, H, qk), ((0, 0), (0, 0), (0, HEAD_SLOT - qk))).reshape(Q_LORA, H * HEAD_SLOT)
    kv = w_ukv.reshape(KV_LORA, H, QK_NOPE + V_HEAD)
    wk = jnp.pad(kv[:, :, :QK_NOPE], ((0, 0), (0, 0), (0, HEAD_SLOT - QK_NOPE))).reshape(KV_LORA, H * HEAD_SLOT)
    wv = kv[:, :, QK_NOPE:].reshape(KV_LORA, H * V_HEAD)
    return w_in_p, w_uq_p, jnp.concatenate([wk, wv], axis=1)


def _mla_unpermute_grads(g_in_p, g_uq_p, g_ukv_p):
    H = MLA_HEADS
    qk = QK_NOPE + QK_ROPE
    lat = Q_LORA + KV_LORA
    g_in = jnp.concatenate([g_in_p[:, :lat], g_in_p[:, lat + QK_NOPE:lat + qk]], axis=1)
    g_uq = g_uq_p.reshape(Q_LORA, H, HEAD_SLOT)[:, :, :qk].reshape(Q_LORA, H * qk)
    gk = g_ukv_p[:, :H * HEAD_SLOT].reshape(KV_LORA, H, HEAD_SLOT)[:, :, :QK_NOPE]
    gv = g_ukv_p[:, H * HEAD_SLOT:].reshape(KV_LORA, H, V_HEAD)
    g_ukv = jnp.concatenate([gk, gv], axis=2).reshape(KV_LORA, H * (QK_NOPE + V_HEAD))
    return g_in, g_uq, g_ukv


def _mla_mixer_fwd(x, pos, p, tag):
    S, D = x.shape
    ts = _rt(S, 512)
    T = _rt(S, ATTN_BLOCK)
    lat = Q_LORA + KV_LORA
    tabs = _mla_tables(pos)
    proj = _mm(x, p["w_in_p"], out_dtype=F32, tm=ts, tn=p["w_in_p"].shape[1], name=tag + "_proj")
    qn = _rms_fwd(proj, p["q_norm"], c0=0, ts=ts, name=tag + "_qn")
    kvn = _rms_fwd(proj, p["kv_norm"], c0=Q_LORA, ts=ts, name=tag + "_kvn")
    qraw = _mm(qn, p["w_uq_p"], out_dtype=F32, tm=ts, tn=1024, name=tag + "_uq")
    kvraw = _mm(kvn, p["w_ukv_p"], out_dtype=F32, tm=ts, tn=1024, name=tag + "_ukv")
    q, k = _mla_prep_fwd(qraw, kvraw, proj, tabs, kpe_c0=lat, ts=ts, name=tag + "_prep")
    o, lse = _attn_fwd(q, k, kvraw, T=T, name=tag + "_attn")
    mix = _mm(o, p["w_out"], out_dtype=F32, tm=ts, tn=D, name=tag + "_out")
    return mix, (proj, qn, kvn, kvraw, q, k, o, lse, tabs)


def _mla_mixer_bwd(dmix, x, p, saved, tag, gbuf, j):
    proj, qn, kvn, kvraw, q, k, o, lse, tabs = saved
    S, D = x.shape
    ts = _rt(S, 512)
    T = _rt(S, ATTN_BLOCK)
    lat = Q_LORA + KV_LORA
    g = {}
    do = _mm_tb([(dmix, 0)], p["w_out"], out_dtype=F32, tm=ts, tk=p["w_out"].shape[0], name=tag + "_do")
    g["w_out"] = _mm_ta(o, dmix, tk=p["w_out"].shape[0], tn=D, tm=ts, name=tag + "_dwout")
    dq, dk, dv = _attn_bwd(q, k, kvraw, do, o, lse, T=T, name=tag + "_attn_bwd")
    dqraw, dkvraw, dkpe = _mla_prep_bwd(dq, dk, dv, tabs, ts=_rt(S, 256), name=tag + "_prepb")
    dqn = _mm_tb([(dqraw, 0)], p["w_uq_p"], out_dtype=F32, tm=ts, tk=Q_LORA, name=tag + "_dqn")
    g_uq_p = _mm_ta(qn, dqraw, tk=Q_LORA, tn=1024, tm=ts, name=tag + "_dwuq")
    dkvn = _mm_tb([(dkvraw, 0)], p["w_ukv_p"], out_dtype=F32, tm=ts, tk=KV_LORA, name=tag + "_dkvn")
    g_ukv_p = _mm_ta(kvn, dkvraw, tk=KV_LORA, tn=1024, tm=ts, name=tag + "_dwukv")
    dcq, dqg8 = _rms_bwd(dqn, proj, p["q_norm"], c0=0, ts=ts, name=tag + "_qnb")
    dckv, dkvg8 = _rms_bwd(dkvn, proj, p["kv_norm"], c0=Q_LORA, ts=ts, name=tag + "_kvnb")
    g["q_norm"] = dqg8.sum(axis=0)
    g["kv_norm"] = dkvg8.sum(axis=0)
    dx = _mm_tb([(dcq, 0), (dckv, Q_LORA), (dkpe, lat)], p["w_in_p"], out_dtype=F32, tm=ts, tk=D, name=tag + "_dx")
    g_in_p = jnp.concatenate(
        [_mm_ta(x, dcq, tk=D, tn=Q_LORA, tm=ts, name=tag + "_dwin_q"),
         _mm_ta(x, dckv, tk=D, tn=KV_LORA, tm=ts, name=tag + "_dwin_kv"),
         _mm_ta(x, dkpe, tk=D, tn=HEAD_SLOT, tm=ts, name=tag + "_dwin_pe")], axis=1)
    g["w_in"], g["w_uq"], g["w_ukv"] = _mla_unpermute_grads(g_in_p, g_uq_p, g_ukv_p)
    return dx, g


RET_QK = 256
RET_V = 512


def _ret_tables(pos, T):
    half = RET_QK // 2
    inv_freq = ROPE_BASE ** (-jnp.arange(0, RET_QK, 2, dtype=F32) / RET_QK)
    ang = pos.astype(F32)[:, None] * inv_freq
    lg = jnp.log1p(-jnp.exp2(-5.0 - jnp.arange(RET_HEADS, dtype=F32)))
    idx = jnp.arange(T, dtype=F32)
    ch = jnp.arange(T) // CHUNK
    dm = jnp.where(ch[None, :] <= ch[:, None], jnp.exp(lg[:, None, None] * jnp.abs(idx[:, None] - idx[None, :])), 0.0)
    xi = jnp.broadcast_to(jnp.exp(lg[:, None] * (idx + 1.0))[:, :, None], (RET_HEADS, T, RET_QK))
    zeta = jnp.broadcast_to(jnp.exp(lg[:, None] * (T - 1.0 - idx))[:, :, None], (RET_HEADS, T, RET_QK))
    g_t = jnp.broadcast_to(jnp.exp(lg * T)[:, None, None], (RET_HEADS, 1, RET_V))
    assert half == LANES
    return jnp.cos(ang), jnp.sin(ang), dm.astype(F32), xi.astype(F32), zeta.astype(F32), g_t.astype(F32)


def _rope_half(x, c, s):
    x1, x2 = x[:, :LANES], x[:, LANES:]
    return jnp.concatenate([x1 * c - x2 * s, x1 * s + x2 * c], axis=1)


def _rope_half_t(g, c, s):
    g1, g2 = g[:, :LANES], g[:, LANES:]
    return jnp.concatenate([g1 * c + g2 * s, g2 * c - g1 * s], axis=1)


def _ret_qkv(q_ref, k_ref, v_ref, c_ref, s_ref):
    c, s = c_ref[...], s_ref[...]
    q = _rope_half(q_ref[...], c, s)
    k = _rope_half(k_ref[...], c, s) * (RET_QK ** -0.5)
    return q, k, v_ref[...].astype(BF16)


def _ret_in_specs(T, H, rev_nb=None):
    rb = (lambda n: n) if rev_nb is None else (lambda n: rev_nb - 1 - n)
    nq = H * RET_QK // RET_QK
    nv = 2 * H * RET_QK // RET_V
    return dict(
        q=pl.BlockSpec((T, RET_QK), lambda h, n: (rb(n), h)),
        k=pl.BlockSpec((T, RET_QK), lambda h, n: (rb(n), nq + h)),
        v=pl.BlockSpec((T, RET_V), lambda h, n: (rb(n), nv + h)),
        g=pl.BlockSpec((T, RET_V), lambda h, n: (rb(n), nv + H + h)),
        yv=pl.BlockSpec((T, RET_V), lambda h, n: (rb(n), h)),
        cs=pl.BlockSpec((T, LANES), lambda h, n: (rb(n), 0)),
        dm=pl.BlockSpec((None, T, T), lambda h, n: (h, 0, 0)),
        xz=pl.BlockSpec((None, T, RET_QK), lambda h, n: (h, 0, 0)),
        gt=pl.BlockSpec((None, 1, RET_V), lambda h, n: (h, 0, 0)),
        gn=pl.BlockSpec((1, RET_V), lambda h, n: (0, h)),
        st=pl.BlockSpec((None, None, RET_QK, RET_V), lambda h, n: (h, rb(n), 0, 0)),
    )


def _ret_fwd(proj, gn_g, tabs, *, T, name):
    S = proj.shape[0]
    H = RET_HEADS
    nb = S // T
    cos, sin, dm, xi, zeta, g_t = tabs
    sp = _ret_in_specs(T, H)

    def body(q_ref, k_ref, v_ref, g_ref, gn_ref, c_ref, s_ref, dm_ref, xi_ref, zeta_ref, gt_ref,
             o_ref, y_ref, st_ref, st):
        @pl.when(pl.program_id(1) == 0)
        def _():
            st[...] = jnp.zeros_like(st)

        q, k, vb = _ret_qkv(q_ref, k_ref, v_ref, c_ref, s_ref)
        qb, kb = q.astype(BF16), k.astype(BF16)
        s0 = st[...]
        s0b = s0.astype(BF16)
        st_ref[...] = s0b
        a = _dot_tb(qb, kb) * dm_ref[...]
        y = _dot(a.astype(BF16), vb) + _dot((q * xi_ref[...]).astype(BF16), s0b)
        st[...] = s0 * gt_ref[...] + _dot_ta((k * zeta_ref[...]).astype(BF16), vb)
        y_ref[...] = y
        mu = jnp.mean(y, axis=-1, keepdims=True)
        yc = y - mu
        var = jnp.mean(yc * yc, axis=-1, keepdims=True)
        gv = g_ref[...]
        o_ref[...] = (gv * _sigmoid(gv) * (yc * lax.rsqrt(var + LN_EPS) * gn_ref[...])).astype(BF16)

    return pl.pallas_call(
        body, name=name, grid=(H, nb),
        in_specs=[sp["q"], sp["k"], sp["v"], sp["g"], sp["gn"], sp["cs"], sp["cs"], sp["dm"], sp["xz"], sp["xz"], sp["gt"]],
        out_specs=[sp["yv"], sp["yv"], sp["st"]],
        out_shape=[jax.ShapeDtypeStruct((S, H * RET_V), BF16), jax.ShapeDtypeStruct((S, H * RET_V), F32),
                   jax.ShapeDtypeStruct((H, nb, RET_QK, RET_V), BF16)],
        scratch_shapes=[pltpu.VMEM((RET_QK, RET_V), F32)],
        compiler_params=_cp(("parallel", "arbitrary")),
    )(proj, proj, proj, proj, gn_g.reshape(1, H * RET_V), cos, sin, dm, xi, zeta, g_t)


def _ret_gn_bwd(dout, proj, y, gn_g, *, ts, name):
    S = proj.shape[0]
    H = RET_HEADS
    goff = 2 * H * RET_QK // RET_V + H

    def body(do_ref, g_ref, y_ref, gn_ref, dy_ref, dg_ref, dgn_ref):
        @pl.when(pl.program_id(1) == 0)
        def _():
            dgn_ref[...] = jnp.zeros_like(dgn_ref)
        y_v = y_ref[...]
        mu = jnp.mean(y_v, axis=-1, keepdims=True)
        yc = y_v - mu
        var = jnp.mean(yc * yc, axis=-1, keepdims=True)
        rstd = lax.rsqrt(var + LN_EPS)
        yhat = yc * rstd
        gv = g_ref[...]
        sg = _sigmoid(gv)
        dout = do_ref[...]
        gn = gn_ref[...]
        dg_ref[...] = (dout * (yhat * gn) * (sg * (1.0 + gv * (1.0 - sg)))).astype(BF16)
        dyn = dout * (gv * sg)
        dgn_ref[...] += _fold8(dyn * yhat)
        dyh = dyn * gn
        m1 = jnp.mean(dyh, axis=-1, keepdims=True)
        m2 = jnp.mean(dyh * yhat, axis=-1, keepdims=True)
        dy_ref[...] = (rstd * (dyh - m1 - yhat * m2)).astype(BF16)

    blk = pl.BlockSpec((ts, RET_V), lambda h, i: (i, h))
    return pl.pallas_call(
        body, name=name, grid=(H, S // ts),
        in_specs=[blk, pl.BlockSpec((ts, RET_V), lambda h, i: (i, goff + h)), blk,
                  pl.BlockSpec((1, RET_V), lambda h, i: (0, h))],
        out_specs=[blk, blk, pl.BlockSpec((SUBLANES, RET_V), lambda h, i: (0, h))],
        out_shape=[jax.ShapeDtypeStruct((S, H * RET_V), BF16), jax.ShapeDtypeStruct((S, H * RET_V), BF16),
                   jax.ShapeDtypeStruct((SUBLANES, H * RET_V), F32)],
        compiler_params=_cp(("parallel", "arbitrary")),
    )(dout, proj, y, gn_g.reshape(1, H * RET_V))


def _ret_bwd(proj, dy, states, tabs, *, T, name):
    S = proj.shape[0]
    H = RET_HEADS
    nb = S // T
    cos, sin, dm, xi, zeta, g_t = tabs
    sp = _ret_in_specs(T, H, rev_nb=nb)

    def body(q_ref, k_ref, v_ref, dy_ref, st_ref, c_ref, s_ref, dm_ref, xi_ref, zeta_ref, gt_ref,
             dq_ref, dk_ref, dv_ref, ds):
        @pl.when(pl.program_id(1) == 0)
        def _():
            ds[...] = jnp.zeros_like(ds)

        q, k, vb = _ret_qkv(q_ref, k_ref, v_ref, c_ref, s_ref)
        qb, kb = q.astype(BF16), k.astype(BF16)
        dyb = dy_ref[...]
        s0b = st_ref[...]
        dmv, xiv, zv = dm_ref[...], xi_ref[...], zeta_ref[...]
        ds_v = ds[...]
        dsb = ds_v.astype(BF16)
        gm = (_dot_tb(dyb, vb) * dmv).astype(BF16)
        ab = (_dot_tb(qb, kb) * dmv).astype(BF16)
        kz = (k * zv).astype(BF16)
        qx = (q * xiv).astype(BF16)
        dq = _dot(gm, kb) + xiv * _dot_tb(dyb, s0b)
        dk = _dot_ta(gm, qb) + zv * _dot_tb(vb, dsb)
        dv_ref[...] = (_dot_ta(ab, dyb) + _dot(kz, dsb)).astype(BF16)
        ds[...] = ds_v * gt_ref[...] + _dot_ta(qx, dyb)
        c, s = c_ref[...], s_ref[...]
        dq_ref[...] = _rope_half_t(dq, c, s).astype(BF16)
        dk_ref[...] = _rope_half_t(dk * (RET_QK ** -0.5), c, s).astype(BF16)

    qblk = pl.BlockSpec((T, RET_QK), lambda h, n: (nb - 1 - n, h))
    return pl.pallas_call(
        body, name=name, grid=(H, nb),
        in_specs=[sp["q"], sp["k"], sp["v"], sp["yv"], sp["st"], sp["cs"], sp["cs"], sp["dm"], sp["xz"], sp["xz"], sp["gt"]],
        out_specs=[qblk, qblk, sp["yv"]],
        out_shape=[jax.ShapeDtypeStruct((S, H * RET_QK), BF16), jax.ShapeDtypeStruct((S, H * RET_QK), BF16),
                   jax.ShapeDtypeStruct((S, H * RET_V), BF16)],
        scratch_shapes=[pltpu.VMEM((RET_QK, RET_V), F32)],
        compiler_params=_cp(("parallel", "arbitrary")),
    )(proj, proj, proj, dy, states, cos, sin, dm, xi, zeta, g_t)


def _ret_mixer_fwd(x, pos, p, tag):
    S, D = x.shape
    ts = _rt(S, 512)
    T = _rt(S, 256)
    tabs = _ret_tables(pos, T)
    proj = _mm(x, p["w_in"], out_dtype=F32, tm=_rt(S, 256), tn=p["w_in"].shape[1], name=tag + "_proj")
    gated, y, states = _ret_fwd(proj, p["gn_g"], tabs, T=T, name=tag + "_ret")
    mix = _mm(gated, p["w_out"], out_dtype=F32, tm=ts, tn=D, name=tag + "_out")
    return mix, (proj, gated, y, states, tabs)


def _ret_mixer_bwd(dmix, x, p, saved, tag, gbuf, j):
    proj, gated, y, states, tabs = saved
    S, D = x.shape
    ts = _rt(S, 512)
    T = _rt(S, 256)
    H = RET_HEADS
    hq, hv = H * RET_QK, H * RET_V
    g = {}
    dout = _mm_tb([(dmix, 0)], p["w_out"], out_dtype=F32, tm=ts, tk=1024, name=tag + "_dgated")
    n_ret = DEPTH // 3
    _grad_into(gbuf, "ret_w_out", (n_ret, hv, D), j, 0, 0, gated, dmix, tk=1024, tn=D, tm=ts, name=tag + "_dwout")
    dy, dgate, dgn8 = _ret_gn_bwd(dout, proj, y, p["gn_g"], ts=_rt(S, 256), name=tag + "_gnb")
    g["gn_g"] = dgn8.sum(axis=0)
    dq, dk, dv = _ret_bwd(proj, dy, states, tabs, T=T, name=tag + "_retb")
    dx = _mm_tb([(dq, 0), (dk, hq), (dv, 2 * hq), (dgate, 2 * hq + hv)], p["w_in"], out_dtype=F32,
                tm=ts, tk=512, name=tag + "_dx")
    w_in_shape = (n_ret, D, 2 * hq + 2 * hv)
    for part, c0, nm in ((dq, 0, "q"), (dk, hq, "k"), (dv, 2 * hq, "v"), (dgate, 2 * hq + hv, "g")):
        _grad_into(gbuf, "ret_w_in", w_in_shape, j, 0, c0, x, part, tk=D, tn=1024, tm=ts, name=tag + "_dwin_" + nm)
    return dx, g


PACK_W = 1024
ANY = pl.BlockSpec(memory_space=pl.ANY)
MESH = pl.DeviceIdType.MESH


def _coords():
    return lax.axis_index("x"), lax.axis_index("y"), lax.axis_index("c")


def _chip_peers(x, y):
    return [(1 - x, y), (x, 1 - y), (1 - x, 1 - y)]


def _slot(ref, axis, s, n):
    if axis is None:
        return ref.at[s]
    size = n // N_CHIPS
    sl = pl.ds(pl.multiple_of(s * size, LANES if axis == 2 else 2 * SUBLANES), size)
    return ref.at[:, sl, :] if axis == 1 else ref.at[:, :, sl]


def _row_half(ref, h):
    if len(ref.shape) == 2:
        n = ref.shape[0] // 2
        return ref.at[pl.ds(pl.multiple_of(h * n, 2 * SUBLANES), n), :]
    n = ref.shape[1] // 2
    return ref.at[:, pl.ds(pl.multiple_of(h * n, 2 * SUBLANES), n), :]


def _gather_chips(items, name):
    n = len(items)
    axes = [ax for _, ax in items]
    out_shapes = []
    for arr, ax in items:
        shp = (N_CHIPS,) + arr.shape if ax is None else tuple(d * (N_CHIPS if i == ax else 1) for i, d in enumerate(arr.shape))
        out_shapes.append(jax.ShapeDtypeStruct(shp, arr.dtype))

    def body(*refs):
        srcs, outs = refs[:n], refs[n:2 * n]
        send_sems, recv_sems, local_sems = refs[2 * n:]
        x, y, c = _coords()
        me = 2 * x + y
        sibling = (x, y, 1 - c)
        dst = lambda t, s: _slot(outs[t], axes[t], s, out_shapes[t].shape[axes[t]] if axes[t] is not None else 0)

        def copy(sem, src, dst_ref, to):
            return pltpu.make_async_remote_copy(src_ref=src, dst_ref=dst_ref, send_sem=send_sems.at[sem],
                                                recv_sem=recv_sems.at[sem], device_id=to, device_id_type=MESH)

        local = [pltpu.make_async_copy(srcs[t], dst(t, me), local_sems.at[t]) for t in range(n)]
        for cp in local:
            cp.start()
        peers = _chip_peers(x, y)
        first, arrive, passed, from_sib = [], [], [], []
        for k, (px, py) in enumerate(peers):
            for t in range(n):
                land = _row_half(dst(t, 2 * px + py), c)
                first.append(copy(k * n + t, _row_half(srcs[t], c), _row_half(dst(t, me), c), (px, py, c)))
                arrive.append(copy(k * n + t, _row_half(srcs[t], c), land, (px, py, c)))
                passed.append(copy((3 + k) * n + t, land, land, sibling))
                from_sib.append(copy((3 + k) * n + t, land, _row_half(dst(t, 2 * px + py), 1 - c), sibling))
        for cp in first:
            cp.start()
        for cp_in, cp_on in zip(arrive, passed):
            cp_in.wait_recv()
            cp_on.start()
        for cp in from_sib:
            cp.wait_recv()
        for cp in first + passed:
            cp.wait_send()
        for cp in local:
            cp.wait()

    return pl.pallas_call(
        body, name=name, in_specs=[ANY] * n, out_specs=[ANY] * n, out_shape=out_shapes,
        scratch_shapes=[pltpu.SemaphoreType.DMA((6 * n,)), pltpu.SemaphoreType.DMA((6 * n,)),
                        pltpu.SemaphoreType.DMA((n,))],
    )(*[arr for arr, _ in items])


def _scatter_chips(items, name):
    n = len(items)
    axes = [ax for _, ax in items]
    out_shapes = []
    for arr, ax in items:
        part = arr.shape[1:] if ax is None else tuple(d // (N_CHIPS if i == ax else 1) for i, d in enumerate(arr.shape))
        out_shapes.append(jax.ShapeDtypeStruct((3,) + part, arr.dtype))

    def body(*refs):
        srcs, outs = refs[:n], refs[n:2 * n]
        send_sems, recv_sems = refs[2 * n:]
        x, y, c = _coords()
        copies = []
        for k, (px, py) in enumerate(_chip_peers(x, y)):
            for t in range(n):
                src = _slot(srcs[t], axes[t], 2 * px + py, srcs[t].shape[axes[t]] if axes[t] is not None else 0)
                copies.append(pltpu.make_async_remote_copy(
                    src_ref=src, dst_ref=outs[t].at[k], send_sem=send_sems.at[k * n + t],
                    recv_sem=recv_sems.at[k * n + t], device_id=(px, py, c), device_id_type=MESH))
        for cp in copies:
            cp.start()
        for cp in copies:
            cp.wait_recv()
        for cp in copies:
            cp.wait_send()

    return pl.pallas_call(
        body, name=name, in_specs=[ANY] * n, out_specs=[ANY] * n, out_shape=out_shapes,
        scratch_shapes=[pltpu.SemaphoreType.DMA((3 * n,)), pltpu.SemaphoreType.DMA((3 * n,))],
    )(*[arr for arr, _ in items])


def _swap_sibling(arrs, name):
    n = len(arrs)

    def body(*refs):
        srcs, outs = refs[:n], refs[n:2 * n]
        send_sems, recv_sems = refs[2 * n:]
        x, y, c = _coords()
        copies = [pltpu.make_async_remote_copy(src_ref=srcs[t], dst_ref=outs[t], send_sem=send_sems.at[t],
                                               recv_sem=recv_sems.at[t], device_id=(x, y, 1 - c), device_id_type=MESH)
                  for t in range(n)]
        for cp in copies:
            cp.start()
        for cp in copies:
            cp.wait_recv()
        for cp in copies:
            cp.wait_send()

    return pl.pallas_call(
        body, name=name, in_specs=[ANY] * n, out_specs=[ANY] * n,
        out_shape=[jax.ShapeDtypeStruct(a_.shape, a_.dtype) for a_ in arrs],
        scratch_shapes=[pltpu.SemaphoreType.DMA((n,)), pltpu.SemaphoreType.DMA((n,))],
    )(*arrs)


def _allreduce_small(v, name):
    R, Wd = v.shape

    def body(v_ref, o_ref, buf, send_sems, recv_sems):
        x, y, c = _coords()
        o_ref[...] = v_ref[...]
        for st, peer in enumerate([(x, y, 1 - c), (x, 1 - y, c), (1 - x, y, c)]):
            cp = pltpu.make_async_remote_copy(src_ref=o_ref, dst_ref=buf.at[st], send_sem=send_sems.at[st],
                                              recv_sem=recv_sems.at[st], device_id=peer, device_id_type=MESH)
            cp.start()
            cp.wait_recv()
            cp.wait_send()
            o_ref[...] = o_ref[...] + buf[st]

    vm = pl.BlockSpec(memory_space=pltpu.VMEM)
    return pl.pallas_call(
        body, name=name, in_specs=[vm], out_specs=vm,
        out_shape=jax.ShapeDtypeStruct((R, Wd), F32),
        scratch_shapes=[pltpu.VMEM((3, R, Wd), F32), pltpu.SemaphoreType.DMA((3,)), pltpu.SemaphoreType.DMA((3,))],
    )(v)


def _row_tile(rows):
    t = rows
    while t > 256:
        assert t % 2 == 0
        t //= 2
    assert t % SUBLANES == 0
    return t


def _sum_partials(g, recv, axis, *, name):
    _, L, R, C = recv.shape
    tr = _row_tile(R)
    me = (2 * lax.axis_index("x") + lax.axis_index("y")).astype(jnp.int32).reshape(1)

    def body(me_ref, g_ref, r_ref, o_ref):
        o_ref[...] = ((g_ref[...] + r_ref[0].astype(F32)) + r_ref[1].astype(F32)) + r_ref[2].astype(F32)

    if axis is None:
        g_spec = pl.BlockSpec((None, None, tr, C), lambda l, i, me: (me[0], l, i, 0))
    elif axis == 1:
        g_spec = pl.BlockSpec((None, tr, C), lambda l, i, me: (l, me[0] * (R // tr) + i, 0))
    else:
        g_spec = pl.BlockSpec((None, tr, C), lambda l, i, me: (l, i, me[0]))
    grid_spec = pltpu.PrefetchScalarGridSpec(
        num_scalar_prefetch=1, grid=(L, R // tr),
        in_specs=[g_spec, pl.BlockSpec((3, None, tr, C), lambda l, i, me: (0, l, i, 0))],
        out_specs=pl.BlockSpec((None, tr, C), lambda l, i, me: (l, i, 0)))
    return pl.pallas_call(
        body, name=name, grid_spec=grid_spec, out_shape=jax.ShapeDtypeStruct((L, R, C), F32),
        compiler_params=_cp(("parallel", "parallel")),
    )(me, g, recv)


def _adamw(w, m, v, ga, gb, *, name):
    L, R, C = w.shape
    tr = _row_tile(R)
    two = gb is not None
    c1 = 1.0 / (1.0 - ADAM_B1 ** ADAM_STEP)
    c2 = 1.0 / (1.0 - ADAM_B2 ** ADAM_STEP)

    def body(*refs):
        if two:
            w_ref, m_ref, v_ref, ga_ref, gb_ref, g_ref, d_ref, mo_ref, vo_ref = refs
            g = ga_ref[...] + gb_ref[...]
        else:
            w_ref, m_ref, v_ref, ga_ref, g_ref, d_ref, mo_ref, vo_ref = refs
            g = ga_ref[...]
        m2 = ADAM_B1 * m_ref[...] + (1.0 - ADAM_B1) * g
        v2 = ADAM_B2 * v_ref[...] + (1.0 - ADAM_B2) * (g * g)
        g_ref[...] = g
        mo_ref[...] = m2
        vo_ref[...] = v2
        d_ref[...] = -ADAM_LR * ((m2 * c1) / (jnp.sqrt(v2 * c2) + ADAM_EPS) + ADAM_WD * w_ref[...])

    blk = pl.BlockSpec((None, tr, C), lambda l, i: (l, i, 0))
    args = [w, m, v, ga] + ([gb] if two else [])
    return pl.pallas_call(
        body, name=name, grid=(L, R // tr), in_specs=[blk] * len(args), out_specs=[blk] * 4,
        out_shape=[jax.ShapeDtypeStruct((L, R, C), F32)] * 4, compiler_params=_cp(("parallel", "parallel")),
    )(*args)


SHARDED = [
    ("ffn_w_up", 2, True), ("ffn_conv_w", 2, False), ("ffn_w_down", 1, True),
    ("lru_w_in", 2, True), ("lru_conv_w", 2, False), ("lru_conv_b", 1, False),
    ("lru_w_a", 2, True), ("lru_b_a", 2, False), ("lru_w_x", 2, True), ("lru_b_x", 2, False),
    ("lru_lambda", 1, False), ("lru_w_out", 1, True),
    ("mla_w_in", 2, True), ("mla_w_uq", 2, True), ("mla_w_ukv", 2, True), ("mla_w_out", 1, True),
    ("ret_w_in", 2, True), ("ret_gn_g", 1, False), ("ret_w_out", 1, True),
]
BIG_AXIS = {"ffn_w_up": 2, "ffn_w_down": 1, "lru_w_in": 2, "lru_w_out": 1, "ret_w_in": 2, "ret_w_out": 1}
REPLICATED = ["ln1_g", "ln1_b", "ln2_g", "ln2_b", "ffn_conv_b", "mla_q_norm", "mla_kv_norm"]
WEIGHTS = ["ln1_g", "ln1_b", "ln2_g", "ln2_b", "ffn_w_up", "ffn_conv_w", "ffn_conv_b", "ffn_w_down", "lru_w_in",
           "lru_conv_w", "lru_conv_b", "lru_w_a", "lru_b_a", "lru_w_x", "lru_b_x", "lru_lambda", "lru_w_out",
           "mla_w_in", "mla_q_norm", "mla_kv_norm", "mla_w_uq", "mla_w_ukv", "mla_w_out", "ret_w_in", "ret_gn_g",
           "ret_w_out"]
PACK_ROWS = 512


def _pack(arrs, dtype, lead=(), rows=PACK_ROWS):
    nl = len(lead)
    flat = jnp.concatenate([a.astype(dtype).reshape(lead + (-1,)) for a in arrs], axis=nl)
    n = flat.shape[nl]
    quantum = rows * PACK_W
    total = -(-n // quantum) * quantum
    flat = jnp.pad(flat, [(0, 0)] * nl + [(0, total - n)])
    return flat.reshape(lead + (total // PACK_W, PACK_W))


def _unpack(buf, shapes, lead=()):
    nl = len(lead)
    flat = buf.reshape(lead + (-1,))
    out, off = [], 0
    for shp in shapes:
        n = int(np.prod(shp))
        out.append(lax.slice_in_dim(flat, off, off + n, axis=nl).reshape(lead + tuple(shp)))
        off += n
    return out


def _layer_params(full, rep, i):
    kind, j = i % 3, i // 3
    ffn = dict(w_up=full["ffn_w_up"][i], conv_w=full["ffn_conv_w"][i], conv_b=rep["ffn_conv_b"][i],
               w_down=full["ffn_w_down"][i])
    if kind == 0:
        mix = dict(w_in=full["lru_w_in"][j], conv_w=full["lru_conv_w"][j], conv_b=full["lru_conv_b"][j],
                   w_a=full["lru_w_a"][j], b_a=full["lru_b_a"][j], w_x=full["lru_w_x"][j], b_x=full["lru_b_x"][j],
                   lam=full["lru_lambda"][j], w_out=full["lru_w_out"][j])
    elif kind == 1:
        w_in_p, w_uq_p, w_ukv_p = _mla_permute_weights(full["mla_w_in"][j], full["mla_w_uq"][j], full["mla_w_ukv"][j])
        mix = dict(w_in_p=w_in_p, w_uq_p=w_uq_p, w_ukv_p=w_ukv_p, q_norm=rep["mla_q_norm"][j],
                   kv_norm=rep["mla_kv_norm"][j], w_out=full["mla_w_out"][j])
    else:
        mix = dict(w_in=full["ret_w_in"][j], gn_g=full["ret_gn_g"][j], w_out=full["ret_w_out"][j])
    return kind, mix, ffn


_MIX_FWD = {0: lambda x, pos, p, tag: _lru_mixer_fwd(x, p, tag), 1: _mla_mixer_fwd, 2: _ret_mixer_fwd}
_MIX_BWD = {0: _lru_mixer_bwd, 1: _mla_mixer_bwd, 2: _ret_mixer_bwd}
_MIX_PREFIX = {0: "lru_", 1: "mla_", 2: "ret_"}
_MIX_KEYS = {0: {"w_in": "lru_w_in", "conv_w": "lru_conv_w", "conv_b": "lru_conv_b", "w_a": "lru_w_a", "b_a": "lru_b_a",
                 "w_x": "lru_w_x", "b_x": "lru_b_x", "lam": "lru_lambda", "w_out": "lru_w_out"},
             1: {"w_in": "mla_w_in", "q_norm": "mla_q_norm", "kv_norm": "mla_kv_norm", "w_uq": "mla_w_uq",
                 "w_ukv": "mla_w_ukv", "w_out": "mla_w_out"},
             2: {"w_in": "ret_w_in", "gn_g": "ret_gn_g", "w_out": "ret_w_out"}}
_FFN_KEYS = {"w_up": "ffn_w_up", "conv_w": "ffn_conv_w", "conv_b": "ffn_conv_b", "w_down": "ffn_w_down"}


def _local_step(x, pos, target, full, rep):
    S, D = x.shape
    ts = _rt(S, 512)
    acts = []
    h = x
    for i in range(DEPTH):
        kind, mp, fp = _layer_params(full, rep, i)
        tag = "l%d" % i
        mix, msaved = _MIX_FWD[kind](h, pos, mp, tag + "m")
        h1, z1 = _ln_fwd(h, mix, rep["ln1_g"][i], rep["ln1_b"][i], ts=ts, name=tag + "_ln1")
        f, fsaved = _ffn_fwd(h1, fp, tag + "f")
        h2, z2 = _ln_fwd(h1, f, rep["ln2_g"][i], rep["ln2_b"][i], ts=ts, name=tag + "_ln2")
        acts.append((kind, mp, fp, h, msaved, h1, z1, fsaved, z2))
        h = h2
    dy, part = _loss_head(h, target, ts=ts, name="loss_head")

    grads = {n: {} for n in WEIGHTS if n not in BIG_AXIS}
    gbuf = {}
    d_a, d_b = dy, None
    for i in reversed(range(DEPTH)):
        kind, mp, fp, h_in, msaved, h1, z1, fsaved, z2 = acts[i]
        tag = "l%d" % i
        dz2, dg8, db8 = _ln_bwd(d_a, d_b, z2, rep["ln2_g"][i], ts=ts, name=tag + "_ln2b")
        grads["ln2_g"][i], grads["ln2_b"][i] = dg8.sum(axis=0), db8.sum(axis=0)
        dx_f, gf = _ffn_bwd(dz2, h1, fp, fsaved, tag + "f", gbuf, i)
        for k, v in gf.items():
            grads[_FFN_KEYS[k]][i] = v
        dz1, dg8, db8 = _ln_bwd(dz2, dx_f, z1, rep["ln1_g"][i], ts=ts, name=tag + "_ln1b")
        grads["ln1_g"][i], grads["ln1_b"][i] = dg8.sum(axis=0), db8.sum(axis=0)
        dx_m, gm = _MIX_BWD[kind](dz1, h_in, mp, msaved, tag + "m", gbuf, i // 3)
        for k, v in gm.items():
            grads[_MIX_KEYS[kind][k]][i // 3] = v
        d_a, d_b = dz1, dx_m
    grad_x = _axpy(d_a, d_b, ts=ts, name="grad_x")
    stacked = {n: jnp.stack([grads[n][j] for j in sorted(grads[n])]) for n in grads}
    return part, grad_x, stacked, gbuf


def kernel(x, positions, ln1_g, ln1_b, ln2_g, ln2_b, ffn_w_up, ffn_conv_w, ffn_conv_b, ffn_w_down, lru_w_in, lru_conv_w, lru_conv_b, lru_w_a, lru_b_a, lru_w_x, lru_b_x, lru_lambda, lru_w_out, mla_w_in, mla_q_norm, mla_kv_norm, mla_w_uq, mla_w_ukv, mla_w_out, ret_w_in, ret_gn_g, ret_w_out, loss_target, m_ln1_g, m_ln1_b, m_ln2_g, m_ln2_b, m_ffn_w_up, m_ffn_conv_w, m_ffn_conv_b, m_ffn_w_down, m_lru_w_in, m_lru_conv_w, m_lru_conv_b, m_lru_w_a, m_lru_b_a, m_lru_w_x, m_lru_b_x, m_lru_lambda, m_lru_w_out, m_mla_w_in, m_mla_q_norm, m_mla_kv_norm, m_mla_w_uq, m_mla_w_ukv, m_mla_w_out, m_ret_w_in, m_ret_gn_g, m_ret_w_out, v_ln1_g, v_ln1_b, v_ln2_g, v_ln2_b, v_ffn_w_up, v_ffn_conv_w, v_ffn_conv_b, v_ffn_w_down, v_lru_w_in, v_lru_conv_w, v_lru_conv_b, v_lru_w_a, v_lru_b_a, v_lru_w_x, v_lru_b_x, v_lru_lambda, v_lru_w_out, v_mla_w_in, v_mla_q_norm, v_mla_kv_norm, v_mla_w_uq, v_mla_w_ukv, v_mla_w_out, v_ret_w_in, v_ret_gn_g, v_ret_w_out):
    w = dict(ln1_g=ln1_g, ln1_b=ln1_b, ln2_g=ln2_g, ln2_b=ln2_b, ffn_w_up=ffn_w_up, ffn_conv_w=ffn_conv_w, ffn_conv_b=ffn_conv_b, ffn_w_down=ffn_w_down, lru_w_in=lru_w_in, lru_conv_w=lru_conv_w, lru_conv_b=lru_conv_b, lru_w_a=lru_w_a, lru_b_a=lru_b_a, lru_w_x=lru_w_x, lru_b_x=lru_b_x, lru_lambda=lru_lambda, lru_w_out=lru_w_out, mla_w_in=mla_w_in, mla_q_norm=mla_q_norm, mla_kv_norm=mla_kv_norm, mla_w_uq=mla_w_uq, mla_w_ukv=mla_w_ukv, mla_w_out=mla_w_out, ret_w_in=ret_w_in, ret_gn_g=ret_gn_g, ret_w_out=ret_w_out)
    m = dict(ln1_g=m_ln1_g, ln1_b=m_ln1_b, ln2_g=m_ln2_g, ln2_b=m_ln2_b, ffn_w_up=m_ffn_w_up, ffn_conv_w=m_ffn_conv_w, ffn_conv_b=m_ffn_conv_b, ffn_w_down=m_ffn_w_down, lru_w_in=m_lru_w_in, lru_conv_w=m_lru_conv_w, lru_conv_b=m_lru_conv_b, lru_w_a=m_lru_w_a, lru_b_a=m_lru_b_a, lru_w_x=m_lru_w_x, lru_b_x=m_lru_b_x, lru_lambda=m_lru_lambda, lru_w_out=m_lru_w_out, mla_w_in=m_mla_w_in, mla_q_norm=m_mla_q_norm, mla_kv_norm=m_mla_kv_norm, mla_w_uq=m_mla_w_uq, mla_w_ukv=m_mla_w_ukv, mla_w_out=m_mla_w_out, ret_w_in=m_ret_w_in, ret_gn_g=m_ret_gn_g, ret_w_out=m_ret_w_out)
    v = dict(ln1_g=v_ln1_g, ln1_b=v_ln1_b, ln2_g=v_ln2_g, ln2_b=v_ln2_b, ffn_w_up=v_ffn_w_up, ffn_conv_w=v_ffn_conv_w, ffn_conv_b=v_ffn_conv_b, ffn_w_down=v_ffn_w_down, lru_w_in=v_lru_w_in, lru_conv_w=v_lru_conv_w, lru_conv_b=v_lru_conv_b, lru_w_a=v_lru_w_a, lru_b_a=v_lru_b_a, lru_w_x=v_lru_w_x, lru_b_x=v_lru_b_x, lru_lambda=v_lru_lambda, lru_w_out=v_lru_w_out, mla_w_in=v_mla_w_in, mla_q_norm=v_mla_q_norm, mla_kv_norm=v_mla_kv_norm, mla_w_uq=v_mla_w_uq, mla_w_ukv=v_mla_w_ukv, mla_w_out=v_mla_w_out, ret_w_in=v_ret_w_in, ret_gn_g=v_ret_gn_g, ret_w_out=v_ret_w_out)
    D = x.shape[-1]
    axis_of = {n: ax for n, ax, _ in SHARDED}
    big = list(BIG_AXIS)
    small_mx = [n for n, _, mx in SHARDED if mx and n not in BIG_AXIS]
    small_vec = [n for n, _, mx in SHARDED if not mx]
    small = small_mx + small_vec

    gathered = _gather_chips([(w[n].astype(BF16), BIG_AXIS[n]) for n in big]
                             + [(_pack([w[n] for n in small_mx], BF16), None), (_pack([w[n] for n in small_vec], F32), None)],
                             "gather_weights")
    full = dict(zip(big, gathered))
    for names, buf in ((small_mx, gathered[-2]), (small_vec, gathered[-1])):
        blocks = _unpack(buf, [w[n].shape for n in names], lead=(N_CHIPS,))
        for n, blk in zip(names, blocks):
            full[n] = jnp.concatenate([blk[s] for s in range(N_CHIPS)], axis=axis_of[n])
    rep = {n: w[n] for n in REPLICATED}

    part, grad_x, grads, gbuf = _local_step(x[0], positions[0], loss_target[0], full, rep)
    loss = lax.psum((0.5 / D) * jnp.sum(part), MESH_AXES)

    g_pack = _pack([jnp.stack(jnp.split(grads[n], N_CHIPS, axis=axis_of[n])) for n in small], F32, lead=(N_CHIPS,))
    recv = _scatter_chips([(gbuf[n][1], BIG_AXIS[n]) for n in big] + [(g_pack, None)], "scatter_grads")
    sums = [_sum_partials(gbuf[n][0], r, BIG_AXIS[n], name="sum_" + n) for n, r in zip(big, recv)]
    sums.append(_sum_partials(g_pack[:, None], recv[-1][:, None], None, name="sum_small"))
    sibs = _swap_sibling(sums, "swap_core_partials")
    res = {kind: {} for kind in "gdmv"}
    for n, p_mine, p_sib in zip(big, sums, sibs):
        for kind, o in zip("gdmv", _adamw(w[n], m[n], v[n], p_mine, p_sib, name="adamw_" + n)):
            res[kind][n] = o
    spack = lambda d: _pack([d[n] for n in small], F32)[None]
    shapes = [w[n].shape for n in small]
    for kind, o in zip("gdmv", _adamw(spack(w), spack(m), spack(v), sums[-1], sibs[-1], name="adamw_small")):
        res[kind].update(zip(small, _unpack(o[0], shapes)))

    r_shapes = [w[n].shape for n in REPLICATED]
    rpack = lambda d: _pack([d[n] for n in REPLICATED], F32, rows=SUBLANES)
    r_sum = _allreduce_small(rpack(grads), "allreduce_replicated")
    r_outs = _adamw(rpack(w)[None], rpack(m)[None], rpack(v)[None], r_sum[None], None, name="adamw_replicated")
    for kind, o in zip("gdmv", r_outs):
        res[kind].update(zip(REPLICATED, _unpack(o[0], r_shapes)))

    return (loss, grad_x[None], *[res["g"][n] for n in WEIGHTS], *[res["d"][n] for n in WEIGHTS],
            *[res["m"][n] for n in WEIGHTS], *[res["v"][n] for n in WEIGHTS])
```

```python
import functools
import math

import numpy as np
import jax
import jax.numpy as jnp
from jax import lax
from jax.experimental import pallas as pl
from jax.experimental.pallas import tpu as pltpu

F32 = jnp.float32
BF16 = jnp.bfloat16

DEPTH = 4
ALPHA = (2.0 * DEPTH) ** 0.25
LN_EPS = 1e-5
RMS_EPS = 1e-6
ROPE_BASE = 10000.0
CHUNK = 64
LRU_C = 8.0
LRU_GROUPS = 4
MLA_HEADS = 16
QK_NOPE, QK_ROPE, V_HEAD = 64, 32, 64
Q_LORA, KV_LORA = 768, 256
RET_HEADS = 4
ADAM_LR, ADAM_B1, ADAM_B2, ADAM_EPS, ADAM_WD, ADAM_STEP = 0.001, 0.9, 0.999, 1e-08, 0.01, 10

LANES = 128
SUBLANES = 8
VMEM_LIMIT = 56 * 1024 * 1024

MESH_AXES = ("x", "y", "c")
N_CHIPS = 4


def _cp(sem):
    return pltpu.CompilerParams(dimension_semantics=sem, vmem_limit_bytes=VMEM_LIMIT)


def _sigmoid(x):
    return 1.0 / (1.0 + jnp.exp(-x))


_GELU_C = math.sqrt(2.0 / math.pi)


def _gelu_parts(x):
    x2 = x * x
    u = _GELU_C * (x + 0.044715 * x * x2)
    t = jnp.tanh(u)
    g = 0.5 * x * (1.0 + t)
    dg = 0.5 * (1.0 + t) + 0.5 * x * (1.0 - t * t) * _GELU_C * (1.0 + 3.0 * 0.044715 * x2)
    return g, dg


def _fold8(v):
    n = v.shape[0] // SUBLANES
    return v.reshape(n, SUBLANES, v.shape[1]).sum(axis=0)


def _dot(a, b):
    return jnp.dot(a, b, preferred_element_type=F32)


def _dot_tb(a, b):
    return lax.dot_general(a, b, (((1,), (1,)), ((), ())), preferred_element_type=F32)


def _dot_ta(a, b):
    return lax.dot_general(a, b, (((0,), (0,)), ((), ())), preferred_element_type=F32)


def _mm(a, b, *, out_dtype, tm, tn, name, a_koff=0):
    M = a.shape[0]
    K, N = b.shape

    def body(a_ref, b_ref, o_ref):
        o_ref[...] = _dot(a_ref[...].astype(BF16), b_ref[...].astype(BF16)).astype(out_dtype)

    return pl.pallas_call(
        body, name=name, grid=(M // tm, N // tn),
        in_specs=[pl.BlockSpec((tm, K), lambda i, j: (i, a_koff)),
                  pl.BlockSpec((K, tn), lambda i, j: (0, j))],
        out_specs=pl.BlockSpec((tm, tn), lambda i, j: (i, j)),
        out_shape=jax.ShapeDtypeStruct((M, N), out_dtype),
        compiler_params=_cp(("parallel", "parallel")),
    )(a, b)


def _mm_tb(pairs, b, *, out_dtype, tm, tk, name):
    M = pairs[0][0].shape[0]
    Kout = b.shape[0]
    n = len(pairs)

    def body(*refs):
        a_refs, b_refs, o_ref = refs[:n], refs[n:2 * n], refs[2 * n]
        acc = None
        for a_ref, b_ref in zip(a_refs, b_refs):
            t = _dot_tb(a_ref[...].astype(BF16), b_ref[...].astype(BF16))
            acc = t if acc is None else acc + t
        o_ref[...] = acc.astype(out_dtype)

    in_specs = [pl.BlockSpec((tm, a.shape[1]), lambda i, j: (i, 0)) for a, _ in pairs]
    for a, c0 in pairs:
        w = a.shape[1]
        assert c0 % w == 0
        in_specs.append(pl.BlockSpec((tk, w), functools.partial(lambda i, j, cb: (j, cb), cb=c0 // w)))
    return pl.pallas_call(
        body, name=name, grid=(M // tm, Kout // tk),
        in_specs=in_specs,
        out_specs=pl.BlockSpec((tm, tk), lambda i, j: (i, j)),
        out_shape=jax.ShapeDtypeStruct((M, Kout), out_dtype),
        compiler_params=_cp(("parallel", "parallel")),
    )(*[a for a, _ in pairs], *[b for _ in pairs])


def _mm_ta(a, b, *, tk, tn, tm, name, a_c0=0, a_w=None, b_c0=0, b_w=None, dest=None):
    M = a.shape[0]
    nm = M // tm
    a_w = a.shape[1] if a_w is None else a_w
    b_w = b.shape[1] if b_w is None else b_w
    assert a_c0 % tk == 0 and b_c0 % tn == 0 and a_w % tk == 0 and b_w % tn == 0

    def body(*refs):
        a_ref, b_ref = refs[0], refs[1]
        o_ref = refs[-1] if dest is None else refs[-2]

        @pl.when(pl.program_id(2) == 0)
        def _():
            o_ref[...] = jnp.zeros_like(o_ref)
        o_ref[...] += _dot_ta(a_ref[...].astype(BF16), b_ref[...].astype(BF16))

        if dest is not None:
            @pl.when(pl.program_id(2) == nm - 1)
            def _():
                refs[-1][...] = o_ref[...].astype(BF16)

    in_specs = [pl.BlockSpec((tm, tk), lambda i, j, m: (m, i + a_c0 // tk)),
                pl.BlockSpec((tm, tn), lambda i, j, m: (m, j + b_c0 // tn))]
    args = [a, b]
    if dest is None:
        out_spec = pl.BlockSpec((tk, tn), lambda i, j, m: (i, j))
        out_shape = jax.ShapeDtypeStruct((a_w, b_w), F32)
        aliases = {}
    else:
        bufs, full_shape, layer, r0, c0 = dest
        assert r0 % tk == 0 and c0 % tn == 0
        spec = pl.BlockSpec((None, tk, tn), lambda i, j, m: (layer, i + r0 // tk, j + c0 // tn))
        out_spec = [spec, spec]
        out_shape = [jax.ShapeDtypeStruct(full_shape, F32), jax.ShapeDtypeStruct(full_shape, BF16)]
        aliases = {}
        if bufs is not None:
            in_specs += [pl.BlockSpec(memory_space=pl.ANY)] * 2
            args += list(bufs)
            aliases = {2: 0, 3: 1}
    return pl.pallas_call(
        body, name=name, grid=(a_w // tk, b_w // tn, nm),
        in_specs=in_specs, out_specs=out_spec, out_shape=out_shape, input_output_aliases=aliases,
        compiler_params=_cp(("parallel", "parallel", "arbitrary")),
    )(*args)


def _grad_into(gbuf, key, full_shape, layer, r0, c0, a, b, **kw):
    gbuf[key] = tuple(_mm_ta(a, b, dest=(gbuf.get(key), full_shape, layer, r0, c0), **kw))


def _ln_fwd(x, mix, g, b, *, ts, name):
    S, D = x.shape

    def body(x_ref, m_ref, g_ref, b_ref, o_ref, z_ref):
        z = ALPHA * x_ref[...] + m_ref[...]
        mu = jnp.mean(z, axis=-1, keepdims=True)
        zc = z - mu
        var = jnp.mean(zc * zc, axis=-1, keepdims=True)
        o_ref[...] = zc * lax.rsqrt(var + LN_EPS) * g_ref[...] + b_ref[...]
        z_ref[...] = z

    row = pl.BlockSpec((ts, D), lambda i: (i, 0))
    vec = pl.BlockSpec((1, D), lambda i: (0, 0))
    return pl.pallas_call(
        body, name=name, grid=(S // ts,),
        in_specs=[row, row, vec, vec], out_specs=[row, row],
        out_shape=[jax.ShapeDtypeStruct((S, D), F32)] * 2,
        compiler_params=_cp(("parallel",)),
    )(x, mix, g.reshape(1, D), b.reshape(1, D))


def _ln_bwd(da, db, z, g, *, ts, name):
    S, D = z.shape
    two = db is not None

    def body(*refs):
        if two:
            da_ref, db_ref, z_ref, g_ref, dz_ref, dg_ref, dbias_ref = refs
            dout = ALPHA * da_ref[...] + db_ref[...]
        else:
            da_ref, z_ref, g_ref, dz_ref, dg_ref, dbias_ref = refs
            dout = da_ref[...]

        @pl.when(pl.program_id(0) == 0)
        def _():
            dg_ref[...] = jnp.zeros_like(dg_ref)
            dbias_ref[...] = jnp.zeros_like(dbias_ref)

        z = z_ref[...]
        mu = jnp.mean(z, axis=-1, keepdims=True)
        zc = z - mu
        var = jnp.mean(zc * zc, axis=-1, keepdims=True)
        rstd = lax.rsqrt(var + LN_EPS)
        xhat = zc * rstd
        dxh = dout * g_ref[...]
        m1 = jnp.mean(dxh, axis=-1, keepdims=True)
        m2 = jnp.mean(dxh * xhat, axis=-1, keepdims=True)
        dz_ref[...] = rstd * (dxh - m1 - xhat * m2)
        dg_ref[...] += _fold8(dout * xhat)
        dbias_ref[...] += _fold8(dout)

    row = pl.BlockSpec((ts, D), lambda i: (i, 0))
    vec = pl.BlockSpec((1, D), lambda i: (0, 0))
    acc = pl.BlockSpec((SUBLANES, D), lambda i: (0, 0))
    args = [da, db, z, g.reshape(1, D)] if two else [da, z, g.reshape(1, D)]
    return pl.pallas_call(
        body, name=name, grid=(S // ts,),
        in_specs=[row] * (3 if two else 2) + [vec],
        out_specs=[row, acc, acc],
        out_shape=[jax.ShapeDtypeStruct((S, D), F32), jax.ShapeDtypeStruct((SUBLANES, D), F32),
                   jax.ShapeDtypeStruct((SUBLANES, D), F32)],
        compiler_params=_cp(("arbitrary",)),
    )(*args)


def _loss_head(y, t, *, ts, name):
    S, D = y.shape

    def body(y_ref, t_ref, dy_ref, p_ref):
        @pl.when(pl.program_id(0) == 0)
        def _():
            p_ref[...] = jnp.zeros_like(p_ref)
        d = y_ref[...] - t_ref[...]
        dy_ref[...] = d * (1.0 / D)
        p_ref[...] += _fold8(d * d)

    row = pl.BlockSpec((ts, D), lambda i: (i, 0))
    acc = pl.BlockSpec((SUBLANES, D), lambda i: (0, 0))
    return pl.pallas_call(
        body, name=name, grid=(S // ts,),
        in_specs=[row, row], out_specs=[row, acc],
        out_shape=[jax.ShapeDtypeStruct((S, D), F32), jax.ShapeDtypeStruct((SUBLANES, D), F32)],
        compiler_params=_cp(("arbitrary",)),
    )(y, t)


def _axpy(a, b, *, ts, name):
    S, D = a.shape

    def body(a_ref, b_ref, o_ref):
        o_ref[...] = ALPHA * a_ref[...] + b_ref[...]

    row = pl.BlockSpec((ts, D), lambda i: (i, 0))
    return pl.pallas_call(
        body, name=name, grid=(S // ts,), in_specs=[row, row], out_specs=row,
        out_shape=jax.ShapeDtypeStruct((S, D), F32), compiler_params=_cp(("parallel",)),
    )(a, b)


def _prev_halo_spec(ts, tc, coff):
    r = ts // SUBLANES
    return pl.BlockSpec((SUBLANES, tc), lambda i, j: (jnp.maximum(i * r - 1, 0), j + coff))


def _fill_prev(buf, halo_ref, cur, i):
    buf[0:SUBLANES, :] = jnp.where(i > 0, halo_ref[...], 0.0)
    buf[SUBLANES:, :] = cur


def _conv_fwd(x, w, b, *, K, ts, tc, x_c0, name):
    S = x.shape[0]
    C = w.shape[1]
    coff = x_c0 // tc

    def body(x_ref, halo_ref, w_ref, b_ref, o_ref, buf):
        _fill_prev(buf, halo_ref, x_ref[...], pl.program_id(0))
        acc = b_ref[...] + w_ref[K - 1:K, :] * x_ref[...]
        for k in range(K - 1):
            acc = acc + w_ref[k:k + 1, :] * buf[pl.ds(SUBLANES - (K - 1) + k, ts), :]
        o_ref[...] = acc

    return pl.pallas_call(
        body, name=name, grid=(S // ts, C // tc),
        in_specs=[pl.BlockSpec((ts, tc), lambda i, j: (i, j + coff)), _prev_halo_spec(ts, tc, coff),
                  pl.BlockSpec((K, tc), lambda i, j: (0, j)), pl.BlockSpec((1, tc), lambda i, j: (0, j))],
        out_specs=pl.BlockSpec((ts, tc), lambda i, j: (i, j)),
        out_shape=jax.ShapeDtypeStruct((S, C), F32),
        scratch_shapes=[pltpu.VMEM((ts + SUBLANES, tc), F32)],
        compiler_params=_cp(("parallel", "parallel")),
    )(x, x, w, b.reshape(1, C))


def _conv_wgrad(dy, x, *, K, ts, tc, x_c0, name):
    S, C = dy.shape
    coff = x_c0 // tc

    def body(dy_ref, x_ref, halo_ref, dw_ref, db_ref, buf):
        i = pl.program_id(1)

        @pl.when(i == 0)
        def _():
            dw_ref[...] = jnp.zeros_like(dw_ref)
            db_ref[...] = jnp.zeros_like(db_ref)

        _fill_prev(buf, halo_ref, x_ref[...], i)
        dy_v = dy_ref[...]
        db_ref[...] += _fold8(dy_v)
        for k in range(K):
            xs = buf[pl.ds(SUBLANES - (K - 1) + k, ts), :]
            dw_ref[k] += _fold8(dy_v * xs)

    r = ts // SUBLANES
    return pl.pallas_call(
        body, name=name, grid=(C // tc, S // ts),
        in_specs=[pl.BlockSpec((ts, tc), lambda j, i: (i, j)),
                  pl.BlockSpec((ts, tc), lambda j, i: (i, j + coff)),
                  pl.BlockSpec((SUBLANES, tc), lambda j, i: (jnp.maximum(i * r - 1, 0), j + coff))],
        out_specs=[pl.BlockSpec((K, SUBLANES, tc), lambda j, i: (0, 0, j)),
                   pl.BlockSpec((SUBLANES, tc), lambda j, i: (0, j))],
        out_shape=[jax.ShapeDtypeStruct((K, SUBLANES, C), F32), jax.ShapeDtypeStruct((SUBLANES, C), F32)],
        scratch_shapes=[pltpu.VMEM((ts + SUBLANES, tc), F32)],
        compiler_params=_cp(("parallel", "arbitrary")),
    )(dy, x, x)


def _conv_bwd(dy, w, *, K, ts, tc, w_c0, out_dtype, name):
    S, C = dy.shape
    nb = S // ts
    r = ts // SUBLANES
    woff = w_c0 // tc

    def body(dy_ref, halo_ref, w_ref, o_ref, buf):
        i = pl.program_id(0)
        buf[0:ts, :] = dy_ref[...]
        buf[ts:, :] = jnp.where(i < nb - 1, halo_ref[...], 0.0)
        acc = w_ref[K - 1:K, :] * dy_ref[...]
        for k in range(K - 1):
            acc = acc + w_ref[k:k + 1, :] * buf[pl.ds(K - 1 - k, ts), :]
        o_ref[...] = acc.astype(out_dtype)

    return pl.pallas_call(
        body, name=name, grid=(nb, C // tc),
        in_specs=[pl.BlockSpec((ts, tc), lambda i, j: (i, j)),
                  pl.BlockSpec((SUBLANES, tc), lambda i, j: (jnp.minimum((i + 1) * r, nb * r - 1), j)),
                  pl.BlockSpec((K, tc), lambda i, j: (0, j + woff))],
        out_specs=pl.BlockSpec((ts, tc), lambda i, j: (i, j)),
        out_shape=jax.ShapeDtypeStruct((S, C), out_dtype),
        scratch_shapes=[pltpu.VMEM((ts + SUBLANES, tc), F32)],
        compiler_params=_cp(("parallel", "parallel")),
    )(dy, dy, w)


HALO16 = 16
FFN_UNROLL = 4


def _ffn_taps_w(w_ref, cs):
    return [w_ref[k:k + 1, cs] for k in range(3)]


def _ffn_taps8(prev, cur):
    row = lax.broadcasted_iota(jnp.int32, cur.shape, 0)
    return (jnp.where(row < 2, pltpu.roll(prev, 2, 0), pltpu.roll(cur, 2, 0)),
            jnp.where(row < 1, pltpu.roll(prev, 1, 0), pltpu.roll(cur, 1, 0)), cur)


def _ffn_conv8(taps, w, b):
    return b + w[0] * taps[0] + w[1] * taps[1] + w[2] * taps[2]


def _ffn_conv8_t(dh, dh_next, w):
    row = lax.broadcasted_iota(jnp.int32, dh.shape, 0)
    s1 = jnp.where(row < SUBLANES - 1, pltpu.roll(dh, SUBLANES - 1, 0), pltpu.roll(dh_next, SUBLANES - 1, 0))
    s2 = jnp.where(row < SUBLANES - 2, pltpu.roll(dh, SUBLANES - 2, 0), pltpu.roll(dh_next, SUBLANES - 2, 0))
    return w[2] * dh + w[1] * s1 + w[0] * s2


def _ffn_mid_specs(ts, tc, nf, nb, with_next):
    r = ts // HALO16
    specs = []
    for off in (0, nf):
        specs.append(pl.BlockSpec((ts, tc), functools.partial(lambda j, i, o: (i, j + o), o=off)))
        specs.append(pl.BlockSpec((HALO16, tc), functools.partial(lambda j, i, o: (jnp.maximum(i * r - 1, 0), j + o), o=off)))
        if with_next:
            specs.append(pl.BlockSpec(
                (HALO16, tc), functools.partial(lambda j, i, o: (jnp.minimum((i + 1) * r, nb * r - 1), j + o), o=off)))
    for rows in (3, 1):
        for off in (0, nf):
            specs.append(pl.BlockSpec((rows, tc), functools.partial(lambda j, i, o: (0, j + o), o=off)))
    return specs


def _ffn_mid_fwd(hpre, w, b, *, ts, tc, name):
    S, F2 = hpre.shape
    F = F2 // 2
    nf = F // tc

    def body(g_ref, gp_ref, u_ref, up_ref, wg_ref, wu_ref, bg_ref, bu_ref, o_ref, gbuf, ubuf, obuf):
        first = pl.program_id(1) == 0
        for buf, prev, cur in ((gbuf, gp_ref, g_ref), (ubuf, up_ref, u_ref)):
            buf[0:HALO16, :] = jnp.where(first, 0.0, prev[...].astype(F32))
            buf[HALO16:, :] = cur[...].astype(F32)
        for lt in range(tc // LANES):
            cs = slice(lt * LANES, (lt + 1) * LANES)
            wg, wu = _ffn_taps_w(wg_ref, cs), _ffn_taps_w(wu_ref, cs)
            bg, bu = bg_ref[:, cs], bu_ref[:, cs]

            def step(c, carry):
                a_g, a_u = carry
                for un in range(FFN_UNROLL):
                    r0 = pl.multiple_of(c * (FFN_UNROLL * SUBLANES), SUBLANES) + un * SUBLANES
                    b_g = gbuf[pl.ds(HALO16 + r0, SUBLANES), cs]
                    b_u = ubuf[pl.ds(HALO16 + r0, SUBLANES), cs]
                    gel, _ = _gelu_parts(_ffn_conv8(_ffn_taps8(a_g, b_g), wg, bg))
                    obuf[pl.ds(r0, SUBLANES), cs] = gel * _ffn_conv8(_ffn_taps8(a_u, b_u), wu, bu)
                    a_g, a_u = b_g, b_u
                return a_g, a_u

            lax.fori_loop(0, ts // (FFN_UNROLL * SUBLANES), step,
                          (gbuf[HALO16 - SUBLANES:HALO16, cs], ubuf[HALO16 - SUBLANES:HALO16, cs]))
        o_ref[...] = obuf[...].astype(BF16)

    b2 = b.reshape(1, F2)
    return pl.pallas_call(
        body, name=name, grid=(nf, S // ts),
        in_specs=_ffn_mid_specs(ts, tc, nf, S // ts, False),
        out_specs=pl.BlockSpec((ts, tc), lambda j, i: (i, j)),
        out_shape=jax.ShapeDtypeStruct((S, F), BF16),
        scratch_shapes=[pltpu.VMEM((ts + HALO16, tc), F32)] * 2 + [pltpu.VMEM((ts, tc), F32)],
        compiler_params=_cp(("parallel", "parallel")),
    )(hpre, hpre, hpre, hpre, w, w, b2, b2)


def _ffn_mid_bwd(hpre, da, w, b, *, ts, tc, name):
    S, F2 = hpre.shape
    F = F2 // 2
    nf = F // tc
    nb = S // ts
    r = ts // HALO16
    nch = ts // SUBLANES

    def body(g_ref, gp_ref, gn_ref, u_ref, up_ref, un_ref, wg_ref, wu_ref, bg_ref, bu_ref, da_ref, dan_ref,
             dpg_ref, dpu_ref, dwg_ref, dwu_ref, dbg_ref, dbu_ref, gbuf, ubuf, dabuf, pgbuf, pubuf):
        i = pl.program_id(1)

        @pl.when(i == 0)
        def _():
            for ref in (dwg_ref, dwu_ref, dbg_ref, dbu_ref):
                ref[...] = jnp.zeros_like(ref)

        for buf, prev, cur, nxt in ((gbuf, gp_ref, g_ref, gn_ref), (ubuf, up_ref, u_ref, un_ref)):
            buf[0:HALO16, :] = jnp.where(i == 0, 0.0, prev[...].astype(F32))
            buf[HALO16:HALO16 + ts, :] = cur[...].astype(F32)
            buf[HALO16 + ts:, :] = nxt[...].astype(F32)
        dabuf[0:ts, :] = da_ref[...].astype(F32)
        dabuf[ts:, :] = jnp.where(i == nb - 1, 0.0, dan_ref[...].astype(F32))

        for lt in range(tc // LANES):
            cs = slice(lt * LANES, (lt + 1) * LANES)
            wg, wu = _ffn_taps_w(wg_ref, cs), _ffn_taps_w(wu_ref, cs)
            bg, bu = bg_ref[:, cs], bu_ref[:, cs]

            def piece(r0, a_g, a_u):
                b_g = gbuf[pl.ds(HALO16 + r0, SUBLANES), cs]
                b_u = ubuf[pl.ds(HALO16 + r0, SUBLANES), cs]
                tg, tu = _ffn_taps8(a_g, b_g), _ffn_taps8(a_u, b_u)
                gel, dgel = _gelu_parts(_ffn_conv8(tg, wg, bg))
                da_v = dabuf[pl.ds(r0, SUBLANES), cs]
                return da_v * _ffn_conv8(tu, wu, bu) * dgel, da_v * gel, tg, tu, b_g, b_u

            def step(c, carry):
                a_g, a_u, pdg, pdu, acc = carry
                for un in range(FFN_UNROLL):
                    r0 = pl.multiple_of(c * (FFN_UNROLL * SUBLANES), SUBLANES) + un * SUBLANES
                    dg, du, tg, tu, a_g, a_u = piece(r0, a_g, a_u)
                    pgbuf[pl.ds(r0, SUBLANES), cs] = _ffn_conv8_t(pdg, dg, wg)
                    pubuf[pl.ds(r0, SUBLANES), cs] = _ffn_conv8_t(pdu, du, wu)
                    acc = (tuple(a + dg * t for a, t in zip(acc[0], tg)), tuple(a + du * t for a, t in zip(acc[1], tu)),
                           acc[2] + dg, acc[3] + du)
                    pdg, pdu = dg, du
                return a_g, a_u, pdg, pdu, acc

            zero = jnp.zeros((SUBLANES, LANES), F32)
            a_g, a_u, pdg, pdu, acc = lax.fori_loop(
                0, nch // FFN_UNROLL, step,
                (gbuf[HALO16 - SUBLANES:HALO16, cs], ubuf[HALO16 - SUBLANES:HALO16, cs], zero, zero,
                 ((zero,) * 3, (zero,) * 3, zero, zero)))
            dg, du, _, _, _, _ = piece(ts, a_g, a_u)
            pgbuf[ts:ts + SUBLANES, cs] = _ffn_conv8_t(pdg, dg, wg)
            pubuf[ts:ts + SUBLANES, cs] = _ffn_conv8_t(pdu, du, wu)
            for k in range(3):
                dwg_ref[k, :, cs] += acc[0][k]
                dwu_ref[k, :, cs] += acc[1][k]
            dbg_ref[:, cs] += acc[2]
            dbu_ref[:, cs] += acc[3]
        dpg_ref[...] = pgbuf[SUBLANES:, :].astype(BF16)
        dpu_ref[...] = pubuf[SUBLANES:, :].astype(BF16)

    b2 = b.reshape(1, F2)
    blk = pl.BlockSpec((ts, tc), lambda j, i: (i, j))
    nxt = pl.BlockSpec((HALO16, tc), lambda j, i: (jnp.minimum((i + 1) * r, nb * r - 1), j))
    in_specs = _ffn_mid_specs(ts, tc, nf, nb, True) + [blk, nxt]
    out_specs = [blk, blk,
                 pl.BlockSpec((3, SUBLANES, tc), lambda j, i: (0, 0, j)), pl.BlockSpec((3, SUBLANES, tc), lambda j, i: (0, 0, j)),
                 pl.BlockSpec((SUBLANES, tc), lambda j, i: (0, j)), pl.BlockSpec((SUBLANES, tc), lambda j, i: (0, j))]
    return pl.pallas_call(
        body, name=name, grid=(nf, nb),
        in_specs=in_specs, out_specs=out_specs,
        out_shape=[jax.ShapeDtypeStruct((S, F), BF16)] * 2 + [jax.ShapeDtypeStruct((3, SUBLANES, F), F32)] * 2
                  + [jax.ShapeDtypeStruct((SUBLANES, F), F32)] * 2,
        scratch_shapes=[pltpu.VMEM((ts + 2 * HALO16, tc), F32)] * 2 + [pltpu.VMEM((ts + HALO16, tc), F32)]
                       + [pltpu.VMEM((ts + SUBLANES, tc), F32)] * 2,
        compiler_params=_cp(("parallel", "arbitrary")),
    )(hpre, hpre, hpre, hpre, hpre, hpre, w, w, b2, b2, da, da)


def _expm1(x):
    u = jnp.exp(x)
    um1 = u - 1.0
    safe = jnp.where(um1 == 0.0, 1.0, jnp.log(u))
    r = jnp.where(um1 == 0.0, x, um1 * x / safe)
    return jnp.where(x < -30.0, -1.0, r)


def _softplus(z):
    return jnp.maximum(z, 0.0) + jnp.log1p(jnp.exp(-jnp.abs(z)))


def _lru_gates(u, wa_ref, ba_ref, wx_ref, bx_ref, lam_ref):
    ub = u.astype(BF16)
    r = _sigmoid(_dot(ub, wa_ref[...]) + ba_ref[...])
    ig = _sigmoid(_dot(ub, wx_ref[...]) + bx_ref[...])
    sp = _softplus(-lam_ref[...])
    la = -LRU_C * r * sp
    a = jnp.exp(la)
    mult = jnp.sqrt(-_expm1(2.0 * la))
    return ub, r, ig, sp, a, mult


def _lru_fwd(u, proj, w_a, b_a, w_x, b_x, lam, *, ts, name):
    S, W = u.shape
    G = LRU_GROUPS
    gw = W // G
    nt = ts // SUBLANES

    def body(u_ref, gb_ref, wa_ref, ba_ref, wx_ref, bx_ref, lam_ref, y_ref, h_ref, a_buf, b_buf, carry):
        @pl.when(pl.program_id(0) == 0)
        def _():
            carry[...] = jnp.zeros_like(carry)

        for g in range(G):
            gs = slice(g * gw, (g + 1) * gw)
            u_v = u_ref[:, gs]
            _, _, ig, _, a, mult = _lru_gates(u_v, wa_ref.at[g], ba_ref.at[g], wx_ref.at[g], bx_ref.at[g], lam_ref.at[g])
            a_buf[:, gs] = a
            b_buf[:, gs] = mult * ig * u_v
        row = lax.broadcasted_iota(jnp.int32, (SUBLANES, W), 0)

        def tile(k, c):
            r0 = pl.multiple_of(k * SUBLANES, SUBLANES)
            A = a_buf[pl.ds(r0, SUBLANES), :]
            B = b_buf[pl.ds(r0, SUBLANES), :]
            for d in (1, 2, 4):
                m = row >= d
                B = jnp.where(m, A * pltpu.roll(B, d, 0) + B, B)
                A = jnp.where(m, A * pltpu.roll(A, d, 0), A)
            h = A * c + B
            h_ref[pl.ds(r0, SUBLANES), :] = h
            return h[SUBLANES - 1:SUBLANES, :]

        carry[...] = lax.fori_loop(0, nt, tile, carry[...])
        gel, _ = _gelu_parts(gb_ref[...])
        y_ref[...] = (gel * h_ref[...]).astype(BF16)

    blk = pl.BlockSpec((ts, W), lambda i: (i, 0))
    wsp = pl.BlockSpec((G, gw, gw), lambda i: (0, 0, 0))
    vsp = pl.BlockSpec((G, 1, gw), lambda i: (0, 0, 0))
    return pl.pallas_call(
        body, name=name, grid=(S // ts,),
        in_specs=[blk, blk, wsp, vsp, wsp, vsp, vsp],
        out_specs=[blk, blk],
        out_shape=[jax.ShapeDtypeStruct((S, W), BF16), jax.ShapeDtypeStruct((S, W), F32)],
        scratch_shapes=[pltpu.VMEM((ts, W), F32), pltpu.VMEM((ts, W), F32), pltpu.VMEM((1, W), F32)],
        compiler_params=_cp(("arbitrary",)),
    )(u, proj, w_a, b_a.reshape(G, 1, gw), w_x, b_x.reshape(G, 1, gw), lam.reshape(G, 1, gw))


def _lru_bwd(dy, u, proj, h, w_a, b_a, w_x, b_x, lam, *, ts, name):
    S, W = u.shape
    G = LRU_GROUPS
    gw = W // G
    nt = ts // SUBLANES
    nb = S // ts
    r8 = ts // SUBLANES

    def body(dy_ref, u_ref, gb_ref, h_ref, hh_ref, wa_ref, ba_ref, wx_ref, bx_ref, lam_ref,
             dgb_ref, du_ref, dwa_ref, dwx_ref, dba_ref, dbx_ref, dlam_ref,
             a_buf, q_buf, p_buf, hbuf, carry):
        i = pl.program_id(0)
        ib = nb - 1 - i

        @pl.when(i == 0)
        def _():
            carry[...] = jnp.zeros_like(carry)
            dwa_ref[...] = jnp.zeros_like(dwa_ref)
            dwx_ref[...] = jnp.zeros_like(dwx_ref)
            dba_ref[...] = jnp.zeros_like(dba_ref)
            dbx_ref[...] = jnp.zeros_like(dbx_ref)
            dlam_ref[...] = jnp.zeros_like(dlam_ref)

        def gates(g):
            gs = slice(g * gw, (g + 1) * gw)
            return gs, _lru_gates(u_ref[:, gs], wa_ref.at[g], ba_ref.at[g], wx_ref.at[g], bx_ref.at[g], lam_ref.at[g])

        for g in range(G):
            gs, (_, _, _, _, a, _) = gates(g)
            gel, dgel = _gelu_parts(gb_ref[:, gs])
            dy_v = dy_ref[:, gs]
            dgb_ref[:, gs] = (dy_v * h_ref[:, gs] * dgel).astype(BF16)
            a_buf[:, gs] = a
            q_buf[:, gs] = a * (dy_v * gel)
        row = lax.broadcasted_iota(jnp.int32, (SUBLANES, W), 0)

        def tile(kk, c):
            r0 = pl.multiple_of((nt - 1 - kk) * SUBLANES, SUBLANES)
            A = a_buf[pl.ds(r0, SUBLANES), :]
            B = q_buf[pl.ds(r0, SUBLANES), :]
            for d in (1, 2, 4):
                m = row < SUBLANES - d
                B = jnp.where(m, A * pltpu.roll(B, SUBLANES - d, 0) + B, B)
                A = jnp.where(m, A * pltpu.roll(A, SUBLANES - d, 0), A)
            P = A * c + B
            p_buf[pl.ds(r0, SUBLANES), :] = jnp.where(row == SUBLANES - 1, c, pltpu.roll(P, SUBLANES - 1, 0))
            return P[0:1, :]

        carry[...] = lax.fori_loop(0, nt, tile, carry[...])
        hbuf[0:SUBLANES, :] = jnp.where(ib > 0, hh_ref[...], 0.0)
        hbuf[SUBLANES:, :] = h_ref[...]
        for g in range(G):
            gs, (ub, r, ig, sp, a, mult) = gates(g)
            u_v = u_ref[:, gs]
            gel, _ = _gelu_parts(gb_ref[:, gs])
            Gt = dy_ref[:, gs] * gel + p_buf[:, gs]
            hprev = hbuf[pl.ds(SUBLANES - 1, ts), gs]
            da = Gt * hprev
            dmult = Gt * (ig * u_v)
            dla = da * a - dmult * (a * a) / mult
            dr = dla * (-LRU_C * sp)
            dlam_ref[g] += _fold8(dla * (LRU_C * r)) * _sigmoid(-lam_ref[g])
            dig = Gt * mult * u_v
            dzr = dr * r * (1.0 - r)
            dzi = dig * ig * (1.0 - ig)
            dzr_b = dzr.astype(BF16)
            dzi_b = dzi.astype(BF16)
            du_ref[:, gs] = Gt * mult * ig + _dot_tb(dzr_b, wa_ref[g]) + _dot_tb(dzi_b, wx_ref[g])
            dwa_ref[g] += _dot_ta(ub, dzr_b)
            dwx_ref[g] += _dot_ta(ub, dzi_b)
            dba_ref[g] += _fold8(dzr)
            dbx_ref[g] += _fold8(dzi)

    rblk = pl.BlockSpec((ts, W), lambda i: (nb - 1 - i, 0))
    halo = pl.BlockSpec((SUBLANES, W), lambda i: (jnp.maximum((nb - 1 - i) * r8 - 1, 0), 0))
    wsp = pl.BlockSpec((G, gw, gw), lambda i: (0, 0, 0))
    vsp = pl.BlockSpec((G, 1, gw), lambda i: (0, 0, 0))
    acc8 = pl.BlockSpec((G, SUBLANES, gw), lambda i: (0, 0, 0))
    return pl.pallas_call(
        body, name=name, grid=(nb,),
        in_specs=[rblk, rblk, rblk, rblk, halo, wsp, vsp, wsp, vsp, vsp],
        out_specs=[rblk, rblk, wsp, wsp, acc8, acc8, acc8],
        out_shape=[jax.ShapeDtypeStruct((S, W), BF16), jax.ShapeDtypeStruct((S, W), F32),
                   jax.ShapeDtypeStruct((G, gw, gw), F32), jax.ShapeDtypeStruct((G, gw, gw), F32),
                   jax.ShapeDtypeStruct((G, SUBLANES, gw), F32), jax.ShapeDtypeStruct((G, SUBLANES, gw), F32),
                   jax.ShapeDtypeStruct((G, SUBLANES, gw), F32)],
        scratch_shapes=[pltpu.VMEM((ts, W), F32)] * 3 + [pltpu.VMEM((ts + SUBLANES, W), F32),
                                                         pltpu.VMEM((1, W), F32)],
        compiler_params=_cp(("arbitrary",)),
    )(dy, u, proj, h, h, w_a, b_a.reshape(G, 1, gw), w_x, b_x.reshape(G, 1, gw), lam.reshape(G, 1, gw))


def _rt(S, pref):
    return min(S, pref)


def _lru_mixer_fwd(x, p, tag):
    S, D = x.shape
    W = p["w_out"].shape[0]
    ts = _rt(S, 512)
    proj = _mm(x, p["w_in"], out_dtype=F32, tm=ts, tn=2 * W, name=tag + "_proj")
    u = _conv_fwd(proj, p["conv_w"], p["conv_b"], K=4, ts=ts, tc=512, x_c0=W, name=tag + "_conv")
    y, h = _lru_fwd(u, proj, p["w_a"], p["b_a"], p["w_x"], p["b_x"], p["lam"], ts=ts, name=tag + "_scan")
    mix = _mm(y, p["w_out"], out_dtype=F32, tm=ts, tn=D, name=tag + "_out")
    return mix, (proj, u, h, y)


def _lru_mixer_bwd(dmix, x, p, saved, tag, gbuf, j):
    proj, u, h, y = saved
    S, D = x.shape
    W = p["w_out"].shape[0]
    ts = _rt(S, 512)
    g = {}
    dy = _mm_tb([(dmix, 0)], p["w_out"], out_dtype=F32, tm=ts, tk=W, name=tag + "_dy")
    n_lru = (DEPTH + 2) // 3
    _grad_into(gbuf, "lru_w_out", (n_lru, W, D), j, 0, 0, y, dmix, tk=W, tn=D, tm=ts, name=tag + "_dwout")
    dgb, du, g["w_a"], g["w_x"], dba8, dbx8, dlam8 = _lru_bwd(
        dy, u, proj, h, p["w_a"], p["b_a"], p["w_x"], p["b_x"], p["lam"], ts=ts, name=tag + "_scanb")
    g["b_a"] = dba8.sum(axis=1)
    g["b_x"] = dbx8.sum(axis=1)
    g["lam"] = dlam8.sum(axis=1).reshape(-1)
    dcw8, dcb8 = _conv_wgrad(du, proj, K=4, ts=ts, tc=512, x_c0=W, name=tag + "_convw")
    g["conv_w"] = dcw8.sum(axis=1)
    g["conv_b"] = dcb8.sum(axis=0)
    drnn = _conv_bwd(du, p["conv_w"], K=4, ts=ts, tc=512, w_c0=0, out_dtype=BF16, name=tag + "_convb")
    dx = _mm_tb([(dgb, 0), (drnn, W)], p["w_in"], out_dtype=F32, tm=ts, tk=D, name=tag + "_dx")
    _grad_into(gbuf, "lru_w_in", (n_lru, D, 2 * W), j, 0, 0, x, dgb, tk=D, tn=W, tm=ts, name=tag + "_dwin_g")
    _grad_into(gbuf, "lru_w_in", (n_lru, D, 2 * W), j, 0, W, x, drnn, tk=D, tn=W, tm=ts, name=tag + "_dwin_r")
    return dx, g


def _ffn_fwd(x, p, tag):
    S, D = x.shape
    F = p["w_down"].shape[0]
    ts = _rt(S, 512)
    tc = F // 2
    hpre = _mm(x, p["w_up"], out_dtype=BF16, tm=ts, tn=2 * F, name=tag + "_up")
    a = _ffn_mid_fwd(hpre, p["conv_w"], p["conv_b"], ts=_rt(S, 256), tc=tc, name=tag + "_mid")
    f = _mm(a, p["w_down"], out_dtype=F32, tm=ts, tn=D, name=tag + "_down")
    return f, (hpre, a)


def _ffn_bwd(df, x, p, saved, tag, gbuf, i):
    hpre, a = saved
    S, D = x.shape
    F = p["w_down"].shape[0]
    ts = _rt(S, 512)
    tw = _rt(S, 1024)
    tc = F // 2
    g = {}
    da = _mm_tb([(df, 0)], p["w_down"], out_dtype=BF16, tm=ts, tk=F, name=tag + "_da")
    _grad_into(gbuf, "ffn_w_down", (DEPTH, F, D), i, 0, 0, a, df, tk=tc, tn=D, tm=tw, name=tag + "_dwdown")
    dpg, dpu, dwg8, dwu8, dbg8, dbu8 = _ffn_mid_bwd(hpre, da, p["conv_w"], p["conv_b"], ts=_rt(S, 256), tc=tc,
                                                    name=tag + "_midb")
    g["conv_w"] = jnp.concatenate([dwg8.sum(axis=1), dwu8.sum(axis=1)], axis=1)
    g["conv_b"] = jnp.concatenate([dbg8.sum(axis=0), dbu8.sum(axis=0)], axis=0)
    dx = _mm_tb([(dpg, 0), (dpu, F)], p["w_up"], out_dtype=F32, tm=ts, tk=D, name=tag + "_dx")
    _grad_into(gbuf, "ffn_w_up", (DEPTH, D, 2 * F), i, 0, 0, x, dpg, tk=D, tn=tc, tm=tw, name=tag + "_dwup_g")
    _grad_into(gbuf, "ffn_w_up", (DEPTH, D, 2 * F), i, 0, F, x, dpu, tk=D, tn=tc, tm=tw, name=tag + "_dwup_u")
    return dx, g


HEAD_SLOT = LANES
MLA_SCALE = (QK_NOPE + QK_ROPE) ** -0.5
NEG_BIG = -1e30
ATTN_BLOCK = 1024


def _rms_fwd(x, g, *, c0, ts, name):
    S = x.shape[0]
    w = g.shape[0]

    def body(x_ref, g_ref, o_ref):
        xv = x_ref[...]
        rstd = lax.rsqrt(jnp.mean(xv * xv, axis=-1, keepdims=True) + RMS_EPS)
        o_ref[...] = (xv * rstd * g_ref[...]).astype(BF16)

    return pl.pallas_call(
        body, name=name, grid=(S // ts,),
        in_specs=[pl.BlockSpec((ts, w), lambda i: (i, c0 // w)), pl.BlockSpec((1, w), lambda i: (0, 0))],
        out_specs=pl.BlockSpec((ts, w), lambda i: (i, 0)),
        out_shape=jax.ShapeDtypeStruct((S, w), BF16), compiler_params=_cp(("parallel",)),
    )(x, g.reshape(1, w))


def _rms_bwd(dy, x, g, *, c0, ts, name):
    S = x.shape[0]
    w = g.shape[0]

    def body(dy_ref, x_ref, g_ref, dx_ref, dg_ref):
        @pl.when(pl.program_id(0) == 0)
        def _():
            dg_ref[...] = jnp.zeros_like(dg_ref)
        xv = x_ref[...]
        dyv = dy_ref[...]
        rstd = lax.rsqrt(jnp.mean(xv * xv, axis=-1, keepdims=True) + RMS_EPS)
        dyg = dyv * g_ref[...]
        m = jnp.mean(dyg * xv, axis=-1, keepdims=True)
        dx_ref[...] = (rstd * (dyg - xv * (rstd * rstd) * m)).astype(BF16)
        dg_ref[...] += _fold8(dyv * xv * rstd)

    return pl.pallas_call(
        body, name=name, grid=(S // ts,),
        in_specs=[pl.BlockSpec((ts, w), lambda i: (i, 0)), pl.BlockSpec((ts, w), lambda i: (i, c0 // w)),
                  pl.BlockSpec((1, w), lambda i: (0, 0))],
        out_specs=[pl.BlockSpec((ts, w), lambda i: (i, 0)), pl.BlockSpec((SUBLANES, w), lambda i: (0, 0))],
        out_shape=[jax.ShapeDtypeStruct((S, w), BF16), jax.ShapeDtypeStruct((SUBLANES, w), F32)],
        compiler_params=_cp(("arbitrary",)),
    )(dy, x, g.reshape(1, w))


def _mla_tables(pos):
    S = pos.shape[0]
    half = QK_ROPE // 2
    inv_freq = ROPE_BASE ** (-jnp.arange(0, QK_ROPE, 2, dtype=F32) / QK_ROPE)
    ang = pos.astype(F32)[:, None] * inv_freq
    cos, sin = jnp.cos(ang), jnp.sin(ang)
    z = lambda n: jnp.zeros((S, n), F32)
    pad = HEAD_SLOT - QK_NOPE - QK_ROPE
    c = jnp.concatenate([jnp.ones((S, QK_NOPE), F32), cos, cos, z(pad)], axis=1)
    s1 = jnp.concatenate([z(QK_NOPE), -sin, z(half), z(pad)], axis=1)
    s2 = jnp.concatenate([z(QK_NOPE), z(half), sin, z(pad)], axis=1)
    return c, s1, s2


def _mla_prep_fwd(qraw, kvraw, proj, tabs, *, kpe_c0, ts, name):
    S = qraw.shape[0]
    H = MLA_HEADS
    half = QK_ROPE // 2

    def body(q_ref, kn_ref, kpe_ref, c_ref, s1_ref, s2_ref, qo_ref, ko_ref):
        c, s1, s2 = c_ref[...], s1_ref[...], s2_ref[...]

        def rope(v):
            return v * c + pltpu.roll(v, HEAD_SLOT - half, 1) * s1 + pltpu.roll(v, half, 1) * s2

        kpe_r = rope(kpe_ref[...])
        for h in range(H):
            sl = slice(h * HEAD_SLOT, (h + 1) * HEAD_SLOT)
            qo_ref[:, sl] = (rope(q_ref[:, sl]) * MLA_SCALE).astype(BF16)
            ko_ref[:, sl] = (kn_ref[:, sl] + kpe_r).astype(BF16)

    wide = pl.BlockSpec((ts, H * HEAD_SLOT), lambda i: (i, 0))
    tab = pl.BlockSpec((ts, HEAD_SLOT), lambda i: (i, 0))
    return pl.pallas_call(
        body, name=name, grid=(S // ts,),
        in_specs=[wide, wide, pl.BlockSpec((ts, HEAD_SLOT), lambda i: (i, kpe_c0 // HEAD_SLOT)), tab, tab, tab],
        out_specs=[wide, wide],
        out_shape=[jax.ShapeDtypeStruct((S, H * HEAD_SLOT), BF16)] * 2,
        compiler_params=_cp(("parallel",)),
    )(qraw, kvraw, proj, *tabs)


def _mla_prep_bwd(dq, dk, dv, tabs, *, ts, name):
    S = dk.shape[0]
    H = MLA_HEADS
    half = QK_ROPE // 2
    kw = H * HEAD_SLOT
    vw = H * V_HEAD

    def body(dq_ref, dk_ref, dv_ref, c_ref, s1_ref, s2_ref, dqr_ref, dkv_ref, dkpe_ref):
        c, s1, s2 = c_ref[...], s1_ref[...], s2_ref[...]

        def rope_t(g):
            return g * c + pltpu.roll(g * s1, half, 1) + pltpu.roll(g * s2, HEAD_SLOT - half, 1)

        gsum = jnp.zeros((ts, HEAD_SLOT), F32)
        for h in range(H):
            sl = slice(h * HEAD_SLOT, (h + 1) * HEAD_SLOT)
            hs = slice((h % 2) * HEAD_SLOT, (h % 2 + 1) * HEAD_SLOT)
            dqr_ref[:, sl] = (rope_t(dq_ref[h // 2, :, hs]) * MLA_SCALE).astype(BF16)
            dkh = dk_ref[:, sl]
            dkv_ref[:, sl] = dkh.astype(BF16)
            gsum = gsum + dkh
        dkv_ref[:, kw:] = dv_ref[...].astype(BF16)
        lane = lax.broadcasted_iota(jnp.int32, (ts, HEAD_SLOT), 1)
        pe = jnp.logical_and(lane >= QK_NOPE, lane < QK_NOPE + QK_ROPE)
        dkpe_ref[...] = rope_t(jnp.where(pe, gsum, 0.0)).astype(BF16)

    tab = pl.BlockSpec((ts, HEAD_SLOT), lambda i: (i, 0))
    return pl.pallas_call(
        body, name=name, grid=(S // ts,),
        in_specs=[pl.BlockSpec((H // 2, ts, 2 * HEAD_SLOT), lambda i: (0, i, 0)), pl.BlockSpec((ts, kw), lambda i: (i, 0)),
                  pl.BlockSpec((ts, vw), lambda i: (i, 0)), tab, tab, tab],
        out_specs=[pl.BlockSpec((ts, kw), lambda i: (i, 0)), pl.BlockSpec((ts, kw + vw), lambda i: (i, 0)), tab],
        out_shape=[jax.ShapeDtypeStruct((S, kw), BF16), jax.ShapeDtypeStruct((S, kw + vw), BF16),
                   jax.ShapeDtypeStruct((S, HEAD_SLOT), BF16)],
        compiler_params=_cp(("parallel",)),
    )(dq, dk, dv, *tabs)


def _attn_pairs(nb, kv_outer):
    if kv_outer:
        pr = [(i, j) for j in range(nb) for i in range(j, nb)]
    else:
        pr = [(i, j) for i in range(nb) for j in range(i + 1)]
    return (jnp.asarray(np.array([p[0] for p in pr], np.int32)), jnp.asarray(np.array([p[1] for p in pr], np.int32)))


def _attn_scores(q_ref, k_ref, hh, diag, T):
    sl = slice(hh * HEAD_SLOT, (hh + 1) * HEAD_SLOT)
    s = _dot_tb(q_ref[:, sl], k_ref[:, sl])
    if not diag:
        return s
    row = lax.broadcasted_iota(jnp.int32, (T, T), 0) // CHUNK
    col = lax.broadcasted_iota(jnp.int32, (T, T), 1) // CHUNK
    return jnp.where(col <= row, s, NEG_BIG)


def _on_block_kind(i, j, step):
    @pl.when(i == j)
    def _():
        step(True)

    @pl.when(i != j)
    def _():
        step(False)


def _attn_fwd(q, k, kvraw, *, T, name):
    S = q.shape[0]
    NP = MLA_HEADS // 2
    nb = S // T
    ii, jj = _attn_pairs(nb, kv_outer=False)
    v_c0 = MLA_HEADS * HEAD_SLOT // LANES

    def body(ii_ref, jj_ref, q_ref, k_ref, v_ref, o_ref, lse_ref, m_sc, l_sc, acc_sc):
        t = pl.program_id(1)
        i, j = ii_ref[t], jj_ref[t]

        @pl.when(j == 0)
        def _():
            m_sc[...] = jnp.full_like(m_sc, NEG_BIG)
            l_sc[...] = jnp.zeros_like(l_sc)
            acc_sc[...] = jnp.zeros_like(acc_sc)

        lo = lax.broadcasted_iota(jnp.int32, (T, LANES), 1) < V_HEAD
        top = lax.broadcasted_iota(jnp.int32, (LANES, T), 0) < V_HEAD

        def step(diag):
            v = v_ref[...].astype(BF16)
            vh = (jnp.where(lo, v, jnp.zeros_like(v)), jnp.where(lo, jnp.zeros_like(v), v))
            alphas, pv = [], None
            for hh in range(2):
                sl = slice(hh * HEAD_SLOT, (hh + 1) * HEAD_SLOT)
                s = _dot_tb(k_ref[:, sl], q_ref[:, sl])
                if diag:
                    krow = lax.broadcasted_iota(jnp.int32, (T, T), 0) // CHUNK
                    qcol = lax.broadcasted_iota(jnp.int32, (T, T), 1) // CHUNK
                    s = jnp.where(krow <= qcol, s, NEG_BIG)
                m_prev = m_sc[hh]
                m_new = jnp.maximum(m_prev, jnp.max(s, axis=0, keepdims=True))
                p = jnp.exp(s - m_new[0:1, :])
                alpha = jnp.exp(m_prev - m_new)
                l_sc[hh] = alpha * l_sc[hh] + jnp.sum(p, axis=0, keepdims=True)
                m_sc[hh] = m_new
                alphas.append(alpha[0:1, :])
                t_pv = _dot_ta(vh[hh], p.astype(BF16))
                pv = t_pv if pv is None else pv + t_pv
            acc_sc[...] = acc_sc[...] * jnp.where(top, alphas[0], alphas[1]) + pv

        _on_block_kind(i, j, step)

        @pl.when(j == i)
        def _():
            l0, l1 = l_sc[0][0:1, :], l_sc[1][0:1, :]
            o_ref[...] = jnp.transpose(acc_sc[...] * jnp.where(top, 1.0 / l0, 1.0 / l1))
            lse_ref[...] = jnp.transpose(jnp.where(top, m_sc[0][0:1, :] + jnp.log(l0), m_sc[1][0:1, :] + jnp.log(l1)))

    grid_spec = pltpu.PrefetchScalarGridSpec(
        num_scalar_prefetch=2, grid=(NP, int(ii.shape[0])),
        in_specs=[pl.BlockSpec((T, 2 * HEAD_SLOT), lambda p, t, ii, jj: (ii[t], p)),
                  pl.BlockSpec((T, 2 * HEAD_SLOT), lambda p, t, ii, jj: (jj[t], p)),
                  pl.BlockSpec((T, LANES), lambda p, t, ii, jj: (jj[t], v_c0 + p))],
        out_specs=[pl.BlockSpec((T, LANES), lambda p, t, ii, jj: (ii[t], p)),
                   pl.BlockSpec((None, T, LANES), lambda p, t, ii, jj: (p, ii[t], 0))],
        scratch_shapes=[pltpu.VMEM((2, SUBLANES, T), F32), pltpu.VMEM((2, SUBLANES, T), F32), pltpu.VMEM((LANES, T), F32)])
    return pl.pallas_call(
        body, name=name, grid_spec=grid_spec,
        out_shape=[jax.ShapeDtypeStruct((S, MLA_HEADS * V_HEAD), F32), jax.ShapeDtypeStruct((NP, S, LANES), F32)],
        compiler_params=_cp(("parallel", "arbitrary")),
    )(ii, jj, q, k, kvraw)


def _attn_bwd_common(q_ref, k_ref, v, do, o, lse, hh, diag, T, lo):
    sel = lo if hh == 0 else jnp.logical_not(lo)
    s = _attn_scores(q_ref, k_ref, hh, diag, T)
    p = jnp.exp(s - lse[:, hh * V_HEAD:hh * V_HEAD + 1])
    do_h = jnp.where(sel, do, 0.0)
    dsum = jnp.sum(do_h * o, axis=1, keepdims=True)
    do_hb = do_h.astype(BF16)
    dp = _dot_tb(do_hb, v)
    return p, p * (dp - dsum), do_hb


def _attn_bwd(q, k, kvraw, do, o, lse, *, T, name):
    S = q.shape[0]
    NP = MLA_HEADS // 2
    nb = S // T
    ii, jj = _attn_pairs(nb, kv_outer=True)
    n_steps = int(ii.shape[0])
    v_c0 = MLA_HEADS * HEAD_SLOT // LANES

    def body(ii_ref, jj_ref, q_ref, k_ref, v_ref, do_ref, o_ref, lse_ref, dq_hbm, dk_ref, dv_ref,
             dq_sc, dkt_sc, dvt_sc, sem):
        pair = pl.program_id(0)
        t = pl.program_id(1)
        i, j = ii_ref[t], jj_ref[t]

        @pl.when(t == 0)
        def _():
            dq_sc[...] = jnp.zeros_like(dq_sc)

        @pl.when(i == j)
        def _():
            dkt_sc[...] = jnp.zeros_like(dkt_sc)
            dvt_sc[...] = jnp.zeros_like(dvt_sc)

        lo = lax.broadcasted_iota(jnp.int32, (T, LANES), 1) < V_HEAD
        rows = pl.ds(pl.multiple_of(i * T, T), T)

        def step(diag):
            v = v_ref[...].astype(BF16)
            do, o_v, lse_v = do_ref[...], o_ref[...], lse_ref[...]
            for hh in range(2):
                sl = slice(hh * HEAD_SLOT, (hh + 1) * HEAD_SLOT)
                p, ds, do_hb = _attn_bwd_common(q_ref, k_ref, v, do, o_v, lse_v, hh, diag, T, lo)
                dsb = ds.astype(BF16)
                do_t = jnp.transpose(do_hb.astype(F32)).astype(BF16)
                q_t = jnp.transpose(q_ref[:, sl].astype(F32)).astype(BF16)
                dvt_sc[...] += _dot(do_t, p.astype(BF16))
                dkt_sc[hh] += _dot(q_t, dsb)
                dq_sc[rows, sl] += _dot(dsb, k_ref[:, sl])

        _on_block_kind(i, j, step)

        @pl.when(i == nb - 1)
        def _():
            for hh in range(2):
                dk_ref[:, hh * HEAD_SLOT:(hh + 1) * HEAD_SLOT] = jnp.transpose(dkt_sc[hh])
            dv_ref[...] = jnp.transpose(dvt_sc[...])

        @pl.when(t == n_steps - 1)
        def _():
            cp = pltpu.make_async_copy(dq_sc, dq_hbm.at[pair], sem)
            cp.start()
            cp.wait()

    qi = lambda p, t, ii, jj: (ii[t], p)
    kj = lambda p, t, ii, jj: (jj[t], p)
    grid_spec = pltpu.PrefetchScalarGridSpec(
        num_scalar_prefetch=2, grid=(NP, n_steps),
        in_specs=[pl.BlockSpec((T, 2 * HEAD_SLOT), qi), pl.BlockSpec((T, 2 * HEAD_SLOT), kj),
                  pl.BlockSpec((T, LANES), lambda p, t, ii, jj: (jj[t], v_c0 + p)),
                  pl.BlockSpec((T, LANES), qi), pl.BlockSpec((T, LANES), qi),
                  pl.BlockSpec((None, T, LANES), lambda p, t, ii, jj: (p, ii[t], 0))],
        out_specs=[pl.BlockSpec(memory_space=pl.ANY), pl.BlockSpec((T, 2 * HEAD_SLOT), kj),
                   pl.BlockSpec((T, LANES), kj)],
        scratch_shapes=[pltpu.VMEM((S, 2 * HEAD_SLOT), F32), pltpu.VMEM((2, HEAD_SLOT, T), F32),
                        pltpu.VMEM((LANES, T), F32), pltpu.SemaphoreType.DMA])
    return pl.pallas_call(
        body, name=name, grid_spec=grid_spec,
        out_shape=[jax.ShapeDtypeStruct((NP, S, 2 * HEAD_SLOT), F32),
                   jax.ShapeDtypeStruct((S, MLA_HEADS * HEAD_SLOT), F32),
                   jax.ShapeDtypeStruct((S, MLA_HEADS * V_HEAD), F32)],
        compiler_params=_cp(("arbitrary", "arbitrary")),
    )(ii, jj, q, k, kvraw, do, o, lse)


def _mla_permute_weights(w_in, w_uq, w_ukv):
    D = w_in.shape[0]
    H = MLA_HEADS
    qk = QK_NOPE + QK_ROPE
    lat = Q_LORA + KV_LORA
    kpe = jnp.zeros((D, HEAD_SLOT), w_in.dtype).at[:, QK_NOPE:qk].set(w_in[:, lat:])
    w_in_p = jnp.concatenate([w_in[:, :lat], kpe], axis=1)
    w_uq_p = jnp.pad(w_uq.reshape(Q_LORA, H, qk), ((0, 0), (0, 0), (0, HEAD_SLOT - qk))).reshape(Q_LORA, H * HEAD_SLOT)
    kv = w_ukv.reshape(KV_LORA, H, QK_NOPE + V_HEAD)
    wk = jnp.pad(kv[:, :, :QK_NOPE], ((0, 0), (0, 0), (0, HEAD_SLOT - QK_NOPE))).reshape(KV_LORA, H * HEAD_SLOT)
    wv = kv[:, :, QK_NOPE:].reshape(KV_LORA, H * V_HEAD)
    return w_in_p, w_uq_p, jnp.concatenate([wk, wv], axis=1)


def _mla_unpermute_grads(g_in_p, g_uq_p, g_ukv_p):
    H = MLA_HEADS
    qk = QK_NOPE + QK_ROPE
    lat = Q_LORA + KV_LORA
    g_in = jnp.concatenate([g_in_p[:, :lat], g_in_p[:, lat + QK_NOPE:lat + qk]], axis=1)
    g_uq = g_uq_p.reshape(Q_LORA, H, HEAD_SLOT)[:, :, :qk].reshape(Q_LORA, H * qk)
    gk = g_ukv_p[:, :H * HEAD_SLOT].reshape(KV_LORA, H, HEAD_SLOT)[:, :, :QK_NOPE]
    gv = g_ukv_p[:, H * HEAD_SLOT:].reshape(KV_LORA, H, V_HEAD)
    g_ukv = jnp.concatenate([gk, gv], axis=2).reshape(KV_LORA, H * (QK_NOPE + V_HEAD))
    return g_in, g_uq, g_ukv


def _mla_mixer_fwd(x, pos, p, tag):
    S, D = x.shape
    ts = _rt(S, 512)
    T = _rt(S, ATTN_BLOCK)
    lat = Q_LORA + KV_LORA
    tabs = _mla_tables(pos)
    proj = _mm(x, p["w_in_p"], out_dtype=F32, tm=ts, tn=p["w_in_p"].shape[1], name=tag + "_proj")
    qn = _rms_fwd(proj, p["q_norm"], c0=0, ts=ts, name=tag + "_qn")
    kvn = _rms_fwd(proj, p["kv_norm"], c0=Q_LORA, ts=ts, name=tag + "_kvn")
    qraw = _mm(qn, p["w_uq_p"], out_dtype=F32, tm=ts, tn=1024, name=tag + "_uq")
    kvraw = _mm(kvn, p["w_ukv_p"], out_dtype=F32, tm=ts, tn=1024, name=tag + "_ukv")
    q, k = _mla_prep_fwd(qraw, kvraw, proj, tabs, kpe_c0=lat, ts=ts, name=tag + "_prep")
    o, lse = _attn_fwd(q, k, kvraw, T=T, name=tag + "_attn")
    mix = _mm(o, p["w_out"], out_dtype=F32, tm=ts, tn=D, name=tag + "_out")
    return mix, (proj, qn, kvn, kvraw, q, k, o, lse, tabs)


def _mla_mixer_bwd(dmix, x, p, saved, tag, gbuf, j):
    proj, qn, kvn, kvraw, q, k, o, lse, tabs = saved
    S, D = x.shape
    ts = _rt(S, 512)
    T = _rt(S, ATTN_BLOCK)
    lat = Q_LORA + KV_LORA
    g = {}
    do = _mm_tb([(dmix, 0)], p["w_out"], out_dtype=F32, tm=ts, tk=p["w_out"].shape[0], name=tag + "_do")
    g["w_out"] = _mm_ta(o, dmix, tk=p["w_out"].shape[0], tn=D, tm=ts, name=tag + "_dwout")
    dq, dk, dv = _attn_bwd(q, k, kvraw, do, o, lse, T=T, name=tag + "_attn_bwd")
    dqraw, dkvraw, dkpe = _mla_prep_bwd(dq, dk, dv, tabs, ts=_rt(S, 256), name=tag + "_prepb")
    dqn = _mm_tb([(dqraw, 0)], p["w_uq_p"], out_dtype=F32, tm=ts, tk=Q_LORA, name=tag + "_dqn")
    g_uq_p = _mm_ta(qn, dqraw, tk=Q_LORA, tn=1024, tm=ts, name=tag + "_dwuq")
    dkvn = _mm_tb([(dkvraw, 0)], p["w_ukv_p"], out_dtype=F32, tm=ts, tk=KV_LORA, name=tag + "_dkvn")
    g_ukv_p = _mm_ta(kvn, dkvraw, tk=KV_LORA, tn=1024, tm=ts, name=tag + "_dwukv")
    dcq, dqg8 = _rms_bwd(dqn, proj, p["q_norm"], c0=0, ts=ts, name=tag + "_qnb")
    dckv, dkvg8 = _rms_bwd(dkvn, proj, p["kv_norm"], c0=Q_LORA, ts=ts, name=tag + "_kvnb")
    g["q_norm"] = dqg8.sum(axis=0)
    g["kv_norm"] = dkvg8.sum(axis=0)
    dx = _mm_tb([(dcq, 0), (dckv, Q_LORA), (dkpe, lat)], p["w_in_p"], out_dtype=F32, tm=ts, tk=D, name=tag + "_dx")
    g_in_p = jnp.concatenate(
        [_mm_ta(x, dcq, tk=D, tn=Q_LORA, tm=ts, name=tag + "_dwin_q"),
         _mm_ta(x, dckv, tk=D, tn=KV_LORA, tm=ts, name=tag + "_dwin_kv"),
         _mm_ta(x, dkpe, tk=D, tn=HEAD_SLOT, tm=ts, name=tag + "_dwin_pe")], axis=1)
    g["w_in"], g["w_uq"], g["w_ukv"] = _mla_unpermute_grads(g_in_p, g_uq_p, g_ukv_p)
    return dx, g


RET_QK = 256
RET_V = 512


def _ret_tables(pos, T):
    half = RET_QK // 2
    inv_freq = ROPE_BASE ** (-jnp.arange(0, RET_QK, 2, dtype=F32) / RET_QK)
    ang = pos.astype(F32)[:, None] * inv_freq
    lg = jnp.log1p(-jnp.exp2(-5.0 - jnp.arange(RET_HEADS, dtype=F32)))
    idx = jnp.arange(T, dtype=F32)
    ch = jnp.arange(T) // CHUNK
    dm = jnp.where(ch[None, :] <= ch[:, None], jnp.exp(lg[:, None, None] * jnp.abs(idx[:, None] - idx[None, :])), 0.0)
    xi = jnp.broadcast_to(jnp.exp(lg[:, None] * (idx + 1.0))[:, :, None], (RET_HEADS, T, RET_QK))
    zeta = jnp.broadcast_to(jnp.exp(lg[:, None] * (T - 1.0 - idx))[:, :, None], (RET_HEADS, T, RET_QK))
    g_t = jnp.broadcast_to(jnp.exp(lg * T)[:, None, None], (RET_HEADS, 1, RET_V))
    assert half == LANES
    return jnp.cos(ang), jnp.sin(ang), dm.astype(F32), xi.astype(F32), zeta.astype(F32), g_t.astype(F32)


def _rope_half(x, c, s):
    x1, x2 = x[:, :LANES], x[:, LANES:]
    return jnp.concatenate([x1 * c - x2 * s, x1 * s + x2 * c], axis=1)


def _rope_half_t(g, c, s):
    g1, g2 = g[:, :LANES], g[:, LANES:]
    return jnp.concatenate([g1 * c + g2 * s, g2 * c - g1 * s], axis=1)


def _ret_qkv(q_ref, k_ref, v_ref, c_ref, s_ref):
    c, s = c_ref[...], s_ref[...]
    q = _rope_half(q_ref[...], c, s)
    k = _rope_half(k_ref[...], c, s) * (RET_QK ** -0.5)
    return q, k, v_ref[...].astype(BF16)


def _ret_in_specs(T, H, rev_nb=None):
    rb = (lambda n: n) if rev_nb is None else (lambda n: rev_nb - 1 - n)
    nq = H * RET_QK // RET_QK
    nv = 2 * H * RET_QK // RET_V
    return dict(
        q=pl.BlockSpec((T, RET_QK), lambda h, n: (rb(n), h)),
        k=pl.BlockSpec((T, RET_QK), lambda h, n: (rb(n), nq + h)),
        v=pl.BlockSpec((T, RET_V), lambda h, n: (rb(n), nv + h)),
        g=pl.BlockSpec((T, RET_V), lambda h, n: (rb(n), nv + H + h)),
        yv=pl.BlockSpec((T, RET_V), lambda h, n: (rb(n), h)),
        cs=pl.BlockSpec((T, LANES), lambda h, n: (rb(n), 0)),
        dm=pl.BlockSpec((None, T, T), lambda h, n: (h, 0, 0)),
        xz=pl.BlockSpec((None, T, RET_QK), lambda h, n: (h, 0, 0)),
        gt=pl.BlockSpec((None, 1, RET_V), lambda h, n: (h, 0, 0)),
        gn=pl.BlockSpec((1, RET_V), lambda h, n: (0, h)),
        st=pl.BlockSpec((None, None, RET_QK, RET_V), lambda h, n: (h, rb(n), 0, 0)),
    )


def _ret_fwd(proj, gn_g, tabs, *, T, name):
    S = proj.shape[0]
    H = RET_HEADS
    nb = S // T
    cos, sin, dm, xi, zeta, g_t = tabs
    sp = _ret_in_specs(T, H)

    def body(q_ref, k_ref, v_ref, g_ref, gn_ref, c_ref, s_ref, dm_ref, xi_ref, zeta_ref, gt_ref,
             o_ref, y_ref, st_ref, st):
        @pl.when(pl.program_id(1) == 0)
        def _():
            st[...] = jnp.zeros_like(st)

        q, k, vb = _ret_qkv(q_ref, k_ref, v_ref, c_ref, s_ref)
        qb, kb = q.astype(BF16), k.astype(BF16)
        s0 = st[...]
        s0b = s0.astype(BF16)
        st_ref[...] = s0b
        a = _dot_tb(qb, kb) * dm_ref[...]
        y = _dot(a.astype(BF16), vb) + _dot((q * xi_ref[...]).astype(BF16), s0b)
        st[...] = s0 * gt_ref[...] + _dot_ta((k * zeta_ref[...]).astype(BF16), vb)
        y_ref[...] = y
        mu = jnp.mean(y, axis=-1, keepdims=True)
        yc = y - mu
        var = jnp.mean(yc * yc, axis=-1, keepdims=True)
        gv = g_ref[...]
        o_ref[...] = (gv * _sigmoid(gv) * (yc * lax.rsqrt(var + LN_EPS) * gn_ref[...])).astype(BF16)

    return pl.pallas_call(
        body, name=name, grid=(H, nb),
        in_specs=[sp["q"], sp["k"], sp["v"], sp["g"], sp["gn"], sp["cs"], sp["cs"], sp["dm"], sp["xz"], sp["xz"], sp["gt"]],
        out_specs=[sp["yv"], sp["yv"], sp["st"]],
        out_shape=[jax.ShapeDtypeStruct((S, H * RET_V), BF16), jax.ShapeDtypeStruct((S, H * RET_V), F32),
                   jax.ShapeDtypeStruct((H, nb, RET_QK, RET_V), BF16)],
        scratch_shapes=[pltpu.VMEM((RET_QK, RET_V), F32)],
        compiler_params=_cp(("parallel", "arbitrary")),
    )(proj, proj, proj, proj, gn_g.reshape(1, H * RET_V), cos, sin, dm, xi, zeta, g_t)


def _ret_gn_bwd(dout, proj, y, gn_g, *, ts, name):
    S = proj.shape[0]
    H = RET_HEADS
    goff = 2 * H * RET_QK // RET_V + H

    def body(do_ref, g_ref, y_ref, gn_ref, dy_ref, dg_ref, dgn_ref):
        @pl.when(pl.program_id(1) == 0)
        def _():
            dgn_ref[...] = jnp.zeros_like(dgn_ref)
        y_v = y_ref[...]
        mu = jnp.mean(y_v, axis=-1, keepdims=True)
        yc = y_v - mu
        var = jnp.mean(yc * yc, axis=-1, keepdims=True)
        rstd = lax.rsqrt(var + LN_EPS)
        yhat = yc * rstd
        gv = g_ref[...]
        sg = _sigmoid(gv)
        dout = do_ref[...]
        gn = gn_ref[...]
        dg_ref[...] = (dout * (yhat * gn) * (sg * (1.0 + gv * (1.0 - sg)))).astype(BF16)
        dyn = dout * (gv * sg)
        dgn_ref[...] += _fold8(dyn * yhat)
        dyh = dyn * gn
        m1 = jnp.mean(dyh, axis=-1, keepdims=True)
        m2 = jnp.mean(dyh * yhat, axis=-1, keepdims=True)
        dy_ref[...] = (rstd * (dyh - m1 - yhat * m2)).astype(BF16)

    blk = pl.BlockSpec((ts, RET_V), lambda h, i: (i, h))
    return pl.pallas_call(
        body, name=name, grid=(H, S // ts),
        in_specs=[blk, pl.BlockSpec((ts, RET_V), lambda h, i: (i, goff + h)), blk,
                  pl.BlockSpec((1, RET_V), lambda h, i: (0, h))],
        out_specs=[blk, blk, pl.BlockSpec((SUBLANES, RET_V), lambda h, i: (0, h))],
        out_shape=[jax.ShapeDtypeStruct((S, H * RET_V), BF16), jax.ShapeDtypeStruct((S, H * RET_V), BF16),
                   jax.ShapeDtypeStruct((SUBLANES, H * RET_V), F32)],
        compiler_params=_cp(("parallel", "arbitrary")),
    )(dout, proj, y, gn_g.reshape(1, H * RET_V))


def _ret_bwd(proj, dy, states, tabs, *, T, name):
    S = proj.shape[0]
    H = RET_HEADS
    nb = S // T
    cos, sin, dm, xi, zeta, g_t = tabs
    sp = _ret_in_specs(T, H, rev_nb=nb)

    def body(q_ref, k_ref, v_ref, dy_ref, st_ref, c_ref, s_ref, dm_ref, xi_ref, zeta_ref, gt_ref,
             dq_ref, dk_ref, dv_ref, ds):
        @pl.when(pl.program_id(1) == 0)
        def _():
            ds[...] = jnp.zeros_like(ds)

        q, k, vb = _ret_qkv(q_ref, k_ref, v_ref, c_ref, s_ref)
        qb, kb = q.astype(BF16), k.astype(BF16)
        dyb = dy_ref[...]
        s0b = st_ref[...]
        dmv, xiv, zv = dm_ref[...], xi_ref[...], zeta_ref[...]
        ds_v = ds[...]
        dsb = ds_v.astype(BF16)
        gm = (_dot_tb(dyb, vb) * dmv).astype(BF16)
        ab = (_dot_tb(qb, kb) * dmv).astype(BF16)
        kz = (k * zv).astype(BF16)
        qx = (q * xiv).astype(BF16)
        dq = _dot(gm, kb) + xiv * _dot_tb(dyb, s0b)
        dk = _dot_ta(gm, qb) + zv * _dot_tb(vb, dsb)
        dv_ref[...] = (_dot_ta(ab, dyb) + _dot(kz, dsb)).astype(BF16)
        ds[...] = ds_v * gt_ref[...] + _dot_ta(qx, dyb)
        c, s = c_ref[...], s_ref[...]
        dq_ref[...] = _rope_half_t(dq, c, s).astype(BF16)
        dk_ref[...] = _rope_half_t(dk * (RET_QK ** -0.5), c, s).astype(BF16)

    qblk = pl.BlockSpec((T, RET_QK), lambda h, n: (nb - 1 - n, h))
    return pl.pallas_call(
        body, name=name, grid=(H, nb),
        in_specs=[sp["q"], sp["k"], sp["v"], sp["yv"], sp["st"], sp["cs"], sp["cs"], sp["dm"], sp["xz"], sp["xz"], sp["gt"]],
        out_specs=[qblk, qblk, sp["yv"]],
        out_shape=[jax.ShapeDtypeStruct((S, H * RET_QK), BF16), jax.ShapeDtypeStruct((S, H * RET_QK), BF16),
                   jax.ShapeDtypeStruct((S, H * RET_V), BF16)],
        scratch_shapes=[pltpu.VMEM((RET_QK, RET_V), F32)],
        compiler_params=_cp(("parallel", "arbitrary")),
    )(proj, proj, proj, dy, states, cos, sin, dm, xi, zeta, g_t)


def _ret_mixer_fwd(x, pos, p, tag):
    S, D = x.shape
    ts = _rt(S, 512)
    T = _rt(S, 256)
    tabs = _ret_tables(pos, T)
    proj = _mm(x, p["w_in"], out_dtype=F32, tm=_rt(S, 256), tn=p["w_in"].shape[1], name=tag + "_proj")
    gated, y, states = _ret_fwd(proj, p["gn_g"], tabs, T=T, name=tag + "_ret")
    mix = _mm(gated, p["w_out"], out_dtype=F32, tm=ts, tn=D, name=tag + "_out")
    return mix, (proj, gated, y, states, tabs)


def _ret_mixer_bwd(dmix, x, p, saved, tag, gbuf, j):
    proj, gated, y, states, tabs = saved
    S, D = x.shape
    ts = _rt(S, 512)
    T = _rt(S, 256)
    H = RET_HEADS
    hq, hv = H * RET_QK, H * RET_V
    g = {}
    dout = _mm_tb([(dmix, 0)], p["w_out"], out_dtype=F32, tm=ts, tk=1024, name=tag + "_dgated")
    n_ret = DEPTH // 3
    _grad_into(gbuf, "ret_w_out", (n_ret, hv, D), j, 0, 0, gated, dmix, tk=1024, tn=D, tm=ts, name=tag + "_dwout")
    dy, dgate, dgn8 = _ret_gn_bwd(dout, proj, y, p["gn_g"], ts=_rt(S, 256), name=tag + "_gnb")
    g["gn_g"] = dgn8.sum(axis=0)
    dq, dk, dv = _ret_bwd(proj, dy, states, tabs, T=T, name=tag + "_retb")
    dx = _mm_tb([(dq, 0), (dk, hq), (dv, 2 * hq), (dgate, 2 * hq + hv)], p["w_in"], out_dtype=F32,
                tm=ts, tk=512, name=tag + "_dx")
    w_in_shape = (n_ret, D, 2 * hq + 2 * hv)
    for part, c0, nm in ((dq, 0, "q"), (dk, hq, "k"), (dv, 2 * hq, "v"), (dgate, 2 * hq + hv, "g")):
        _grad_into(gbuf, "ret_w_in", w_in_shape, j, 0, c0, x, part, tk=D, tn=1024, tm=ts, name=tag + "_dwin_" + nm)
    return dx, g


PACK_W = 1024
ANY = pl.BlockSpec(memory_space=pl.ANY)
MESH = pl.DeviceIdType.MESH


def _coords():
    return lax.axis_index("x"), lax.axis_index("y"), lax.axis_index("c")


def _chip_peers(x, y):
    return [(1 - x, y), (x, 1 - y), (1 - x, 1 - y)]


def _slot(ref, axis, s, n):
    if axis is None:
        return ref.at[s]
    size = n // N_CHIPS
    sl = pl.ds(pl.multiple_of(s * size, LANES if axis == 2 else 2 * SUBLANES), size)
    return ref.at[:, sl, :] if axis == 1 else ref.at[:, :, sl]


def _row_half(ref, h):
    if len(ref.shape) == 2:
        n = ref.shape[0] // 2
        return ref.at[pl.ds(pl.multiple_of(h * n, 2 * SUBLANES), n), :]
    n = ref.shape[1] // 2
    return ref.at[:, pl.ds(pl.multiple_of(h * n, 2 * SUBLANES), n), :]


def _gather_chips(items, name):
    n = len(items)
    axes = [ax for _, ax in items]
    out_shapes = []
    for arr, ax in items:
        shp = (N_CHIPS,) + arr.shape if ax is None else tuple(d * (N_CHIPS if i == ax else 1) for i, d in enumerate(arr.shape))
        out_shapes.append(jax.ShapeDtypeStruct(shp, arr.dtype))

    def body(*refs):
        srcs, outs = refs[:n], refs[n:2 * n]
        send_sems, recv_sems, local_sems = refs[2 * n:]
        x, y, c = _coords()
        me = 2 * x + y
        sibling = (x, y, 1 - c)
        dst = lambda t, s: _slot(outs[t], axes[t], s, out_shapes[t].shape[axes[t]] if axes[t] is not None else 0)

        def copy(sem, src, dst_ref, to):
            return pltpu.make_async_remote_copy(src_ref=src, dst_ref=dst_ref, send_sem=send_sems.at[sem],
                                                recv_sem=recv_sems.at[sem], device_id=to, device_id_type=MESH)

        local = [pltpu.make_async_copy(srcs[t], dst(t, me), local_sems.at[t]) for t in range(n)]
        for cp in local:
            cp.start()
        peers = _chip_peers(x, y)
        first, arrive, passed, from_sib = [], [], [], []
        for k, (px, py) in enumerate(peers):
            for t in range(n):
                land = _row_half(dst(t, 2 * px + py), c)
                first.append(copy(k * n + t, _row_half(srcs[t], c), _row_half(dst(t, me), c), (px, py, c)))
                arrive.append(copy(k * n + t, _row_half(srcs[t], c), land, (px, py, c)))
                passed.append(copy((3 + k) * n + t, land, land, sibling))
                from_sib.append(copy((3 + k) * n + t, land, _row_half(dst(t, 2 * px + py), 1 - c), sibling))
        for cp in first:
            cp.start()
        for cp_in, cp_on in zip(arrive, passed):
            cp_in.wait_recv()
            cp_on.start()
        for cp in from_sib:
            cp.wait_recv()
        for cp in first + passed:
            cp.wait_send()
        for cp in local:
            cp.wait()

    return pl.pallas_call(
        body, name=name, in_specs=[ANY] * n, out_specs=[ANY] * n, out_shape=out_shapes,
        scratch_shapes=[pltpu.SemaphoreType.DMA((6 * n,)), pltpu.SemaphoreType.DMA((6 * n,)),
                        pltpu.SemaphoreType.DMA((n,))],
    )(*[arr for arr, _ in items])


def _scatter_chips(items, name):
    n = len(items)
    axes = [ax for _, ax in items]
    out_shapes = []
    for arr, ax in items:
        part = arr.shape[1:] if ax is None else tuple(d // (N_CHIPS if i == ax else 1) for i, d in enumerate(arr.shape))
        out_shapes.append(jax.ShapeDtypeStruct((3,) + part, arr.dtype))

    def body(*refs):
        srcs, outs = refs[:n], refs[n:2 * n]
        send_sems, recv_sems = refs[2 * n:]
        x, y, c = _coords()
        copies = []
        for k, (px, py) in enumerate(_chip_peers(x, y)):
            for t in range(n):
                src = _slot(srcs[t], axes[t], 2 * px + py, srcs[t].shape[axes[t]] if axes[t] is not None else 0)
                copies.append(pltpu.make_async_remote_copy(
                    src_ref=src, dst_ref=outs[t].at[k], send_sem=send_sems.at[k * n + t],
                    recv_sem=recv_sems.at[k * n + t], device_id=(px, py, c), device_id_type=MESH))
        for cp in copies:
            cp.start()
        for cp in copies:
            cp.wait_recv()
        for cp in copies:
            cp.wait_send()

    return pl.pallas_call(
        body, name=name, in_specs=[ANY] * n, out_specs=[ANY] * n, out_shape=out_shapes,
        scratch_shapes=[pltpu.SemaphoreType.DMA((3 * n,)), pltpu.SemaphoreType.DMA((3 * n,))],
    )(*[arr for arr, _ in items])


def _swap_sibling(arrs, name):
    n = len(arrs)

    def body(*refs):
        srcs, outs = refs[:n], refs[n:2 * n]
        send_sems, recv_sems = refs[2 * n:]
        x, y, c = _coords()
        copies = [pltpu.make_async_remote_copy(src_ref=srcs[t], dst_ref=outs[t], send_sem=send_sems.at[t],
                                               recv_sem=recv_sems.at[t], device_id=(x, y, 1 - c), device_id_type=MESH)
                  for t in range(n)]
        for cp in copies:
            cp.start()
        for cp in copies:
            cp.wait_recv()
        for cp in copies:
            cp.wait_send()

    return pl.pallas_call(
        body, name=name, in_specs=[ANY] * n, out_specs=[ANY] * n,
        out_shape=[jax.ShapeDtypeStruct(a_.shape, a_.dtype) for a_ in arrs],
        scratch_shapes=[pltpu.SemaphoreType.DMA((n,)), pltpu.SemaphoreType.DMA((n,))],
    )(*arrs)


def _allreduce_small(v, name):
    R, Wd = v.shape

    def body(v_ref, o_ref, buf, send_sems, recv_sems):
        x, y, c = _coords()
        o_ref[...] = v_ref[...]
        for st, peer in enumerate([(x, y, 1 - c), (x, 1 - y, c), (1 - x, y, c)]):
            cp = pltpu.make_async_remote_copy(src_ref=o_ref, dst_ref=buf.at[st], send_sem=send_sems.at[st],
                                              recv_sem=recv_sems.at[st], device_id=peer, device_id_type=MESH)
            cp.start()
            cp.wait_recv()
            cp.wait_send()
            o_ref[...] = o_ref[...] + buf[st]

    vm = pl.BlockSpec(memory_space=pltpu.VMEM)
    return pl.pallas_call(
        body, name=name, in_specs=[vm], out_specs=vm,
        out_shape=jax.ShapeDtypeStruct((R, Wd), F32),
        scratch_shapes=[pltpu.VMEM((3, R, Wd), F32), pltpu.SemaphoreType.DMA((3,)), pltpu.SemaphoreType.DMA((3,))],
    )(v)


def _row_tile(rows):
    t = rows
    while t > 256:
        assert t % 2 == 0
        t //= 2
    assert t % SUBLANES == 0
    return t


def _sum_partials(g, recv, axis, *, name):
    _, L, R, C = recv.shape
    tr = _row_tile(R)
    me = (2 * lax.axis_index("x") + lax.axis_index("y")).astype(jnp.int32).reshape(1)

    def body(me_ref, g_ref, r_ref, o_ref):
        o_ref[...] = ((g_ref[...] + r_ref[0].astype(F32)) + r_ref[1].astype(F32)) + r_ref[2].astype(F32)

    if axis is None:
        g_spec = pl.BlockSpec((None, None, tr, C), lambda l, i, me: (me[0], l, i, 0))
    elif axis == 1:
        g_spec = pl.BlockSpec((None, tr, C), lambda l, i, me: (l, me[0] * (R // tr) + i, 0))
    else:
        g_spec = pl.BlockSpec((None, tr, C), lambda l, i, me: (l, i, me[0]))
    grid_spec = pltpu.PrefetchScalarGridSpec(
        num_scalar_prefetch=1, grid=(L, R // tr),
        in_specs=[g_spec, pl.BlockSpec((3, None, tr, C), lambda l, i, me: (0, l, i, 0))],
        out_specs=pl.BlockSpec((None, tr, C), lambda l, i, me: (l, i, 0)))
    return pl.pallas_call(
        body, name=name, grid_spec=grid_spec, out_shape=jax.ShapeDtypeStruct((L, R, C), F32),
        compiler_params=_cp(("parallel", "parallel")),
    )(me, g, recv)


def _adamw(w, m, v, ga, gb, *, name):
    L, R, C = w.shape
    tr = _row_tile(R)
    two = gb is not None
    c1 = 1.0 / (1.0 - ADAM_B1 ** ADAM_STEP)
    c2 = 1.0 / (1.0 - ADAM_B2 ** ADAM_STEP)

    def body(*refs):
        if two:
            w_ref, m_ref, v_ref, ga_ref, gb_ref, g_ref, d_ref, mo_ref, vo_ref = refs
            g = ga_ref[...] + gb_ref[...]
        else:
            w_ref, m_ref, v_ref, ga_ref, g_ref, d_ref, mo_ref, vo_ref = refs
            g = ga_ref[...]
        m2 = ADAM_B1 * m_ref[...] + (1.0 - ADAM_B1) * g
        v2 = ADAM_B2 * v_ref[...] + (1.0 - ADAM_B2) * (g * g)
        g_ref[...] = g
        mo_ref[...] = m2
        vo_ref[...] = v2
        d_ref[...] = -ADAM_LR * ((m2 * c1) / (jnp.sqrt(v2 * c2) + ADAM_EPS) + ADAM_WD * w_ref[...])

    blk = pl.BlockSpec((None, tr, C), lambda l, i: (l, i, 0))
    args = [w, m, v, ga] + ([gb] if two else [])
    return pl.pallas_call(
        body, name=name, grid=(L, R // tr), in_specs=[blk] * len(args), out_specs=[blk] * 4,
        out_shape=[jax.ShapeDtypeStruct((L, R, C), F32)] * 4, compiler_params=_cp(("parallel", "parallel")),
    )(*args)


SHARDED = [
    ("ffn_w_up", 2, True), ("ffn_conv_w", 2, False), ("ffn_w_down", 1, True),
    ("lru_w_in", 2, True), ("lru_conv_w", 2, False), ("lru_conv_b", 1, False),
    ("lru_w_a", 2, True), ("lru_b_a", 2, False), ("lru_w_x", 2, True), ("lru_b_x", 2, False),
    ("lru_lambda", 1, False), ("lru_w_out", 1, True),
    ("mla_w_in", 2, True), ("mla_w_uq", 2, True), ("mla_w_ukv", 2, True), ("mla_w_out", 1, True),
    ("ret_w_in", 2, True), ("ret_gn_g", 1, False), ("ret_w_out", 1, True),
]
BIG_AXIS = {"ffn_w_up": 2, "ffn_w_down": 1, "lru_w_in": 2, "lru_w_out": 1, "ret_w_in": 2, "ret_w_out": 1}
REPLICATED = ["ln1_g", "ln1_b", "ln2_g", "ln2_b", "ffn_conv_b", "mla_q_norm", "mla_kv_norm"]
WEIGHTS = ["ln1_g", "ln1_b", "ln2_g", "ln2_b", "ffn_w_up", "ffn_conv_w", "ffn_conv_b", "ffn_w_down", "lru_w_in",
           "lru_conv_w", "lru_conv_b", "lru_w_a", "lru_b_a", "lru_w_x", "lru_b_x", "lru_lambda", "lru_w_out",
           "mla_w_in", "mla_q_norm", "mla_kv_norm", "mla_w_uq", "mla_w_ukv", "mla_w_out", "ret_w_in", "ret_gn_g",
           "ret_w_out"]
PACK_ROWS = 512


def _pack(arrs, dtype, lead=(), rows=PACK_ROWS):
    nl = len(lead)
    flat = jnp.concatenate([a.astype(dtype).reshape(lead + (-1,)) for a in arrs], axis=nl)
    n = flat.shape[nl]
    quantum = rows * PACK_W
    total = -(-n // quantum) * quantum
    flat = jnp.pad(flat, [(0, 0)] * nl + [(0, total - n)])
    return flat.reshape(lead + (total // PACK_W, PACK_W))


def _unpack(buf, shapes, lead=()):
    nl = len(lead)
    flat = buf.reshape(lead + (-1,))
    out, off = [], 0
    for shp in shapes:
        n = int(np.prod(shp))
        out.append(lax.slice_in_dim(flat, off, off + n, axis=nl).reshape(lead + tuple(shp)))
        off += n
    return out


def _layer_params(full, rep, i):
    kind, j = i % 3, i // 3
    ffn = dict(w_up=full["ffn_w_up"][i], conv_w=full["ffn_conv_w"][i], conv_b=rep["ffn_conv_b"][i],
               w_down=full["ffn_w_down"][i])
    if kind == 0:
        mix = dict(w_in=full["lru_w_in"][j], conv_w=full["lru_conv_w"][j], conv_b=full["lru_conv_b"][j],
                   w_a=full["lru_w_a"][j], b_a=full["lru_b_a"][j], w_x=full["lru_w_x"][j], b_x=full["lru_b_x"][j],
                   lam=full["lru_lambda"][j], w_out=full["lru_w_out"][j])
    elif kind == 1:
        w_in_p, w_uq_p, w_ukv_p = _mla_permute_weights(full["mla_w_in"][j], full["mla_w_uq"][j], full["mla_w_ukv"][j])
        mix = dict(w_in_p=w_in_p, w_uq_p=w_uq_p, w_ukv_p=w_ukv_p, q_norm=rep["mla_q_norm"][j],
                   kv_norm=rep["mla_kv_norm"][j], w_out=full["mla_w_out"][j])
    else:
        mix = dict(w_in=full["ret_w_in"][j], gn_g=full["ret_gn_g"][j], w_out=full["ret_w_out"][j])
    return kind, mix, ffn


_MIX_FWD = {0: lambda x, pos, p, tag: _lru_mixer_fwd(x, p, tag), 1: _mla_mixer_fwd, 2: _ret_mixer_fwd}
_MIX_BWD = {0: _lru_mixer_bwd, 1: _mla_mixer_bwd, 2: _ret_mixer_bwd}
_MIX_PREFIX = {0: "lru_", 1: "mla_", 2: "ret_"}
_MIX_KEYS = {0: {"w_in": "lru_w_in", "conv_w": "lru_conv_w", "conv_b": "lru_conv_b", "w_a": "lru_w_a", "b_a": "lru_b_a",
                 "w_x": "lru_w_x", "b_x": "lru_b_x", "lam": "lru_lambda", "w_out": "lru_w_out"},
             1: {"w_in": "mla_w_in", "q_norm": "mla_q_norm", "kv_norm": "mla_kv_norm", "w_uq": "mla_w_uq",
                 "w_ukv": "mla_w_ukv", "w_out": "mla_w_out"},
             2: {"w_in": "ret_w_in", "gn_g": "ret_gn_g", "w_out": "ret_w_out"}}
_FFN_KEYS = {"w_up": "ffn_w_up", "conv_w": "ffn_conv_w", "conv_b": "ffn_conv_b", "w_down": "ffn_w_down"}


def _local_step(x, pos, target, full, rep):
    S, D = x.shape
    ts = _rt(S, 512)
    acts = []
    h = x
    for i in range(DEPTH):
        kind, mp, fp = _layer_params(full, rep, i)
        tag = "l%d" % i
        mix, msaved = _MIX_FWD[kind](h, pos, mp, tag + "m")
        h1, z1 = _ln_fwd(h, mix, rep["ln1_g"][i], rep["ln1_b"][i], ts=ts, name=tag + "_ln1")
        f, fsaved = _ffn_fwd(h1, fp, tag + "f")
        h2, z2 = _ln_fwd(h1, f, rep["ln2_g"][i], rep["ln2_b"][i], ts=ts, name=tag + "_ln2")
        acts.append((kind, mp, fp, h, msaved, h1, z1, fsaved, z2))
        h = h2
    dy, part = _loss_head(h, target, ts=ts, name="loss_head")

    grads = {n: {} for n in WEIGHTS if n not in BIG_AXIS}
    gbuf = {}
    d_a, d_b = dy, None
    for i in reversed(range(DEPTH)):
        kind, mp, fp, h_in, msaved, h1, z1, fsaved, z2 = acts[i]
        tag = "l%d" % i
        dz2, dg8, db8 = _ln_bwd(d_a, d_b, z2, rep["ln2_g"][i], ts=ts, name=tag + "_ln2b")
        grads["ln2_g"][i], grads["ln2_b"][i] = dg8.sum(axis=0), db8.sum(axis=0)
        dx_f, gf = _ffn_bwd(dz2, h1, fp, fsaved, tag + "f", gbuf, i)
        for k, v in gf.items():
            grads[_FFN_KEYS[k]][i] = v
        dz1, dg8, db8 = _ln_bwd(dz2, dx_f, z1, rep["ln1_g"][i], ts=ts, name=tag + "_ln1b")
        grads["ln1_g"][i], grads["ln1_b"][i] = dg8.sum(axis=0), db8.sum(axis=0)
        dx_m, gm = _MIX_BWD[kind](dz1, h_in, mp, msaved, tag + "m", gbuf, i // 3)
        for k, v in gm.items():
            grads[_MIX_KEYS[kind][k]][i // 3] = v
        d_a, d_b = dz1, dx_m
    grad_x = _axpy(d_a, d_b, ts=ts, name="grad_x")
    stacked = {n: jnp.stack([grads[n][j] for j in sorted(grads[n])]) for n in grads}
    return part, grad_x, stacked, gbuf


def kernel(x, positions, ln1_g, ln1_b, ln2_g, ln2_b, ffn_w_up, ffn_conv_w, ffn_conv_b, ffn_w_down, lru_w_in, lru_conv_w, lru_conv_b, lru_w_a, lru_b_a, lru_w_x, lru_b_x, lru_lambda, lru_w_out, mla_w_in, mla_q_norm, mla_kv_norm, mla_w_uq, mla_w_ukv, mla_w_out, ret_w_in, ret_gn_g, ret_w_out, loss_target, m_ln1_g, m_ln1_b, m_ln2_g, m_ln2_b, m_ffn_w_up, m_ffn_conv_w, m_ffn_conv_b, m_ffn_w_down, m_lru_w_in, m_lru_conv_w, m_lru_conv_b, m_lru_w_a, m_lru_b_a, m_lru_w_x, m_lru_b_x, m_lru_lambda, m_lru_w_out, m_mla_w_in, m_mla_q_norm, m_mla_kv_norm, m_mla_w_uq, m_mla_w_ukv, m_mla_w_out, m_ret_w_in, m_ret_gn_g, m_ret_w_out, v_ln1_g, v_ln1_b, v_ln2_g, v_ln2_b, v_ffn_w_up, v_ffn_conv_w, v_ffn_conv_b, v_ffn_w_down, v_lru_w_in, v_lru_conv_w, v_lru_conv_b, v_lru_w_a, v_lru_b_a, v_lru_w_x, v_lru_b_x, v_lru_lambda, v_lru_w_out, v_mla_w_in, v_mla_q_norm, v_mla_kv_norm, v_mla_w_uq, v_mla_w_ukv, v_mla_w_out, v_ret_w_in, v_ret_gn_g, v_ret_w_out):
    w = dict(ln1_g=ln1_g, ln1_b=ln1_b, ln2_g=ln2_g, ln2_b=ln2_b, ffn_w_up=ffn_w_up, ffn_conv_w=ffn_conv_w, ffn_conv_b=ffn_conv_b, ffn_w_down=ffn_w_down, lru_w_in=lru_w_in, lru_conv_w=lru_conv_w, lru_conv_b=lru_conv_b, lru_w_a=lru_w_a, lru_b_a=lru_b_a, lru_w_x=lru_w_x, lru_b_x=lru_b_x, lru_lambda=lru_lambda, lru_w_out=lru_w_out, mla_w_in=mla_w_in, mla_q_norm=mla_q_norm, mla_kv_norm=mla_kv_norm, mla_w_uq=mla_w_uq, mla_w_ukv=mla_w_ukv, mla_w_out=mla_w_out, ret_w_in=ret_w_in, ret_gn_g=ret_gn_g, ret_w_out=ret_w_out)
    m = dict(ln1_g=m_ln1_g, ln1_b=m_ln1_b, ln2_g=m_ln2_g, ln2_b=m_ln2_b, ffn_w_up=m_ffn_w_up, ffn_conv_w=m_ffn_conv_w, ffn_conv_b=m_ffn_conv_b, ffn_w_down=m_ffn_w_down, lru_w_in=m_lru_w_in, lru_conv_w=m_lru_conv_w, lru_conv_b=m_lru_conv_b, lru_w_a=m_lru_w_a, lru_b_a=m_lru_b_a, lru_w_x=m_lru_w_x, lru_b_x=m_lru_b_x, lru_lambda=m_lru_lambda, lru_w_out=m_lru_w_out, mla_w_in=m_mla_w_in, mla_q_norm=m_mla_q_norm, mla_kv_norm=m_mla_kv_norm, mla_w_uq=m_mla_w_uq, mla_w_ukv=m_mla_w_ukv, mla_w_out=m_mla_w_out, ret_w_in=m_ret_w_in, ret_gn_g=m_ret_gn_g, ret_w_out=m_ret_w_out)
    v = dict(ln1_g=v_ln1_g, ln1_b=v_ln1_b, ln2_g=v_ln2_g, ln2_b=v_ln2_b, ffn_w_up=v_ffn_w_up, ffn_conv_w=v_ffn_conv_w, ffn_conv_b=v_ffn_conv_b, ffn_w_down=v_ffn_w_down, lru_w_in=v_lru_w_in, lru_conv_w=v_lru_conv_w, lru_conv_b=v_lru_conv_b, lru_w_a=v_lru_w_a, lru_b_a=v_lru_b_a, lru_w_x=v_lru_w_x, lru_b_x=v_lru_b_x, lru_lambda=v_lru_lambda, lru_w_out=v_lru_w_out, mla_w_in=v_mla_w_in, mla_q_norm=v_mla_q_norm, mla_kv_norm=v_mla_kv_norm, mla_w_uq=v_mla_w_uq, mla_w_ukv=v_mla_w_ukv, mla_w_out=v_mla_w_out, ret_w_in=v_ret_w_in, ret_gn_g=v_ret_gn_g, ret_w_out=v_ret_w_out)
    D = x.shape[-1]
    axis_of = {n: ax for n, ax, _ in SHARDED}
    big = list(BIG_AXIS)
    small_mx = [n for n, _, mx in SHARDED if mx and n not in BIG_AXIS]
    small_vec = [n for n, _, mx in SHARDED if not mx]
    small = small_mx + small_vec

    gathered = _gather_chips([(w[n].astype(BF16), BIG_AXIS[n]) for n in big]
                             + [(_pack([w[n] for n in small_mx], BF16), None), (_pack([w[n] for n in small_vec], F32), None)],
                             "gather_weights")
    full = dict(zip(big, gathered))
    for names, buf in ((small_mx, gathered[-2]), (small_vec, gathered[-1])):
        blocks = _unpack(buf, [w[n].shape for n in names], lead=(N_CHIPS,))
        for n, blk in zip(names, blocks):
            full[n] = jnp.concatenate([blk[s] for s in range(N_CHIPS)], axis=axis_of[n])
    rep = {n: w[n] for n in REPLICATED}

    part, grad_x, grads, gbuf = _local_step(x[0], positions[0], loss_target[0], full, rep)
    loss = lax.psum((0.5 / D) * jnp.sum(part), MESH_AXES)

    g_pack = _pack([jnp.stack(jnp.split(grads[n], N_CHIPS, axis=axis_of[n])) for n in small], F32, lead=(N_CHIPS,))
    recv = _scatter_chips([(gbuf[n][1], BIG_AXIS[n]) for n in big] + [(g_pack, None)], "scatter_grads")
    sums = [_sum_partials(gbuf[n][0], r, BIG_AXIS[n], name="sum_" + n) for n, r in zip(big, recv)]
    sums.append(_sum_partials(g_pack[:, None], recv[-1][:, None], None, name="sum_small"))
    sibs = _swap_sibling(sums, "swap_core_partials")
    res = {kind: {} for kind in "gdmv"}
    for n, p_mine, p_sib in zip(big, sums, sibs):
        for kind, o in zip("gdmv", _adamw(w[n], m[n], v[n], p_mine, p_sib, name="adamw_" + n)):
            res[kind][n] = o
    spack = lambda d: _pack([d[n] for n in small], F32)[None]
    shapes = [w[n].shape for n in small]
    for kind, o in zip("gdmv", _adamw(spack(w), spack(m), spack(v), sums[-1], sibs[-1], name="adamw_small")):
        res[kind].update(zip(small, _unpack(o[0], shapes)))

    r_shapes = [w[n].shape for n in REPLICATED]
    rpack = lambda d: _pack([d[n] for n in REPLICATED], F32, rows=SUBLANES)
    r_sum = _allreduce_small(rpack(grads), "allreduce_replicated")
    r_outs = _adamw(rpack(w)[None], rpack(m)[None], rpack(v)[None], r_sum[None], None, name="adamw_replicated")
    for kind, o in zip("gdmv", r_outs):
        res[kind].update(zip(REPLICATED, _unpack(o[0], r_shapes)))

    return (loss, grad_x[None], *[res["g"][n] for n in WEIGHTS], *[res["d"][n] for n in WEIGHTS],
            *[res["m"][n] for n in WEIGHTS], *[res["v"][n] for n in WEIGHTS])
```

```python
import functools
import math

import numpy as np
import jax
import jax.numpy as jnp
from jax import lax
from jax.experimental import pallas as pl
from jax.experimental.pallas import tpu as pltpu

F32 = jnp.float32
BF16 = jnp.bfloat16

DEPTH = 4
ALPHA = (2.0 * DEPTH) ** 0.25
LN_EPS = 1e-5
RMS_EPS = 1e-6
ROPE_BASE = 10000.0
CHUNK = 64
LRU_C = 8.0
LRU_GROUPS = 4
MLA_HEADS = 16
QK_NOPE, QK_ROPE, V_HEAD = 64, 32, 64
Q_LORA, KV_LORA = 768, 256
RET_HEADS = 4
ADAM_LR, ADAM_B1, ADAM_B2, ADAM_EPS, ADAM_WD, ADAM_STEP = 0.001, 0.9, 0.999, 1e-08, 0.01, 10

LANES = 128
SUBLANES = 8
VMEM_LIMIT = 56 * 1024 * 1024

MESH_AXES = ("x", "y", "c")
N_CHIPS = 4


def _cp(sem):
    return pltpu.CompilerParams(dimension_semantics=sem, vmem_limit_bytes=VMEM_LIMIT)


def _sigmoid(x):
    return 1.0 / (1.0 + jnp.exp(-x))


_GELU_C = math.sqrt(2.0 / math.pi)


def _gelu_parts(x):
    x2 = x * x
    u = _GELU_C * (x + 0.044715 * x * x2)
    t = jnp.tanh(u)
    g = 0.5 * x * (1.0 + t)
    dg = 0.5 * (1.0 + t) + 0.5 * x * (1.0 - t * t) * _GELU_C * (1.0 + 3.0 * 0.044715 * x2)
    return g, dg


def _fold8(v):
    n = v.shape[0] // SUBLANES
    return v.reshape(n, SUBLANES, v.shape[1]).sum(axis=0)


def _dot(a, b):
    return jnp.dot(a, b, preferred_element_type=F32)


def _dot_tb(a, b):
    return lax.dot_general(a, b, (((1,), (1,)), ((), ())), preferred_element_type=F32)


def _dot_ta(a, b):
    return lax.dot_general(a, b, (((0,), (0,)), ((), ())), preferred_element_type=F32)


def _mm(a, b, *, out_dtype, tm, tn, name, a_koff=0):
    M = a.shape[0]
    K, N = b.shape

    def body(a_ref, b_ref, o_ref):
        o_ref[...] = _dot(a_ref[...].astype(BF16), b_ref[...].astype(BF16)).astype(out_dtype)

    return pl.pallas_call(
        body, name=name, grid=(M // tm, N // tn),
        in_specs=[pl.BlockSpec((tm, K), lambda i, j: (i, a_koff)),
                  pl.BlockSpec((K, tn), lambda i, j: (0, j))],
        out_specs=pl.BlockSpec((tm, tn), lambda i, j: (i, j)),
        out_shape=jax.ShapeDtypeStruct((M, N), out_dtype),
        compiler_params=_cp(("parallel", "parallel")),
    )(a, b)


def _mm_tb(pairs, b, *, out_dtype, tm, tk, name):
    M = pairs[0][0].shape[0]
    Kout = b.shape[0]
    n = len(pairs)

    def body(*refs):
        a_refs, b_refs, o_ref = refs[:n], refs[n:2 * n], refs[2 * n]
        acc = None
        for a_ref, b_ref in zip(a_refs, b_refs):
            t = _dot_tb(a_ref[...].astype(BF16), b_ref[...].astype(BF16))
            acc = t if acc is None else acc + t
        o_ref[...] = acc.astype(out_dtype)

    in_specs = [pl.BlockSpec((tm, a.shape[1]), lambda i, j: (i, 0)) for a, _ in pairs]
    for a, c0 in pairs:
        w = a.shape[1]
        assert c0 % w == 0
        in_specs.append(pl.BlockSpec((tk, w), functools.partial(lambda i, j, cb: (j, cb), cb=c0 // w)))
    return pl.pallas_call(
        body, name=name, grid=(M // tm, Kout // tk),
        in_specs=in_specs,
        out_specs=pl.BlockSpec((tm, tk), lambda i, j: (i, j)),
        out_shape=jax.ShapeDtypeStruct((M, Kout), out_dtype),
        compiler_params=_cp(("parallel", "parallel")),
    )(*[a for a, _ in pairs], *[b for _ in pairs])


def _mm_ta(a, b, *, tk, tn, tm, name, a_c0=0, a_w=None, b_c0=0, b_w=None, dest=None):
    M = a.shape[0]
    nm = M // tm
    a_w = a.shape[1] if a_w is None else a_w
    b_w = b.shape[1] if b_w is None else b_w
    assert a_c0 % tk == 0 and b_c0 % tn == 0 and a_w % tk == 0 and b_w % tn == 0

    def body(*refs):
        a_ref, b_ref = refs[0], refs[1]
        o_ref = refs[-1] if dest is None else refs[-2]

        @pl.when(pl.program_id(2) == 0)
        def _():
            o_ref[...] = jnp.zeros_like(o_ref)
        o_ref[...] += _dot_ta(a_ref[...].astype(BF16), b_ref[...].astype(BF16))

        if dest is not None:
            @pl.when(pl.program_id(2) == nm - 1)
            def _():
                refs[-1][...] = o_ref[...].astype(BF16)

    in_specs = [pl.BlockSpec((tm, tk), lambda i, j, m: (m, i + a_c0 // tk)),
                pl.BlockSpec((tm, tn), lambda i, j, m: (m, j + b_c0 // tn))]
    args = [a, b]
    if dest is None:
        out_spec = pl.BlockSpec((tk, tn), lambda i, j, m: (i, j))
        out_shape = jax.ShapeDtypeStruct((a_w, b_w), F32)
        aliases = {}
    else:
        bufs, full_shape, layer, r0, c0 = dest
        assert r0 % tk == 0 and c0 % tn == 0
        spec = pl.BlockSpec((None, tk, tn), lambda i, j, m: (layer, i + r0 // tk, j + c0 // tn))
        out_spec = [spec, spec]
        out_shape = [jax.ShapeDtypeStruct(full_shape, F32), jax.ShapeDtypeStruct(full_shape, BF16)]
        aliases = {}
        if bufs is not None:
            in_specs += [pl.BlockSpec(memory_space=pl.ANY)] * 2
            args += list(bufs)
            aliases = {2: 0, 3: 1}
    return pl.pallas_call(
        body, name=name, grid=(a_w // tk, b_w // tn, nm),
        in_specs=in_specs, out_specs=out_spec, out_shape=out_shape, input_output_aliases=aliases,
        compiler_params=_cp(("parallel", "parallel", "arbitrary")),
    )(*args)


def _grad_into(gbuf, key, full_shape, layer, r0, c0, a, b, **kw):
    gbuf[key] = tuple(_mm_ta(a, b, dest=(gbuf.get(key), full_shape, layer, r0, c0), **kw))


def _ln_fwd(x, mix, g, b, *, ts, name):
    S, D = x.shape

    def body(x_ref, m_ref, g_ref, b_ref, o_ref, z_ref):
        z = ALPHA * x_ref[...] + m_ref[...]
        mu = jnp.mean(z, axis=-1, keepdims=True)
        zc = z - mu
        var = jnp.mean(zc * zc, axis=-1, keepdims=True)
        o_ref[...] = zc * lax.rsqrt(var + LN_EPS) * g_ref[...] + b_ref[...]
        z_ref[...] = z

    row = pl.BlockSpec((ts, D), lambda i: (i, 0))
    vec = pl.BlockSpec((1, D), lambda i: (0, 0))
    return pl.pallas_call(
        body, name=name, grid=(S // ts,),
        in_specs=[row, row, vec, vec], out_specs=[row, row],
        out_shape=[jax.ShapeDtypeStruct((S, D), F32)] * 2,
        compiler_params=_cp(("parallel",)),
    )(x, mix, g.reshape(1, D), b.reshape(1, D))


def _ln_bwd(da, db, z, g, *, ts, name):
    S, D = z.shape
    two = db is not None

    def body(*refs):
        if two:
            da_ref, db_ref, z_ref, g_ref, dz_ref, dg_ref, dbias_ref = refs
            dout = ALPHA * da_ref[...] + db_ref[...]
        else:
            da_ref, z_ref, g_ref, dz_ref, dg_ref, dbias_ref = refs
            dout = da_ref[...]

        @pl.when(pl.program_id(0) == 0)
        def _():
            dg_ref[...] = jnp.zeros_like(dg_ref)
            dbias_ref[...] = jnp.zeros_like(dbias_ref)

        z = z_ref[...]
        mu = jnp.mean(z, axis=-1, keepdims=True)
        zc = z - mu
        var = jnp.mean(zc * zc, axis=-1, keepdims=True)
        rstd = lax.rsqrt(var + LN_EPS)
        xhat = zc * rstd
        dxh = dout * g_ref[...]
        m1 = jnp.mean(dxh, axis=-1, keepdims=True)
        m2 = jnp.mean(dxh * xhat, axis=-1, keepdims=True)
        dz_ref[...] = rstd * (dxh - m1 - xhat * m2)
        dg_ref[...] += _fold8(dout * xhat)
        dbias_ref[...] += _fold8(dout)

    row = pl.BlockSpec((ts, D), lambda i: (i, 0))
    vec = pl.BlockSpec((1, D), lambda i: (0, 0))
    acc = pl.BlockSpec((SUBLANES, D), lambda i: (0, 0))
    args = [da, db, z, g.reshape(1, D)] if two else [da, z, g.reshape(1, D)]
    return pl.pallas_call(
        body, name=name, grid=(S // ts,),
        in_specs=[row] * (3 if two else 2) + [vec],
        out_specs=[row, acc, acc],
        out_shape=[jax.ShapeDtypeStruct((S, D), F32), jax.ShapeDtypeStruct((SUBLANES, D), F32),
                   jax.ShapeDtypeStruct((SUBLANES, D), F32)],
        compiler_params=_cp(("arbitrary",)),
    )(*args)


def _loss_head(y, t, *, ts, name):
    S, D = y.shape

    def body(y_ref, t_ref, dy_ref, p_ref):
        @pl.when(pl.program_id(0) == 0)
        def _():
            p_ref[...] = jnp.zeros_like(p_ref)
        d = y_ref[...] - t_ref[...]
        dy_ref[...] = d * (1.0 / D)
        p_ref[...] += _fold8(d * d)

    row = pl.BlockSpec((ts, D), lambda i: (i, 0))
    acc = pl.BlockSpec((SUBLANES, D), lambda i: (0, 0))
    return pl.pallas_call(
        body, name=name, grid=(S // ts,),
        in_specs=[row, row], out_specs=[row, acc],
        out_shape=[jax.ShapeDtypeStruct((S, D), F32), jax.ShapeDtypeStruct((SUBLANES, D), F32)],
        compiler_params=_cp(("arbitrary",)),
    )(y, t)


def _axpy(a, b, *, ts, name):
    S, D = a.shape

    def body(a_ref, b_ref, o_ref):
        o_ref[...] = ALPHA * a_ref[...] + b_ref[...]

    row = pl.BlockSpec((ts, D), lambda i: (i, 0))
    return pl.pallas_call(
        body, name=name, grid=(S // ts,), in_specs=[row, row], out_specs=row,
        out_shape=jax.ShapeDtypeStruct((S, D), F32), compiler_params=_cp(("parallel",)),
    )(a, b)


def _prev_halo_spec(ts, tc, coff):
    r = ts // SUBLANES
    return pl.BlockSpec((SUBLANES, tc), lambda i, j: (jnp.maximum(i * r - 1, 0), j + coff))


def _fill_prev(buf, halo_ref, cur, i):
    buf[0:SUBLANES, :] = jnp.where(i > 0, halo_ref[...], 0.0)
    buf[SUBLANES:, :] = cur


def _conv_fwd(x, w, b, *, K, ts, tc, x_c0, name):
    S = x.shape[0]
    C = w.shape[1]
    coff = x_c0 // tc

    def body(x_ref, halo_ref, w_ref, b_ref, o_ref, buf):
        _fill_prev(buf, halo_ref, x_ref[...], pl.program_id(0))
        acc = b_ref[...] + w_ref[K - 1:K, :] * x_ref[...]
        for k in range(K - 1):
            acc = acc + w_ref[k:k + 1, :] * buf[pl.ds(SUBLANES - (K - 1) + k, ts), :]
        o_ref[...] = acc

    return pl.pallas_call(
        body, name=name, grid=(S // ts, C // tc),
        in_specs=[pl.BlockSpec((ts, tc), lambda i, j: (i, j + coff)), _prev_halo_spec(ts, tc, coff),
                  pl.BlockSpec((K, tc), lambda i, j: (0, j)), pl.BlockSpec((1, tc), lambda i, j: (0, j))],
        out_specs=pl.BlockSpec((ts, tc), lambda i, j: (i, j)),
        out_shape=jax.ShapeDtypeStruct((S, C), F32),
        scratch_shapes=[pltpu.VMEM((ts + SUBLANES, tc), F32)],
        compiler_params=_cp(("parallel", "parallel")),
    )(x, x, w, b.reshape(1, C))


def _conv_wgrad(dy, x, *, K, ts, tc, x_c0, name):
    S, C = dy.shape
    coff = x_c0 // tc

    def body(dy_ref, x_ref, halo_ref, dw_ref, db_ref, buf):
        i = pl.program_id(1)

        @pl.when(i == 0)
        def _():
            dw_ref[...] = jnp.zeros_like(dw_ref)
            db_ref[...] = jnp.zeros_like(db_ref)

        _fill_prev(buf, halo_ref, x_ref[...], i)
        dy_v = dy_ref[...]
        db_ref[...] += _fold8(dy_v)
        for k in range(K):
            xs = buf[pl.ds(SUBLANES - (K - 1) + k, ts), :]
            dw_ref[k] += _fold8(dy_v * xs)

    r = ts // SUBLANES
    return pl.pallas_call(
        body, name=name, grid=(C // tc, S // ts),
        in_specs=[pl.BlockSpec((ts, tc), lambda j, i: (i, j)),
                  pl.BlockSpec((ts, tc), lambda j, i: (i, j + coff)),
                  pl.BlockSpec((SUBLANES, tc), lambda j, i: (jnp.maximum(i * r - 1, 0), j + coff))],
        out_specs=[pl.BlockSpec((K, SUBLANES, tc), lambda j, i: (0, 0, j)),
                   pl.BlockSpec((SUBLANES, tc), lambda j, i: (0, j))],
        out_shape=[jax.ShapeDtypeStruct((K, SUBLANES, C), F32), jax.ShapeDtypeStruct((SUBLANES, C), F32)],
        scratch_shapes=[pltpu.VMEM((ts + SUBLANES, tc), F32)],
        compiler_params=_cp(("parallel", "arbitrary")),
    )(dy, x, x)


def _conv_bwd(dy, w, *, K, ts, tc, w_c0, out_dtype, name):
    S, C = dy.shape
    nb = S // ts
    r = ts // SUBLANES
    woff = w_c0 // tc

    def body(dy_ref, halo_ref, w_ref, o_ref, buf):
        i = pl.program_id(0)
        buf[0:ts, :] = dy_ref[...]
        buf[ts:, :] = jnp.where(i < nb - 1, halo_ref[...], 0.0)
        acc = w_ref[K - 1:K, :] * dy_ref[...]
        for k in range(K - 1):
            acc = acc + w_ref[k:k + 1, :] * buf[pl.ds(K - 1 - k, ts), :]
        o_ref[...] = acc.astype(out_dtype)

    return pl.pallas_call(
        body, name=name, grid=(nb, C // tc),
        in_specs=[pl.BlockSpec((ts, tc), lambda i, j: (i, j)),
                  pl.BlockSpec((SUBLANES, tc), lambda i, j: (jnp.minimum((i + 1) * r, nb * r - 1), j)),
                  pl.BlockSpec((K, tc), lambda i, j: (0, j + woff))],
        out_specs=pl.BlockSpec((ts, tc), lambda i, j: (i, j)),
        out_shape=jax.ShapeDtypeStruct((S, C), out_dtype),
        scratch_shapes=[pltpu.VMEM((ts + SUBLANES, tc), F32)],
        compiler_params=_cp(("parallel", "parallel")),
    )(dy, dy, w)


HALO16 = 16
FFN_UNROLL = 4
FFN_FWD_UNROLL = 8


def _ffn_taps_w(w_ref, cs):
    return [w_ref[k:k + 1, cs] for k in range(3)]


def _ffn_taps8(prev, cur):
    row = lax.broadcasted_iota(jnp.int32, cur.shape, 0)
    return (jnp.where(row < 2, pltpu.roll(prev, 2, 0), pltpu.roll(cur, 2, 0)),
            jnp.where(row < 1, pltpu.roll(prev, 1, 0), pltpu.roll(cur, 1, 0)), cur)


def _ffn_conv8(taps, w, b):
    return b + w[0] * taps[0] + w[1] * taps[1] + w[2] * taps[2]


def _ffn_conv8_t(dh, dh_next, w):
    row = lax.broadcasted_iota(jnp.int32, dh.shape, 0)
    s1 = jnp.where(row < SUBLANES - 1, pltpu.roll(dh, SUBLANES - 1, 0), pltpu.roll(dh_next, SUBLANES - 1, 0))
    s2 = jnp.where(row < SUBLANES - 2, pltpu.roll(dh, SUBLANES - 2, 0), pltpu.roll(dh_next, SUBLANES - 2, 0))
    return w[2] * dh + w[1] * s1 + w[0] * s2


def _ffn_mid_specs(ts, tc, nf, nb, with_next):
    r = ts // HALO16
    specs = []
    for off in (0, nf):
        specs.append(pl.BlockSpec((ts, tc), functools.partial(lambda j, i, o: (i, j + o), o=off)))
        specs.append(pl.BlockSpec((HALO16, tc), functools.partial(lambda j, i, o: (jnp.maximum(i * r - 1, 0), j + o), o=off)))
        if with_next:
            specs.append(pl.BlockSpec(
                (HALO16, tc), functools.partial(lambda j, i, o: (jnp.minimum((i + 1) * r, nb * r - 1), j + o), o=off)))
    for rows in (3, 1):
        for off in (0, nf):
            specs.append(pl.BlockSpec((rows, tc), functools.partial(lambda j, i, o: (0, j + o), o=off)))
    return specs


def _ffn_mid_fwd(hpre, w, b, *, ts, tc, name):
    S, F2 = hpre.shape
    F = F2 // 2
    nf = F // tc

    def body(g_ref, gp_ref, u_ref, up_ref, wg_ref, wu_ref, bg_ref, bu_ref, o_ref, gbuf, ubuf, obuf):
        first = pl.program_id(1) == 0
        for buf, prev, cur in ((gbuf, gp_ref, g_ref), (ubuf, up_ref, u_ref)):
            buf[0:HALO16, :] = jnp.where(first, 0.0, prev[...].astype(F32))
            buf[HALO16:, :] = cur[...].astype(F32)
        for lt in range(tc // LANES):
            cs = slice(lt * LANES, (lt + 1) * LANES)
            wg, wu = _ffn_taps_w(wg_ref, cs), _ffn_taps_w(wu_ref, cs)
            bg, bu = bg_ref[:, cs], bu_ref[:, cs]

            def step(c, carry):
                a_g, a_u = carry
                for un in range(FFN_FWD_UNROLL):
                    r0 = pl.multiple_of(c * (FFN_FWD_UNROLL * SUBLANES), SUBLANES) + un * SUBLANES
                    b_g = gbuf[pl.ds(HALO16 + r0, SUBLANES), cs]
                    b_u = ubuf[pl.ds(HALO16 + r0, SUBLANES), cs]
                    gel, _ = _gelu_parts(_ffn_conv8(_ffn_taps8(a_g, b_g), wg, bg))
                    obuf[pl.ds(r0, SUBLANES), cs] = gel * _ffn_conv8(_ffn_taps8(a_u, b_u), wu, bu)
                    a_g, a_u = b_g, b_u
                return a_g, a_u

            lax.fori_loop(0, ts // (FFN_FWD_UNROLL * SUBLANES), step,
                          (gbuf[HALO16 - SUBLANES:HALO16, cs], ubuf[HALO16 - SUBLANES:HALO16, cs]))
        o_ref[...] = obuf[...].astype(BF16)

    b2 = b.reshape(1, F2)
    return pl.pallas_call(
        body, name=name, grid=(nf, S // ts),
        in_specs=_ffn_mid_specs(ts, tc, nf, S // ts, False),
        out_specs=pl.BlockSpec((ts, tc), lambda j, i: (i, j)),
        out_shape=jax.ShapeDtypeStruct((S, F), BF16),
        scratch_shapes=[pltpu.VMEM((ts + HALO16, tc), F32)] * 2 + [pltpu.VMEM((ts, tc), F32)],
        compiler_params=_cp(("parallel", "parallel")),
    )(hpre, hpre, hpre, hpre, w, w, b2, b2)


def _ffn_mid_bwd(hpre, da, w, b, *, ts, tc, name):
    S, F2 = hpre.shape
    F = F2 // 2
    nf = F // tc
    nb = S // ts
    r = ts // HALO16
    nch = ts // SUBLANES

    def body(g_ref, gp_ref, gn_ref, u_ref, up_ref, un_ref, wg_ref, wu_ref, bg_ref, bu_ref, da_ref, dan_ref,
             dpg_ref, dpu_ref, dwg_ref, dwu_ref, dbg_ref, dbu_ref, gbuf, ubuf, dabuf, pgbuf, pubuf):
        i = pl.program_id(1)

        @pl.when(i == 0)
        def _():
            for ref in (dwg_ref, dwu_ref, dbg_ref, dbu_ref):
                ref[...] = jnp.zeros_like(ref)

        for buf, prev, cur, nxt in ((gbuf, gp_ref, g_ref, gn_ref), (ubuf, up_ref, u_ref, un_ref)):
            buf[0:HALO16, :] = jnp.where(i == 0, 0.0, prev[...].astype(F32))
            buf[HALO16:HALO16 + ts, :] = cur[...].astype(F32)
            buf[HALO16 + ts:, :] = nxt[...].astype(F32)
        dabuf[0:ts, :] = da_ref[...].astype(F32)
        dabuf[ts:, :] = jnp.where(i == nb - 1, 0.0, dan_ref[...].astype(F32))

        for lt in range(tc // LANES):
            cs = slice(lt * LANES, (lt + 1) * LANES)
            wg, wu = _ffn_taps_w(wg_ref, cs), _ffn_taps_w(wu_ref, cs)
            bg, bu = bg_ref[:, cs], bu_ref[:, cs]

            def piece(r0, a_g, a_u):
                b_g = gbuf[pl.ds(HALO16 + r0, SUBLANES), cs]
                b_u = ubuf[pl.ds(HALO16 + r0, SUBLANES), cs]
                tg, tu = _ffn_taps8(a_g, b_g), _ffn_taps8(a_u, b_u)
                gel, dgel = _gelu_parts(_ffn_conv8(tg, wg, bg))
                da_v = dabuf[pl.ds(r0, SUBLANES), cs]
                return da_v * _ffn_conv8(tu, wu, bu) * dgel, da_v * gel, tg, tu, b_g, b_u

            def step(c, carry):
                a_g, a_u, pdg, pdu, acc = carry
                for un in range(FFN_UNROLL):
                    r0 = pl.multiple_of(c * (FFN_UNROLL * SUBLANES), SUBLANES) + un * SUBLANES
                    dg, du, tg, tu, a_g, a_u = piece(r0, a_g, a_u)
                    pgbuf[pl.ds(r0, SUBLANES), cs] = _ffn_conv8_t(pdg, dg, wg)
                    pubuf[pl.ds(r0, SUBLANES), cs] = _ffn_conv8_t(pdu, du, wu)
                    acc = (tuple(a + dg * t for a, t in zip(acc[0], tg)), tuple(a + du * t for a, t in zip(acc[1], tu)),
                           acc[2] + dg, acc[3] + du)
                    pdg, pdu = dg, du
                return a_g, a_u, pdg, pdu, acc

            zero = jnp.zeros((SUBLANES, LANES), F32)
            a_g, a_u, pdg, pdu, acc = lax.fori_loop(
                0, nch // FFN_UNROLL, step,
                (gbuf[HALO16 - SUBLANES:HALO16, cs], ubuf[HALO16 - SUBLANES:HALO16, cs], zero, zero,
                 ((zero,) * 3, (zero,) * 3, zero, zero)))
            dg, du, _, _, _, _ = piece(ts, a_g, a_u)
            pgbuf[ts:ts + SUBLANES, cs] = _ffn_conv8_t(pdg, dg, wg)
            pubuf[ts:ts + SUBLANES, cs] = _ffn_conv8_t(pdu, du, wu)
            for k in range(3):
                dwg_ref[k, :, cs] += acc[0][k]
                dwu_ref[k, :, cs] += acc[1][k]
            dbg_ref[:, cs] += acc[2]
            dbu_ref[:, cs] += acc[3]
        dpg_ref[...] = pgbuf[SUBLANES:, :].astype(BF16)
        dpu_ref[...] = pubuf[SUBLANES:, :].astype(BF16)

    b2 = b.reshape(1, F2)
    blk = pl.BlockSpec((ts, tc), lambda j, i: (i, j))
    nxt = pl.BlockSpec((HALO16, tc), lambda j, i: (jnp.minimum((i + 1) * r, nb * r - 1), j))
    in_specs = _ffn_mid_specs(ts, tc, nf, nb, True) + [blk, nxt]
    out_specs = [blk, blk,
                 pl.BlockSpec((3, SUBLANES, tc), lambda j, i: (0, 0, j)), pl.BlockSpec((3, SUBLANES, tc), lambda j, i: (0, 0, j)),
                 pl.BlockSpec((SUBLANES, tc), lambda j, i: (0, j)), pl.BlockSpec((SUBLANES, tc), lambda j, i: (0, j))]
    return pl.pallas_call(
        body, name=name, grid=(nf, nb),
        in_specs=in_specs, out_specs=out_specs,
        out_shape=[jax.ShapeDtypeStruct((S, F), BF16)] * 2 + [jax.ShapeDtypeStruct((3, SUBLANES, F), F32)] * 2
                  + [jax.ShapeDtypeStruct((SUBLANES, F), F32)] * 2,
        scratch_shapes=[pltpu.VMEM((ts + 2 * HALO16, tc), F32)] * 2 + [pltpu.VMEM((ts + HALO16, tc), F32)]
                       + [pltpu.VMEM((ts + SUBLANES, tc), F32)] * 2,
        compiler_params=_cp(("parallel", "arbitrary")),
    )(hpre, hpre, hpre, hpre, hpre, hpre, w, w, b2, b2, da, da)


def _expm1(x):
    u = jnp.exp(x)
    um1 = u - 1.0
    safe = jnp.where(um1 == 0.0, 1.0, jnp.log(u))
    r = jnp.where(um1 == 0.0, x, um1 * x / safe)
    return jnp.where(x < -30.0, -1.0, r)


def _softplus(z):
    return jnp.maximum(z, 0.0) + jnp.log1p(jnp.exp(-jnp.abs(z)))


def _lru_gates(u, wa_ref, ba_ref, wx_ref, bx_ref, lam_ref):
    ub = u.astype(BF16)
    r = _sigmoid(_dot(ub, wa_ref[...]) + ba_ref[...])
    ig = _sigmoid(_dot(ub, wx_ref[...]) + bx_ref[...])
    sp = _softplus(-lam_ref[...])
    la = -LRU_C * r * sp
    a = jnp.exp(la)
    mult = jnp.sqrt(-_expm1(2.0 * la))
    return ub, r, ig, sp, a, mult


def _lru_fwd(u, proj, w_a, b_a, w_x, b_x, lam, *, ts, name):
    S, W = u.shape
    G = LRU_GROUPS
    gw = W // G
    nt = ts // SUBLANES

    def body(u_ref, gb_ref, wa_ref, ba_ref, wx_ref, bx_ref, lam_ref, y_ref, h_ref, a_buf, b_buf, carry):
        @pl.when(pl.program_id(0) == 0)
        def _():
            carry[...] = jnp.zeros_like(carry)

        for g in range(G):
            gs = slice(g * gw, (g + 1) * gw)
            u_v = u_ref[:, gs]
            _, _, ig, _, a, mult = _lru_gates(u_v, wa_ref.at[g], ba_ref.at[g], wx_ref.at[g], bx_ref.at[g], lam_ref.at[g])
            a_buf[:, gs] = a
            b_buf[:, gs] = mult * ig * u_v
        row = lax.broadcasted_iota(jnp.int32, (SUBLANES, W), 0)

        def tile(k, c):
            r0 = pl.multiple_of(k * SUBLANES, SUBLANES)
            A = a_buf[pl.ds(r0, SUBLANES), :]
            B = b_buf[pl.ds(r0, SUBLANES), :]
            for d in (1, 2, 4):
                m = row >= d
                B = jnp.where(m, A * pltpu.roll(B, d, 0) + B, B)
                A = jnp.where(m, A * pltpu.roll(A, d, 0), A)
            h = A * c + B
            h_ref[pl.ds(r0, SUBLANES), :] = h
            return h[SUBLANES - 1:SUBLANES, :]

        carry[...] = lax.fori_loop(0, nt, tile, carry[...])
        gel, _ = _gelu_parts(gb_ref[...])
        y_ref[...] = (gel * h_ref[...]).astype(BF16)

    blk = pl.BlockSpec((ts, W), lambda i: (i, 0))
    wsp = pl.BlockSpec((G, gw, gw), lambda i: (0, 0, 0))
    vsp = pl.BlockSpec((G, 1, gw), lambda i: (0, 0, 0))
    return pl.pallas_call(
        body, name=name, grid=(S // ts,),
        in_specs=[blk, blk, wsp, vsp, wsp, vsp, vsp],
        out_specs=[blk, blk],
        out_shape=[jax.ShapeDtypeStruct((S, W), BF16), jax.ShapeDtypeStruct((S, W), F32)],
        scratch_shapes=[pltpu.VMEM((ts, W), F32), pltpu.VMEM((ts, W), F32), pltpu.VMEM((1, W), F32)],
        compiler_params=_cp(("arbitrary",)),
    )(u, proj, w_a, b_a.reshape(G, 1, gw), w_x, b_x.reshape(G, 1, gw), lam.reshape(G, 1, gw))


def _lru_bwd(dy, u, proj, h, w_a, b_a, w_x, b_x, lam, *, ts, name):
    S, W = u.shape
    G = LRU_GROUPS
    gw = W // G
    nt = ts // SUBLANES
    nb = S // ts
    r8 = ts // SUBLANES

    def body(dy_ref, u_ref, gb_ref, h_ref, hh_ref, wa_ref, ba_ref, wx_ref, bx_ref, lam_ref,
             dgb_ref, du_ref, dwa_ref, dwx_ref, dba_ref, dbx_ref, dlam_ref,
             a_buf, q_buf, p_buf, hbuf, carry):
        i = pl.program_id(0)
        ib = nb - 1 - i

        @pl.when(i == 0)
        def _():
            carry[...] = jnp.zeros_like(carry)
            dwa_ref[...] = jnp.zeros_like(dwa_ref)
            dwx_ref[...] = jnp.zeros_like(dwx_ref)
            dba_ref[...] = jnp.zeros_like(dba_ref)
            dbx_ref[...] = jnp.zeros_like(dbx_ref)
            dlam_ref[...] = jnp.zeros_like(dlam_ref)

        def gates(g):
            gs = slice(g * gw, (g + 1) * gw)
            return gs, _lru_gates(u_ref[:, gs], wa_ref.at[g], ba_ref.at[g], wx_ref.at[g], bx_ref.at[g], lam_ref.at[g])

        for g in range(G):
            gs, (_, _, _, _, a, _) = gates(g)
            gel, dgel = _gelu_parts(gb_ref[:, gs])
            dy_v = dy_ref[:, gs]
            dgb_ref[:, gs] = (dy_v * h_ref[:, gs] * dgel).astype(BF16)
            a_buf[:, gs] = a
            q_buf[:, gs] = a * (dy_v * gel)
        row = lax.broadcasted_iota(jnp.int32, (SUBLANES, W), 0)

        def tile(kk, c):
            r0 = pl.multiple_of((nt - 1 - kk) * SUBLANES, SUBLANES)
            A = a_buf[pl.ds(r0, SUBLANES), :]
            B = q_buf[pl.ds(r0, SUBLANES), :]
            for d in (1, 2, 4):
                m = row < SUBLANES - d
                B = jnp.where(m, A * pltpu.roll(B, SUBLANES - d, 0) + B, B)
                A = jnp.where(m, A * pltpu.roll(A, SUBLANES - d, 0), A)
            P = A * c + B
            p_buf[pl.ds(r0, SUBLANES), :] = jnp.where(row == SUBLANES - 1, c, pltpu.roll(P, SUBLANES - 1, 0))
            return P[0:1, :]

        carry[...] = lax.fori_loop(0, nt, tile, carry[...])
        hbuf[0:SUBLANES, :] = jnp.where(ib > 0, hh_ref[...], 0.0)
        hbuf[SUBLANES:, :] = h_ref[...]
        for g in range(G):
            gs, (ub, r, ig, sp, a, mult) = gates(g)
            u_v = u_ref[:, gs]
            gel, _ = _gelu_parts(gb_ref[:, gs])
            Gt = dy_ref[:, gs] * gel + p_buf[:, gs]
            hprev = hbuf[pl.ds(SUBLANES - 1, ts), gs]
            da = Gt * hprev
            dmult = Gt * (ig * u_v)
            dla = da * a - dmult * (a * a) / mult
            dr = dla * (-LRU_C * sp)
            dlam_ref[g] += _fold8(dla * (LRU_C * r)) * _sigmoid(-lam_ref[g])
            dig = Gt * mult * u_v
            dzr = dr * r * (1.0 - r)
            dzi = dig * ig * (1.0 - ig)
            dzr_b = dzr.astype(BF16)
            dzi_b = dzi.astype(BF16)
            du_ref[:, gs] = Gt * mult * ig + _dot_tb(dzr_b, wa_ref[g]) + _dot_tb(dzi_b, wx_ref[g])
            dwa_ref[g] += _dot_ta(ub, dzr_b)
            dwx_ref[g] += _dot_ta(ub, dzi_b)
            dba_ref[g] += _fold8(dzr)
            dbx_ref[g] += _fold8(dzi)

    rblk = pl.BlockSpec((ts, W), lambda i: (nb - 1 - i, 0))
    halo = pl.BlockSpec((SUBLANES, W), lambda i: (jnp.maximum((nb - 1 - i) * r8 - 1, 0), 0))
    wsp = pl.BlockSpec((G, gw, gw), lambda i: (0, 0, 0))
    vsp = pl.BlockSpec((G, 1, gw), lambda i: (0, 0, 0))
    acc8 = pl.BlockSpec((G, SUBLANES, gw), lambda i: (0, 0, 0))
    return pl.pallas_call(
        body, name=name, grid=(nb,),
        in_specs=[rblk, rblk, rblk, rblk, halo, wsp, vsp, wsp, vsp, vsp],
        out_specs=[rblk, rblk, wsp, wsp, acc8, acc8, acc8],
        out_shape=[jax.ShapeDtypeStruct((S, W), BF16), jax.ShapeDtypeStruct((S, W), F32),
                   jax.ShapeDtypeStruct((G, gw, gw), F32), jax.ShapeDtypeStruct((G, gw, gw), F32),
                   jax.ShapeDtypeStruct((G, SUBLANES, gw), F32), jax.ShapeDtypeStruct((G, SUBLANES, gw), F32),
                   jax.ShapeDtypeStruct((G, SUBLANES, gw), F32)],
        scratch_shapes=[pltpu.VMEM((ts, W), F32)] * 3 + [pltpu.VMEM((ts + SUBLANES, W), F32),
                                                         pltpu.VMEM((1, W), F32)],
        compiler_params=_cp(("arbitrary",)),
    )(dy, u, proj, h, h, w_a, b_a.reshape(G, 1, gw), w_x, b_x.reshape(G, 1, gw), lam.reshape(G, 1, gw))


def _rt(S, pref):
    return min(S, pref)


def _lru_mixer_fwd(x, p, tag):
    S, D = x.shape
    W = p["w_out"].shape[0]
    ts = _rt(S, 512)
    proj = _mm(x, p["w_in"], out_dtype=F32, tm=ts, tn=2 * W, name=tag + "_proj")
    u = _conv_fwd(proj, p["conv_w"], p["conv_b"], K=4, ts=ts, tc=512, x_c0=W, name=tag + "_conv")
    y, h = _lru_fwd(u, proj, p["w_a"], p["b_a"], p["w_x"], p["b_x"], p["lam"], ts=ts, name=tag + "_scan")
    mix = _mm(y, p["w_out"], out_dtype=F32, tm=ts, tn=D, name=tag + "_out")
    return mix, (proj, u, h, y)


def _lru_mixer_bwd(dmix, x, p, saved, tag, gbuf, j):
    proj, u, h, y = saved
    S, D = x.shape
    W = p["w_out"].shape[0]
    ts = _rt(S, 512)
    g = {}
    dy = _mm_tb([(dmix, 0)], p["w_out"], out_dtype=F32, tm=ts, tk=W, name=tag + "_dy")
    n_lru = (DEPTH + 2) // 3
    _grad_into(gbuf, "lru_w_out", (n_lru, W, D), j, 0, 0, y, dmix, tk=W, tn=D, tm=ts, name=tag + "_dwout")
    dgb, du, g["w_a"], g["w_x"], dba8, dbx8, dlam8 = _lru_bwd(
        dy, u, proj, h, p["w_a"], p["b_a"], p["w_x"], p["b_x"], p["lam"], ts=ts, name=tag + "_scanb")
    g["b_a"] = dba8.sum(axis=1)
    g["b_x"] = dbx8.sum(axis=1)
    g["lam"] = dlam8.sum(axis=1).reshape(-1)
    dcw8, dcb8 = _conv_wgrad(du, proj, K=4, ts=ts, tc=512, x_c0=W, name=tag + "_convw")
    g["conv_w"] = dcw8.sum(axis=1)
    g["conv_b"] = dcb8.sum(axis=0)
    drnn = _conv_bwd(du, p["conv_w"], K=4, ts=ts, tc=512, w_c0=0, out_dtype=BF16, name=tag + "_convb")
    dx = _mm_tb([(dgb, 0), (drnn, W)], p["w_in"], out_dtype=F32, tm=ts, tk=D, name=tag + "_dx")
    _grad_into(gbuf, "lru_w_in", (n_lru, D, 2 * W), j, 0, 0, x, dgb, tk=D, tn=W, tm=ts, name=tag + "_dwin_g")
    _grad_into(gbuf, "lru_w_in", (n_lru, D, 2 * W), j, 0, W, x, drnn, tk=D, tn=W, tm=ts, name=tag + "_dwin_r")
    return dx, g


def _ffn_fwd(x, p, tag):
    S, D = x.shape
    F = p["w_down"].shape[0]
    ts = _rt(S, 512)
    tc = F // 2
    hpre = _mm(x, p["w_up"], out_dtype=BF16, tm=ts, tn=2 * F, name=tag + "_up")
    a = _ffn_mid_fwd(hpre, p["conv_w"], p["conv_b"], ts=_rt(S, 256), tc=tc, name=tag + "_mid")
    f = _mm(a, p["w_down"], out_dtype=F32, tm=ts, tn=D, name=tag + "_down")
    return f, (hpre, a)


def _ffn_bwd(df, x, p, saved, tag, gbuf, i):
    hpre, a = saved
    S, D = x.shape
    F = p["w_down"].shape[0]
    ts = _rt(S, 512)
    tw = _rt(S, 1024)
    tc = F // 2
    g = {}
    da = _mm_tb([(df, 0)], p["w_down"], out_dtype=BF16, tm=ts, tk=F, name=tag + "_da")
    _grad_into(gbuf, "ffn_w_down", (DEPTH, F, D), i, 0, 0, a, df, tk=tc, tn=D, tm=tw, name=tag + "_dwdown")
    dpg, dpu, dwg8, dwu8, dbg8, dbu8 = _ffn_mid_bwd(hpre, da, p["conv_w"], p["conv_b"], ts=_rt(S, 256), tc=tc,
                                                    name=tag + "_midb")
    g["conv_w"] = jnp.concatenate([dwg8.sum(axis=1), dwu8.sum(axis=1)], axis=1)
    g["conv_b"] = jnp.concatenate([dbg8.sum(axis=0), dbu8.sum(axis=0)], axis=0)
    dx = _mm_tb([(dpg, 0), (dpu, F)], p["w_up"], out_dtype=F32, tm=ts, tk=D, name=tag + "_dx")
    _grad_into(gbuf, "ffn_w_up", (DEPTH, D, 2 * F), i, 0, 0, x, dpg, tk=D, tn=tc, tm=tw, name=tag + "_dwup_g")
    _grad_into(gbuf, "ffn_w_up", (DEPTH, D, 2 * F), i, 0, F, x, dpu, tk=D, tn=tc, tm=tw, name=tag + "_dwup_u")
    return dx, g


HEAD_SLOT = LANES
MLA_SCALE = (QK_NOPE + QK_ROPE) ** -0.5
NEG_BIG = -1e30
ATTN_BLOCK = 1024


def _rms_fwd(x, g, *, c0, ts, name):
    S = x.shape[0]
    w = g.shape[0]

    def body(x_ref, g_ref, o_ref):
        xv = x_ref[...]
        rstd = lax.rsqrt(jnp.mean(xv * xv, axis=-1, keepdims=True) + RMS_EPS)
        o_ref[...] = (xv * rstd * g_ref[...]).astype(BF16)

    return pl.pallas_call(
        body, name=name, grid=(S // ts,),
        in_specs=[pl.BlockSpec((ts, w), lambda i: (i, c0 // w)), pl.BlockSpec((1, w), lambda i: (0, 0))],
        out_specs=pl.BlockSpec((ts, w), lambda i: (i, 0)),
        out_shape=jax.ShapeDtypeStruct((S, w), BF16), compiler_params=_cp(("parallel",)),
    )(x, g.reshape(1, w))


def _rms_bwd(dy, x, g, *, c0, ts, name):
    S = x.shape[0]
    w = g.shape[0]

    def body(dy_ref, x_ref, g_ref, dx_ref, dg_ref):
        @pl.when(pl.program_id(0) == 0)
        def _():
            dg_ref[...] = jnp.zeros_like(dg_ref)
        xv = x_ref[...]
        dyv = dy_ref[...]
        rstd = lax.rsqrt(jnp.mean(xv * xv, axis=-1, keepdims=True) + RMS_EPS)
        dyg = dyv * g_ref[...]
        m = jnp.mean(dyg * xv, axis=-1, keepdims=True)
        dx_ref[...] = (rstd * (dyg - xv * (rstd * rstd) * m)).astype(BF16)
        dg_ref[...] += _fold8(dyv * xv * rstd)

    return pl.pallas_call(
        body, name=name, grid=(S // ts,),
        in_specs=[pl.BlockSpec((ts, w), lambda i: (i, 0)), pl.BlockSpec((ts, w), lambda i: (i, c0 // w)),
                  pl.BlockSpec((1, w), lambda i: (0, 0))],
        out_specs=[pl.BlockSpec((ts, w), lambda i: (i, 0)), pl.BlockSpec((SUBLANES, w), lambda i: (0, 0))],
        out_shape=[jax.ShapeDtypeStruct((S, w), BF16), jax.ShapeDtypeStruct((SUBLANES, w), F32)],
        compiler_params=_cp(("arbitrary",)),
    )(dy, x, g.reshape(1, w))


def _mla_tables(pos):
    S = pos.shape[0]
    half = QK_ROPE // 2
    inv_freq = ROPE_BASE ** (-jnp.arange(0, QK_ROPE, 2, dtype=F32) / QK_ROPE)
    ang = pos.astype(F32)[:, None] * inv_freq
    cos, sin = jnp.cos(ang), jnp.sin(ang)
    z = lambda n: jnp.zeros((S, n), F32)
    pad = HEAD_SLOT - QK_NOPE - QK_ROPE
    c = jnp.concatenate([jnp.ones((S, QK_NOPE), F32), cos, cos, z(pad)], axis=1)
    s1 = jnp.concatenate([z(QK_NOPE), -sin, z(half), z(pad)], axis=1)
    s2 = jnp.concatenate([z(QK_NOPE), z(half), sin, z(pad)], axis=1)
    return c, s1, s2


def _mla_prep_fwd(qraw, kvraw, proj, tabs, *, kpe_c0, ts, name):
    S = qraw.shape[0]
    H = MLA_HEADS
    half = QK_ROPE // 2

    def body(q_ref, kn_ref, kpe_ref, c_ref, s1_ref, s2_ref, qo_ref, ko_ref):
        c, s1, s2 = c_ref[...], s1_ref[...], s2_ref[...]

        def rope(v):
            return v * c + pltpu.roll(v, HEAD_SLOT - half, 1) * s1 + pltpu.roll(v, half, 1) * s2

        kpe_r = rope(kpe_ref[...])
        for h in range(H):
            sl = slice(h * HEAD_SLOT, (h + 1) * HEAD_SLOT)
            qo_ref[:, sl] = (rope(q_ref[:, sl]) * MLA_SCALE).astype(BF16)
            ko_ref[:, sl] = (kn_ref[:, sl] + kpe_r).astype(BF16)

    wide = pl.BlockSpec((ts, H * HEAD_SLOT), lambda i: (i, 0))
    tab = pl.BlockSpec((ts, HEAD_SLOT), lambda i: (i, 0))
    return pl.pallas_call(
        body, name=name, grid=(S // ts,),
        in_specs=[wide, wide, pl.BlockSpec((ts, HEAD_SLOT), lambda i: (i, kpe_c0 // HEAD_SLOT)), tab, tab, tab],
        out_specs=[wide, wide],
        out_shape=[jax.ShapeDtypeStruct((S, H * HEAD_SLOT), BF16)] * 2,
        compiler_params=_cp(("parallel",)),
    )(qraw, kvraw, proj, *tabs)


def _mla_prep_bwd(dq, dk, dv, tabs, *, ts, name):
    S = dk.shape[0]
    H = MLA_HEADS
    half = QK_ROPE // 2
    kw = H * HEAD_SLOT
    vw = H * V_HEAD

    def body(dq_ref, dk_ref, dv_ref, c_ref, s1_ref, s2_ref, dqr_ref, dkv_ref, dkpe_ref):
        c, s1, s2 = c_ref[...], s1_ref[...], s2_ref[...]

        def rope_t(g):
            return g * c + pltpu.roll(g * s1, half, 1) + pltpu.roll(g * s2, HEAD_SLOT - half, 1)

        gsum = jnp.zeros((ts, HEAD_SLOT), F32)
        for h in range(H):
            sl = slice(h * HEAD_SLOT, (h + 1) * HEAD_SLOT)
            hs = slice((h % 2) * HEAD_SLOT, (h % 2 + 1) * HEAD_SLOT)
            dqr_ref[:, sl] = (rope_t(dq_ref[h // 2, :, hs]) * MLA_SCALE).astype(BF16)
            dkh = dk_ref[:, sl]
            dkv_ref[:, sl] = dkh.astype(BF16)
            gsum = gsum + dkh
        dkv_ref[:, kw:] = dv_ref[...].astype(BF16)
        lane = lax.broadcasted_iota(jnp.int32, (ts, HEAD_SLOT), 1)
        pe = jnp.logical_and(lane >= QK_NOPE, lane < QK_NOPE + QK_ROPE)
        dkpe_ref[...] = rope_t(jnp.where(pe, gsum, 0.0)).astype(BF16)

    tab = pl.BlockSpec((ts, HEAD_SLOT), lambda i: (i, 0))
    return pl.pallas_call(
        body, name=name, grid=(S // ts,),
        in_specs=[pl.BlockSpec((H // 2, ts, 2 * HEAD_SLOT), lambda i: (0, i, 0)), pl.BlockSpec((ts, kw), lambda i: (i, 0)),
                  pl.BlockSpec((ts, vw), lambda i: (i, 0)), tab, tab, tab],
        out_specs=[pl.BlockSpec((ts, kw), lambda i: (i, 0)), pl.BlockSpec((ts, kw + vw), lambda i: (i, 0)), tab],
        out_shape=[jax.ShapeDtypeStruct((S, kw), BF16), jax.ShapeDtypeStruct((S, kw + vw), BF16),
                   jax.ShapeDtypeStruct((S, HEAD_SLOT), BF16)],
        compiler_params=_cp(("parallel",)),
    )(dq, dk, dv, *tabs)


def _attn_pairs(nb, kv_outer):
    if kv_outer:
        pr = [(i, j) for j in range(nb) for i in range(j, nb)]
    else:
        pr = [(i, j) for i in range(nb) for j in range(i + 1)]
    return (jnp.asarray(np.array([p[0] for p in pr], np.int32)), jnp.asarray(np.array([p[1] for p in pr], np.int32)))


def _attn_scores(q_ref, k_ref, hh, diag, T):
    sl = slice(hh * HEAD_SLOT, (hh + 1) * HEAD_SLOT)
    s = _dot_tb(q_ref[:, sl], k_ref[:, sl])
    if not diag:
        return s
    row = lax.broadcasted_iota(jnp.int32, (T, T), 0) // CHUNK
    col = lax.broadcasted_iota(jnp.int32, (T, T), 1) // CHUNK
    return jnp.where(col <= row, s, NEG_BIG)


def _on_block_kind(i, j, step):
    @pl.when(i == j)
    def _():
        step(True)

    @pl.when(i != j)
    def _():
        step(False)


def _attn_fwd(q, k, kvraw, *, T, name):
    S = q.shape[0]
    NP = MLA_HEADS // 2
    nb = S // T
    ii, jj = _attn_pairs(nb, kv_outer=False)
    v_c0 = MLA_HEADS * HEAD_SLOT // LANES

    def body(ii_ref, jj_ref, q_ref, k_ref, v_ref, o_ref, lse_ref, m_sc, l_sc, acc_sc):
        t = pl.program_id(1)
        i, j = ii_ref[t], jj_ref[t]

        @pl.when(j == 0)
        def _():
            m_sc[...] = jnp.full_like(m_sc, NEG_BIG)
            l_sc[...] = jnp.zeros_like(l_sc)
            acc_sc[...] = jnp.zeros_like(acc_sc)

        lo = lax.broadcasted_iota(jnp.int32, (T, LANES), 1) < V_HEAD
        top = lax.broadcasted_iota(jnp.int32, (LANES, T), 0) < V_HEAD

        def step(diag):
            v = v_ref[...].astype(BF16)
            vh = (jnp.where(lo, v, jnp.zeros_like(v)), jnp.where(lo, jnp.zeros_like(v), v))
            alphas, pv = [], None
            for hh in range(2):
                sl = slice(hh * HEAD_SLOT, (hh + 1) * HEAD_SLOT)
                s = _dot_tb(k_ref[:, sl], q_ref[:, sl])
                if diag:
                    krow = lax.broadcasted_iota(jnp.int32, (T, T), 0) // CHUNK
                    qcol = lax.broadcasted_iota(jnp.int32, (T, T), 1) // CHUNK
                    s = jnp.where(krow <= qcol, s, NEG_BIG)
                m_prev = m_sc[hh]
                m_new = jnp.maximum(m_prev, jnp.max(s, axis=0, keepdims=True))
                p = jnp.exp(s - m_new[0:1, :])
                alpha = jnp.exp(m_prev - m_new)
                l_sc[hh] = alpha * l_sc[hh] + jnp.sum(p, axis=0, keepdims=True)
                m_sc[hh] = m_new
                alphas.append(alpha[0:1, :])
                t_pv = _dot_ta(vh[hh], p.astype(BF16))
                pv = t_pv if pv is None else pv + t_pv
            acc_sc[...] = acc_sc[...] * jnp.where(top, alphas[0], alphas[1]) + pv

        _on_block_kind(i, j, step)

        @pl.when(j == i)
        def _():
            l0, l1 = l_sc[0][0:1, :], l_sc[1][0:1, :]
            o_ref[...] = jnp.transpose(acc_sc[...] * jnp.where(top, 1.0 / l0, 1.0 / l1))
            lse_ref[...] = jnp.transpose(jnp.where(top, m_sc[0][0:1, :] + jnp.log(l0), m_sc[1][0:1, :] + jnp.log(l1)))

    grid_spec = pltpu.PrefetchScalarGridSpec(
        num_scalar_prefetch=2, grid=(NP, int(ii.shape[0])),
        in_specs=[pl.BlockSpec((T, 2 * HEAD_SLOT), lambda p, t, ii, jj: (ii[t], p)),
                  pl.BlockSpec((T, 2 * HEAD_SLOT), lambda p, t, ii, jj: (jj[t], p)),
                  pl.BlockSpec((T, LANES), lambda p, t, ii, jj: (jj[t], v_c0 + p))],
        out_specs=[pl.BlockSpec((T, LANES), lambda p, t, ii, jj: (ii[t], p)),
                   pl.BlockSpec((None, T, LANES), lambda p, t, ii, jj: (p, ii[t], 0))],
        scratch_shapes=[pltpu.VMEM((2, SUBLANES, T), F32), pltpu.VMEM((2, SUBLANES, T), F32), pltpu.VMEM((LANES, T), F32)])
    return pl.pallas_call(
        body, name=name, grid_spec=grid_spec,
        out_shape=[jax.ShapeDtypeStruct((S, MLA_HEADS * V_HEAD), F32), jax.ShapeDtypeStruct((NP, S, LANES), F32)],
        compiler_params=_cp(("parallel", "arbitrary")),
    )(ii, jj, q, k, kvraw)


def _attn_bwd_common(q_ref, k_ref, v, do, o, lse, hh, diag, T, lo):
    sel = lo if hh == 0 else jnp.logical_not(lo)
    s = _attn_scores(q_ref, k_ref, hh, diag, T)
    p = jnp.exp(s - lse[:, hh * V_HEAD:hh * V_HEAD + 1])
    do_h = jnp.where(sel, do, 0.0)
    dsum = jnp.sum(do_h * o, axis=1, keepdims=True)
    do_hb = do_h.astype(BF16)
    dp = _dot_tb(do_hb, v)
    return p, p * (dp - dsum), do_hb


def _attn_bwd(q, k, kvraw, do, o, lse, *, T, name):
    S = q.shape[0]
    NP = MLA_HEADS // 2
    nb = S // T
    ii, jj = _attn_pairs(nb, kv_outer=True)
    n_steps = int(ii.shape[0])
    v_c0 = MLA_HEADS * HEAD_SLOT // LANES

    def body(ii_ref, jj_ref, q_ref, k_ref, v_ref, do_ref, o_ref, lse_ref, dq_hbm, dk_ref, dv_ref,
             dq_sc, dkt_sc, dvt_sc, sem):
        pair = pl.program_id(0)
        t = pl.program_id(1)
        i, j = ii_ref[t], jj_ref[t]

        @pl.when(t == 0)
        def _():
            dq_sc[...] = jnp.zeros_like(dq_sc)

        @pl.when(i == j)
        def _():
            dkt_sc[...] = jnp.zeros_like(dkt_sc)
            dvt_sc[...] = jnp.zeros_like(dvt_sc)

        lo = lax.broadcasted_iota(jnp.int32, (T, LANES), 1) < V_HEAD
        rows = pl.ds(pl.multiple_of(i * T, T), T)

        def step(diag):
            v = v_ref[...].astype(BF16)
            do, o_v, lse_v = do_ref[...], o_ref[...], lse_ref[...]
            for hh in range(2):
                sl = slice(hh * HEAD_SLOT, (hh + 1) * HEAD_SLOT)
                p, ds, do_hb = _attn_bwd_common(q_ref, k_ref, v, do, o_v, lse_v, hh, diag, T, lo)
                dsb = ds.astype(BF16)
                do_t = jnp.transpose(do_hb.astype(F32)).astype(BF16)
                q_t = jnp.transpose(q_ref[:, sl].astype(F32)).astype(BF16)
                dvt_sc[...] += _dot(do_t, p.astype(BF16))
                dkt_sc[hh] += _dot(q_t, dsb)
                dq_sc[rows, sl] += _dot(dsb, k_ref[:, sl])

        _on_block_kind(i, j, step)

        @pl.when(i == nb - 1)
        def _():
            for hh in range(2):
                dk_ref[:, hh * HEAD_SLOT:(hh + 1) * HEAD_SLOT] = jnp.transpose(dkt_sc[hh])
            dv_ref[...] = jnp.transpose(dvt_sc[...])

        @pl.when(t == n_steps - 1)
        def _():
            cp = pltpu.make_async_copy(dq_sc, dq_hbm.at[pair], sem)
            cp.start()
            cp.wait()

    qi = lambda p, t, ii, jj: (ii[t], p)
    kj = lambda p, t, ii, jj: (jj[t], p)
    grid_spec = pltpu.PrefetchScalarGridSpec(
        num_scalar_prefetch=2, grid=(NP, n_steps),
        in_specs=[pl.BlockSpec((T, 2 * HEAD_SLOT), qi), pl.BlockSpec((T, 2 * HEAD_SLOT), kj),
                  pl.BlockSpec((T, LANES), lambda p, t, ii, jj: (jj[t], v_c0 + p)),
                  pl.BlockSpec((T, LANES), qi), pl.BlockSpec((T, LANES), qi),
                  pl.BlockSpec((None, T, LANES), lambda p, t, ii, jj: (p, ii[t], 0))],
        out_specs=[pl.BlockSpec(memory_space=pl.ANY), pl.BlockSpec((T, 2 * HEAD_SLOT), kj),
                   pl.BlockSpec((T, LANES), kj)],
        scratch_shapes=[pltpu.VMEM((S, 2 * HEAD_SLOT), F32), pltpu.VMEM((2, HEAD_SLOT, T), F32),
                        pltpu.VMEM((LANES, T), F32), pltpu.SemaphoreType.DMA])
    return pl.pallas_call(
        body, name=name, grid_spec=grid_spec,
        out_shape=[jax.ShapeDtypeStruct((NP, S, 2 * HEAD_SLOT), F32),
                   jax.ShapeDtypeStruct((S, MLA_HEADS * HEAD_SLOT), F32),
                   jax.ShapeDtypeStruct((S, MLA_HEADS * V_HEAD), F32)],
        compiler_params=_cp(("arbitrary", "arbitrary")),
    )(ii, jj, q, k, kvraw, do, o, lse)


def _mla_permute_weights(w_in, w_uq, w_ukv):
    D = w_in.shape[0]
    H = MLA_HEADS
    qk = QK_NOPE + QK_ROPE
    lat = Q_LORA + KV_LORA
    kpe = jnp.zeros((D, HEAD_SLOT), w_in.dtype).at[:, QK_NOPE:qk].set(w_in[:, lat:])
    w_in_p = jnp.concatenate([w_in[:, :lat], kpe], axis=1)
    w_uq_p = jnp.pad(w_uq.reshape(Q_LORA, H, qk), ((0, 0), (0, 0), (0, HEAD_SLOT - qk))).reshape(Q_LORA, H * HEAD_SLOT)
    kv = w_ukv.reshape(KV_LORA, H, QK_NOPE + V_HEAD)
    wk = jnp.pad(kv[:, :, :QK_NOPE], ((0, 0), (0, 0), (0, HEAD_SLOT - QK_NOPE))).reshape(KV_LORA, H * HEAD_SLOT)
    wv = kv[:, :, QK_NOPE:].reshape(KV_LORA, H * V_HEAD)
    return w_in_p, w_uq_p, jnp.concatenate([wk, wv], axis=1)


def _mla_unpermute_grads(g_in_p, g_uq_p, g_ukv_p):
    H = MLA_HEADS
    qk = QK_NOPE + QK_ROPE
    lat = Q_LORA + KV_LORA
    g_in = jnp.concatenate([g_in_p[:, :lat], g_in_p[:, lat + QK_NOPE:lat + qk]], axis=1)
    g_uq = g_uq_p.reshape(Q_LORA, H, HEAD_SLOT)[:, :, :qk].reshape(Q_LORA, H * qk)
    gk = g_ukv_p[:, :H * HEAD_SLOT].reshape(KV_LORA, H, HEAD_SLOT)[:, :, :QK_NOPE]
    gv = g_ukv_p[:, H * HEAD_SLOT:].reshape(KV_LORA, H, V_HEAD)
    g_ukv = jnp.concatenate([gk, gv], axis=2).reshape(KV_LORA, H * (QK_NOPE + V_HEAD))
    return g_in, g_uq, g_ukv


def _mla_mixer_fwd(x, pos, p, tag):
    S, D = x.shape
    ts = _rt(S, 512)
    T = _rt(S, ATTN_BLOCK)
    lat = Q_LORA + KV_LORA
    tabs = _mla_tables(pos)
    proj = _mm(x, p["w_in_p"], out_dtype=F32, tm=ts, tn=p["w_in_p"].shape[1], name=tag + "_proj")
    qn = _rms_fwd(proj, p["q_norm"], c0=0, ts=ts, name=tag + "_qn")
    kvn = _rms_fwd(proj, p["kv_norm"], c0=Q_LORA, ts=ts, name=tag + "_kvn")
    qraw = _mm(qn, p["w_uq_p"], out_dtype=F32, tm=ts, tn=1024, name=tag + "_uq")
    kvraw = _mm(kvn, p["w_ukv_p"], out_dtype=F32, tm=ts, tn=1024, name=tag + "_ukv")
    q, k = _mla_prep_fwd(qraw, kvraw, proj, tabs, kpe_c0=lat, ts=ts, name=tag + "_prep")
    o, lse = _attn_fwd(q, k, kvraw, T=T, name=tag + "_attn")
    mix = _mm(o, p["w_out"], out_dtype=F32, tm=ts, tn=D, name=tag + "_out")
    return mix, (proj, qn, kvn, kvraw, q, k, o, lse, tabs)


def _mla_mixer_bwd(dmix, x, p, saved, tag, gbuf, j):
    proj, qn, kvn, kvraw, q, k, o, lse, tabs = saved
    S, D = x.shape
    ts = _rt(S, 512)
    T = _rt(S, ATTN_BLOCK)
    lat = Q_LORA + KV_LORA
    g = {}
    do = _mm_tb([(dmix, 0)], p["w_out"], out_dtype=F32, tm=ts, tk=p["w_out"].shape[0], name=tag + "_do")
    g["w_out"] = _mm_ta(o, dmix, tk=p["w_out"].shape[0], tn=D, tm=ts, name=tag + "_dwout")
    dq, dk, dv = _attn_bwd(q, k, kvraw, do, o, lse, T=T, name=tag + "_attn_bwd")
    dqraw, dkvraw, dkpe = _mla_prep_bwd(dq, dk, dv, tabs, ts=_rt(S, 256), name=tag + "_prepb")
    dqn = _mm_tb([(dqraw, 0)], p["w_uq_p"], out_dtype=F32, tm=ts, tk=Q_LORA, name=tag + "_dqn")
    g_uq_p = _mm_ta(qn, dqraw, tk=Q_LORA, tn=1024, tm=ts, name=tag + "_dwuq")
    dkvn = _mm_tb([(dkvraw, 0)], p["w_ukv_p"], out_dtype=F32, tm=ts, tk=KV_LORA, name=tag + "_dkvn")
    g_ukv_p = _mm_ta(kvn, dkvraw, tk=KV_LORA, tn=1024, tm=ts, name=tag + "_dwukv")
    dcq, dqg8 = _rms_bwd(dqn, proj, p["q_norm"], c0=0, ts=ts, name=tag + "_qnb")
    dckv, dkvg8 = _rms_bwd(dkvn, proj, p["kv_norm"], c0=Q_LORA, ts=ts, name=tag + "_kvnb")
    g["q_norm"] = dqg8.sum(axis=0)
    g["kv_norm"] = dkvg8.sum(axis=0)
    dx = _mm_tb([(dcq, 0), (dckv, Q_LORA), (dkpe, lat)], p["w_in_p"], out_dtype=F32, tm=ts, tk=D, name=tag + "_dx")
    g_in_p = jnp.concatenate(
        [_mm_ta(x, dcq, tk=D, tn=Q_LORA, tm=ts, name=tag + "_dwin_q"),
         _mm_ta(x, dckv, tk=D, tn=KV_LORA, tm=ts, name=tag + "_dwin_kv"),
         _mm_ta(x, dkpe, tk=D, tn=HEAD_SLOT, tm=ts, name=tag + "_dwin_pe")], axis=1)
    g["w_in"], g["w_uq"], g["w_ukv"] = _mla_unpermute_grads(g_in_p, g_uq_p, g_ukv_p)
    return dx, g


RET_QK = 256
RET_V = 512


def _ret_tables(pos, T):
    half = RET_QK // 2
    inv_freq = ROPE_BASE ** (-jnp.arange(0, RET_QK, 2, dtype=F32) / RET_QK)
    ang = pos.astype(F32)[:, None] * inv_freq
    lg = jnp.log1p(-jnp.exp2(-5.0 - jnp.arange(RET_HEADS, dtype=F32)))
    idx = jnp.arange(T, dtype=F32)
    ch = jnp.arange(T) // CHUNK
    dm = jnp.where(ch[None, :] <= ch[:, None], jnp.exp(lg[:, None, None] * jnp.abs(idx[:, None] - idx[None, :])), 0.0)
    xi = jnp.broadcast_to(jnp.exp(lg[:, None] * (idx + 1.0))[:, :, None], (RET_HEADS, T, RET_QK))
    zeta = jnp.broadcast_to(jnp.exp(lg[:, None] * (T - 1.0 - idx))[:, :, None], (RET_HEADS, T, RET_QK))
    g_t = jnp.broadcast_to(jnp.exp(lg * T)[:, None, None], (RET_HEADS, 1, RET_V))
    assert half == LANES
    return jnp.cos(ang), jnp.sin(ang), dm.astype(F32), xi.astype(F32), zeta.astype(F32), g_t.astype(F32)


def _rope_half(x, c, s):
    x1, x2 = x[:, :LANES], x[:, LANES:]
    return jnp.concatenate([x1 * c - x2 * s, x1 * s + x2 * c], axis=1)


def _rope_half_t(g, c, s):
    g1, g2 = g[:, :LANES], g[:, LANES:]
    return jnp.concatenate([g1 * c + g2 * s, g2 * c - g1 * s], axis=1)


def _ret_qkv(q_ref, k_ref, v_ref, c_ref, s_ref):
    c, s = c_ref[...], s_ref[...]
    q = _rope_half(q_ref[...], c, s)
    k = _rope_half(k_ref[...], c, s) * (RET_QK ** -0.5)
    return q, k, v_ref[...].astype(BF16)


def _ret_in_specs(T, H, rev_nb=None):
    rb = (lambda n: n) if rev_nb is None else (lambda n: rev_nb - 1 - n)
    nq = H * RET_QK // RET_QK
    nv = 2 * H * RET_QK // RET_V
    return dict(
        q=pl.BlockSpec((T, RET_QK), lambda h, n: (rb(n), h)),
        k=pl.BlockSpec((T, RET_QK), lambda h, n: (rb(n), nq + h)),
        v=pl.BlockSpec((T, RET_V), lambda h, n: (rb(n), nv + h)),
        g=pl.BlockSpec((T, RET_V), lambda h, n: (rb(n), nv + H + h)),
        yv=pl.BlockSpec((T, RET_V), lambda h, n: (rb(n), h)),
        cs=pl.BlockSpec((T, LANES), lambda h, n: (rb(n), 0)),
        dm=pl.BlockSpec((None, T, T), lambda h, n: (h, 0, 0)),
        xz=pl.BlockSpec((None, T, RET_QK), lambda h, n: (h, 0, 0)),
        gt=pl.BlockSpec((None, 1, RET_V), lambda h, n: (h, 0, 0)),
        gn=pl.BlockSpec((1, RET_V), lambda h, n: (0, h)),
        st=pl.BlockSpec((None, None, RET_QK, RET_V), lambda h, n: (h, rb(n), 0, 0)),
    )


def _ret_fwd(proj, gn_g, tabs, *, T, name):
    S = proj.shape[0]
    H = RET_HEADS
    nb = S // T
    cos, sin, dm, xi, zeta, g_t = tabs
    sp = _ret_in_specs(T, H)

    def body(q_ref, k_ref, v_ref, g_ref, gn_ref, c_ref, s_ref, dm_ref, xi_ref, zeta_ref, gt_ref,
             o_ref, y_ref, st_ref, st):
        @pl.when(pl.program_id(1) == 0)
        def _():
            st[...] = jnp.zeros_like(st)

        q, k, vb = _ret_qkv(q_ref, k_ref, v_ref, c_ref, s_ref)
        qb, kb = q.astype(BF16), k.astype(BF16)
        s0 = st[...]
        s0b = s0.astype(BF16)
        st_ref[...] = s0b
        a = _dot_tb(qb, kb) * dm_ref[...]
        y = _dot(a.astype(BF16), vb) + _dot((q * xi_ref[...]).astype(BF16), s0b)
        st[...] = s0 * gt_ref[...] + _dot_ta((k * zeta_ref[...]).astype(BF16), vb)
        y_ref[...] = y
        mu = jnp.mean(y, axis=-1, keepdims=True)
        yc = y - mu
        var = jnp.mean(yc * yc, axis=-1, keepdims=True)
        gv = g_ref[...]
        o_ref[...] = (gv * _sigmoid(gv) * (yc * lax.rsqrt(var + LN_EPS) * gn_ref[...])).astype(BF16)

    return pl.pallas_call(
        body, name=name, grid=(H, nb),
        in_specs=[sp["q"], sp["k"], sp["v"], sp["g"], sp["gn"], sp["cs"], sp["cs"], sp["dm"], sp["xz"], sp["xz"], sp["gt"]],
        out_specs=[sp["yv"], sp["yv"], sp["st"]],
        out_shape=[jax.ShapeDtypeStruct((S, H * RET_V), BF16), jax.ShapeDtypeStruct((S, H * RET_V), F32),
                   jax.ShapeDtypeStruct((H, nb, RET_QK, RET_V), BF16)],
        scratch_shapes=[pltpu.VMEM((RET_QK, RET_V), F32)],
        compiler_params=_cp(("parallel", "arbitrary")),
    )(proj, proj, proj, proj, gn_g.reshape(1, H * RET_V), cos, sin, dm, xi, zeta, g_t)


def _ret_gn_bwd(dout, proj, y, gn_g, *, ts, name):
    S = proj.shape[0]
    H = RET_HEADS
    goff = 2 * H * RET_QK // RET_V + H

    def body(do_ref, g_ref, y_ref, gn_ref, dy_ref, dg_ref, dgn_ref):
        @pl.when(pl.program_id(1) == 0)
        def _():
            dgn_ref[...] = jnp.zeros_like(dgn_ref)
        y_v = y_ref[...]
        mu = jnp.mean(y_v, axis=-1, keepdims=True)
        yc = y_v - mu
        var = jnp.mean(yc * yc, axis=-1, keepdims=True)
        rstd = lax.rsqrt(var + LN_EPS)
        yhat = yc * rstd
        gv = g_ref[...]
        sg = _sigmoid(gv)
        dout = do_ref[...]
        gn = gn_ref[...]
        dg_ref[...] = (dout * (yhat * gn) * (sg * (1.0 + gv * (1.0 - sg)))).astype(BF16)
        dyn = dout * (gv * sg)
        dgn_ref[...] += _fold8(dyn * yhat)
        dyh = dyn * gn
        m1 = jnp.mean(dyh, axis=-1, keepdims=True)
        m2 = jnp.mean(dyh * yhat, axis=-1, keepdims=True)
        dy_ref[...] = (rstd * (dyh - m1 - yhat * m2)).astype(BF16)

    blk = pl.BlockSpec((ts, RET_V), lambda h, i: (i, h))
    return pl.pallas_call(
        body, name=name, grid=(H, S // ts),
        in_specs=[blk, pl.BlockSpec((ts, RET_V), lambda h, i: (i, goff + h)), blk,
                  pl.BlockSpec((1, RET_V), lambda h, i: (0, h))],
        out_specs=[blk, blk, pl.BlockSpec((SUBLANES, RET_V), lambda h, i: (0, h))],
        out_shape=[jax.ShapeDtypeStruct((S, H * RET_V), BF16), jax.ShapeDtypeStruct((S, H * RET_V), BF16),
                   jax.ShapeDtypeStruct((SUBLANES, H * RET_V), F32)],
        compiler_params=_cp(("parallel", "arbitrary")),
    )(dout, proj, y, gn_g.reshape(1, H * RET_V))


def _ret_bwd(proj, dy, states, tabs, *, T, name):
    S = proj.shape[0]
    H = RET_HEADS
    nb = S // T
    cos, sin, dm, xi, zeta, g_t = tabs
    sp = _ret_in_specs(T, H, rev_nb=nb)

    def body(q_ref, k_ref, v_ref, dy_ref, st_ref, c_ref, s_ref, dm_ref, xi_ref, zeta_ref, gt_ref,
             dq_ref, dk_ref, dv_ref, ds):
        @pl.when(pl.program_id(1) == 0)
        def _():
            ds[...] = jnp.zeros_like(ds)

        q, k, vb = _ret_qkv(q_ref, k_ref, v_ref, c_ref, s_ref)
        qb, kb = q.astype(BF16), k.astype(BF16)
        dyb = dy_ref[...]
        s0b = st_ref[...]
        dmv, xiv, zv = dm_ref[...], xi_ref[...], zeta_ref[...]
        ds_v = ds[...]
        dsb = ds_v.astype(BF16)
        gm = (_dot_tb(dyb, vb) * dmv).astype(BF16)
        ab = (_dot_tb(qb, kb) * dmv).astype(BF16)
        kz = (k * zv).astype(BF16)
        qx = (q * xiv).astype(BF16)
        dq = _dot(gm, kb) + xiv * _dot_tb(dyb, s0b)
        dk = _dot_ta(gm, qb) + zv * _dot_tb(vb, dsb)
        dv_ref[...] = (_dot_ta(ab, dyb) + _dot(kz, dsb)).astype(BF16)
        ds[...] = ds_v * gt_ref[...] + _dot_ta(qx, dyb)
        c, s = c_ref[...], s_ref[...]
        dq_ref[...] = _rope_half_t(dq, c, s).astype(BF16)
        dk_ref[...] = _rope_half_t(dk * (RET_QK ** -0.5), c, s).astype(BF16)

    qblk = pl.BlockSpec((T, RET_QK), lambda h, n: (nb - 1 - n, h))
    return pl.pallas_call(
        body, name=name, grid=(H, nb),
        in_specs=[sp["q"], sp["k"], sp["v"], sp["yv"], sp["st"], sp["cs"], sp["cs"], sp["dm"], sp["xz"], sp["xz"], sp["gt"]],
        out_specs=[qblk, qblk, sp["yv"]],
        out_shape=[jax.ShapeDtypeStruct((S, H * RET_QK), BF16), jax.ShapeDtypeStruct((S, H * RET_QK), BF16),
                   jax.ShapeDtypeStruct((S, H * RET_V), BF16)],
        scratch_shapes=[pltpu.VMEM((RET_QK, RET_V), F32)],
        compiler_params=_cp(("parallel", "arbitrary")),
    )(proj, proj, proj, dy, states, cos, sin, dm, xi, zeta, g_t)


def _ret_mixer_fwd(x, pos, p, tag):
    S, D = x.shape
    ts = _rt(S, 512)
    T = _rt(S, 256)
    tabs = _ret_tables(pos, T)
    proj = _mm(x, p["w_in"], out_dtype=F32, tm=_rt(S, 256), tn=p["w_in"].shape[1], name=tag + "_proj")
    gated, y, states = _ret_fwd(proj, p["gn_g"], tabs, T=T, name=tag + "_ret")
    mix = _mm(gated, p["w_out"], out_dtype=F32, tm=ts, tn=D, name=tag + "_out")
    return mix, (proj, gated, y, states, tabs)


def _ret_mixer_bwd(dmix, x, p, saved, tag, gbuf, j):
    proj, gated, y, states, tabs = saved
    S, D = x.shape
    ts = _rt(S, 512)
    T = _rt(S, 256)
    H = RET_HEADS
    hq, hv = H * RET_QK, H * RET_V
    g = {}
    dout = _mm_tb([(dmix, 0)], p["w_out"], out_dtype=F32, tm=ts, tk=1024, name=tag + "_dgated")
    n_ret = DEPTH // 3
    _grad_into(gbuf, "ret_w_out", (n_ret, hv, D), j, 0, 0, gated, dmix, tk=1024, tn=D, tm=ts, name=tag + "_dwout")
    dy, dgate, dgn8 = _ret_gn_bwd(dout, proj, y, p["gn_g"], ts=_rt(S, 256), name=tag + "_gnb")
    g["gn_g"] = dgn8.sum(axis=0)
    dq, dk, dv = _ret_bwd(proj, dy, states, tabs, T=T, name=tag + "_retb")
    dx = _mm_tb([(dq, 0), (dk, hq), (dv, 2 * hq), (dgate, 2 * hq + hv)], p["w_in"], out_dtype=F32,
                tm=ts, tk=512, name=tag + "_dx")
    w_in_shape = (n_ret, D, 2 * hq + 2 * hv)
    for part, c0, nm in ((dq, 0, "q"), (dk, hq, "k"), (dv, 2 * hq, "v"), (dgate, 2 * hq + hv, "g")):
        _grad_into(gbuf, "ret_w_in", w_in_shape, j, 0, c0, x, part, tk=D, tn=1024, tm=ts, name=tag + "_dwin_" + nm)
    return dx, g


PACK_W = 1024
ANY = pl.BlockSpec(memory_space=pl.ANY)
MESH = pl.DeviceIdType.MESH


def _coords():
    return lax.axis_index("x"), lax.axis_index("y"), lax.axis_index("c")


def _chip_peers(x, y):
    return [(1 - x, y), (x, 1 - y), (1 - x, 1 - y)]


def _slot(ref, axis, s, n):
    if axis is None:
        return ref.at[s]
    size = n // N_CHIPS
    sl = pl.ds(pl.multiple_of(s * size, LANES if axis == 2 else 2 * SUBLANES), size)
    return ref.at[:, sl, :] if axis == 1 else ref.at[:, :, sl]


def _row_half(ref, h):
    if len(ref.shape) == 2:
        n = ref.shape[0] // 2
        return ref.at[pl.ds(pl.multiple_of(h * n, 2 * SUBLANES), n), :]
    n = ref.shape[1] // 2
    return ref.at[:, pl.ds(pl.multiple_of(h * n, 2 * SUBLANES), n), :]


def _gather_chips(items, name):
    n = len(items)
    axes = [ax for _, ax in items]
    out_shapes = []
    for arr, ax in items:
        shp = (N_CHIPS,) + arr.shape if ax is None else tuple(d * (N_CHIPS if i == ax else 1) for i, d in enumerate(arr.shape))
        out_shapes.append(jax.ShapeDtypeStruct(shp, arr.dtype))

    def body(*refs):
        srcs, outs = refs[:n], refs[n:2 * n]
        send_sems, recv_sems, local_sems = refs[2 * n:]
        x, y, c = _coords()
        me = 2 * x + y
        sibling = (x, y, 1 - c)
        dst = lambda t, s: _slot(outs[t], axes[t], s, out_shapes[t].shape[axes[t]] if axes[t] is not None else 0)

        def copy(sem, src, dst_ref, to):
            return pltpu.make_async_remote_copy(src_ref=src, dst_ref=dst_ref, send_sem=send_sems.at[sem],
                                                recv_sem=recv_sems.at[sem], device_id=to, device_id_type=MESH)

        local = [pltpu.make_async_copy(srcs[t], dst(t, me), local_sems.at[t]) for t in range(n)]
        for cp in local:
            cp.start()
        peers = _chip_peers(x, y)
        first, arrive, passed, from_sib = [], [], [], []
        for k, (px, py) in enumerate(peers):
            for t in range(n):
                land = _row_half(dst(t, 2 * px + py), c)
                first.append(copy(k * n + t, _row_half(srcs[t], c), _row_half(dst(t, me), c), (px, py, c)))
                arrive.append(copy(k * n + t, _row_half(srcs[t], c), land, (px, py, c)))
                passed.append(copy((3 + k) * n + t, land, land, sibling))
                from_sib.append(copy((3 + k) * n + t, land, _row_half(dst(t, 2 * px + py), 1 - c), sibling))
        for cp in first:
            cp.start()
        for cp_in, cp_on in zip(arrive, passed):
            cp_in.wait_recv()
            cp_on.start()
        for cp in from_sib:
            cp.wait_recv()
        for cp in first + passed:
            cp.wait_send()
        for cp in local:
            cp.wait()

    return pl.pallas_call(
        body, name=name, in_specs=[ANY] * n, out_specs=[ANY] * n, out_shape=out_shapes,
        scratch_shapes=[pltpu.SemaphoreType.DMA((6 * n,)), pltpu.SemaphoreType.DMA((6 * n,)),
                        pltpu.SemaphoreType.DMA((n,))],
    )(*[arr for arr, _ in items])


def _scatter_chips(items, name):
    n = len(items)
    axes = [ax for _, ax in items]
    out_shapes = []
    for arr, ax in items:
        part = arr.shape[1:] if ax is None else tuple(d // (N_CHIPS if i == ax else 1) for i, d in enumerate(arr.shape))
        out_shapes.append(jax.ShapeDtypeStruct((3,) + part, arr.dtype))

    def body(*refs):
        srcs, outs = refs[:n], refs[n:2 * n]
        send_sems, recv_sems = refs[2 * n:]
        x, y, c = _coords()
        copies = []
        for k, (px, py) in enumerate(_chip_peers(x, y)):
            for t in range(n):
                src = _slot(srcs[t], axes[t], 2 * px + py, srcs[t].shape[axes[t]] if axes[t] is not None else 0)
                copies.append(pltpu.make_async_remote_copy(
                    src_ref=src, dst_ref=outs[t].at[k], send_sem=send_sems.at[k * n + t],
                    recv_sem=recv_sems.at[k * n + t], device_id=(px, py, c), device_id_type=MESH))
        for cp in copies:
            cp.start()
        for cp in copies:
            cp.wait_recv()
        for cp in copies:
            cp.wait_send()

    return pl.pallas_call(
        body, name=name, in_specs=[ANY] * n, out_specs=[ANY] * n, out_shape=out_shapes,
        scratch_shapes=[pltpu.SemaphoreType.DMA((3 * n,)), pltpu.SemaphoreType.DMA((3 * n,))],
    )(*[arr for arr, _ in items])


def _swap_sibling(arrs, name):
    n = len(arrs)

    def body(*refs):
        srcs, outs = refs[:n], refs[n:2 * n]
        send_sems, recv_sems = refs[2 * n:]
        x, y, c = _coords()
        copies = [pltpu.make_async_remote_copy(src_ref=srcs[t], dst_ref=outs[t], send_sem=send_sems.at[t],
                                               recv_sem=recv_sems.at[t], device_id=(x, y, 1 - c), device_id_type=MESH)
                  for t in range(n)]
        for cp in copies:
            cp.start()
        for cp in copies:
            cp.wait_recv()
        for cp in copies:
            cp.wait_send()

    return pl.pallas_call(
        body, name=name, in_specs=[ANY] * n, out_specs=[ANY] * n,
        out_shape=[jax.ShapeDtypeStruct(a_.shape, a_.dtype) for a_ in arrs],
        scratch_shapes=[pltpu.SemaphoreType.DMA((n,)), pltpu.SemaphoreType.DMA((n,))],
    )(*arrs)


def _allreduce_small(v, name):
    R, Wd = v.shape

    def body(v_ref, o_ref, buf, send_sems, recv_sems):
        x, y, c = _coords()
        o_ref[...] = v_ref[...]
        for st, peer in enumerate([(x, y, 1 - c), (x, 1 - y, c), (1 - x, y, c)]):
            cp = pltpu.make_async_remote_copy(src_ref=o_ref, dst_ref=buf.at[st], send_sem=send_sems.at[st],
                                              recv_sem=recv_sems.at[st], device_id=peer, device_id_type=MESH)
            cp.start()
            cp.wait_recv()
            cp.wait_send()
            o_ref[...] = o_ref[...] + buf[st]

    vm = pl.BlockSpec(memory_space=pltpu.VMEM)
    return pl.pallas_call(
        body, name=name, in_specs=[vm], out_specs=vm,
        out_shape=jax.ShapeDtypeStruct((R, Wd), F32),
        scratch_shapes=[pltpu.VMEM((3, R, Wd), F32), pltpu.SemaphoreType.DMA((3,)), pltpu.SemaphoreType.DMA((3,))],
    )(v)


def _row_tile(rows):
    t = rows
    while t > 256:
        assert t % 2 == 0
        t //= 2
    assert t % SUBLANES == 0
    return t


def _sum_partials(g, recv, axis, *, name):
    _, L, R, C = recv.shape
    tr = _row_tile(R)
    me = (2 * lax.axis_index("x") + lax.axis_index("y")).astype(jnp.int32).reshape(1)

    def body(me_ref, g_ref, r_ref, o_ref):
        o_ref[...] = ((g_ref[...] + r_ref[0].astype(F32)) + r_ref[1].astype(F32)) + r_ref[2].astype(F32)

    if axis is None:
        g_spec = pl.BlockSpec((None, None, tr, C), lambda l, i, me: (me[0], l, i, 0))
    elif axis == 1:
        g_spec = pl.BlockSpec((None, tr, C), lambda l, i, me: (l, me[0] * (R // tr) + i, 0))
    else:
        g_spec = pl.BlockSpec((None, tr, C), lambda l, i, me: (l, i, me[0]))
    grid_spec = pltpu.PrefetchScalarGridSpec(
        num_scalar_prefetch=1, grid=(L, R // tr),
        in_specs=[g_spec, pl.BlockSpec((3, None, tr, C), lambda l, i, me: (0, l, i, 0))],
        out_specs=pl.BlockSpec((None, tr, C), lambda l, i, me: (l, i, 0)))
    return pl.pallas_call(
        body, name=name, grid_spec=grid_spec, out_shape=jax.ShapeDtypeStruct((L, R, C), F32),
        compiler_params=_cp(("parallel", "parallel")),
    )(me, g, recv)


def _adamw(w, m, v, ga, gb, *, name):
    L, R, C = w.shape
    tr = _row_tile(R)
    two = gb is not None
    c1 = 1.0 / (1.0 - ADAM_B1 ** ADAM_STEP)
    c2 = 1.0 / (1.0 - ADAM_B2 ** ADAM_STEP)

    def body(*refs):
        if two:
            w_ref, m_ref, v_ref, ga_ref, gb_ref, g_ref, d_ref, mo_ref, vo_ref = refs
            g = ga_ref[...] + gb_ref[...]
        else:
            w_ref, m_ref, v_ref, ga_ref, g_ref, d_ref, mo_ref, vo_ref = refs
            g = ga_ref[...]
        m2 = ADAM_B1 * m_ref[...] + (1.0 - ADAM_B1) * g
        v2 = ADAM_B2 * v_ref[...] + (1.0 - ADAM_B2) * (g * g)
        g_ref[...] = g
        mo_ref[...] = m2
        vo_ref[...] = v2
        d_ref[...] = -ADAM_LR * ((m2 * c1) / (jnp.sqrt(v2 * c2) + ADAM_EPS) + ADAM_WD * w_ref[...])

    blk = pl.BlockSpec((None, tr, C), lambda l, i: (l, i, 0))
    args = [w, m, v, ga] + ([gb] if two else [])
    return pl.pallas_call(
        body, name=name, grid=(L, R // tr), in_specs=[blk] * len(args), out_specs=[blk] * 4,
        out_shape=[jax.ShapeDtypeStruct((L, R, C), F32)] * 4, compiler_params=_cp(("parallel", "parallel")),
    )(*args)


SHARDED = [
    ("ffn_w_up", 2, True), ("ffn_conv_w", 2, False), ("ffn_w_down", 1, True),
    ("lru_w_in", 2, True), ("lru_conv_w", 2, False), ("lru_conv_b", 1, False),
    ("lru_w_a", 2, True), ("lru_b_a", 2, False), ("lru_w_x", 2, True), ("lru_b_x", 2, False),
    ("lru_lambda", 1, False), ("lru_w_out", 1, True),
    ("mla_w_in", 2, True), ("mla_w_uq", 2, True), ("mla_w_ukv", 2, True), ("mla_w_out", 1, True),
    ("ret_w_in", 2, True), ("ret_gn_g", 1, False), ("ret_w_out", 1, True),
]
BIG_AXIS = {"ffn_w_up": 2, "ffn_w_down": 1, "lru_w_in": 2, "lru_w_out": 1, "ret_w_in": 2, "ret_w_out": 1}
REPLICATED = ["ln1_g", "ln1_b", "ln2_g", "ln2_b", "ffn_conv_b", "mla_q_norm", "mla_kv_norm"]
WEIGHTS = ["ln1_g", "ln1_b", "ln2_g", "ln2_b", "ffn_w_up", "ffn_conv_w", "ffn_conv_b", "ffn_w_down", "lru_w_in",
           "lru_conv_w", "lru_conv_b", "lru_w_a", "lru_b_a", "lru_w_x", "lru_b_x", "lru_lambda", "lru_w_out",
           "mla_w_in", "mla_q_norm", "mla_kv_norm", "mla_w_uq", "mla_w_ukv", "mla_w_out", "ret_w_in", "ret_gn_g",
           "ret_w_out"]
PACK_ROWS = 512


def _pack(arrs, dtype, lead=(), rows=PACK_ROWS):
    nl = len(lead)
    flat = jnp.concatenate([a.astype(dtype).reshape(lead + (-1,)) for a in arrs], axis=nl)
    n = flat.shape[nl]
    quantum = rows * PACK_W
    total = -(-n // quantum) * quantum
    flat = jnp.pad(flat, [(0, 0)] * nl + [(0, total - n)])
    return flat.reshape(lead + (total // PACK_W, PACK_W))


def _unpack(buf, shapes, lead=()):
    nl = len(lead)
    flat = buf.reshape(lead + (-1,))
    out, off = [], 0
    for shp in shapes:
        n = int(np.prod(shp))
        out.append(lax.slice_in_dim(flat, off, off + n, axis=nl).reshape(lead + tuple(shp)))
        off += n
    return out


def _layer_params(full, rep, i):
    kind, j = i % 3, i // 3
    ffn = dict(w_up=full["ffn_w_up"][i], conv_w=full["ffn_conv_w"][i], conv_b=rep["ffn_conv_b"][i],
               w_down=full["ffn_w_down"][i])
    if kind == 0:
        mix = dict(w_in=full["lru_w_in"][j], conv_w=full["lru_conv_w"][j], conv_b=full["lru_conv_b"][j],
                   w_a=full["lru_w_a"][j], b_a=full["lru_b_a"][j], w_x=full["lru_w_x"][j], b_x=full["lru_b_x"][j],
                   lam=full["lru_lambda"][j], w_out=full["lru_w_out"][j])
    elif kind == 1:
        w_in_p, w_uq_p, w_ukv_p = _mla_permute_weights(full["mla_w_in"][j], full["mla_w_uq"][j], full["mla_w_ukv"][j])
        mix = dict(w_in_p=w_in_p, w_uq_p=w_uq_p, w_ukv_p=w_ukv_p, q_norm=rep["mla_q_norm"][j],
                   kv_norm=rep["mla_kv_norm"][j], w_out=full["mla_w_out"][j])
    else:
        mix = dict(w_in=full["ret_w_in"][j], gn_g=full["ret_gn_g"][j], w_out=full["ret_w_out"][j])
    return kind, mix, ffn


_MIX_FWD = {0: lambda x, pos, p, tag: _lru_mixer_fwd(x, p, tag), 1: _mla_mixer_fwd, 2: _ret_mixer_fwd}
_MIX_BWD = {0: _lru_mixer_bwd, 1: _mla_mixer_bwd, 2: _ret_mixer_bwd}
_MIX_PREFIX = {0: "lru_", 1: "mla_", 2: "ret_"}
_MIX_KEYS = {0: {"w_in": "lru_w_in", "conv_w": "lru_conv_w", "conv_b": "lru_conv_b", "w_a": "lru_w_a", "b_a": "lru_b_a",
                 "w_x": "lru_w_x", "b_x": "lru_b_x", "lam": "lru_lambda", "w_out": "lru_w_out"},
             1: {"w_in": "mla_w_in", "q_norm": "mla_q_norm", "kv_norm": "mla_kv_norm", "w_uq": "mla_w_uq",
                 "w_ukv": "mla_w_ukv", "w_out": "mla_w_out"},
             2: {"w_in": "ret_w_in", "gn_g": "ret_gn_g", "w_out": "ret_w_out"}}
_FFN_KEYS = {"w_up": "ffn_w_up", "conv_w": "ffn_conv_w", "conv_b": "ffn_conv_b", "w_down": "ffn_w_down"}


def _local_step(x, pos, target, full, rep):
    S, D = x.shape
    ts = _rt(S, 512)
    acts = []
    h = x
    for i in range(DEPTH):
        kind, mp, fp = _layer_params(full, rep, i)
        tag = "l%d" % i
        mix, msaved = _MIX_FWD[kind](h, pos, mp, tag + "m")
        h1, z1 = _ln_fwd(h, mix, rep["ln1_g"][i], rep["ln1_b"][i], ts=ts, name=tag + "_ln1")
        f, fsaved = _ffn_fwd(h1, fp, tag + "f")
        h2, z2 = _ln_fwd(h1, f, rep["ln2_g"][i], rep["ln2_b"][i], ts=ts, name=tag + "_ln2")
        acts.append((kind, mp, fp, h, msaved, h1, z1, fsaved, z2))
        h = h2
    dy, part = _loss_head(h, target, ts=ts, name="loss_head")

    grads = {n: {} for n in WEIGHTS if n not in BIG_AXIS}
    gbuf = {}
    d_a, d_b = dy, None
    for i in reversed(range(DEPTH)):
        kind, mp, fp, h_in, msaved, h1, z1, fsaved, z2 = acts[i]
        tag = "l%d" % i
        dz2, dg8, db8 = _ln_bwd(d_a, d_b, z2, rep["ln2_g"][i], ts=ts, name=tag + "_ln2b")
        grads["ln2_g"][i], grads["ln2_b"][i] = dg8.sum(axis=0), db8.sum(axis=0)
        dx_f, gf = _ffn_bwd(dz2, h1, fp, fsaved, tag + "f", gbuf, i)
        for k, v in gf.items():
            grads[_FFN_KEYS[k]][i] = v
        dz1, dg8, db8 = _ln_bwd(dz2, dx_f, z1, rep["ln1_g"][i], ts=ts, name=tag + "_ln1b")
        grads["ln1_g"][i], grads["ln1_b"][i] = dg8.sum(axis=0), db8.sum(axis=0)
        dx_m, gm = _MIX_BWD[kind](dz1, h_in, mp, msaved, tag + "m", gbuf, i // 3)
        for k, v in gm.items():
            grads[_MIX_KEYS[kind][k]][i // 3] = v
        d_a, d_b = dz1, dx_m
    grad_x = _axpy(d_a, d_b, ts=ts, name="grad_x")
    stacked = {n: jnp.stack([grads[n][j] for j in sorted(grads[n])]) for n in grads}
    return part, grad_x, stacked, gbuf


def kernel(x, positions, ln1_g, ln1_b, ln2_g, ln2_b, ffn_w_up, ffn_conv_w, ffn_conv_b, ffn_w_down, lru_w_in, lru_conv_w, lru_conv_b, lru_w_a, lru_b_a, lru_w_x, lru_b_x, lru_lambda, lru_w_out, mla_w_in, mla_q_norm, mla_kv_norm, mla_w_uq, mla_w_ukv, mla_w_out, ret_w_in, ret_gn_g, ret_w_out, loss_target, m_ln1_g, m_ln1_b, m_ln2_g, m_ln2_b, m_ffn_w_up, m_ffn_conv_w, m_ffn_conv_b, m_ffn_w_down, m_lru_w_in, m_lru_conv_w, m_lru_conv_b, m_lru_w_a, m_lru_b_a, m_lru_w_x, m_lru_b_x, m_lru_lambda, m_lru_w_out, m_mla_w_in, m_mla_q_norm, m_mla_kv_norm, m_mla_w_uq, m_mla_w_ukv, m_mla_w_out, m_ret_w_in, m_ret_gn_g, m_ret_w_out, v_ln1_g, v_ln1_b, v_ln2_g, v_ln2_b, v_ffn_w_up, v_ffn_conv_w, v_ffn_conv_b, v_ffn_w_down, v_lru_w_in, v_lru_conv_w, v_lru_conv_b, v_lru_w_a, v_lru_b_a, v_lru_w_x, v_lru_b_x, v_lru_lambda, v_lru_w_out, v_mla_w_in, v_mla_q_norm, v_mla_kv_norm, v_mla_w_uq, v_mla_w_ukv, v_mla_w_out, v_ret_w_in, v_ret_gn_g, v_ret_w_out):
    w = dict(ln1_g=ln1_g, ln1_b=ln1_b, ln2_g=ln2_g, ln2_b=ln2_b, ffn_w_up=ffn_w_up, ffn_conv_w=ffn_conv_w, ffn_conv_b=ffn_conv_b, ffn_w_down=ffn_w_down, lru_w_in=lru_w_in, lru_conv_w=lru_conv_w, lru_conv_b=lru_conv_b, lru_w_a=lru_w_a, lru_b_a=lru_b_a, lru_w_x=lru_w_x, lru_b_x=lru_b_x, lru_lambda=lru_lambda, lru_w_out=lru_w_out, mla_w_in=mla_w_in, mla_q_norm=mla_q_norm, mla_kv_norm=mla_kv_norm, mla_w_uq=mla_w_uq, mla_w_ukv=mla_w_ukv, mla_w_out=mla_w_out, ret_w_in=ret_w_in, ret_gn_g=ret_gn_g, ret_w_out=ret_w_out)
    m = dict(ln1_g=m_ln1_g, ln1_b=m_ln1_b, ln2_g=m_ln2_g, ln2_b=m_ln2_b, ffn_w_up=m_ffn_w_up, ffn_conv_w=m_ffn_conv_w, ffn_conv_b=m_ffn_conv_b, ffn_w_down=m_ffn_w_down, lru_w_in=m_lru_w_in, lru_conv_w=m_lru_conv_w, lru_conv_b=m_lru_conv_b, lru_w_a=m_lru_w_a, lru_b_a=m_lru_b_a, lru_w_x=m_lru_w_x, lru_b_x=m_lru_b_x, lru_lambda=m_lru_lambda, lru_w_out=m_lru_w_out, mla_w_in=m_mla_w_in, mla_q_norm=m_mla_q_norm, mla_kv_norm=m_mla_kv_norm, mla_w_uq=m_mla_w_uq, mla_w_ukv=m_mla_w_ukv, mla_w_out=m_mla_w_out, ret_w_in=m_ret_w_in, ret_gn_g=m_ret_gn_g, ret_w_out=m_ret_w_out)
    v = dict(ln1_g=v_ln1_g, ln1_b=v_ln1_b, ln2_g=v_ln2_g, ln2_b=v_ln2_b, ffn_w_up=v_ffn_w_up, ffn_conv_w=v_ffn_conv_w, ffn_conv_b=v_ffn_conv_b, ffn_w_down=v_ffn_w_down, lru_w_in=v_lru_w_in, lru_conv_w=v_lru_conv_w, lru_conv_b=v_lru_conv_b, lru_w_a=v_lru_w_a, lru_b_a=v_lru_b_a, lru_w_x=v_lru_w_x, lru_b_x=v_lru_b_x, lru_lambda=v_lru_lambda, lru_w_out=v_lru_w_out, mla_w_in=v_mla_w_in, mla_q_norm=v_mla_q_norm, mla_kv_norm=v_mla_kv_norm, mla_w_uq=v_mla_w_uq, mla_w_ukv=v_mla_w_ukv, mla_w_out=v_mla_w_out, ret_w_in=v_ret_w_in, ret_gn_g=v_ret_gn_g, ret_w_out=v_ret_w_out)
    D = x.shape[-1]
    axis_of = {n: ax for n, ax, _ in SHARDED}
    big = list(BIG_AXIS)
    small_mx = [n for n, _, mx in SHARDED if mx and n not in BIG_AXIS]
    small_vec = [n for n, _, mx in SHARDED if not mx]
    small = small_mx + small_vec

    gathered = _gather_chips([(w[n].astype(BF16), BIG_AXIS[n]) for n in big]
                             + [(_pack([w[n] for n in small_mx], BF16), None), (_pack([w[n] for n in small_vec], F32), None)],
                             "gather_weights")
    full = dict(zip(big, gathered))
    for names, buf in ((small_mx, gathered[-2]), (small_vec, gathered[-1])):
        blocks = _unpack(buf, [w[n].shape for n in names], lead=(N_CHIPS,))
        for n, blk in zip(names, blocks):
            full[n] = jnp.concatenate([blk[s] for s in range(N_CHIPS)], axis=axis_of[n])
    rep = {n: w[n] for n in REPLICATED}

    part, grad_x, grads, gbuf = _local_step(x[0], positions[0], loss_target[0], full, rep)
    loss = lax.psum((0.5 / D) * jnp.sum(part), MESH_AXES)

    g_pack = _pack([jnp.stack(jnp.split(grads[n], N_CHIPS, axis=axis_of[n])) for n in small], F32, lead=(N_CHIPS,))
    recv = _scatter_chips([(gbuf[n][1], BIG_AXIS[n]) for n in big] + [(g_pack, None)], "scatter_grads")
    sums = [_sum_partials(gbuf[n][0], r, BIG_AXIS[n], name="sum_" + n) for n, r in zip(big, recv)]
    sums.append(_sum_partials(g_pack[:, None], recv[-1][:, None], None, name="sum_small"))
    sibs = _swap_sibling(sums, "swap_core_partials")
    res = {kind: {} for kind in "gdmv"}
    for n, p_mine, p_sib in zip(big, sums, sibs):
        for kind, o in zip("gdmv", _adamw(w[n], m[n], v[n], p_mine, p_sib, name="adamw_" + n)):
            res[kind][n] = o
    spack = lambda d: _pack([d[n] for n in small], F32)[None]
    shapes = [w[n].shape for n in small]
    for kind, o in zip("gdmv", _adamw(spack(w), spack(m), spack(v), sums[-1], sibs[-1], name="adamw_small")):
        res[kind].update(zip(small, _unpack(o[0], shapes)))

    r_shapes = [w[n].shape for n in REPLICATED]
    rpack = lambda d: _pack([d[n] for n in REPLICATED], F32, rows=SUBLANES)
    r_sum = _allreduce_small(rpack(grads), "allreduce_replicated")
    r_outs = _adamw(rpack(w)[None], rpack(m)[None], rpack(v)[None], r_sum[None], None, name="adamw_replicated")
    for kind, o in zip("gdmv", r_outs):
        res[kind].update(zip(REPLICATED, _unpack(o[0], r_shapes)))

    return (loss, grad_x[None], *[res["g"][n] for n in WEIGHTS], *[res["d"][n] for n in WEIGHTS],
            *[res["m"][n] for n in WEIGHTS], *[res["v"][n] for n in WEIGHTS])
```

```python
import functools
import math

import numpy as np
import jax
import jax.numpy as jnp
from jax import lax
from jax.experimental import pallas as pl
from jax.experimental.pallas import tpu as pltpu

F32 = jnp.float32
BF16 = jnp.bfloat16

DEPTH = 4
ALPHA = (2.0 * DEPTH) ** 0.25
LN_EPS = 1e-5
RMS_EPS = 1e-6
ROPE_BASE = 10000.0
CHUNK = 64
LRU_C = 8.0
LRU_GROUPS = 4
MLA_HEADS = 16
QK_NOPE, QK_ROPE, V_HEAD = 64, 32, 64
Q_LORA, KV_LORA = 768, 256
RET_HEADS = 4
ADAM_LR, ADAM_B1, ADAM_B2, ADAM_EPS, ADAM_WD, ADAM_STEP = 0.001, 0.9, 0.999, 1e-08, 0.01, 10

LANES = 128
SUBLANES = 8
VMEM_LIMIT = 56 * 1024 * 1024

MESH_AXES = ("x", "y", "c")
N_CHIPS = 4


def _cp(sem):
    return pltpu.CompilerParams(dimension_semantics=sem, vmem_limit_bytes=VMEM_LIMIT)


def _sigmoid(x):
    return 1.0 / (1.0 + jnp.exp(-x))


_GELU_C = math.sqrt(2.0 / math.pi)


def _gelu_parts(x):
    x2 = x * x
    u = _GELU_C * (x + 0.044715 * x * x2)
    t = jnp.tanh(u)
    g = 0.5 * x * (1.0 + t)
    dg = 0.5 * (1.0 + t) + 0.5 * x * (1.0 - t * t) * _GELU_C * (1.0 + 3.0 * 0.044715 * x2)
    return g, dg


def _fold8(v):
    n = v.shape[0] // SUBLANES
    return v.reshape(n, SUBLANES, v.shape[1]).sum(axis=0)


def _dot(a, b):
    return jnp.dot(a, b, preferred_element_type=F32)


def _dot_tb(a, b):
    return lax.dot_general(a, b, (((1,), (1,)), ((), ())), preferred_element_type=F32)


def _dot_ta(a, b):
    return lax.dot_general(a, b, (((0,), (0,)), ((), ())), preferred_element_type=F32)


def _mm(a, b, *, out_dtype, tm, tn, name, a_koff=0):
    M = a.shape[0]
    K, N = b.shape

    def body(a_ref, b_ref, o_ref):
        o_ref[...] = _dot(a_ref[...].astype(BF16), b_ref[...].astype(BF16)).astype(out_dtype)

    return pl.pallas_call(
        body, name=name, grid=(M // tm, N // tn),
        in_specs=[pl.BlockSpec((tm, K), lambda i, j: (i, a_koff)),
                  pl.BlockSpec((K, tn), lambda i, j: (0, j))],
        out_specs=pl.BlockSpec((tm, tn), lambda i, j: (i, j)),
        out_shape=jax.ShapeDtypeStruct((M, N), out_dtype),
        compiler_params=_cp(("parallel", "parallel")),
    )(a, b)


def _mm_tb(pairs, b, *, out_dtype, tm, tk, name):
    M = pairs[0][0].shape[0]
    Kout = b.shape[0]
    n = len(pairs)

    def body(*refs):
        a_refs, b_refs, o_ref = refs[:n], refs[n:2 * n], refs[2 * n]
        acc = None
        for a_ref, b_ref in zip(a_refs, b_refs):
            t = _dot_tb(a_ref[...].astype(BF16), b_ref[...].astype(BF16))
            acc = t if acc is None else acc + t
        o_ref[...] = acc.astype(out_dtype)

    in_specs = [pl.BlockSpec((tm, a.shape[1]), lambda i, j: (i, 0)) for a, _ in pairs]
    for a, c0 in pairs:
        w = a.shape[1]
        assert c0 % w == 0
        in_specs.append(pl.BlockSpec((tk, w), functools.partial(lambda i, j, cb: (j, cb), cb=c0 // w)))
    return pl.pallas_call(
        body, name=name, grid=(M // tm, Kout // tk),
        in_specs=in_specs,
        out_specs=pl.BlockSpec((tm, tk), lambda i, j: (i, j)),
        out_shape=jax.ShapeDtypeStruct((M, Kout), out_dtype),
        compiler_params=_cp(("parallel", "parallel")),
    )(*[a for a, _ in pairs], *[b for _ in pairs])


def _mm_ta(a, b, *, tk, tn, tm, name, a_c0=0, a_w=None, b_c0=0, b_w=None, dest=None):
    M = a.shape[0]
    nm = M // tm
    a_w = a.shape[1] if a_w is None else a_w
    b_w = b.shape[1] if b_w is None else b_w
    assert a_c0 % tk == 0 and b_c0 % tn == 0 and a_w % tk == 0 and b_w % tn == 0

    def body(*refs):
        a_ref, b_ref = refs[0], refs[1]
        o_ref = refs[-1] if dest is None else refs[-2]

        @pl.when(pl.program_id(2) == 0)
        def _():
            o_ref[...] = jnp.zeros_like(o_ref)
        o_ref[...] += _dot_ta(a_ref[...].astype(BF16), b_ref[...].astype(BF16))

        if dest is not None:
            @pl.when(pl.program_id(2) == nm - 1)
            def _():
                refs[-1][...] = o_ref[...].astype(BF16)

    in_specs = [pl.BlockSpec((tm, tk), lambda i, j, m: (m, i + a_c0 // tk)),
                pl.BlockSpec((tm, tn), lambda i, j, m: (m, j + b_c0 // tn))]
    args = [a, b]
    if dest is None:
        out_spec = pl.BlockSpec((tk, tn), lambda i, j, m: (i, j))
        out_shape = jax.ShapeDtypeStruct((a_w, b_w), F32)
        aliases = {}
    else:
        bufs, full_shape, layer, r0, c0 = dest
        assert r0 % tk == 0 and c0 % tn == 0
        spec = pl.BlockSpec((None, tk, tn), lambda i, j, m: (layer, i + r0 // tk, j + c0 // tn))
        out_spec = [spec, spec]
        out_shape = [jax.ShapeDtypeStruct(full_shape, F32), jax.ShapeDtypeStruct(full_shape, BF16)]
        aliases = {}
        if bufs is not None:
            in_specs += [pl.BlockSpec(memory_space=pl.ANY)] * 2
            args += list(bufs)
            aliases = {2: 0, 3: 1}
    return pl.pallas_call(
        body, name=name, grid=(a_w // tk, b_w // tn, nm),
        in_specs=in_specs, out_specs=out_spec, out_shape=out_shape, input_output_aliases=aliases,
        compiler_params=_cp(("parallel", "parallel", "arbitrary")),
    )(*args)


def _grad_into(gbuf, key, full_shape, layer, r0, c0, a, b, **kw):
    gbuf[key] = tuple(_mm_ta(a, b, dest=(gbuf.get(key), full_shape, layer, r0, c0), **kw))


def _ln_fwd(x, mix, g, b, *, ts, name):
    S, D = x.shape

    def body(x_ref, m_ref, g_ref, b_ref, o_ref, z_ref):
        z = ALPHA * x_ref[...] + m_ref[...]
        mu = jnp.mean(z, axis=-1, keepdims=True)
        zc = z - mu
        var = jnp.mean(zc * zc, axis=-1, keepdims=True)
        o_ref[...] = zc * lax.rsqrt(var + LN_EPS) * g_ref[...] + b_ref[...]
        z_ref[...] = z

    row = pl.BlockSpec((ts, D), lambda i: (i, 0))
    vec = pl.BlockSpec((1, D), lambda i: (0, 0))
    return pl.pallas_call(
        body, name=name, grid=(S // ts,),
        in_specs=[row, row, vec, vec], out_specs=[row, row],
        out_shape=[jax.ShapeDtypeStruct((S, D), F32)] * 2,
        compiler_params=_cp(("parallel",)),
    )(x, mix, g.reshape(1, D), b.reshape(1, D))


def _ln_bwd(da, db, z, g, *, ts, name):
    S, D = z.shape
    two = db is not None

    def body(*refs):
        if two:
            da_ref, db_ref, z_ref, g_ref, dz_ref, dg_ref, dbias_ref = refs
            dout = ALPHA * da_ref[...] + db_ref[...]
        else:
            da_ref, z_ref, g_ref, dz_ref, dg_ref, dbias_ref = refs
            dout = da_ref[...]

        @pl.when(pl.program_id(0) == 0)
        def _():
            dg_ref[...] = jnp.zeros_like(dg_ref)
            dbias_ref[...] = jnp.zeros_like(dbias_ref)

        z = z_ref[...]
        mu = jnp.mean(z, axis=-1, keepdims=True)
        zc = z - mu
        var = jnp.mean(zc * zc, axis=-1, keepdims=True)
        rstd = lax.rsqrt(var + LN_EPS)
        xhat = zc * rstd
        dxh = dout * g_ref[...]
        m1 = jnp.mean(dxh, axis=-1, keepdims=True)
        m2 = jnp.mean(dxh * xhat, axis=-1, keepdims=True)
        dz_ref[...] = rstd * (dxh - m1 - xhat * m2)
        dg_ref[...] += _fold8(dout * xhat)
        dbias_ref[...] += _fold8(dout)

    row = pl.BlockSpec((ts, D), lambda i: (i, 0))
    vec = pl.BlockSpec((1, D), lambda i: (0, 0))
    acc = pl.BlockSpec((SUBLANES, D), lambda i: (0, 0))
    args = [da, db, z, g.reshape(1, D)] if two else [da, z, g.reshape(1, D)]
    return pl.pallas_call(
        body, name=name, grid=(S // ts,),
        in_specs=[row] * (3 if two else 2) + [vec],
        out_specs=[row, acc, acc],
        out_shape=[jax.ShapeDtypeStruct((S, D), F32), jax.ShapeDtypeStruct((SUBLANES, D), F32),
                   jax.ShapeDtypeStruct((SUBLANES, D), F32)],
        compiler_params=_cp(("arbitrary",)),
    )(*args)


def _loss_head(y, t, *, ts, name):
    S, D = y.shape

    def body(y_ref, t_ref, dy_ref, p_ref):
        @pl.when(pl.program_id(0) == 0)
        def _():
            p_ref[...] = jnp.zeros_like(p_ref)
        d = y_ref[...] - t_ref[...]
        dy_ref[...] = d * (1.0 / D)
        p_ref[...] += _fold8(d * d)

    row = pl.BlockSpec((ts, D), lambda i: (i, 0))
    acc = pl.BlockSpec((SUBLANES, D), lambda i: (0, 0))
    return pl.pallas_call(
        body, name=name, grid=(S // ts,),
        in_specs=[row, row], out_specs=[row, acc],
        out_shape=[jax.ShapeDtypeStruct((S, D), F32), jax.ShapeDtypeStruct((SUBLANES, D), F32)],
        compiler_params=_cp(("arbitrary",)),
    )(y, t)


def _axpy(a, b, *, ts, name):
    S, D = a.shape

    def body(a_ref, b_ref, o_ref):
        o_ref[...] = ALPHA * a_ref[...] + b_ref[...]

    row = pl.BlockSpec((ts, D), lambda i: (i, 0))
    return pl.pallas_call(
        body, name=name, grid=(S // ts,), in_specs=[row, row], out_specs=row,
        out_shape=jax.ShapeDtypeStruct((S, D), F32), compiler_params=_cp(("parallel",)),
    )(a, b)


def _prev_halo_spec(ts, tc, coff):
    r = ts // SUBLANES
    return pl.BlockSpec((SUBLANES, tc), lambda i, j: (jnp.maximum(i * r - 1, 0), j + coff))


def _fill_prev(buf, halo_ref, cur, i):
    buf[0:SUBLANES, :] = jnp.where(i > 0, halo_ref[...], 0.0)
    buf[SUBLANES:, :] = cur


def _conv_fwd(x, w, b, *, K, ts, tc, x_c0, name):
    S = x.shape[0]
    C = w.shape[1]
    coff = x_c0 // tc

    def body(x_ref, halo_ref, w_ref, b_ref, o_ref, buf):
        _fill_prev(buf, halo_ref, x_ref[...], pl.program_id(0))
        acc = b_ref[...] + w_ref[K - 1:K, :] * x_ref[...]
        for k in range(K - 1):
            acc = acc + w_ref[k:k + 1, :] * buf[pl.ds(SUBLANES - (K - 1) + k, ts), :]
        o_ref[...] = acc

    return pl.pallas_call(
        body, name=name, grid=(S // ts, C // tc),
        in_specs=[pl.BlockSpec((ts, tc), lambda i, j: (i, j + coff)), _prev_halo_spec(ts, tc, coff),
                  pl.BlockSpec((K, tc), lambda i, j: (0, j)), pl.BlockSpec((1, tc), lambda i, j: (0, j))],
        out_specs=pl.BlockSpec((ts, tc), lambda i, j: (i, j)),
        out_shape=jax.ShapeDtypeStruct((S, C), F32),
        scratch_shapes=[pltpu.VMEM((ts + SUBLANES, tc), F32)],
        compiler_params=_cp(("parallel", "parallel")),
    )(x, x, w, b.reshape(1, C))


def _conv_wgrad(dy, x, *, K, ts, tc, x_c0, name):
    S, C = dy.shape
    coff = x_c0 // tc

    def body(dy_ref, x_ref, halo_ref, dw_ref, db_ref, buf):
        i = pl.program_id(1)

        @pl.when(i == 0)
        def _():
            dw_ref[...] = jnp.zeros_like(dw_ref)
            db_ref[...] = jnp.zeros_like(db_ref)

        _fill_prev(buf, halo_ref, x_ref[...], i)
        dy_v = dy_ref[...]
        db_ref[...] += _fold8(dy_v)
        for k in range(K):
            xs = buf[pl.ds(SUBLANES - (K - 1) + k, ts), :]
            dw_ref[k] += _fold8(dy_v * xs)

    r = ts // SUBLANES
    return pl.pallas_call(
        body, name=name, grid=(C // tc, S // ts),
        in_specs=[pl.BlockSpec((ts, tc), lambda j, i: (i, j)),
                  pl.BlockSpec((ts, tc), lambda j, i: (i, j + coff)),
                  pl.BlockSpec((SUBLANES, tc), lambda j, i: (jnp.maximum(i * r - 1, 0), j + coff))],
        out_specs=[pl.BlockSpec((K, SUBLANES, tc), lambda j, i: (0, 0, j)),
                   pl.BlockSpec((SUBLANES, tc), lambda j, i: (0, j))],
        out_shape=[jax.ShapeDtypeStruct((K, SUBLANES, C), F32), jax.ShapeDtypeStruct((SUBLANES, C), F32)],
        scratch_shapes=[pltpu.VMEM((ts + SUBLANES, tc), F32)],
        compiler_params=_cp(("parallel", "arbitrary")),
    )(dy, x, x)


def _conv_bwd(dy, w, *, K, ts, tc, w_c0, out_dtype, name):
    S, C = dy.shape
    nb = S // ts
    r = ts // SUBLANES
    woff = w_c0 // tc

    def body(dy_ref, halo_ref, w_ref, o_ref, buf):
        i = pl.program_id(0)
        buf[0:ts, :] = dy_ref[...]
        buf[ts:, :] = jnp.where(i < nb - 1, halo_ref[...], 0.0)
        acc = w_ref[K - 1:K, :] * dy_ref[...]
        for k in range(K - 1):
            acc = acc + w_ref[k:k + 1, :] * buf[pl.ds(K - 1 - k, ts), :]
        o_ref[...] = acc.astype(out_dtype)

    return pl.pallas_call(
        body, name=name, grid=(nb, C // tc),
        in_specs=[pl.BlockSpec((ts, tc), lambda i, j: (i, j)),
                  pl.BlockSpec((SUBLANES, tc), lambda i, j: (jnp.minimum((i + 1) * r, nb * r - 1), j)),
                  pl.BlockSpec((K, tc), lambda i, j: (0, j + woff))],
        out_specs=pl.BlockSpec((ts, tc), lambda i, j: (i, j)),
        out_shape=jax.ShapeDtypeStruct((S, C), out_dtype),
        scratch_shapes=[pltpu.VMEM((ts + SUBLANES, tc), F32)],
        compiler_params=_cp(("parallel", "parallel")),
    )(dy, dy, w)


HALO16 = 16
FFN_UNROLL = 4
FFN_FWD_UNROLL = 8


def _ffn_taps_w(w_ref, cs):
    return [w_ref[k:k + 1, cs] for k in range(3)]


def _ffn_taps8(prev, cur):
    row = lax.broadcasted_iota(jnp.int32, cur.shape, 0)
    return (jnp.where(row < 2, pltpu.roll(prev, 2, 0), pltpu.roll(cur, 2, 0)),
            jnp.where(row < 1, pltpu.roll(prev, 1, 0), pltpu.roll(cur, 1, 0)), cur)


def _ffn_conv8(taps, w, b):
    return b + w[0] * taps[0] + w[1] * taps[1] + w[2] * taps[2]


def _ffn_conv8_t(dh, dh_next, w):
    row = lax.broadcasted_iota(jnp.int32, dh.shape, 0)
    s1 = jnp.where(row < SUBLANES - 1, pltpu.roll(dh, SUBLANES - 1, 0), pltpu.roll(dh_next, SUBLANES - 1, 0))
    s2 = jnp.where(row < SUBLANES - 2, pltpu.roll(dh, SUBLANES - 2, 0), pltpu.roll(dh_next, SUBLANES - 2, 0))
    return w[2] * dh + w[1] * s1 + w[0] * s2


def _ffn_mid_specs(ts, tc, nf, nb, with_next):
    r = ts // HALO16
    specs = []
    for off in (0, nf):
        specs.append(pl.BlockSpec((ts, tc), functools.partial(lambda j, i, o: (i, j + o), o=off)))
        specs.append(pl.BlockSpec((HALO16, tc), functools.partial(lambda j, i, o: (jnp.maximum(i * r - 1, 0), j + o), o=off)))
        if with_next:
            specs.append(pl.BlockSpec(
                (HALO16, tc), functools.partial(lambda j, i, o: (jnp.minimum((i + 1) * r, nb * r - 1), j + o), o=off)))
    for rows in (3, 1):
        for off in (0, nf):
            specs.append(pl.BlockSpec((rows, tc), functools.partial(lambda j, i, o: (0, j + o), o=off)))
    return specs


def _ffn_mid_fwd(hpre, w, b, *, ts, tc, name):
    S, F2 = hpre.shape
    F = F2 // 2
    nf = F // tc

    def body(g_ref, gp_ref, u_ref, up_ref, wg_ref, wu_ref, bg_ref, bu_ref, o_ref, gbuf, ubuf, obuf):
        first = pl.program_id(1) == 0
        for buf, prev, cur in ((gbuf, gp_ref, g_ref), (ubuf, up_ref, u_ref)):
            buf[0:HALO16, :] = jnp.where(first, 0.0, prev[...].astype(F32))
            buf[HALO16:, :] = cur[...].astype(F32)
        for lt in range(tc // LANES):
            cs = slice(lt * LANES, (lt + 1) * LANES)
            wg, wu = _ffn_taps_w(wg_ref, cs), _ffn_taps_w(wu_ref, cs)
            bg, bu = bg_ref[:, cs], bu_ref[:, cs]

            def step(c, carry):
                a_g, a_u = carry
                for un in range(FFN_FWD_UNROLL):
                    r0 = pl.multiple_of(c * (FFN_FWD_UNROLL * SUBLANES), SUBLANES) + un * SUBLANES
                    b_g = gbuf[pl.ds(HALO16 + r0, SUBLANES), cs]
                    b_u = ubuf[pl.ds(HALO16 + r0, SUBLANES), cs]
                    gel, _ = _gelu_parts(_ffn_conv8(_ffn_taps8(a_g, b_g), wg, bg))
                    obuf[pl.ds(r0, SUBLANES), cs] = gel * _ffn_conv8(_ffn_taps8(a_u, b_u), wu, bu)
                    a_g, a_u = b_g, b_u
                return a_g, a_u

            lax.fori_loop(0, ts // (FFN_FWD_UNROLL * SUBLANES), step,
                          (gbuf[HALO16 - SUBLANES:HALO16, cs], ubuf[HALO16 - SUBLANES:HALO16, cs]))
        o_ref[...] = obuf[...].astype(BF16)

    b2 = b.reshape(1, F2)
    return pl.pallas_call(
        body, name=name, grid=(nf, S // ts),
        in_specs=_ffn_mid_specs(ts, tc, nf, S // ts, False),
        out_specs=pl.BlockSpec((ts, tc), lambda j, i: (i, j)),
        out_shape=jax.ShapeDtypeStruct((S, F), BF16),
        scratch_shapes=[pltpu.VMEM((ts + HALO16, tc), F32)] * 2 + [pltpu.VMEM((ts, tc), F32)],
        compiler_params=_cp(("parallel", "parallel")),
    )(hpre, hpre, hpre, hpre, w, w, b2, b2)


def _ffn_mid_bwd(hpre, da, w, b, *, ts, tc, name):
    S, F2 = hpre.shape
    F = F2 // 2
    nf = F // tc
    nb = S // ts
    r = ts // HALO16
    nch = ts // SUBLANES

    def body(g_ref, gp_ref, gn_ref, u_ref, up_ref, un_ref, wg_ref, wu_ref, bg_ref, bu_ref, da_ref, dan_ref,
             dpg_ref, dpu_ref, dwg_ref, dwu_ref, dbg_ref, dbu_ref, gbuf, ubuf, dabuf, pgbuf, pubuf):
        i = pl.program_id(1)

        @pl.when(i == 0)
        def _():
            for ref in (dwg_ref, dwu_ref, dbg_ref, dbu_ref):
                ref[...] = jnp.zeros_like(ref)

        for buf, prev, cur, nxt in ((gbuf, gp_ref, g_ref, gn_ref), (ubuf, up_ref, u_ref, un_ref)):
            buf[0:HALO16, :] = jnp.where(i == 0, 0.0, prev[...].astype(F32))
            buf[HALO16:HALO16 + ts, :] = cur[...].astype(F32)
            buf[HALO16 + ts:, :] = nxt[...].astype(F32)
        dabuf[0:ts, :] = da_ref[...].astype(F32)
        dabuf[ts:, :] = jnp.where(i == nb - 1, 0.0, dan_ref[...].astype(F32))

        for lt in range(tc // LANES):
            cs = slice(lt * LANES, (lt + 1) * LANES)
            wg, wu = _ffn_taps_w(wg_ref, cs), _ffn_taps_w(wu_ref, cs)
            bg, bu = bg_ref[:, cs], bu_ref[:, cs]

            def piece(r0, a_g, a_u):
                b_g = gbuf[pl.ds(HALO16 + r0, SUBLANES), cs]
                b_u = ubuf[pl.ds(HALO16 + r0, SUBLANES), cs]
                tg, tu = _ffn_taps8(a_g, b_g), _ffn_taps8(a_u, b_u)
                gel, dgel = _gelu_parts(_ffn_conv8(tg, wg, bg))
                da_v = dabuf[pl.ds(r0, SUBLANES), cs]
                return da_v * _ffn_conv8(tu, wu, bu) * dgel, da_v * gel, tg, tu, b_g, b_u

            def step(c, carry):
                a_g, a_u, pdg, pdu, acc = carry
                for un in range(FFN_UNROLL):
                    r0 = pl.multiple_of(c * (FFN_UNROLL * SUBLANES), SUBLANES) + un * SUBLANES
                    dg, du, tg, tu, a_g, a_u = piece(r0, a_g, a_u)
                    pgbuf[pl.ds(r0, SUBLANES), cs] = _ffn_conv8_t(pdg, dg, wg)
                    pubuf[pl.ds(r0, SUBLANES), cs] = _ffn_conv8_t(pdu, du, wu)
                    acc = (tuple(a + dg * t for a, t in zip(acc[0], tg)), tuple(a + du * t for a, t in zip(acc[1], tu)),
                           acc[2] + dg, acc[3] + du)
                    pdg, pdu = dg, du
                return a_g, a_u, pdg, pdu, acc

            zero = jnp.zeros((SUBLANES, LANES), F32)
            a_g, a_u, pdg, pdu, acc = lax.fori_loop(
                0, nch // FFN_UNROLL, step,
                (gbuf[HALO16 - SUBLANES:HALO16, cs], ubuf[HALO16 - SUBLANES:HALO16, cs], zero, zero,
                 ((zero,) * 3, (zero,) * 3, zero, zero)))
            dg, du, _, _, _, _ = piece(ts, a_g, a_u)
            pgbuf[ts:ts + SUBLANES, cs] = _ffn_conv8_t(pdg, dg, wg)
            pubuf[ts:ts + SUBLANES, cs] = _ffn_conv8_t(pdu, du, wu)
            for k in range(3):
                dwg_ref[k, :, cs] += acc[0][k]
                dwu_ref[k, :, cs] += acc[1][k]
            dbg_ref[:, cs] += acc[2]
            dbu_ref[:, cs] += acc[3]
        dpg_ref[...] = pgbuf[SUBLANES:, :].astype(BF16)
        dpu_ref[...] = pubuf[SUBLANES:, :].astype(BF16)

    b2 = b.reshape(1, F2)
    blk = pl.BlockSpec((ts, tc), lambda j, i: (i, j))
    nxt = pl.BlockSpec((HALO16, tc), lambda j, i: (jnp.minimum((i + 1) * r, nb * r - 1), j))
    in_specs = _ffn_mid_specs(ts, tc, nf, nb, True) + [blk, nxt]
    out_specs = [blk, blk,
                 pl.BlockSpec((3, SUBLANES, tc), lambda j, i: (0, 0, j)), pl.BlockSpec((3, SUBLANES, tc), lambda j, i: (0, 0, j)),
                 pl.BlockSpec((SUBLANES, tc), lambda j, i: (0, j)), pl.BlockSpec((SUBLANES, tc), lambda j, i: (0, j))]
    return pl.pallas_call(
        body, name=name, grid=(nf, nb),
        in_specs=in_specs, out_specs=out_specs,
        out_shape=[jax.ShapeDtypeStruct((S, F), BF16)] * 2 + [jax.ShapeDtypeStruct((3, SUBLANES, F), F32)] * 2
                  + [jax.ShapeDtypeStruct((SUBLANES, F), F32)] * 2,
        scratch_shapes=[pltpu.VMEM((ts + 2 * HALO16, tc), F32)] * 2 + [pltpu.VMEM((ts + HALO16, tc), F32)]
                       + [pltpu.VMEM((ts + SUBLANES, tc), F32)] * 2,
        compiler_params=_cp(("parallel", "arbitrary")),
    )(hpre, hpre, hpre, hpre, hpre, hpre, w, w, b2, b2, da, da)


def _expm1(x):
    u = jnp.exp(x)
    um1 = u - 1.0
    safe = jnp.where(um1 == 0.0, 1.0, jnp.log(u))
    r = jnp.where(um1 == 0.0, x, um1 * x / safe)
    return jnp.where(x < -30.0, -1.0, r)


def _softplus(z):
    return jnp.maximum(z, 0.0) + jnp.log1p(jnp.exp(-jnp.abs(z)))


def _lru_gates(u, wa_ref, ba_ref, wx_ref, bx_ref, lam_ref):
    ub = u.astype(BF16)
    r = _sigmoid(_dot(ub, wa_ref[...]) + ba_ref[...])
    ig = _sigmoid(_dot(ub, wx_ref[...]) + bx_ref[...])
    sp = _softplus(-lam_ref[...])
    la = -LRU_C * r * sp
    a = jnp.exp(la)
    mult = jnp.sqrt(-_expm1(2.0 * la))
    return ub, r, ig, sp, a, mult


def _lru_fwd(u, proj, w_a, b_a, w_x, b_x, lam, *, ts, name):
    S, W = u.shape
    G = LRU_GROUPS
    gw = W // G
    nt = ts // SUBLANES

    def body(u_ref, gb_ref, wa_ref, ba_ref, wx_ref, bx_ref, lam_ref, y_ref, h_ref, a_buf, b_buf, carry):
        @pl.when(pl.program_id(0) == 0)
        def _():
            carry[...] = jnp.zeros_like(carry)

        for g in range(G):
            gs = slice(g * gw, (g + 1) * gw)
            u_v = u_ref[:, gs]
            _, _, ig, _, a, mult = _lru_gates(u_v, wa_ref.at[g], ba_ref.at[g], wx_ref.at[g], bx_ref.at[g], lam_ref.at[g])
            a_buf[:, gs] = a
            b_buf[:, gs] = mult * ig * u_v
        row = lax.broadcasted_iota(jnp.int32, (SUBLANES, W), 0)

        def tile(k, c):
            r0 = pl.multiple_of(k * SUBLANES, SUBLANES)
            A = a_buf[pl.ds(r0, SUBLANES), :]
            B = b_buf[pl.ds(r0, SUBLANES), :]
            for d in (1, 2, 4):
                m = row >= d
                B = jnp.where(m, A * pltpu.roll(B, d, 0) + B, B)
                A = jnp.where(m, A * pltpu.roll(A, d, 0), A)
            h = A * c + B
            h_ref[pl.ds(r0, SUBLANES), :] = h
            return h[SUBLANES - 1:SUBLANES, :]

        carry[...] = lax.fori_loop(0, nt, tile, carry[...])
        gel, _ = _gelu_parts(gb_ref[...])
        y_ref[...] = (gel * h_ref[...]).astype(BF16)

    blk = pl.BlockSpec((ts, W), lambda i: (i, 0))
    wsp = pl.BlockSpec((G, gw, gw), lambda i: (0, 0, 0))
    vsp = pl.BlockSpec((G, 1, gw), lambda i: (0, 0, 0))
    return pl.pallas_call(
        body, name=name, grid=(S // ts,),
        in_specs=[blk, blk, wsp, vsp, wsp, vsp, vsp],
        out_specs=[blk, blk],
        out_shape=[jax.ShapeDtypeStruct((S, W), BF16), jax.ShapeDtypeStruct((S, W), F32)],
        scratch_shapes=[pltpu.VMEM((ts, W), F32), pltpu.VMEM((ts, W), F32), pltpu.VMEM((1, W), F32)],
        compiler_params=_cp(("arbitrary",)),
    )(u, proj, w_a, b_a.reshape(G, 1, gw), w_x, b_x.reshape(G, 1, gw), lam.reshape(G, 1, gw))


def _lru_bwd(dy, u, proj, h, w_a, b_a, w_x, b_x, lam, *, ts, name):
    S, W = u.shape
    G = LRU_GROUPS
    gw = W // G
    nt = ts // SUBLANES
    nb = S // ts
    r8 = ts // SUBLANES

    def body(dy_ref, u_ref, gb_ref, h_ref, hh_ref, wa_ref, ba_ref, wx_ref, bx_ref, lam_ref,
             dgb_ref, du_ref, dwa_ref, dwx_ref, dba_ref, dbx_ref, dlam_ref,
             a_buf, q_buf, p_buf, hbuf, carry):
        i = pl.program_id(0)
        ib = nb - 1 - i

        @pl.when(i == 0)
        def _():
            carry[...] = jnp.zeros_like(carry)
            dwa_ref[...] = jnp.zeros_like(dwa_ref)
            dwx_ref[...] = jnp.zeros_like(dwx_ref)
            dba_ref[...] = jnp.zeros_like(dba_ref)
            dbx_ref[...] = jnp.zeros_like(dbx_ref)
            dlam_ref[...] = jnp.zeros_like(dlam_ref)

        def gates(g):
            gs = slice(g * gw, (g + 1) * gw)
            return gs, _lru_gates(u_ref[:, gs], wa_ref.at[g], ba_ref.at[g], wx_ref.at[g], bx_ref.at[g], lam_ref.at[g])

        for g in range(G):
            gs, (_, _, _, _, a, _) = gates(g)
            gel, dgel = _gelu_parts(gb_ref[:, gs])
            dy_v = dy_ref[:, gs]
            dgb_ref[:, gs] = (dy_v * h_ref[:, gs] * dgel).astype(BF16)
            a_buf[:, gs] = a
            q_buf[:, gs] = a * (dy_v * gel)
        row = lax.broadcasted_iota(jnp.int32, (SUBLANES, W), 0)

        def tile(kk, c):
            r0 = pl.multiple_of((nt - 1 - kk) * SUBLANES, SUBLANES)
            A = a_buf[pl.ds(r0, SUBLANES), :]
            B = q_buf[pl.ds(r0, SUBLANES), :]
            for d in (1, 2, 4):
                m = row < SUBLANES - d
                B = jnp.where(m, A * pltpu.roll(B, SUBLANES - d, 0) + B, B)
                A = jnp.where(m, A * pltpu.roll(A, SUBLANES - d, 0), A)
            P = A * c + B
            p_buf[pl.ds(r0, SUBLANES), :] = jnp.where(row == SUBLANES - 1, c, pltpu.roll(P, SUBLANES - 1, 0))
            return P[0:1, :]

        carry[...] = lax.fori_loop(0, nt, tile, carry[...])
        hbuf[0:SUBLANES, :] = jnp.where(ib > 0, hh_ref[...], 0.0)
        hbuf[SUBLANES:, :] = h_ref[...]
        for g in range(G):
            gs, (ub, r, ig, sp, a, mult) = gates(g)
            u_v = u_ref[:, gs]
            gel, _ = _gelu_parts(gb_ref[:, gs])
            Gt = dy_ref[:, gs] * gel + p_buf[:, gs]
            hprev = hbuf[pl.ds(SUBLANES - 1, ts), gs]
            da = Gt * hprev
            dmult = Gt * (ig * u_v)
            dla = da * a - dmult * (a * a) / mult
            dr = dla * (-LRU_C * sp)
            dlam_ref[g] += _fold8(dla * (LRU_C * r)) * _sigmoid(-lam_ref[g])
            dig = Gt * mult * u_v
            dzr = dr * r * (1.0 - r)
            dzi = dig * ig * (1.0 - ig)
            dzr_b = dzr.astype(BF16)
            dzi_b = dzi.astype(BF16)
            du_ref[:, gs] = Gt * mult * ig + _dot_tb(dzr_b, wa_ref[g]) + _dot_tb(dzi_b, wx_ref[g])
            dwa_ref[g] += _dot_ta(ub, dzr_b)
            dwx_ref[g] += _dot_ta(ub, dzi_b)
            dba_ref[g] += _fold8(dzr)
            dbx_ref[g] += _fold8(dzi)

    rblk = pl.BlockSpec((ts, W), lambda i: (nb - 1 - i, 0))
    halo = pl.BlockSpec((SUBLANES, W), lambda i: (jnp.maximum((nb - 1 - i) * r8 - 1, 0), 0))
    wsp = pl.BlockSpec((G, gw, gw), lambda i: (0, 0, 0))
    vsp = pl.BlockSpec((G, 1, gw), lambda i: (0, 0, 0))
    acc8 = pl.BlockSpec((G, SUBLANES, gw), lambda i: (0, 0, 0))
    return pl.pallas_call(
        body, name=name, grid=(nb,),
        in_specs=[rblk, rblk, rblk, rblk, halo, wsp, vsp, wsp, vsp, vsp],
        out_specs=[rblk, rblk, wsp, wsp, acc8, acc8, acc8],
        out_shape=[jax.ShapeDtypeStruct((S, W), BF16), jax.ShapeDtypeStruct((S, W), F32),
                   jax.ShapeDtypeStruct((G, gw, gw), F32), jax.ShapeDtypeStruct((G, gw, gw), F32),
                   jax.ShapeDtypeStruct((G, SUBLANES, gw), F32), jax.ShapeDtypeStruct((G, SUBLANES, gw), F32),
                   jax.ShapeDtypeStruct((G, SUBLANES, gw), F32)],
        scratch_shapes=[pltpu.VMEM((ts, W), F32)] * 3 + [pltpu.VMEM((ts + SUBLANES, W), F32),
                                                         pltpu.VMEM((1, W), F32)],
        compiler_params=_cp(("arbitrary",)),
    )(dy, u, proj, h, h, w_a, b_a.reshape(G, 1, gw), w_x, b_x.reshape(G, 1, gw), lam.reshape(G, 1, gw))


def _rt(S, pref):
    return min(S, pref)


def _lru_mixer_fwd(x, p, tag):
    S, D = x.shape
    W = p["w_out"].shape[0]
    ts = _rt(S, 512)
    proj = _mm(x, p["w_in"], out_dtype=F32, tm=ts, tn=2 * W, name=tag + "_proj")
    u = _conv_fwd(proj, p["conv_w"], p["conv_b"], K=4, ts=ts, tc=512, x_c0=W, name=tag + "_conv")
    y, h = _lru_fwd(u, proj, p["w_a"], p["b_a"], p["w_x"], p["b_x"], p["lam"], ts=ts, name=tag + "_scan")
    mix = _mm(y, p["w_out"], out_dtype=F32, tm=ts, tn=D, name=tag + "_out")
    return mix, (proj, u, h, y)


def _lru_mixer_bwd(dmix, x, p, saved, tag, gbuf, j):
    proj, u, h, y = saved
    S, D = x.shape
    W = p["w_out"].shape[0]
    ts = _rt(S, 512)
    g = {}
    dy = _mm_tb([(dmix, 0)], p["w_out"], out_dtype=F32, tm=ts, tk=W, name=tag + "_dy")
    n_lru = (DEPTH + 2) // 3
    _grad_into(gbuf, "lru_w_out", (n_lru, W, D), j, 0, 0, y, dmix, tk=W, tn=D, tm=ts, name=tag + "_dwout")
    dgb, du, g["w_a"], g["w_x"], dba8, dbx8, dlam8 = _lru_bwd(
        dy, u, proj, h, p["w_a"], p["b_a"], p["w_x"], p["b_x"], p["lam"], ts=ts, name=tag + "_scanb")
    g["b_a"] = dba8.sum(axis=1)
    g["b_x"] = dbx8.sum(axis=1)
    g["lam"] = dlam8.sum(axis=1).reshape(-1)
    dcw8, dcb8 = _conv_wgrad(du, proj, K=4, ts=ts, tc=512, x_c0=W, name=tag + "_convw")
    g["conv_w"] = dcw8.sum(axis=1)
    g["conv_b"] = dcb8.sum(axis=0)
    drnn = _conv_bwd(du, p["conv_w"], K=4, ts=ts, tc=512, w_c0=0, out_dtype=BF16, name=tag + "_convb")
    dx = _mm_tb([(dgb, 0), (drnn, W)], p["w_in"], out_dtype=F32, tm=ts, tk=D, name=tag + "_dx")
    _grad_into(gbuf, "lru_w_in", (n_lru, D, 2 * W), j, 0, 0, x, dgb, tk=D, tn=W, tm=ts, name=tag + "_dwin_g")
    _grad_into(gbuf, "lru_w_in", (n_lru, D, 2 * W), j, 0, W, x, drnn, tk=D, tn=W, tm=ts, name=tag + "_dwin_r")
    return dx, g


def _ffn_fwd(x, p, tag):
    S, D = x.shape
    F = p["w_down"].shape[0]
    ts = _rt(S, 512)
    tc = F // 2
    hpre = _mm(x, p["w_up"], out_dtype=BF16, tm=ts, tn=2 * F, name=tag + "_up")
    a = _ffn_mid_fwd(hpre, p["conv_w"], p["conv_b"], ts=_rt(S, 256), tc=tc, name=tag + "_mid")
    f = _mm(a, p["w_down"], out_dtype=F32, tm=ts, tn=D, name=tag + "_down")
    return f, (hpre, a)


def _ffn_bwd(df, x, p, saved, tag, gbuf, i):
    hpre, a = saved
    S, D = x.shape
    F = p["w_down"].shape[0]
    ts = _rt(S, 512)
    tw = _rt(S, 1024)
    tc = F // 2
    g = {}
    da = _mm_tb([(df, 0)], p["w_down"], out_dtype=BF16, tm=ts, tk=F, name=tag + "_da")
    _grad_into(gbuf, "ffn_w_down", (DEPTH, F, D), i, 0, 0, a, df, tk=tc, tn=D, tm=tw, name=tag + "_dwdown")
    dpg, dpu, dwg8, dwu8, dbg8, dbu8 = _ffn_mid_bwd(hpre, da, p["conv_w"], p["conv_b"], ts=_rt(S, 256), tc=tc,
                                                    name=tag + "_midb")
    g["conv_w"] = jnp.concatenate([dwg8.sum(axis=1), dwu8.sum(axis=1)], axis=1)
    g["conv_b"] = jnp.concatenate([dbg8.sum(axis=0), dbu8.sum(axis=0)], axis=0)
    dx = _mm_tb([(dpg, 0), (dpu, F)], p["w_up"], out_dtype=F32, tm=ts, tk=D, name=tag + "_dx")
    _grad_into(gbuf, "ffn_w_up", (DEPTH, D, 2 * F), i, 0, 0, x, dpg, tk=D, tn=tc, tm=tw, name=tag + "_dwup_g")
    _grad_into(gbuf, "ffn_w_up", (DEPTH, D, 2 * F), i, 0, F, x, dpu, tk=D, tn=tc, tm=tw, name=tag + "_dwup_u")
    return dx, g


HEAD_SLOT = LANES
MLA_SCALE = (QK_NOPE + QK_ROPE) ** -0.5
NEG_BIG = -1e30
ATTN_BLOCK = 1024


def _rms_fwd(x, g, *, c0, ts, name):
    S = x.shape[0]
    w = g.shape[0]

    def body(x_ref, g_ref, o_ref):
        xv = x_ref[...]
        rstd = lax.rsqrt(jnp.mean(xv * xv, axis=-1, keepdims=True) + RMS_EPS)
        o_ref[...] = (xv * rstd * g_ref[...]).astype(BF16)

    return pl.pallas_call(
        body, name=name, grid=(S // ts,),
        in_specs=[pl.BlockSpec((ts, w), lambda i: (i, c0 // w)), pl.BlockSpec((1, w), lambda i: (0, 0))],
        out_specs=pl.BlockSpec((ts, w), lambda i: (i, 0)),
        out_shape=jax.ShapeDtypeStruct((S, w), BF16), compiler_params=_cp(("parallel",)),
    )(x, g.reshape(1, w))


def _rms_bwd(dy, x, g, *, c0, ts, name):
    S = x.shape[0]
    w = g.shape[0]

    def body(dy_ref, x_ref, g_ref, dx_ref, dg_ref):
        @pl.when(pl.program_id(0) == 0)
        def _():
            dg_ref[...] = jnp.zeros_like(dg_ref)
        xv = x_ref[...]
        dyv = dy_ref[...]
        rstd = lax.rsqrt(jnp.mean(xv * xv, axis=-1, keepdims=True) + RMS_EPS)
        dyg = dyv * g_ref[...]
        m = jnp.mean(dyg * xv, axis=-1, keepdims=True)
        dx_ref[...] = (rstd * (dyg - xv * (rstd * rstd) * m)).astype(BF16)
        dg_ref[...] += _fold8(dyv * xv * rstd)

    return pl.pallas_call(
        body, name=name, grid=(S // ts,),
        in_specs=[pl.BlockSpec((ts, w), lambda i: (i, 0)), pl.BlockSpec((ts, w), lambda i: (i, c0 // w)),
                  pl.BlockSpec((1, w), lambda i: (0, 0))],
        out_specs=[pl.BlockSpec((ts, w), lambda i: (i, 0)), pl.BlockSpec((SUBLANES, w), lambda i: (0, 0))],
        out_shape=[jax.ShapeDtypeStruct((S, w), BF16), jax.ShapeDtypeStruct((SUBLANES, w), F32)],
        compiler_params=_cp(("arbitrary",)),
    )(dy, x, g.reshape(1, w))


def _mla_tables(pos):
    S = pos.shape[0]
    half = QK_ROPE // 2
    inv_freq = ROPE_BASE ** (-jnp.arange(0, QK_ROPE, 2, dtype=F32) / QK_ROPE)
    ang = pos.astype(F32)[:, None] * inv_freq
    cos, sin = jnp.cos(ang), jnp.sin(ang)
    z = lambda n: jnp.zeros((S, n), F32)
    pad = HEAD_SLOT - QK_NOPE - QK_ROPE
    c = jnp.concatenate([jnp.ones((S, QK_NOPE), F32), cos, cos, z(pad)], axis=1)
    s1 = jnp.concatenate([z(QK_NOPE), -sin, z(half), z(pad)], axis=1)
    s2 = jnp.concatenate([z(QK_NOPE), z(half), sin, z(pad)], axis=1)
    return c, s1, s2


def _mla_prep_fwd(qraw, kvraw, proj, tabs, *, kpe_c0, ts, name):
    S = qraw.shape[0]
    H = MLA_HEADS
    half = QK_ROPE // 2

    def body(q_ref, kn_ref, kpe_ref, c_ref, s1_ref, s2_ref, qo_ref, ko_ref):
        c, s1, s2 = c_ref[...], s1_ref[...], s2_ref[...]

        def rope(v):
            return v * c + pltpu.roll(v, HEAD_SLOT - half, 1) * s1 + pltpu.roll(v, half, 1) * s2

        kpe_r = rope(kpe_ref[...])
        for h in range(H):
            sl = slice(h * HEAD_SLOT, (h + 1) * HEAD_SLOT)
            qo_ref[:, sl] = (rope(q_ref[:, sl]) * MLA_SCALE).astype(BF16)
            ko_ref[:, sl] = (kn_ref[:, sl] + kpe_r).astype(BF16)

    wide = pl.BlockSpec((ts, H * HEAD_SLOT), lambda i: (i, 0))
    tab = pl.BlockSpec((ts, HEAD_SLOT), lambda i: (i, 0))
    return pl.pallas_call(
        body, name=name, grid=(S // ts,),
        in_specs=[wide, wide, pl.BlockSpec((ts, HEAD_SLOT), lambda i: (i, kpe_c0 // HEAD_SLOT)), tab, tab, tab],
        out_specs=[wide, wide],
        out_shape=[jax.ShapeDtypeStruct((S, H * HEAD_SLOT), BF16)] * 2,
        compiler_params=_cp(("parallel",)),
    )(qraw, kvraw, proj, *tabs)


def _mla_prep_bwd(dq, dk, dv, tabs, *, ts, name):
    S = dk.shape[0]
    H = MLA_HEADS
    half = QK_ROPE // 2
    kw = H * HEAD_SLOT
    vw = H * V_HEAD

    def body(dq_ref, dk_ref, dv_ref, c_ref, s1_ref, s2_ref, dqr_ref, dkv_ref, dkpe_ref):
        c, s1, s2 = c_ref[...], s1_ref[...], s2_ref[...]

        def rope_t(g):
            return g * c + pltpu.roll(g * s1, half, 1) + pltpu.roll(g * s2, HEAD_SLOT - half, 1)

        gsum = jnp.zeros((ts, HEAD_SLOT), F32)
        for h in range(H):
            sl = slice(h * HEAD_SLOT, (h + 1) * HEAD_SLOT)
            hs = slice((h % 2) * HEAD_SLOT, (h % 2 + 1) * HEAD_SLOT)
            dqr_ref[:, sl] = (rope_t(dq_ref[h // 2, :, hs]) * MLA_SCALE).astype(BF16)
            dkh = dk_ref[:, sl]
            dkv_ref[:, sl] = dkh.astype(BF16)
            gsum = gsum + dkh
        dkv_ref[:, kw:] = dv_ref[...].astype(BF16)
        lane = lax.broadcasted_iota(jnp.int32, (ts, HEAD_SLOT), 1)
        pe = jnp.logical_and(lane >= QK_NOPE, lane < QK_NOPE + QK_ROPE)
        dkpe_ref[...] = rope_t(jnp.where(pe, gsum, 0.0)).astype(BF16)

    tab = pl.BlockSpec((ts, HEAD_SLOT), lambda i: (i, 0))
    return pl.pallas_call(
        body, name=name, grid=(S // ts,),
        in_specs=[pl.BlockSpec((H // 2, ts, 2 * HEAD_SLOT), lambda i: (0, i, 0)), pl.BlockSpec((ts, kw), lambda i: (i, 0)),
                  pl.BlockSpec((ts, vw), lambda i: (i, 0)), tab, tab, tab],
        out_specs=[pl.BlockSpec((ts, kw), lambda i: (i, 0)), pl.BlockSpec((ts, kw + vw), lambda i: (i, 0)), tab],
        out_shape=[jax.ShapeDtypeStruct((S, kw), BF16), jax.ShapeDtypeStruct((S, kw + vw), BF16),
                   jax.ShapeDtypeStruct((S, HEAD_SLOT), BF16)],
        compiler_params=_cp(("parallel",)),
    )(dq, dk, dv, *tabs)


def _attn_pairs(nb, kv_outer):
    if kv_outer:
        pr = [(i, j) for j in range(nb) for i in range(j, nb)]
    else:
        pr = [(i, j) for i in range(nb) for j in range(i + 1)]
    return (jnp.asarray(np.array([p[0] for p in pr], np.int32)), jnp.asarray(np.array([p[1] for p in pr], np.int32)))


def _attn_scores(q_ref, k_ref, hh, diag, T):
    sl = slice(hh * HEAD_SLOT, (hh + 1) * HEAD_SLOT)
    s = _dot_tb(q_ref[:, sl], k_ref[:, sl])
    if not diag:
        return s
    row = lax.broadcasted_iota(jnp.int32, (T, T), 0) // CHUNK
    col = lax.broadcasted_iota(jnp.int32, (T, T), 1) // CHUNK
    return jnp.where(col <= row, s, NEG_BIG)


def _on_block_kind(i, j, step):
    @pl.when(i == j)
    def _():
        step(True)

    @pl.when(i != j)
    def _():
        step(False)


def _attn_fwd(q, k, kvraw, *, T, name):
    S = q.shape[0]
    NP = MLA_HEADS // 2
    nb = S // T
    ii, jj = _attn_pairs(nb, kv_outer=False)
    v_c0 = MLA_HEADS * HEAD_SLOT // LANES

    def body(ii_ref, jj_ref, q_ref, k_ref, v_ref, o_ref, lse_ref, m_sc, l_sc, acc_sc):
        t = pl.program_id(1)
        i, j = ii_ref[t], jj_ref[t]

        @pl.when(j == 0)
        def _():
            m_sc[...] = jnp.full_like(m_sc, NEG_BIG)
            l_sc[...] = jnp.zeros_like(l_sc)
            acc_sc[...] = jnp.zeros_like(acc_sc)

        lo = lax.broadcasted_iota(jnp.int32, (T, LANES), 1) < V_HEAD
        top = lax.broadcasted_iota(jnp.int32, (LANES, T), 0) < V_HEAD

        def step(diag):
            v = v_ref[...].astype(BF16)
            vh = (jnp.where(lo, v, jnp.zeros_like(v)), jnp.where(lo, jnp.zeros_like(v), v))
            alphas, pv = [], None
            for hh in range(2):
                sl = slice(hh * HEAD_SLOT, (hh + 1) * HEAD_SLOT)
                s = _dot_tb(k_ref[:, sl], q_ref[:, sl])
                if diag:
                    krow = lax.broadcasted_iota(jnp.int32, (T, T), 0) // CHUNK
                    qcol = lax.broadcasted_iota(jnp.int32, (T, T), 1) // CHUNK
                    s = jnp.where(krow <= qcol, s, NEG_BIG)
                m_prev = m_sc[hh]
                m_new = jnp.maximum(m_prev, jnp.max(s, axis=0, keepdims=True))
                p = jnp.exp(s - m_new[0:1, :])
                alpha = jnp.exp(m_prev - m_new)
                l_sc[hh] = alpha * l_sc[hh] + jnp.sum(p, axis=0, keepdims=True)
                m_sc[hh] = m_new
                alphas.append(alpha[0:1, :])
                t_pv = _dot_ta(vh[hh], p.astype(BF16))
                pv = t_pv if pv is None else pv + t_pv
            acc_sc[...] = acc_sc[...] * jnp.where(top, alphas[0], alphas[1]) + pv

        _on_block_kind(i, j, step)

        @pl.when(j == i)
        def _():
            l0, l1 = l_sc[0][0:1, :], l_sc[1][0:1, :]
            o_ref[...] = jnp.transpose(acc_sc[...] * jnp.where(top, 1.0 / l0, 1.0 / l1))
            lse_ref[...] = jnp.transpose(jnp.where(top, m_sc[0][0:1, :] + jnp.log(l0), m_sc[1][0:1, :] + jnp.log(l1)))

    grid_spec = pltpu.PrefetchScalarGridSpec(
        num_scalar_prefetch=2, grid=(NP, int(ii.shape[0])),
        in_specs=[pl.BlockSpec((T, 2 * HEAD_SLOT), lambda p, t, ii, jj: (ii[t], p)),
                  pl.BlockSpec((T, 2 * HEAD_SLOT), lambda p, t, ii, jj: (jj[t], p)),
                  pl.BlockSpec((T, LANES), lambda p, t, ii, jj: (jj[t], v_c0 + p))],
        out_specs=[pl.BlockSpec((T, LANES), lambda p, t, ii, jj: (ii[t], p)),
                   pl.BlockSpec((None, T, LANES), lambda p, t, ii, jj: (p, ii[t], 0))],
        scratch_shapes=[pltpu.VMEM((2, SUBLANES, T), F32), pltpu.VMEM((2, SUBLANES, T), F32), pltpu.VMEM((LANES, T), F32)])
    return pl.pallas_call(
        body, name=name, grid_spec=grid_spec,
        out_shape=[jax.ShapeDtypeStruct((S, MLA_HEADS * V_HEAD), F32), jax.ShapeDtypeStruct((NP, S, LANES), F32)],
        compiler_params=_cp(("parallel", "arbitrary")),
    )(ii, jj, q, k, kvraw)


def _attn_bwd_common(q_ref, k_ref, v, do, o, lse, hh, diag, T, lo):
    sel = lo if hh == 0 else jnp.logical_not(lo)
    s = _attn_scores(q_ref, k_ref, hh, diag, T)
    p = jnp.exp(s - lse[:, hh * V_HEAD:hh * V_HEAD + 1])
    do_h = jnp.where(sel, do, 0.0)
    dsum = jnp.sum(do_h * o, axis=1, keepdims=True)
    do_hb = do_h.astype(BF16)
    dp = _dot_tb(do_hb, v)
    return p, p * (dp - dsum), do_hb


def _attn_bwd(q, k, kvraw, do, o, lse, *, T, name):
    S = q.shape[0]
    NP = MLA_HEADS // 2
    nb = S // T
    ii, jj = _attn_pairs(nb, kv_outer=True)
    n_steps = int(ii.shape[0])
    v_c0 = MLA_HEADS * HEAD_SLOT // LANES

    def body(ii_ref, jj_ref, q_ref, k_ref, v_ref, do_ref, o_ref, lse_ref, dq_hbm, dk_ref, dv_ref,
             dq_sc, dkt_sc, dvt_sc, sem):
        pair = pl.program_id(0)
        t = pl.program_id(1)
        i, j = ii_ref[t], jj_ref[t]

        @pl.when(t == 0)
        def _():
            dq_sc[...] = jnp.zeros_like(dq_sc)

        @pl.when(i == j)
        def _():
            dkt_sc[...] = jnp.zeros_like(dkt_sc)
            dvt_sc[...] = jnp.zeros_like(dvt_sc)

        lo = lax.broadcasted_iota(jnp.int32, (T, LANES), 1) < V_HEAD
        rows = pl.ds(pl.multiple_of(i * T, T), T)

        def step(diag):
            v = v_ref[...].astype(BF16)
            do, o_v, lse_v = do_ref[...], o_ref[...], lse_ref[...]
            for hh in range(2):
                sl = slice(hh * HEAD_SLOT, (hh + 1) * HEAD_SLOT)
                p, ds, do_hb = _attn_bwd_common(q_ref, k_ref, v, do, o_v, lse_v, hh, diag, T, lo)
                dsb = ds.astype(BF16)
                do_t = jnp.transpose(do_hb.astype(F32)).astype(BF16)
                q_t = jnp.transpose(q_ref[:, sl].astype(F32)).astype(BF16)
                dvt_sc[...] += _dot(do_t, p.astype(BF16))
                dkt_sc[hh] += _dot(q_t, dsb)
                dq_sc[rows, sl] += _dot(dsb, k_ref[:, sl])

        _on_block_kind(i, j, step)

        @pl.when(i == nb - 1)
        def _():
            for hh in range(2):
                dk_ref[:, hh * HEAD_SLOT:(hh + 1) * HEAD_SLOT] = jnp.transpose(dkt_sc[hh])
            dv_ref[...] = jnp.transpose(dvt_sc[...])

        @pl.when(t == n_steps - 1)
        def _():
            cp = pltpu.make_async_copy(dq_sc, dq_hbm.at[pair], sem)
            cp.start()
            cp.wait()

    qi = lambda p, t, ii, jj: (ii[t], p)
    kj = lambda p, t, ii, jj: (jj[t], p)
    grid_spec = pltpu.PrefetchScalarGridSpec(
        num_scalar_prefetch=2, grid=(NP, n_steps),
        in_specs=[pl.BlockSpec((T, 2 * HEAD_SLOT), qi), pl.BlockSpec((T, 2 * HEAD_SLOT), kj),
                  pl.BlockSpec((T, LANES), lambda p, t, ii, jj: (jj[t], v_c0 + p)),
                  pl.BlockSpec((T, LANES), qi), pl.BlockSpec((T, LANES), qi),
                  pl.BlockSpec((None, T, LANES), lambda p, t, ii, jj: (p, ii[t], 0))],
        out_specs=[pl.BlockSpec(memory_space=pl.ANY), pl.BlockSpec((T, 2 * HEAD_SLOT), kj),
                   pl.BlockSpec((T, LANES), kj)],
        scratch_shapes=[pltpu.VMEM((S, 2 * HEAD_SLOT), F32), pltpu.VMEM((2, HEAD_SLOT, T), F32),
                        pltpu.VMEM((LANES, T), F32), pltpu.SemaphoreType.DMA])
    return pl.pallas_call(
        body, name=name, grid_spec=grid_spec,
        out_shape=[jax.ShapeDtypeStruct((NP, S, 2 * HEAD_SLOT), F32),
                   jax.ShapeDtypeStruct((S, MLA_HEADS * HEAD_SLOT), F32),
                   jax.ShapeDtypeStruct((S, MLA_HEADS * V_HEAD), F32)],
        compiler_params=_cp(("arbitrary", "arbitrary")),
    )(ii, jj, q, k, kvraw, do, o, lse)


def _mla_permute_weights(w_in, w_uq, w_ukv):
    D = w_in.shape[0]
    H = MLA_HEADS
    qk = QK_NOPE + QK_ROPE
    lat = Q_LORA + KV_LORA
    kpe = jnp.zeros((D, HEAD_SLOT), w_in.dtype).at[:, QK_NOPE:qk].set(w_in[:, lat:])
    w_in_p = jnp.concatenate([w_in[:, :lat], kpe], axis=1)
    w_uq_p = jnp.pad(w_uq.reshape(Q_LORA, H, qk), ((0, 0), (0, 0), (0, HEAD_SLOT - qk))).reshape(Q_LORA, H * HEAD_SLOT)
    kv = w_ukv.reshape(KV_LORA, H, QK_NOPE + V_HEAD)
    wk = jnp.pad(kv[:, :, :QK_NOPE], ((0, 0), (0, 0), (0, HEAD_SLOT - QK_NOPE))).reshape(KV_LORA, H * HEAD_SLOT)
    wv = kv[:, :, QK_NOPE:].reshape(KV_LORA, H * V_HEAD)
    return w_in_p, w_uq_p, jnp.concatenate([wk, wv], axis=1)


def _mla_unpermute_grads(g_in_p, g_uq_p, g_ukv_p):
    H = MLA_HEADS
    qk = QK_NOPE + QK_ROPE
    lat = Q_LORA + KV_LORA
    g_in = jnp.concatenate([g_in_p[:, :lat], g_in_p[:, lat + QK_NOPE:lat + qk]], axis=1)
    g_uq = g_uq_p.reshape(Q_LORA, H, HEAD_SLOT)[:, :, :qk].reshape(Q_LORA, H * qk)
    gk = g_ukv_p[:, :H * HEAD_SLOT].reshape(KV_LORA, H, HEAD_SLOT)[:, :, :QK_NOPE]
    gv = g_ukv_p[:, H * HEAD_SLOT:].reshape(KV_LORA, H, V_HEAD)
    g_ukv = jnp.concatenate([gk, gv], axis=2).reshape(KV_LORA, H * (QK_NOPE + V_HEAD))
    return g_in, g_uq, g_ukv


def _mla_mixer_fwd(x, pos, p, tag):
    S, D = x.shape
    ts = _rt(S, 512)
    T = _rt(S, ATTN_BLOCK)
    lat = Q_LORA + KV_LORA
    tabs = _mla_tables(pos)
    proj = _mm(x, p["w_in_p"], out_dtype=F32, tm=ts, tn=p["w_in_p"].shape[1], name=tag + "_proj")
    qn = _rms_fwd(proj, p["q_norm"], c0=0, ts=ts, name=tag + "_qn")
    kvn = _rms_fwd(proj, p["kv_norm"], c0=Q_LORA, ts=ts, name=tag + "_kvn")
    qraw = _mm(qn, p["w_uq_p"], out_dtype=F32, tm=ts, tn=1024, name=tag + "_uq")
    kvraw = _mm(kvn, p["w_ukv_p"], out_dtype=F32, tm=ts, tn=1024, name=tag + "_ukv")
    q, k = _mla_prep_fwd(qraw, kvraw, proj, tabs, kpe_c0=lat, ts=ts, name=tag + "_prep")
    o, lse = _attn_fwd(q, k, kvraw, T=T, name=tag + "_attn")
    mix = _mm(o, p["w_out"], out_dtype=F32, tm=ts, tn=D, name=tag + "_out")
    return mix, (proj, qn, kvn, kvraw, q, k, o, lse, tabs)


def _mla_mixer_bwd(dmix, x, p, saved, tag, gbuf, j):
    proj, qn, kvn, kvraw, q, k, o, lse, tabs = saved
    S, D = x.shape
    ts = _rt(S, 512)
    T = _rt(S, ATTN_BLOCK)
    lat = Q_LORA + KV_LORA
    g = {}
    do = _mm_tb([(dmix, 0)], p["w_out"], out_dtype=F32, tm=ts, tk=p["w_out"].shape[0], name=tag + "_do")
    g["w_out"] = _mm_ta(o, dmix, tk=p["w_out"].shape[0], tn=D, tm=ts, name=tag + "_dwout")
    dq, dk, dv = _attn_bwd(q, k, kvraw, do, o, lse, T=T, name=tag + "_attn_bwd")
    dqraw, dkvraw, dkpe = _mla_prep_bwd(dq, dk, dv, tabs, ts=_rt(S, 256), name=tag + "_prepb")
    dqn = _mm_tb([(dqraw, 0)], p["w_uq_p"], out_dtype=F32, tm=ts, tk=Q_LORA, name=tag + "_dqn")
    g_uq_p = _mm_ta(qn, dqraw, tk=Q_LORA, tn=1024, tm=ts, name=tag + "_dwuq")
    dkvn = _mm_tb([(dkvraw, 0)], p["w_ukv_p"], out_dtype=F32, tm=ts, tk=KV_LORA, name=tag + "_dkvn")
    g_ukv_p = _mm_ta(kvn, dkvraw, tk=KV_LORA, tn=1024, tm=ts, name=tag + "_dwukv")
    dcq, dqg8 = _rms_bwd(dqn, proj, p["q_norm"], c0=0, ts=ts, name=tag + "_qnb")
    dckv, dkvg8 = _rms_bwd(dkvn, proj, p["kv_norm"], c0=Q_LORA, ts=ts, name=tag + "_kvnb")
    g["q_norm"] = dqg8.sum(axis=0)
    g["kv_norm"] = dkvg8.sum(axis=0)
    dx = _mm_tb([(dcq, 0), (dckv, Q_LORA), (dkpe, lat)], p["w_in_p"], out_dtype=F32, tm=ts, tk=D, name=tag + "_dx")
    g_in_p = jnp.concatenate(
        [_mm_ta(x, dcq, tk=D, tn=Q_LORA, tm=ts, name=tag + "_dwin_q"),
         _mm_ta(x, dckv, tk=D, tn=KV_LORA, tm=ts, name=tag + "_dwin_kv"),
         _mm_ta(x, dkpe, tk=D, tn=HEAD_SLOT, tm=ts, name=tag + "_dwin_pe")], axis=1)
    g["w_in"], g["w_uq"], g["w_ukv"] = _mla_unpermute_grads(g_in_p, g_uq_p, g_ukv_p)
    return dx, g


RET_QK = 256
RET_V = 512
RET_BLOCK = 512


def _ret_tables(pos, T):
    half = RET_QK // 2
    inv_freq = ROPE_BASE ** (-jnp.arange(0, RET_QK, 2, dtype=F32) / RET_QK)
    ang = pos.astype(F32)[:, None] * inv_freq
    lg = jnp.log1p(-jnp.exp2(-5.0 - jnp.arange(RET_HEADS, dtype=F32)))
    idx = jnp.arange(T, dtype=F32)
    ch = jnp.arange(T) // CHUNK
    dm = jnp.where(ch[None, :] <= ch[:, None], jnp.exp(lg[:, None, None] * jnp.abs(idx[:, None] - idx[None, :])), 0.0)
    xi = jnp.broadcast_to(jnp.exp(lg[:, None] * (idx + 1.0))[:, :, None], (RET_HEADS, T, RET_QK))
    zeta = jnp.broadcast_to(jnp.exp(lg[:, None] * (T - 1.0 - idx))[:, :, None], (RET_HEADS, T, RET_QK))
    g_t = jnp.broadcast_to(jnp.exp(lg * T)[:, None, None], (RET_HEADS, 1, RET_V))
    assert half == LANES
    return jnp.cos(ang), jnp.sin(ang), dm.astype(F32), xi.astype(F32), zeta.astype(F32), g_t.astype(F32)


def _rope_half(x, c, s):
    x1, x2 = x[:, :LANES], x[:, LANES:]
    return jnp.concatenate([x1 * c - x2 * s, x1 * s + x2 * c], axis=1)


def _rope_half_t(g, c, s):
    g1, g2 = g[:, :LANES], g[:, LANES:]
    return jnp.concatenate([g1 * c + g2 * s, g2 * c - g1 * s], axis=1)


def _ret_qkv(q_ref, k_ref, v_ref, c_ref, s_ref):
    c, s = c_ref[...], s_ref[...]
    q = _rope_half(q_ref[...], c, s)
    k = _rope_half(k_ref[...], c, s) * (RET_QK ** -0.5)
    return q, k, v_ref[...].astype(BF16)


def _ret_in_specs(T, H, rev_nb=None):
    rb = (lambda n: n) if rev_nb is None else (lambda n: rev_nb - 1 - n)
    nq = H * RET_QK // RET_QK
    nv = 2 * H * RET_QK // RET_V
    return dict(
        q=pl.BlockSpec((T, RET_QK), lambda h, n: (rb(n), h)),
        k=pl.BlockSpec((T, RET_QK), lambda h, n: (rb(n), nq + h)),
        v=pl.BlockSpec((T, RET_V), lambda h, n: (rb(n), nv + h)),
        g=pl.BlockSpec((T, RET_V), lambda h, n: (rb(n), nv + H + h)),
        yv=pl.BlockSpec((T, RET_V), lambda h, n: (rb(n), h)),
        cs=pl.BlockSpec((T, LANES), lambda h, n: (rb(n), 0)),
        dm=pl.BlockSpec((None, T, T), lambda h, n: (h, 0, 0)),
        xz=pl.BlockSpec((None, T, RET_QK), lambda h, n: (h, 0, 0)),
        gt=pl.BlockSpec((None, 1, RET_V), lambda h, n: (h, 0, 0)),
        gn=pl.BlockSpec((1, RET_V), lambda h, n: (0, h)),
        st=pl.BlockSpec((None, None, RET_QK, RET_V), lambda h, n: (h, rb(n), 0, 0)),
    )


def _ret_fwd(proj, gn_g, tabs, *, T, name):
    S = proj.shape[0]
    H = RET_HEADS
    nb = S // T
    cos, sin, dm, xi, zeta, g_t = tabs
    sp = _ret_in_specs(T, H)

    def body(q_ref, k_ref, v_ref, g_ref, gn_ref, c_ref, s_ref, dm_ref, xi_ref, zeta_ref, gt_ref,
             o_ref, y_ref, st_ref, st):
        @pl.when(pl.program_id(1) == 0)
        def _():
            st[...] = jnp.zeros_like(st)

        q, k, vb = _ret_qkv(q_ref, k_ref, v_ref, c_ref, s_ref)
        qb, kb = q.astype(BF16), k.astype(BF16)
        s0 = st[...]
        s0b = s0.astype(BF16)
        st_ref[...] = s0b
        a = _dot_tb(qb, kb) * dm_ref[...]
        y = _dot(a.astype(BF16), vb) + _dot((q * xi_ref[...]).astype(BF16), s0b)
        st[...] = s0 * gt_ref[...] + _dot_ta((k * zeta_ref[...]).astype(BF16), vb)
        y_ref[...] = y
        mu = jnp.mean(y, axis=-1, keepdims=True)
        yc = y - mu
        var = jnp.mean(yc * yc, axis=-1, keepdims=True)
        gv = g_ref[...]
        o_ref[...] = (gv * _sigmoid(gv) * (yc * lax.rsqrt(var + LN_EPS) * gn_ref[...])).astype(BF16)

    return pl.pallas_call(
        body, name=name, grid=(H, nb),
        in_specs=[sp["q"], sp["k"], sp["v"], sp["g"], sp["gn"], sp["cs"], sp["cs"], sp["dm"], sp["xz"], sp["xz"], sp["gt"]],
        out_specs=[sp["yv"], sp["yv"], sp["st"]],
        out_shape=[jax.ShapeDtypeStruct((S, H * RET_V), BF16), jax.ShapeDtypeStruct((S, H * RET_V), F32),
                   jax.ShapeDtypeStruct((H, nb, RET_QK, RET_V), BF16)],
        scratch_shapes=[pltpu.VMEM((RET_QK, RET_V), F32)],
        compiler_params=_cp(("parallel", "arbitrary")),
    )(proj, proj, proj, proj, gn_g.reshape(1, H * RET_V), cos, sin, dm, xi, zeta, g_t)


def _ret_gn_bwd(dout, proj, y, gn_g, *, ts, name):
    S = proj.shape[0]
    H = RET_HEADS
    goff = 2 * H * RET_QK // RET_V + H

    def body(do_ref, g_ref, y_ref, gn_ref, dy_ref, dg_ref, dgn_ref):
        @pl.when(pl.program_id(1) == 0)
        def _():
            dgn_ref[...] = jnp.zeros_like(dgn_ref)
        y_v = y_ref[...]
        mu = jnp.mean(y_v, axis=-1, keepdims=True)
        yc = y_v - mu
        var = jnp.mean(yc * yc, axis=-1, keepdims=True)
        rstd = lax.rsqrt(var + LN_EPS)
        yhat = yc * rstd
        gv = g_ref[...]
        sg = _sigmoid(gv)
        dout = do_ref[...]
        gn = gn_ref[...]
        dg_ref[...] = (dout * (yhat * gn) * (sg * (1.0 + gv * (1.0 - sg)))).astype(BF16)
        dyn = dout * (gv * sg)
        dgn_ref[...] += _fold8(dyn * yhat)
        dyh = dyn * gn
        m1 = jnp.mean(dyh, axis=-1, keepdims=True)
        m2 = jnp.mean(dyh * yhat, axis=-1, keepdims=True)
        dy_ref[...] = (rstd * (dyh - m1 - yhat * m2)).astype(BF16)

    blk = pl.BlockSpec((ts, RET_V), lambda h, i: (i, h))
    return pl.pallas_call(
        body, name=name, grid=(H, S // ts),
        in_specs=[blk, pl.BlockSpec((ts, RET_V), lambda h, i: (i, goff + h)), blk,
                  pl.BlockSpec((1, RET_V), lambda h, i: (0, h))],
        out_specs=[blk, blk, pl.BlockSpec((SUBLANES, RET_V), lambda h, i: (0, h))],
        out_shape=[jax.ShapeDtypeStruct((S, H * RET_V), BF16), jax.ShapeDtypeStruct((S, H * RET_V), BF16),
                   jax.ShapeDtypeStruct((SUBLANES, H * RET_V), F32)],
        compiler_params=_cp(("parallel", "arbitrary")),
    )(dout, proj, y, gn_g.reshape(1, H * RET_V))


def _ret_bwd(proj, dy, states, tabs, *, T, name):
    S = proj.shape[0]
    H = RET_HEADS
    nb = S // T
    cos, sin, dm, xi, zeta, g_t = tabs
    sp = _ret_in_specs(T, H, rev_nb=nb)

    def body(q_ref, k_ref, v_ref, dy_ref, st_ref, c_ref, s_ref, dm_ref, xi_ref, zeta_ref, gt_ref,
             dq_ref, dk_ref, dv_ref, ds):
        @pl.when(pl.program_id(1) == 0)
        def _():
            ds[...] = jnp.zeros_like(ds)

        q, k, vb = _ret_qkv(q_ref, k_ref, v_ref, c_ref, s_ref)
        qb, kb = q.astype(BF16), k.astype(BF16)
        dyb = dy_ref[...]
        s0b = st_ref[...]
        dmv, xiv, zv = dm_ref[...], xi_ref[...], zeta_ref[...]
        ds_v = ds[...]
        dsb = ds_v.astype(BF16)
        gm = (_dot_tb(dyb, vb) * dmv).astype(BF16)
        ab = (_dot_tb(qb, kb) * dmv).astype(BF16)
        kz = (k * zv).astype(BF16)
        qx = (q * xiv).astype(BF16)
        dq = _dot(gm, kb) + xiv * _dot_tb(dyb, s0b)
        dk = _dot_ta(gm, qb) + zv * _dot_tb(vb, dsb)
        dv_ref[...] = (_dot_ta(ab, dyb) + _dot(kz, dsb)).astype(BF16)
        ds[...] = ds_v * gt_ref[...] + _dot_ta(qx, dyb)
        c, s = c_ref[...], s_ref[...]
        dq_ref[...] = _rope_half_t(dq, c, s).astype(BF16)
        dk_ref[...] = _rope_half_t(dk * (RET_QK ** -0.5), c, s).astype(BF16)

    qblk = pl.BlockSpec((T, RET_QK), lambda h, n: (nb - 1 - n, h))
    return pl.pallas_call(
        body, name=name, grid=(H, nb),
        in_specs=[sp["q"], sp["k"], sp["v"], sp["yv"], sp["st"], sp["cs"], sp["cs"], sp["dm"], sp["xz"], sp["xz"], sp["gt"]],
        out_specs=[qblk, qblk, sp["yv"]],
        out_shape=[jax.ShapeDtypeStruct((S, H * RET_QK), BF16), jax.ShapeDtypeStruct((S, H * RET_QK), BF16),
                   jax.ShapeDtypeStruct((S, H * RET_V), BF16)],
        scratch_shapes=[pltpu.VMEM((RET_QK, RET_V), F32)],
        compiler_params=_cp(("parallel", "arbitrary")),
    )(proj, proj, proj, dy, states, cos, sin, dm, xi, zeta, g_t)


def _ret_mixer_fwd(x, pos, p, tag):
    S, D = x.shape
    ts = _rt(S, 512)
    T = _rt(S, RET_BLOCK)
    tabs = _ret_tables(pos, T)
    proj = _mm(x, p["w_in"], out_dtype=F32, tm=_rt(S, 256), tn=p["w_in"].shape[1], name=tag + "_proj")
    gated, y, states = _ret_fwd(proj, p["gn_g"], tabs, T=T, name=tag + "_ret")
    mix = _mm(gated, p["w_out"], out_dtype=F32, tm=ts, tn=D, name=tag + "_out")
    return mix, (proj, gated, y, states, tabs)


def _ret_mixer_bwd(dmix, x, p, saved, tag, gbuf, j):
    proj, gated, y, states, tabs = saved
    S, D = x.shape
    ts = _rt(S, 512)
    T = _rt(S, RET_BLOCK)
    H = RET_HEADS
    hq, hv = H * RET_QK, H * RET_V
    g = {}
    dout = _mm_tb([(dmix, 0)], p["w_out"], out_dtype=F32, tm=ts, tk=1024, name=tag + "_dgated")
    n_ret = DEPTH // 3
    _grad_into(gbuf, "ret_w_out", (n_ret, hv, D), j, 0, 0, gated, dmix, tk=1024, tn=D, tm=ts, name=tag + "_dwout")
    dy, dgate, dgn8 = _ret_gn_bwd(dout, proj, y, p["gn_g"], ts=_rt(S, 256), name=tag + "_gnb")
    g["gn_g"] = dgn8.sum(axis=0)
    dq, dk, dv = _ret_bwd(proj, dy, states, tabs, T=T, name=tag + "_retb")
    dx = _mm_tb([(dq, 0), (dk, hq), (dv, 2 * hq), (dgate, 2 * hq + hv)], p["w_in"], out_dtype=F32,
                tm=ts, tk=512, name=tag + "_dx")
    w_in_shape = (n_ret, D, 2 * hq + 2 * hv)
    for part, c0, nm in ((dq, 0, "q"), (dk, hq, "k"), (dv, 2 * hq, "v"), (dgate, 2 * hq + hv, "g")):
        _grad_into(gbuf, "ret_w_in", w_in_shape, j, 0, c0, x, part, tk=D, tn=1024, tm=ts, name=tag + "_dwin_" + nm)
    return dx, g


PACK_W = 1024
ANY = pl.BlockSpec(memory_space=pl.ANY)
MESH = pl.DeviceIdType.MESH


def _coords():
    return lax.axis_index("x"), lax.axis_index("y"), lax.axis_index("c")


def _chip_peers(x, y):
    return [(1 - x, y), (x, 1 - y), (1 - x, 1 - y)]


def _slot(ref, axis, s, n):
    if axis is None:
        return ref.at[s]
    size = n // N_CHIPS
    sl = pl.ds(pl.multiple_of(s * size, LANES if axis == 2 else 2 * SUBLANES), size)
    return ref.at[:, sl, :] if axis == 1 else ref.at[:, :, sl]


def _row_half(ref, h):
    if len(ref.shape) == 2:
        n = ref.shape[0] // 2
        return ref.at[pl.ds(pl.multiple_of(h * n, 2 * SUBLANES), n), :]
    n = ref.shape[1] // 2
    return ref.at[:, pl.ds(pl.multiple_of(h * n, 2 * SUBLANES), n), :]


def _gather_chips(items, name):
    n = len(items)
    axes = [ax for _, ax in items]
    out_shapes = []
    for arr, ax in items:
        shp = (N_CHIPS,) + arr.shape if ax is None else tuple(d * (N_CHIPS if i == ax else 1) for i, d in enumerate(arr.shape))
        out_shapes.append(jax.ShapeDtypeStruct(shp, arr.dtype))

    def body(*refs):
        srcs, outs = refs[:n], refs[n:2 * n]
        send_sems, recv_sems, local_sems = refs[2 * n:]
        x, y, c = _coords()
        me = 2 * x + y
        sibling = (x, y, 1 - c)
        dst = lambda t, s: _slot(outs[t], axes[t], s, out_shapes[t].shape[axes[t]] if axes[t] is not None else 0)

        def copy(sem, src, dst_ref, to):
            return pltpu.make_async_remote_copy(src_ref=src, dst_ref=dst_ref, send_sem=send_sems.at[sem],
                                                recv_sem=recv_sems.at[sem], device_id=to, device_id_type=MESH)

        local = [pltpu.make_async_copy(srcs[t], dst(t, me), local_sems.at[t]) for t in range(n)]
        for cp in local:
            cp.start()
        peers = _chip_peers(x, y)
        first, arrive, passed, from_sib = [], [], [], []
        for k, (px, py) in enumerate(peers):
            for t in range(n):
                land = _row_half(dst(t, 2 * px + py), c)
                first.append(copy(k * n + t, _row_half(srcs[t], c), _row_half(dst(t, me), c), (px, py, c)))
                arrive.append(copy(k * n + t, _row_half(srcs[t], c), land, (px, py, c)))
                passed.append(copy((3 + k) * n + t, land, land, sibling))
                from_sib.append(copy((3 + k) * n + t, land, _row_half(dst(t, 2 * px + py), 1 - c), sibling))
        for cp in first:
            cp.start()
        for cp_in, cp_on in zip(arrive, passed):
            cp_in.wait_recv()
            cp_on.start()
        for cp in from_sib:
            cp.wait_recv()
        for cp in first + passed:
            cp.wait_send()
        for cp in local:
            cp.wait()

    return pl.pallas_call(
        body, name=name, in_specs=[ANY] * n, out_specs=[ANY] * n, out_shape=out_shapes,
        scratch_shapes=[pltpu.SemaphoreType.DMA((6 * n,)), pltpu.SemaphoreType.DMA((6 * n,)),
                        pltpu.SemaphoreType.DMA((n,))],
    )(*[arr for arr, _ in items])


def _scatter_chips(items, name):
    n = len(items)
    axes = [ax for _, ax in items]
    out_shapes = []
    for arr, ax in items:
        part = arr.shape[1:] if ax is None else tuple(d // (N_CHIPS if i == ax else 1) for i, d in enumerate(arr.shape))
        out_shapes.append(jax.ShapeDtypeStruct((3,) + part, arr.dtype))

    def body(*refs):
        srcs, outs = refs[:n], refs[n:2 * n]
        send_sems, recv_sems = refs[2 * n:]
        x, y, c = _coords()
        copies = []
        for k, (px, py) in enumerate(_chip_peers(x, y)):
            for t in range(n):
                src = _slot(srcs[t], axes[t], 2 * px + py, srcs[t].shape[axes[t]] if axes[t] is not None else 0)
                copies.append(pltpu.make_async_remote_copy(
                    src_ref=src, dst_ref=outs[t].at[k], send_sem=send_sems.at[k * n + t],
                    recv_sem=recv_sems.at[k * n + t], device_id=(px, py, c), device_id_type=MESH))
        for cp in copies:
            cp.start()
        for cp in copies:
            cp.wait_recv()
        for cp in copies:
            cp.wait_send()

    return pl.pallas_call(
        body, name=name, in_specs=[ANY] * n, out_specs=[ANY] * n, out_shape=out_shapes,
        scratch_shapes=[pltpu.SemaphoreType.DMA((3 * n,)), pltpu.SemaphoreType.DMA((3 * n,))],
    )(*[arr for arr, _ in items])


def _swap_sibling(arrs, name):
    n = len(arrs)

    def body(*refs):
        srcs, outs = refs[:n], refs[n:2 * n]
        send_sems, recv_sems = refs[2 * n:]
        x, y, c = _coords()
        copies = [pltpu.make_async_remote_copy(src_ref=srcs[t], dst_ref=outs[t], send_sem=send_sems.at[t],
                                               recv_sem=recv_sems.at[t], device_id=(x, y, 1 - c), device_id_type=MESH)
                  for t in range(n)]
        for cp in copies:
            cp.start()
        for cp in copies:
            cp.wait_recv()
        for cp in copies:
            cp.wait_send()

    return pl.pallas_call(
        body, name=name, in_specs=[ANY] * n, out_specs=[ANY] * n,
        out_shape=[jax.ShapeDtypeStruct(a_.shape, a_.dtype) for a_ in arrs],
        scratch_shapes=[pltpu.SemaphoreType.DMA((n,)), pltpu.SemaphoreType.DMA((n,))],
    )(*arrs)


def _allreduce_small(v, name):
    R, Wd = v.shape

    def body(v_ref, o_ref, buf, send_sems, recv_sems):
        x, y, c = _coords()
        o_ref[...] = v_ref[...]
        for st, peer in enumerate([(x, y, 1 - c), (x, 1 - y, c), (1 - x, y, c)]):
            cp = pltpu.make_async_remote_copy(src_ref=o_ref, dst_ref=buf.at[st], send_sem=send_sems.at[st],
                                              recv_sem=recv_sems.at[st], device_id=peer, device_id_type=MESH)
            cp.start()
            cp.wait_recv()
            cp.wait_send()
            o_ref[...] = o_ref[...] + buf[st]

    vm = pl.BlockSpec(memory_space=pltpu.VMEM)
    return pl.pallas_call(
        body, name=name, in_specs=[vm], out_specs=vm,
        out_shape=jax.ShapeDtypeStruct((R, Wd), F32),
        scratch_shapes=[pltpu.VMEM((3, R, Wd), F32), pltpu.SemaphoreType.DMA((3,)), pltpu.SemaphoreType.DMA((3,))],
    )(v)


def _row_tile(rows):
    t = rows
    while t > 256:
        assert t % 2 == 0
        t //= 2
    assert t % SUBLANES == 0
    return t


def _sum_partials(g, recv, axis, *, name):
    _, L, R, C = recv.shape
    tr = _row_tile(R)
    me = (2 * lax.axis_index("x") + lax.axis_index("y")).astype(jnp.int32).reshape(1)

    def body(me_ref, g_ref, r_ref, o_ref):
        o_ref[...] = ((g_ref[...] + r_ref[0].astype(F32)) + r_ref[1].astype(F32)) + r_ref[2].astype(F32)

    if axis is None:
        g_spec = pl.BlockSpec((None, None, tr, C), lambda l, i, me: (me[0], l, i, 0))
    elif axis == 1:
        g_spec = pl.BlockSpec((None, tr, C), lambda l, i, me: (l, me[0] * (R // tr) + i, 0))
    else:
        g_spec = pl.BlockSpec((None, tr, C), lambda l, i, me: (l, i, me[0]))
    grid_spec = pltpu.PrefetchScalarGridSpec(
        num_scalar_prefetch=1, grid=(L, R // tr),
        in_specs=[g_spec, pl.BlockSpec((3, None, tr, C), lambda l, i, me: (0, l, i, 0))],
        out_specs=pl.BlockSpec((None, tr, C), lambda l, i, me: (l, i, 0)))
    return pl.pallas_call(
        body, name=name, grid_spec=grid_spec, out_shape=jax.ShapeDtypeStruct((L, R, C), F32),
        compiler_params=_cp(("parallel", "parallel")),
    )(me, g, recv)


def _adamw(w, m, v, ga, gb, *, name):
    L, R, C = w.shape
    tr = _row_tile(R)
    two = gb is not None
    c1 = 1.0 / (1.0 - ADAM_B1 ** ADAM_STEP)
    c2 = 1.0 / (1.0 - ADAM_B2 ** ADAM_STEP)

    def body(*refs):
        if two:
            w_ref, m_ref, v_ref, ga_ref, gb_ref, g_ref, d_ref, mo_ref, vo_ref = refs
            g = ga_ref[...] + gb_ref[...]
        else:
            w_ref, m_ref, v_ref, ga_ref, g_ref, d_ref, mo_ref, vo_ref = refs
            g = ga_ref[...]
        m2 = ADAM_B1 * m_ref[...] + (1.0 - ADAM_B1) * g
        v2 = ADAM_B2 * v_ref[...] + (1.0 - ADAM_B2) * (g * g)
        g_ref[...] = g
        mo_ref[...] = m2
        vo_ref[...] = v2
        d_ref[...] = -ADAM_LR * ((m2 * c1) / (jnp.sqrt(v2 * c2) + ADAM_EPS) + ADAM_WD * w_ref[...])

    blk = pl.BlockSpec((None, tr, C), lambda l, i: (l, i, 0))
    args = [w, m, v, ga] + ([gb] if two else [])
    return pl.pallas_call(
        body, name=name, grid=(L, R // tr), in_specs=[blk] * len(args), out_specs=[blk] * 4,
        out_shape=[jax.ShapeDtypeStruct((L, R, C), F32)] * 4, compiler_params=_cp(("parallel", "parallel")),
    )(*args)


SHARDED = [
    ("ffn_w_up", 2, True), ("ffn_conv_w", 2, False), ("ffn_w_down", 1, True),
    ("lru_w_in", 2, True), ("lru_conv_w", 2, False), ("lru_conv_b", 1, False),
    ("lru_w_a", 2, True), ("lru_b_a", 2, False), ("lru_w_x", 2, True), ("lru_b_x", 2, False),
    ("lru_lambda", 1, False), ("lru_w_out", 1, True),
    ("mla_w_in", 2, True), ("mla_w_uq", 2, True), ("mla_w_ukv", 2, True), ("mla_w_out", 1, True),
    ("ret_w_in", 2, True), ("ret_gn_g", 1, False), ("ret_w_out", 1, True),
]
BIG_AXIS = {"ffn_w_up": 2, "ffn_w_down": 1, "lru_w_in": 2, "lru_w_out": 1, "ret_w_in": 2, "ret_w_out": 1}
REPLICATED = ["ln1_g", "ln1_b", "ln2_g", "ln2_b", "ffn_conv_b", "mla_q_norm", "mla_kv_norm"]
WEIGHTS = ["ln1_g", "ln1_b", "ln2_g", "ln2_b", "ffn_w_up", "ffn_conv_w", "ffn_conv_b", "ffn_w_down", "lru_w_in",
           "lru_conv_w", "lru_conv_b", "lru_w_a", "lru_b_a", "lru_w_x", "lru_b_x", "lru_lambda", "lru_w_out",
           "mla_w_in", "mla_q_norm", "mla_kv_norm", "mla_w_uq", "mla_w_ukv", "mla_w_out", "ret_w_in", "ret_gn_g",
           "ret_w_out"]
PACK_ROWS = 512


def _pack(arrs, dtype, lead=(), rows=PACK_ROWS):
    nl = len(lead)
    flat = jnp.concatenate([a.astype(dtype).reshape(lead + (-1,)) for a in arrs], axis=nl)
    n = flat.shape[nl]
    quantum = rows * PACK_W
    total = -(-n // quantum) * quantum
    flat = jnp.pad(flat, [(0, 0)] * nl + [(0, total - n)])
    return flat.reshape(lead + (total // PACK_W, PACK_W))


def _unpack(buf, shapes, lead=()):
    nl = len(lead)
    flat = buf.reshape(lead + (-1,))
    out, off = [], 0
    for shp in shapes:
        n = int(np.prod(shp))
        out.append(lax.slice_in_dim(flat, off, off + n, axis=nl).reshape(lead + tuple(shp)))
        off += n
    return out


def _layer_params(full, rep, i):
    kind, j = i % 3, i // 3
    ffn = dict(w_up=full["ffn_w_up"][i], conv_w=full["ffn_conv_w"][i], conv_b=rep["ffn_conv_b"][i],
               w_down=full["ffn_w_down"][i])
    if kind == 0:
        mix = dict(w_in=full["lru_w_in"][j], conv_w=full["lru_conv_w"][j], conv_b=full["lru_conv_b"][j],
                   w_a=full["lru_w_a"][j], b_a=full["lru_b_a"][j], w_x=full["lru_w_x"][j], b_x=full["lru_b_x"][j],
                   lam=full["lru_lambda"][j], w_out=full["lru_w_out"][j])
    elif kind == 1:
        w_in_p, w_uq_p, w_ukv_p = _mla_permute_weights(full["mla_w_in"][j], full["mla_w_uq"][j], full["mla_w_ukv"][j])
        mix = dict(w_in_p=w_in_p, w_uq_p=w_uq_p, w_ukv_p=w_ukv_p, q_norm=rep["mla_q_norm"][j],
                   kv_norm=rep["mla_kv_norm"][j], w_out=full["mla_w_out"][j])
    else:
        mix = dict(w_in=full["ret_w_in"][j], gn_g=full["ret_gn_g"][j], w_out=full["ret_w_out"][j])
    return kind, mix, ffn


_MIX_FWD = {0: lambda x, pos, p, tag: _lru_mixer_fwd(x, p, tag), 1: _mla_mixer_fwd, 2: _ret_mixer_fwd}
_MIX_BWD = {0: _lru_mixer_bwd, 1: _mla_mixer_bwd, 2: _ret_mixer_bwd}
_MIX_PREFIX = {0: "lru_", 1: "mla_", 2: "ret_"}
_MIX_KEYS = {0: {"w_in": "lru_w_in", "conv_w": "lru_conv_w", "conv_b": "lru_conv_b", "w_a": "lru_w_a", "b_a": "lru_b_a",
                 "w_x": "lru_w_x", "b_x": "lru_b_x", "lam": "lru_lambda", "w_out": "lru_w_out"},
             1: {"w_in": "mla_w_in", "q_norm": "mla_q_norm", "kv_norm": "mla_kv_norm", "w_uq": "mla_w_uq",
                 "w_ukv": "mla_w_ukv", "w_out": "mla_w_out"},
             2: {"w_in": "ret_w_in", "gn_g": "ret_gn_g", "w_out": "ret_w_out"}}
_FFN_KEYS = {"w_up": "ffn_w_up", "conv_w": "ffn_conv_w", "conv_b": "ffn_conv_b", "w_down": "ffn_w_down"}


def _local_step(x, pos, target, full, rep):
    S, D = x.shape
    ts = _rt(S, 512)
    acts = []
    h = x
    for i in range(DEPTH):
        kind, mp, fp = _layer_params(full, rep, i)
        tag = "l%d" % i
        mix, msaved = _MIX_FWD[kind](h, pos, mp, tag + "m")
        h1, z1 = _ln_fwd(h, mix, rep["ln1_g"][i], rep["ln1_b"][i], ts=ts, name=tag + "_ln1")
        f, fsaved = _ffn_fwd(h1, fp, tag + "f")
        h2, z2 = _ln_fwd(h1, f, rep["ln2_g"][i], rep["ln2_b"][i], ts=ts, name=tag + "_ln2")
        acts.append((kind, mp, fp, h, msaved, h1, z1, fsaved, z2))
        h = h2
    dy, part = _loss_head(h, target, ts=ts, name="loss_head")

    grads = {n: {} for n in WEIGHTS if n not in BIG_AXIS}
    gbuf = {}
    d_a, d_b = dy, None
    for i in reversed(range(DEPTH)):
        kind, mp, fp, h_in, msaved, h1, z1, fsaved, z2 = acts[i]
        tag = "l%d" % i
        dz2, dg8, db8 = _ln_bwd(d_a, d_b, z2, rep["ln2_g"][i], ts=ts, name=tag + "_ln2b")
        grads["ln2_g"][i], grads["ln2_b"][i] = dg8.sum(axis=0), db8.sum(axis=0)
        dx_f, gf = _ffn_bwd(dz2, h1, fp, fsaved, tag + "f", gbuf, i)
        for k, v in gf.items():
            grads[_FFN_KEYS[k]][i] = v
        dz1, dg8, db8 = _ln_bwd(dz2, dx_f, z1, rep["ln1_g"][i], ts=ts, name=tag + "_ln1b")
        grads["ln1_g"][i], grads["ln1_b"][i] = dg8.sum(axis=0), db8.sum(axis=0)
        dx_m, gm = _MIX_BWD[kind](dz1, h_in, mp, msaved, tag + "m", gbuf, i // 3)
        for k, v in gm.items():
            grads[_MIX_KEYS[kind][k]][i // 3] = v
        d_a, d_b = dz1, dx_m
    grad_x = _axpy(d_a, d_b, ts=ts, name="grad_x")
    stacked = {n: jnp.stack([grads[n][j] for j in sorted(grads[n])]) for n in grads}
    return part, grad_x, stacked, gbuf


def kernel(x, positions, ln1_g, ln1_b, ln2_g, ln2_b, ffn_w_up, ffn_conv_w, ffn_conv_b, ffn_w_down, lru_w_in, lru_conv_w, lru_conv_b, lru_w_a, lru_b_a, lru_w_x, lru_b_x, lru_lambda, lru_w_out, mla_w_in, mla_q_norm, mla_kv_norm, mla_w_uq, mla_w_ukv, mla_w_out, ret_w_in, ret_gn_g, ret_w_out, loss_target, m_ln1_g, m_ln1_b, m_ln2_g, m_ln2_b, m_ffn_w_up, m_ffn_conv_w, m_ffn_conv_b, m_ffn_w_down, m_lru_w_in, m_lru_conv_w, m_lru_conv_b, m_lru_w_a, m_lru_b_a, m_lru_w_x, m_lru_b_x, m_lru_lambda, m_lru_w_out, m_mla_w_in, m_mla_q_norm, m_mla_kv_norm, m_mla_w_uq, m_mla_w_ukv, m_mla_w_out, m_ret_w_in, m_ret_gn_g, m_ret_w_out, v_ln1_g, v_ln1_b, v_ln2_g, v_ln2_b, v_ffn_w_up, v_ffn_conv_w, v_ffn_conv_b, v_ffn_w_down, v_lru_w_in, v_lru_conv_w, v_lru_conv_b, v_lru_w_a, v_lru_b_a, v_lru_w_x, v_lru_b_x, v_lru_lambda, v_lru_w_out, v_mla_w_in, v_mla_q_norm, v_mla_kv_norm, v_mla_w_uq, v_mla_w_ukv, v_mla_w_out, v_ret_w_in, v_ret_gn_g, v_ret_w_out):
    w = dict(ln1_g=ln1_g, ln1_b=ln1_b, ln2_g=ln2_g, ln2_b=ln2_b, ffn_w_up=ffn_w_up, ffn_conv_w=ffn_conv_w, ffn_conv_b=ffn_conv_b, ffn_w_down=ffn_w_down, lru_w_in=lru_w_in, lru_conv_w=lru_conv_w, lru_conv_b=lru_conv_b, lru_w_a=lru_w_a, lru_b_a=lru_b_a, lru_w_x=lru_w_x, lru_b_x=lru_b_x, lru_lambda=lru_lambda, lru_w_out=lru_w_out, mla_w_in=mla_w_in, mla_q_norm=mla_q_norm, mla_kv_norm=mla_kv_norm, mla_w_uq=mla_w_uq, mla_w_ukv=mla_w_ukv, mla_w_out=mla_w_out, ret_w_in=ret_w_in, ret_gn_g=ret_gn_g, ret_w_out=ret_w_out)
    m = dict(ln1_g=m_ln1_g, ln1_b=m_ln1_b, ln2_g=m_ln2_g, ln2_b=m_ln2_b, ffn_w_up=m_ffn_w_up, ffn_conv_w=m_ffn_conv_w, ffn_conv_b=m_ffn_conv_b, ffn_w_down=m_ffn_w_down, lru_w_in=m_lru_w_in, lru_conv_w=m_lru_conv_w, lru_conv_b=m_lru_conv_b, lru_w_a=m_lru_w_a, lru_b_a=m_lru_b_a, lru_w_x=m_lru_w_x, lru_b_x=m_lru_b_x, lru_lambda=m_lru_lambda, lru_w_out=m_lru_w_out, mla_w_in=m_mla_w_in, mla_q_norm=m_mla_q_norm, mla_kv_norm=m_mla_kv_norm, mla_w_uq=m_mla_w_uq, mla_w_ukv=m_mla_w_ukv, mla_w_out=m_mla_w_out, ret_w_in=m_ret_w_in, ret_gn_g=m_ret_gn_g, ret_w_out=m_ret_w_out)
    v = dict(ln1_g=v_ln1_g, ln1_b=v_ln1_b, ln2_g=v_ln2_g, ln2_b=v_ln2_b, ffn_w_up=v_ffn_w_up, ffn_conv_w=v_ffn_conv_w, ffn_conv_b=v_ffn_conv_b, ffn_w_down=v_ffn_w_down, lru_w_in=v_lru_w_in, lru_conv_w=v_lru_conv_w, lru_conv_b=v_lru_conv_b, lru_w_a=v_lru_w_a, lru_b_a=v_lru_b_a, lru_w_x=v_lru_w_x, lru_b_x=v_lru_b_x, lru_lambda=v_lru_lambda, lru_w_out=v_lru_w_out, mla_w_in=v_mla_w_in, mla_q_norm=v_mla_q_norm, mla_kv_norm=v_mla_kv_norm, mla_w_uq=v_mla_w_uq, mla_w_ukv=v_mla_w_ukv, mla_w_out=v_mla_w_out, ret_w_in=v_ret_w_in, ret_gn_g=v_ret_gn_g, ret_w_out=v_ret_w_out)
    D = x.shape[-1]
    axis_of = {n: ax for n, ax, _ in SHARDED}
    big = list(BIG_AXIS)
    small_mx = [n for n, _, mx in SHARDED if mx and n not in BIG_AXIS]
    small_vec = [n for n, _, mx in SHARDED if not mx]
    small = small_mx + small_vec

    gathered = _gather_chips([(w[n].astype(BF16), BIG_AXIS[n]) for n in big]
                             + [(_pack([w[n] for n in small_mx], BF16), None), (_pack([w[n] for n in small_vec], F32), None)],
                             "gather_weights")
    full = dict(zip(big, gathered))
    for names, buf in ((small_mx, gathered[-2]), (small_vec, gathered[-1])):
        blocks = _unpack(buf, [w[n].shape for n in names], lead=(N_CHIPS,))
        for n, blk in zip(names, blocks):
            full[n] = jnp.concatenate([blk[s] for s in range(N_CHIPS)], axis=axis_of[n])
    rep = {n: w[n] for n in REPLICATED}

    part, grad_x, grads, gbuf = _local_step(x[0], positions[0], loss_target[0], full, rep)
    loss = lax.psum((0.5 / D) * jnp.sum(part), MESH_AXES)

    g_pack = _pack([jnp.stack(jnp.split(grads[n], N_CHIPS, axis=axis_of[n])) for n in small], F32, lead=(N_CHIPS,))
    recv = _scatter_chips([(gbuf[n][1], BIG_AXIS[n]) for n in big] + [(g_pack, None)], "scatter_grads")
    sums = [_sum_partials(gbuf[n][0], r, BIG_AXIS[n], name="sum_" + n) for n, r in zip(big, recv)]
    sums.append(_sum_partials(g_pack[:, None], recv[-1][:, None], None, name="sum_small"))
    sibs = _swap_sibling(sums, "swap_core_partials")
    res = {kind: {} for kind in "gdmv"}
    for n, p_mine, p_sib in zip(big, sums, sibs):
        for kind, o in zip("gdmv", _adamw(w[n], m[n], v[n], p_mine, p_sib, name="adamw_" + n)):
            res[kind][n] = o
    spack = lambda d: _pack([d[n] for n in small], F32)[None]
    shapes = [w[n].shape for n in small]
    for kind, o in zip("gdmv", _adamw(spack(w), spack(m), spack(v), sums[-1], sibs[-1], name="adamw_small")):
        res[kind].update(zip(small, _unpack(o[0], shapes)))

    r_shapes = [w[n].shape for n in REPLICATED]
    rpack = lambda d: _pack([d[n] for n in REPLICATED], F32, rows=SUBLANES)
    r_sum = _allreduce_small(rpack(grads), "allreduce_replicated")
    r_outs = _adamw(rpack(w)[None], rpack(m)[None], rpack(v)[None], r_sum[None], None, name="adamw_replicated")
    for kind, o in zip("gdmv", r_outs):
        res[kind].update(zip(REPLICATED, _unpack(o[0], r_shapes)))

    return (loss, grad_x[None], *[res["g"][n] for n in WEIGHTS], *[res["d"][n] for n in WEIGHTS],
            *[res["m"][n] for n in WEIGHTS], *[res["v"][n] for n in WEIGHTS])
```
